```python
import math
import jax, jax.numpy as jnp
from jax import lax
import numpy as np

D_MODEL = 2048
BATCH = 2
SEQ = 4096
DEPTH = 1
DEC_BATCH = 8
DEC_SEQ = 1
PAST_LEN = 16384
PAGE_SIZE = 128

HEAD_DIM = 64
N_ATT_HEADS = D_MODEL // 2 // HEAD_DIM
N_KV_HEADS = 4
KV_REP = N_ATT_HEADS // N_KV_HEADS
ATT_WIDTH = N_ATT_HEADS * HEAD_DIM
KV_WIDTH = N_KV_HEADS * HEAD_DIM
WINDOWS = (128, 512, 2048)
DILATIONS = (1, 4, 16)
N_BRANCH = 3
N_TAPS = 129
W_MAX = 2048
Q_BLOCK = 128
ATT_SCALE = HEAD_DIM ** -0.5
N_BUCKETS = 32
BUCKET_MAX_DIST = 2048

SSM_HEAD_DIM = 64
N_SSM_HEADS = D_MODEL // 2 // SSM_HEAD_DIM
SSM_WIDTH = N_SSM_HEADS * SSM_HEAD_DIM
N_SSM_GROUPS = 2
D_STATE = 128
CONV_W = 4
CONV_DIM = SSM_WIDTH + 2 * N_SSM_GROUPS * D_STATE
SSD_CHUNK = 128

MIX_WIDTH = ATT_WIDTH + SSM_WIDTH
IN_PROJ = ATT_WIDTH + 2 * KV_WIDTH + SSM_WIDTH + CONV_DIM + N_SSM_HEADS
SPLITS = [ATT_WIDTH, ATT_WIDTH + KV_WIDTH, ATT_WIDTH + 2 * KV_WIDTH,
          ATT_WIDTH + 2 * KV_WIDTH + SSM_WIDTH, ATT_WIDTH + 2 * KV_WIDTH + SSM_WIDTH + CONV_DIM]

N_EXPERTS = 32
TOP_K = 4
D_FF = D_MODEL
SWIGLU_LIMIT = 7.0
SWIGLU_ALPHA = 1.702
MOE_BLOCK = 128

EPS = 1e-5

kernel_name = 'hymba_dilated_ssd_moe_step'


def _rmsnorm(x, g):
    xf = x.astype(jnp.float32)
    y = xf * lax.rsqrt(jnp.mean(xf * xf, axis=-1, keepdims=True) + EPS)
    return (y * g.astype(jnp.float32)).astype(x.dtype)


def _branch_distances():
    return np.arange(N_TAPS)[None, :] * np.array(DILATIONS)[:, None]


def _t5_bucket(dist):
    max_exact = N_BUCKETS // 2
    d = dist.astype(jnp.float32)
    large = max_exact + (jnp.log(jnp.maximum(d, 1.0) / max_exact)
                         / math.log(BUCKET_MAX_DIST / max_exact) * (N_BUCKETS - max_exact)).astype(jnp.int32)
    large = jnp.minimum(large, N_BUCKETS - 1)
    return jnp.where(dist < max_exact, dist, large)


def _project(h, w_in):
    b, t, _ = h.shape
    q, k, v, z, xbc, dt_raw = jnp.split(h @ w_in, SPLITS, axis=-1)
    return (q.reshape(b, t, N_ATT_HEADS, HEAD_DIM), k.reshape(b, t, N_KV_HEADS, HEAD_DIM),
            v.reshape(b, t, N_KV_HEADS, HEAD_DIM), z, xbc, dt_raw)


def _dilated_attn(q, kwin, vwin, idx, valid, bias):
    b, t = q.shape[:2]
    qg = q.reshape(b, t, N_KV_HEADS, KV_REP, HEAD_DIM)
    kg = kwin[:, idx]
    vg = vwin[:, idx]
    s = jnp.einsum('btkrd,btgjkd->btkrgj', qg, kg).astype(jnp.float32) * ATT_SCALE
    s = s + bias.reshape(N_KV_HEADS, KV_REP, N_BRANCH, N_TAPS)
    s = jnp.where(valid[None, :, None, None], s, -jnp.inf)
    lse = jax.nn.logsumexp(s, axis=-1, keepdims=True)
    p = jnp.exp(s - lse) * jax.nn.softmax(lse, axis=-2)
    o = jnp.einsum('btkrgj,btgjkd->btkrd', p.astype(vg.dtype), vg)
    return o.reshape(b, t, ATT_WIDTH)


def _attn_prompt(q, k, v, bias):
    b, s = q.shape[:2]
    dist = _branch_distances()
    qi = np.arange(Q_BLOCK)[:, None, None]
    idx = qi + W_MAX - dist[None]
    pad = ((0, 0), (W_MAX, 0), (0, 0), (0, 0))
    kp = jnp.pad(k, pad)
    vp = jnp.pad(v, pad)

    def one_block(i):
        start = i * Q_BLOCK
        qb = lax.dynamic_slice_in_dim(q, start, Q_BLOCK, axis=1)
        kw = lax.dynamic_slice_in_dim(kp, start, Q_BLOCK + W_MAX, axis=1)
        vw = lax.dynamic_slice_in_dim(vp, start, Q_BLOCK + W_MAX, axis=1)
        valid = (start + qi - dist[None]) >= 0
        return _dilated_attn(qb, kw, vw, idx, valid, bias)

    o = lax.map(one_block, jnp.arange(s // Q_BLOCK))
    return jnp.moveaxis(o, 0, 1).reshape(b, s, ATT_WIDTH)


def _attn_cached(q, k, v, k_buf, v_buf, bias):
    t = q.shape[1]
    w = k_buf.shape[1]
    kc = jnp.concatenate([k_buf.astype(k.dtype), k], axis=1)
    vc = jnp.concatenate([v_buf.astype(v.dtype), v], axis=1)
    idx = (w + np.arange(t))[:, None, None] - _branch_distances()[None]
    valid = idx >= 0
    return _dilated_attn(q, kc, vc, np.maximum(idx, 0), valid, bias)


def _causal_conv(xbc, buf, w, b):
    t = xbc.shape[1]
    xp = jnp.concatenate([buf.astype(xbc.dtype), xbc], axis=1)
    y = b
    for i in range(CONV_W):
        y = y + xp[:, i:i + t] * w[i]
    return jax.nn.silu(y), xp[:, t:]


def _ssm_prepare(xbc, dt_raw, dt_bias, a_log):
    b, t = xbc.shape[:2]
    xs, bg, cg = jnp.split(xbc, [SSM_WIDTH, SSM_WIDTH + N_SSM_GROUPS * D_STATE], axis=-1)
    xh = xs.reshape(b, t, N_SSM_HEADS, SSM_HEAD_DIM)
    rep = N_SSM_HEADS // N_SSM_GROUPS
    bh = jnp.repeat(bg.reshape(b, t, N_SSM_GROUPS, D_STATE), rep, axis=2)
    ch = jnp.repeat(cg.reshape(b, t, N_SSM_GROUPS, D_STATE), rep, axis=2)
    dt = jax.nn.softplus(dt_raw.astype(jnp.float32) + dt_bias.astype(jnp.float32))
    da = dt * (-jnp.exp(a_log.astype(jnp.float32)))
    xdt = xh.astype(jnp.float32) * dt[..., None]
    return xh, xdt, da, bh, ch


def _segsum(x):
    t = x.shape[-1]
    xr = jnp.broadcast_to(x[..., None], x.shape + (t,))
    ss = jnp.cumsum(jnp.where(np.tril(np.ones((t, t), bool), -1), xr, 0.0), axis=-2)
    return jnp.where(np.tril(np.ones((t, t), bool)), ss, -jnp.inf)


def _ssd_chunked(xdt, da, bh, ch):
    b, l, h, p = xdt.shape
    n = bh.shape[-1]
    c = l // SSD_CHUNK
    x = xdt.reshape(b, c, SSD_CHUNK, h, p)
    bc = bh.reshape(b, c, SSD_CHUNK, h, n)
    cc = ch.reshape(b, c, SSD_CHUNK, h, n)
    a = jnp.transpose(da.reshape(b, c, SSD_CHUNK, h), (0, 3, 1, 2))
    a_cs = jnp.cumsum(a, axis=-1)
    decay = jnp.exp(_segsum(a))
    scores = jnp.einsum('bclhn,bcshn->bhcls', cc, bc).astype(jnp.float32) * decay
    y_diag = jnp.einsum('bhcls,bcshp->bclhp', scores, x)
    decay_states = jnp.exp(a_cs[..., -1:] - a_cs)
    states = jnp.einsum('bclhn,bhcl,bclhp->bchpn', bc, decay_states, x)
    states = jnp.concatenate([jnp.zeros_like(states[:, :1]), states], axis=1)
    decay_chunk = jnp.exp(_segsum(jnp.pad(a_cs[..., -1], ((0, 0), (0, 0), (1, 0)))))
    states = jnp.einsum('bhzc,bchpn->bzhpn', decay_chunk, states)
    y_off = jnp.einsum('bclhn,bchpn,bhcl->bclhp', cc, states[:, :-1], jnp.exp(a_cs))
    return (y_diag + y_off).reshape(b, l, h, p), states[:, -1]


def _ssd_recurrent(xdt, da, bh, ch, h0):
    def step(hs, inp):
        x_t, a_t, b_t, c_t = inp
        hs = jnp.exp(a_t)[..., None, None] * hs + jnp.einsum('bhp,bhn->bhpn', x_t, b_t).astype(jnp.float32)
        return hs, jnp.einsum('bhpn,bhn->bhp', hs, c_t).astype(jnp.float32)
    h_t, ys = lax.scan(step, h0.astype(jnp.float32),
                       (jnp.moveaxis(xdt, 1, 0), jnp.moveaxis(da, 1, 0),
                        jnp.moveaxis(bh, 1, 0), jnp.moveaxis(ch, 1, 0)))
    return jnp.moveaxis(ys, 0, 1), h_t


def _ssm_output(y, xh, z, d_skip, g):
    b, t = z.shape[:2]
    gw = SSM_WIDTH // N_SSM_GROUPS
    y = y + d_skip.astype(jnp.float32)[:, None] * xh.astype(jnp.float32)
    u = y.reshape(b, t, N_SSM_GROUPS, gw) * jax.nn.silu(z.astype(jnp.float32)).reshape(b, t, N_SSM_GROUPS, gw)
    u = u * lax.rsqrt(jnp.mean(u * u, axis=-1, keepdims=True) + EPS)
    return (u.reshape(b, t, SSM_WIDTH) * g.astype(jnp.float32)).astype(z.dtype)


def _ssm_branch(xbc, dt_raw, z, conv_buf, h0, conv_w, conv_b, dt_bias, a_log, d_skip, ssd_norm):
    xbc_a, conv_new = _causal_conv(xbc, conv_buf, conv_w, conv_b)
    xh, xdt, da, bh, ch = _ssm_prepare(xbc_a, dt_raw, dt_bias, a_log)
    if h0 is None:
        y, h_new = _ssd_chunked(xdt, da, bh, ch)
    else:
        y, h_new = _ssd_recurrent(xdt, da, bh, ch, h0)
    return _ssm_output(y, xh, z, d_skip, ssd_norm), conv_new, h_new


def _moe(xt, w_router, b_router, w_gu, b_gu, w_dn, b_dn):
    m, d = xt.shape
    logits = (xt @ w_router).astype(jnp.float32) + b_router.astype(jnp.float32)
    top_v, top_i = lax.top_k(logits, TOP_K)
    gate = jax.nn.softmax(top_v, axis=-1)
    a = m * TOP_K
    flat_e = top_i.reshape(a)
    order = jnp.argsort(flat_e)
    se = flat_e[order]
    stok = (order // TOP_K).astype(jnp.int32)
    sw = gate.reshape(a)[order]
    counts = jnp.zeros((N_EXPERTS,), jnp.int32).at[flat_e].add(1)
    starts = jnp.cumsum(counts) - counts
    pcounts = (counts + MOE_BLOCK - 1) // MOE_BLOCK * MOE_BLOCK
    pends = jnp.cumsum(pcounts)
    dest = pends[se] - pcounts[se] + jnp.arange(a, dtype=jnp.int32) - starts[se]
    n_blk = (a + N_EXPERTS * (MOE_BLOCK - 1)) // MOE_BLOCK + 1
    r = n_blk * MOE_BLOCK
    slot_tok = jnp.full((r,), m, jnp.int32).at[dest].set(stok)
    slot_w = jnp.zeros((r,), jnp.float32).at[dest].set(sw)
    blk_e = jnp.minimum(jnp.searchsorted(pends, jnp.arange(n_blk, dtype=jnp.int32) * MOE_BLOCK, side='right'),
                        N_EXPERTS - 1)
    xpad = jnp.concatenate([xt, jnp.zeros((1, d), xt.dtype)], axis=0)

    def one_block(args):
        e, tok, w = args
        hdn = xpad[tok] @ w_gu[e] + b_gu[e]
        g = jnp.minimum(hdn[:, :D_FF], SWIGLU_LIMIT)
        u = jnp.clip(hdn[:, D_FF:], -SWIGLU_LIMIT, SWIGLU_LIMIT)
        act = g * jax.nn.sigmoid(SWIGLU_ALPHA * g) * (u + 1.0)
        out = act @ w_dn[e] + b_dn[e]
        return out * w[:, None].astype(out.dtype)

    ys = lax.map(one_block, (blk_e, slot_tok.reshape(n_blk, MOE_BLOCK), slot_w.reshape(n_blk, MOE_BLOCK)))
    y = jax.ops.segment_sum(ys.reshape(r, d), slot_tok, num_segments=m + 1)
    return y[:m].astype(xt.dtype)


def setup_inputs(seed: int = 0) -> dict:
    key = jax.random.key(seed)
    ks = jax.random.split(key, 24)
    f32 = jnp.float32

    def nrm(k, shape, scale):
        return jax.random.normal(k, shape, f32) * scale

    w_buf = min(W_MAX, PAST_LEN)
    dt = jnp.exp(jax.random.uniform(ks[11], (DEPTH, N_SSM_HEADS), f32, math.log(1e-3), math.log(1e-1)))
    return {
        'x_prompt': nrm(ks[0], (BATCH, SEQ, D_MODEL), 1.0),
        'x_sample': nrm(ks[1], (DEC_BATCH, DEC_SEQ, D_MODEL), 1.0),
        'cache_k_win': nrm(ks[2], (DEPTH, DEC_BATCH, w_buf, N_KV_HEADS, HEAD_DIM), 1.0),
        'cache_v_win': nrm(ks[3], (DEPTH, DEC_BATCH, w_buf, N_KV_HEADS, HEAD_DIM), 1.0),
        'state_conv': nrm(ks[4], (DEPTH, DEC_BATCH, CONV_W - 1, CONV_DIM), 1.0),
        'state_ssm': nrm(ks[5], (DEPTH, DEC_BATCH, N_SSM_HEADS, SSM_HEAD_DIM, D_STATE), 0.1),
        'rel_bias': nrm(ks[6], (N_BUCKETS, N_ATT_HEADS), 0.1),
        'attn_norm': 1.0 + nrm(ks[7], (DEPTH, D_MODEL), 0.01),
        'w_in': nrm(ks[8], (DEPTH, D_MODEL, IN_PROJ), D_MODEL ** -0.5),
        'conv_w': nrm(ks[9], (DEPTH, CONV_W, CONV_DIM), CONV_W ** -0.5),
        'conv_b': nrm(ks[10], (DEPTH, CONV_DIM), 0.01),
        'dt_bias': dt + jnp.log(-jnp.expm1(-dt)),
        'a_log': jnp.log(jax.random.uniform(ks[12], (DEPTH, N_SSM_HEADS), f32, 1.0, 16.0)),
        'd_skip': 1.0 + nrm(ks[13], (DEPTH, N_SSM_HEADS), 0.1),
        'ssd_norm': 1.0 + nrm(ks[14], (DEPTH, SSM_WIDTH), 0.01),
        'w_out': nrm(ks[15], (DEPTH, MIX_WIDTH, D_MODEL), MIX_WIDTH ** -0.5),
        'ffn_norm': 1.0 + nrm(ks[16], (DEPTH, D_MODEL), 0.01),
        'w_router': nrm(ks[17], (DEPTH, D_MODEL, N_EXPERTS), D_MODEL ** -0.5),
        'b_router': nrm(ks[18], (DEPTH, N_EXPERTS), 0.01),
        'w_gate_up': nrm(ks[19], (DEPTH, N_EXPERTS, D_MODEL, 2 * D_FF), D_MODEL ** -0.5),
        'b_gate_up': nrm(ks[20], (DEPTH, N_EXPERTS, 2 * D_FF), 0.01),
        'w_down': nrm(ks[21], (DEPTH, N_EXPERTS, D_FF, D_MODEL), D_FF ** -0.5),
        'b_down': nrm(ks[22], (DEPTH, N_EXPERTS, D_MODEL), 0.01),
        'final_norm': 1.0 + nrm(ks[23], (D_MODEL,), 0.01),
    }


def reference(x_prompt, x_sample, cache_k_win, cache_v_win, state_conv, state_ssm, rel_bias,
              attn_norm, w_in, conv_w, conv_b, dt_bias, a_log, d_skip, ssd_norm, w_out,
              ffn_norm, w_router, b_router, w_gate_up, b_gate_up, w_down, b_down, final_norm):
    dist = jnp.asarray(_branch_distances(), jnp.int32)
    bias = jnp.transpose(rel_bias[_t5_bucket(dist)], (2, 0, 1)).astype(jnp.float32)
    xp, xs = x_prompt, x_sample
    bp, tp, _ = xp.shape
    bs, ts, _ = xs.shape
    keep = min(W_MAX, tp)
    kp_l, vp_l, cp_l, sp_l, ks_l, vs_l, cs_l, ss_l = [], [], [], [], [], [], [], []
    for l in range(DEPTH):
        q, k, v, z, xbc, dt_raw = _project(_rmsnorm(xp, attn_norm[l]), w_in[l])
        att = _attn_prompt(q, k, v, bias)
        ssm, conv_new, h_new = _ssm_branch(xbc, dt_raw, z, jnp.zeros((bp, CONV_W - 1, CONV_DIM), xbc.dtype), None,
                                           conv_w[l], conv_b[l], dt_bias[l], a_log[l], d_skip[l], ssd_norm[l])
        xp = xp + jnp.concatenate([att, ssm], axis=-1) @ w_out[l]
        kp_l.append(k[:, tp - keep:])
        vp_l.append(v[:, tp - keep:])
        cp_l.append(conv_new)
        sp_l.append(h_new.astype(xp.dtype))
        q, k, v, z, xbc, dt_raw = _project(_rmsnorm(xs, attn_norm[l]), w_in[l])
        att = _attn_cached(q, k, v, cache_k_win[l], cache_v_win[l], bias)
        ssm, conv_new, h_new = _ssm_branch(xbc, dt_raw, z, state_conv[l], state_ssm[l],
                                           conv_w[l], conv_b[l], dt_bias[l], a_log[l], d_skip[l], ssd_norm[l])
        xs = xs + jnp.concatenate([att, ssm], axis=-1) @ w_out[l]
        ks_l.append(k)
        vs_l.append(v)
        cs_l.append(conv_new)
        ss_l.append(h_new.astype(xs.dtype))
        tok = jnp.concatenate([_rmsnorm(xp, ffn_norm[l]).reshape(bp * tp, D_MODEL),
                               _rmsnorm(xs, ffn_norm[l]).reshape(bs * ts, D_MODEL)], axis=0)
        f = _moe(tok, w_router[l], b_router[l], w_gate_up[l], b_gate_up[l], w_down[l], b_down[l])
        xp = xp + f[:bp * tp].reshape(bp, tp, D_MODEL)
        xs = xs + f[bp * tp:].reshape(bs, ts, D_MODEL)
    y_prompt = _rmsnorm(xp, final_norm)
    y_sample = _rmsnorm(xs, final_norm)
    return (y_prompt, y_sample, jnp.stack(kp_l), jnp.stack(vp_l), jnp.stack(cp_l), jnp.stack(sp_l),
            jnp.stack(ks_l), jnp.stack(vs_l), jnp.stack(cs_l), jnp.stack(ss_l))
```

```python
import functools
import math

import jax
import jax.numpy as jnp
import numpy as np
from jax import lax
from jax.experimental import pallas as pl
from jax.experimental.pallas import tpu as pltpu

F32 = jnp.float32
BF16 = jnp.bfloat16
HIGHEST = lax.Precision.HIGHEST

LANES = 128
SUBLANES = 8
VMEM_LIMIT = 56 * 1024 * 1024

HEAD_DIM = 64
N_ATT_HEADS = 16
N_KV_HEADS = 4
KV_REP = N_ATT_HEADS // N_KV_HEADS
ATT_WIDTH = N_ATT_HEADS * HEAD_DIM
KV_WIDTH = N_KV_HEADS * HEAD_DIM
DILATIONS = (1, 4, 16)
N_TAPS = 129
WIN = N_TAPS - 1
ATT_SCALE = HEAD_DIM ** -0.5
N_BUCKETS = 32
BUCKET_MAX_DIST = 2048
SSM_HEAD_DIM = 64
N_SSM_HEADS = 16
SSM_WIDTH = N_SSM_HEADS * SSM_HEAD_DIM
N_SSM_GROUPS = 2
HEADS_PER_GROUP = N_SSM_HEADS // N_SSM_GROUPS
D_STATE = 128
CONV_W = 4
CONV_DIM = SSM_WIDTH + 2 * N_SSM_GROUPS * D_STATE
SSD_CHUNK = 128
N_EXPERTS = 32
TOP_K = 4
SWIGLU_LIMIT = 7.0
SWIGLU_ALPHA = 1.702
EPS = 1e-5
NEG = -1e30

Q0, K0, V0, Z0, X0, DT0 = 0, 1024, 1280, 1536, 2560, 4096
IN_PROJ = DT0 + N_SSM_HEADS
IN_PROJ_PAD = DT0 + LANES

TM_PROJ = 512
TM_OUT = 256
TM_MOE = 512
TF_MOE = 512
TM_FIN = 256


def _cparams(sem):
    return pltpu.CompilerParams(dimension_semantics=sem, vmem_limit_bytes=VMEM_LIMIT)


def _const_spec(shape):
    nd = len(shape)
    return pl.BlockSpec(shape, lambda *_: (0,) * nd, pipeline_mode=pl.Buffered(1))


def _rms(x, g):
    ms = jnp.mean(x * x, axis=-1, keepdims=True)
    return x * lax.rsqrt(ms + EPS) * g


def _silu(x):
    return x * (1.0 / (1.0 + jnp.exp(-x)))


def _inproj_kernel(x_ref, g_ref, w_ref, q_ref, k_ref, v_ref, z_ref, xbc_ref, dt_ref):
    xn = _rms(x_ref[...], g_ref[...]).astype(BF16)

    def mm(lo, hi):
        return jnp.dot(xn, w_ref[:, lo:hi], preferred_element_type=F32)

    q_ref[...] = (mm(Q0, K0) * ATT_SCALE).astype(BF16)
    k_ref[...] = mm(K0, V0)
    v_ref[...] = mm(V0, Z0)
    z_ref[...] = mm(Z0, X0)
    xbc_ref[...] = mm(X0, DT0)
    dt_ref[...] = mm(DT0, IN_PROJ_PAD)


def _inproj(x2d, g, w_bf, tm):
    m, d = x2d.shape
    widths = (ATT_WIDTH, KV_WIDTH, KV_WIDTH, SSM_WIDTH, CONV_DIM, LANES)
    dtypes = (BF16, F32, F32, F32, F32, F32)
    return pl.pallas_call(
        _inproj_kernel,
        grid=(m // tm,),
        in_specs=[pl.BlockSpec((tm, d), lambda i: (i, 0)),
                  _const_spec((1, d)),
                  _const_spec((d, IN_PROJ_PAD))],
        out_specs=[pl.BlockSpec((tm, w), lambda i: (i, 0)) for w in widths],
        out_shape=[jax.ShapeDtypeStruct((m, w), t) for w, t in zip(widths, dtypes)],
        compiler_params=_cparams(("parallel",)),
        name=f"inproj_{tm}",
    )(x2d, g.reshape(1, d), w_bf)


def _attn_kernel(q_ref, kp_ref, kc_ref, vp_ref, vc_ref, bias_ref, o_ref, lse_ref):
    first = pl.program_id(2) == 0
    lane = lax.broadcasted_iota(jnp.int32, (1, 2 * WIN), 1)
    prev_mask = jnp.where(jnp.logical_and(first, lane < WIN), NEG, 0.0)
    for kvh in range(N_KV_HEADS):
        cs = slice(kvh * HEAD_DIM, (kvh + 1) * HEAD_DIM)
        kw = jnp.concatenate([kp_ref[0, :, cs], kc_ref[0, :, cs]], axis=0).astype(BF16)
        vw = jnp.concatenate([vp_ref[0, :, cs], vc_ref[0, :, cs]], axis=0).astype(BF16)
        for pair in range(KV_REP // 2):
            outs, lses = [], []
            for r in range(2):
                h = kvh * KV_REP + pair * 2 + r
                qh = q_ref[0, :, h * HEAD_DIM:(h + 1) * HEAD_DIM]
                s = lax.dot_general(qh, kw, (((1,), (1,)), ((), ())), preferred_element_type=F32)
                s = s + bias_ref[h] + prev_mask
                mx = jnp.max(s, axis=-1, keepdims=True)
                p = jnp.exp(s - mx)
                l = jnp.sum(p, axis=-1, keepdims=True)
                o = jnp.dot(p.astype(BF16), vw, preferred_element_type=F32)
                outs.append(o * (1.0 / l))
                lses.append(jnp.broadcast_to(mx + jnp.log(l), (WIN, HEAD_DIM)))
            h0 = kvh * KV_REP + pair * 2
            o_ref[0, :, h0 * HEAD_DIM:(h0 + 2) * HEAD_DIM] = jnp.concatenate(outs, axis=-1)
            lse_ref[0, :, h0 * HEAD_DIM:(h0 + 2) * HEAD_DIM] = jnp.concatenate(lses, axis=-1)


def _attn_branch(q, k, v, bias_mat, dil):
    b, s, _ = q.shape
    sub = s // dil
    nb = sub // WIN
    qv = q.reshape(b, sub, dil * ATT_WIDTH)
    kv_ = k.reshape(b, sub, dil * KV_WIDTH)
    vv = v.reshape(b, sub, dil * KV_WIDTH)
    cur = lambda bb, r, i: (bb, i, r)
    prev = lambda bb, r, i: (bb, jnp.maximum(i - 1, 0), r)
    o, lse = pl.pallas_call(
        _attn_kernel,
        grid=(b, dil, nb),
        in_specs=[pl.BlockSpec((1, WIN, ATT_WIDTH), cur),
                  pl.BlockSpec((1, WIN, KV_WIDTH), prev),
                  pl.BlockSpec((1, WIN, KV_WIDTH), cur),
                  pl.BlockSpec((1, WIN, KV_WIDTH), prev),
                  pl.BlockSpec((1, WIN, KV_WIDTH), cur),
                  _const_spec((N_ATT_HEADS, WIN, 2 * WIN))],
        out_specs=[pl.BlockSpec((1, WIN, ATT_WIDTH), cur),
                   pl.BlockSpec((1, WIN, ATT_WIDTH), cur)],
        out_shape=[jax.ShapeDtypeStruct((b, sub, dil * ATT_WIDTH), F32),
                   jax.ShapeDtypeStruct((b, sub, dil * ATT_WIDTH), F32)],
        compiler_params=_cparams(("parallel", "parallel", "arbitrary")),
        name=f"attn_dil{dil}",
    )(qv, kv_, kv_, vv, vv, bias_mat)
    return o.reshape(b * s, ATT_WIDTH), lse.reshape(b * s, ATT_WIDTH)


def _softplus(x):
    return jnp.maximum(x, 0.0) + jnp.log(1.0 + jnp.exp(-jnp.abs(x)))


def _ssd_kernel(xbc_ref, dt_ref, z_ref, cw_ref, cb_ref, dtb_ref, alog_ref, dskip_ref, gn_ref, e_ref,
                y_ref, st_ref, ext_ref, state_ref):
    c = pl.program_id(1)
    L = SSD_CHUNK

    @pl.when(c == 0)
    def _():
        ext_ref[0:SUBLANES, :] = jnp.zeros((SUBLANES, CONV_DIM), F32)
        state_ref[...] = jnp.zeros_like(state_ref)

    ext_ref[SUBLANES:SUBLANES + L, :] = xbc_ref[...]
    acc = cb_ref[...] + ext_ref[SUBLANES:SUBLANES + L, :] * cw_ref[CONV_W - 1:CONV_W, :]
    for i in range(CONV_W - 1):
        off = SUBLANES - (CONV_W - 1) + i
        acc = acc + ext_ref[off:off + L, :] * cw_ref[i:i + 1, :]
    ext_ref[0:SUBLANES, :] = ext_ref[L:L + SUBLANES, :]
    xa = _silu(acc)

    dt = _softplus(dt_ref[...] + dtb_ref[...])
    da = dt * (-jnp.exp(alog_ref[...]))
    row = lax.broadcasted_iota(jnp.int32, (L, L), 0)
    col = lax.broadcasted_iota(jnp.int32, (L, L), 1)
    tri = row >= col
    a_cs = jnp.dot(tri.astype(F32), da, preferred_element_type=F32, precision=HIGHEST)
    a_cs_t = a_cs.T
    expand = e_ref[...]
    acs_full = jnp.dot(a_cs, expand, preferred_element_type=F32, precision=HIGHEST)
    dt_full = jnp.dot(dt, expand, preferred_element_type=F32, precision=HIGHEST)
    exp_acs = jnp.exp(acs_full)
    a_last = acs_full[L - 1:L, :]
    exp_last = exp_acs[L - 1:L, :]
    xs = xa[:, :SSM_WIDTH]
    xdt = xs * dt_full
    xw = xdt * jnp.exp(a_last - acs_full)

    for g in range(N_SSM_GROUPS):
        b0 = SSM_WIDTH + g * D_STATE
        c0 = SSM_WIDTH + N_SSM_GROUPS * D_STATE + g * D_STATE
        bg_t = xa[:, b0:b0 + D_STATE].T.astype(BF16)
        cg = xa[:, c0:c0 + D_STATE].astype(BF16)
        gram = jnp.dot(cg, bg_t, preferred_element_type=F32)
        for hh in range(HEADS_PER_GROUP):
            h = g * HEADS_PER_GROUP + hh
            hs = slice(h * SSM_HEAD_DIM, (h + 1) * SSM_HEAD_DIM)
            seg = jnp.where(tri, a_cs[:, h:h + 1] - a_cs_t[h:h + 1, :], NEG)
            scores = (gram * jnp.exp(seg)).astype(BF16)
            y_diag = jnp.dot(scores, xdt[:, hs].astype(BF16), preferred_element_type=F32)
            st = state_ref[h]
            y_off = jnp.dot(cg, st.astype(BF16), preferred_element_type=F32) * exp_acs[:, hs]
            y_ref[:, hs] = y_diag + y_off
            state_ref[h] = exp_last[:, hs] * st + jnp.dot(bg_t, xw[:, hs].astype(BF16),
                                                          preferred_element_type=F32)

    y = y_ref[...] + dskip_ref[...] * xs
    u = y * _silu(z_ref[...])
    gw = SSM_WIDTH // N_SSM_GROUPS
    parts = []
    for g in range(N_SSM_GROUPS):
        ug = u[:, g * gw:(g + 1) * gw]
        parts.append(ug * lax.rsqrt(jnp.mean(ug * ug, axis=-1, keepdims=True) + EPS))
    y_ref[...] = jnp.concatenate(parts, axis=-1) * gn_ref[...]

    @pl.when(c == pl.num_programs(1) - 1)
    def _():
        st_ref[0] = state_ref[...]


def _head_expand():
    e = np.zeros((LANES, SSM_WIDTH), np.float32)
    for h in range(N_SSM_HEADS):
        e[h, h * SSM_HEAD_DIM:(h + 1) * SSM_HEAD_DIM] = 1.0
    return jnp.asarray(e)


def _pad_lanes(v):
    return jnp.pad(v.astype(F32), (0, LANES - v.shape[0])).reshape(1, LANES)


def _ssd_prompt(xbc, dt_raw, z, conv_w, conv_b, dt_bias, a_log, d_skip, ssd_norm, batch):
    m = xbc.shape[0]
    nc = m // batch // SSD_CHUNK
    L = SSD_CHUNK
    rows = lambda b, c: (b * nc + c, 0)
    y, st = pl.pallas_call(
        _ssd_kernel,
        grid=(batch, nc),
        in_specs=[pl.BlockSpec((L, CONV_DIM), rows),
                  pl.BlockSpec((L, LANES), rows),
                  pl.BlockSpec((L, SSM_WIDTH), rows),
                  _const_spec((CONV_W, CONV_DIM)),
                  _const_spec((1, CONV_DIM)),
                  _const_spec((1, LANES)),
                  _const_spec((1, LANES)),
                  _const_spec((1, SSM_WIDTH)),
                  _const_spec((1, SSM_WIDTH)),
                  _const_spec((LANES, SSM_WIDTH))],
        out_specs=[pl.BlockSpec((L, SSM_WIDTH), rows),
                   pl.BlockSpec((1, N_SSM_HEADS, D_STATE, SSM_HEAD_DIM), lambda b, c: (b, 0, 0, 0))],
        out_shape=[jax.ShapeDtypeStruct((m, SSM_WIDTH), F32),
                   jax.ShapeDtypeStruct((batch, N_SSM_HEADS, D_STATE, SSM_HEAD_DIM), F32)],
        scratch_shapes=[pltpu.VMEM((SUBLANES + L, CONV_DIM), F32),
                        pltpu.VMEM((N_SSM_HEADS, D_STATE, SSM_HEAD_DIM), F32)],
        compiler_params=_cparams(("parallel", "arbitrary")),
        name="ssd_prompt",
    )(xbc, dt_raw, z, conv_w, conv_b.reshape(1, CONV_DIM), _pad_lanes(dt_bias), _pad_lanes(a_log),
      jnp.repeat(d_skip, SSM_HEAD_DIM).reshape(1, SSM_WIDTH), ssd_norm.reshape(1, SSM_WIDTH),
      _head_expand())
    return y, jnp.swapaxes(st, -1, -2)


def _router(xn_bf, wr_ref, br_ref):
    return jnp.dot(xn_bf, wr_ref[...], preferred_element_type=F32) + br_ref[...]


def _outproj_kernel(o0, l0, o1, l1, o2, l2, ssm_ref, x_ref, w_ref, g_ref, wr_ref, br_ref,
                    x1_ref, xn_ref, lg_ref):
    la, lb, lc = l0[...], l1[...], l2[...]
    mx = jnp.maximum(jnp.maximum(la, lb), lc)
    wa, wb, wc = jnp.exp(la - mx), jnp.exp(lb - mx), jnp.exp(lc - mx)
    att = (wa * o0[...] + wb * o1[...] + wc * o2[...]) * (1.0 / (wa + wb + wc))
    y = jnp.dot(att.astype(BF16), w_ref[:ATT_WIDTH, :], preferred_element_type=F32)
    y = y + jnp.dot(ssm_ref[...].astype(BF16), w_ref[ATT_WIDTH:, :], preferred_element_type=F32)
    x1 = x_ref[...] + y
    x1_ref[...] = x1
    xn = _rms(x1, g_ref[...]).astype(BF16)
    xn_ref[...] = xn
    lg_ref[...] = _router(xn, wr_ref, br_ref)


def _outproj_prompt(att_parts, ssm, x2d, w_bf, g, wr, br):
    m, d = x2d.shape
    mix = w_bf.shape[0]
    row = lambda w: pl.BlockSpec((TM_OUT, w), lambda i: (i, 0))
    return pl.pallas_call(
        _outproj_kernel,
        grid=(m // TM_OUT,),
        in_specs=[row(ATT_WIDTH)] * 6 + [row(SSM_WIDTH), row(d),
                                         _const_spec((mix, d)), _const_spec((1, d)),
                                         _const_spec((d, LANES)), _const_spec((1, LANES))],
        out_specs=[row(d), row(d), row(LANES)],
        out_shape=[jax.ShapeDtypeStruct((m, d), F32), jax.ShapeDtypeStruct((m, d), BF16),
                   jax.ShapeDtypeStruct((m, LANES), F32)],
        compiler_params=_cparams(("parallel",)),
        name="outproj_prompt",
    )(*att_parts, ssm, x2d, w_bf, g.reshape(1, d), wr, br)


def _attn_sample_kernel(q_ref, kn_ref, vn_ref, k0_ref, k1_ref, k2_ref, v0_ref, v1_ref, v2_ref,
                        bias_ref, bias0_ref, o_ref):
    q = q_ref[0]
    head_grp = lax.broadcasted_iota(jnp.int32, (N_ATT_HEADS, 1), 0) // KV_REP
    kn = kn_ref[0].astype(BF16).astype(F32)
    vn = vn_ref[0].astype(BF16).astype(F32)
    s_self = jnp.sum(q.astype(F32) * kn, axis=-1, keepdims=True)
    scores, selfs, lses = [], [], []
    for g, kc_ref in enumerate((k0_ref, k1_ref, k2_ref)):
        kk = kc_ref[0].astype(BF16)
        s = jnp.zeros((N_ATT_HEADS, WIN), F32)
        for kvh in range(N_KV_HEADS):
            cs = slice(kvh * HEAD_DIM, (kvh + 1) * HEAD_DIM)
            sk = lax.dot_general(q, kk[:, cs], (((1,), (1,)), ((), ())), preferred_element_type=F32)
            s = jnp.where(head_grp == kvh, sk, s)
        s = s + bias_ref[g]
        s0 = s_self + bias0_ref[g]
        mx = jnp.maximum(jnp.max(s, axis=-1, keepdims=True), s0)
        lse = mx + jnp.log(jnp.sum(jnp.exp(s - mx), axis=-1, keepdims=True) + jnp.exp(s0 - mx))
        scores.append(s); selfs.append(s0); lses.append(lse)
    top = functools.reduce(jnp.maximum, lses)
    es = [jnp.exp(l - top) for l in lses]
    tot = functools.reduce(jnp.add, es)
    o = jnp.zeros((N_ATT_HEADS, HEAD_DIM), F32)
    for s, s0, lse, e, vc_ref in zip(scores, selfs, lses, es, (v0_ref, v1_ref, v2_ref)):
        wgt = e / tot
        p = (jnp.exp(s - lse) * wgt).astype(BF16)
        p0 = (jnp.exp(s0 - lse) * wgt).astype(BF16).astype(F32)
        vv = vc_ref[0].astype(BF16)
        o = o + p0 * vn
        for kvh in range(N_KV_HEADS):
            cs = slice(kvh * HEAD_DIM, (kvh + 1) * HEAD_DIM)
            ok = jnp.dot(p, vv[:, cs], preferred_element_type=F32)
            o = o + jnp.where(head_grp == kvh, ok, 0.0)
    o_ref[0] = o


def _attn_sample(q, k_new, v_new, k_cache, v_cache, bias_s, bias0_s):
    n, w = k_cache.shape[0], k_cache.shape[1]
    assert w % (max(DILATIONS) * WIN) == 0
    tok = lambda b: (b, 0, 0)
    head = pl.BlockSpec((1, N_ATT_HEADS, HEAD_DIM), tok)
    taps = [pl.BlockSpec((1, WIN, KV_WIDTH), functools.partial(lambda nb, b: (b, nb - 1, 0), w // dil // WIN))
            for dil in DILATIONS]
    views = lambda c: [c.reshape(n, w // dil, dil * KV_WIDTH) for dil in DILATIONS]
    return pl.pallas_call(
        _attn_sample_kernel,
        grid=(n,),
        in_specs=[head, head, head] + taps + taps +
                 [_const_spec((len(DILATIONS), N_ATT_HEADS, WIN)),
                  _const_spec((len(DILATIONS), N_ATT_HEADS, 1))],
        out_specs=head,
        out_shape=jax.ShapeDtypeStruct((n, N_ATT_HEADS, HEAD_DIM), F32),
        compiler_params=_cparams(("parallel",)),
        name="attn_sample",
    )(q, jnp.repeat(k_new, KV_REP, axis=1), jnp.repeat(v_new, KV_REP, axis=1), *views(k_cache),
      *views(v_cache), bias_s, bias0_s)


def _conv_sample_kernel(xbc_ref, b0_ref, b1_ref, b2_ref, cw_ref, cb_ref, dt_ref, dtb_ref, alog_ref,
                        e_ref, xa_ref, xdt_ref, decay_ref):
    acc = cb_ref[...] + xbc_ref[...] * cw_ref[CONV_W - 1:CONV_W, :]
    for i, buf in enumerate((b0_ref, b1_ref, b2_ref)):
        acc = acc + buf[...] * cw_ref[i:i + 1, :]
    xa = _silu(acc)
    xa_ref[...] = xa
    dt = _softplus(dt_ref[...] + dtb_ref[...])
    decay_ref[...] = jnp.exp(dt * (-jnp.exp(alog_ref[...])))
    dt_full = jnp.dot(dt, e_ref[...], preferred_element_type=F32, precision=HIGHEST)
    xdt_ref[...] = xa[:, :SSM_WIDTH] * dt_full


def _conv_sample(xbc, conv_buf, conv_w, conv_b, dt_raw, dt_bias, a_log):
    n = xbc.shape[0]
    args = (xbc, conv_buf[:, 0], conv_buf[:, 1], conv_buf[:, 2], conv_w, conv_b.reshape(1, CONV_DIM),
            dt_raw, _pad_lanes(dt_bias), _pad_lanes(a_log),
            _head_expand())
    return pl.pallas_call(
        _conv_sample_kernel,
        grid=(1,),
        in_specs=[_const_spec(a.shape) for a in args],
        out_specs=[_const_spec((n, CONV_DIM)), _const_spec((n, SSM_WIDTH)), _const_spec((n, LANES))],
        out_shape=[jax.ShapeDtypeStruct((n, CONV_DIM), F32), jax.ShapeDtypeStruct((n, SSM_WIDTH), F32),
                   jax.ShapeDtypeStruct((n, LANES), F32)],
        compiler_params=_cparams(("arbitrary",)),
        name="conv_sample",
    )(*args)


def _ssm_sample_kernel(xdt_ref, decay_ref, b_ref, c_ref, h0_ref, hn_ref, y_ref):
    for g in range(N_SSM_GROUPS):
        hs = slice(g * HEADS_PER_GROUP, (g + 1) * HEADS_PER_GROUP)
        hn = decay_ref[0, hs] * h0_ref[0, hs] + xdt_ref[0, hs] * b_ref[0, g]
        hn_ref[0, hs] = hn
        c_row = c_ref[0, g].astype(BF16).astype(F32)
        y_ref[0, hs] = jnp.sum(hn.astype(BF16).astype(F32) * c_row, axis=-1, keepdims=True)


def _ssm_sample(xdt, decay, bmat, cmat, h0):
    n = xdt.shape[0]
    p = SSM_HEAD_DIM
    tok4 = lambda b: (b, 0, 0, 0)
    hn, y = pl.pallas_call(
        _ssm_sample_kernel,
        grid=(n,),
        in_specs=[pl.BlockSpec((1, N_SSM_HEADS, p, 1), tok4),
                  pl.BlockSpec((1, N_SSM_HEADS, 1, 1), tok4),
                  pl.BlockSpec((1, N_SSM_GROUPS, 1, D_STATE), tok4),
                  pl.BlockSpec((1, N_SSM_GROUPS, 1, D_STATE), tok4),
                  pl.BlockSpec((1, N_SSM_HEADS, p, D_STATE), tok4)],
        out_specs=[pl.BlockSpec((1, N_SSM_HEADS, p, D_STATE), tok4),
                   pl.BlockSpec((1, N_SSM_HEADS, p, 1), tok4)],
        out_shape=[jax.ShapeDtypeStruct((n, N_SSM_HEADS, p, D_STATE), F32),
                   jax.ShapeDtypeStruct((n, N_SSM_HEADS, p, 1), F32)],
        compiler_params=_cparams(("parallel",)),
        name="ssm_sample",
    )(xdt.reshape(n, N_SSM_HEADS, p, 1), decay[:, :N_SSM_HEADS].reshape(n, N_SSM_HEADS, 1, 1),
      bmat.reshape(n, N_SSM_GROUPS, 1, D_STATE), cmat.reshape(n, N_SSM_GROUPS, 1, D_STATE), h0)
    return hn, y.reshape(n, SSM_WIDTH)


def _outproj_sample_kernel(att_ref, y_ref, xs_ref, z_ref, dskip_ref, gn_ref, x_ref, w_ref, g_ref,
                           wr_ref, br_ref, x1_ref, xn_ref, lg_ref):
    y = y_ref[...] + dskip_ref[...] * xs_ref[...]
    u = y * _silu(z_ref[...])
    gw = SSM_WIDTH // N_SSM_GROUPS
    parts = []
    for g in range(N_SSM_GROUPS):
        ug = u[:, g * gw:(g + 1) * gw]
        parts.append(ug * lax.rsqrt(jnp.mean(ug * ug, axis=-1, keepdims=True) + EPS)
                     * gn_ref[:, g * gw:(g + 1) * gw])
    mix = jnp.concatenate([att_ref[...]] + parts, axis=-1).astype(BF16)
    x1 = x_ref[...] + jnp.dot(mix, w_ref[...], preferred_element_type=F32)
    x1_ref[...] = x1
    xn = _rms(x1, g_ref[...]).astype(BF16)
    xn_ref[...] = xn
    lg_ref[...] = _router(xn, wr_ref, br_ref)


def _outproj_sample(att, y, xs, z, d_skip, ssd_norm, x2d, w_bf, g, wr, br):
    n, d = x2d.shape
    args = (att, y, xs, z, jnp.repeat(d_skip, SSM_HEAD_DIM).reshape(1, SSM_WIDTH),
            ssd_norm.reshape(1, SSM_WIDTH), x2d, w_bf, g.reshape(1, d), wr, br)
    return pl.pallas_call(
        _outproj_sample_kernel,
        grid=(1,),
        in_specs=[_const_spec(a.shape) for a in args],
        out_specs=[_const_spec((n, d)), _const_spec((n, d)), _const_spec((n, LANES))],
        out_shape=[jax.ShapeDtypeStruct((n, d), F32), jax.ShapeDtypeStruct((n, d), BF16),
                   jax.ShapeDtypeStruct((n, LANES), F32)],
        compiler_params=_cparams(("arbitrary",)),
        name="outproj_sample",
    )(*args)


def _moe_kernel(blk_e_ref, n_used_ref, x_ref, wg_ref, wu_ref, bg_ref, bu_ref, wd_ref, bd_ref, sw_ref,
                o_ref):
    i = pl.program_id(0)
    f = pl.program_id(1)

    @pl.when(i < n_used_ref[0])
    def _():
        x = x_ref[...]
        hg = jnp.dot(x, wg_ref[0], preferred_element_type=F32) + bg_ref[0]
        hu = jnp.dot(x, wu_ref[0], preferred_element_type=F32) + bu_ref[0]
        gg = jnp.minimum(hg, SWIGLU_LIMIT)
        uu = jnp.clip(hu, -SWIGLU_LIMIT, SWIGLU_LIMIT)
        act = gg * (1.0 / (1.0 + jnp.exp(-SWIGLU_ALPHA * gg))) * (uu + 1.0)
        part = jnp.dot(act.astype(BF16), wd_ref[0], preferred_element_type=F32)

        @pl.when(f == 0)
        def _():
            o_ref[...] = part + bd_ref[0]

        @pl.when(f > 0)
        def _():
            o_ref[...] += part

        @pl.when(f == pl.num_programs(1) - 1)
        def _():
            o_ref[...] = o_ref[...] * sw_ref[...]


def _moe_ffn(x_sorted, slot_w, blk_e, n_used, wgu_bf, b_gu, wd_bf, b_dn):
    r, d = x_sorted.shape
    n_blk = r // TM_MOE
    d_ff = wd_bf.shape[1]
    nf = d_ff // TF_MOE

    def blk(i, n_used_ref):
        return jnp.minimum(i, n_used_ref[0] - 1)

    rows = lambda i, f, be, nu: (blk(i, nu), 0)
    expert = lambda i, be, nu: be[blk(i, nu)]
    bgu3 = b_gu.reshape(N_EXPERTS, 1, 2 * d_ff)
    bdn3 = b_dn.reshape(N_EXPERTS, 1, d)
    grid_spec = pltpu.PrefetchScalarGridSpec(
        num_scalar_prefetch=2,
        grid=(n_blk, nf),
        in_specs=[pl.BlockSpec((TM_MOE, d), rows),
                  pl.BlockSpec((1, d, TF_MOE), lambda i, f, be, nu: (expert(i, be, nu), 0, f)),
                  pl.BlockSpec((1, d, TF_MOE), lambda i, f, be, nu: (expert(i, be, nu), 0, nf + f)),
                  pl.BlockSpec((1, 1, TF_MOE), lambda i, f, be, nu: (expert(i, be, nu), 0, f)),
                  pl.BlockSpec((1, 1, TF_MOE), lambda i, f, be, nu: (expert(i, be, nu), 0, nf + f)),
                  pl.BlockSpec((1, TF_MOE, d), lambda i, f, be, nu: (expert(i, be, nu), f, 0)),
                  pl.BlockSpec((1, 1, d), lambda i, f, be, nu: (expert(i, be, nu), 0, 0)),
                  pl.BlockSpec((TM_MOE, 1), rows)],
        out_specs=pl.BlockSpec((TM_MOE, d), rows),
    )
    return pl.pallas_call(
        _moe_kernel,
        grid_spec=grid_spec,
        out_shape=jax.ShapeDtypeStruct((r, d), F32),
        compiler_params=_cparams(("arbitrary", "arbitrary")),
        name="moe_ffn",
    )(blk_e, n_used, x_sorted, wgu_bf, wgu_bf, bgu3, bgu3, wd_bf, bdn3, slot_w.reshape(r, 1))


def _route(logits, m):
    top_v, top_i = lax.top_k(logits, TOP_K)
    gate = jax.nn.softmax(top_v, axis=-1)
    a = m * TOP_K
    flat_e = top_i.reshape(a)
    order = jnp.argsort(flat_e)
    se = flat_e[order]
    stok = (order // TOP_K).astype(jnp.int32)
    sw = gate.reshape(a)[order]
    counts = jnp.zeros((N_EXPERTS,), jnp.int32).at[flat_e].add(1)
    starts = jnp.cumsum(counts) - counts
    pcounts = (counts + TM_MOE - 1) // TM_MOE * TM_MOE
    pends = jnp.cumsum(pcounts)
    dest = pends[se] - pcounts[se] + jnp.arange(a, dtype=jnp.int32) - starts[se]
    n_blk = (a + N_EXPERTS * (TM_MOE - 1)) // TM_MOE + 1
    r = n_blk * TM_MOE
    slot_tok = jnp.full((r,), m, jnp.int32).at[dest].set(stok)
    slot_w = jnp.zeros((r,), F32).at[dest].set(sw)
    blk_e = jnp.minimum(jnp.searchsorted(pends, jnp.arange(n_blk, dtype=jnp.int32) * TM_MOE, side='right'),
                        N_EXPERTS - 1).astype(jnp.int32)
    n_used = jnp.maximum(pends[-1] // TM_MOE, 1).astype(jnp.int32).reshape(1)
    pos = jnp.zeros((a,), jnp.int32).at[order].set(dest)
    return slot_tok, slot_w, blk_e, n_used, pos


def _final_kernel(x_ref, y_ref, g_ref, o_ref):
    d = x_ref.shape[1]
    f = y_ref[:, 0:d]
    for kk in range(1, TOP_K):
        f = f + y_ref[:, kk * d:(kk + 1) * d]
    o_ref[...] = _rms(x_ref[...] + f, g_ref[...])


def _final(x1, yg, g, tm):
    m, d = x1.shape
    return pl.pallas_call(
        _final_kernel,
        grid=(m // tm,),
        in_specs=[pl.BlockSpec((tm, d), lambda i: (i, 0)),
                  pl.BlockSpec((tm, TOP_K * d), lambda i: (i, 0)),
                  _const_spec((1, d))],
        out_specs=pl.BlockSpec((tm, d), lambda i: (i, 0)),
        out_shape=jax.ShapeDtypeStruct((m, d), F32),
        compiler_params=_cparams(("parallel",)),
        name="final_norm",
    )(x1, yg, g.reshape(1, d))


def _t5_bucket(dist):
    max_exact = N_BUCKETS // 2
    dd = dist.astype(F32)
    large = max_exact + (jnp.log(jnp.maximum(dd, 1.0) / max_exact)
                         / math.log(BUCKET_MAX_DIST / max_exact) * (N_BUCKETS - max_exact)).astype(jnp.int32)
    large = jnp.minimum(large, N_BUCKETS - 1)
    return jnp.where(dist < max_exact, dist, large)


def _bias_tables(rel_bias):
    dist = jnp.asarray(np.arange(N_TAPS)[None, :] * np.array(DILATIONS)[:, None], jnp.int32)
    bias = jnp.transpose(rel_bias[_t5_bucket(dist)], (2, 0, 1)).astype(F32)
    tap = WIN + np.arange(WIN)[:, None] - np.arange(2 * WIN)[None, :]
    ok = (tap >= 0) & (tap <= WIN)
    band = jnp.where(ok[None, None], jnp.transpose(bias, (1, 0, 2))[:, :, np.clip(tap, 0, WIN)], NEG)
    by_branch = jnp.transpose(bias, (1, 0, 2))
    return band, by_branch[:, :, WIN - np.arange(WIN)], by_branch[:, :, 0:1]


def kernel(x_prompt, x_sample, cache_k_win, cache_v_win, state_conv, state_ssm, rel_bias, attn_norm, w_in, conv_w, conv_b, dt_bias, a_log, d_skip, ssd_norm, w_out, ffn_norm, w_router, b_router, w_gate_up, b_gate_up, w_down, b_down, final_norm):
    bp, tp, d = x_prompt.shape
    bs, ts, _ = x_sample.shape
    depth = w_in.shape[0]
    assert depth == 1 and ts == 1 and tp % (max(DILATIONS) * WIN) == 0
    keep = min(max(DILATIONS) * WIN, tp)
    band, samp, samp0 = _bias_tables(rel_bias)
    l = 0

    xp = x_prompt.reshape(bp * tp, d)
    xs = x_sample.reshape(bs * ts, d)
    w_in_bf = jnp.pad(w_in[l], ((0, 0), (0, IN_PROJ_PAD - IN_PROJ))).astype(BF16)
    w_out_bf = w_out[l].astype(BF16)
    wr = jnp.pad(w_router[l], ((0, 0), (0, LANES - N_EXPERTS))).astype(BF16)
    br = jnp.pad(b_router[l], (0, LANES - N_EXPERTS), constant_values=NEG).reshape(1, LANES)

    q, k, v, z, xbc, dt_raw = _inproj(xp, attn_norm[l], w_in_bf, TM_PROJ)
    q3 = q.reshape(bp, tp, ATT_WIDTH)
    k3 = k.reshape(bp, tp, KV_WIDTH)
    v3 = v.reshape(bp, tp, KV_WIDTH)
    att_parts = []
    for gi, dil in enumerate(DILATIONS):
        att_parts.extend(_attn_branch(q3, k3, v3, band[gi], dil))
    ssm, st_p = _ssd_prompt(xbc, dt_raw, z, conv_w[l], conv_b[l], dt_bias[l], a_log[l], d_skip[l],
                            ssd_norm[l], bp)
    x1p, xnp_, lgp = _outproj_prompt(att_parts, ssm, xp, w_out_bf, ffn_norm[l], wr, br)
    k_win_p = k3[:, tp - keep:].reshape(1, bp, keep, N_KV_HEADS, HEAD_DIM)
    v_win_p = v3[:, tp - keep:].reshape(1, bp, keep, N_KV_HEADS, HEAD_DIM)
    conv_p = xbc.reshape(bp, tp, CONV_DIM)[:, tp - (CONV_W - 1):][None]

    q_s, k_s, v_s, z_s, xbc_s, dt_s = _inproj(xs, attn_norm[l], w_in_bf, bs * ts)
    q_s = q_s.reshape(bs, N_ATT_HEADS, HEAD_DIM)
    k_s = k_s.reshape(bs, N_KV_HEADS, HEAD_DIM)
    v_s = v_s.reshape(bs, N_KV_HEADS, HEAD_DIM)
    wbuf = cache_k_win.shape[2]
    att_s = _attn_sample(q_s, k_s, v_s, cache_k_win[l].reshape(bs, wbuf, KV_WIDTH),
                         cache_v_win[l].reshape(bs, wbuf, KV_WIDTH), samp, samp0)
    xa_s, xdt_s, decay_s = _conv_sample(xbc_s, state_conv[l], conv_w[l], conv_b[l], dt_s, dt_bias[l],
                                        a_log[l])
    nbc = N_SSM_GROUPS * D_STATE
    h_s, y_s = _ssm_sample(xdt_s, decay_s, xa_s[:, SSM_WIDTH:SSM_WIDTH + nbc], xa_s[:, SSM_WIDTH + nbc:],
                           state_ssm[l])
    x1s, xns, lgs = _outproj_sample(att_s.reshape(bs, ATT_WIDTH), y_s, xa_s[:, :SSM_WIDTH], z_s,
                                    d_skip[l], ssd_norm[l], xs, w_out_bf, ffn_norm[l], wr, br)
    conv_s = jnp.concatenate([state_conv[l][:, 1:], xbc_s[:, None]], axis=1)[None]

    m = bp * tp + bs * ts
    logits = jnp.concatenate([lgp[:, :N_EXPERTS], lgs[:, :N_EXPERTS]], axis=0)
    slot_tok, slot_w, blk_e, n_used, pos = _route(logits, m)
    xpad = jnp.concatenate([xnp_, xns, jnp.zeros((1, d), BF16)], axis=0)
    x_sorted = jnp.take(xpad, slot_tok, axis=0)
    out_sorted = _moe_ffn(x_sorted, slot_w, blk_e, n_used, w_gate_up[l].astype(BF16), b_gate_up[l],
                          w_down[l].astype(BF16), b_down[l])
    yg = jnp.take(out_sorted, pos, axis=0).reshape(m, TOP_K * d)
    y_p = _final(x1p, yg[:bp * tp], final_norm, TM_FIN)
    y_s_out = _final(x1s, yg[bp * tp:], final_norm, bs * ts)

    return (y_p.reshape(bp, tp, d), y_s_out.reshape(bs, ts, d), k_win_p, v_win_p, conv_p, st_p[None],
            k_s.reshape(1, bs, ts, N_KV_HEADS, HEAD_DIM), v_s.reshape(1, bs, ts, N_KV_HEADS, HEAD_DIM),
            conv_s, h_s[None])
```

```python
import functools
import math

import jax
import jax.numpy as jnp
import numpy as np
from jax import lax
from jax.experimental import pallas as pl
from jax.experimental.pallas import tpu as pltpu

F32 = jnp.float32
BF16 = jnp.bfloat16
HIGHEST = lax.Precision.HIGHEST

LANES = 128
SUBLANES = 8
VMEM_LIMIT = 56 * 1024 * 1024

HEAD_DIM = 64
N_ATT_HEADS = 16
N_KV_HEADS = 4
KV_REP = N_ATT_HEADS // N_KV_HEADS
ATT_WIDTH = N_ATT_HEADS * HEAD_DIM
KV_WIDTH = N_KV_HEADS * HEAD_DIM
DILATIONS = (1, 4, 16)
N_TAPS = 129
WIN = N_TAPS - 1
ATT_SCALE = HEAD_DIM ** -0.5
N_BUCKETS = 32
BUCKET_MAX_DIST = 2048
SSM_HEAD_DIM = 64
N_SSM_HEADS = 16
SSM_WIDTH = N_SSM_HEADS * SSM_HEAD_DIM
N_SSM_GROUPS = 2
HEADS_PER_GROUP = N_SSM_HEADS // N_SSM_GROUPS
D_STATE = 128
CONV_W = 4
CONV_DIM = SSM_WIDTH + 2 * N_SSM_GROUPS * D_STATE
SSD_CHUNK = 128
N_EXPERTS = 32
TOP_K = 4
SWIGLU_LIMIT = 7.0
SWIGLU_ALPHA = 1.702
EPS = 1e-5
NEG = -1e30

Q0, K0, V0, Z0, X0, DT0 = 0, 1024, 1280, 1536, 2560, 4096
IN_PROJ = DT0 + N_SSM_HEADS
IN_PROJ_PAD = DT0 + LANES

TM_PROJ = 512
TM_OUT = 256
TM_MOE = 512
TF_MOE = 512
TM_FIN = 256


def _cparams(sem):
    return pltpu.CompilerParams(dimension_semantics=sem, vmem_limit_bytes=VMEM_LIMIT)


def _const_spec(shape):
    nd = len(shape)
    return pl.BlockSpec(shape, lambda *_: (0,) * nd, pipeline_mode=pl.Buffered(1))


def _rms(x, g):
    ms = jnp.mean(x * x, axis=-1, keepdims=True)
    return x * lax.rsqrt(ms + EPS) * g


def _silu(x):
    return x * (1.0 / (1.0 + jnp.exp(-x)))


def _inproj_kernel(x_ref, g_ref, w_ref, q_ref, k_ref, v_ref, z_ref, xbc_ref, dt_ref):
    xn = _rms(x_ref[...], g_ref[...]).astype(BF16)

    def mm(lo, hi):
        return jnp.dot(xn, w_ref[:, lo:hi], preferred_element_type=F32)

    q_ref[...] = (mm(Q0, K0) * ATT_SCALE).astype(BF16)
    k_ref[...] = mm(K0, V0)
    v_ref[...] = mm(V0, Z0)
    z_ref[...] = mm(Z0, X0)
    xbc_ref[...] = mm(X0, DT0)
    dt_ref[...] = mm(DT0, IN_PROJ_PAD)


def _inproj(x2d, g, w_bf, tm):
    m, d = x2d.shape
    widths = (ATT_WIDTH, KV_WIDTH, KV_WIDTH, SSM_WIDTH, CONV_DIM, LANES)
    dtypes = (BF16, F32, F32, F32, F32, F32)
    return pl.pallas_call(
        _inproj_kernel,
        grid=(m // tm,),
        in_specs=[pl.BlockSpec((tm, d), lambda i: (i, 0)),
                  _const_spec((1, d)),
                  _const_spec((d, IN_PROJ_PAD))],
        out_specs=[pl.BlockSpec((tm, w), lambda i: (i, 0)) for w in widths],
        out_shape=[jax.ShapeDtypeStruct((m, w), t) for w, t in zip(widths, dtypes)],
        compiler_params=_cparams(("parallel",)),
        name=f"inproj_{tm}",
    )(x2d, g.reshape(1, d), w_bf)


def _attn_kernel(q_ref, kp_ref, kc_ref, vp_ref, vc_ref, bias_ref, o_ref, lse_ref):
    first = pl.program_id(2) == 0
    lane = lax.broadcasted_iota(jnp.int32, (1, 2 * WIN), 1)
    prev_mask = jnp.where(jnp.logical_and(first, lane < WIN), NEG, 0.0)
    for kvh in range(N_KV_HEADS):
        cs = slice(kvh * HEAD_DIM, (kvh + 1) * HEAD_DIM)
        kw = jnp.concatenate([kp_ref[0, :, cs], kc_ref[0, :, cs]], axis=0).astype(BF16)
        vw = jnp.concatenate([vp_ref[0, :, cs], vc_ref[0, :, cs]], axis=0).astype(BF16)
        for pair in range(KV_REP // 2):
            outs, lses = [], []
            for r in range(2):
                h = kvh * KV_REP + pair * 2 + r
                qh = q_ref[0, :, h * HEAD_DIM:(h + 1) * HEAD_DIM]
                s = lax.dot_general(qh, kw, (((1,), (1,)), ((), ())), preferred_element_type=F32)
                s = s + bias_ref[h] + prev_mask
                mx = jnp.max(s, axis=-1, keepdims=True)
                p = jnp.exp(s - mx)
                l = jnp.sum(p, axis=-1, keepdims=True)
                o = jnp.dot(p.astype(BF16), vw, preferred_element_type=F32)
                outs.append(o * (1.0 / l))
                lses.append(jnp.broadcast_to(mx + jnp.log(l), (WIN, HEAD_DIM)))
            h0 = kvh * KV_REP + pair * 2
            o_ref[0, :, h0 * HEAD_DIM:(h0 + 2) * HEAD_DIM] = jnp.concatenate(outs, axis=-1)
            lse_ref[0, :, h0 * HEAD_DIM:(h0 + 2) * HEAD_DIM] = jnp.concatenate(lses, axis=-1)


def _attn_branch(q, k, v, bias_mat, dil):
    b, s, _ = q.shape
    sub = s // dil
    nb = sub // WIN
    qv = q.reshape(b, sub, dil * ATT_WIDTH)
    kv_ = k.reshape(b, sub, dil * KV_WIDTH)
    vv = v.reshape(b, sub, dil * KV_WIDTH)
    cur = lambda bb, r, i: (bb, i, r)
    prev = lambda bb, r, i: (bb, jnp.maximum(i - 1, 0), r)
    o, lse = pl.pallas_call(
        _attn_kernel,
        grid=(b, dil, nb),
        in_specs=[pl.BlockSpec((1, WIN, ATT_WIDTH), cur),
                  pl.BlockSpec((1, WIN, KV_WIDTH), prev),
                  pl.BlockSpec((1, WIN, KV_WIDTH), cur),
                  pl.BlockSpec((1, WIN, KV_WIDTH), prev),
                  pl.BlockSpec((1, WIN, KV_WIDTH), cur),
                  _const_spec((N_ATT_HEADS, WIN, 2 * WIN))],
        out_specs=[pl.BlockSpec((1, WIN, ATT_WIDTH), cur),
                   pl.BlockSpec((1, WIN, ATT_WIDTH), cur)],
        out_shape=[jax.ShapeDtypeStruct((b, sub, dil * ATT_WIDTH), F32),
                   jax.ShapeDtypeStruct((b, sub, dil * ATT_WIDTH), F32)],
        compiler_params=_cparams(("parallel", "parallel", "arbitrary")),
        name=f"attn_dil{dil}",
    )(qv, kv_, kv_, vv, vv, bias_mat)
    return o.reshape(b * s, ATT_WIDTH), lse.reshape(b * s, ATT_WIDTH)


def _softplus(x):
    return jnp.maximum(x, 0.0) + jnp.log(1.0 + jnp.exp(-jnp.abs(x)))


def _ssd_kernel(xbc_ref, dt_ref, z_ref, cw_ref, cb_ref, dtb_ref, alog_ref, dskip_ref, gn_ref, e_ref,
                y_ref, st_ref, ext_ref, state_ref):
    c = pl.program_id(1)
    L = SSD_CHUNK

    @pl.when(c == 0)
    def _():
        ext_ref[0:SUBLANES, :] = jnp.zeros((SUBLANES, CONV_DIM), F32)
        state_ref[...] = jnp.zeros_like(state_ref)

    ext_ref[SUBLANES:SUBLANES + L, :] = xbc_ref[...]
    acc = cb_ref[...] + ext_ref[SUBLANES:SUBLANES + L, :] * cw_ref[CONV_W - 1:CONV_W, :]
    for i in range(CONV_W - 1):
        off = SUBLANES - (CONV_W - 1) + i
        acc = acc + ext_ref[off:off + L, :] * cw_ref[i:i + 1, :]
    ext_ref[0:SUBLANES, :] = ext_ref[L:L + SUBLANES, :]
    xa = _silu(acc)

    dt = _softplus(dt_ref[...] + dtb_ref[...])
    da = dt * (-jnp.exp(alog_ref[...]))
    row = lax.broadcasted_iota(jnp.int32, (L, L), 0)
    col = lax.broadcasted_iota(jnp.int32, (L, L), 1)
    tri = row >= col
    a_cs = jnp.dot(tri.astype(F32), da, preferred_element_type=F32, precision=HIGHEST)
    a_cs_t = a_cs.T
    expand = e_ref[...]
    acs_full = jnp.dot(a_cs, expand, preferred_element_type=F32, precision=HIGHEST)
    dt_full = jnp.dot(dt, expand, preferred_element_type=F32, precision=HIGHEST)
    exp_acs = jnp.exp(acs_full)
    a_last = acs_full[L - 1:L, :]
    exp_last = exp_acs[L - 1:L, :]
    xs = xa[:, :SSM_WIDTH]
    xdt = xs * dt_full
    xw = xdt * jnp.exp(a_last - acs_full)

    for g in range(N_SSM_GROUPS):
        b0 = SSM_WIDTH + g * D_STATE
        c0 = SSM_WIDTH + N_SSM_GROUPS * D_STATE + g * D_STATE
        bg_t = xa[:, b0:b0 + D_STATE].T.astype(BF16)
        cg = xa[:, c0:c0 + D_STATE].astype(BF16)
        gram = jnp.dot(cg, bg_t, preferred_element_type=F32)
        for hh in range(HEADS_PER_GROUP):
            h = g * HEADS_PER_GROUP + hh
            hs = slice(h * SSM_HEAD_DIM, (h + 1) * SSM_HEAD_DIM)
            seg = jnp.where(tri, a_cs[:, h:h + 1] - a_cs_t[h:h + 1, :], NEG)
            scores = (gram * jnp.exp(seg)).astype(BF16)
            y_diag = jnp.dot(scores, xdt[:, hs].astype(BF16), preferred_element_type=F32)
            st = state_ref[h]
            y_off = jnp.dot(cg, st.astype(BF16), preferred_element_type=F32) * exp_acs[:, hs]
            y_ref[:, hs] = y_diag + y_off
            state_ref[h] = exp_last[:, hs] * st + jnp.dot(bg_t, xw[:, hs].astype(BF16),
                                                          preferred_element_type=F32)

    y = y_ref[...] + dskip_ref[...] * xs
    u = y * _silu(z_ref[...])
    gw = SSM_WIDTH // N_SSM_GROUPS
    parts = []
    for g in range(N_SSM_GROUPS):
        ug = u[:, g * gw:(g + 1) * gw]
        parts.append(ug * lax.rsqrt(jnp.mean(ug * ug, axis=-1, keepdims=True) + EPS))
    y_ref[...] = jnp.concatenate(parts, axis=-1) * gn_ref[...]

    @pl.when(c == pl.num_programs(1) - 1)
    def _():
        st_ref[0] = state_ref[...]


def _head_expand():
    e = np.zeros((LANES, SSM_WIDTH), np.float32)
    for h in range(N_SSM_HEADS):
        e[h, h * SSM_HEAD_DIM:(h + 1) * SSM_HEAD_DIM] = 1.0
    return jnp.asarray(e)


def _pad_lanes(v):
    return jnp.pad(v.astype(F32), (0, LANES - v.shape[0])).reshape(1, LANES)


def _ssd_prompt(xbc, dt_raw, z, conv_w, conv_b, dt_bias, a_log, d_skip, ssd_norm, batch):
    m = xbc.shape[0]
    nc = m // batch // SSD_CHUNK
    L = SSD_CHUNK
    rows = lambda b, c: (b * nc + c, 0)
    y, st = pl.pallas_call(
        _ssd_kernel,
        grid=(batch, nc),
        in_specs=[pl.BlockSpec((L, CONV_DIM), rows),
                  pl.BlockSpec((L, LANES), rows),
                  pl.BlockSpec((L, SSM_WIDTH), rows),
                  _const_spec((CONV_W, CONV_DIM)),
                  _const_spec((1, CONV_DIM)),
                  _const_spec((1, LANES)),
                  _const_spec((1, LANES)),
                  _const_spec((1, SSM_WIDTH)),
                  _const_spec((1, SSM_WIDTH)),
                  _const_spec((LANES, SSM_WIDTH))],
        out_specs=[pl.BlockSpec((L, SSM_WIDTH), rows),
                   pl.BlockSpec((1, N_SSM_HEADS, D_STATE, SSM_HEAD_DIM), lambda b, c: (b, 0, 0, 0))],
        out_shape=[jax.ShapeDtypeStruct((m, SSM_WIDTH), F32),
                   jax.ShapeDtypeStruct((batch, N_SSM_HEADS, D_STATE, SSM_HEAD_DIM), F32)],
        scratch_shapes=[pltpu.VMEM((SUBLANES + L, CONV_DIM), F32),
                        pltpu.VMEM((N_SSM_HEADS, D_STATE, SSM_HEAD_DIM), F32)],
        compiler_params=_cparams(("parallel", "arbitrary")),
        name="ssd_prompt",
    )(xbc, dt_raw, z, conv_w, conv_b.reshape(1, CONV_DIM), _pad_lanes(dt_bias), _pad_lanes(a_log),
      jnp.repeat(d_skip, SSM_HEAD_DIM).reshape(1, SSM_WIDTH), ssd_norm.reshape(1, SSM_WIDTH),
      _head_expand())
    return y, jnp.swapaxes(st, -1, -2)


def _router(xn_bf, wr_ref, br_ref):
    return jnp.dot(xn_bf, wr_ref[...], preferred_element_type=F32) + br_ref[...]


def _outproj_kernel(o0, l0, o1, l1, o2, l2, ssm_ref, x_ref, w_ref, g_ref, wr_ref, br_ref,
                    x1_ref, xn_ref, lg_ref):
    la, lb, lc = l0[...], l1[...], l2[...]
    mx = jnp.maximum(jnp.maximum(la, lb), lc)
    wa, wb, wc = jnp.exp(la - mx), jnp.exp(lb - mx), jnp.exp(lc - mx)
    att = (wa * o0[...] + wb * o1[...] + wc * o2[...]) * (1.0 / (wa + wb + wc))
    y = jnp.dot(att.astype(BF16), w_ref[:ATT_WIDTH, :], preferred_element_type=F32)
    y = y + jnp.dot(ssm_ref[...].astype(BF16), w_ref[ATT_WIDTH:, :], preferred_element_type=F32)
    x1 = x_ref[...] + y
    x1_ref[...] = x1
    xn = _rms(x1, g_ref[...]).astype(BF16)
    xn_ref[...] = xn
    lg_ref[...] = _router(xn, wr_ref, br_ref)


def _outproj_prompt(att_parts, ssm, x2d, w_bf, g, wr, br):
    m, d = x2d.shape
    mix = w_bf.shape[0]
    row = lambda w: pl.BlockSpec((TM_OUT, w), lambda i: (i, 0))
    return pl.pallas_call(
        _outproj_kernel,
        grid=(m // TM_OUT,),
        in_specs=[row(ATT_WIDTH)] * 6 + [row(SSM_WIDTH), row(d),
                                         _const_spec((mix, d)), _const_spec((1, d)),
                                         _const_spec((d, LANES)), _const_spec((1, LANES))],
        out_specs=[row(d), row(d), row(LANES)],
        out_shape=[jax.ShapeDtypeStruct((m, d), F32), jax.ShapeDtypeStruct((m, d), BF16),
                   jax.ShapeDtypeStruct((m, LANES), F32)],
        compiler_params=_cparams(("parallel",)),
        name="outproj_prompt",
    )(*att_parts, ssm, x2d, w_bf, g.reshape(1, d), wr, br)


def _attn_sample_kernel(q_ref, kn_ref, vn_ref, k0_ref, k1_ref, k2_ref, v0_ref, v1_ref, v2_ref,
                        bias_ref, bias0_ref, o_ref):
    q = q_ref[0]
    head_grp = lax.broadcasted_iota(jnp.int32, (N_ATT_HEADS, 1), 0) // KV_REP
    kn = kn_ref[0].astype(BF16).astype(F32)
    vn = vn_ref[0].astype(BF16).astype(F32)
    s_self = jnp.sum(q.astype(F32) * kn, axis=-1, keepdims=True)
    scores, selfs, lses = [], [], []
    for g, kc_ref in enumerate((k0_ref, k1_ref, k2_ref)):
        kk = kc_ref[0].astype(BF16)
        s = jnp.zeros((N_ATT_HEADS, WIN), F32)
        for kvh in range(N_KV_HEADS):
            cs = slice(kvh * HEAD_DIM, (kvh + 1) * HEAD_DIM)
            sk = lax.dot_general(q, kk[:, cs], (((1,), (1,)), ((), ())), preferred_element_type=F32)
            s = jnp.where(head_grp == kvh, sk, s)
        s = s + bias_ref[g]
        s0 = s_self + bias0_ref[g]
        mx = jnp.maximum(jnp.max(s, axis=-1, keepdims=True), s0)
        lse = mx + jnp.log(jnp.sum(jnp.exp(s - mx), axis=-1, keepdims=True) + jnp.exp(s0 - mx))
        scores.append(s); selfs.append(s0); lses.append(lse)
    top = functools.reduce(jnp.maximum, lses)
    es = [jnp.exp(l - top) for l in lses]
    tot = functools.reduce(jnp.add, es)
    o = jnp.zeros((N_ATT_HEADS, HEAD_DIM), F32)
    for s, s0, lse, e, vc_ref in zip(scores, selfs, lses, es, (v0_ref, v1_ref, v2_ref)):
        wgt = e / tot
        p = (jnp.exp(s - lse) * wgt).astype(BF16)
        p0 = (jnp.exp(s0 - lse) * wgt).astype(BF16).astype(F32)
        vv = vc_ref[0].astype(BF16)
        o = o + p0 * vn
        for kvh in range(N_KV_HEADS):
            cs = slice(kvh * HEAD_DIM, (kvh + 1) * HEAD_DIM)
            ok = jnp.dot(p, vv[:, cs], preferred_element_type=F32)
            o = o + jnp.where(head_grp == kvh, ok, 0.0)
    o_ref[0] = o


def _attn_sample(q, k_new, v_new, k_cache, v_cache, bias_s, bias0_s):
    n, w = k_cache.shape[0], k_cache.shape[1]
    assert w % (max(DILATIONS) * WIN) == 0
    tok = lambda b: (b, 0, 0)
    head = pl.BlockSpec((1, N_ATT_HEADS, HEAD_DIM), tok)
    taps = [pl.BlockSpec((1, WIN, KV_WIDTH), functools.partial(lambda nb, b: (b, nb - 1, 0), w // dil // WIN))
            for dil in DILATIONS]
    views = lambda c: [c.reshape(n, w // dil, dil * KV_WIDTH) for dil in DILATIONS]
    return pl.pallas_call(
        _attn_sample_kernel,
        grid=(n,),
        in_specs=[head, head, head] + taps + taps +
                 [_const_spec((len(DILATIONS), N_ATT_HEADS, WIN)),
                  _const_spec((len(DILATIONS), N_ATT_HEADS, 1))],
        out_specs=head,
        out_shape=jax.ShapeDtypeStruct((n, N_ATT_HEADS, HEAD_DIM), F32),
        compiler_params=_cparams(("parallel",)),
        name="attn_sample",
    )(q, jnp.repeat(k_new, KV_REP, axis=1), jnp.repeat(v_new, KV_REP, axis=1), *views(k_cache),
      *views(v_cache), bias_s, bias0_s)


def _conv_sample_kernel(xbc_ref, b0_ref, b1_ref, b2_ref, cw_ref, cb_ref, dt_ref, dtb_ref, alog_ref,
                        e_ref, xa_ref, xdt_ref, decay_ref):
    acc = cb_ref[...] + xbc_ref[...] * cw_ref[CONV_W - 1:CONV_W, :]
    for i, buf in enumerate((b0_ref, b1_ref, b2_ref)):
        acc = acc + buf[...] * cw_ref[i:i + 1, :]
    xa = _silu(acc)
    xa_ref[...] = xa
    dt = _softplus(dt_ref[...] + dtb_ref[...])
    decay_ref[...] = jnp.exp(dt * (-jnp.exp(alog_ref[...])))
    dt_full = jnp.dot(dt, e_ref[...], preferred_element_type=F32, precision=HIGHEST)
    xdt_ref[...] = xa[:, :SSM_WIDTH] * dt_full


def _conv_sample(xbc, conv_buf, conv_w, conv_b, dt_raw, dt_bias, a_log):
    n = xbc.shape[0]
    args = (xbc, conv_buf[:, 0], conv_buf[:, 1], conv_buf[:, 2], conv_w, conv_b.reshape(1, CONV_DIM),
            dt_raw, _pad_lanes(dt_bias), _pad_lanes(a_log),
            _head_expand())
    return pl.pallas_call(
        _conv_sample_kernel,
        grid=(1,),
        in_specs=[_const_spec(a.shape) for a in args],
        out_specs=[_const_spec((n, CONV_DIM)), _const_spec((n, SSM_WIDTH)), _const_spec((n, LANES))],
        out_shape=[jax.ShapeDtypeStruct((n, CONV_DIM), F32), jax.ShapeDtypeStruct((n, SSM_WIDTH), F32),
                   jax.ShapeDtypeStruct((n, LANES), F32)],
        compiler_params=_cparams(("arbitrary",)),
        name="conv_sample",
    )(*args)


def _ssm_sample_kernel(xdt_ref, decay_ref, b_ref, c_ref, h0_ref, hn_ref, y_ref):
    for g in range(N_SSM_GROUPS):
        hs = slice(g * HEADS_PER_GROUP, (g + 1) * HEADS_PER_GROUP)
        hn = decay_ref[0, hs] * h0_ref[0, hs] + xdt_ref[0, hs] * b_ref[0, g]
        hn_ref[0, hs] = hn
        c_row = c_ref[0, g].astype(BF16).astype(F32)
        y_ref[0, hs] = jnp.sum(hn.astype(BF16).astype(F32) * c_row, axis=-1, keepdims=True)


def _ssm_sample(xdt, decay, bmat, cmat, h0):
    n = xdt.shape[0]
    p = SSM_HEAD_DIM
    tok4 = lambda b: (b, 0, 0, 0)
    hn, y = pl.pallas_call(
        _ssm_sample_kernel,
        grid=(n,),
        in_specs=[pl.BlockSpec((1, N_SSM_HEADS, p, 1), tok4),
                  pl.BlockSpec((1, N_SSM_HEADS, 1, 1), tok4),
                  pl.BlockSpec((1, N_SSM_GROUPS, 1, D_STATE), tok4),
                  pl.BlockSpec((1, N_SSM_GROUPS, 1, D_STATE), tok4),
                  pl.BlockSpec((1, N_SSM_HEADS, p, D_STATE), tok4)],
        out_specs=[pl.BlockSpec((1, N_SSM_HEADS, p, D_STATE), tok4),
                   pl.BlockSpec((1, N_SSM_HEADS, p, 1), tok4)],
        out_shape=[jax.ShapeDtypeStruct((n, N_SSM_HEADS, p, D_STATE), F32),
                   jax.ShapeDtypeStruct((n, N_SSM_HEADS, p, 1), F32)],
        compiler_params=_cparams(("parallel",)),
        name="ssm_sample",
    )(xdt.reshape(n, N_SSM_HEADS, p, 1), decay[:, :N_SSM_HEADS].reshape(n, N_SSM_HEADS, 1, 1),
      bmat.reshape(n, N_SSM_GROUPS, 1, D_STATE), cmat.reshape(n, N_SSM_GROUPS, 1, D_STATE), h0)
    return hn, y.reshape(n, SSM_WIDTH)


def _outproj_sample_kernel(att_ref, y_ref, xs_ref, z_ref, dskip_ref, gn_ref, x_ref, w_ref, g_ref,
                           wr_ref, br_ref, x1_ref, xn_ref, lg_ref):
    y = y_ref[...] + dskip_ref[...] * xs_ref[...]
    u = y * _silu(z_ref[...])
    gw = SSM_WIDTH // N_SSM_GROUPS
    parts = []
    for g in range(N_SSM_GROUPS):
        ug = u[:, g * gw:(g + 1) * gw]
        parts.append(ug * lax.rsqrt(jnp.mean(ug * ug, axis=-1, keepdims=True) + EPS)
                     * gn_ref[:, g * gw:(g + 1) * gw])
    mix = jnp.concatenate([att_ref[...]] + parts, axis=-1).astype(BF16)
    x1 = x_ref[...] + jnp.dot(mix, w_ref[...], preferred_element_type=F32)
    x1_ref[...] = x1
    xn = _rms(x1, g_ref[...]).astype(BF16)
    xn_ref[...] = xn
    lg_ref[...] = _router(xn, wr_ref, br_ref)


def _outproj_sample(att, y, xs, z, d_skip, ssd_norm, x2d, w_bf, g, wr, br):
    n, d = x2d.shape
    args = (att, y, xs, z, jnp.repeat(d_skip, SSM_HEAD_DIM).reshape(1, SSM_WIDTH),
            ssd_norm.reshape(1, SSM_WIDTH), x2d, w_bf, g.reshape(1, d), wr, br)
    return pl.pallas_call(
        _outproj_sample_kernel,
        grid=(1,),
        in_specs=[_const_spec(a.shape) for a in args],
        out_specs=[_const_spec((n, d)), _const_spec((n, d)), _const_spec((n, LANES))],
        out_shape=[jax.ShapeDtypeStruct((n, d), F32), jax.ShapeDtypeStruct((n, d), BF16),
                   jax.ShapeDtypeStruct((n, LANES), F32)],
        compiler_params=_cparams(("arbitrary",)),
        name="outproj_sample",
    )(*args)


TT = 256
LROWS = TT * TOP_K + N_EXPERTS * SUBLANES
CHUNK_SIZES = (256, 128, 64, 32, 16, 8)
SUB = 256
RG = 6 * SUB
TF = 256


def _chunk_loop(cnt_ref, loff_ref, dest_ref, tile, fn):
    def per_expert(e, carry):
        idx = tile * N_EXPERTS + e
        n, off, dst = cnt_ref[idx], loff_ref[idx], dest_ref[idx]
        for size in CHUNK_SIZES:
            take = (n & size) != 0

            @pl.when(take)
            def _(off=off, dst=dst, size=size):
                fn(pl.multiple_of(off, SUBLANES), pl.multiple_of(dst, SUBLANES), size)

            step = jnp.where(take, size, 0)
            off, dst = off + step, dst + step
        return carry

    lax.fori_loop(0, N_EXPERTS, per_expert, 0)


def _dispatch_kernel(cnt_ref, loff_ref, dest_ref, xp_ref, xs_ref, lpos_ref, gate_ref, out_hbm, buf, sem):
    t = pl.program_id(0)
    d = xp_ref.shape[1]
    x = jnp.where(t == pl.num_programs(0) - 1, xs_ref[...], xp_ref[...])
    rows = lax.broadcasted_iota(jnp.int32, (LROWS, TT), 0)
    onehot = jnp.zeros((LROWS, TT), F32)
    wcol = jnp.zeros((LROWS, 1), F32)
    for k in range(TOP_K):
        hit = jnp.where(rows == lpos_ref[0, k:k + 1, :], 1.0, 0.0)
        onehot = onehot + hit
        wcol = wcol + jnp.sum(hit * gate_ref[0, k:k + 1, :], axis=-1, keepdims=True)
    buf[:, 0:d] = jnp.dot(onehot.astype(BF16), x, preferred_element_type=F32)
    buf[:, d:d + LANES] = jnp.broadcast_to(wcol, (LROWS, LANES))

    def copy(off, dst, size):
        return pltpu.make_async_copy(buf.at[pl.ds(off, size)], out_hbm.at[pl.ds(dst, size)], sem)

    _chunk_loop(cnt_ref, loff_ref, dest_ref, t, lambda o, g, s: copy(o, g, s).start())
    _chunk_loop(cnt_ref, loff_ref, dest_ref, t, lambda o, g, s: copy(o, g, s).wait())


def _dispatch(tabs, xn_p, xn_s, lpos_t, gate_t, n_rows):
    nt = lpos_t.shape[0]
    d = xn_p.shape[1]
    last_p = xn_p.shape[0] // TT - 1
    grid_spec = pltpu.PrefetchScalarGridSpec(
        num_scalar_prefetch=3,
        grid=(nt,),
        in_specs=[pl.BlockSpec((TT, d), lambda t, *_: (jnp.minimum(t, last_p), 0)),
                  pl.BlockSpec((TT, d), lambda t, *_: (0, 0)),
                  pl.BlockSpec((1, TOP_K, TT), lambda t, *_: (t, 0, 0)),
                  pl.BlockSpec((1, TOP_K, TT), lambda t, *_: (t, 0, 0))],
        out_specs=pl.BlockSpec(memory_space=pl.ANY),
        scratch_shapes=[pltpu.VMEM((LROWS, d + LANES), F32), pltpu.SemaphoreType.DMA(())],
    )
    return pl.pallas_call(
        _dispatch_kernel,
        grid_spec=grid_spec,
        out_shape=jax.ShapeDtypeStruct((n_rows, d + LANES), F32),
        compiler_params=_cparams(("arbitrary",)),
        name="moe_dispatch",
    )(*tabs, xn_p, xn_s, lpos_t, gate_t)


def _ffn_kernel(ge_ref, gs_ref, gn_ref, ng_ref, xs_hbm, wgu_hbm, wd_hbm, bgu_ref, bdn_ref, out_hbm,
                xbuf, acc, wg_st, wu_st, wd_st, wg_bf, wu_bf, wd_bf, sem_w, sem_x, sem_o):
    d = acc.shape[1]
    d_ff = wd_hbm.shape[1]
    nf = d_ff // TF
    total = ng_ref[0] * nf

    def w_copies(s, slot):
        g = s // nf
        f = s - g * nf
        e = ge_ref[g]
        c0 = pl.multiple_of(f * TF, TF)
        return (pltpu.make_async_copy(wgu_hbm.at[e, :, pl.ds(c0, TF)], wg_st.at[slot], sem_w.at[slot, 0]),
                pltpu.make_async_copy(wgu_hbm.at[e, :, pl.ds(d_ff + c0, TF)], wu_st.at[slot], sem_w.at[slot, 1]),
                pltpu.make_async_copy(wd_hbm.at[e, pl.ds(c0, TF), :], wd_st.at[slot], sem_w.at[slot, 2]))

    def x_copy(start, j):
        r0 = pl.multiple_of(j * SUB, SUB)
        return pltpu.make_async_copy(xs_hbm.at[pl.ds(pl.multiple_of(start + r0, SUBLANES), SUB)],
                                     xbuf.at[pl.ds(r0, SUB)], sem_x)

    def o_copy(start, j):
        r0 = pl.multiple_of(j * SUB, SUB)
        return pltpu.make_async_copy(acc.at[pl.ds(r0, SUB)],
                                     out_hbm.at[pl.ds(pl.multiple_of(start + r0, SUBLANES), SUB)], sem_o)

    def loop(n, fn):
        lax.fori_loop(0, n, lambda j, c: (fn(j), c)[1], 0)

    @pl.when(total > 0)
    def _():
        for c in w_copies(0, 0):
            c.start()

    def item(s, carry):
        slot = s % 2
        g = s // nf
        f = s - g * nf
        e = ge_ref[g]
        nsub = gn_ref[g]
        start = gs_ref[g]

        @pl.when(s + 1 < total)
        def _():
            for c in w_copies(s + 1, 1 - slot):
                c.start()

        @pl.when(f == 0)
        def _():
            loop(nsub, lambda j: x_copy(start, j).start())
            loop(nsub, lambda j: x_copy(start, j).wait())

        for c in w_copies(s, slot):
            c.wait()
        wg_bf[...] = wg_st[slot].astype(BF16)
        wu_bf[...] = wu_st[slot].astype(BF16)
        wd_bf[...] = wd_st[slot].astype(BF16)
        bg = bgu_ref[pl.ds(e * 2 * nf + f, 1), :]
        bu = bgu_ref[pl.ds(e * 2 * nf + nf + f, 1), :]

        def sub_block(j, first):
            rs = pl.ds(pl.multiple_of(j * SUB, SUB), SUB)
            x = xbuf[rs, 0:d].astype(BF16)
            hg = jnp.dot(x, wg_bf[...], preferred_element_type=F32) + bg
            hu = jnp.dot(x, wu_bf[...], preferred_element_type=F32) + bu
            gg = jnp.minimum(hg, SWIGLU_LIMIT)
            uu = jnp.clip(hu, -SWIGLU_LIMIT, SWIGLU_LIMIT)
            act = gg * (1.0 / (1.0 + jnp.exp(-SWIGLU_ALPHA * gg))) * (uu + 1.0)
            part = jnp.dot(act.astype(BF16), wd_bf[...], preferred_element_type=F32)
            if first:
                acc[rs, :] = part + bdn_ref[pl.ds(e, 1), :]
            else:
                acc[rs, :] += part

        @pl.when(f == 0)
        def _():
            loop(nsub, lambda j: sub_block(j, True))

        @pl.when(f > 0)
        def _():
            loop(nsub, lambda j: sub_block(j, False))

        @pl.when(f == nf - 1)
        def _():
            def scale_and_store(j):
                rs = pl.ds(pl.multiple_of(j * SUB, SUB), SUB)
                acc[rs, :] = acc[rs, :] * xbuf[rs, d:d + 1]
                o_copy(start, j).start()

            loop(nsub, scale_and_store)
            loop(nsub, lambda j: o_copy(start, j).wait())

        return carry

    lax.fori_loop(0, total, item, 0)


def _moe_ffn(groups, x_sorted, w_gate_up, b_gu, w_down, b_dn):
    n_rows = x_sorted.shape[0]
    _, d_ff, d = w_down.shape
    nf = d_ff // TF
    bgu2 = b_gu.reshape(N_EXPERTS * 2 * nf, TF)
    grid_spec = pltpu.PrefetchScalarGridSpec(
        num_scalar_prefetch=4,
        grid=(1,),
        in_specs=[pl.BlockSpec(memory_space=pl.ANY), pl.BlockSpec(memory_space=pl.ANY),
                  pl.BlockSpec(memory_space=pl.ANY),
                  pl.BlockSpec(bgu2.shape, lambda i, *_: (0, 0), pipeline_mode=pl.Buffered(1)),
                  pl.BlockSpec(b_dn.shape, lambda i, *_: (0, 0), pipeline_mode=pl.Buffered(1))],
        out_specs=pl.BlockSpec(memory_space=pl.ANY),
        scratch_shapes=[pltpu.VMEM((RG, d + LANES), F32), pltpu.VMEM((RG, d), F32),
                        pltpu.VMEM((2, d, TF), F32), pltpu.VMEM((2, d, TF), F32), pltpu.VMEM((2, TF, d), F32),
                        pltpu.VMEM((d, TF), BF16), pltpu.VMEM((d, TF), BF16), pltpu.VMEM((TF, d), BF16),
                        pltpu.SemaphoreType.DMA((2, 3)), pltpu.SemaphoreType.DMA(()),
                        pltpu.SemaphoreType.DMA(())],
    )
    return pl.pallas_call(
        _ffn_kernel,
        grid_spec=grid_spec,
        out_shape=jax.ShapeDtypeStruct((n_rows, d), F32),
        compiler_params=_cparams(("arbitrary",)),
        name="moe_ffn",
    )(*groups, x_sorted, w_gate_up, w_down, bgu2, b_dn)


def _route(logits, m_pad):
    m = logits.shape[0]
    nt = m_pad // TT
    top_v, top_i = lax.top_k(logits, TOP_K)
    gate = jnp.pad(jax.nn.softmax(top_v, axis=-1), ((0, m_pad - m), (0, 0)))
    top_i = jnp.pad(top_i.astype(jnp.int32), ((0, m_pad - m), (0, 0)), constant_values=-1)
    picks = (top_i[:, :, None] == jnp.arange(N_EXPERTS, dtype=jnp.int32)).astype(jnp.int32).sum(axis=1)
    tiles = picks.reshape(nt, TT, N_EXPERTS)
    cnt = (tiles.sum(axis=1) + SUBLANES - 1) // SUBLANES * SUBLANES
    loff = jnp.cumsum(cnt, axis=1) - cnt
    seg = cnt.sum(axis=0)
    seg_start = jnp.cumsum(seg) - seg
    dest = seg_start[None, :] + jnp.cumsum(cnt, axis=0) - cnt
    rank = jnp.cumsum(tiles, axis=1) - tiles
    lpos_all = (loff[:, None, :] + rank).reshape(m_pad, N_EXPERTS)
    lpos = jnp.take_along_axis(lpos_all, jnp.maximum(top_i, 0), axis=1)
    lpos = jnp.where(top_i >= 0, lpos, -1).astype(jnp.int32)
    n_rows_bound = nt * LROWS
    n_grp_max = n_rows_bound // RG + N_EXPERTS
    grp = (seg + RG - 1) // RG
    grp_end = jnp.cumsum(grp)
    gi = jnp.arange(n_grp_max, dtype=jnp.int32)
    g_exp = jnp.minimum(jnp.searchsorted(grp_end, gi, side='right'), N_EXPERTS - 1).astype(jnp.int32)
    within = gi - (grp_end[g_exp] - grp[g_exp])
    g_start = (seg_start[g_exp] + within * RG).astype(jnp.int32)
    g_rows = jnp.clip(seg[g_exp] - within * RG, 0, RG)
    g_nsub = jnp.where(gi < grp_end[-1], (g_rows + SUB - 1) // SUB, 0).astype(jnp.int32)
    groups = (g_exp, g_start, g_nsub, grp_end[-1:].astype(jnp.int32))
    tabs = tuple(a.reshape(-1).astype(jnp.int32) for a in (cnt, loff, dest))
    return tabs, lpos, gate, groups, n_rows_bound + SUB


def _combine_kernel(cnt_ref, loff_ref, dest_ref, src_hbm, lpos_ref, x1p_ref, x1s_ref, g_ref,
                    yp_ref, ys_ref, buf, sem):
    t = pl.program_id(0)
    last = pl.num_programs(0) - 1

    @pl.when(t == 0)
    def _():
        buf[...] = jnp.zeros_like(buf)

    def copy(off, src, size):
        return pltpu.make_async_copy(src_hbm.at[pl.ds(src, size)], buf.at[pl.ds(off, size)], sem)

    _chunk_loop(cnt_ref, loff_ref, dest_ref, t, lambda o, g, s: copy(o, g, s).start())
    _chunk_loop(cnt_ref, loff_ref, dest_ref, t, lambda o, g, s: copy(o, g, s).wait())
    cols = lax.broadcasted_iota(jnp.int32, (TT, LROWS), 1)
    sel = jnp.zeros((TT, LROWS), F32)
    for k in range(TOP_K):
        sel = sel + jnp.where(cols == lpos_ref[:, k:k + 1], 1.0, 0.0)
    sel = sel.astype(BF16)
    rows = buf[...]
    hi = rows.astype(BF16)
    lo = (rows - hi.astype(F32)).astype(BF16)
    f = jnp.dot(sel, hi, preferred_element_type=F32) + jnp.dot(sel, lo, preferred_element_type=F32)

    @pl.when(t < last)
    def _():
        yp_ref[...] = _rms(x1p_ref[...] + f, g_ref[...])

    @pl.when(t == last)
    def _():
        ys_ref[...] = _rms(x1s_ref[...] + f, g_ref[...])


def _combine(tabs, out_sorted, lpos, x1p, x1s, g):
    d = x1p.shape[1]
    nt = lpos.shape[0] // TT
    last_p = x1p.shape[0] // TT - 1
    prompt = pl.BlockSpec((TT, d), lambda t, *_: (jnp.minimum(t, last_p), 0))
    sample = pl.BlockSpec((TT, d), lambda t, *_: (0, 0))
    grid_spec = pltpu.PrefetchScalarGridSpec(
        num_scalar_prefetch=3,
        grid=(nt,),
        in_specs=[pl.BlockSpec(memory_space=pl.ANY),
                  pl.BlockSpec((TT, TOP_K), lambda t, *_: (t, 0)),
                  prompt, sample,
                  pl.BlockSpec((1, d), lambda t, *_: (0, 0))],
        out_specs=[prompt, sample],
        scratch_shapes=[pltpu.VMEM((LROWS, d), F32), pltpu.SemaphoreType.DMA(())],
    )
    return pl.pallas_call(
        _combine_kernel,
        grid_spec=grid_spec,
        out_shape=[jax.ShapeDtypeStruct(x1p.shape, F32), jax.ShapeDtypeStruct((TT, d), F32)],
        compiler_params=_cparams(("arbitrary",)),
        name="moe_combine",
    )(*tabs, out_sorted, lpos, x1p, x1s, g.reshape(1, d))


def _t5_bucket(dist):
    max_exact = N_BUCKETS // 2
    dd = dist.astype(F32)
    large = max_exact + (jnp.log(jnp.maximum(dd, 1.0) / max_exact)
                         / math.log(BUCKET_MAX_DIST / max_exact) * (N_BUCKETS - max_exact)).astype(jnp.int32)
    large = jnp.minimum(large, N_BUCKETS - 1)
    return jnp.where(dist < max_exact, dist, large)


def _bias_tables(rel_bias):
    dist = jnp.asarray(np.arange(N_TAPS)[None, :] * np.array(DILATIONS)[:, None], jnp.int32)
    bias = jnp.transpose(rel_bias[_t5_bucket(dist)], (2, 0, 1)).astype(F32)
    tap = WIN + np.arange(WIN)[:, None] - np.arange(2 * WIN)[None, :]
    ok = (tap >= 0) & (tap <= WIN)
    band = jnp.where(ok[None, None], jnp.transpose(bias, (1, 0, 2))[:, :, np.clip(tap, 0, WIN)], NEG)
    by_branch = jnp.transpose(bias, (1, 0, 2))
    return band, by_branch[:, :, WIN - np.arange(WIN)], by_branch[:, :, 0:1]


def kernel(x_prompt, x_sample, cache_k_win, cache_v_win, state_conv, state_ssm, rel_bias, attn_norm, w_in, conv_w, conv_b, dt_bias, a_log, d_skip, ssd_norm, w_out, ffn_norm, w_router, b_router, w_gate_up, b_gate_up, w_down, b_down, final_norm):
    bp, tp, d = x_prompt.shape
    bs, ts, _ = x_sample.shape
    depth = w_in.shape[0]
    assert depth == 1 and ts == 1 and tp % (max(DILATIONS) * WIN) == 0
    keep = min(max(DILATIONS) * WIN, tp)
    band, samp, samp0 = _bias_tables(rel_bias)
    l = 0

    xp = x_prompt.reshape(bp * tp, d)
    xs = x_sample.reshape(bs * ts, d)
    w_in_bf = jnp.pad(w_in[l], ((0, 0), (0, IN_PROJ_PAD - IN_PROJ))).astype(BF16)
    w_out_bf = w_out[l].astype(BF16)
    wr = jnp.pad(w_router[l], ((0, 0), (0, LANES - N_EXPERTS))).astype(BF16)
    br = jnp.pad(b_router[l], (0, LANES - N_EXPERTS), constant_values=NEG).reshape(1, LANES)

    q, k, v, z, xbc, dt_raw = _inproj(xp, attn_norm[l], w_in_bf, TM_PROJ)
    q3 = q.reshape(bp, tp, ATT_WIDTH)
    k3 = k.reshape(bp, tp, KV_WIDTH)
    v3 = v.reshape(bp, tp, KV_WIDTH)
    att_parts = []
    for gi, dil in enumerate(DILATIONS):
        att_parts.extend(_attn_branch(q3, k3, v3, band[gi], dil))
    ssm, st_p = _ssd_prompt(xbc, dt_raw, z, conv_w[l], conv_b[l], dt_bias[l], a_log[l], d_skip[l],
                            ssd_norm[l], bp)
    x1p, xnp_, lgp = _outproj_prompt(att_parts, ssm, xp, w_out_bf, ffn_norm[l], wr, br)
    k_win_p = k3[:, tp - keep:].reshape(1, bp, keep, N_KV_HEADS, HEAD_DIM)
    v_win_p = v3[:, tp - keep:].reshape(1, bp, keep, N_KV_HEADS, HEAD_DIM)
    conv_p = xbc.reshape(bp, tp, CONV_DIM)[:, tp - (CONV_W - 1):][None]

    q_s, k_s, v_s, z_s, xbc_s, dt_s = _inproj(xs, attn_norm[l], w_in_bf, bs * ts)
    q_s = q_s.reshape(bs, N_ATT_HEADS, HEAD_DIM)
    k_s = k_s.reshape(bs, N_KV_HEADS, HEAD_DIM)
    v_s = v_s.reshape(bs, N_KV_HEADS, HEAD_DIM)
    wbuf = cache_k_win.shape[2]
    att_s = _attn_sample(q_s, k_s, v_s, cache_k_win[l].reshape(bs, wbuf, KV_WIDTH),
                         cache_v_win[l].reshape(bs, wbuf, KV_WIDTH), samp, samp0)
    xa_s, xdt_s, decay_s = _conv_sample(xbc_s, state_conv[l], conv_w[l], conv_b[l], dt_s, dt_bias[l],
                                        a_log[l])
    nbc = N_SSM_GROUPS * D_STATE
    h_s, y_s = _ssm_sample(xdt_s, decay_s, xa_s[:, SSM_WIDTH:SSM_WIDTH + nbc], xa_s[:, SSM_WIDTH + nbc:],
                           state_ssm[l])
    x1s, xns, lgs = _outproj_sample(att_s.reshape(bs, ATT_WIDTH), y_s, xa_s[:, :SSM_WIDTH], z_s,
                                    d_skip[l], ssd_norm[l], xs, w_out_bf, ffn_norm[l], wr, br)
    conv_s = jnp.concatenate([state_conv[l][:, 1:], xbc_s[:, None]], axis=1)[None]

    n_s = bs * ts
    assert (bp * tp) % TT == 0 and n_s <= TT
    m_pad = bp * tp + TT
    logits = jnp.concatenate([lgp[:, :N_EXPERTS], lgs[:, :N_EXPERTS]], axis=0)
    tabs, lpos, gate, groups, n_rows = _route(logits, m_pad)
    by_tile = lambda a: jnp.transpose(a.reshape(m_pad // TT, TT, TOP_K), (0, 2, 1))
    pad_rows = lambda a: jnp.pad(a, ((0, TT - n_s), (0, 0)))
    x_sorted = _dispatch(tabs, xnp_, pad_rows(xns), by_tile(lpos), by_tile(gate), n_rows)
    out_sorted = _moe_ffn(groups, x_sorted, w_gate_up[l], b_gate_up[l], w_down[l], b_down[l])
    y_p, y_s_out = _combine(tabs, out_sorted, lpos, x1p, pad_rows(x1s), final_norm)
    y_s_out = y_s_out[:n_s]

    return (y_p.reshape(bp, tp, d), y_s_out.reshape(bs, ts, d), k_win_p, v_win_p, conv_p, st_p[None],
            k_s.reshape(1, bs, ts, N_KV_HEADS, HEAD_DIM), v_s.reshape(1, bs, ts, N_KV_HEADS, HEAD_DIM),
            conv_s, h_s[None])
```

```python
import functools
import math

import jax
import jax.numpy as jnp
import numpy as np
from jax import lax
from jax.experimental import pallas as pl
from jax.experimental.pallas import tpu as pltpu

F32 = jnp.float32
BF16 = jnp.bfloat16
HIGHEST = lax.Precision.HIGHEST

LANES = 128
SUBLANES = 8
VMEM_LIMIT = 56 * 1024 * 1024

HEAD_DIM = 64
N_ATT_HEADS = 16
N_KV_HEADS = 4
KV_REP = N_ATT_HEADS // N_KV_HEADS
ATT_WIDTH = N_ATT_HEADS * HEAD_DIM
KV_WIDTH = N_KV_HEADS * HEAD_DIM
DILATIONS = (1, 4, 16)
N_TAPS = 129
WIN = N_TAPS - 1
ATT_SCALE = HEAD_DIM ** -0.5
N_BUCKETS = 32
BUCKET_MAX_DIST = 2048
SSM_HEAD_DIM = 64
N_SSM_HEADS = 16
SSM_WIDTH = N_SSM_HEADS * SSM_HEAD_DIM
N_SSM_GROUPS = 2
HEADS_PER_GROUP = N_SSM_HEADS // N_SSM_GROUPS
D_STATE = 128
CONV_W = 4
CONV_DIM = SSM_WIDTH + 2 * N_SSM_GROUPS * D_STATE
SSD_CHUNK = 128
N_EXPERTS = 32
TOP_K = 4
SWIGLU_LIMIT = 7.0
SWIGLU_ALPHA = 1.702
EPS = 1e-5
NEG = -1e30

Q0, K0, V0, Z0, X0, DT0 = 0, 1024, 1280, 1536, 2560, 4096
IN_PROJ = DT0 + N_SSM_HEADS
IN_PROJ_PAD = DT0 + LANES

TM_PROJ = 512
TM_OUT = 256
TM_MOE = 512
TF_MOE = 512
TM_FIN = 256


def _cparams(sem):
    return pltpu.CompilerParams(dimension_semantics=sem, vmem_limit_bytes=VMEM_LIMIT)


def _const_spec(shape):
    nd = len(shape)
    return pl.BlockSpec(shape, lambda *_: (0,) * nd, pipeline_mode=pl.Buffered(1))


def _rms(x, g):
    ms = jnp.mean(x * x, axis=-1, keepdims=True)
    return x * lax.rsqrt(ms + EPS) * g


def _silu(x):
    return x * (1.0 / (1.0 + jnp.exp(-x)))


def _inproj_kernel(x_ref, g_ref, w_ref, q_ref, k_ref, v_ref, z_ref, xbc_ref, dt_ref):
    xn = _rms(x_ref[...], g_ref[...]).astype(BF16)

    def mm(lo, hi):
        return jnp.dot(xn, w_ref[:, lo:hi], preferred_element_type=F32)

    q_ref[...] = (mm(Q0, K0) * ATT_SCALE).astype(BF16)
    k_ref[...] = mm(K0, V0)
    v_ref[...] = mm(V0, Z0)
    z_ref[...] = mm(Z0, X0)
    xbc_ref[...] = mm(X0, DT0)
    dt_ref[...] = mm(DT0, IN_PROJ_PAD)


def _inproj(x2d, g, w_bf, tm):
    m, d = x2d.shape
    widths = (ATT_WIDTH, KV_WIDTH, KV_WIDTH, SSM_WIDTH, CONV_DIM, LANES)
    dtypes = (BF16, F32, F32, F32, F32, F32)
    return pl.pallas_call(
        _inproj_kernel,
        grid=(m // tm,),
        in_specs=[pl.BlockSpec((tm, d), lambda i: (i, 0)),
                  _const_spec((1, d)),
                  _const_spec((d, IN_PROJ_PAD))],
        out_specs=[pl.BlockSpec((tm, w), lambda i: (i, 0)) for w in widths],
        out_shape=[jax.ShapeDtypeStruct((m, w), t) for w, t in zip(widths, dtypes)],
        compiler_params=_cparams(("parallel",)),
        name=f"inproj_{tm}",
    )(x2d, g.reshape(1, d), w_bf)


def _attn_kernel(q_ref, kp_ref, kc_ref, vp_ref, vc_ref, bias_ref, o_ref, lse_ref):
    first = pl.program_id(2) == 0
    lane = lax.broadcasted_iota(jnp.int32, (1, 2 * WIN), 1)
    prev_mask = jnp.where(jnp.logical_and(first, lane < WIN), NEG, 0.0)
    for kvh in range(N_KV_HEADS):
        cs = slice(kvh * HEAD_DIM, (kvh + 1) * HEAD_DIM)
        kw = jnp.concatenate([kp_ref[0, :, cs], kc_ref[0, :, cs]], axis=0).astype(BF16)
        vw = jnp.concatenate([vp_ref[0, :, cs], vc_ref[0, :, cs]], axis=0).astype(BF16)
        for pair in range(KV_REP // 2):
            outs, lses = [], []
            for r in range(2):
                h = kvh * KV_REP + pair * 2 + r
                qh = q_ref[0, :, h * HEAD_DIM:(h + 1) * HEAD_DIM]
                s = lax.dot_general(qh, kw, (((1,), (1,)), ((), ())), preferred_element_type=F32)
                s = s + bias_ref[h] + prev_mask
                mx = jnp.max(s, axis=-1, keepdims=True)
                p = jnp.exp(s - mx)
                l = jnp.sum(p, axis=-1, keepdims=True)
                o = jnp.dot(p.astype(BF16), vw, preferred_element_type=F32)
                outs.append(o * (1.0 / l))
                lses.append(jnp.broadcast_to(mx + jnp.log(l), (WIN, HEAD_DIM)))
            h0 = kvh * KV_REP + pair * 2
            o_ref[0, :, h0 * HEAD_DIM:(h0 + 2) * HEAD_DIM] = jnp.concatenate(outs, axis=-1)
            lse_ref[0, :, h0 * HEAD_DIM:(h0 + 2) * HEAD_DIM] = jnp.concatenate(lses, axis=-1)


def _attn_branch(q, k, v, bias_mat, dil):
    b, s, _ = q.shape
    sub = s // dil
    nb = sub // WIN
    qv = q.reshape(b, sub, dil * ATT_WIDTH)
    kv_ = k.reshape(b, sub, dil * KV_WIDTH)
    vv = v.reshape(b, sub, dil * KV_WIDTH)
    cur = lambda bb, r, i: (bb, i, r)
    prev = lambda bb, r, i: (bb, jnp.maximum(i - 1, 0), r)
    o, lse = pl.pallas_call(
        _attn_kernel,
        grid=(b, dil, nb),
        in_specs=[pl.BlockSpec((1, WIN, ATT_WIDTH), cur),
                  pl.BlockSpec((1, WIN, KV_WIDTH), prev),
                  pl.BlockSpec((1, WIN, KV_WIDTH), cur),
                  pl.BlockSpec((1, WIN, KV_WIDTH), prev),
                  pl.BlockSpec((1, WIN, KV_WIDTH), cur),
                  _const_spec((N_ATT_HEADS, WIN, 2 * WIN))],
        out_specs=[pl.BlockSpec((1, WIN, ATT_WIDTH), cur),
                   pl.BlockSpec((1, WIN, ATT_WIDTH), cur)],
        out_shape=[jax.ShapeDtypeStruct((b, sub, dil * ATT_WIDTH), F32),
                   jax.ShapeDtypeStruct((b, sub, dil * ATT_WIDTH), F32)],
        compiler_params=_cparams(("parallel", "parallel", "arbitrary")),
        name=f"attn_dil{dil}",
    )(qv, kv_, kv_, vv, vv, bias_mat)
    return o.reshape(b * s, ATT_WIDTH), lse.reshape(b * s, ATT_WIDTH)


def _softplus(x):
    return jnp.maximum(x, 0.0) + jnp.log(1.0 + jnp.exp(-jnp.abs(x)))


def _ssd_kernel(xbc_ref, dt_ref, z_ref, cw_ref, cb_ref, dtb_ref, alog_ref, dskip_ref, gn_ref, e_ref,
                y_ref, st_ref, ext_ref, state_ref):
    c = pl.program_id(1)
    L = SSD_CHUNK

    @pl.when(c == 0)
    def _():
        ext_ref[0:SUBLANES, :] = jnp.zeros((SUBLANES, CONV_DIM), F32)
        state_ref[...] = jnp.zeros_like(state_ref)

    ext_ref[SUBLANES:SUBLANES + L, :] = xbc_ref[...]
    acc = cb_ref[...] + ext_ref[SUBLANES:SUBLANES + L, :] * cw_ref[CONV_W - 1:CONV_W, :]
    for i in range(CONV_W - 1):
        off = SUBLANES - (CONV_W - 1) + i
        acc = acc + ext_ref[off:off + L, :] * cw_ref[i:i + 1, :]
    ext_ref[0:SUBLANES, :] = ext_ref[L:L + SUBLANES, :]
    xa = _silu(acc)

    dt = _softplus(dt_ref[...] + dtb_ref[...])
    da = dt * (-jnp.exp(alog_ref[...]))
    row = lax.broadcasted_iota(jnp.int32, (L, L), 0)
    col = lax.broadcasted_iota(jnp.int32, (L, L), 1)
    tri = row >= col
    a_cs = jnp.dot(tri.astype(F32), da, preferred_element_type=F32, precision=HIGHEST)
    a_cs_t = a_cs.T
    expand = e_ref[...]
    acs_full = jnp.dot(a_cs, expand, preferred_element_type=F32, precision=HIGHEST)
    dt_full = jnp.dot(dt, expand, preferred_element_type=F32, precision=HIGHEST)
    exp_acs = jnp.exp(acs_full)
    a_last = acs_full[L - 1:L, :]
    exp_last = exp_acs[L - 1:L, :]
    xs = xa[:, :SSM_WIDTH]
    xdt = xs * dt_full
    xw = xdt * jnp.exp(a_last - acs_full)

    for g in range(N_SSM_GROUPS):
        b0 = SSM_WIDTH + g * D_STATE
        c0 = SSM_WIDTH + N_SSM_GROUPS * D_STATE + g * D_STATE
        bg_t = xa[:, b0:b0 + D_STATE].T.astype(BF16)
        cg = xa[:, c0:c0 + D_STATE].astype(BF16)
        gram = jnp.dot(cg, bg_t, preferred_element_type=F32)
        for hh in range(HEADS_PER_GROUP):
            h = g * HEADS_PER_GROUP + hh
            hs = slice(h * SSM_HEAD_DIM, (h + 1) * SSM_HEAD_DIM)
            seg = jnp.where(tri, a_cs[:, h:h + 1] - a_cs_t[h:h + 1, :], NEG)
            scores = (gram * jnp.exp(seg)).astype(BF16)
            y_diag = jnp.dot(scores, xdt[:, hs].astype(BF16), preferred_element_type=F32)
            st = state_ref[h]
            y_off = jnp.dot(cg, st.astype(BF16), preferred_element_type=F32) * exp_acs[:, hs]
            y_ref[:, hs] = y_diag + y_off
            state_ref[h] = exp_last[:, hs] * st + jnp.dot(bg_t, xw[:, hs].astype(BF16),
                                                          preferred_element_type=F32)

    y = y_ref[...] + dskip_ref[...] * xs
    u = y * _silu(z_ref[...])
    gw = SSM_WIDTH // N_SSM_GROUPS
    parts = []
    for g in range(N_SSM_GROUPS):
        ug = u[:, g * gw:(g + 1) * gw]
        parts.append(ug * lax.rsqrt(jnp.mean(ug * ug, axis=-1, keepdims=True) + EPS))
    y_ref[...] = jnp.concatenate(parts, axis=-1) * gn_ref[...]

    @pl.when(c == pl.num_programs(1) - 1)
    def _():
        st_ref[0] = state_ref[...]


def _head_expand():
    e = np.zeros((LANES, SSM_WIDTH), np.float32)
    for h in range(N_SSM_HEADS):
        e[h, h * SSM_HEAD_DIM:(h + 1) * SSM_HEAD_DIM] = 1.0
    return jnp.asarray(e)


def _pad_lanes(v):
    return jnp.pad(v.astype(F32), (0, LANES - v.shape[0])).reshape(1, LANES)


def _ssd_prompt(xbc, dt_raw, z, conv_w, conv_b, dt_bias, a_log, d_skip, ssd_norm, batch):
    m = xbc.shape[0]
    nc = m // batch // SSD_CHUNK
    L = SSD_CHUNK
    rows = lambda b, c: (b * nc + c, 0)
    y, st = pl.pallas_call(
        _ssd_kernel,
        grid=(batch, nc),
        in_specs=[pl.BlockSpec((L, CONV_DIM), rows),
                  pl.BlockSpec((L, LANES), rows),
                  pl.BlockSpec((L, SSM_WIDTH), rows),
                  _const_spec((CONV_W, CONV_DIM)),
                  _const_spec((1, CONV_DIM)),
                  _const_spec((1, LANES)),
                  _const_spec((1, LANES)),
                  _const_spec((1, SSM_WIDTH)),
                  _const_spec((1, SSM_WIDTH)),
                  _const_spec((LANES, SSM_WIDTH))],
        out_specs=[pl.BlockSpec((L, SSM_WIDTH), rows),
                   pl.BlockSpec((1, N_SSM_HEADS, D_STATE, SSM_HEAD_DIM), lambda b, c: (b, 0, 0, 0))],
        out_shape=[jax.ShapeDtypeStruct((m, SSM_WIDTH), F32),
                   jax.ShapeDtypeStruct((batch, N_SSM_HEADS, D_STATE, SSM_HEAD_DIM), F32)],
        scratch_shapes=[pltpu.VMEM((SUBLANES + L, CONV_DIM), F32),
                        pltpu.VMEM((N_SSM_HEADS, D_STATE, SSM_HEAD_DIM), F32)],
        compiler_params=_cparams(("parallel", "arbitrary")),
        name="ssd_prompt",
    )(xbc, dt_raw, z, conv_w, conv_b.reshape(1, CONV_DIM), _pad_lanes(dt_bias), _pad_lanes(a_log),
      jnp.repeat(d_skip, SSM_HEAD_DIM).reshape(1, SSM_WIDTH), ssd_norm.reshape(1, SSM_WIDTH),
      _head_expand())
    return y, jnp.swapaxes(st, -1, -2)


def _router(xn_bf, wr_ref, br_ref):
    return jnp.dot(xn_bf, wr_ref[...], preferred_element_type=F32) + br_ref[...]


def _outproj_kernel(o0, l0, o1, l1, o2, l2, ssm_ref, x_ref, w_ref, g_ref, wr_ref, br_ref,
                    x1_ref, xn_ref, lg_ref):
    la, lb, lc = l0[...], l1[...], l2[...]
    mx = jnp.maximum(jnp.maximum(la, lb), lc)
    wa, wb, wc = jnp.exp(la - mx), jnp.exp(lb - mx), jnp.exp(lc - mx)
    att = (wa * o0[...] + wb * o1[...] + wc * o2[...]) * (1.0 / (wa + wb + wc))
    y = jnp.dot(att.astype(BF16), w_ref[:ATT_WIDTH, :], preferred_element_type=F32)
    y = y + jnp.dot(ssm_ref[...].astype(BF16), w_ref[ATT_WIDTH:, :], preferred_element_type=F32)
    x1 = x_ref[...] + y
    x1_ref[...] = x1
    xn = _rms(x1, g_ref[...]).astype(BF16)
    xn_ref[...] = xn
    lg_ref[...] = _router(xn, wr_ref, br_ref)


def _outproj_prompt(att_parts, ssm, x2d, w_bf, g, wr, br):
    m, d = x2d.shape
    mix = w_bf.shape[0]
    row = lambda w: pl.BlockSpec((TM_OUT, w), lambda i: (i, 0))
    return pl.pallas_call(
        _outproj_kernel,
        grid=(m // TM_OUT,),
        in_specs=[row(ATT_WIDTH)] * 6 + [row(SSM_WIDTH), row(d),
                                         _const_spec((mix, d)), _const_spec((1, d)),
                                         _const_spec((d, LANES)), _const_spec((1, LANES))],
        out_specs=[row(d), row(d), row(LANES)],
        out_shape=[jax.ShapeDtypeStruct((m, d), F32), jax.ShapeDtypeStruct((m, d), BF16),
                   jax.ShapeDtypeStruct((m, LANES), F32)],
        compiler_params=_cparams(("parallel",)),
        name="outproj_prompt",
    )(*att_parts, ssm, x2d, w_bf, g.reshape(1, d), wr, br)


def _attn_sample_kernel(q_ref, kn_ref, vn_ref, k0_ref, k1_ref, k2_ref, v0_ref, v1_ref, v2_ref,
                        bias_ref, bias0_ref, o_ref):
    q = q_ref[0]
    head_grp = lax.broadcasted_iota(jnp.int32, (N_ATT_HEADS, 1), 0) // KV_REP
    kn = kn_ref[0].astype(BF16).astype(F32)
    vn = vn_ref[0].astype(BF16).astype(F32)
    s_self = jnp.sum(q.astype(F32) * kn, axis=-1, keepdims=True)
    scores, selfs, lses = [], [], []
    for g, kc_ref in enumerate((k0_ref, k1_ref, k2_ref)):
        kk = kc_ref[0].astype(BF16)
        s = jnp.zeros((N_ATT_HEADS, WIN), F32)
        for kvh in range(N_KV_HEADS):
            cs = slice(kvh * HEAD_DIM, (kvh + 1) * HEAD_DIM)
            sk = lax.dot_general(q, kk[:, cs], (((1,), (1,)), ((), ())), preferred_element_type=F32)
            s = jnp.where(head_grp == kvh, sk, s)
        s = s + bias_ref[g]
        s0 = s_self + bias0_ref[g]
        mx = jnp.maximum(jnp.max(s, axis=-1, keepdims=True), s0)
        lse = mx + jnp.log(jnp.sum(jnp.exp(s - mx), axis=-1, keepdims=True) + jnp.exp(s0 - mx))
        scores.append(s); selfs.append(s0); lses.append(lse)
    top = functools.reduce(jnp.maximum, lses)
    es = [jnp.exp(l - top) for l in lses]
    tot = functools.reduce(jnp.add, es)
    o = jnp.zeros((N_ATT_HEADS, HEAD_DIM), F32)
    for s, s0, lse, e, vc_ref in zip(scores, selfs, lses, es, (v0_ref, v1_ref, v2_ref)):
        wgt = e / tot
        p = (jnp.exp(s - lse) * wgt).astype(BF16)
        p0 = (jnp.exp(s0 - lse) * wgt).astype(BF16).astype(F32)
        vv = vc_ref[0].astype(BF16)
        o = o + p0 * vn
        for kvh in range(N_KV_HEADS):
            cs = slice(kvh * HEAD_DIM, (kvh + 1) * HEAD_DIM)
            ok = jnp.dot(p, vv[:, cs], preferred_element_type=F32)
            o = o + jnp.where(head_grp == kvh, ok, 0.0)
    o_ref[0] = o


def _attn_sample(q, k_new, v_new, k_cache, v_cache, bias_s, bias0_s):
    n, w = k_cache.shape[0], k_cache.shape[1]
    assert w % (max(DILATIONS) * WIN) == 0
    tok = lambda b: (b, 0, 0)
    head = pl.BlockSpec((1, N_ATT_HEADS, HEAD_DIM), tok)
    taps = [pl.BlockSpec((1, WIN, KV_WIDTH), functools.partial(lambda nb, b: (b, nb - 1, 0), w // dil // WIN))
            for dil in DILATIONS]
    views = lambda c: [c.reshape(n, w // dil, dil * KV_WIDTH) for dil in DILATIONS]
    return pl.pallas_call(
        _attn_sample_kernel,
        grid=(n,),
        in_specs=[head, head, head] + taps + taps +
                 [_const_spec((len(DILATIONS), N_ATT_HEADS, WIN)),
                  _const_spec((len(DILATIONS), N_ATT_HEADS, 1))],
        out_specs=head,
        out_shape=jax.ShapeDtypeStruct((n, N_ATT_HEADS, HEAD_DIM), F32),
        compiler_params=_cparams(("parallel",)),
        name="attn_sample",
    )(q, jnp.repeat(k_new, KV_REP, axis=1), jnp.repeat(v_new, KV_REP, axis=1), *views(k_cache),
      *views(v_cache), bias_s, bias0_s)


def _conv_sample_kernel(xbc_ref, b0_ref, b1_ref, b2_ref, cw_ref, cb_ref, dt_ref, dtb_ref, alog_ref,
                        e_ref, xa_ref, xdt_ref, decay_ref):
    acc = cb_ref[...] + xbc_ref[...] * cw_ref[CONV_W - 1:CONV_W, :]
    for i, buf in enumerate((b0_ref, b1_ref, b2_ref)):
        acc = acc + buf[...] * cw_ref[i:i + 1, :]
    xa = _silu(acc)
    xa_ref[...] = xa
    dt = _softplus(dt_ref[...] + dtb_ref[...])
    decay_ref[...] = jnp.exp(dt * (-jnp.exp(alog_ref[...])))
    dt_full = jnp.dot(dt, e_ref[...], preferred_element_type=F32, precision=HIGHEST)
    xdt_ref[...] = xa[:, :SSM_WIDTH] * dt_full


def _conv_sample(xbc, conv_buf, conv_w, conv_b, dt_raw, dt_bias, a_log):
    n = xbc.shape[0]
    args = (xbc, conv_buf[:, 0], conv_buf[:, 1], conv_buf[:, 2], conv_w, conv_b.reshape(1, CONV_DIM),
            dt_raw, _pad_lanes(dt_bias), _pad_lanes(a_log),
            _head_expand())
    return pl.pallas_call(
        _conv_sample_kernel,
        grid=(1,),
        in_specs=[_const_spec(a.shape) for a in args],
        out_specs=[_const_spec((n, CONV_DIM)), _const_spec((n, SSM_WIDTH)), _const_spec((n, LANES))],
        out_shape=[jax.ShapeDtypeStruct((n, CONV_DIM), F32), jax.ShapeDtypeStruct((n, SSM_WIDTH), F32),
                   jax.ShapeDtypeStruct((n, LANES), F32)],
        compiler_params=_cparams(("arbitrary",)),
        name="conv_sample",
    )(*args)


def _ssm_sample_kernel(xdt_ref, decay_ref, b_ref, c_ref, h0_ref, hn_ref, y_ref):
    for g in range(N_SSM_GROUPS):
        hs = slice(g * HEADS_PER_GROUP, (g + 1) * HEADS_PER_GROUP)
        hn = decay_ref[0, hs] * h0_ref[0, hs] + xdt_ref[0, hs] * b_ref[0, g]
        hn_ref[0, hs] = hn
        c_row = c_ref[0, g].astype(BF16).astype(F32)
        y_ref[0, hs] = jnp.sum(hn.astype(BF16).astype(F32) * c_row, axis=-1, keepdims=True)


def _ssm_sample(xdt, decay, bmat, cmat, h0):
    n = xdt.shape[0]
    p = SSM_HEAD_DIM
    tok4 = lambda b: (b, 0, 0, 0)
    hn, y = pl.pallas_call(
        _ssm_sample_kernel,
        grid=(n,),
        in_specs=[pl.BlockSpec((1, N_SSM_HEADS, p, 1), tok4),
                  pl.BlockSpec((1, N_SSM_HEADS, 1, 1), tok4),
                  pl.BlockSpec((1, N_SSM_GROUPS, 1, D_STATE), tok4),
                  pl.BlockSpec((1, N_SSM_GROUPS, 1, D_STATE), tok4),
                  pl.BlockSpec((1, N_SSM_HEADS, p, D_STATE), tok4)],
        out_specs=[pl.BlockSpec((1, N_SSM_HEADS, p, D_STATE), tok4),
                   pl.BlockSpec((1, N_SSM_HEADS, p, 1), tok4)],
        out_shape=[jax.ShapeDtypeStruct((n, N_SSM_HEADS, p, D_STATE), F32),
                   jax.ShapeDtypeStruct((n, N_SSM_HEADS, p, 1), F32)],
        compiler_params=_cparams(("parallel",)),
        name="ssm_sample",
    )(xdt.reshape(n, N_SSM_HEADS, p, 1), decay[:, :N_SSM_HEADS].reshape(n, N_SSM_HEADS, 1, 1),
      bmat.reshape(n, N_SSM_GROUPS, 1, D_STATE), cmat.reshape(n, N_SSM_GROUPS, 1, D_STATE), h0)
    return hn, y.reshape(n, SSM_WIDTH)


def _outproj_sample_kernel(att_ref, y_ref, xs_ref, z_ref, dskip_ref, gn_ref, x_ref, w_ref, g_ref,
                           wr_ref, br_ref, x1_ref, xn_ref, lg_ref):
    y = y_ref[...] + dskip_ref[...] * xs_ref[...]
    u = y * _silu(z_ref[...])
    gw = SSM_WIDTH // N_SSM_GROUPS
    parts = []
    for g in range(N_SSM_GROUPS):
        ug = u[:, g * gw:(g + 1) * gw]
        parts.append(ug * lax.rsqrt(jnp.mean(ug * ug, axis=-1, keepdims=True) + EPS)
                     * gn_ref[:, g * gw:(g + 1) * gw])
    mix = jnp.concatenate([att_ref[...]] + parts, axis=-1).astype(BF16)
    x1 = x_ref[...] + jnp.dot(mix, w_ref[...], preferred_element_type=F32)
    x1_ref[...] = x1
    xn = _rms(x1, g_ref[...]).astype(BF16)
    xn_ref[...] = xn
    lg_ref[...] = _router(xn, wr_ref, br_ref)


def _outproj_sample(att, y, xs, z, d_skip, ssd_norm, x2d, w_bf, g, wr, br):
    n, d = x2d.shape
    args = (att, y, xs, z, jnp.repeat(d_skip, SSM_HEAD_DIM).reshape(1, SSM_WIDTH),
            ssd_norm.reshape(1, SSM_WIDTH), x2d, w_bf, g.reshape(1, d), wr, br)
    return pl.pallas_call(
        _outproj_sample_kernel,
        grid=(1,),
        in_specs=[_const_spec(a.shape) for a in args],
        out_specs=[_const_spec((n, d)), _const_spec((n, d)), _const_spec((n, LANES))],
        out_shape=[jax.ShapeDtypeStruct((n, d), F32), jax.ShapeDtypeStruct((n, d), BF16),
                   jax.ShapeDtypeStruct((n, LANES), F32)],
        compiler_params=_cparams(("arbitrary",)),
        name="outproj_sample",
    )(*args)


TT = 256
LROWS = TT * TOP_K + N_EXPERTS * SUBLANES
CHUNK_SIZES = (256, 128, 64, 32, 16, 8)
SUB = 256
RG = 5 * SUB
TF = 256
VMEM_LIMIT_FFN = 60000 * 1024


def _chunk_loop(cnt_ref, loff_ref, dest_ref, tile, fn):
    def per_expert(e, carry):
        idx = tile * N_EXPERTS + e
        n, off, dst = cnt_ref[idx], loff_ref[idx], dest_ref[idx]
        for size in CHUNK_SIZES:
            take = (n & size) != 0

            @pl.when(take)
            def _(off=off, dst=dst, size=size):
                fn(pl.multiple_of(off, SUBLANES), pl.multiple_of(dst, SUBLANES), size)

            step = jnp.where(take, size, 0)
            off, dst = off + step, dst + step
        return carry

    lax.fori_loop(0, N_EXPERTS, per_expert, 0)


def _dispatch_kernel(cnt_ref, loff_ref, dest_ref, xp_ref, xs_ref, lpos_ref, gate_ref, out_hbm, buf, sem):
    t = pl.program_id(0)
    d = xp_ref.shape[1]
    x = jnp.where(t == pl.num_programs(0) - 1, xs_ref[...], xp_ref[...])
    rows = lax.broadcasted_iota(jnp.int32, (LROWS, TT), 0)
    onehot = jnp.zeros((LROWS, TT), F32)
    wcol = jnp.zeros((LROWS, 1), F32)
    for k in range(TOP_K):
        hit = jnp.where(rows == lpos_ref[0, k:k + 1, :], 1.0, 0.0)
        onehot = onehot + hit
        wcol = wcol + jnp.sum(hit * gate_ref[0, k:k + 1, :], axis=-1, keepdims=True)
    buf[:, 0:d] = jnp.dot(onehot.astype(BF16), x, preferred_element_type=F32)
    buf[:, d:d + LANES] = jnp.broadcast_to(wcol, (LROWS, LANES))

    def copy(off, dst, size):
        return pltpu.make_async_copy(buf.at[pl.ds(off, size)], out_hbm.at[pl.ds(dst, size)], sem)

    _chunk_loop(cnt_ref, loff_ref, dest_ref, t, lambda o, g, s: copy(o, g, s).start())
    _chunk_loop(cnt_ref, loff_ref, dest_ref, t, lambda o, g, s: copy(o, g, s).wait())


def _dispatch(tabs, xn_p, xn_s, lpos_t, gate_t, n_rows):
    nt = lpos_t.shape[0]
    d = xn_p.shape[1]
    last_p = xn_p.shape[0] // TT - 1
    grid_spec = pltpu.PrefetchScalarGridSpec(
        num_scalar_prefetch=3,
        grid=(nt,),
        in_specs=[pl.BlockSpec((TT, d), lambda t, *_: (jnp.minimum(t, last_p), 0)),
                  pl.BlockSpec((TT, d), lambda t, *_: (0, 0)),
                  pl.BlockSpec((1, TOP_K, TT), lambda t, *_: (t, 0, 0)),
                  pl.BlockSpec((1, TOP_K, TT), lambda t, *_: (t, 0, 0))],
        out_specs=pl.BlockSpec(memory_space=pl.ANY),
        scratch_shapes=[pltpu.VMEM((LROWS, d + LANES), F32), pltpu.SemaphoreType.DMA(())],
    )
    return pl.pallas_call(
        _dispatch_kernel,
        grid_spec=grid_spec,
        out_shape=jax.ShapeDtypeStruct((n_rows, d + LANES), F32),
        compiler_params=_cparams(("arbitrary",)),
        name="moe_dispatch",
    )(*tabs, xn_p, xn_s, lpos_t, gate_t)


def _ffn_kernel(ge_ref, gs_ref, gn_ref, ng_ref, xs_hbm, wgu_hbm, wd_hbm, bgu_ref, bdn_ref, out_hbm,
                xbuf, acc, ostage, wg_st, wu_st, wd_st, wg_bf, wu_bf, wd_bf, sem_w, sem_x, sem_o):
    d = acc.shape[1]
    d_ff = wd_hbm.shape[1]
    nf = d_ff // TF
    n_groups = ng_ref[0]
    total = n_groups * nf

    def w_copies(s, slot):
        g = s // nf
        f = s - g * nf
        e = ge_ref[g]
        c0 = pl.multiple_of(f * TF, TF)
        return (pltpu.make_async_copy(wgu_hbm.at[e, :, pl.ds(c0, TF)], wg_st.at[slot], sem_w.at[slot, 0]),
                pltpu.make_async_copy(wgu_hbm.at[e, :, pl.ds(d_ff + c0, TF)], wu_st.at[slot], sem_w.at[slot, 1]),
                pltpu.make_async_copy(wd_hbm.at[e, pl.ds(c0, TF), :], wd_st.at[slot], sem_w.at[slot, 2]))

    def x_copy(g, j):
        r0 = pl.multiple_of(j * SUB, SUB)
        return pltpu.make_async_copy(xs_hbm.at[pl.ds(pl.multiple_of(gs_ref[g] + r0, SUBLANES), SUB)],
                                     xbuf.at[g % 2, pl.ds(r0, SUB)], sem_x)

    def o_copy(g, j):
        r0 = pl.multiple_of(j * SUB, SUB)
        return pltpu.make_async_copy(ostage.at[j % 2],
                                     out_hbm.at[pl.ds(pl.multiple_of(gs_ref[g] + r0, SUBLANES), SUB)],
                                     sem_o.at[j % 2])

    def loop(n, fn):
        lax.fori_loop(0, n, lambda j, c: (fn(j), c)[1], 0)

    def drain_stores(g):
        n = gn_ref[g]
        for back in (1, 2):
            @pl.when(n >= back)
            def _():
                o_copy(g, n - back).wait()

    @pl.when(total > 0)
    def _():
        for c in w_copies(0, 0):
            c.start()
        loop(gn_ref[0], lambda j: x_copy(0, j).start())

    def item(s, carry):
        slot = s % 2
        g = s // nf
        f = s - g * nf
        e = ge_ref[g]
        nsub = gn_ref[g]
        xg = xbuf.at[g % 2]

        @pl.when(s + 1 < total)
        def _():
            for c in w_copies(s + 1, 1 - slot):
                c.start()

        @pl.when(f == 0)
        def _():
            loop(nsub, lambda j: x_copy(g, j).wait())

        @pl.when(jnp.logical_and(f == 1, g + 1 < n_groups))
        def _():
            loop(gn_ref[g + 1], lambda j: x_copy(g + 1, j).start())

        for c in w_copies(s, slot):
            c.wait()
        wg_bf[...] = wg_st[slot].astype(BF16)
        wu_bf[...] = wu_st[slot].astype(BF16)
        wd_bf[...] = wd_st[slot].astype(BF16)
        bg = bgu_ref[pl.ds(e * 2 * nf + f, 1), :]
        bu = bgu_ref[pl.ds(e * 2 * nf + nf + f, 1), :]

        def sub_block(j, phase):
            rs = pl.ds(pl.multiple_of(j * SUB, SUB), SUB)
            x = xg[rs, 0:d].astype(BF16)
            hg = jnp.dot(x, wg_bf[...], preferred_element_type=F32) + bg
            hu = jnp.dot(x, wu_bf[...], preferred_element_type=F32) + bu
            gg = jnp.minimum(hg, SWIGLU_LIMIT)
            uu = jnp.clip(hu, -SWIGLU_LIMIT, SWIGLU_LIMIT)
            act = gg * (1.0 / (1.0 + jnp.exp(-SWIGLU_ALPHA * gg))) * (uu + 1.0)
            part = jnp.dot(act.astype(BF16), wd_bf[...], preferred_element_type=F32)
            if phase == "first":
                acc[rs, :] = part + bdn_ref[pl.ds(e, 1), :]
            elif phase == "middle":
                acc[rs, :] += part
            else:
                @pl.when(j >= 2)
                def _():
                    o_copy(g, j - 2).wait()

                ostage[j % 2] = (acc[rs, :] + part) * xg[rs, d:d + 1]
                o_copy(g, j).start()

        @pl.when(f == 0)
        def _():
            loop(nsub, lambda j: sub_block(j, "first"))

        @pl.when(jnp.logical_and(f > 0, f < nf - 1))
        def _():
            loop(nsub, lambda j: sub_block(j, "middle"))

        @pl.when(f == nf - 1)
        def _():
            @pl.when(g > 0)
            def _():
                drain_stores(g - 1)

            loop(nsub, lambda j: sub_block(j, "last"))

        return carry

    lax.fori_loop(0, total, item, 0)

    @pl.when(total > 0)
    def _():
        drain_stores(n_groups - 1)


def _moe_ffn(groups, x_sorted, w_gate_up, b_gu, w_down, b_dn):
    n_rows = x_sorted.shape[0]
    _, d_ff, d = w_down.shape
    nf = d_ff // TF
    assert nf >= 2
    bgu2 = b_gu.reshape(N_EXPERTS * 2 * nf, TF)
    grid_spec = pltpu.PrefetchScalarGridSpec(
        num_scalar_prefetch=4,
        grid=(1,),
        in_specs=[pl.BlockSpec(memory_space=pl.ANY), pl.BlockSpec(memory_space=pl.ANY),
                  pl.BlockSpec(memory_space=pl.ANY),
                  pl.BlockSpec(bgu2.shape, lambda i, *_: (0, 0), pipeline_mode=pl.Buffered(1)),
                  pl.BlockSpec(b_dn.shape, lambda i, *_: (0, 0), pipeline_mode=pl.Buffered(1))],
        out_specs=pl.BlockSpec(memory_space=pl.ANY),
        scratch_shapes=[pltpu.VMEM((2, RG, d + LANES), F32), pltpu.VMEM((RG, d), F32),
                        pltpu.VMEM((2, SUB, d), F32),
                        pltpu.VMEM((2, d, TF), F32), pltpu.VMEM((2, d, TF), F32), pltpu.VMEM((2, TF, d), F32),
                        pltpu.VMEM((d, TF), BF16), pltpu.VMEM((d, TF), BF16), pltpu.VMEM((TF, d), BF16),
                        pltpu.SemaphoreType.DMA((2, 3)), pltpu.SemaphoreType.DMA(()),
                        pltpu.SemaphoreType.DMA((2,))],
    )
    return pl.pallas_call(
        _ffn_kernel,
        grid_spec=grid_spec,
        out_shape=jax.ShapeDtypeStruct((n_rows, d), F32),
        compiler_params=pltpu.CompilerParams(dimension_semantics=("arbitrary",),
                                             vmem_limit_bytes=VMEM_LIMIT_FFN),
        name="moe_ffn",
    )(*groups, x_sorted, w_gate_up, w_down, bgu2, b_dn)


def _route(logits, m_pad):
    m = logits.shape[0]
    nt = m_pad // TT
    top_v, top_i = lax.top_k(logits, TOP_K)
    gate = jnp.pad(jax.nn.softmax(top_v, axis=-1), ((0, m_pad - m), (0, 0)))
    top_i = jnp.pad(top_i.astype(jnp.int32), ((0, m_pad - m), (0, 0)), constant_values=-1)
    chosen = (top_i[:, :, None] == jnp.arange(N_EXPERTS, dtype=jnp.int32)).astype(jnp.int32)
    tiles = chosen.sum(axis=1).reshape(nt, TT, N_EXPERTS)
    cnt = (tiles.sum(axis=1) + SUBLANES - 1) // SUBLANES * SUBLANES
    loff = jnp.cumsum(cnt, axis=1) - cnt
    seg = cnt.sum(axis=0)
    seg_start = jnp.cumsum(seg) - seg
    dest = seg_start[None, :] + jnp.cumsum(cnt, axis=0) - cnt
    rank = jnp.cumsum(tiles, axis=1) - tiles
    lpos_all = (loff[:, None, :] + rank).reshape(m_pad, N_EXPERTS)
    lpos = (lpos_all[:, None, :] * chosen).sum(axis=-1)
    lpos = jnp.where(top_i >= 0, lpos, -1).astype(jnp.int32)
    n_rows_bound = nt * LROWS
    n_grp_max = n_rows_bound // RG + N_EXPERTS
    grp = (seg + RG - 1) // RG
    grp_end = jnp.cumsum(grp)
    gi = jnp.arange(n_grp_max, dtype=jnp.int32)
    g_exp = jnp.minimum(jnp.searchsorted(grp_end, gi, side='right'), N_EXPERTS - 1).astype(jnp.int32)
    within = gi - (grp_end[g_exp] - grp[g_exp])
    g_start = (seg_start[g_exp] + within * RG).astype(jnp.int32)
    g_rows = jnp.clip(seg[g_exp] - within * RG, 0, RG)
    g_nsub = jnp.where(gi < grp_end[-1], (g_rows + SUB - 1) // SUB, 0).astype(jnp.int32)
    groups = (g_exp, g_start, g_nsub, grp_end[-1:].astype(jnp.int32))
    tabs = tuple(a.reshape(-1).astype(jnp.int32) for a in (cnt, loff, dest))
    return tabs, lpos, gate, groups, n_rows_bound + SUB


def _combine_kernel(cnt_ref, loff_ref, dest_ref, src_hbm, lpos_ref, x1p_ref, x1s_ref, g_ref,
                    yp_ref, ys_ref, buf, sem):
    t = pl.program_id(0)
    last = pl.num_programs(0) - 1

    @pl.when(t == 0)
    def _():
        buf[...] = jnp.zeros_like(buf)

    def copy(off, src, size):
        return pltpu.make_async_copy(src_hbm.at[pl.ds(src, size)], buf.at[pl.ds(off, size)], sem)

    _chunk_loop(cnt_ref, loff_ref, dest_ref, t, lambda o, g, s: copy(o, g, s).start())
    _chunk_loop(cnt_ref, loff_ref, dest_ref, t, lambda o, g, s: copy(o, g, s).wait())
    cols = lax.broadcasted_iota(jnp.int32, (TT, LROWS), 1)
    sel = jnp.zeros((TT, LROWS), F32)
    for k in range(TOP_K):
        sel = sel + jnp.where(cols == lpos_ref[:, k:k + 1], 1.0, 0.0)
    sel = sel.astype(BF16)
    rows = buf[...]
    hi = rows.astype(BF16)
    lo = (rows - hi.astype(F32)).astype(BF16)
    f = jnp.dot(sel, hi, preferred_element_type=F32) + jnp.dot(sel, lo, preferred_element_type=F32)

    @pl.when(t < last)
    def _():
        yp_ref[...] = _rms(x1p_ref[...] + f, g_ref[...])

    @pl.when(t == last)
    def _():
        ys_ref[...] = _rms(x1s_ref[...] + f, g_ref[...])


def _combine(tabs, out_sorted, lpos, x1p, x1s, g):
    d = x1p.shape[1]
    nt = lpos.shape[0] // TT
    last_p = x1p.shape[0] // TT - 1
    prompt = pl.BlockSpec((TT, d), lambda t, *_: (jnp.minimum(t, last_p), 0))
    sample = pl.BlockSpec((TT, d), lambda t, *_: (0, 0))
    grid_spec = pltpu.PrefetchScalarGridSpec(
        num_scalar_prefetch=3,
        grid=(nt,),
        in_specs=[pl.BlockSpec(memory_space=pl.ANY),
                  pl.BlockSpec((TT, TOP_K), lambda t, *_: (t, 0)),
                  prompt, sample,
                  pl.BlockSpec((1, d), lambda t, *_: (0, 0))],
        out_specs=[prompt, sample],
        scratch_shapes=[pltpu.VMEM((LROWS, d), F32), pltpu.SemaphoreType.DMA(())],
    )
    return pl.pallas_call(
        _combine_kernel,
        grid_spec=grid_spec,
        out_shape=[jax.ShapeDtypeStruct(x1p.shape, F32), jax.ShapeDtypeStruct((TT, d), F32)],
        compiler_params=_cparams(("arbitrary",)),
        name="moe_combine",
    )(*tabs, out_sorted, lpos, x1p, x1s, g.reshape(1, d))


def _t5_bucket(dist):
    max_exact = N_BUCKETS // 2
    dd = dist.astype(F32)
    large = max_exact + (jnp.log(jnp.maximum(dd, 1.0) / max_exact)
                         / math.log(BUCKET_MAX_DIST / max_exact) * (N_BUCKETS - max_exact)).astype(jnp.int32)
    large = jnp.minimum(large, N_BUCKETS - 1)
    return jnp.where(dist < max_exact, dist, large)


def _bias_tables(rel_bias):
    dist = jnp.asarray(np.arange(N_TAPS)[None, :] * np.array(DILATIONS)[:, None], jnp.int32)
    bias = jnp.transpose(rel_bias[_t5_bucket(dist)], (2, 0, 1)).astype(F32)
    tap = WIN + np.arange(WIN)[:, None] - np.arange(2 * WIN)[None, :]
    ok = (tap >= 0) & (tap <= WIN)
    band = jnp.where(ok[None, None], jnp.transpose(bias, (1, 0, 2))[:, :, np.clip(tap, 0, WIN)], NEG)
    by_branch = jnp.transpose(bias, (1, 0, 2))
    return band, by_branch[:, :, WIN - np.arange(WIN)], by_branch[:, :, 0:1]


def kernel(x_prompt, x_sample, cache_k_win, cache_v_win, state_conv, state_ssm, rel_bias, attn_norm, w_in, conv_w, conv_b, dt_bias, a_log, d_skip, ssd_norm, w_out, ffn_norm, w_router, b_router, w_gate_up, b_gate_up, w_down, b_down, final_norm):
    bp, tp, d = x_prompt.shape
    bs, ts, _ = x_sample.shape
    depth = w_in.shape[0]
    assert depth == 1 and ts == 1 and tp % (max(DILATIONS) * WIN) == 0
    keep = min(max(DILATIONS) * WIN, tp)
    band, samp, samp0 = _bias_tables(rel_bias)
    l = 0

    xp = x_prompt.reshape(bp * tp, d)
    xs = x_sample.reshape(bs * ts, d)
    w_in_bf = jnp.pad(w_in[l], ((0, 0), (0, IN_PROJ_PAD - IN_PROJ))).astype(BF16)
    w_out_bf = w_out[l].astype(BF16)
    wr = jnp.pad(w_router[l], ((0, 0), (0, LANES - N_EXPERTS))).astype(BF16)
    br = jnp.pad(b_router[l], (0, LANES - N_EXPERTS), constant_values=NEG).reshape(1, LANES)

    q, k, v, z, xbc, dt_raw = _inproj(xp, attn_norm[l], w_in_bf, TM_PROJ)
    q3 = q.reshape(bp, tp, ATT_WIDTH)
    k3 = k.reshape(bp, tp, KV_WIDTH)
    v3 = v.reshape(bp, tp, KV_WIDTH)
    att_parts = []
    for gi, dil in enumerate(DILATIONS):
        att_parts.extend(_attn_branch(q3, k3, v3, band[gi], dil))
    ssm, st_p = _ssd_prompt(xbc, dt_raw, z, conv_w[l], conv_b[l], dt_bias[l], a_log[l], d_skip[l],
                            ssd_norm[l], bp)
    x1p, xnp_, lgp = _outproj_prompt(att_parts, ssm, xp, w_out_bf, ffn_norm[l], wr, br)
    k_win_p = k3[:, tp - keep:].reshape(1, bp, keep, N_KV_HEADS, HEAD_DIM)
    v_win_p = v3[:, tp - keep:].reshape(1, bp, keep, N_KV_HEADS, HEAD_DIM)
    conv_p = xbc.reshape(bp, tp, CONV_DIM)[:, tp - (CONV_W - 1):][None]

    q_s, k_s, v_s, z_s, xbc_s, dt_s = _inproj(xs, attn_norm[l], w_in_bf, bs * ts)
    q_s = q_s.reshape(bs, N_ATT_HEADS, HEAD_DIM)
    k_s = k_s.reshape(bs, N_KV_HEADS, HEAD_DIM)
    v_s = v_s.reshape(bs, N_KV_HEADS, HEAD_DIM)
    wbuf = cache_k_win.shape[2]
    att_s = _attn_sample(q_s, k_s, v_s, cache_k_win[l].reshape(bs, wbuf, KV_WIDTH),
                         cache_v_win[l].reshape(bs, wbuf, KV_WIDTH), samp, samp0)
    xa_s, xdt_s, decay_s = _conv_sample(xbc_s, state_conv[l], conv_w[l], conv_b[l], dt_s, dt_bias[l],
                                        a_log[l])
    nbc = N_SSM_GROUPS * D_STATE
    h_s, y_s = _ssm_sample(xdt_s, decay_s, xa_s[:, SSM_WIDTH:SSM_WIDTH + nbc], xa_s[:, SSM_WIDTH + nbc:],
                           state_ssm[l])
    x1s, xns, lgs = _outproj_sample(att_s.reshape(bs, ATT_WIDTH), y_s, xa_s[:, :SSM_WIDTH], z_s,
                                    d_skip[l], ssd_norm[l], xs, w_out_bf, ffn_norm[l], wr, br)
    conv_s = jnp.concatenate([state_conv[l][:, 1:], xbc_s[:, None]], axis=1)[None]

    n_s = bs * ts
    assert (bp * tp) % TT == 0 and n_s <= TT
    m_pad = bp * tp + TT
    logits = jnp.concatenate([lgp[:, :N_EXPERTS], lgs[:, :N_EXPERTS]], axis=0)
    tabs, lpos, gate, groups, n_rows = _route(logits, m_pad)
    by_tile = lambda a: jnp.transpose(a.reshape(m_pad // TT, TT, TOP_K), (0, 2, 1))
    pad_rows = lambda a: jnp.pad(a, ((0, TT - n_s), (0, 0)))
    x_sorted = _dispatch(tabs, xnp_, pad_rows(xns), by_tile(lpos), by_tile(gate), n_rows)
    out_sorted = _moe_ffn(groups, x_sorted, w_gate_up[l], b_gate_up[l], w_down[l], b_down[l])
    y_p, y_s_out = _combine(tabs, out_sorted, lpos, x1p, pad_rows(x1s), final_norm)
    y_s_out = y_s_out[:n_s]

    return (y_p.reshape(bp, tp, d), y_s_out.reshape(bs, ts, d), k_win_p, v_win_p, conv_p, st_p[None],
            k_s.reshape(1, bs, ts, N_KV_HEADS, HEAD_DIM), v_s.reshape(1, bs, ts, N_KV_HEADS, HEAD_DIM),
            conv_s, h_s[None])
```

```python
import functools
import math

import jax
import jax.numpy as jnp
import numpy as np
from jax import lax
from jax.experimental import pallas as pl
from jax.experimental.pallas import tpu as pltpu

F32 = jnp.float32
BF16 = jnp.bfloat16
HIGHEST = lax.Precision.HIGHEST

LANES = 128
SUBLANES = 8
VMEM_LIMIT = 56 * 1024 * 1024

HEAD_DIM = 64
N_ATT_HEADS = 16
N_KV_HEADS = 4
KV_REP = N_ATT_HEADS // N_KV_HEADS
ATT_WIDTH = N_ATT_HEADS * HEAD_DIM
KV_WIDTH = N_KV_HEADS * HEAD_DIM
DILATIONS = (1, 4, 16)
N_TAPS = 129
WIN = N_TAPS - 1
ATT_SCALE = HEAD_DIM ** -0.5
N_BUCKETS = 32
BUCKET_MAX_DIST = 2048
SSM_HEAD_DIM = 64
N_SSM_HEADS = 16
SSM_WIDTH = N_SSM_HEADS * SSM_HEAD_DIM
N_SSM_GROUPS = 2
HEADS_PER_GROUP = N_SSM_HEADS // N_SSM_GROUPS
D_STATE = 128
CONV_W = 4
CONV_DIM = SSM_WIDTH + 2 * N_SSM_GROUPS * D_STATE
SSD_CHUNK = 128
N_EXPERTS = 32
TOP_K = 4
SWIGLU_LIMIT = 7.0
SWIGLU_ALPHA = 1.702
EPS = 1e-5
NEG = -1e30

Q0, K0, V0, Z0, X0, DT0 = 0, 1024, 1280, 1536, 2560, 4096
IN_PROJ = DT0 + N_SSM_HEADS
IN_PROJ_PAD = DT0 + LANES

TM_PROJ = 512
TM_OUT = 256
TM_MOE = 512
TF_MOE = 512
TM_FIN = 256


def _cparams(sem):
    return pltpu.CompilerParams(dimension_semantics=sem, vmem_limit_bytes=VMEM_LIMIT)


def _const_spec(shape):
    nd = len(shape)
    return pl.BlockSpec(shape, lambda *_: (0,) * nd, pipeline_mode=pl.Buffered(1))


def _rms(x, g):
    ms = jnp.mean(x * x, axis=-1, keepdims=True)
    return x * lax.rsqrt(ms + EPS) * g


def _silu(x):
    return x * (1.0 / (1.0 + jnp.exp(-x)))


def _inproj_kernel(x_ref, g_ref, w_ref, q_ref, k_ref, v_ref, z_ref, xbc_ref, dt_ref):
    xn = _rms(x_ref[...], g_ref[...]).astype(BF16)

    def mm(lo, hi):
        return jnp.dot(xn, w_ref[:, lo:hi], preferred_element_type=F32)

    q_ref[...] = (mm(Q0, K0) * ATT_SCALE).astype(BF16)
    k_ref[...] = mm(K0, V0)
    v_ref[...] = mm(V0, Z0)
    z_ref[...] = mm(Z0, X0)
    xbc_ref[...] = mm(X0, DT0)
    dt_ref[...] = mm(DT0, IN_PROJ_PAD)


def _inproj(x2d, g, w_bf, tm):
    m, d = x2d.shape
    widths = (ATT_WIDTH, KV_WIDTH, KV_WIDTH, SSM_WIDTH, CONV_DIM, LANES)
    dtypes = (BF16, F32, F32, F32, F32, F32)
    return pl.pallas_call(
        _inproj_kernel,
        grid=(m // tm,),
        in_specs=[pl.BlockSpec((tm, d), lambda i: (i, 0)),
                  _const_spec((1, d)),
                  _const_spec((d, IN_PROJ_PAD))],
        out_specs=[pl.BlockSpec((tm, w), lambda i: (i, 0)) for w in widths],
        out_shape=[jax.ShapeDtypeStruct((m, w), t) for w, t in zip(widths, dtypes)],
        compiler_params=_cparams(("parallel",)),
        name=f"inproj_{tm}",
    )(x2d, g.reshape(1, d), w_bf)


def _attn_kernel(q_ref, kp_ref, kc_ref, vp_ref, vc_ref, bias_ref, o_ref, lse_ref):
    first = pl.program_id(2) == 0
    lane = lax.broadcasted_iota(jnp.int32, (1, 2 * WIN), 1)
    prev_mask = jnp.where(jnp.logical_and(first, lane < WIN), NEG, 0.0)
    for kvh in range(N_KV_HEADS):
        cs = slice(kvh * HEAD_DIM, (kvh + 1) * HEAD_DIM)
        kw = jnp.concatenate([kp_ref[0, :, cs], kc_ref[0, :, cs]], axis=0).astype(BF16)
        vw = jnp.concatenate([vp_ref[0, :, cs], vc_ref[0, :, cs]], axis=0).astype(BF16)
        for pair in range(KV_REP // 2):
            outs, lses = [], []
            for r in range(2):
                h = kvh * KV_REP + pair * 2 + r
                qh = q_ref[0, :, h * HEAD_DIM:(h + 1) * HEAD_DIM]
                s = lax.dot_general(qh, kw, (((1,), (1,)), ((), ())), preferred_element_type=F32)
                s = s + bias_ref[h] + prev_mask
                mx = jnp.max(s, axis=-1, keepdims=True)
                p = jnp.exp(s - mx)
                l = jnp.sum(p, axis=-1, keepdims=True)
                o = jnp.dot(p.astype(BF16), vw, preferred_element_type=F32)
                outs.append(o * (1.0 / l))
                lses.append(jnp.broadcast_to(mx + jnp.log(l), (WIN, HEAD_DIM)))
            h0 = kvh * KV_REP + pair * 2
            o_ref[0, :, h0 * HEAD_DIM:(h0 + 2) * HEAD_DIM] = jnp.concatenate(outs, axis=-1)
            lse_ref[0, :, h0 * HEAD_DIM:(h0 + 2) * HEAD_DIM] = jnp.concatenate(lses, axis=-1)


def _attn_branch(q, k, v, bias_mat, dil):
    b, s, _ = q.shape
    sub = s // dil
    nb = sub // WIN
    qv = q.reshape(b, sub, dil * ATT_WIDTH)
    kv_ = k.reshape(b, sub, dil * KV_WIDTH)
    vv = v.reshape(b, sub, dil * KV_WIDTH)
    cur = lambda bb, r, i: (bb, i, r)
    prev = lambda bb, r, i: (bb, jnp.maximum(i - 1, 0), r)
    o, lse = pl.pallas_call(
        _attn_kernel,
        grid=(b, dil, nb),
        in_specs=[pl.BlockSpec((1, WIN, ATT_WIDTH), cur),
                  pl.BlockSpec((1, WIN, KV_WIDTH), prev),
                  pl.BlockSpec((1, WIN, KV_WIDTH), cur),
                  pl.BlockSpec((1, WIN, KV_WIDTH), prev),
                  pl.BlockSpec((1, WIN, KV_WIDTH), cur),
                  _const_spec((N_ATT_HEADS, WIN, 2 * WIN))],
        out_specs=[pl.BlockSpec((1, WIN, ATT_WIDTH), cur),
                   pl.BlockSpec((1, WIN, ATT_WIDTH), cur)],
        out_shape=[jax.ShapeDtypeStruct((b, sub, dil * ATT_WIDTH), F32),
                   jax.ShapeDtypeStruct((b, sub, dil * ATT_WIDTH), F32)],
        compiler_params=_cparams(("parallel", "parallel", "arbitrary")),
        name=f"attn_dil{dil}",
    )(qv, kv_, kv_, vv, vv, bias_mat)
    return o.reshape(b * s, ATT_WIDTH), lse.reshape(b * s, ATT_WIDTH)


def _softplus(x):
    return jnp.maximum(x, 0.0) + jnp.log(1.0 + jnp.exp(-jnp.abs(x)))


def _ssd_kernel(xbc_ref, dt_ref, z_ref, cw_ref, cb_ref, dtb_ref, alog_ref, dskip_ref, gn_ref, e_ref,
                y_ref, st_ref, ext_ref, state_ref):
    c = pl.program_id(1)
    L = SSD_CHUNK

    @pl.when(c == 0)
    def _():
        ext_ref[0:SUBLANES, :] = jnp.zeros((SUBLANES, CONV_DIM), F32)
        state_ref[...] = jnp.zeros_like(state_ref)

    ext_ref[SUBLANES:SUBLANES + L, :] = xbc_ref[...]
    acc = cb_ref[...] + ext_ref[SUBLANES:SUBLANES + L, :] * cw_ref[CONV_W - 1:CONV_W, :]
    for i in range(CONV_W - 1):
        off = SUBLANES - (CONV_W - 1) + i
        acc = acc + ext_ref[off:off + L, :] * cw_ref[i:i + 1, :]
    ext_ref[0:SUBLANES, :] = ext_ref[L:L + SUBLANES, :]
    xa = _silu(acc)

    dt = _softplus(dt_ref[...] + dtb_ref[...])
    da = dt * (-jnp.exp(alog_ref[...]))
    row = lax.broadcasted_iota(jnp.int32, (L, L), 0)
    col = lax.broadcasted_iota(jnp.int32, (L, L), 1)
    tri = row >= col
    a_cs = jnp.dot(tri.astype(F32), da, preferred_element_type=F32, precision=HIGHEST)
    a_cs_t = a_cs.T
    expand = e_ref[...]
    acs_full = jnp.dot(a_cs, expand, preferred_element_type=F32, precision=HIGHEST)
    dt_full = jnp.dot(dt, expand, preferred_element_type=F32, precision=HIGHEST)
    exp_acs = jnp.exp(acs_full)
    a_last = acs_full[L - 1:L, :]
    exp_last = exp_acs[L - 1:L, :]
    xs = xa[:, :SSM_WIDTH]
    xdt = xs * dt_full
    xw = xdt * jnp.exp(a_last - acs_full)

    for g in range(N_SSM_GROUPS):
        b0 = SSM_WIDTH + g * D_STATE
        c0 = SSM_WIDTH + N_SSM_GROUPS * D_STATE + g * D_STATE
        bg_t = xa[:, b0:b0 + D_STATE].T.astype(BF16)
        cg = xa[:, c0:c0 + D_STATE].astype(BF16)
        gram = jnp.dot(cg, bg_t, preferred_element_type=F32)
        for hh in range(HEADS_PER_GROUP):
            h = g * HEADS_PER_GROUP + hh
            hs = slice(h * SSM_HEAD_DIM, (h + 1) * SSM_HEAD_DIM)
            seg = jnp.where(tri, a_cs[:, h:h + 1] - a_cs_t[h:h + 1, :], NEG)
            scores = (gram * jnp.exp(seg)).astype(BF16)
            y_diag = jnp.dot(scores, xdt[:, hs].astype(BF16), preferred_element_type=F32)
            st = state_ref[h]
            y_off = jnp.dot(cg, st.astype(BF16), preferred_element_type=F32) * exp_acs[:, hs]
            y_ref[:, hs] = y_diag + y_off
            state_ref[h] = exp_last[:, hs] * st + jnp.dot(bg_t, xw[:, hs].astype(BF16),
                                                          preferred_element_type=F32)

    y = y_ref[...] + dskip_ref[...] * xs
    u = y * _silu(z_ref[...])
    gw = SSM_WIDTH // N_SSM_GROUPS
    parts = []
    for g in range(N_SSM_GROUPS):
        ug = u[:, g * gw:(g + 1) * gw]
        parts.append(ug * lax.rsqrt(jnp.mean(ug * ug, axis=-1, keepdims=True) + EPS))
    y_ref[...] = jnp.concatenate(parts, axis=-1) * gn_ref[...]

    @pl.when(c == pl.num_programs(1) - 1)
    def _():
        st_ref[0] = state_ref[...]


def _head_expand():
    e = np.zeros((LANES, SSM_WIDTH), np.float32)
    for h in range(N_SSM_HEADS):
        e[h, h * SSM_HEAD_DIM:(h + 1) * SSM_HEAD_DIM] = 1.0
    return jnp.asarray(e)


def _pad_lanes(v):
    return jnp.pad(v.astype(F32), (0, LANES - v.shape[0])).reshape(1, LANES)


def _ssd_prompt(xbc, dt_raw, z, conv_w, conv_b, dt_bias, a_log, d_skip, ssd_norm, batch):
    m = xbc.shape[0]
    nc = m // batch // SSD_CHUNK
    L = SSD_CHUNK
    rows = lambda b, c: (b * nc + c, 0)
    y, st = pl.pallas_call(
        _ssd_kernel,
        grid=(batch, nc),
        in_specs=[pl.BlockSpec((L, CONV_DIM), rows),
                  pl.BlockSpec((L, LANES), rows),
                  pl.BlockSpec((L, SSM_WIDTH), rows),
                  _const_spec((CONV_W, CONV_DIM)),
                  _const_spec((1, CONV_DIM)),
                  _const_spec((1, LANES)),
                  _const_spec((1, LANES)),
                  _const_spec((1, SSM_WIDTH)),
                  _const_spec((1, SSM_WIDTH)),
                  _const_spec((LANES, SSM_WIDTH))],
        out_specs=[pl.BlockSpec((L, SSM_WIDTH), rows),
                   pl.BlockSpec((1, N_SSM_HEADS, D_STATE, SSM_HEAD_DIM), lambda b, c: (b, 0, 0, 0))],
        out_shape=[jax.ShapeDtypeStruct((m, SSM_WIDTH), F32),
                   jax.ShapeDtypeStruct((batch, N_SSM_HEADS, D_STATE, SSM_HEAD_DIM), F32)],
        scratch_shapes=[pltpu.VMEM((SUBLANES + L, CONV_DIM), F32),
                        pltpu.VMEM((N_SSM_HEADS, D_STATE, SSM_HEAD_DIM), F32)],
        compiler_params=_cparams(("parallel", "arbitrary")),
        name="ssd_prompt",
    )(xbc, dt_raw, z, conv_w, conv_b.reshape(1, CONV_DIM), _pad_lanes(dt_bias), _pad_lanes(a_log),
      jnp.repeat(d_skip, SSM_HEAD_DIM).reshape(1, SSM_WIDTH), ssd_norm.reshape(1, SSM_WIDTH),
      _head_expand())
    return y, jnp.swapaxes(st, -1, -2)


def _router(xn_bf, wr_ref, br_ref):
    return jnp.dot(xn_bf, wr_ref[...], preferred_element_type=F32) + br_ref[...]


def _outproj_kernel(o0, l0, o1, l1, o2, l2, ssm_ref, x_ref, w_ref, g_ref, wr_ref, br_ref,
                    x1_ref, xn_ref, lg_ref):
    la, lb, lc = l0[...], l1[...], l2[...]
    mx = jnp.maximum(jnp.maximum(la, lb), lc)
    wa, wb, wc = jnp.exp(la - mx), jnp.exp(lb - mx), jnp.exp(lc - mx)
    att = (wa * o0[...] + wb * o1[...] + wc * o2[...]) * (1.0 / (wa + wb + wc))
    y = jnp.dot(att.astype(BF16), w_ref[:ATT_WIDTH, :], preferred_element_type=F32)
    y = y + jnp.dot(ssm_ref[...].astype(BF16), w_ref[ATT_WIDTH:, :], preferred_element_type=F32)
    x1 = x_ref[...] + y
    x1_ref[...] = x1
    xn = _rms(x1, g_ref[...]).astype(BF16)
    xn_ref[...] = xn
    lg_ref[...] = _router(xn, wr_ref, br_ref)


def _outproj_prompt(att_parts, ssm, x2d, w_bf, g, wr, br):
    m, d = x2d.shape
    mix = w_bf.shape[0]
    row = lambda w: pl.BlockSpec((TM_OUT, w), lambda i: (i, 0))
    return pl.pallas_call(
        _outproj_kernel,
        grid=(m // TM_OUT,),
        in_specs=[row(ATT_WIDTH)] * 6 + [row(SSM_WIDTH), row(d),
                                         _const_spec((mix, d)), _const_spec((1, d)),
                                         _const_spec((d, LANES)), _const_spec((1, LANES))],
        out_specs=[row(d), row(d), row(LANES)],
        out_shape=[jax.ShapeDtypeStruct((m, d), F32), jax.ShapeDtypeStruct((m, d), BF16),
                   jax.ShapeDtypeStruct((m, LANES), F32)],
        compiler_params=_cparams(("parallel",)),
        name="outproj_prompt",
    )(*att_parts, ssm, x2d, w_bf, g.reshape(1, d), wr, br)


def _attn_sample_kernel(q_ref, kn_ref, vn_ref, k0_ref, k1_ref, k2_ref, v0_ref, v1_ref, v2_ref,
                        bias_ref, bias0_ref, o_ref):
    q = q_ref[0]
    head_grp = lax.broadcasted_iota(jnp.int32, (N_ATT_HEADS, 1), 0) // KV_REP
    kn = kn_ref[0].astype(BF16).astype(F32)
    vn = vn_ref[0].astype(BF16).astype(F32)
    s_self = jnp.sum(q.astype(F32) * kn, axis=-1, keepdims=True)
    scores, selfs, lses = [], [], []
    for g, kc_ref in enumerate((k0_ref, k1_ref, k2_ref)):
        kk = kc_ref[0].astype(BF16)
        s = jnp.zeros((N_ATT_HEADS, WIN), F32)
        for kvh in range(N_KV_HEADS):
            cs = slice(kvh * HEAD_DIM, (kvh + 1) * HEAD_DIM)
            sk = lax.dot_general(q, kk[:, cs], (((1,), (1,)), ((), ())), preferred_element_type=F32)
            s = jnp.where(head_grp == kvh, sk, s)
        s = s + bias_ref[g]
        s0 = s_self + bias0_ref[g]
        mx = jnp.maximum(jnp.max(s, axis=-1, keepdims=True), s0)
        lse = mx + jnp.log(jnp.sum(jnp.exp(s - mx), axis=-1, keepdims=True) + jnp.exp(s0 - mx))
        scores.append(s); selfs.append(s0); lses.append(lse)
    top = functools.reduce(jnp.maximum, lses)
    es = [jnp.exp(l - top) for l in lses]
    tot = functools.reduce(jnp.add, es)
    o = jnp.zeros((N_ATT_HEADS, HEAD_DIM), F32)
    for s, s0, lse, e, vc_ref in zip(scores, selfs, lses, es, (v0_ref, v1_ref, v2_ref)):
        wgt = e / tot
        p = (jnp.exp(s - lse) * wgt).astype(BF16)
        p0 = (jnp.exp(s0 - lse) * wgt).astype(BF16).astype(F32)
        vv = vc_ref[0].astype(BF16)
        o = o + p0 * vn
        for kvh in range(N_KV_HEADS):
            cs = slice(kvh * HEAD_DIM, (kvh + 1) * HEAD_DIM)
            ok = jnp.dot(p, vv[:, cs], preferred_element_type=F32)
            o = o + jnp.where(head_grp == kvh, ok, 0.0)
    o_ref[0] = o


def _attn_sample(q, k_new, v_new, k_cache, v_cache, bias_s, bias0_s):
    n, w = k_cache.shape[0], k_cache.shape[1]
    assert w % (max(DILATIONS) * WIN) == 0
    tok = lambda b: (b, 0, 0)
    head = pl.BlockSpec((1, N_ATT_HEADS, HEAD_DIM), tok)
    taps = [pl.BlockSpec((1, WIN, KV_WIDTH), functools.partial(lambda nb, b: (b, nb - 1, 0), w // dil // WIN))
            for dil in DILATIONS]
    views = lambda c: [c.reshape(n, w // dil, dil * KV_WIDTH) for dil in DILATIONS]
    return pl.pallas_call(
        _attn_sample_kernel,
        grid=(n,),
        in_specs=[head, head, head] + taps + taps +
                 [_const_spec((len(DILATIONS), N_ATT_HEADS, WIN)),
                  _const_spec((len(DILATIONS), N_ATT_HEADS, 1))],
        out_specs=head,
        out_shape=jax.ShapeDtypeStruct((n, N_ATT_HEADS, HEAD_DIM), F32),
        compiler_params=_cparams(("parallel",)),
        name="attn_sample",
    )(q, jnp.repeat(k_new, KV_REP, axis=1), jnp.repeat(v_new, KV_REP, axis=1), *views(k_cache),
      *views(v_cache), bias_s, bias0_s)


def _conv_sample_kernel(xbc_ref, b0_ref, b1_ref, b2_ref, cw_ref, cb_ref, dt_ref, dtb_ref, alog_ref,
                        e_ref, xa_ref, xdt_ref, decay_ref):
    acc = cb_ref[...] + xbc_ref[...] * cw_ref[CONV_W - 1:CONV_W, :]
    for i, buf in enumerate((b0_ref, b1_ref, b2_ref)):
        acc = acc + buf[...] * cw_ref[i:i + 1, :]
    xa = _silu(acc)
    xa_ref[...] = xa
    dt = _softplus(dt_ref[...] + dtb_ref[...])
    decay_ref[...] = jnp.exp(dt * (-jnp.exp(alog_ref[...])))
    dt_full = jnp.dot(dt, e_ref[...], preferred_element_type=F32, precision=HIGHEST)
    xdt_ref[...] = xa[:, :SSM_WIDTH] * dt_full


def _conv_sample(xbc, conv_buf, conv_w, conv_b, dt_raw, dt_bias, a_log):
    n = xbc.shape[0]
    args = (xbc, conv_buf[:, 0], conv_buf[:, 1], conv_buf[:, 2], conv_w, conv_b.reshape(1, CONV_DIM),
            dt_raw, _pad_lanes(dt_bias), _pad_lanes(a_log),
            _head_expand())
    return pl.pallas_call(
        _conv_sample_kernel,
        grid=(1,),
        in_specs=[_const_spec(a.shape) for a in args],
        out_specs=[_const_spec((n, CONV_DIM)), _const_spec((n, SSM_WIDTH)), _const_spec((n, LANES))],
        out_shape=[jax.ShapeDtypeStruct((n, CONV_DIM), F32), jax.ShapeDtypeStruct((n, SSM_WIDTH), F32),
                   jax.ShapeDtypeStruct((n, LANES), F32)],
        compiler_params=_cparams(("arbitrary",)),
        name="conv_sample",
    )(*args)


def _ssm_sample_kernel(xdt_ref, decay_ref, b_ref, c_ref, h0_ref, hn_ref, y_ref):
    for g in range(N_SSM_GROUPS):
        hs = slice(g * HEADS_PER_GROUP, (g + 1) * HEADS_PER_GROUP)
        hn = decay_ref[0, hs] * h0_ref[0, hs] + xdt_ref[0, hs] * b_ref[0, g]
        hn_ref[0, hs] = hn
        c_row = c_ref[0, g].astype(BF16).astype(F32)
        y_ref[0, hs] = jnp.sum(hn.astype(BF16).astype(F32) * c_row, axis=-1, keepdims=True)


def _ssm_sample(xdt, decay, bmat, cmat, h0):
    n = xdt.shape[0]
    p = SSM_HEAD_DIM
    tok4 = lambda b: (b, 0, 0, 0)
    hn, y = pl.pallas_call(
        _ssm_sample_kernel,
        grid=(n,),
        in_specs=[pl.BlockSpec((1, N_SSM_HEADS, p, 1), tok4),
                  pl.BlockSpec((1, N_SSM_HEADS, 1, 1), tok4),
                  pl.BlockSpec((1, N_SSM_GROUPS, 1, D_STATE), tok4),
                  pl.BlockSpec((1, N_SSM_GROUPS, 1, D_STATE), tok4),
                  pl.BlockSpec((1, N_SSM_HEADS, p, D_STATE), tok4)],
        out_specs=[pl.BlockSpec((1, N_SSM_HEADS, p, D_STATE), tok4),
                   pl.BlockSpec((1, N_SSM_HEADS, p, 1), tok4)],
        out_shape=[jax.ShapeDtypeStruct((n, N_SSM_HEADS, p, D_STATE), F32),
                   jax.ShapeDtypeStruct((n, N_SSM_HEADS, p, 1), F32)],
        compiler_params=_cparams(("parallel",)),
        name="ssm_sample",
    )(xdt.reshape(n, N_SSM_HEADS, p, 1), decay[:, :N_SSM_HEADS].reshape(n, N_SSM_HEADS, 1, 1),
      bmat.reshape(n, N_SSM_GROUPS, 1, D_STATE), cmat.reshape(n, N_SSM_GROUPS, 1, D_STATE), h0)
    return hn, y.reshape(n, SSM_WIDTH)


def _outproj_sample_kernel(att_ref, y_ref, xs_ref, z_ref, dskip_ref, gn_ref, x_ref, w_ref, g_ref,
                           wr_ref, br_ref, x1_ref, xn_ref, lg_ref):
    y = y_ref[...] + dskip_ref[...] * xs_ref[...]
    u = y * _silu(z_ref[...])
    gw = SSM_WIDTH // N_SSM_GROUPS
    parts = []
    for g in range(N_SSM_GROUPS):
        ug = u[:, g * gw:(g + 1) * gw]
        parts.append(ug * lax.rsqrt(jnp.mean(ug * ug, axis=-1, keepdims=True) + EPS)
                     * gn_ref[:, g * gw:(g + 1) * gw])
    mix = jnp.concatenate([att_ref[...]] + parts, axis=-1).astype(BF16)
    x1 = x_ref[...] + jnp.dot(mix, w_ref[...], preferred_element_type=F32)
    x1_ref[...] = x1
    xn = _rms(x1, g_ref[...]).astype(BF16)
    xn_ref[...] = xn
    lg_ref[...] = _router(xn, wr_ref, br_ref)


def _outproj_sample(att, y, xs, z, d_skip, ssd_norm, x2d, w_bf, g, wr, br):
    n, d = x2d.shape
    args = (att, y, xs, z, jnp.repeat(d_skip, SSM_HEAD_DIM).reshape(1, SSM_WIDTH),
            ssd_norm.reshape(1, SSM_WIDTH), x2d, w_bf, g.reshape(1, d), wr, br)
    return pl.pallas_call(
        _outproj_sample_kernel,
        grid=(1,),
        in_specs=[_const_spec(a.shape) for a in args],
        out_specs=[_const_spec((n, d)), _const_spec((n, d)), _const_spec((n, LANES))],
        out_shape=[jax.ShapeDtypeStruct((n, d), F32), jax.ShapeDtypeStruct((n, d), BF16),
                   jax.ShapeDtypeStruct((n, LANES), F32)],
        compiler_params=_cparams(("arbitrary",)),
        name="outproj_sample",
    )(*args)


TT = 256
LROWS = TT * TOP_K + N_EXPERTS * SUBLANES
CHUNK_SIZES = (256, 128, 64, 32, 16, 8)
SUB = 256
RG = 5 * SUB
TF = 256
VMEM_LIMIT_FFN = 60000 * 1024


def _chunk_loop(cnt_ref, loff_ref, dest_ref, tile, fn):
    def per_expert(e, carry):
        idx = tile * N_EXPERTS + e
        n, off, dst = cnt_ref[idx], loff_ref[idx], dest_ref[idx]
        for size in CHUNK_SIZES:
            take = (n & size) != 0

            @pl.when(take)
            def _(off=off, dst=dst, size=size):
                fn(pl.multiple_of(off, SUBLANES), pl.multiple_of(dst, SUBLANES), size)

            step = jnp.where(take, size, 0)
            off, dst = off + step, dst + step
        return carry

    lax.fori_loop(0, N_EXPERTS, per_expert, 0)


def _dispatch_kernel(cnt_ref, loff_ref, dest_ref, xp_ref, xs_ref, lpos_ref, gate_ref, out_hbm, buf, sem):
    t = pl.program_id(0)
    d = xp_ref.shape[1]
    x = jnp.where(t == pl.num_programs(0) - 1, xs_ref[...], xp_ref[...])
    rows = lax.broadcasted_iota(jnp.int32, (LROWS, TT), 0)
    onehot = jnp.zeros((LROWS, TT), F32)
    wcol = jnp.zeros((LROWS, 1), F32)
    for k in range(TOP_K):
        hit = jnp.where(rows == lpos_ref[0, k:k + 1, :], 1.0, 0.0)
        onehot = onehot + hit
        wcol = wcol + jnp.sum(hit * gate_ref[0, k:k + 1, :], axis=-1, keepdims=True)
    buf[:, 0:d] = jnp.dot(onehot.astype(BF16), x, preferred_element_type=F32)
    buf[:, d:d + LANES] = jnp.broadcast_to(wcol, (LROWS, LANES))

    def copy(off, dst, size):
        return pltpu.make_async_copy(buf.at[pl.ds(off, size)], out_hbm.at[pl.ds(dst, size)], sem)

    _chunk_loop(cnt_ref, loff_ref, dest_ref, t, lambda o, g, s: copy(o, g, s).start())
    _chunk_loop(cnt_ref, loff_ref, dest_ref, t, lambda o, g, s: copy(o, g, s).wait())


def _dispatch(tabs, xn_p, xn_s, lpos_t, gate_t, n_rows):
    nt = lpos_t.shape[0]
    d = xn_p.shape[1]
    last_p = xn_p.shape[0] // TT - 1
    grid_spec = pltpu.PrefetchScalarGridSpec(
        num_scalar_prefetch=3,
        grid=(nt,),
        in_specs=[pl.BlockSpec((TT, d), lambda t, *_: (jnp.minimum(t, last_p), 0)),
                  pl.BlockSpec((TT, d), lambda t, *_: (0, 0)),
                  pl.BlockSpec((1, TOP_K, TT), lambda t, *_: (t, 0, 0)),
                  pl.BlockSpec((1, TOP_K, TT), lambda t, *_: (t, 0, 0))],
        out_specs=pl.BlockSpec(memory_space=pl.ANY),
        scratch_shapes=[pltpu.VMEM((LROWS, d + LANES), F32), pltpu.SemaphoreType.DMA(())],
    )
    return pl.pallas_call(
        _dispatch_kernel,
        grid_spec=grid_spec,
        out_shape=jax.ShapeDtypeStruct((n_rows, d + LANES), F32),
        compiler_params=_cparams(("arbitrary",)),
        name="moe_dispatch",
    )(*tabs, xn_p, xn_s, lpos_t, gate_t)


def _ffn_kernel(ge_ref, gs_ref, gn_ref, ng_ref, xs_hbm, wgu_hbm, wd_hbm, bgu_ref, bdn_ref, out_hbm,
                xbuf, acc, ostage, wg_st, wu_st, wd_st, wg_bf, wu_bf, wd_bf, sem_w, sem_x, sem_o):
    d = acc.shape[1]
    d_ff = wd_hbm.shape[1]
    nf = d_ff // TF
    n_groups = ng_ref[0]
    total = n_groups * nf

    def w_copies(s, slot):
        g = s // nf
        f = s - g * nf
        e = ge_ref[g]
        c0 = pl.multiple_of(f * TF, TF)
        return (pltpu.make_async_copy(wgu_hbm.at[e, :, pl.ds(c0, TF)], wg_st.at[slot], sem_w.at[slot, 0]),
                pltpu.make_async_copy(wgu_hbm.at[e, :, pl.ds(d_ff + c0, TF)], wu_st.at[slot], sem_w.at[slot, 1]),
                pltpu.make_async_copy(wd_hbm.at[e, pl.ds(c0, TF), :], wd_st.at[slot], sem_w.at[slot, 2]))

    def x_copy(g, j):
        r0 = pl.multiple_of(j * SUB, SUB)
        return pltpu.make_async_copy(xs_hbm.at[pl.ds(pl.multiple_of(gs_ref[g] + r0, SUBLANES), SUB)],
                                     xbuf.at[g % 2, pl.ds(r0, SUB)], sem_x)

    def o_copy(g, j):
        r0 = pl.multiple_of(j * SUB, SUB)
        return pltpu.make_async_copy(ostage.at[j % 2],
                                     out_hbm.at[pl.ds(pl.multiple_of(gs_ref[g] + r0, SUBLANES), SUB)],
                                     sem_o.at[j % 2])

    def loop(n, fn):
        lax.fori_loop(0, n, lambda j, c: (fn(j), c)[1], 0)

    def drain_stores(g):
        n = gn_ref[g]
        for back in (1, 2):
            @pl.when(n >= back)
            def _():
                o_copy(g, n - back).wait()

    @pl.when(total > 0)
    def _():
        for c in w_copies(0, 0):
            c.start()
        loop(gn_ref[0], lambda j: x_copy(0, j).start())

    def item(s, carry):
        slot = s % 2
        g = s // nf
        f = s - g * nf
        e = ge_ref[g]
        nsub = gn_ref[g]
        xg = xbuf.at[g % 2]

        @pl.when(s + 1 < total)
        def _():
            for c in w_copies(s + 1, 1 - slot):
                c.start()

        @pl.when(f == 0)
        def _():
            loop(nsub, lambda j: x_copy(g, j).wait())

        @pl.when(jnp.logical_and(f == 1, g + 1 < n_groups))
        def _():
            loop(gn_ref[g + 1], lambda j: x_copy(g + 1, j).start())

        for c in w_copies(s, slot):
            c.wait()
        wg_bf[...] = wg_st[slot].astype(BF16)
        wu_bf[...] = wu_st[slot].astype(BF16)
        wd_bf[...] = wd_st[slot].astype(BF16)
        bg = bgu_ref[pl.ds(e * 2 * nf + f, 1), :]
        bu = bgu_ref[pl.ds(e * 2 * nf + nf + f, 1), :]

        def sub_block(j, phase):
            rs = pl.ds(pl.multiple_of(j * SUB, SUB), SUB)
            x = xg[rs, 0:d].astype(BF16)
            hg = jnp.dot(x, wg_bf[...], preferred_element_type=F32) + bg
            hu = jnp.dot(x, wu_bf[...], preferred_element_type=F32) + bu
            gg = jnp.minimum(hg, SWIGLU_LIMIT)
            uu = jnp.clip(hu, -SWIGLU_LIMIT, SWIGLU_LIMIT)
            act = gg * (1.0 / (1.0 + jnp.exp(-SWIGLU_ALPHA * gg))) * (uu + 1.0)
            part = jnp.dot(act.astype(BF16), wd_bf[...], preferred_element_type=F32)
            if phase == "first":
                acc[rs, :] = part + bdn_ref[pl.ds(e, 1), :]
            elif phase == "middle":
                acc[rs, :] += part
            else:
                @pl.when(j >= 2)
                def _():
                    o_copy(g, j - 2).wait()

                ostage[j % 2] = (acc[rs, :] + part) * xg[rs, d:d + 1]
                o_copy(g, j).start()

        def all_sub_blocks(phase):
            def pair(p):
                sub_block(2 * p, phase)
                sub_block(2 * p + 1, phase)

            loop(nsub // 2, pair)

            @pl.when(nsub % 2 == 1)
            def _():
                sub_block(nsub - 1, phase)

        @pl.when(f == 0)
        def _():
            all_sub_blocks("first")

        @pl.when(jnp.logical_and(f > 0, f < nf - 1))
        def _():
            all_sub_blocks("middle")

        @pl.when(f == nf - 1)
        def _():
            @pl.when(g > 0)
            def _():
                drain_stores(g - 1)

            all_sub_blocks("last")

        return carry

    lax.fori_loop(0, total, item, 0)

    @pl.when(total > 0)
    def _():
        drain_stores(n_groups - 1)


def _moe_ffn(groups, x_sorted, w_gate_up, b_gu, w_down, b_dn):
    n_rows = x_sorted.shape[0]
    _, d_ff, d = w_down.shape
    nf = d_ff // TF
    assert nf >= 2
    bgu2 = b_gu.reshape(N_EXPERTS * 2 * nf, TF)
    grid_spec = pltpu.PrefetchScalarGridSpec(
        num_scalar_prefetch=4,
        grid=(1,),
        in_specs=[pl.BlockSpec(memory_space=pl.ANY), pl.BlockSpec(memory_space=pl.ANY),
                  pl.BlockSpec(memory_space=pl.ANY),
                  pl.BlockSpec(bgu2.shape, lambda i, *_: (0, 0), pipeline_mode=pl.Buffered(1)),
                  pl.BlockSpec(b_dn.shape, lambda i, *_: (0, 0), pipeline_mode=pl.Buffered(1))],
        out_specs=pl.BlockSpec(memory_space=pl.ANY),
        scratch_shapes=[pltpu.VMEM((2, RG, d + LANES), F32), pltpu.VMEM((RG, d), F32),
                        pltpu.VMEM((2, SUB, d), F32),
                        pltpu.VMEM((2, d, TF), F32), pltpu.VMEM((2, d, TF), F32), pltpu.VMEM((2, TF, d), F32),
                        pltpu.VMEM((d, TF), BF16), pltpu.VMEM((d, TF), BF16), pltpu.VMEM((TF, d), BF16),
                        pltpu.SemaphoreType.DMA((2, 3)), pltpu.SemaphoreType.DMA(()),
                        pltpu.SemaphoreType.DMA((2,))],
    )
    return pl.pallas_call(
        _ffn_kernel,
        grid_spec=grid_spec,
        out_shape=jax.ShapeDtypeStruct((n_rows, d), F32),
        compiler_params=pltpu.CompilerParams(dimension_semantics=("arbitrary",),
                                             vmem_limit_bytes=VMEM_LIMIT_FFN),
        name="moe_ffn",
    )(*groups, x_sorted, w_gate_up, w_down, bgu2, b_dn)


def _route(logits, m_pad):
    m = logits.shape[0]
    nt = m_pad // TT
    top_v, top_i = lax.top_k(logits, TOP_K)
    gate = jnp.pad(jax.nn.softmax(top_v, axis=-1), ((0, m_pad - m), (0, 0)))
    top_i = jnp.pad(top_i.astype(jnp.int32), ((0, m_pad - m), (0, 0)), constant_values=-1)
    chosen = (top_i[:, :, None] == jnp.arange(N_EXPERTS, dtype=jnp.int32)).astype(jnp.int32)
    tiles = chosen.sum(axis=1).reshape(nt, TT, N_EXPERTS)
    cnt = (tiles.sum(axis=1) + SUBLANES - 1) // SUBLANES * SUBLANES
    loff = jnp.cumsum(cnt, axis=1) - cnt
    seg = cnt.sum(axis=0)
    seg_start = jnp.cumsum(seg) - seg
    dest = seg_start[None, :] + jnp.cumsum(cnt, axis=0) - cnt
    rank = jnp.cumsum(tiles, axis=1) - tiles
    lpos_all = (loff[:, None, :] + rank).reshape(m_pad, N_EXPERTS)
    lpos = (lpos_all[:, None, :] * chosen).sum(axis=-1)
    lpos = jnp.where(top_i >= 0, lpos, -1).astype(jnp.int32)
    n_rows_bound = nt * LROWS
    n_grp_max = n_rows_bound // RG + N_EXPERTS
    grp = (seg + RG - 1) // RG
    grp_end = jnp.cumsum(grp)
    gi = jnp.arange(n_grp_max, dtype=jnp.int32)
    g_exp = jnp.minimum(jnp.searchsorted(grp_end, gi, side='right'), N_EXPERTS - 1).astype(jnp.int32)
    within = gi - (grp_end[g_exp] - grp[g_exp])
    g_start = (seg_start[g_exp] + within * RG).astype(jnp.int32)
    g_rows = jnp.clip(seg[g_exp] - within * RG, 0, RG)
    g_nsub = jnp.where(gi < grp_end[-1], (g_rows + SUB - 1) // SUB, 0).astype(jnp.int32)
    groups = (g_exp, g_start, g_nsub, grp_end[-1:].astype(jnp.int32))
    tabs = tuple(a.reshape(-1).astype(jnp.int32) for a in (cnt, loff, dest))
    return tabs, lpos, gate, groups, n_rows_bound + SUB


def _combine_kernel(cnt_ref, loff_ref, dest_ref, src_hbm, lpos_ref, x1p_ref, x1s_ref, g_ref,
                    yp_ref, ys_ref, buf, sem):
    t = pl.program_id(0)
    last = pl.num_programs(0) - 1

    @pl.when(t == 0)
    def _():
        buf[...] = jnp.zeros_like(buf)

    def copy(off, src, size):
        return pltpu.make_async_copy(src_hbm.at[pl.ds(src, size)], buf.at[pl.ds(off, size)], sem)

    _chunk_loop(cnt_ref, loff_ref, dest_ref, t, lambda o, g, s: copy(o, g, s).start())
    _chunk_loop(cnt_ref, loff_ref, dest_ref, t, lambda o, g, s: copy(o, g, s).wait())
    cols = lax.broadcasted_iota(jnp.int32, (TT, LROWS), 1)
    sel = jnp.zeros((TT, LROWS), F32)
    for k in range(TOP_K):
        sel = sel + jnp.where(cols == lpos_ref[:, k:k + 1], 1.0, 0.0)
    sel = sel.astype(BF16)
    rows = buf[...]
    hi = rows.astype(BF16)
    lo = (rows - hi.astype(F32)).astype(BF16)
    f = jnp.dot(sel, hi, preferred_element_type=F32) + jnp.dot(sel, lo, preferred_element_type=F32)

    @pl.when(t < last)
    def _():
        yp_ref[...] = _rms(x1p_ref[...] + f, g_ref[...])

    @pl.when(t == last)
    def _():
        ys_ref[...] = _rms(x1s_ref[...] + f, g_ref[...])


def _combine(tabs, out_sorted, lpos, x1p, x1s, g):
    d = x1p.shape[1]
    nt = lpos.shape[0] // TT
    last_p = x1p.shape[0] // TT - 1
    prompt = pl.BlockSpec((TT, d), lambda t, *_: (jnp.minimum(t, last_p), 0))
    sample = pl.BlockSpec((TT, d), lambda t, *_: (0, 0))
    grid_spec = pltpu.PrefetchScalarGridSpec(
        num_scalar_prefetch=3,
        grid=(nt,),
        in_specs=[pl.BlockSpec(memory_space=pl.ANY),
                  pl.BlockSpec((TT, TOP_K), lambda t, *_: (t, 0)),
                  prompt, sample,
                  pl.BlockSpec((1, d), lambda t, *_: (0, 0))],
        out_specs=[prompt, sample],
        scratch_shapes=[pltpu.VMEM((LROWS, d), F32), pltpu.SemaphoreType.DMA(())],
    )
    return pl.pallas_call(
        _combine_kernel,
        grid_spec=grid_spec,
        out_shape=[jax.ShapeDtypeStruct(x1p.shape, F32), jax.ShapeDtypeStruct((TT, d), F32)],
        compiler_params=_cparams(("arbitrary",)),
        name="moe_combine",
    )(*tabs, out_sorted, lpos, x1p, x1s, g.reshape(1, d))


def _t5_bucket(dist):
    max_exact = N_BUCKETS // 2
    dd = dist.astype(F32)
    large = max_exact + (jnp.log(jnp.maximum(dd, 1.0) / max_exact)
                         / math.log(BUCKET_MAX_DIST / max_exact) * (N_BUCKETS - max_exact)).astype(jnp.int32)
    large = jnp.minimum(large, N_BUCKETS - 1)
    return jnp.where(dist < max_exact, dist, large)


def _bias_tables(rel_bias):
    dist = jnp.asarray(np.arange(N_TAPS)[None, :] * np.array(DILATIONS)[:, None], jnp.int32)
    bias = jnp.transpose(rel_bias[_t5_bucket(dist)], (2, 0, 1)).astype(F32)
    by_branch = jnp.transpose(bias, (1, 0, 2))
    g, h = by_branch.shape[:2]
    row = jnp.concatenate([by_branch[:, :, ::-1], jnp.full((g, h, WIN), NEG, F32)], axis=-1)
    flat = jnp.broadcast_to(row[:, :, None, :], (g, h, WIN, 2 * WIN + 1)).reshape(g, h, -1)
    band = flat[:, :, :WIN * 2 * WIN].reshape(g, h, WIN, 2 * WIN)
    return band, by_branch[:, :, :0:-1], by_branch[:, :, 0:1]


def kernel(x_prompt, x_sample, cache_k_win, cache_v_win, state_conv, state_ssm, rel_bias, attn_norm, w_in, conv_w, conv_b, dt_bias, a_log, d_skip, ssd_norm, w_out, ffn_norm, w_router, b_router, w_gate_up, b_gate_up, w_down, b_down, final_norm):
    bp, tp, d = x_prompt.shape
    bs, ts, _ = x_sample.shape
    depth = w_in.shape[0]
    assert depth == 1 and ts == 1 and tp % (max(DILATIONS) * WIN) == 0
    keep = min(max(DILATIONS) * WIN, tp)
    band, samp, samp0 = _bias_tables(rel_bias)
    l = 0

    xp = x_prompt.reshape(bp * tp, d)
    xs = x_sample.reshape(bs * ts, d)
    w_in_bf = jnp.pad(w_in[l], ((0, 0), (0, IN_PROJ_PAD - IN_PROJ))).astype(BF16)
    w_out_bf = w_out[l].astype(BF16)
    wr = jnp.pad(w_router[l], ((0, 0), (0, LANES - N_EXPERTS))).astype(BF16)
    br = jnp.pad(b_router[l], (0, LANES - N_EXPERTS), constant_values=NEG).reshape(1, LANES)

    q, k, v, z, xbc, dt_raw = _inproj(xp, attn_norm[l], w_in_bf, TM_PROJ)
    q3 = q.reshape(bp, tp, ATT_WIDTH)
    k3 = k.reshape(bp, tp, KV_WIDTH)
    v3 = v.reshape(bp, tp, KV_WIDTH)
    att_parts = []
    for gi, dil in enumerate(DILATIONS):
        att_parts.extend(_attn_branch(q3, k3, v3, band[gi], dil))
    ssm, st_p = _ssd_prompt(xbc, dt_raw, z, conv_w[l], conv_b[l], dt_bias[l], a_log[l], d_skip[l],
                            ssd_norm[l], bp)
    x1p, xnp_, lgp = _outproj_prompt(att_parts, ssm, xp, w_out_bf, ffn_norm[l], wr, br)
    k_win_p = k3[:, tp - keep:].reshape(1, bp, keep, N_KV_HEADS, HEAD_DIM)
    v_win_p = v3[:, tp - keep:].reshape(1, bp, keep, N_KV_HEADS, HEAD_DIM)
    conv_p = xbc.reshape(bp, tp, CONV_DIM)[:, tp - (CONV_W - 1):][None]

    q_s, k_s, v_s, z_s, xbc_s, dt_s = _inproj(xs, attn_norm[l], w_in_bf, bs * ts)
    q_s = q_s.reshape(bs, N_ATT_HEADS, HEAD_DIM)
    k_s = k_s.reshape(bs, N_KV_HEADS, HEAD_DIM)
    v_s = v_s.reshape(bs, N_KV_HEADS, HEAD_DIM)
    wbuf = cache_k_win.shape[2]
    att_s = _attn_sample(q_s, k_s, v_s, cache_k_win[l].reshape(bs, wbuf, KV_WIDTH),
                         cache_v_win[l].reshape(bs, wbuf, KV_WIDTH), samp, samp0)
    xa_s, xdt_s, decay_s = _conv_sample(xbc_s, state_conv[l], conv_w[l], conv_b[l], dt_s, dt_bias[l],
                                        a_log[l])
    nbc = N_SSM_GROUPS * D_STATE
    h_s, y_s = _ssm_sample(xdt_s, decay_s, xa_s[:, SSM_WIDTH:SSM_WIDTH + nbc], xa_s[:, SSM_WIDTH + nbc:],
                           state_ssm[l])
    x1s, xns, lgs = _outproj_sample(att_s.reshape(bs, ATT_WIDTH), y_s, xa_s[:, :SSM_WIDTH], z_s,
                                    d_skip[l], ssd_norm[l], xs, w_out_bf, ffn_norm[l], wr, br)
    conv_s = jnp.concatenate([state_conv[l][:, 1:], xbc_s[:, None]], axis=1)[None]

    n_s = bs * ts
    assert (bp * tp) % TT == 0 and n_s <= TT
    m_pad = bp * tp + TT
    logits = jnp.concatenate([lgp[:, :N_EXPERTS], lgs[:, :N_EXPERTS]], axis=0)
    tabs, lpos, gate, groups, n_rows = _route(logits, m_pad)
    by_tile = lambda a: jnp.transpose(a.reshape(m_pad // TT, TT, TOP_K), (0, 2, 1))
    pad_rows = lambda a: jnp.pad(a, ((0, TT - n_s), (0, 0)))
    x_sorted = _dispatch(tabs, xnp_, pad_rows(xns), by_tile(lpos), by_tile(gate), n_rows)
    out_sorted = _moe_ffn(groups, x_sorted, w_gate_up[l], b_gate_up[l], w_down[l], b_down[l])
    y_p, y_s_out = _combine(tabs, out_sorted, lpos, x1p, pad_rows(x1s), final_norm)
    y_s_out = y_s_out[:n_s]

    return (y_p.reshape(bp, tp, d), y_s_out.reshape(bs, ts, d), k_win_p, v_win_p, conv_p, st_p[None],
            k_s.reshape(1, bs, ts, N_KV_HEADS, HEAD_DIM), v_s.reshape(1, bs, ts, N_KV_HEADS, HEAD_DIM),
            conv_s, h_s[None])
```

```python
import functools
import math

import jax
import jax.numpy as jnp
import numpy as np
from jax import lax
from jax.experimental import pallas as pl
from jax.experimental.pallas import tpu as pltpu

F32 = jnp.float32
BF16 = jnp.bfloat16
HIGHEST = lax.Precision.HIGHEST

LANES = 128
SUBLANES = 8
VMEM_LIMIT = 56 * 1024 * 1024

HEAD_DIM = 64
N_ATT_HEADS = 16
N_KV_HEADS = 4
KV_REP = N_ATT_HEADS // N_KV_HEADS
ATT_WIDTH = N_ATT_HEADS * HEAD_DIM
KV_WIDTH = N_KV_HEADS * HEAD_DIM
DILATIONS = (1, 4, 16)
N_TAPS = 129
WIN = N_TAPS - 1
ATT_SCALE = HEAD_DIM ** -0.5
N_BUCKETS = 32
BUCKET_MAX_DIST = 2048
SSM_HEAD_DIM = 64
N_SSM_HEADS = 16
SSM_WIDTH = N_SSM_HEADS * SSM_HEAD_DIM
N_SSM_GROUPS = 2
HEADS_PER_GROUP = N_SSM_HEADS // N_SSM_GROUPS
D_STATE = 128
CONV_W = 4
CONV_DIM = SSM_WIDTH + 2 * N_SSM_GROUPS * D_STATE
SSD_CHUNK = 128
N_EXPERTS = 32
TOP_K = 4
SWIGLU_LIMIT = 7.0
SWIGLU_ALPHA = 1.702
EPS = 1e-5
NEG = -1e30

Q0, K0, V0, Z0, X0, DT0 = 0, 1024, 1280, 1536, 2560, 4096
IN_PROJ = DT0 + N_SSM_HEADS
IN_PROJ_PAD = DT0 + LANES

TM_PROJ = 512
TM_OUT = 256
TM_MOE = 512
TF_MOE = 512
TM_FIN = 256


def _cparams(sem):
    return pltpu.CompilerParams(dimension_semantics=sem, vmem_limit_bytes=VMEM_LIMIT)


def _const_spec(shape):
    nd = len(shape)
    return pl.BlockSpec(shape, lambda *_: (0,) * nd, pipeline_mode=pl.Buffered(1))


def _rms(x, g):
    ms = jnp.mean(x * x, axis=-1, keepdims=True)
    return x * lax.rsqrt(ms + EPS) * g


def _silu(x):
    return x * (1.0 / (1.0 + jnp.exp(-x)))


def _inproj_kernel(x_ref, g_ref, w_ref, q_ref, k_ref, v_ref, z_ref, xbc_ref, dt_ref):
    xn = _rms(x_ref[...], g_ref[...]).astype(BF16)

    def mm(lo, hi):
        return jnp.dot(xn, w_ref[:, lo:hi], preferred_element_type=F32)

    q_ref[...] = (mm(Q0, K0) * ATT_SCALE).astype(BF16)
    k_ref[...] = mm(K0, V0)
    v_ref[...] = mm(V0, Z0)
    z_ref[...] = mm(Z0, X0)
    xbc_ref[...] = mm(X0, DT0)
    dt_ref[...] = mm(DT0, IN_PROJ_PAD)


def _inproj(x2d, g, w_bf, tm):
    m, d = x2d.shape
    widths = (ATT_WIDTH, KV_WIDTH, KV_WIDTH, SSM_WIDTH, CONV_DIM, LANES)
    dtypes = (BF16, F32, F32, F32, F32, F32)
    return pl.pallas_call(
        _inproj_kernel,
        grid=(m // tm,),
        in_specs=[pl.BlockSpec((tm, d), lambda i: (i, 0)),
                  _const_spec((1, d)),
                  _const_spec((d, IN_PROJ_PAD))],
        out_specs=[pl.BlockSpec((tm, w), lambda i: (i, 0)) for w in widths],
        out_shape=[jax.ShapeDtypeStruct((m, w), t) for w, t in zip(widths, dtypes)],
        compiler_params=_cparams(("parallel",)),
        name=f"inproj_{tm}",
    )(x2d, g.reshape(1, d), w_bf)


def _attn_kernel(q_ref, kp_ref, kc_ref, vp_ref, vc_ref, bias_ref, o_ref, lse_ref):
    first = pl.program_id(2) == 0
    lane = lax.broadcasted_iota(jnp.int32, (1, 2 * WIN), 1)
    prev_mask = jnp.where(jnp.logical_and(first, lane < WIN), NEG, 0.0)
    for kvh in range(N_KV_HEADS):
        cs = slice(kvh * HEAD_DIM, (kvh + 1) * HEAD_DIM)
        kw = jnp.concatenate([kp_ref[0, :, cs], kc_ref[0, :, cs]], axis=0).astype(BF16)
        vw = jnp.concatenate([vp_ref[0, :, cs], vc_ref[0, :, cs]], axis=0).astype(BF16)
        for pair in range(KV_REP // 2):
            outs, lses = [], []
            for r in range(2):
                h = kvh * KV_REP + pair * 2 + r
                qh = q_ref[0, :, h * HEAD_DIM:(h + 1) * HEAD_DIM]
                s = lax.dot_general(qh, kw, (((1,), (1,)), ((), ())), preferred_element_type=F32)
                s = s + bias_ref[h] + prev_mask
                mx = jnp.max(s, axis=-1, keepdims=True)
                p = jnp.exp(s - mx)
                l = jnp.sum(p, axis=-1, keepdims=True)
                o = jnp.dot(p.astype(BF16), vw, preferred_element_type=F32)
                outs.append(o * (1.0 / l))
                lses.append(jnp.broadcast_to(mx + jnp.log(l), (WIN, HEAD_DIM)))
            h0 = kvh * KV_REP + pair * 2
            o_ref[0, :, h0 * HEAD_DIM:(h0 + 2) * HEAD_DIM] = jnp.concatenate(outs, axis=-1)
            lse_ref[0, :, h0 * HEAD_DIM:(h0 + 2) * HEAD_DIM] = jnp.concatenate(lses, axis=-1)


def _attn_branch(q, k, v, bias_mat, dil):
    b, s, _ = q.shape
    sub = s // dil
    nb = sub // WIN
    qv = q.reshape(b, sub, dil * ATT_WIDTH)
    kv_ = k.reshape(b, sub, dil * KV_WIDTH)
    vv = v.reshape(b, sub, dil * KV_WIDTH)
    cur = lambda bb, r, i: (bb, i, r)
    prev = lambda bb, r, i: (bb, jnp.maximum(i - 1, 0), r)
    o, lse = pl.pallas_call(
        _attn_kernel,
        grid=(b, dil, nb),
        in_specs=[pl.BlockSpec((1, WIN, ATT_WIDTH), cur),
                  pl.BlockSpec((1, WIN, KV_WIDTH), prev),
                  pl.BlockSpec((1, WIN, KV_WIDTH), cur),
                  pl.BlockSpec((1, WIN, KV_WIDTH), prev),
                  pl.BlockSpec((1, WIN, KV_WIDTH), cur),
                  _const_spec((N_ATT_HEADS, WIN, 2 * WIN))],
        out_specs=[pl.BlockSpec((1, WIN, ATT_WIDTH), cur),
                   pl.BlockSpec((1, WIN, ATT_WIDTH), cur)],
        out_shape=[jax.ShapeDtypeStruct((b, sub, dil * ATT_WIDTH), F32),
                   jax.ShapeDtypeStruct((b, sub, dil * ATT_WIDTH), F32)],
        compiler_params=_cparams(("parallel", "parallel", "arbitrary")),
        name=f"attn_dil{dil}",
    )(qv, kv_, kv_, vv, vv, bias_mat)
    return o.reshape(b * s, ATT_WIDTH), lse.reshape(b * s, ATT_WIDTH)


def _softplus(x):
    return jnp.maximum(x, 0.0) + jnp.log(1.0 + jnp.exp(-jnp.abs(x)))


def _ssd_kernel(xbc_ref, dt_ref, z_ref, cw_ref, cb_ref, dtb_ref, alog_ref, dskip_ref, gn_ref, e_ref,
                y_ref, st_ref, ext_ref, state_ref):
    c = pl.program_id(1)
    L = SSD_CHUNK

    @pl.when(c == 0)
    def _():
        ext_ref[0:SUBLANES, :] = jnp.zeros((SUBLANES, CONV_DIM), F32)
        state_ref[...] = jnp.zeros_like(state_ref)

    ext_ref[SUBLANES:SUBLANES + L, :] = xbc_ref[...]
    acc = cb_ref[...] + ext_ref[SUBLANES:SUBLANES + L, :] * cw_ref[CONV_W - 1:CONV_W, :]
    for i in range(CONV_W - 1):
        off = SUBLANES - (CONV_W - 1) + i
        acc = acc + ext_ref[off:off + L, :] * cw_ref[i:i + 1, :]
    ext_ref[0:SUBLANES, :] = ext_ref[L:L + SUBLANES, :]
    xa = _silu(acc)

    dt = _softplus(dt_ref[...] + dtb_ref[...])
    da = dt * (-jnp.exp(alog_ref[...]))
    row = lax.broadcasted_iota(jnp.int32, (L, L), 0)
    col = lax.broadcasted_iota(jnp.int32, (L, L), 1)
    tri = row >= col
    a_cs = jnp.dot(tri.astype(F32), da, preferred_element_type=F32, precision=HIGHEST)
    a_cs_t = a_cs.T
    expand = e_ref[...]
    acs_full = jnp.dot(a_cs, expand, preferred_element_type=F32, precision=HIGHEST)
    dt_full = jnp.dot(dt, expand, preferred_element_type=F32, precision=HIGHEST)
    exp_acs = jnp.exp(acs_full)
    a_last = acs_full[L - 1:L, :]
    exp_last = exp_acs[L - 1:L, :]
    xs = xa[:, :SSM_WIDTH]
    xdt = xs * dt_full
    xw = xdt * jnp.exp(a_last - acs_full)

    for g in range(N_SSM_GROUPS):
        b0 = SSM_WIDTH + g * D_STATE
        c0 = SSM_WIDTH + N_SSM_GROUPS * D_STATE + g * D_STATE
        bg_t = xa[:, b0:b0 + D_STATE].T.astype(BF16)
        cg = xa[:, c0:c0 + D_STATE].astype(BF16)
        gram = jnp.dot(cg, bg_t, preferred_element_type=F32)
        for hh in range(HEADS_PER_GROUP):
            h = g * HEADS_PER_GROUP + hh
            hs = slice(h * SSM_HEAD_DIM, (h + 1) * SSM_HEAD_DIM)
            seg = jnp.where(tri, a_cs[:, h:h + 1] - a_cs_t[h:h + 1, :], NEG)
            scores = (gram * jnp.exp(seg)).astype(BF16)
            y_diag = jnp.dot(scores, xdt[:, hs].astype(BF16), preferred_element_type=F32)
            st = state_ref[h]
            y_off = jnp.dot(cg, st.astype(BF16), preferred_element_type=F32) * exp_acs[:, hs]
            y_ref[:, hs] = y_diag + y_off
            state_ref[h] = exp_last[:, hs] * st + jnp.dot(bg_t, xw[:, hs].astype(BF16),
                                                          preferred_element_type=F32)

    y = y_ref[...] + dskip_ref[...] * xs
    u = y * _silu(z_ref[...])
    gw = SSM_WIDTH // N_SSM_GROUPS
    parts = []
    for g in range(N_SSM_GROUPS):
        ug = u[:, g * gw:(g + 1) * gw]
        parts.append(ug * lax.rsqrt(jnp.mean(ug * ug, axis=-1, keepdims=True) + EPS))
    y_ref[...] = jnp.concatenate(parts, axis=-1) * gn_ref[...]

    @pl.when(c == pl.num_programs(1) - 1)
    def _():
        st_ref[0] = state_ref[...]


def _head_expand():
    e = np.zeros((LANES, SSM_WIDTH), np.float32)
    for h in range(N_SSM_HEADS):
        e[h, h * SSM_HEAD_DIM:(h + 1) * SSM_HEAD_DIM] = 1.0
    return jnp.asarray(e)


def _pad_lanes(v):
    return jnp.pad(v.astype(F32), (0, LANES - v.shape[0])).reshape(1, LANES)


def _ssd_prompt(xbc, dt_raw, z, conv_w, conv_b, dt_bias, a_log, d_skip, ssd_norm, batch):
    m = xbc.shape[0]
    nc = m // batch // SSD_CHUNK
    L = SSD_CHUNK
    rows = lambda b, c: (b * nc + c, 0)
    y, st = pl.pallas_call(
        _ssd_kernel,
        grid=(batch, nc),
        in_specs=[pl.BlockSpec((L, CONV_DIM), rows),
                  pl.BlockSpec((L, LANES), rows),
                  pl.BlockSpec((L, SSM_WIDTH), rows),
                  _const_spec((CONV_W, CONV_DIM)),
                  _const_spec((1, CONV_DIM)),
                  _const_spec((1, LANES)),
                  _const_spec((1, LANES)),
                  _const_spec((1, SSM_WIDTH)),
                  _const_spec((1, SSM_WIDTH)),
                  _const_spec((LANES, SSM_WIDTH))],
        out_specs=[pl.BlockSpec((L, SSM_WIDTH), rows),
                   pl.BlockSpec((1, N_SSM_HEADS, D_STATE, SSM_HEAD_DIM), lambda b, c: (b, 0, 0, 0))],
        out_shape=[jax.ShapeDtypeStruct((m, SSM_WIDTH), F32),
                   jax.ShapeDtypeStruct((batch, N_SSM_HEADS, D_STATE, SSM_HEAD_DIM), F32)],
        scratch_shapes=[pltpu.VMEM((SUBLANES + L, CONV_DIM), F32),
                        pltpu.VMEM((N_SSM_HEADS, D_STATE, SSM_HEAD_DIM), F32)],
        compiler_params=_cparams(("parallel", "arbitrary")),
        name="ssd_prompt",
    )(xbc, dt_raw, z, conv_w, conv_b.reshape(1, CONV_DIM), _pad_lanes(dt_bias), _pad_lanes(a_log),
      jnp.repeat(d_skip, SSM_HEAD_DIM).reshape(1, SSM_WIDTH), ssd_norm.reshape(1, SSM_WIDTH),
      _head_expand())
    return y, jnp.swapaxes(st, -1, -2)


def _router(xn_bf, wr_ref, br_ref):
    return jnp.dot(xn_bf, wr_ref[...], preferred_element_type=F32) + br_ref[...]


def _outproj_kernel(o0, l0, o1, l1, o2, l2, ssm_ref, x_ref, w_ref, g_ref, wr_ref, br_ref,
                    x1_ref, xn_ref, lg_ref):
    la, lb, lc = l0[...], l1[...], l2[...]
    mx = jnp.maximum(jnp.maximum(la, lb), lc)
    wa, wb, wc = jnp.exp(la - mx), jnp.exp(lb - mx), jnp.exp(lc - mx)
    att = (wa * o0[...] + wb * o1[...] + wc * o2[...]) * (1.0 / (wa + wb + wc))
    y = jnp.dot(att.astype(BF16), w_ref[:ATT_WIDTH, :], preferred_element_type=F32)
    y = y + jnp.dot(ssm_ref[...].astype(BF16), w_ref[ATT_WIDTH:, :], preferred_element_type=F32)
    x1 = x_ref[...] + y
    x1_ref[...] = x1
    xn = _rms(x1, g_ref[...]).astype(BF16)
    xn_ref[...] = xn
    lg_ref[...] = _router(xn, wr_ref, br_ref)


def _outproj_prompt(att_parts, ssm, x2d, w_bf, g, wr, br):
    m, d = x2d.shape
    mix = w_bf.shape[0]
    row = lambda w: pl.BlockSpec((TM_OUT, w), lambda i: (i, 0))
    return pl.pallas_call(
        _outproj_kernel,
        grid=(m // TM_OUT,),
        in_specs=[row(ATT_WIDTH)] * 6 + [row(SSM_WIDTH), row(d),
                                         _const_spec((mix, d)), _const_spec((1, d)),
                                         _const_spec((d, LANES)), _const_spec((1, LANES))],
        out_specs=[row(d), row(d), row(LANES)],
        out_shape=[jax.ShapeDtypeStruct((m, d), F32), jax.ShapeDtypeStruct((m, d), BF16),
                   jax.ShapeDtypeStruct((m, LANES), F32)],
        compiler_params=_cparams(("parallel",)),
        name="outproj_prompt",
    )(*att_parts, ssm, x2d, w_bf, g.reshape(1, d), wr, br)


def _attn_sample_kernel(q_ref, kn_ref, vn_ref, kc_ref, vc_ref, bias_ref, bias0_ref, o_ref):
    w = kc_ref.shape[1]
    q = q_ref[0]
    head_grp = lax.broadcasted_iota(jnp.int32, (N_ATT_HEADS, 1), 0) // KV_REP
    kn = kn_ref[0].astype(BF16).astype(F32)
    vn = vn_ref[0].astype(BF16).astype(F32)
    s_self = jnp.sum(q.astype(F32) * kn, axis=-1, keepdims=True)

    def taps(c_ref, dil):
        span = WIN * dil
        rows = c_ref[0, w - span:w, :].astype(BF16)
        if dil == 1:
            return rows
        tap = lax.broadcasted_iota(jnp.int32, (WIN, span), 0)
        col = lax.broadcasted_iota(jnp.int32, (WIN, span), 1)
        pick = jnp.where(col == tap * dil, 1.0, 0.0).astype(BF16)
        return jnp.dot(pick, rows, preferred_element_type=F32).astype(BF16)

    scores, selfs, lses = [], [], []
    for g, dil in enumerate(DILATIONS):
        kk = taps(kc_ref, dil)
        s = jnp.zeros((N_ATT_HEADS, WIN), F32)
        for kvh in range(N_KV_HEADS):
            cs = slice(kvh * HEAD_DIM, (kvh + 1) * HEAD_DIM)
            sk = lax.dot_general(q, kk[:, cs], (((1,), (1,)), ((), ())), preferred_element_type=F32)
            s = jnp.where(head_grp == kvh, sk, s)
        s = s + bias_ref[g]
        s0 = s_self + bias0_ref[g]
        mx = jnp.maximum(jnp.max(s, axis=-1, keepdims=True), s0)
        lse = mx + jnp.log(jnp.sum(jnp.exp(s - mx), axis=-1, keepdims=True) + jnp.exp(s0 - mx))
        scores.append(s); selfs.append(s0); lses.append(lse)
    top = functools.reduce(jnp.maximum, lses)
    es = [jnp.exp(l - top) for l in lses]
    tot = functools.reduce(jnp.add, es)
    o = jnp.zeros((N_ATT_HEADS, HEAD_DIM), F32)
    for s, s0, lse, e, dil in zip(scores, selfs, lses, es, DILATIONS):
        wgt = e / tot
        p = (jnp.exp(s - lse) * wgt).astype(BF16)
        p0 = (jnp.exp(s0 - lse) * wgt).astype(BF16).astype(F32)
        vv = taps(vc_ref, dil)
        o = o + p0 * vn
        for kvh in range(N_KV_HEADS):
            cs = slice(kvh * HEAD_DIM, (kvh + 1) * HEAD_DIM)
            ok = jnp.dot(p, vv[:, cs], preferred_element_type=F32)
            o = o + jnp.where(head_grp == kvh, ok, 0.0)
    o_ref[0] = o


def _attn_sample(q, k_new, v_new, k_cache, v_cache, bias_s, bias0_s):
    n, w = k_cache.shape[0], k_cache.shape[1]
    assert w % (max(DILATIONS) * WIN) == 0
    tok = lambda b: (b, 0, 0)
    head = pl.BlockSpec((1, N_ATT_HEADS, HEAD_DIM), tok)
    window = pl.BlockSpec((1, w, KV_WIDTH), tok)
    return pl.pallas_call(
        _attn_sample_kernel,
        grid=(n,),
        in_specs=[head, head, head, window, window,
                  _const_spec((len(DILATIONS), N_ATT_HEADS, WIN)),
                  _const_spec((len(DILATIONS), N_ATT_HEADS, 1))],
        out_specs=head,
        out_shape=jax.ShapeDtypeStruct((n, N_ATT_HEADS, HEAD_DIM), F32),
        compiler_params=_cparams(("parallel",)),
        name="attn_sample",
    )(q, jnp.repeat(k_new, KV_REP, axis=1), jnp.repeat(v_new, KV_REP, axis=1), k_cache, v_cache,
      bias_s, bias0_s)


def _conv_sample_kernel(xbc_ref, b0_ref, b1_ref, b2_ref, cw_ref, cb_ref, dt_ref, dtb_ref, alog_ref,
                        e_ref, xa_ref, xdt_ref, decay_ref):
    acc = cb_ref[...] + xbc_ref[...] * cw_ref[CONV_W - 1:CONV_W, :]
    for i, buf in enumerate((b0_ref, b1_ref, b2_ref)):
        acc = acc + buf[...] * cw_ref[i:i + 1, :]
    xa = _silu(acc)
    xa_ref[...] = xa
    dt = _softplus(dt_ref[...] + dtb_ref[...])
    decay_ref[...] = jnp.exp(dt * (-jnp.exp(alog_ref[...])))
    dt_full = jnp.dot(dt, e_ref[...], preferred_element_type=F32, precision=HIGHEST)
    xdt_ref[...] = xa[:, :SSM_WIDTH] * dt_full


def _conv_sample(xbc, conv_buf, conv_w, conv_b, dt_raw, dt_bias, a_log):
    n = xbc.shape[0]
    args = (xbc, conv_buf[:, 0], conv_buf[:, 1], conv_buf[:, 2], conv_w, conv_b.reshape(1, CONV_DIM),
            dt_raw, _pad_lanes(dt_bias), _pad_lanes(a_log),
            _head_expand())
    return pl.pallas_call(
        _conv_sample_kernel,
        grid=(1,),
        in_specs=[_const_spec(a.shape) for a in args],
        out_specs=[_const_spec((n, CONV_DIM)), _const_spec((n, SSM_WIDTH)), _const_spec((n, LANES))],
        out_shape=[jax.ShapeDtypeStruct((n, CONV_DIM), F32), jax.ShapeDtypeStruct((n, SSM_WIDTH), F32),
                   jax.ShapeDtypeStruct((n, LANES), F32)],
        compiler_params=_cparams(("arbitrary",)),
        name="conv_sample",
    )(*args)


def _ssm_sample_kernel(xdt_ref, decay_ref, b_ref, c_ref, h0_ref, hn_ref, y_ref):
    for g in range(N_SSM_GROUPS):
        hs = slice(g * HEADS_PER_GROUP, (g + 1) * HEADS_PER_GROUP)
        hn = decay_ref[0, hs] * h0_ref[0, hs] + xdt_ref[0, hs] * b_ref[0, g]
        hn_ref[0, hs] = hn
        c_row = c_ref[0, g].astype(BF16).astype(F32)
        y_ref[0, hs] = jnp.sum(hn.astype(BF16).astype(F32) * c_row, axis=-1, keepdims=True)


def _ssm_sample(xdt, decay, bmat, cmat, h0):
    n = xdt.shape[0]
    p = SSM_HEAD_DIM
    tok4 = lambda b: (b, 0, 0, 0)
    hn, y = pl.pallas_call(
        _ssm_sample_kernel,
        grid=(n,),
        in_specs=[pl.BlockSpec((1, N_SSM_HEADS, p, 1), tok4),
                  pl.BlockSpec((1, N_SSM_HEADS, 1, 1), tok4),
                  pl.BlockSpec((1, N_SSM_GROUPS, 1, D_STATE), tok4),
                  pl.BlockSpec((1, N_SSM_GROUPS, 1, D_STATE), tok4),
                  pl.BlockSpec((1, N_SSM_HEADS, p, D_STATE), tok4)],
        out_specs=[pl.BlockSpec((1, N_SSM_HEADS, p, D_STATE), tok4),
                   pl.BlockSpec((1, N_SSM_HEADS, p, 1), tok4)],
        out_shape=[jax.ShapeDtypeStruct((n, N_SSM_HEADS, p, D_STATE), F32),
                   jax.ShapeDtypeStruct((n, N_SSM_HEADS, p, 1), F32)],
        compiler_params=_cparams(("parallel",)),
        name="ssm_sample",
    )(xdt.reshape(n, N_SSM_HEADS, p, 1), decay[:, :N_SSM_HEADS].reshape(n, N_SSM_HEADS, 1, 1),
      bmat.reshape(n, N_SSM_GROUPS, 1, D_STATE), cmat.reshape(n, N_SSM_GROUPS, 1, D_STATE), h0)
    return hn, y.reshape(n, SSM_WIDTH)


def _outproj_sample_kernel(att_ref, y_ref, xs_ref, z_ref, dskip_ref, gn_ref, x_ref, w_ref, g_ref,
                           wr_ref, br_ref, x1_ref, xn_ref, lg_ref):
    y = y_ref[...] + dskip_ref[...] * xs_ref[...]
    u = y * _silu(z_ref[...])
    gw = SSM_WIDTH // N_SSM_GROUPS
    parts = []
    for g in range(N_SSM_GROUPS):
        ug = u[:, g * gw:(g + 1) * gw]
        parts.append(ug * lax.rsqrt(jnp.mean(ug * ug, axis=-1, keepdims=True) + EPS)
                     * gn_ref[:, g * gw:(g + 1) * gw])
    mix = jnp.concatenate([att_ref[...]] + parts, axis=-1).astype(BF16)
    x1 = x_ref[...] + jnp.dot(mix, w_ref[...], preferred_element_type=F32)
    x1_ref[...] = x1
    xn = _rms(x1, g_ref[...]).astype(BF16)
    xn_ref[...] = xn
    lg_ref[...] = _router(xn, wr_ref, br_ref)


def _outproj_sample(att, y, xs, z, d_skip, ssd_norm, x2d, w_bf, g, wr, br):
    n, d = x2d.shape
    args = (att, y, xs, z, jnp.repeat(d_skip, SSM_HEAD_DIM).reshape(1, SSM_WIDTH),
            ssd_norm.reshape(1, SSM_WIDTH), x2d, w_bf, g.reshape(1, d), wr, br)
    return pl.pallas_call(
        _outproj_sample_kernel,
        grid=(1,),
        in_specs=[_const_spec(a.shape) for a in args],
        out_specs=[_const_spec((n, d)), _const_spec((n, d)), _const_spec((n, LANES))],
        out_shape=[jax.ShapeDtypeStruct((n, d), F32), jax.ShapeDtypeStruct((n, d), BF16),
                   jax.ShapeDtypeStruct((n, LANES), F32)],
        compiler_params=_cparams(("arbitrary",)),
        name="outproj_sample",
    )(*args)


TT = 256
LROWS = TT * TOP_K + N_EXPERTS * SUBLANES
CHUNK_SIZES = (256, 128, 64, 32, 16, 8)
SUB = 256
RG = 5 * SUB
TF = 256
VMEM_LIMIT_FFN = 60000 * 1024


def _chunk_loop(cnt_ref, loff_ref, dest_ref, tile, fn):
    def per_expert(e, carry):
        idx = tile * N_EXPERTS + e
        n, off, dst = cnt_ref[idx], loff_ref[idx], dest_ref[idx]
        for size in CHUNK_SIZES:
            take = (n & size) != 0

            @pl.when(take)
            def _(off=off, dst=dst, size=size):
                fn(pl.multiple_of(off, SUBLANES), pl.multiple_of(dst, SUBLANES), size)

            step = jnp.where(take, size, 0)
            off, dst = off + step, dst + step
        return carry

    lax.fori_loop(0, N_EXPERTS, per_expert, 0)


def _dispatch_kernel(cnt_ref, loff_ref, dest_ref, xp_ref, xs_ref, lpos_ref, gate_ref, out_hbm, buf, sem):
    t = pl.program_id(0)
    last = pl.num_programs(0) - 1
    d = xp_ref.shape[1]
    x = jnp.where(t == last, xs_ref[...], xp_ref[...])
    rows = lax.broadcasted_iota(jnp.int32, (LROWS, TT), 0)
    onehot = jnp.zeros((LROWS, TT), F32)
    wcol = jnp.zeros((LROWS, 1), F32)
    for k in range(TOP_K):
        hit = jnp.where(rows == lpos_ref[0, k:k + 1, :], 1.0, 0.0)
        onehot = onehot + hit
        wcol = wcol + jnp.sum(hit * gate_ref[0, k:k + 1, :], axis=-1, keepdims=True)
    tile_buf = buf.at[t % 2]
    tile_buf[:, 0:d] = jnp.dot(onehot.astype(BF16), x, preferred_element_type=F32)
    tile_buf[:, d:d + LANES] = jnp.broadcast_to(wcol, (LROWS, LANES))

    def copy(tile, off, dst, size):
        return pltpu.make_async_copy(buf.at[tile % 2, pl.ds(off, size)], out_hbm.at[pl.ds(dst, size)],
                                     sem.at[tile % 2])

    _chunk_loop(cnt_ref, loff_ref, dest_ref, t, lambda o, g, s: copy(t, o, g, s).start())

    @pl.when(t > 0)
    def _():
        _chunk_loop(cnt_ref, loff_ref, dest_ref, t - 1, lambda o, g, s: copy(t - 1, o, g, s).wait())

    @pl.when(t == last)
    def _():
        _chunk_loop(cnt_ref, loff_ref, dest_ref, t, lambda o, g, s: copy(t, o, g, s).wait())


def _dispatch(tabs, xn_p, xn_s, lpos_t, gate_t, n_rows):
    nt = lpos_t.shape[0]
    d = xn_p.shape[1]
    last_p = xn_p.shape[0] // TT - 1
    grid_spec = pltpu.PrefetchScalarGridSpec(
        num_scalar_prefetch=3,
        grid=(nt,),
        in_specs=[pl.BlockSpec((TT, d), lambda t, *_: (jnp.minimum(t, last_p), 0)),
                  pl.BlockSpec((TT, d), lambda t, *_: (0, 0)),
                  pl.BlockSpec((1, TOP_K, TT), lambda t, *_: (t, 0, 0)),
                  pl.BlockSpec((1, TOP_K, TT), lambda t, *_: (t, 0, 0))],
        out_specs=pl.BlockSpec(memory_space=pl.ANY),
        scratch_shapes=[pltpu.VMEM((2, LROWS, d + LANES), F32), pltpu.SemaphoreType.DMA((2,))],
    )
    return pl.pallas_call(
        _dispatch_kernel,
        grid_spec=grid_spec,
        out_shape=jax.ShapeDtypeStruct((n_rows, d + LANES), F32),
        compiler_params=_cparams(("arbitrary",)),
        name="moe_dispatch",
    )(*tabs, xn_p, xn_s, lpos_t, gate_t)


def _ffn_kernel(ge_ref, gs_ref, gn_ref, ng_ref, xs_hbm, wgu_hbm, wd_hbm, bgu_ref, bdn_ref, out_hbm,
                xbuf, acc, ostage, wg_st, wu_st, wd_st, wg_bf, wu_bf, wd_bf, sem_w, sem_x, sem_o):
    d = acc.shape[1]
    d_ff = wd_hbm.shape[1]
    nf = d_ff // TF
    n_groups = ng_ref[0]
    total = n_groups * nf

    def w_copies(s, slot):
        g = s // nf
        f = s - g * nf
        e = ge_ref[g]
        c0 = pl.multiple_of(f * TF, TF)
        return (pltpu.make_async_copy(wgu_hbm.at[e, :, pl.ds(c0, TF)], wg_st.at[slot], sem_w.at[slot, 0]),
                pltpu.make_async_copy(wgu_hbm.at[e, :, pl.ds(d_ff + c0, TF)], wu_st.at[slot], sem_w.at[slot, 1]),
                pltpu.make_async_copy(wd_hbm.at[e, pl.ds(c0, TF), :], wd_st.at[slot], sem_w.at[slot, 2]))

    def x_copy(g, j):
        r0 = pl.multiple_of(j * SUB, SUB)
        return pltpu.make_async_copy(xs_hbm.at[pl.ds(pl.multiple_of(gs_ref[g] + r0, SUBLANES), SUB)],
                                     xbuf.at[g % 2, pl.ds(r0, SUB)], sem_x)

    def o_copy(g, j):
        r0 = pl.multiple_of(j * SUB, SUB)
        return pltpu.make_async_copy(ostage.at[j % 2],
                                     out_hbm.at[pl.ds(pl.multiple_of(gs_ref[g] + r0, SUBLANES), SUB)],
                                     sem_o.at[j % 2])

    def loop(n, fn):
        lax.fori_loop(0, n, lambda j, c: (fn(j), c)[1], 0)

    def drain_stores(g):
        n = gn_ref[g]
        for back in (1, 2):
            @pl.when(n >= back)
            def _():
                o_copy(g, n - back).wait()

    @pl.when(total > 0)
    def _():
        for c in w_copies(0, 0):
            c.start()
        loop(gn_ref[0], lambda j: x_copy(0, j).start())

    def item(s, carry):
        slot = s % 2
        g = s // nf
        f = s - g * nf
        e = ge_ref[g]
        nsub = gn_ref[g]
        xg = xbuf.at[g % 2]

        @pl.when(s + 1 < total)
        def _():
            for c in w_copies(s + 1, 1 - slot):
                c.start()

        @pl.when(f == 0)
        def _():
            loop(nsub, lambda j: x_copy(g, j).wait())

        @pl.when(jnp.logical_and(f == 1, g + 1 < n_groups))
        def _():
            loop(gn_ref[g + 1], lambda j: x_copy(g + 1, j).start())

        for c in w_copies(s, slot):
            c.wait()
        wg_bf[...] = wg_st[slot].astype(BF16)
        wu_bf[...] = wu_st[slot].astype(BF16)
        wd_bf[...] = wd_st[slot].astype(BF16)
        bg = bgu_ref[pl.ds(e * 2 * nf + f, 1), :]
        bu = bgu_ref[pl.ds(e * 2 * nf + nf + f, 1), :]

        def sub_block(j, phase):
            rs = pl.ds(pl.multiple_of(j * SUB, SUB), SUB)
            x = xg[rs, 0:d].astype(BF16)
            hg = jnp.dot(x, wg_bf[...], preferred_element_type=F32) + bg
            hu = jnp.dot(x, wu_bf[...], preferred_element_type=F32) + bu
            gg = jnp.minimum(hg, SWIGLU_LIMIT)
            uu = jnp.clip(hu, -SWIGLU_LIMIT, SWIGLU_LIMIT)
            act = gg * (1.0 / (1.0 + jnp.exp(-SWIGLU_ALPHA * gg))) * (uu + 1.0)
            part = jnp.dot(act.astype(BF16), wd_bf[...], preferred_element_type=F32)
            if phase == "first":
                acc[rs, :] = part + bdn_ref[pl.ds(e, 1), :]
            elif phase == "middle":
                acc[rs, :] += part
            else:
                @pl.when(j >= 2)
                def _():
                    o_copy(g, j - 2).wait()

                ostage[j % 2] = (acc[rs, :] + part) * xg[rs, d:d + 1]
                o_copy(g, j).start()

        def all_sub_blocks(phase):
            def pair(p):
                sub_block(2 * p, phase)
                sub_block(2 * p + 1, phase)

            loop(nsub // 2, pair)

            @pl.when(nsub % 2 == 1)
            def _():
                sub_block(nsub - 1, phase)

        @pl.when(f == 0)
        def _():
            all_sub_blocks("first")

        @pl.when(jnp.logical_and(f > 0, f < nf - 1))
        def _():
            all_sub_blocks("middle")

        @pl.when(f == nf - 1)
        def _():
            @pl.when(g > 0)
            def _():
                drain_stores(g - 1)

            all_sub_blocks("last")

        return carry

    lax.fori_loop(0, total, item, 0)

    @pl.when(total > 0)
    def _():
        drain_stores(n_groups - 1)


def _moe_ffn(groups, x_sorted, w_gate_up, b_gu, w_down, b_dn):
    n_rows = x_sorted.shape[0]
    _, d_ff, d = w_down.shape
    nf = d_ff // TF
    assert nf >= 2
    bgu2 = b_gu.reshape(N_EXPERTS * 2 * nf, TF)
    grid_spec = pltpu.PrefetchScalarGridSpec(
        num_scalar_prefetch=4,
        grid=(1,),
        in_specs=[pl.BlockSpec(memory_space=pl.ANY), pl.BlockSpec(memory_space=pl.ANY),
                  pl.BlockSpec(memory_space=pl.ANY),
                  pl.BlockSpec(bgu2.shape, lambda i, *_: (0, 0), pipeline_mode=pl.Buffered(1)),
                  pl.BlockSpec(b_dn.shape, lambda i, *_: (0, 0), pipeline_mode=pl.Buffered(1))],
        out_specs=pl.BlockSpec(memory_space=pl.ANY),
        scratch_shapes=[pltpu.VMEM((2, RG, d + LANES), F32), pltpu.VMEM((RG, d), F32),
                        pltpu.VMEM((2, SUB, d), F32),
                        pltpu.VMEM((2, d, TF), F32), pltpu.VMEM((2, d, TF), F32), pltpu.VMEM((2, TF, d), F32),
                        pltpu.VMEM((d, TF), BF16), pltpu.VMEM((d, TF), BF16), pltpu.VMEM((TF, d), BF16),
                        pltpu.SemaphoreType.DMA((2, 3)), pltpu.SemaphoreType.DMA(()),
                        pltpu.SemaphoreType.DMA((2,))],
    )
    return pl.pallas_call(
        _ffn_kernel,
        grid_spec=grid_spec,
        out_shape=jax.ShapeDtypeStruct((n_rows, d), F32),
        compiler_params=pltpu.CompilerParams(dimension_semantics=("arbitrary",),
                                             vmem_limit_bytes=VMEM_LIMIT_FFN),
        name="moe_ffn",
    )(*groups, x_sorted, w_gate_up, w_down, bgu2, b_dn)


def _route(logits, m_pad):
    m = logits.shape[0]
    nt = m_pad // TT
    top_v, top_i = lax.top_k(logits, TOP_K)
    gate = jnp.pad(jax.nn.softmax(top_v, axis=-1), ((0, m_pad - m), (0, 0)))
    top_i = jnp.pad(top_i.astype(jnp.int32), ((0, m_pad - m), (0, 0)), constant_values=-1)
    chosen = (top_i[:, :, None] == jnp.arange(N_EXPERTS, dtype=jnp.int32)).astype(jnp.int32)
    tiles = chosen.sum(axis=1).reshape(nt, TT, N_EXPERTS)
    cnt = (tiles.sum(axis=1) + SUBLANES - 1) // SUBLANES * SUBLANES
    loff = jnp.cumsum(cnt, axis=1) - cnt
    seg = cnt.sum(axis=0)
    seg_start = jnp.cumsum(seg) - seg
    dest = seg_start[None, :] + jnp.cumsum(cnt, axis=0) - cnt
    rank = jnp.cumsum(tiles, axis=1) - tiles
    lpos_all = (loff[:, None, :] + rank).reshape(m_pad, N_EXPERTS)
    lpos = (lpos_all[:, None, :] * chosen).sum(axis=-1)
    lpos = jnp.where(top_i >= 0, lpos, -1).astype(jnp.int32)
    n_rows_bound = nt * LROWS
    n_grp_max = n_rows_bound // RG + N_EXPERTS
    grp = (seg + RG - 1) // RG
    grp_end = jnp.cumsum(grp)
    gi = jnp.arange(n_grp_max, dtype=jnp.int32)
    g_exp = jnp.minimum(jnp.searchsorted(grp_end, gi, side='right'), N_EXPERTS - 1).astype(jnp.int32)
    within = gi - (grp_end[g_exp] - grp[g_exp])
    g_start = (seg_start[g_exp] + within * RG).astype(jnp.int32)
    g_rows = jnp.clip(seg[g_exp] - within * RG, 0, RG)
    g_nsub = jnp.where(gi < grp_end[-1], (g_rows + SUB - 1) // SUB, 0).astype(jnp.int32)
    groups = (g_exp, g_start, g_nsub, grp_end[-1:].astype(jnp.int32))
    tabs = tuple(a.reshape(-1).astype(jnp.int32) for a in (cnt, loff, dest))
    return tabs, lpos, gate, groups, n_rows_bound + SUB


def _combine_kernel(cnt_ref, loff_ref, dest_ref, src_hbm, lpos_ref, x1p_ref, x1s_ref, g_ref,
                    yp_ref, ys_ref, buf, sem):
    t = pl.program_id(0)
    last = pl.num_programs(0) - 1

    def copy(tile, off, src, size):
        return pltpu.make_async_copy(src_hbm.at[pl.ds(src, size)], buf.at[tile % 2, pl.ds(off, size)],
                                     sem.at[tile % 2])

    @pl.when(t == 0)
    def _():
        buf[...] = jnp.zeros_like(buf)
        _chunk_loop(cnt_ref, loff_ref, dest_ref, t, lambda o, g, s: copy(t, o, g, s).start())

    @pl.when(t < last)
    def _():
        _chunk_loop(cnt_ref, loff_ref, dest_ref, t + 1, lambda o, g, s: copy(t + 1, o, g, s).start())

    _chunk_loop(cnt_ref, loff_ref, dest_ref, t, lambda o, g, s: copy(t, o, g, s).wait())
    cols = lax.broadcasted_iota(jnp.int32, (TT, LROWS), 1)
    sel = jnp.zeros((TT, LROWS), F32)
    for k in range(TOP_K):
        sel = sel + jnp.where(cols == lpos_ref[:, k:k + 1], 1.0, 0.0)
    f = jnp.dot(sel.astype(BF16), buf[t % 2].astype(BF16), preferred_element_type=F32)

    @pl.when(t < last)
    def _():
        yp_ref[...] = _rms(x1p_ref[...] + f, g_ref[...])

    @pl.when(t == last)
    def _():
        ys_ref[...] = _rms(x1s_ref[...] + f, g_ref[...])


def _combine(tabs, out_sorted, lpos, x1p, x1s, g):
    d = x1p.shape[1]
    nt = lpos.shape[0] // TT
    last_p = x1p.shape[0] // TT - 1
    prompt = pl.BlockSpec((TT, d), lambda t, *_: (jnp.minimum(t, last_p), 0))
    sample = pl.BlockSpec((TT, d), lambda t, *_: (0, 0))
    grid_spec = pltpu.PrefetchScalarGridSpec(
        num_scalar_prefetch=3,
        grid=(nt,),
        in_specs=[pl.BlockSpec(memory_space=pl.ANY),
                  pl.BlockSpec((TT, TOP_K), lambda t, *_: (t, 0)),
                  prompt, sample,
                  pl.BlockSpec((1, d), lambda t, *_: (0, 0))],
        out_specs=[prompt, sample],
        scratch_shapes=[pltpu.VMEM((2, LROWS, d), F32), pltpu.SemaphoreType.DMA((2,))],
    )
    return pl.pallas_call(
        _combine_kernel,
        grid_spec=grid_spec,
        out_shape=[jax.ShapeDtypeStruct(x1p.shape, F32), jax.ShapeDtypeStruct((TT, d), F32)],
        compiler_params=_cparams(("arbitrary",)),
        name="moe_combine",
    )(*tabs, out_sorted, lpos, x1p, x1s, g.reshape(1, d))


def _t5_bucket(dist):
    max_exact = N_BUCKETS // 2
    dd = dist.astype(F32)
    large = max_exact + (jnp.log(jnp.maximum(dd, 1.0) / max_exact)
                         / math.log(BUCKET_MAX_DIST / max_exact) * (N_BUCKETS - max_exact)).astype(jnp.int32)
    large = jnp.minimum(large, N_BUCKETS - 1)
    return jnp.where(dist < max_exact, dist, large)


def _bias_tables(rel_bias):
    dist = jnp.asarray(np.arange(N_TAPS)[None, :] * np.array(DILATIONS)[:, None], jnp.int32)
    bias = jnp.transpose(rel_bias[_t5_bucket(dist)], (2, 0, 1)).astype(F32)
    by_branch = jnp.transpose(bias, (1, 0, 2))
    g, h = by_branch.shape[:2]
    row = jnp.concatenate([by_branch[:, :, ::-1], jnp.full((g, h, WIN), NEG, F32)], axis=-1)
    flat = jnp.broadcast_to(row[:, :, None, :], (g, h, WIN, 2 * WIN + 1)).reshape(g, h, -1)
    band = flat[:, :, :WIN * 2 * WIN].reshape(g, h, WIN, 2 * WIN)
    return band, by_branch[:, :, :0:-1], by_branch[:, :, 0:1]


def kernel(x_prompt, x_sample, cache_k_win, cache_v_win, state_conv, state_ssm, rel_bias, attn_norm, w_in, conv_w, conv_b, dt_bias, a_log, d_skip, ssd_norm, w_out, ffn_norm, w_router, b_router, w_gate_up, b_gate_up, w_down, b_down, final_norm):
    bp, tp, d = x_prompt.shape
    bs, ts, _ = x_sample.shape
    depth = w_in.shape[0]
    assert depth == 1 and ts == 1 and tp % (max(DILATIONS) * WIN) == 0
    keep = min(max(DILATIONS) * WIN, tp)
    band, samp, samp0 = _bias_tables(rel_bias)
    l = 0

    xp = x_prompt.reshape(bp * tp, d)
    xs = x_sample.reshape(bs * ts, d)
    w_in_bf = jnp.pad(w_in[l], ((0, 0), (0, IN_PROJ_PAD - IN_PROJ))).astype(BF16)
    w_out_bf = w_out[l].astype(BF16)
    wr = jnp.pad(w_router[l], ((0, 0), (0, LANES - N_EXPERTS))).astype(BF16)
    br = jnp.pad(b_router[l], (0, LANES - N_EXPERTS), constant_values=NEG).reshape(1, LANES)

    q, k, v, z, xbc, dt_raw = _inproj(xp, attn_norm[l], w_in_bf, TM_PROJ)
    q3 = q.reshape(bp, tp, ATT_WIDTH)
    k3 = k.reshape(bp, tp, KV_WIDTH)
    v3 = v.reshape(bp, tp, KV_WIDTH)
    att_parts = []
    for gi, dil in enumerate(DILATIONS):
        att_parts.extend(_attn_branch(q3, k3, v3, band[gi], dil))
    ssm, st_p = _ssd_prompt(xbc, dt_raw, z, conv_w[l], conv_b[l], dt_bias[l], a_log[l], d_skip[l],
                            ssd_norm[l], bp)
    x1p, xnp_, lgp = _outproj_prompt(att_parts, ssm, xp, w_out_bf, ffn_norm[l], wr, br)
    k_win_p = k3[:, tp - keep:].reshape(1, bp, keep, N_KV_HEADS, HEAD_DIM)
    v_win_p = v3[:, tp - keep:].reshape(1, bp, keep, N_KV_HEADS, HEAD_DIM)
    conv_p = xbc.reshape(bp, tp, CONV_DIM)[:, tp - (CONV_W - 1):][None]

    q_s, k_s, v_s, z_s, xbc_s, dt_s = _inproj(xs, attn_norm[l], w_in_bf, bs * ts)
    q_s = q_s.reshape(bs, N_ATT_HEADS, HEAD_DIM)
    k_s = k_s.reshape(bs, N_KV_HEADS, HEAD_DIM)
    v_s = v_s.reshape(bs, N_KV_HEADS, HEAD_DIM)
    wbuf = cache_k_win.shape[2]
    att_s = _attn_sample(q_s, k_s, v_s, cache_k_win[l].reshape(bs, wbuf, KV_WIDTH),
                         cache_v_win[l].reshape(bs, wbuf, KV_WIDTH), samp, samp0)
    xa_s, xdt_s, decay_s = _conv_sample(xbc_s, state_conv[l], conv_w[l], conv_b[l], dt_s, dt_bias[l],
                                        a_log[l])
    nbc = N_SSM_GROUPS * D_STATE
    h_s, y_s = _ssm_sample(xdt_s, decay_s, xa_s[:, SSM_WIDTH:SSM_WIDTH + nbc], xa_s[:, SSM_WIDTH + nbc:],
                           state_ssm[l])
    x1s, xns, lgs = _outproj_sample(att_s.reshape(bs, ATT_WIDTH), y_s, xa_s[:, :SSM_WIDTH], z_s,
                                    d_skip[l], ssd_norm[l], xs, w_out_bf, ffn_norm[l], wr, br)
    conv_s = jnp.concatenate([state_conv[l][:, 1:], xbc_s[:, None]], axis=1)[None]

    n_s = bs * ts
    assert (bp * tp) % TT == 0 and n_s <= TT
    m_pad = bp * tp + TT
    logits = jnp.concatenate([lgp[:, :N_EXPERTS], lgs[:, :N_EXPERTS]], axis=0)
    tabs, lpos, gate, groups, n_rows = _route(logits, m_pad)
    by_tile = lambda a: jnp.transpose(a.reshape(m_pad // TT, TT, TOP_K), (0, 2, 1))
    pad_rows = lambda a: jnp.pad(a, ((0, TT - n_s), (0, 0)))
    x_sorted = _dispatch(tabs, xnp_, pad_rows(xns), by_tile(lpos), by_tile(gate), n_rows)
    out_sorted = _moe_ffn(groups, x_sorted, w_gate_up[l], b_gate_up[l], w_down[l], b_down[l])
    y_p, y_s_out = _combine(tabs, out_sorted, lpos, x1p, pad_rows(x1s), final_norm)
    y_s_out = y_s_out[:n_s]

    return (y_p.reshape(bp, tp, d), y_s_out.reshape(bs, ts, d), k_win_p, v_win_p, conv_p, st_p[None],
            k_s.reshape(1, bs, ts, N_KV_HEADS, HEAD_DIM), v_s.reshape(1, bs, ts, N_KV_HEADS, HEAD_DIM),
            conv_s, h_s[None])
```

```python
import functools
import math

import jax
import jax.numpy as jnp
import numpy as np
from jax import lax
from jax.experimental import pallas as pl
from jax.experimental.pallas import tpu as pltpu

F32 = jnp.float32
BF16 = jnp.bfloat16
HIGHEST = lax.Precision.HIGHEST

LANES = 128
SUBLANES = 8
VMEM_LIMIT = 56 * 1024 * 1024

HEAD_DIM = 64
N_ATT_HEADS = 16
N_KV_HEADS = 4
KV_REP = N_ATT_HEADS // N_KV_HEADS
ATT_WIDTH = N_ATT_HEADS * HEAD_DIM
KV_WIDTH = N_KV_HEADS * HEAD_DIM
DILATIONS = (1, 4, 16)
N_TAPS = 129
WIN = N_TAPS - 1
ATT_SCALE = HEAD_DIM ** -0.5
N_BUCKETS = 32
BUCKET_MAX_DIST = 2048
SSM_HEAD_DIM = 64
N_SSM_HEADS = 16
SSM_WIDTH = N_SSM_HEADS * SSM_HEAD_DIM
N_SSM_GROUPS = 2
HEADS_PER_GROUP = N_SSM_HEADS // N_SSM_GROUPS
D_STATE = 128
CONV_W = 4
CONV_DIM = SSM_WIDTH + 2 * N_SSM_GROUPS * D_STATE
SSD_CHUNK = 128
N_EXPERTS = 32
TOP_K = 4
SWIGLU_LIMIT = 7.0
SWIGLU_ALPHA = 1.702
EPS = 1e-5
NEG = -1e30

Q0, K0, V0, Z0, X0, DT0 = 0, 1024, 1280, 1536, 2560, 4096
IN_PROJ = DT0 + N_SSM_HEADS
IN_PROJ_PAD = DT0 + LANES

TM_PROJ = 512
TM_OUT = 256
TM_MOE = 512
TF_MOE = 512
TM_FIN = 256


def _cparams(sem):
    return pltpu.CompilerParams(dimension_semantics=sem, vmem_limit_bytes=VMEM_LIMIT)


def _const_spec(shape):
    nd = len(shape)
    return pl.BlockSpec(shape, lambda *_: (0,) * nd, pipeline_mode=pl.Buffered(1))


def _rms(x, g):
    ms = jnp.mean(x * x, axis=-1, keepdims=True)
    return x * lax.rsqrt(ms + EPS) * g


def _silu(x):
    return x * (1.0 / (1.0 + jnp.exp(-x)))


def _inproj_kernel(x_ref, g_ref, w_ref, q_ref, k_ref, v_ref, z_ref, xbc_ref, dt_ref):
    xn = _rms(x_ref[...], g_ref[...]).astype(BF16)

    def mm(lo, hi):
        return jnp.dot(xn, w_ref[:, lo:hi], preferred_element_type=F32)

    q_ref[...] = (mm(Q0, K0) * ATT_SCALE).astype(BF16)
    k_ref[...] = mm(K0, V0)
    v_ref[...] = mm(V0, Z0)
    z_ref[...] = mm(Z0, X0)
    xbc_ref[...] = mm(X0, DT0)
    dt_ref[...] = mm(DT0, IN_PROJ_PAD)


def _inproj(x2d, g, w_bf, tm):
    m, d = x2d.shape
    widths = (ATT_WIDTH, KV_WIDTH, KV_WIDTH, SSM_WIDTH, CONV_DIM, LANES)
    dtypes = (BF16, F32, F32, F32, F32, F32)
    return pl.pallas_call(
        _inproj_kernel,
        grid=(m // tm,),
        in_specs=[pl.BlockSpec((tm, d), lambda i: (i, 0)),
                  _const_spec((1, d)),
                  _const_spec((d, IN_PROJ_PAD))],
        out_specs=[pl.BlockSpec((tm, w), lambda i: (i, 0)) for w in widths],
        out_shape=[jax.ShapeDtypeStruct((m, w), t) for w, t in zip(widths, dtypes)],
        compiler_params=_cparams(("parallel",)),
        name=f"inproj_{tm}",
    )(x2d, g.reshape(1, d), w_bf)


def _attn_kernel(q_ref, kp_ref, kc_ref, vp_ref, vc_ref, bias_ref, o_ref, lse_ref):
    first = pl.program_id(2) == 0
    lane = lax.broadcasted_iota(jnp.int32, (1, 2 * WIN), 1)
    prev_mask = jnp.where(jnp.logical_and(first, lane < WIN), NEG, 0.0)
    for kvh in range(N_KV_HEADS):
        cs = slice(kvh * HEAD_DIM, (kvh + 1) * HEAD_DIM)
        kw = jnp.concatenate([kp_ref[0, :, cs], kc_ref[0, :, cs]], axis=0).astype(BF16)
        vw = jnp.concatenate([vp_ref[0, :, cs], vc_ref[0, :, cs]], axis=0).astype(BF16)
        for pair in range(KV_REP // 2):
            outs, lses = [], []
            for r in range(2):
                h = kvh * KV_REP + pair * 2 + r
                qh = q_ref[0, :, h * HEAD_DIM:(h + 1) * HEAD_DIM]
                s = lax.dot_general(qh, kw, (((1,), (1,)), ((), ())), preferred_element_type=F32)
                s = s + bias_ref[h] + prev_mask
                mx = jnp.max(s, axis=-1, keepdims=True)
                p = jnp.exp(s - mx)
                l = jnp.sum(p, axis=-1, keepdims=True)
                o = jnp.dot(p.astype(BF16), vw, preferred_element_type=F32)
                outs.append(o * (1.0 / l))
                lses.append(jnp.broadcast_to(mx + jnp.log(l), (WIN, HEAD_DIM)))
            h0 = kvh * KV_REP + pair * 2
            o_ref[0, :, h0 * HEAD_DIM:(h0 + 2) * HEAD_DIM] = jnp.concatenate(outs, axis=-1)
            lse_ref[0, :, h0 * HEAD_DIM:(h0 + 2) * HEAD_DIM] = jnp.concatenate(lses, axis=-1)


def _attn_branch(q, k, v, bias_mat, dil):
    b, s, _ = q.shape
    sub = s // dil
    nb = sub // WIN
    qv = q.reshape(b, sub, dil * ATT_WIDTH)
    kv_ = k.reshape(b, sub, dil * KV_WIDTH)
    vv = v.reshape(b, sub, dil * KV_WIDTH)
    cur = lambda bb, r, i: (bb, i, r)
    prev = lambda bb, r, i: (bb, jnp.maximum(i - 1, 0), r)
    o, lse = pl.pallas_call(
        _attn_kernel,
        grid=(b, dil, nb),
        in_specs=[pl.BlockSpec((1, WIN, ATT_WIDTH), cur),
                  pl.BlockSpec((1, WIN, KV_WIDTH), prev),
                  pl.BlockSpec((1, WIN, KV_WIDTH), cur),
                  pl.BlockSpec((1, WIN, KV_WIDTH), prev),
                  pl.BlockSpec((1, WIN, KV_WIDTH), cur),
                  _const_spec((N_ATT_HEADS, WIN, 2 * WIN))],
        out_specs=[pl.BlockSpec((1, WIN, ATT_WIDTH), cur),
                   pl.BlockSpec((1, WIN, ATT_WIDTH), cur)],
        out_shape=[jax.ShapeDtypeStruct((b, sub, dil * ATT_WIDTH), F32),
                   jax.ShapeDtypeStruct((b, sub, dil * ATT_WIDTH), F32)],
        compiler_params=_cparams(("parallel", "parallel", "arbitrary")),
        name=f"attn_dil{dil}",
    )(qv, kv_, kv_, vv, vv, bias_mat)
    return o.reshape(b * s, ATT_WIDTH), lse.reshape(b * s, ATT_WIDTH)


def _softplus(x):
    return jnp.maximum(x, 0.0) + jnp.log(1.0 + jnp.exp(-jnp.abs(x)))


def _ssd_kernel(xbc_ref, dt_ref, z_ref, cw_ref, cb_ref, dtb_ref, alog_ref, dskip_ref, gn_ref, e_ref,
                y_ref, st_ref, ext_ref, state_ref):
    c = pl.program_id(1)
    L = SSD_CHUNK

    @pl.when(c == 0)
    def _():
        ext_ref[0:SUBLANES, :] = jnp.zeros((SUBLANES, CONV_DIM), F32)
        state_ref[...] = jnp.zeros_like(state_ref)

    ext_ref[SUBLANES:SUBLANES + L, :] = xbc_ref[...]
    acc = cb_ref[...] + ext_ref[SUBLANES:SUBLANES + L, :] * cw_ref[CONV_W - 1:CONV_W, :]
    for i in range(CONV_W - 1):
        off = SUBLANES - (CONV_W - 1) + i
        acc = acc + ext_ref[off:off + L, :] * cw_ref[i:i + 1, :]
    ext_ref[0:SUBLANES, :] = ext_ref[L:L + SUBLANES, :]
    xa = _silu(acc)

    dt = _softplus(dt_ref[...] + dtb_ref[...])
    da = dt * (-jnp.exp(alog_ref[...]))
    row = lax.broadcasted_iota(jnp.int32, (L, L), 0)
    col = lax.broadcasted_iota(jnp.int32, (L, L), 1)
    tri = row >= col
    a_cs = jnp.dot(tri.astype(F32), da, preferred_element_type=F32, precision=HIGHEST)
    a_cs_t = a_cs.T
    expand = e_ref[...]
    acs_full = jnp.dot(a_cs, expand, preferred_element_type=F32, precision=HIGHEST)
    dt_full = jnp.dot(dt, expand, preferred_element_type=F32, precision=HIGHEST)
    exp_acs = jnp.exp(acs_full)
    a_last = acs_full[L - 1:L, :]
    exp_last = exp_acs[L - 1:L, :]
    xs = xa[:, :SSM_WIDTH]
    xdt = xs * dt_full
    xw = xdt * jnp.exp(a_last - acs_full)

    for g in range(N_SSM_GROUPS):
        b0 = SSM_WIDTH + g * D_STATE
        c0 = SSM_WIDTH + N_SSM_GROUPS * D_STATE + g * D_STATE
        bg_t = xa[:, b0:b0 + D_STATE].T.astype(BF16)
        cg = xa[:, c0:c0 + D_STATE].astype(BF16)
        gram = jnp.dot(cg, bg_t, preferred_element_type=F32)
        for hh in range(HEADS_PER_GROUP):
            h = g * HEADS_PER_GROUP + hh
            hs = slice(h * SSM_HEAD_DIM, (h + 1) * SSM_HEAD_DIM)
            seg = jnp.where(tri, a_cs[:, h:h + 1] - a_cs_t[h:h + 1, :], NEG)
            scores = (gram * jnp.exp(seg)).astype(BF16)
            y_diag = jnp.dot(scores, xdt[:, hs].astype(BF16), preferred_element_type=F32)
            st = state_ref[h]
            y_off = jnp.dot(cg, st.astype(BF16), preferred_element_type=F32) * exp_acs[:, hs]
            y_ref[:, hs] = y_diag + y_off
            state_ref[h] = exp_last[:, hs] * st + jnp.dot(bg_t, xw[:, hs].astype(BF16),
                                                          preferred_element_type=F32)

    y = y_ref[...] + dskip_ref[...] * xs
    u = y * _silu(z_ref[...])
    gw = SSM_WIDTH // N_SSM_GROUPS
    parts = []
    for g in range(N_SSM_GROUPS):
        ug = u[:, g * gw:(g + 1) * gw]
        parts.append(ug * lax.rsqrt(jnp.mean(ug * ug, axis=-1, keepdims=True) + EPS))
    y_ref[...] = jnp.concatenate(parts, axis=-1) * gn_ref[...]

    @pl.when(c == pl.num_programs(1) - 1)
    def _():
        st_ref[0] = state_ref[...]


def _head_expand():
    e = np.zeros((LANES, SSM_WIDTH), np.float32)
    for h in range(N_SSM_HEADS):
        e[h, h * SSM_HEAD_DIM:(h + 1) * SSM_HEAD_DIM] = 1.0
    return jnp.asarray(e)


def _pad_lanes(v):
    return jnp.pad(v.astype(F32), (0, LANES - v.shape[0])).reshape(1, LANES)


def _ssd_prompt(xbc, dt_raw, z, conv_w, conv_b, dt_bias, a_log, d_skip, ssd_norm, batch):
    m = xbc.shape[0]
    nc = m // batch // SSD_CHUNK
    L = SSD_CHUNK
    rows = lambda b, c: (b * nc + c, 0)
    y, st = pl.pallas_call(
        _ssd_kernel,
        grid=(batch, nc),
        in_specs=[pl.BlockSpec((L, CONV_DIM), rows),
                  pl.BlockSpec((L, LANES), rows),
                  pl.BlockSpec((L, SSM_WIDTH), rows),
                  _const_spec((CONV_W, CONV_DIM)),
                  _const_spec((1, CONV_DIM)),
                  _const_spec((1, LANES)),
                  _const_spec((1, LANES)),
                  _const_spec((1, SSM_WIDTH)),
                  _const_spec((1, SSM_WIDTH)),
                  _const_spec((LANES, SSM_WIDTH))],
        out_specs=[pl.BlockSpec((L, SSM_WIDTH), rows),
                   pl.BlockSpec((1, N_SSM_HEADS, D_STATE, SSM_HEAD_DIM), lambda b, c: (b, 0, 0, 0))],
        out_shape=[jax.ShapeDtypeStruct((m, SSM_WIDTH), F32),
                   jax.ShapeDtypeStruct((batch, N_SSM_HEADS, D_STATE, SSM_HEAD_DIM), F32)],
        scratch_shapes=[pltpu.VMEM((SUBLANES + L, CONV_DIM), F32),
                        pltpu.VMEM((N_SSM_HEADS, D_STATE, SSM_HEAD_DIM), F32)],
        compiler_params=_cparams(("parallel", "arbitrary")),
        name="ssd_prompt",
    )(xbc, dt_raw, z, conv_w, conv_b.reshape(1, CONV_DIM), _pad_lanes(dt_bias), _pad_lanes(a_log),
      jnp.repeat(d_skip, SSM_HEAD_DIM).reshape(1, SSM_WIDTH), ssd_norm.reshape(1, SSM_WIDTH),
      _head_expand())
    return y, jnp.swapaxes(st, -1, -2)


def _router(xn_bf, wr_ref, br_ref):
    return jnp.dot(xn_bf, wr_ref[...], preferred_element_type=F32) + br_ref[...]


def _outproj_kernel(o0, l0, o1, l1, o2, l2, ssm_ref, x_ref, w_ref, g_ref, wr_ref, br_ref,
                    x1_ref, xn_ref, lg_ref):
    la, lb, lc = l0[...], l1[...], l2[...]
    mx = jnp.maximum(jnp.maximum(la, lb), lc)
    wa, wb, wc = jnp.exp(la - mx), jnp.exp(lb - mx), jnp.exp(lc - mx)
    att = (wa * o0[...] + wb * o1[...] + wc * o2[...]) * (1.0 / (wa + wb + wc))
    y = jnp.dot(att.astype(BF16), w_ref[:ATT_WIDTH, :], preferred_element_type=F32)
    y = y + jnp.dot(ssm_ref[...].astype(BF16), w_ref[ATT_WIDTH:, :], preferred_element_type=F32)
    x1 = x_ref[...] + y
    x1_ref[...] = x1
    xn = _rms(x1, g_ref[...]).astype(BF16)
    xn_ref[...] = xn
    lg_ref[...] = _router(xn, wr_ref, br_ref)


def _outproj_prompt(att_parts, ssm, x2d, w_bf, g, wr, br):
    m, d = x2d.shape
    mix = w_bf.shape[0]
    row = lambda w: pl.BlockSpec((TM_OUT, w), lambda i: (i, 0))
    return pl.pallas_call(
        _outproj_kernel,
        grid=(m // TM_OUT,),
        in_specs=[row(ATT_WIDTH)] * 6 + [row(SSM_WIDTH), row(d),
                                         _const_spec((mix, d)), _const_spec((1, d)),
                                         _const_spec((d, LANES)), _const_spec((1, LANES))],
        out_specs=[row(d), row(d), row(LANES)],
        out_shape=[jax.ShapeDtypeStruct((m, d), F32), jax.ShapeDtypeStruct((m, d), BF16),
                   jax.ShapeDtypeStruct((m, LANES), F32)],
        compiler_params=_cparams(("parallel",)),
        name="outproj_prompt",
    )(*att_parts, ssm, x2d, w_bf, g.reshape(1, d), wr, br)


def _attn_sample_kernel(q_ref, kn_ref, vn_ref, kc_ref, vc_ref, bias_ref, bias0_ref, o_ref):
    w = kc_ref.shape[1]
    q = q_ref[0]
    head_grp = lax.broadcasted_iota(jnp.int32, (N_ATT_HEADS, 1), 0) // KV_REP
    kn = kn_ref[0].astype(BF16).astype(F32)
    vn = vn_ref[0].astype(BF16).astype(F32)
    s_self = jnp.sum(q.astype(F32) * kn, axis=-1, keepdims=True)

    def taps(c_ref, dil):
        span = WIN * dil
        rows = c_ref[0, w - span:w, :].astype(BF16)
        if dil == 1:
            return rows
        tap = lax.broadcasted_iota(jnp.int32, (WIN, span), 0)
        col = lax.broadcasted_iota(jnp.int32, (WIN, span), 1)
        pick = jnp.where(col == tap * dil, 1.0, 0.0).astype(BF16)
        return jnp.dot(pick, rows, preferred_element_type=F32).astype(BF16)

    scores, selfs, lses = [], [], []
    for g, dil in enumerate(DILATIONS):
        kk = taps(kc_ref, dil)
        s = jnp.zeros((N_ATT_HEADS, WIN), F32)
        for kvh in range(N_KV_HEADS):
            cs = slice(kvh * HEAD_DIM, (kvh + 1) * HEAD_DIM)
            sk = lax.dot_general(q, kk[:, cs], (((1,), (1,)), ((), ())), preferred_element_type=F32)
            s = jnp.where(head_grp == kvh, sk, s)
        s = s + bias_ref[g]
        s0 = s_self + bias0_ref[g]
        mx = jnp.maximum(jnp.max(s, axis=-1, keepdims=True), s0)
        lse = mx + jnp.log(jnp.sum(jnp.exp(s - mx), axis=-1, keepdims=True) + jnp.exp(s0 - mx))
        scores.append(s); selfs.append(s0); lses.append(lse)
    top = functools.reduce(jnp.maximum, lses)
    es = [jnp.exp(l - top) for l in lses]
    tot = functools.reduce(jnp.add, es)
    o = jnp.zeros((N_ATT_HEADS, HEAD_DIM), F32)
    for s, s0, lse, e, dil in zip(scores, selfs, lses, es, DILATIONS):
        wgt = e / tot
        p = (jnp.exp(s - lse) * wgt).astype(BF16)
        p0 = (jnp.exp(s0 - lse) * wgt).astype(BF16).astype(F32)
        vv = taps(vc_ref, dil)
        o = o + p0 * vn
        for kvh in range(N_KV_HEADS):
            cs = slice(kvh * HEAD_DIM, (kvh + 1) * HEAD_DIM)
            ok = jnp.dot(p, vv[:, cs], preferred_element_type=F32)
            o = o + jnp.where(head_grp == kvh, ok, 0.0)
    o_ref[0] = o


def _attn_sample(q, k_new, v_new, k_cache, v_cache, bias_s, bias0_s):
    n, w = k_cache.shape[0], k_cache.shape[1]
    assert w % (max(DILATIONS) * WIN) == 0
    tok = lambda b: (b, 0, 0)
    head = pl.BlockSpec((1, N_ATT_HEADS, HEAD_DIM), tok)
    window = pl.BlockSpec((1, w, KV_WIDTH), tok)
    return pl.pallas_call(
        _attn_sample_kernel,
        grid=(n,),
        in_specs=[head, head, head, window, window,
                  _const_spec((len(DILATIONS), N_ATT_HEADS, WIN)),
                  _const_spec((len(DILATIONS), N_ATT_HEADS, 1))],
        out_specs=head,
        out_shape=jax.ShapeDtypeStruct((n, N_ATT_HEADS, HEAD_DIM), F32),
        compiler_params=_cparams(("parallel",)),
        name="attn_sample",
    )(q, jnp.repeat(k_new, KV_REP, axis=1), jnp.repeat(v_new, KV_REP, axis=1), k_cache, v_cache,
      bias_s, bias0_s)


def _conv_sample_kernel(xbc_ref, b0_ref, b1_ref, b2_ref, cw_ref, cb_ref, dt_ref, dtb_ref, alog_ref,
                        e_ref, xa_ref, xdt_ref, decay_ref):
    acc = cb_ref[...] + xbc_ref[...] * cw_ref[CONV_W - 1:CONV_W, :]
    for i, buf in enumerate((b0_ref, b1_ref, b2_ref)):
        acc = acc + buf[...] * cw_ref[i:i + 1, :]
    xa = _silu(acc)
    xa_ref[...] = xa
    dt = _softplus(dt_ref[...] + dtb_ref[...])
    decay_ref[...] = jnp.exp(dt * (-jnp.exp(alog_ref[...])))
    dt_full = jnp.dot(dt, e_ref[...], preferred_element_type=F32, precision=HIGHEST)
    xdt_ref[...] = xa[:, :SSM_WIDTH] * dt_full


def _conv_sample(xbc, conv_buf, conv_w, conv_b, dt_raw, dt_bias, a_log):
    n = xbc.shape[0]
    args = (xbc, conv_buf[:, 0], conv_buf[:, 1], conv_buf[:, 2], conv_w, conv_b.reshape(1, CONV_DIM),
            dt_raw, _pad_lanes(dt_bias), _pad_lanes(a_log),
            _head_expand())
    return pl.pallas_call(
        _conv_sample_kernel,
        grid=(1,),
        in_specs=[_const_spec(a.shape) for a in args],
        out_specs=[_const_spec((n, CONV_DIM)), _const_spec((n, SSM_WIDTH)), _const_spec((n, LANES))],
        out_shape=[jax.ShapeDtypeStruct((n, CONV_DIM), F32), jax.ShapeDtypeStruct((n, SSM_WIDTH), F32),
                   jax.ShapeDtypeStruct((n, LANES), F32)],
        compiler_params=_cparams(("arbitrary",)),
        name="conv_sample",
    )(*args)


def _ssm_sample_kernel(xdt_ref, decay_ref, b_ref, c_ref, h0_ref, hn_ref, y_ref):
    for g in range(N_SSM_GROUPS):
        hs = slice(g * HEADS_PER_GROUP, (g + 1) * HEADS_PER_GROUP)
        hn = decay_ref[0, hs] * h0_ref[0, hs] + xdt_ref[0, hs] * b_ref[0, g]
        hn_ref[0, hs] = hn
        c_row = c_ref[0, g].astype(BF16).astype(F32)
        y_ref[0, hs] = jnp.sum(hn.astype(BF16).astype(F32) * c_row, axis=-1, keepdims=True)


def _ssm_sample(xdt, decay, bmat, cmat, h0):
    n = xdt.shape[0]
    p = SSM_HEAD_DIM
    tok4 = lambda b: (b, 0, 0, 0)
    hn, y = pl.pallas_call(
        _ssm_sample_kernel,
        grid=(n,),
        in_specs=[pl.BlockSpec((1, N_SSM_HEADS, p, 1), tok4),
                  pl.BlockSpec((1, N_SSM_HEADS, 1, 1), tok4),
                  pl.BlockSpec((1, N_SSM_GROUPS, 1, D_STATE), tok4),
                  pl.BlockSpec((1, N_SSM_GROUPS, 1, D_STATE), tok4),
                  pl.BlockSpec((1, N_SSM_HEADS, p, D_STATE), tok4)],
        out_specs=[pl.BlockSpec((1, N_SSM_HEADS, p, D_STATE), tok4),
                   pl.BlockSpec((1, N_SSM_HEADS, p, 1), tok4)],
        out_shape=[jax.ShapeDtypeStruct((n, N_SSM_HEADS, p, D_STATE), F32),
                   jax.ShapeDtypeStruct((n, N_SSM_HEADS, p, 1), F32)],
        compiler_params=_cparams(("parallel",)),
        name="ssm_sample",
    )(xdt.reshape(n, N_SSM_HEADS, p, 1), decay[:, :N_SSM_HEADS].reshape(n, N_SSM_HEADS, 1, 1),
      bmat.reshape(n, N_SSM_GROUPS, 1, D_STATE), cmat.reshape(n, N_SSM_GROUPS, 1, D_STATE), h0)
    return hn, y.reshape(n, SSM_WIDTH)


def _outproj_sample_kernel(att_ref, y_ref, xs_ref, z_ref, dskip_ref, gn_ref, x_ref, w_ref, g_ref,
                           wr_ref, br_ref, x1_ref, xn_ref, lg_ref):
    y = y_ref[...] + dskip_ref[...] * xs_ref[...]
    u = y * _silu(z_ref[...])
    gw = SSM_WIDTH // N_SSM_GROUPS
    parts = []
    for g in range(N_SSM_GROUPS):
        ug = u[:, g * gw:(g + 1) * gw]
        parts.append(ug * lax.rsqrt(jnp.mean(ug * ug, axis=-1, keepdims=True) + EPS)
                     * gn_ref[:, g * gw:(g + 1) * gw])
    mix = jnp.concatenate([att_ref[...]] + parts, axis=-1).astype(BF16)
    x1 = x_ref[...] + jnp.dot(mix, w_ref[...], preferred_element_type=F32)
    x1_ref[...] = x1
    xn = _rms(x1, g_ref[...]).astype(BF16)
    xn_ref[...] = xn
    lg_ref[...] = _router(xn, wr_ref, br_ref)


def _outproj_sample(att, y, xs, z, d_skip, ssd_norm, x2d, w_bf, g, wr, br):
    n, d = x2d.shape
    args = (att, y, xs, z, jnp.repeat(d_skip, SSM_HEAD_DIM).reshape(1, SSM_WIDTH),
            ssd_norm.reshape(1, SSM_WIDTH), x2d, w_bf, g.reshape(1, d), wr, br)
    return pl.pallas_call(
        _outproj_sample_kernel,
        grid=(1,),
        in_specs=[_const_spec(a.shape) for a in args],
        out_specs=[_const_spec((n, d)), _const_spec((n, d)), _const_spec((n, LANES))],
        out_shape=[jax.ShapeDtypeStruct((n, d), F32), jax.ShapeDtypeStruct((n, d), BF16),
                   jax.ShapeDtypeStruct((n, LANES), F32)],
        compiler_params=_cparams(("arbitrary",)),
        name="outproj_sample",
    )(*args)


TT = 256
CHUNK_ALIGN = SUBLANES
LROWS = -(-(TT * TOP_K + N_EXPERTS * (CHUNK_ALIGN - 1)) // TT) * TT
CHUNK_SIZES = tuple(1 << b for b in range(TT.bit_length() - 1, CHUNK_ALIGN.bit_length() - 2, -1))
SUB = 256
RG = 5 * SUB
TF = 256
VMEM_LIMIT_FFN = 60000 * 1024


def _chunk_loop(cnt_ref, loff_ref, dest_ref, tile, fn):
    def per_expert(e, carry):
        idx = tile * N_EXPERTS + e
        n, off, dst = cnt_ref[idx], loff_ref[idx], dest_ref[idx]
        for size in CHUNK_SIZES:
            take = (n & size) != 0

            @pl.when(take)
            def _(off=off, dst=dst, size=size):
                fn(pl.multiple_of(off, CHUNK_ALIGN), pl.multiple_of(dst, CHUNK_ALIGN), size)

            step = jnp.where(take, size, 0)
            off, dst = off + step, dst + step
        return carry

    lax.fori_loop(0, N_EXPERTS, per_expert, 0)


def _dispatch_kernel(cnt_ref, loff_ref, dest_ref, xp_ref, xs_ref, lpos_ref, gate_ref, out_hbm, buf, sem):
    t = pl.program_id(0)
    last = pl.num_programs(0) - 1
    d = xp_ref.shape[1]
    x = jnp.where(t == last, xs_ref[...], xp_ref[...])
    rows = lax.broadcasted_iota(jnp.int32, (LROWS, TT), 0)
    onehot = jnp.zeros((LROWS, TT), F32)
    wcol = jnp.zeros((LROWS, 1), F32)
    for k in range(TOP_K):
        hit = jnp.where(rows == lpos_ref[0, k:k + 1, :], 1.0, 0.0)
        onehot = onehot + hit
        wcol = wcol + jnp.sum(hit * gate_ref[0, k:k + 1, :], axis=-1, keepdims=True)
    tile_buf = buf.at[t % 2]
    tile_buf[:, 0:d] = jnp.dot(onehot.astype(BF16), x, preferred_element_type=F32)
    tile_buf[:, d:d + LANES] = jnp.broadcast_to(wcol, (LROWS, LANES))

    def copy(tile, off, dst, size):
        return pltpu.make_async_copy(buf.at[tile % 2, pl.ds(off, size)], out_hbm.at[pl.ds(dst, size)],
                                     sem.at[tile % 2])

    _chunk_loop(cnt_ref, loff_ref, dest_ref, t, lambda o, g, s: copy(t, o, g, s).start())

    @pl.when(t > 0)
    def _():
        _chunk_loop(cnt_ref, loff_ref, dest_ref, t - 1, lambda o, g, s: copy(t - 1, o, g, s).wait())

    @pl.when(t == last)
    def _():
        _chunk_loop(cnt_ref, loff_ref, dest_ref, t, lambda o, g, s: copy(t, o, g, s).wait())


def _dispatch(tabs, xn_p, xn_s, lpos_t, gate_t, n_rows):
    nt = lpos_t.shape[0]
    d = xn_p.shape[1]
    last_p = xn_p.shape[0] // TT - 1
    grid_spec = pltpu.PrefetchScalarGridSpec(
        num_scalar_prefetch=3,
        grid=(nt,),
        in_specs=[pl.BlockSpec((TT, d), lambda t, *_: (jnp.minimum(t, last_p), 0)),
                  pl.BlockSpec((TT, d), lambda t, *_: (0, 0)),
                  pl.BlockSpec((1, TOP_K, TT), lambda t, *_: (t, 0, 0)),
                  pl.BlockSpec((1, TOP_K, TT), lambda t, *_: (t, 0, 0))],
        out_specs=pl.BlockSpec(memory_space=pl.ANY),
        scratch_shapes=[pltpu.VMEM((2, LROWS, d + LANES), F32), pltpu.SemaphoreType.DMA((2,))],
    )
    return pl.pallas_call(
        _dispatch_kernel,
        grid_spec=grid_spec,
        out_shape=jax.ShapeDtypeStruct((n_rows, d + LANES), F32),
        compiler_params=_cparams(("arbitrary",)),
        name="moe_dispatch",
    )(*tabs, xn_p, xn_s, lpos_t, gate_t)


def _ffn_kernel(ge_ref, gs_ref, gn_ref, ng_ref, xs_hbm, wgu_hbm, wd_hbm, bgu_ref, bdn_ref, out_hbm,
                xbuf, acc, ostage, wg_st, wu_st, wd_st, wg_bf, wu_bf, wd_bf, sem_w, sem_x, sem_o):
    d = acc.shape[1]
    d_ff = wd_hbm.shape[1]
    nf = d_ff // TF
    n_groups = ng_ref[0]
    total = n_groups * nf

    def w_copies(s, slot):
        g = s // nf
        f = s - g * nf
        e = ge_ref[g]
        c0 = pl.multiple_of(f * TF, TF)
        return (pltpu.make_async_copy(wgu_hbm.at[e, :, pl.ds(c0, TF)], wg_st.at[slot], sem_w.at[slot, 0]),
                pltpu.make_async_copy(wgu_hbm.at[e, :, pl.ds(d_ff + c0, TF)], wu_st.at[slot], sem_w.at[slot, 1]),
                pltpu.make_async_copy(wd_hbm.at[e, pl.ds(c0, TF), :], wd_st.at[slot], sem_w.at[slot, 2]))

    def x_copy(g, j):
        r0 = pl.multiple_of(j * SUB, SUB)
        return pltpu.make_async_copy(xs_hbm.at[pl.ds(pl.multiple_of(gs_ref[g] + r0, CHUNK_ALIGN), SUB)],
                                     xbuf.at[g % 2, pl.ds(r0, SUB)], sem_x)

    def o_copy(g, j):
        r0 = pl.multiple_of(j * SUB, SUB)
        return pltpu.make_async_copy(ostage.at[j % 2],
                                     out_hbm.at[pl.ds(pl.multiple_of(gs_ref[g] + r0, CHUNK_ALIGN), SUB)],
                                     sem_o.at[j % 2])

    def loop(n, fn):
        lax.fori_loop(0, n, lambda j, c: (fn(j), c)[1], 0)

    def drain_stores(g):
        n = gn_ref[g]
        for back in (1, 2):
            @pl.when(n >= back)
            def _():
                o_copy(g, n - back).wait()

    @pl.when(total > 0)
    def _():
        for c in w_copies(0, 0):
            c.start()
        loop(gn_ref[0], lambda j: x_copy(0, j).start())

    def item(s, carry):
        slot = s % 2
        g = s // nf
        f = s - g * nf
        e = ge_ref[g]
        nsub = gn_ref[g]
        xg = xbuf.at[g % 2]

        @pl.when(s + 1 < total)
        def _():
            for c in w_copies(s + 1, 1 - slot):
                c.start()

        @pl.when(f == 0)
        def _():
            loop(nsub, lambda j: x_copy(g, j).wait())

        @pl.when(jnp.logical_and(f == 1, g + 1 < n_groups))
        def _():
            loop(gn_ref[g + 1], lambda j: x_copy(g + 1, j).start())

        for c in w_copies(s, slot):
            c.wait()
        bg = bgu_ref[pl.ds(e * 2 * nf + f, 1), :]
        bu = bgu_ref[pl.ds(e * 2 * nf + nf + f, 1), :]

        def sub_block(j, phase, cast=False):
            rs = pl.ds(pl.multiple_of(j * SUB, SUB), SUB)
            x = xg[rs, 0:d].astype(BF16)
            if cast:
                wg, wu, wd = (st[slot].astype(BF16) for st in (wg_st, wu_st, wd_st))
                wg_bf[...], wu_bf[...], wd_bf[...] = wg, wu, wd
            else:
                wg, wu, wd = wg_bf[...], wu_bf[...], wd_bf[...]
            hg = jnp.dot(x, wg, preferred_element_type=F32) + bg
            hu = jnp.dot(x, wu, preferred_element_type=F32) + bu
            gg = jnp.minimum(hg, SWIGLU_LIMIT)
            uu = jnp.clip(hu, -SWIGLU_LIMIT, SWIGLU_LIMIT)
            act = gg * (1.0 / (1.0 + jnp.exp(-SWIGLU_ALPHA * gg))) * (uu + 1.0)
            part = jnp.dot(act.astype(BF16), wd, preferred_element_type=F32)
            if phase == "first":
                acc[rs, :] = part + bdn_ref[pl.ds(e, 1), :]
            elif phase == "middle":
                acc[rs, :] += part
            else:
                @pl.when(j >= 2)
                def _():
                    o_copy(g, j - 2).wait()

                ostage[j % 2] = (acc[rs, :] + part) * xg[rs, d:d + 1]
                o_copy(g, j).start()

        def all_sub_blocks(phase):
            sub_block(jnp.int32(0), phase, cast=True)

            def pair(p):
                sub_block(2 * p + 1, phase)
                sub_block(2 * p + 2, phase)

            loop((nsub - 1) // 2, pair)

            @pl.when(nsub % 2 == 0)
            def _():
                sub_block(nsub - 1, phase)

        @pl.when(f == 0)
        def _():
            all_sub_blocks("first")

        @pl.when(jnp.logical_and(f > 0, f < nf - 1))
        def _():
            all_sub_blocks("middle")

        @pl.when(f == nf - 1)
        def _():
            @pl.when(g > 0)
            def _():
                drain_stores(g - 1)

            all_sub_blocks("last")

        return carry

    lax.fori_loop(0, total, item, 0)

    @pl.when(total > 0)
    def _():
        drain_stores(n_groups - 1)


def _moe_ffn(groups, x_sorted, w_gate_up, b_gu, w_down, b_dn):
    n_rows = x_sorted.shape[0]
    _, d_ff, d = w_down.shape
    nf = d_ff // TF
    assert nf >= 2
    bgu2 = b_gu.reshape(N_EXPERTS * 2 * nf, TF)
    grid_spec = pltpu.PrefetchScalarGridSpec(
        num_scalar_prefetch=4,
        grid=(1,),
        in_specs=[pl.BlockSpec(memory_space=pl.ANY), pl.BlockSpec(memory_space=pl.ANY),
                  pl.BlockSpec(memory_space=pl.ANY),
                  pl.BlockSpec(bgu2.shape, lambda i, *_: (0, 0), pipeline_mode=pl.Buffered(1)),
                  pl.BlockSpec(b_dn.shape, lambda i, *_: (0, 0), pipeline_mode=pl.Buffered(1))],
        out_specs=pl.BlockSpec(memory_space=pl.ANY),
        scratch_shapes=[pltpu.VMEM((2, RG, d + LANES), F32), pltpu.VMEM((RG, d), F32),
                        pltpu.VMEM((2, SUB, d), F32),
                        pltpu.VMEM((2, d, TF), F32), pltpu.VMEM((2, d, TF), F32), pltpu.VMEM((2, TF, d), F32),
                        pltpu.VMEM((d, TF), BF16), pltpu.VMEM((d, TF), BF16), pltpu.VMEM((TF, d), BF16),
                        pltpu.SemaphoreType.DMA((2, 3)), pltpu.SemaphoreType.DMA(()),
                        pltpu.SemaphoreType.DMA((2,))],
    )
    return pl.pallas_call(
        _ffn_kernel,
        grid_spec=grid_spec,
        out_shape=jax.ShapeDtypeStruct((n_rows, d), F32),
        compiler_params=pltpu.CompilerParams(dimension_semantics=("arbitrary",),
                                             vmem_limit_bytes=VMEM_LIMIT_FFN),
        name="moe_ffn",
    )(*groups, x_sorted, w_gate_up, w_down, bgu2, b_dn)


def _route(logits, m_pad):
    m = logits.shape[0]
    nt = m_pad // TT
    top_v, top_i = lax.top_k(logits, TOP_K)
    gate = jnp.pad(jax.nn.softmax(top_v, axis=-1), ((0, m_pad - m), (0, 0)))
    top_i = jnp.pad(top_i.astype(jnp.int32), ((0, m_pad - m), (0, 0)), constant_values=-1)
    chosen = (top_i[:, :, None] == jnp.arange(N_EXPERTS, dtype=jnp.int32)).astype(jnp.int32)
    tiles = chosen.sum(axis=1).reshape(nt, TT, N_EXPERTS)
    cnt = (tiles.sum(axis=1) + CHUNK_ALIGN - 1) // CHUNK_ALIGN * CHUNK_ALIGN
    loff = jnp.cumsum(cnt, axis=1) - cnt
    seg = cnt.sum(axis=0)
    seg_start = jnp.cumsum(seg) - seg
    dest = seg_start[None, :] + jnp.cumsum(cnt, axis=0) - cnt
    rank = jnp.cumsum(tiles, axis=1) - tiles
    lpos_all = (loff[:, None, :] + rank).reshape(m_pad, N_EXPERTS)
    lpos = (lpos_all[:, None, :] * chosen).sum(axis=-1)
    lpos = jnp.where(top_i >= 0, lpos, -1).astype(jnp.int32)
    n_rows_bound = nt * LROWS
    n_grp_max = n_rows_bound // RG + N_EXPERTS
    grp = (seg + RG - 1) // RG
    grp_end = jnp.cumsum(grp)
    gi = jnp.arange(n_grp_max, dtype=jnp.int32)
    g_exp = jnp.minimum(jnp.searchsorted(grp_end, gi, side='right'), N_EXPERTS - 1).astype(jnp.int32)
    within = gi - (grp_end[g_exp] - grp[g_exp])
    g_start = (seg_start[g_exp] + within * RG).astype(jnp.int32)
    g_rows = jnp.clip(seg[g_exp] - within * RG, 0, RG)
    g_nsub = jnp.where(gi < grp_end[-1], (g_rows + SUB - 1) // SUB, 0).astype(jnp.int32)
    groups = (g_exp, g_start, g_nsub, grp_end[-1:].astype(jnp.int32))
    tabs = tuple(a.reshape(-1).astype(jnp.int32) for a in (cnt, loff, dest))
    return tabs, lpos, gate, groups, n_rows_bound + SUB


def _combine_kernel(cnt_ref, loff_ref, dest_ref, src_hbm, lpos_ref, x1p_ref, x1s_ref, g_ref,
                    yp_ref, ys_ref, buf, sem):
    t = pl.program_id(0)
    last = pl.num_programs(0) - 1

    def copy(tile, off, src, size):
        return pltpu.make_async_copy(src_hbm.at[pl.ds(src, size)], buf.at[tile % 2, pl.ds(off, size)],
                                     sem.at[tile % 2])

    @pl.when(t == 0)
    def _():
        buf[...] = jnp.zeros_like(buf)
        _chunk_loop(cnt_ref, loff_ref, dest_ref, t, lambda o, g, s: copy(t, o, g, s).start())

    @pl.when(t < last)
    def _():
        _chunk_loop(cnt_ref, loff_ref, dest_ref, t + 1, lambda o, g, s: copy(t + 1, o, g, s).start())

    _chunk_loop(cnt_ref, loff_ref, dest_ref, t, lambda o, g, s: copy(t, o, g, s).wait())
    cols = lax.broadcasted_iota(jnp.int32, (TT, LROWS), 1)
    sel = jnp.zeros((TT, LROWS), F32)
    for k in range(TOP_K):
        sel = sel + jnp.where(cols == lpos_ref[:, k:k + 1], 1.0, 0.0)
    f = jnp.dot(sel.astype(BF16), buf[t % 2].astype(BF16), preferred_element_type=F32)

    @pl.when(t < last)
    def _():
        yp_ref[...] = _rms(x1p_ref[...] + f, g_ref[...])

    @pl.when(t == last)
    def _():
        ys_ref[...] = _rms(x1s_ref[...] + f, g_ref[...])


def _combine(tabs, out_sorted, lpos, x1p, x1s, g):
    d = x1p.shape[1]
    nt = lpos.shape[0] // TT
    last_p = x1p.shape[0] // TT - 1
    prompt = pl.BlockSpec((TT, d), lambda t, *_: (jnp.minimum(t, last_p), 0))
    sample = pl.BlockSpec((TT, d), lambda t, *_: (0, 0))
    grid_spec = pltpu.PrefetchScalarGridSpec(
        num_scalar_prefetch=3,
        grid=(nt,),
        in_specs=[pl.BlockSpec(memory_space=pl.ANY),
                  pl.BlockSpec((TT, TOP_K), lambda t, *_: (t, 0)),
                  prompt, sample,
                  pl.BlockSpec((1, d), lambda t, *_: (0, 0))],
        out_specs=[prompt, sample],
        scratch_shapes=[pltpu.VMEM((2, LROWS, d), F32), pltpu.SemaphoreType.DMA((2,))],
    )
    return pl.pallas_call(
        _combine_kernel,
        grid_spec=grid_spec,
        out_shape=[jax.ShapeDtypeStruct(x1p.shape, F32), jax.ShapeDtypeStruct((TT, d), F32)],
        compiler_params=_cparams(("arbitrary",)),
        name="moe_combine",
    )(*tabs, out_sorted, lpos, x1p, x1s, g.reshape(1, d))


def _t5_bucket(dist):
    max_exact = N_BUCKETS // 2
    dd = dist.astype(F32)
    large = max_exact + (jnp.log(jnp.maximum(dd, 1.0) / max_exact)
                         / math.log(BUCKET_MAX_DIST / max_exact) * (N_BUCKETS - max_exact)).astype(jnp.int32)
    large = jnp.minimum(large, N_BUCKETS - 1)
    return jnp.where(dist < max_exact, dist, large)


def _bias_tables(rel_bias):
    dist = jnp.asarray(np.arange(N_TAPS)[None, :] * np.array(DILATIONS)[:, None], jnp.int32)
    bias = jnp.transpose(rel_bias[_t5_bucket(dist)], (2, 0, 1)).astype(F32)
    by_branch = jnp.transpose(bias, (1, 0, 2))
    g, h = by_branch.shape[:2]
    row = jnp.concatenate([by_branch[:, :, ::-1], jnp.full((g, h, WIN), NEG, F32)], axis=-1)
    flat = jnp.broadcast_to(row[:, :, None, :], (g, h, WIN, 2 * WIN + 1)).reshape(g, h, -1)
    band = flat[:, :, :WIN * 2 * WIN].reshape(g, h, WIN, 2 * WIN)
    return band, by_branch[:, :, :0:-1], by_branch[:, :, 0:1]


def kernel(x_prompt, x_sample, cache_k_win, cache_v_win, state_conv, state_ssm, rel_bias, attn_norm, w_in, conv_w, conv_b, dt_bias, a_log, d_skip, ssd_norm, w_out, ffn_norm, w_router, b_router, w_gate_up, b_gate_up, w_down, b_down, final_norm):
    bp, tp, d = x_prompt.shape
    bs, ts, _ = x_sample.shape
    depth = w_in.shape[0]
    assert depth == 1 and ts == 1 and tp % (max(DILATIONS) * WIN) == 0
    keep = min(max(DILATIONS) * WIN, tp)
    band, samp, samp0 = _bias_tables(rel_bias)
    l = 0

    xp = x_prompt.reshape(bp * tp, d)
    xs = x_sample.reshape(bs * ts, d)
    w_in_bf = jnp.pad(w_in[l], ((0, 0), (0, IN_PROJ_PAD - IN_PROJ))).astype(BF16)
    w_out_bf = w_out[l].astype(BF16)
    wr = jnp.pad(w_router[l], ((0, 0), (0, LANES - N_EXPERTS))).astype(BF16)
    br = jnp.pad(b_router[l], (0, LANES - N_EXPERTS), constant_values=NEG).reshape(1, LANES)

    q, k, v, z, xbc, dt_raw = _inproj(xp, attn_norm[l], w_in_bf, TM_PROJ)
    q3 = q.reshape(bp, tp, ATT_WIDTH)
    k3 = k.reshape(bp, tp, KV_WIDTH)
    v3 = v.reshape(bp, tp, KV_WIDTH)
    att_parts = []
    for gi, dil in enumerate(DILATIONS):
        att_parts.extend(_attn_branch(q3, k3, v3, band[gi], dil))
    ssm, st_p = _ssd_prompt(xbc, dt_raw, z, conv_w[l], conv_b[l], dt_bias[l], a_log[l], d_skip[l],
                            ssd_norm[l], bp)
    x1p, xnp_, lgp = _outproj_prompt(att_parts, ssm, xp, w_out_bf, ffn_norm[l], wr, br)
    k_win_p = k3[:, tp - keep:].reshape(1, bp, keep, N_KV_HEADS, HEAD_DIM)
    v_win_p = v3[:, tp - keep:].reshape(1, bp, keep, N_KV_HEADS, HEAD_DIM)
    conv_p = xbc.reshape(bp, tp, CONV_DIM)[:, tp - (CONV_W - 1):][None]

    q_s, k_s, v_s, z_s, xbc_s, dt_s = _inproj(xs, attn_norm[l], w_in_bf, bs * ts)
    q_s = q_s.reshape(bs, N_ATT_HEADS, HEAD_DIM)
    k_s = k_s.reshape(bs, N_KV_HEADS, HEAD_DIM)
    v_s = v_s.reshape(bs, N_KV_HEADS, HEAD_DIM)
    wbuf = cache_k_win.shape[2]
    att_s = _attn_sample(q_s, k_s, v_s, cache_k_win[l].reshape(bs, wbuf, KV_WIDTH),
                         cache_v_win[l].reshape(bs, wbuf, KV_WIDTH), samp, samp0)
    xa_s, xdt_s, decay_s = _conv_sample(xbc_s, state_conv[l], conv_w[l], conv_b[l], dt_s, dt_bias[l],
                                        a_log[l])
    nbc = N_SSM_GROUPS * D_STATE
    h_s, y_s = _ssm_sample(xdt_s, decay_s, xa_s[:, SSM_WIDTH:SSM_WIDTH + nbc], xa_s[:, SSM_WIDTH + nbc:],
                           state_ssm[l])
    x1s, xns, lgs = _outproj_sample(att_s.reshape(bs, ATT_WIDTH), y_s, xa_s[:, :SSM_WIDTH], z_s,
                                    d_skip[l], ssd_norm[l], xs, w_out_bf, ffn_norm[l], wr, br)
    conv_s = jnp.concatenate([state_conv[l][:, 1:], xbc_s[:, None]], axis=1)[None]

    n_s = bs * ts
    assert (bp * tp) % TT == 0 and n_s <= TT
    m_pad = bp * tp + TT
    logits = jnp.concatenate([lgp[:, :N_EXPERTS], lgs[:, :N_EXPERTS]], axis=0)
    tabs, lpos, gate, groups, n_rows = _route(logits, m_pad)
    by_tile = lambda a: jnp.transpose(a.reshape(m_pad // TT, TT, TOP_K), (0, 2, 1))
    pad_rows = lambda a: jnp.pad(a, ((0, TT - n_s), (0, 0)))
    x_sorted = _dispatch(tabs, xnp_, pad_rows(xns), by_tile(lpos), by_tile(gate), n_rows)
    out_sorted = _moe_ffn(groups, x_sorted, w_gate_up[l], b_gate_up[l], w_down[l], b_down[l])
    y_p, y_s_out = _combine(tabs, out_sorted, lpos, x1p, pad_rows(x1s), final_norm)
    y_s_out = y_s_out[:n_s]

    return (y_p.reshape(bp, tp, d), y_s_out.reshape(bs, ts, d), k_win_p, v_win_p, conv_p, st_p[None],
            k_s.reshape(1, bs, ts, N_KV_HEADS, HEAD_DIM), v_s.reshape(1, bs, ts, N_KV_HEADS, HEAD_DIM),
            conv_s, h_s[None])
```

```python
import functools
import math

import jax
import jax.numpy as jnp
import numpy as np
from jax import lax
from jax.experimental import pallas as pl
from jax.experimental.pallas import tpu as pltpu

F32 = jnp.float32
BF16 = jnp.bfloat16
HIGHEST = lax.Precision.HIGHEST

LANES = 128
SUBLANES = 8
VMEM_LIMIT = 56 * 1024 * 1024

HEAD_DIM = 64
N_ATT_HEADS = 16
N_KV_HEADS = 4
KV_REP = N_ATT_HEADS // N_KV_HEADS
ATT_WIDTH = N_ATT_HEADS * HEAD_DIM
KV_WIDTH = N_KV_HEADS * HEAD_DIM
DILATIONS = (1, 4, 16)
N_TAPS = 129
WIN = N_TAPS - 1
ATT_SCALE = HEAD_DIM ** -0.5
N_BUCKETS = 32
BUCKET_MAX_DIST = 2048
SSM_HEAD_DIM = 64
N_SSM_HEADS = 16
SSM_WIDTH = N_SSM_HEADS * SSM_HEAD_DIM
N_SSM_GROUPS = 2
HEADS_PER_GROUP = N_SSM_HEADS // N_SSM_GROUPS
D_STATE = 128
CONV_W = 4
CONV_DIM = SSM_WIDTH + 2 * N_SSM_GROUPS * D_STATE
SSD_CHUNK = 128
N_EXPERTS = 32
TOP_K = 4
SWIGLU_LIMIT = 7.0
SWIGLU_ALPHA = 1.702
EPS = 1e-5
NEG = -1e30

Q0, K0, V0, Z0, X0, DT0 = 0, 1024, 1280, 1536, 2560, 4096
IN_PROJ = DT0 + N_SSM_HEADS
IN_PROJ_PAD = DT0 + LANES

TM_PROJ = 512
TM_OUT = 256
TM_MOE = 512
TF_MOE = 512
TM_FIN = 256


def _cparams(sem):
    return pltpu.CompilerParams(dimension_semantics=sem, vmem_limit_bytes=VMEM_LIMIT)


def _const_spec(shape):
    nd = len(shape)
    return pl.BlockSpec(shape, lambda *_: (0,) * nd, pipeline_mode=pl.Buffered(1))


def _rms(x, g):
    ms = jnp.mean(x * x, axis=-1, keepdims=True)
    return x * lax.rsqrt(ms + EPS) * g


def _silu(x):
    return x * (1.0 / (1.0 + jnp.exp(-x)))


def _inproj_kernel(x_ref, g_ref, w_ref, q_ref, k_ref, v_ref, z_ref, xbc_ref, dt_ref):
    xn = _rms(x_ref[...], g_ref[...]).astype(BF16)

    def mm(lo, hi):
        return jnp.dot(xn, w_ref[:, lo:hi], preferred_element_type=F32)

    q_ref[...] = (mm(Q0, K0) * ATT_SCALE).astype(BF16)
    k_ref[...] = mm(K0, V0)
    v_ref[...] = mm(V0, Z0)
    z_ref[...] = mm(Z0, X0)
    xbc_ref[...] = mm(X0, DT0)
    dt_ref[...] = mm(DT0, IN_PROJ_PAD)


def _inproj(x2d, g, w_bf, tm):
    m, d = x2d.shape
    widths = (ATT_WIDTH, KV_WIDTH, KV_WIDTH, SSM_WIDTH, CONV_DIM, LANES)
    dtypes = (BF16, F32, F32, F32, F32, F32)
    return pl.pallas_call(
        _inproj_kernel,
        grid=(m // tm,),
        in_specs=[pl.BlockSpec((tm, d), lambda i: (i, 0)),
                  _const_spec((1, d)),
                  _const_spec((d, IN_PROJ_PAD))],
        out_specs=[pl.BlockSpec((tm, w), lambda i: (i, 0)) for w in widths],
        out_shape=[jax.ShapeDtypeStruct((m, w), t) for w, t in zip(widths, dtypes)],
        compiler_params=_cparams(("parallel",)),
        name=f"inproj_{tm}",
    )(x2d, g.reshape(1, d), w_bf)


def _deinterleave_matrix(n, dil):
    p = np.zeros((n, n), np.float32)
    src = np.arange(n)
    p[(src % dil) * (n // dil) + src // dil, src] = 1.0
    return p


def _inproj_prompt_kernel(x_ref, g_ref, w_ref, p_mid_ref, p_far_ref, k_ref, v_ref, z_ref, xbc_ref, dt_ref,
                          q0_ref, kv0_ref, q1_ref, kv1_ref, q2_ref, kv2_ref):
    tm = x_ref.shape[0]
    xn = _rms(x_ref[...], g_ref[...]).astype(BF16)

    def mm(lo, hi):
        return jnp.dot(xn, w_ref[:, lo:hi], preferred_element_type=F32)

    q = (mm(Q0, K0) * ATT_SCALE).astype(BF16)
    k = mm(K0, V0)
    v = mm(V0, Z0)
    k_ref[...] = k
    v_ref[...] = v
    z_ref[...] = mm(Z0, X0)
    xbc_ref[...] = mm(X0, DT0)
    dt_ref[...] = mm(DT0, IN_PROJ_PAD)
    kv = jnp.concatenate([k, v], axis=-1).astype(BF16)
    q0_ref[0, 0] = q
    kv0_ref[0, 0] = kv
    for dil, p_ref, qd_ref, kvd_ref in ((DILATIONS[1], p_mid_ref, q1_ref, kv1_ref),
                                        (DILATIONS[2], p_far_ref, q2_ref, kv2_ref)):
        qp = jnp.dot(p_ref[...], q, preferred_element_type=F32).astype(BF16)
        kvp = jnp.dot(p_ref[...], kv, preferred_element_type=F32).astype(BF16)
        rows = tm // dil
        for r in range(dil):
            qd_ref[0, r] = qp[r * rows:(r + 1) * rows]
            kvd_ref[0, r] = kvp[r * rows:(r + 1) * rows]


def _inproj_prompt(x2d, g, w_bf, batch):
    m, d = x2d.shape
    tm = TM_PROJ
    seq = m // batch
    per_b = seq // tm
    assert DILATIONS[0] == 1 and seq % tm == 0 and tm % (max(DILATIONS) * 2 * SUBLANES) == 0
    widths = (KV_WIDTH, KV_WIDTH, SSM_WIDTH, CONV_DIM, LANES)
    rows = lambda w: pl.BlockSpec((tm, w), lambda i: (i, 0))
    split = lambda dil, w: pl.BlockSpec((1, dil, tm // dil, w), lambda i: (i // per_b, 0, i % per_b, 0))
    branch_specs, branch_shapes = [], []
    for dil in DILATIONS:
        for w in (ATT_WIDTH, 2 * KV_WIDTH):
            branch_specs.append(split(dil, w))
            branch_shapes.append(jax.ShapeDtypeStruct((batch, dil, seq // dil, w), BF16))
    perms = [jnp.asarray(_deinterleave_matrix(tm, dil), BF16) for dil in DILATIONS[1:]]
    return pl.pallas_call(
        _inproj_prompt_kernel,
        grid=(m // tm,),
        in_specs=[rows(d), _const_spec((1, d)), _const_spec((d, IN_PROJ_PAD)),
                  _const_spec((tm, tm)), _const_spec((tm, tm))],
        out_specs=[rows(w) for w in widths] + branch_specs,
        out_shape=[jax.ShapeDtypeStruct((m, w), F32) for w in widths] + branch_shapes,
        compiler_params=_cparams(("parallel",)),
        name="inproj_prompt",
    )(x2d, g.reshape(1, d), w_bf, *perms)


def _attn_kernel(q_ref, kvp_ref, kvc_ref, bias_ref, o_ref, lse_ref):
    first = pl.program_id(2) == 0
    lane = lax.broadcasted_iota(jnp.int32, (1, 2 * WIN), 1)
    prev_mask = jnp.where(jnp.logical_and(first, lane < WIN), NEG, 0.0)
    head_lane = lax.broadcasted_iota(jnp.int32, (WIN, LANES), 1)
    lse_tile = jnp.zeros((WIN, LANES), F32)
    for kvh in range(N_KV_HEADS):
        ks = slice(kvh * HEAD_DIM, (kvh + 1) * HEAD_DIM)
        vs = slice(KV_WIDTH + kvh * HEAD_DIM, KV_WIDTH + (kvh + 1) * HEAD_DIM)
        kw = jnp.concatenate([kvp_ref[0, 0, :, ks], kvc_ref[0, 0, :, ks]], axis=0)
        vw = jnp.concatenate([kvp_ref[0, 0, :, vs], kvc_ref[0, 0, :, vs]], axis=0)
        for pair in range(KV_REP // 2):
            outs = []
            for r in range(2):
                h = kvh * KV_REP + pair * 2 + r
                qh = q_ref[0, 0, :, h * HEAD_DIM:(h + 1) * HEAD_DIM]
                s = lax.dot_general(qh, kw, (((1,), (1,)), ((), ())), preferred_element_type=F32)
                s = s + bias_ref[h] + prev_mask
                mx = jnp.max(s, axis=-1, keepdims=True)
                p = jnp.exp(s - mx)
                l = jnp.sum(p, axis=-1, keepdims=True)
                o = jnp.dot(p.astype(BF16), vw, preferred_element_type=F32)
                outs.append(o * (1.0 / l))
                lse_tile = jnp.where(head_lane == h, mx + jnp.log(l), lse_tile)
            h0 = kvh * KV_REP + pair * 2
            o_ref[0, 0, :, h0 * HEAD_DIM:(h0 + 2) * HEAD_DIM] = jnp.concatenate(outs, axis=-1).astype(BF16)
    lse_ref[0, 0] = lse_tile


def _attn_branch(q, kv, bias_mat):
    b, dil, sub, _ = q.shape
    nb = sub // WIN
    cur = lambda bb, r, i: (bb, r, i, 0)
    prev = lambda bb, r, i: (bb, r, jnp.maximum(i - 1, 0), 0)
    return pl.pallas_call(
        _attn_kernel,
        grid=(b, dil, nb),
        in_specs=[pl.BlockSpec((1, 1, WIN, ATT_WIDTH), cur),
                  pl.BlockSpec((1, 1, WIN, 2 * KV_WIDTH), prev),
                  pl.BlockSpec((1, 1, WIN, 2 * KV_WIDTH), cur),
                  _const_spec((N_ATT_HEADS, WIN, 2 * WIN))],
        out_specs=[pl.BlockSpec((1, 1, WIN, ATT_WIDTH), cur),
                   pl.BlockSpec((1, 1, WIN, LANES), cur)],
        out_shape=[jax.ShapeDtypeStruct((b, dil, sub, ATT_WIDTH), BF16),
                   jax.ShapeDtypeStruct((b, dil, sub, LANES), F32)],
        compiler_params=_cparams(("parallel", "parallel", "arbitrary")),
        name=f"attn_dil{dil}",
    )(q, kv, kv, bias_mat)


def _softplus(x):
    return jnp.maximum(x, 0.0) + jnp.log(1.0 + jnp.exp(-jnp.abs(x)))


def _ssd_kernel(xbc_ref, dt_ref, z_ref, cw_ref, cb_ref, dtb_ref, alog_ref, dskip_ref, gn_ref, e_ref,
                y_ref, st_ref, ext_ref, state_ref):
    c = pl.program_id(1)
    L = SSD_CHUNK

    @pl.when(c == 0)
    def _():
        ext_ref[0:SUBLANES, :] = jnp.zeros((SUBLANES, CONV_DIM), F32)
        state_ref[...] = jnp.zeros_like(state_ref)

    ext_ref[SUBLANES:SUBLANES + L, :] = xbc_ref[...]
    acc = cb_ref[...] + ext_ref[SUBLANES:SUBLANES + L, :] * cw_ref[CONV_W - 1:CONV_W, :]
    for i in range(CONV_W - 1):
        off = SUBLANES - (CONV_W - 1) + i
        acc = acc + ext_ref[off:off + L, :] * cw_ref[i:i + 1, :]
    ext_ref[0:SUBLANES, :] = ext_ref[L:L + SUBLANES, :]
    xa = _silu(acc)

    dt = _softplus(dt_ref[...] + dtb_ref[...])
    da = dt * (-jnp.exp(alog_ref[...]))
    row = lax.broadcasted_iota(jnp.int32, (L, L), 0)
    col = lax.broadcasted_iota(jnp.int32, (L, L), 1)
    tri = row >= col
    a_cs = jnp.dot(tri.astype(F32), da, preferred_element_type=F32, precision=HIGHEST)
    a_cs_t = a_cs.T
    expand = e_ref[...]
    acs_full = jnp.dot(a_cs, expand, preferred_element_type=F32, precision=HIGHEST)
    dt_full = jnp.dot(dt, expand, preferred_element_type=F32, precision=HIGHEST)
    exp_acs = jnp.exp(acs_full)
    a_last = acs_full[L - 1:L, :]
    exp_last = exp_acs[L - 1:L, :]
    xs = xa[:, :SSM_WIDTH]
    xdt = xs * dt_full
    xw = xdt * jnp.exp(a_last - acs_full)

    for g in range(N_SSM_GROUPS):
        b0 = SSM_WIDTH + g * D_STATE
        c0 = SSM_WIDTH + N_SSM_GROUPS * D_STATE + g * D_STATE
        bg_t = xa[:, b0:b0 + D_STATE].T.astype(BF16)
        cg = xa[:, c0:c0 + D_STATE].astype(BF16)
        gram = jnp.dot(cg, bg_t, preferred_element_type=F32)
        for hh in range(HEADS_PER_GROUP):
            h = g * HEADS_PER_GROUP + hh
            hs = slice(h * SSM_HEAD_DIM, (h + 1) * SSM_HEAD_DIM)
            seg = jnp.where(tri, a_cs[:, h:h + 1] - a_cs_t[h:h + 1, :], NEG)
            scores = (gram * jnp.exp(seg)).astype(BF16)
            y_diag = jnp.dot(scores, xdt[:, hs].astype(BF16), preferred_element_type=F32)
            st = state_ref[h]
            y_off = jnp.dot(cg, st.astype(BF16), preferred_element_type=F32) * exp_acs[:, hs]
            y_ref[:, hs] = y_diag + y_off
            state_ref[h] = exp_last[:, hs] * st + jnp.dot(bg_t, xw[:, hs].astype(BF16),
                                                          preferred_element_type=F32)

    y = y_ref[...] + dskip_ref[...] * xs
    u = y * _silu(z_ref[...])
    gw = SSM_WIDTH // N_SSM_GROUPS
    parts = []
    for g in range(N_SSM_GROUPS):
        ug = u[:, g * gw:(g + 1) * gw]
        parts.append(ug * lax.rsqrt(jnp.mean(ug * ug, axis=-1, keepdims=True) + EPS))
    y_ref[...] = jnp.concatenate(parts, axis=-1) * gn_ref[...]

    @pl.when(c == pl.num_programs(1) - 1)
    def _():
        st_ref[0] = state_ref[...]


def _head_expand():
    e = np.zeros((LANES, SSM_WIDTH), np.float32)
    for h in range(N_SSM_HEADS):
        e[h, h * SSM_HEAD_DIM:(h + 1) * SSM_HEAD_DIM] = 1.0
    return jnp.asarray(e)


def _pad_lanes(v):
    return jnp.pad(v.astype(F32), (0, LANES - v.shape[0])).reshape(1, LANES)


def _ssd_prompt(xbc, dt_raw, z, conv_w, conv_b, dt_bias, a_log, d_skip, ssd_norm, batch):
    m = xbc.shape[0]
    nc = m // batch // SSD_CHUNK
    L = SSD_CHUNK
    rows = lambda b, c: (b * nc + c, 0)
    y, st = pl.pallas_call(
        _ssd_kernel,
        grid=(batch, nc),
        in_specs=[pl.BlockSpec((L, CONV_DIM), rows),
                  pl.BlockSpec((L, LANES), rows),
                  pl.BlockSpec((L, SSM_WIDTH), rows),
                  _const_spec((CONV_W, CONV_DIM)),
                  _const_spec((1, CONV_DIM)),
                  _const_spec((1, LANES)),
                  _const_spec((1, LANES)),
                  _const_spec((1, SSM_WIDTH)),
                  _const_spec((1, SSM_WIDTH)),
                  _const_spec((LANES, SSM_WIDTH))],
        out_specs=[pl.BlockSpec((L, SSM_WIDTH), rows),
                   pl.BlockSpec((1, N_SSM_HEADS, D_STATE, SSM_HEAD_DIM), lambda b, c: (b, 0, 0, 0))],
        out_shape=[jax.ShapeDtypeStruct((m, SSM_WIDTH), F32),
                   jax.ShapeDtypeStruct((batch, N_SSM_HEADS, D_STATE, SSM_HEAD_DIM), F32)],
        scratch_shapes=[pltpu.VMEM((SUBLANES + L, CONV_DIM), F32),
                        pltpu.VMEM((N_SSM_HEADS, D_STATE, SSM_HEAD_DIM), F32)],
        compiler_params=_cparams(("parallel", "arbitrary")),
        name="ssd_prompt",
    )(xbc, dt_raw, z, conv_w, conv_b.reshape(1, CONV_DIM), _pad_lanes(dt_bias), _pad_lanes(a_log),
      jnp.repeat(d_skip, SSM_HEAD_DIM).reshape(1, SSM_WIDTH), ssd_norm.reshape(1, SSM_WIDTH),
      _head_expand())
    return y, jnp.swapaxes(st, -1, -2)


def _router(xn_bf, wr_ref, br_ref):
    return jnp.dot(xn_bf, wr_ref[...], preferred_element_type=F32) + br_ref[...]


def _outproj_kernel(o0, l0, o1, l1, o2, l2, u1_ref, u2_ref, e_ref, ssm_ref, x_ref, w_ref, g_ref, wr_ref,
                    br_ref, x1_ref, xn_ref, lg_ref):
    def natural(o_ref, l_ref, u_ref):
        dil = o_ref.shape[1]
        if dil == 1:
            return o_ref[0, 0].astype(F32), l_ref[0, 0]
        o_cat = jnp.concatenate([o_ref[0, r] for r in range(dil)], axis=0)
        l_cat = jnp.concatenate([l_ref[0, r] for r in range(dil)], axis=0)
        hi = l_cat.astype(BF16)
        lo = (l_cat - hi.astype(F32)).astype(BF16)
        u = u_ref[...]
        return (jnp.dot(u, o_cat, preferred_element_type=F32),
                jnp.dot(u, hi, preferred_element_type=F32) + jnp.dot(u, lo, preferred_element_type=F32))

    branches = [natural(o0, l0, None), natural(o1, l1, u1_ref), natural(o2, l2, u2_ref)]
    lses = [jnp.dot(l, e_ref[...], preferred_element_type=F32, precision=HIGHEST) for _, l in branches]
    mx = functools.reduce(jnp.maximum, lses)
    ws = [jnp.exp(l - mx) for l in lses]
    num = functools.reduce(jnp.add, [w * o for w, (o, _) in zip(ws, branches)])
    att = num * (1.0 / functools.reduce(jnp.add, ws))
    y = jnp.dot(att.astype(BF16), w_ref[:ATT_WIDTH, :], preferred_element_type=F32)
    y = y + jnp.dot(ssm_ref[...].astype(BF16), w_ref[ATT_WIDTH:, :], preferred_element_type=F32)
    x1 = x_ref[...] + y
    x1_ref[...] = x1
    xn = _rms(x1, g_ref[...]).astype(BF16)
    xn_ref[...] = xn
    lg_ref[...] = _router(xn, wr_ref, br_ref)


def _outproj_prompt(att_parts, ssm, x2d, w_bf, g, wr, br):
    m, d = x2d.shape
    mix = w_bf.shape[0]
    tm = TM_OUT
    batch = att_parts[0][0].shape[0]
    per_b = m // batch // tm
    assert tm % (max(DILATIONS) * 2 * SUBLANES) == 0
    row = lambda w: pl.BlockSpec((tm, w), lambda i: (i, 0))
    split = lambda dil, w: pl.BlockSpec((1, dil, tm // dil, w), lambda i: (i // per_b, 0, i % per_b, 0))
    branch_specs = [split(o.shape[1], w) for o, _ in att_parts for w in (ATT_WIDTH, LANES)]
    interleave = [jnp.asarray(_deinterleave_matrix(tm, dil).T, BF16) for dil in DILATIONS[1:]]
    return pl.pallas_call(
        _outproj_kernel,
        grid=(m // tm,),
        in_specs=branch_specs + [_const_spec((tm, tm)), _const_spec((tm, tm)),
                                 _const_spec((LANES, ATT_WIDTH)), row(SSM_WIDTH), row(d),
                                 _const_spec((mix, d)), _const_spec((1, d)),
                                 _const_spec((d, LANES)), _const_spec((1, LANES))],
        out_specs=[row(d), row(d), row(LANES)],
        out_shape=[jax.ShapeDtypeStruct((m, d), F32), jax.ShapeDtypeStruct((m, d), BF16),
                   jax.ShapeDtypeStruct((m, LANES), F32)],
        compiler_params=_cparams(("parallel",)),
        name="outproj_prompt",
    )(*[a for part in att_parts for a in part], *interleave, _head_expand(), ssm, x2d, w_bf,
      g.reshape(1, d), wr, br)


def _attn_sample_kernel(q_ref, kn_ref, vn_ref, kc_ref, vc_ref, bias_ref, bias0_ref, o_ref):
    w = kc_ref.shape[1]
    q = q_ref[0]
    head_grp = lax.broadcasted_iota(jnp.int32, (N_ATT_HEADS, 1), 0) // KV_REP
    kn = kn_ref[0].astype(BF16).astype(F32)
    vn = vn_ref[0].astype(BF16).astype(F32)
    s_self = jnp.sum(q.astype(F32) * kn, axis=-1, keepdims=True)

    def taps(c_ref, dil):
        span = WIN * dil
        rows = c_ref[0, w - span:w, :].astype(BF16)
        if dil == 1:
            return rows
        tap = lax.broadcasted_iota(jnp.int32, (WIN, span), 0)
        col = lax.broadcasted_iota(jnp.int32, (WIN, span), 1)
        pick = jnp.where(col == tap * dil, 1.0, 0.0).astype(BF16)
        return jnp.dot(pick, rows, preferred_element_type=F32).astype(BF16)

    scores, selfs, lses = [], [], []
    for g, dil in enumerate(DILATIONS):
        kk = taps(kc_ref, dil)
        s = jnp.zeros((N_ATT_HEADS, WIN), F32)
        for kvh in range(N_KV_HEADS):
            cs = slice(kvh * HEAD_DIM, (kvh + 1) * HEAD_DIM)
            sk = lax.dot_general(q, kk[:, cs], (((1,), (1,)), ((), ())), preferred_element_type=F32)
            s = jnp.where(head_grp == kvh, sk, s)
        s = s + bias_ref[g]
        s0 = s_self + bias0_ref[g]
        mx = jnp.maximum(jnp.max(s, axis=-1, keepdims=True), s0)
        lse = mx + jnp.log(jnp.sum(jnp.exp(s - mx), axis=-1, keepdims=True) + jnp.exp(s0 - mx))
        scores.append(s); selfs.append(s0); lses.append(lse)
    top = functools.reduce(jnp.maximum, lses)
    es = [jnp.exp(l - top) for l in lses]
    tot = functools.reduce(jnp.add, es)
    o = jnp.zeros((N_ATT_HEADS, HEAD_DIM), F32)
    for s, s0, lse, e, dil in zip(scores, selfs, lses, es, DILATIONS):
        wgt = e / tot
        p = (jnp.exp(s - lse) * wgt).astype(BF16)
        p0 = (jnp.exp(s0 - lse) * wgt).astype(BF16).astype(F32)
        vv = taps(vc_ref, dil)
        o = o + p0 * vn
        for kvh in range(N_KV_HEADS):
            cs = slice(kvh * HEAD_DIM, (kvh + 1) * HEAD_DIM)
            ok = jnp.dot(p, vv[:, cs], preferred_element_type=F32)
            o = o + jnp.where(head_grp == kvh, ok, 0.0)
    o_ref[0] = o


def _attn_sample(q, k_new, v_new, k_cache, v_cache, bias_s, bias0_s):
    n, w = k_cache.shape[0], k_cache.shape[1]
    assert w % (max(DILATIONS) * WIN) == 0
    tok = lambda b: (b, 0, 0)
    head = pl.BlockSpec((1, N_ATT_HEADS, HEAD_DIM), tok)
    window = pl.BlockSpec((1, w, KV_WIDTH), tok)
    return pl.pallas_call(
        _attn_sample_kernel,
        grid=(n,),
        in_specs=[head, head, head, window, window,
                  _const_spec((len(DILATIONS), N_ATT_HEADS, WIN)),
                  _const_spec((len(DILATIONS), N_ATT_HEADS, 1))],
        out_specs=head,
        out_shape=jax.ShapeDtypeStruct((n, N_ATT_HEADS, HEAD_DIM), F32),
        compiler_params=_cparams(("parallel",)),
        name="attn_sample",
    )(q, jnp.repeat(k_new, KV_REP, axis=1), jnp.repeat(v_new, KV_REP, axis=1), k_cache, v_cache,
      bias_s, bias0_s)


def _conv_sample_kernel(xbc_ref, b0_ref, b1_ref, b2_ref, cw_ref, cb_ref, dt_ref, dtb_ref, alog_ref,
                        e_ref, xa_ref, xdt_ref, decay_ref):
    acc = cb_ref[...] + xbc_ref[...] * cw_ref[CONV_W - 1:CONV_W, :]
    for i, buf in enumerate((b0_ref, b1_ref, b2_ref)):
        acc = acc + buf[...] * cw_ref[i:i + 1, :]
    xa = _silu(acc)
    xa_ref[...] = xa
    dt = _softplus(dt_ref[...] + dtb_ref[...])
    decay_ref[...] = jnp.exp(dt * (-jnp.exp(alog_ref[...])))
    dt_full = jnp.dot(dt, e_ref[...], preferred_element_type=F32, precision=HIGHEST)
    xdt_ref[...] = xa[:, :SSM_WIDTH] * dt_full


def _conv_sample(xbc, conv_buf, conv_w, conv_b, dt_raw, dt_bias, a_log):
    n = xbc.shape[0]
    args = (xbc, conv_buf[:, 0], conv_buf[:, 1], conv_buf[:, 2], conv_w, conv_b.reshape(1, CONV_DIM),
            dt_raw, _pad_lanes(dt_bias), _pad_lanes(a_log),
            _head_expand())
    return pl.pallas_call(
        _conv_sample_kernel,
        grid=(1,),
        in_specs=[_const_spec(a.shape) for a in args],
        out_specs=[_const_spec((n, CONV_DIM)), _const_spec((n, SSM_WIDTH)), _const_spec((n, LANES))],
        out_shape=[jax.ShapeDtypeStruct((n, CONV_DIM), F32), jax.ShapeDtypeStruct((n, SSM_WIDTH), F32),
                   jax.ShapeDtypeStruct((n, LANES), F32)],
        compiler_params=_cparams(("arbitrary",)),
        name="conv_sample",
    )(*args)


def _ssm_sample_kernel(xdt_ref, decay_ref, b_ref, c_ref, h0_ref, hn_ref, y_ref):
    for g in range(N_SSM_GROUPS):
        hs = slice(g * HEADS_PER_GROUP, (g + 1) * HEADS_PER_GROUP)
        hn = decay_ref[0, hs] * h0_ref[0, hs] + xdt_ref[0, hs] * b_ref[0, g]
        hn_ref[0, hs] = hn
        c_row = c_ref[0, g].astype(BF16).astype(F32)
        y_ref[0, hs] = jnp.sum(hn.astype(BF16).astype(F32) * c_row, axis=-1, keepdims=True)


def _ssm_sample(xdt, decay, bmat, cmat, h0):
    n = xdt.shape[0]
    p = SSM_HEAD_DIM
    tok4 = lambda b: (b, 0, 0, 0)
    hn, y = pl.pallas_call(
        _ssm_sample_kernel,
        grid=(n,),
        in_specs=[pl.BlockSpec((1, N_SSM_HEADS, p, 1), tok4),
                  pl.BlockSpec((1, N_SSM_HEADS, 1, 1), tok4),
                  pl.BlockSpec((1, N_SSM_GROUPS, 1, D_STATE), tok4),
                  pl.BlockSpec((1, N_SSM_GROUPS, 1, D_STATE), tok4),
                  pl.BlockSpec((1, N_SSM_HEADS, p, D_STATE), tok4)],
        out_specs=[pl.BlockSpec((1, N_SSM_HEADS, p, D_STATE), tok4),
                   pl.BlockSpec((1, N_SSM_HEADS, p, 1), tok4)],
        out_shape=[jax.ShapeDtypeStruct((n, N_SSM_HEADS, p, D_STATE), F32),
                   jax.ShapeDtypeStruct((n, N_SSM_HEADS, p, 1), F32)],
        compiler_params=_cparams(("parallel",)),
        name="ssm_sample",
    )(xdt.reshape(n, N_SSM_HEADS, p, 1), decay[:, :N_SSM_HEADS].reshape(n, N_SSM_HEADS, 1, 1),
      bmat.reshape(n, N_SSM_GROUPS, 1, D_STATE), cmat.reshape(n, N_SSM_GROUPS, 1, D_STATE), h0)
    return hn, y.reshape(n, SSM_WIDTH)


def _outproj_sample_kernel(att_ref, y_ref, xs_ref, z_ref, dskip_ref, gn_ref, x_ref, w_ref, g_ref,
                           wr_ref, br_ref, x1_ref, xn_ref, lg_ref):
    y = y_ref[...] + dskip_ref[...] * xs_ref[...]
    u = y * _silu(z_ref[...])
    gw = SSM_WIDTH // N_SSM_GROUPS
    parts = []
    for g in range(N_SSM_GROUPS):
        ug = u[:, g * gw:(g + 1) * gw]
        parts.append(ug * lax.rsqrt(jnp.mean(ug * ug, axis=-1, keepdims=True) + EPS)
                     * gn_ref[:, g * gw:(g + 1) * gw])
    mix = jnp.concatenate([att_ref[...]] + parts, axis=-1).astype(BF16)
    x1 = x_ref[...] + jnp.dot(mix, w_ref[...], preferred_element_type=F32)
    x1_ref[...] = x1
    xn = _rms(x1, g_ref[...]).astype(BF16)
    xn_ref[...] = xn
    lg_ref[...] = _router(xn, wr_ref, br_ref)


def _outproj_sample(att, y, xs, z, d_skip, ssd_norm, x2d, w_bf, g, wr, br):
    n, d = x2d.shape
    args = (att, y, xs, z, jnp.repeat(d_skip, SSM_HEAD_DIM).reshape(1, SSM_WIDTH),
            ssd_norm.reshape(1, SSM_WIDTH), x2d, w_bf, g.reshape(1, d), wr, br)
    return pl.pallas_call(
        _outproj_sample_kernel,
        grid=(1,),
        in_specs=[_const_spec(a.shape) for a in args],
        out_specs=[_const_spec((n, d)), _const_spec((n, d)), _const_spec((n, LANES))],
        out_shape=[jax.ShapeDtypeStruct((n, d), F32), jax.ShapeDtypeStruct((n, d), BF16),
                   jax.ShapeDtypeStruct((n, LANES), F32)],
        compiler_params=_cparams(("arbitrary",)),
        name="outproj_sample",
    )(*args)


TT = 256
CHUNK_ALIGN = SUBLANES
LROWS = -(-(TT * TOP_K + N_EXPERTS * (CHUNK_ALIGN - 1)) // TT) * TT
CHUNK_SIZES = tuple(1 << b for b in range(TT.bit_length() - 1, CHUNK_ALIGN.bit_length() - 2, -1))
SUB = 256
RG = 5 * SUB
TF = 256
VMEM_LIMIT_FFN = 60000 * 1024


def _chunk_loop(cnt_ref, loff_ref, dest_ref, tile, fn):
    def per_expert(e, carry):
        idx = tile * N_EXPERTS + e
        n, off, dst = cnt_ref[idx], loff_ref[idx], dest_ref[idx]
        for size in CHUNK_SIZES:
            take = (n & size) != 0

            @pl.when(take)
            def _(off=off, dst=dst, size=size):
                fn(pl.multiple_of(off, CHUNK_ALIGN), pl.multiple_of(dst, CHUNK_ALIGN), size)

            step = jnp.where(take, size, 0)
            off, dst = off + step, dst + step
        return carry

    lax.fori_loop(0, N_EXPERTS, per_expert, 0)


def _dispatch_kernel(cnt_ref, loff_ref, dest_ref, xp_ref, xs_ref, lpos_ref, gate_ref, out_hbm, buf, sem):
    t = pl.program_id(0)
    last = pl.num_programs(0) - 1
    d = xp_ref.shape[1]
    x = jnp.where(t == last, xs_ref[...], xp_ref[...])
    rows = lax.broadcasted_iota(jnp.int32, (LROWS, TT), 0)
    onehot = jnp.zeros((LROWS, TT), F32)
    wcol = jnp.zeros((LROWS, 1), F32)
    for k in range(TOP_K):
        hit = jnp.where(rows == lpos_ref[0, k:k + 1, :], 1.0, 0.0)
        onehot = onehot + hit
        wcol = wcol + jnp.sum(hit * gate_ref[0, k:k + 1, :], axis=-1, keepdims=True)
    tile_buf = buf.at[t % 2]
    tile_buf[:, 0:d] = jnp.dot(onehot.astype(BF16), x, preferred_element_type=F32)
    tile_buf[:, d:d + LANES] = jnp.broadcast_to(wcol, (LROWS, LANES))

    def copy(tile, off, dst, size):
        return pltpu.make_async_copy(buf.at[tile % 2, pl.ds(off, size)], out_hbm.at[pl.ds(dst, size)],
                                     sem.at[tile % 2])

    _chunk_loop(cnt_ref, loff_ref, dest_ref, t, lambda o, g, s: copy(t, o, g, s).start())

    @pl.when(t > 0)
    def _():
        _chunk_loop(cnt_ref, loff_ref, dest_ref, t - 1, lambda o, g, s: copy(t - 1, o, g, s).wait())

    @pl.when(t == last)
    def _():
        _chunk_loop(cnt_ref, loff_ref, dest_ref, t, lambda o, g, s: copy(t, o, g, s).wait())


def _dispatch(tabs, xn_p, xn_s, lpos_t, gate_t, n_rows):
    nt = lpos_t.shape[0]
    d = xn_p.shape[1]
    last_p = xn_p.shape[0] // TT - 1
    grid_spec = pltpu.PrefetchScalarGridSpec(
        num_scalar_prefetch=3,
        grid=(nt,),
        in_specs=[pl.BlockSpec((TT, d), lambda t, *_: (jnp.minimum(t, last_p), 0)),
                  pl.BlockSpec((TT, d), lambda t, *_: (0, 0)),
                  pl.BlockSpec((1, TOP_K, TT), lambda t, *_: (t, 0, 0)),
                  pl.BlockSpec((1, TOP_K, TT), lambda t, *_: (t, 0, 0))],
        out_specs=pl.BlockSpec(memory_space=pl.ANY),
        scratch_shapes=[pltpu.VMEM((2, LROWS, d + LANES), F32), pltpu.SemaphoreType.DMA((2,))],
    )
    return pl.pallas_call(
        _dispatch_kernel,
        grid_spec=grid_spec,
        out_shape=jax.ShapeDtypeStruct((n_rows, d + LANES), F32),
        compiler_params=_cparams(("arbitrary",)),
        name="moe_dispatch",
    )(*tabs, xn_p, xn_s, lpos_t, gate_t)


def _ffn_kernel(ge_ref, gs_ref, gn_ref, ng_ref, xs_hbm, wgu_hbm, wd_hbm, bgu_ref, bdn_ref, out_hbm,
                xbuf, acc, ostage, wg_st, wu_st, wd_st, wg_bf, wu_bf, wd_bf, sem_w, sem_x, sem_o):
    d = acc.shape[1]
    d_ff = wd_hbm.shape[1]
    nf = d_ff // TF
    n_groups = ng_ref[0]
    total = n_groups * nf

    def w_copies(s, slot):
        g = s // nf
        f = s - g * nf
        e = ge_ref[g]
        c0 = pl.multiple_of(f * TF, TF)
        return (pltpu.make_async_copy(wgu_hbm.at[e, :, pl.ds(c0, TF)], wg_st.at[slot], sem_w.at[slot, 0]),
                pltpu.make_async_copy(wgu_hbm.at[e, :, pl.ds(d_ff + c0, TF)], wu_st.at[slot], sem_w.at[slot, 1]),
                pltpu.make_async_copy(wd_hbm.at[e, pl.ds(c0, TF), :], wd_st.at[slot], sem_w.at[slot, 2]))

    def x_copy(g, j):
        r0 = pl.multiple_of(j * SUB, SUB)
        return pltpu.make_async_copy(xs_hbm.at[pl.ds(pl.multiple_of(gs_ref[g] + r0, CHUNK_ALIGN), SUB)],
                                     xbuf.at[g % 2, pl.ds(r0, SUB)], sem_x)

    def o_copy(g, j):
        r0 = pl.multiple_of(j * SUB, SUB)
        return pltpu.make_async_copy(ostage.at[j % 2],
                                     out_hbm.at[pl.ds(pl.multiple_of(gs_ref[g] + r0, CHUNK_ALIGN), SUB)],
                                     sem_o.at[j % 2])

    def loop(n, fn):
        lax.fori_loop(0, n, lambda j, c: (fn(j), c)[1], 0)

    def drain_stores(g):
        n = gn_ref[g]
        for back in (1, 2):
            @pl.when(n >= back)
            def _():
                o_copy(g, n - back).wait()

    @pl.when(total > 0)
    def _():
        for c in w_copies(0, 0):
            c.start()
        loop(gn_ref[0], lambda j: x_copy(0, j).start())

    def item(s, carry):
        slot = s % 2
        g = s // nf
        f = s - g * nf
        e = ge_ref[g]
        nsub = gn_ref[g]
        xg = xbuf.at[g % 2]

        @pl.when(s + 1 < total)
        def _():
            for c in w_copies(s + 1, 1 - slot):
                c.start()

        @pl.when(f == 0)
        def _():
            loop(nsub, lambda j: x_copy(g, j).wait())

        @pl.when(jnp.logical_and(f == 1, g + 1 < n_groups))
        def _():
            loop(gn_ref[g + 1], lambda j: x_copy(g + 1, j).start())

        for c in w_copies(s, slot):
            c.wait()
        bg = bgu_ref[pl.ds(e * 2 * nf + f, 1), :]
        bu = bgu_ref[pl.ds(e * 2 * nf + nf + f, 1), :]

        def sub_block(j, phase, cast=False):
            rs = pl.ds(pl.multiple_of(j * SUB, SUB), SUB)
            x = xg[rs, 0:d].astype(BF16)
            if cast:
                wg, wu, wd = (st[slot].astype(BF16) for st in (wg_st, wu_st, wd_st))
                wg_bf[...], wu_bf[...], wd_bf[...] = wg, wu, wd
            else:
                wg, wu, wd = wg_bf[...], wu_bf[...], wd_bf[...]
            hg = jnp.dot(x, wg, preferred_element_type=F32) + bg
            hu = jnp.dot(x, wu, preferred_element_type=F32) + bu
            gg = jnp.minimum(hg, SWIGLU_LIMIT)
            uu = jnp.clip(hu, -SWIGLU_LIMIT, SWIGLU_LIMIT)
            act = gg * (1.0 / (1.0 + jnp.exp(-SWIGLU_ALPHA * gg))) * (uu + 1.0)
            part = jnp.dot(act.astype(BF16), wd, preferred_element_type=F32)
            if phase == "first":
                acc[rs, :] = part + bdn_ref[pl.ds(e, 1), :]
            elif phase == "middle":
                acc[rs, :] += part
            else:
                @pl.when(j >= 2)
                def _():
                    o_copy(g, j - 2).wait()

                ostage[j % 2] = (acc[rs, :] + part) * xg[rs, d:d + 1]
                o_copy(g, j).start()

        def all_sub_blocks(phase):
            sub_block(jnp.int32(0), phase, cast=True)

            def pair(p):
                sub_block(2 * p + 1, phase)
                sub_block(2 * p + 2, phase)

            loop((nsub - 1) // 2, pair)

            @pl.when(nsub % 2 == 0)
            def _():
                sub_block(nsub - 1, phase)

        @pl.when(f == 0)
        def _():
            all_sub_blocks("first")

        @pl.when(jnp.logical_and(f > 0, f < nf - 1))
        def _():
            all_sub_blocks("middle")

        @pl.when(f == nf - 1)
        def _():
            @pl.when(g > 0)
            def _():
                drain_stores(g - 1)

            all_sub_blocks("last")

        return carry

    lax.fori_loop(0, total, item, 0)

    @pl.when(total > 0)
    def _():
        drain_stores(n_groups - 1)


def _moe_ffn(groups, x_sorted, w_gate_up, b_gu, w_down, b_dn):
    n_rows = x_sorted.shape[0]
    _, d_ff, d = w_down.shape
    nf = d_ff // TF
    assert nf >= 2
    bgu2 = b_gu.reshape(N_EXPERTS * 2 * nf, TF)
    grid_spec = pltpu.PrefetchScalarGridSpec(
        num_scalar_prefetch=4,
        grid=(1,),
        in_specs=[pl.BlockSpec(memory_space=pl.ANY), pl.BlockSpec(memory_space=pl.ANY),
                  pl.BlockSpec(memory_space=pl.ANY),
                  pl.BlockSpec(bgu2.shape, lambda i, *_: (0, 0), pipeline_mode=pl.Buffered(1)),
                  pl.BlockSpec(b_dn.shape, lambda i, *_: (0, 0), pipeline_mode=pl.Buffered(1))],
        out_specs=pl.BlockSpec(memory_space=pl.ANY),
        scratch_shapes=[pltpu.VMEM((2, RG, d + LANES), F32), pltpu.VMEM((RG, d), F32),
                        pltpu.VMEM((2, SUB, d), F32),
                        pltpu.VMEM((2, d, TF), F32), pltpu.VMEM((2, d, TF), F32), pltpu.VMEM((2, TF, d), F32),
                        pltpu.VMEM((d, TF), BF16), pltpu.VMEM((d, TF), BF16), pltpu.VMEM((TF, d), BF16),
                        pltpu.SemaphoreType.DMA((2, 3)), pltpu.SemaphoreType.DMA(()),
                        pltpu.SemaphoreType.DMA((2,))],
    )
    return pl.pallas_call(
        _ffn_kernel,
        grid_spec=grid_spec,
        out_shape=jax.ShapeDtypeStruct((n_rows, d), F32),
        compiler_params=pltpu.CompilerParams(dimension_semantics=("arbitrary",),
                                             vmem_limit_bytes=VMEM_LIMIT_FFN),
        name="moe_ffn",
    )(*groups, x_sorted, w_gate_up, w_down, bgu2, b_dn)


def _route(logits, m_pad):
    m = logits.shape[0]
    nt = m_pad // TT
    top_v, top_i = lax.top_k(logits, TOP_K)
    gate = jnp.pad(jax.nn.softmax(top_v, axis=-1), ((0, m_pad - m), (0, 0)))
    top_i = jnp.pad(top_i.astype(jnp.int32), ((0, m_pad - m), (0, 0)), constant_values=-1)
    chosen = (top_i[:, :, None] == jnp.arange(N_EXPERTS, dtype=jnp.int32)).astype(jnp.int32)
    tiles = chosen.sum(axis=1).reshape(nt, TT, N_EXPERTS)
    cnt = (tiles.sum(axis=1) + CHUNK_ALIGN - 1) // CHUNK_ALIGN * CHUNK_ALIGN
    loff = jnp.cumsum(cnt, axis=1) - cnt
    seg = cnt.sum(axis=0)
    seg_start = jnp.cumsum(seg) - seg
    dest = seg_start[None, :] + jnp.cumsum(cnt, axis=0) - cnt
    rank = jnp.cumsum(tiles, axis=1) - tiles
    lpos_all = (loff[:, None, :] + rank).reshape(m_pad, N_EXPERTS)
    lpos = (lpos_all[:, None, :] * chosen).sum(axis=-1)
    lpos = jnp.where(top_i >= 0, lpos, -1).astype(jnp.int32)
    n_rows_bound = nt * LROWS
    n_grp_max = n_rows_bound // RG + N_EXPERTS
    grp = (seg + RG - 1) // RG
    grp_end = jnp.cumsum(grp)
    gi = jnp.arange(n_grp_max, dtype=jnp.int32)
    g_exp = jnp.minimum(jnp.searchsorted(grp_end, gi, side='right'), N_EXPERTS - 1).astype(jnp.int32)
    within = gi - (grp_end[g_exp] - grp[g_exp])
    g_start = (seg_start[g_exp] + within * RG).astype(jnp.int32)
    g_rows = jnp.clip(seg[g_exp] - within * RG, 0, RG)
    g_nsub = jnp.where(gi < grp_end[-1], (g_rows + SUB - 1) // SUB, 0).astype(jnp.int32)
    groups = (g_exp, g_start, g_nsub, grp_end[-1:].astype(jnp.int32))
    tabs = tuple(a.reshape(-1).astype(jnp.int32) for a in (cnt, loff, dest))
    return tabs, lpos, gate, groups, n_rows_bound + SUB


def _combine_kernel(cnt_ref, loff_ref, dest_ref, src_hbm, lpos_ref, x1p_ref, x1s_ref, g_ref,
                    yp_ref, ys_ref, buf, sem):
    t = pl.program_id(0)
    last = pl.num_programs(0) - 1

    def copy(tile, off, src, size):
        return pltpu.make_async_copy(src_hbm.at[pl.ds(src, size)], buf.at[tile % 2, pl.ds(off, size)],
                                     sem.at[tile % 2])

    @pl.when(t == 0)
    def _():
        buf[...] = jnp.zeros_like(buf)
        _chunk_loop(cnt_ref, loff_ref, dest_ref, t, lambda o, g, s: copy(t, o, g, s).start())

    @pl.when(t < last)
    def _():
        _chunk_loop(cnt_ref, loff_ref, dest_ref, t + 1, lambda o, g, s: copy(t + 1, o, g, s).start())

    _chunk_loop(cnt_ref, loff_ref, dest_ref, t, lambda o, g, s: copy(t, o, g, s).wait())
    cols = lax.broadcasted_iota(jnp.int32, (TT, LROWS), 1)
    sel = jnp.zeros((TT, LROWS), F32)
    for k in range(TOP_K):
        sel = sel + jnp.where(cols == lpos_ref[:, k:k + 1], 1.0, 0.0)
    f = jnp.dot(sel.astype(BF16), buf[t % 2].astype(BF16), preferred_element_type=F32)

    @pl.when(t < last)
    def _():
        yp_ref[...] = _rms(x1p_ref[...] + f, g_ref[...])

    @pl.when(t == last)
    def _():
        ys_ref[...] = _rms(x1s_ref[...] + f, g_ref[...])


def _combine(tabs, out_sorted, lpos, x1p, x1s, g):
    d = x1p.shape[1]
    nt = lpos.shape[0] // TT
    last_p = x1p.shape[0] // TT - 1
    prompt = pl.BlockSpec((TT, d), lambda t, *_: (jnp.minimum(t, last_p), 0))
    sample = pl.BlockSpec((TT, d), lambda t, *_: (0, 0))
    grid_spec = pltpu.PrefetchScalarGridSpec(
        num_scalar_prefetch=3,
        grid=(nt,),
        in_specs=[pl.BlockSpec(memory_space=pl.ANY),
                  pl.BlockSpec((TT, TOP_K), lambda t, *_: (t, 0)),
                  prompt, sample,
                  pl.BlockSpec((1, d), lambda t, *_: (0, 0))],
        out_specs=[prompt, sample],
        scratch_shapes=[pltpu.VMEM((2, LROWS, d), F32), pltpu.SemaphoreType.DMA((2,))],
    )
    return pl.pallas_call(
        _combine_kernel,
        grid_spec=grid_spec,
        out_shape=[jax.ShapeDtypeStruct(x1p.shape, F32), jax.ShapeDtypeStruct((TT, d), F32)],
        compiler_params=_cparams(("arbitrary",)),
        name="moe_combine",
    )(*tabs, out_sorted, lpos, x1p, x1s, g.reshape(1, d))


def _t5_bucket(dist):
    max_exact = N_BUCKETS // 2
    dd = dist.astype(F32)
    large = max_exact + (jnp.log(jnp.maximum(dd, 1.0) / max_exact)
                         / math.log(BUCKET_MAX_DIST / max_exact) * (N_BUCKETS - max_exact)).astype(jnp.int32)
    large = jnp.minimum(large, N_BUCKETS - 1)
    return jnp.where(dist < max_exact, dist, large)


def _bias_tables(rel_bias):
    dist = jnp.asarray(np.arange(N_TAPS)[None, :] * np.array(DILATIONS)[:, None], jnp.int32)
    bias = jnp.transpose(rel_bias[_t5_bucket(dist)], (2, 0, 1)).astype(F32)
    by_branch = jnp.transpose(bias, (1, 0, 2))
    g, h = by_branch.shape[:2]
    row = jnp.concatenate([by_branch[:, :, ::-1], jnp.full((g, h, WIN), NEG, F32)], axis=-1)
    flat = jnp.broadcast_to(row[:, :, None, :], (g, h, WIN, 2 * WIN + 1)).reshape(g, h, -1)
    band = flat[:, :, :WIN * 2 * WIN].reshape(g, h, WIN, 2 * WIN)
    return band, by_branch[:, :, :0:-1], by_branch[:, :, 0:1]


def kernel(x_prompt, x_sample, cache_k_win, cache_v_win, state_conv, state_ssm, rel_bias, attn_norm, w_in, conv_w, conv_b, dt_bias, a_log, d_skip, ssd_norm, w_out, ffn_norm, w_router, b_router, w_gate_up, b_gate_up, w_down, b_down, final_norm):
    bp, tp, d = x_prompt.shape
    bs, ts, _ = x_sample.shape
    depth = w_in.shape[0]
    assert depth == 1 and ts == 1 and tp % (max(DILATIONS) * WIN) == 0
    keep = min(max(DILATIONS) * WIN, tp)
    band, samp, samp0 = _bias_tables(rel_bias)
    l = 0

    xp = x_prompt.reshape(bp * tp, d)
    xs = x_sample.reshape(bs * ts, d)
    w_in_bf = jnp.pad(w_in[l], ((0, 0), (0, IN_PROJ_PAD - IN_PROJ))).astype(BF16)
    w_out_bf = w_out[l].astype(BF16)
    wr = jnp.pad(w_router[l], ((0, 0), (0, LANES - N_EXPERTS))).astype(BF16)
    br = jnp.pad(b_router[l], (0, LANES - N_EXPERTS), constant_values=NEG).reshape(1, LANES)

    k, v, z, xbc, dt_raw, *branch_qkv = _inproj_prompt(xp, attn_norm[l], w_in_bf, bp)
    k3 = k.reshape(bp, tp, KV_WIDTH)
    v3 = v.reshape(bp, tp, KV_WIDTH)
    att_parts = [_attn_branch(branch_qkv[2 * gi], branch_qkv[2 * gi + 1], band[gi])
                 for gi in range(len(DILATIONS))]
    ssm, st_p = _ssd_prompt(xbc, dt_raw, z, conv_w[l], conv_b[l], dt_bias[l], a_log[l], d_skip[l],
                            ssd_norm[l], bp)
    x1p, xnp_, lgp = _outproj_prompt(att_parts, ssm, xp, w_out_bf, ffn_norm[l], wr, br)
    k_win_p = k3[:, tp - keep:].reshape(1, bp, keep, N_KV_HEADS, HEAD_DIM)
    v_win_p = v3[:, tp - keep:].reshape(1, bp, keep, N_KV_HEADS, HEAD_DIM)
    conv_p = xbc.reshape(bp, tp, CONV_DIM)[:, tp - (CONV_W - 1):][None]

    q_s, k_s, v_s, z_s, xbc_s, dt_s = _inproj(xs, attn_norm[l], w_in_bf, bs * ts)
    q_s = q_s.reshape(bs, N_ATT_HEADS, HEAD_DIM)
    k_s = k_s.reshape(bs, N_KV_HEADS, HEAD_DIM)
    v_s = v_s.reshape(bs, N_KV_HEADS, HEAD_DIM)
    wbuf = cache_k_win.shape[2]
    att_s = _attn_sample(q_s, k_s, v_s, cache_k_win[l].reshape(bs, wbuf, KV_WIDTH),
                         cache_v_win[l].reshape(bs, wbuf, KV_WIDTH), samp, samp0)
    xa_s, xdt_s, decay_s = _conv_sample(xbc_s, state_conv[l], conv_w[l], conv_b[l], dt_s, dt_bias[l],
                                        a_log[l])
    nbc = N_SSM_GROUPS * D_STATE
    h_s, y_s = _ssm_sample(xdt_s, decay_s, xa_s[:, SSM_WIDTH:SSM_WIDTH + nbc], xa_s[:, SSM_WIDTH + nbc:],
                           state_ssm[l])
    x1s, xns, lgs = _outproj_sample(att_s.reshape(bs, ATT_WIDTH), y_s, xa_s[:, :SSM_WIDTH], z_s,
                                    d_skip[l], ssd_norm[l], xs, w_out_bf, ffn_norm[l], wr, br)
    conv_s = jnp.concatenate([state_conv[l][:, 1:], xbc_s[:, None]], axis=1)[None]

    n_s = bs * ts
    assert (bp * tp) % TT == 0 and n_s <= TT
    m_pad = bp * tp + TT
    logits = jnp.concatenate([lgp[:, :N_EXPERTS], lgs[:, :N_EXPERTS]], axis=0)
    tabs, lpos, gate, groups, n_rows = _route(logits, m_pad)
    by_tile = lambda a: jnp.transpose(a.reshape(m_pad // TT, TT, TOP_K), (0, 2, 1))
    pad_rows = lambda a: jnp.pad(a, ((0, TT - n_s), (0, 0)))
    x_sorted = _dispatch(tabs, xnp_, pad_rows(xns), by_tile(lpos), by_tile(gate), n_rows)
    out_sorted = _moe_ffn(groups, x_sorted, w_gate_up[l], b_gate_up[l], w_down[l], b_down[l])
    y_p, y_s_out = _combine(tabs, out_sorted, lpos, x1p, pad_rows(x1s), final_norm)
    y_s_out = y_s_out[:n_s]

    return (y_p.reshape(bp, tp, d), y_s_out.reshape(bs, ts, d), k_win_p, v_win_p, conv_p, st_p[None],
            k_s.reshape(1, bs, ts, N_KV_HEADS, HEAD_DIM), v_s.reshape(1, bs, ts, N_KV_HEADS, HEAD_DIM),
            conv_s, h_s[None])
```

```python
import functools
import math

import jax
import jax.numpy as jnp
import numpy as np
from jax import lax
from jax.experimental import pallas as pl
from jax.experimental.pallas import tpu as pltpu

F32 = jnp.float32
BF16 = jnp.bfloat16
HIGHEST = lax.Precision.HIGHEST

LANES = 128
SUBLANES = 8
VMEM_LIMIT = 56 * 1024 * 1024

HEAD_DIM = 64
N_ATT_HEADS = 16
N_KV_HEADS = 4
KV_REP = N_ATT_HEADS // N_KV_HEADS
ATT_WIDTH = N_ATT_HEADS * HEAD_DIM
KV_WIDTH = N_KV_HEADS * HEAD_DIM
DILATIONS = (1, 4, 16)
N_TAPS = 129
WIN = N_TAPS - 1
ATT_SCALE = HEAD_DIM ** -0.5
N_BUCKETS = 32
BUCKET_MAX_DIST = 2048
SSM_HEAD_DIM = 64
N_SSM_HEADS = 16
SSM_WIDTH = N_SSM_HEADS * SSM_HEAD_DIM
N_SSM_GROUPS = 2
HEADS_PER_GROUP = N_SSM_HEADS // N_SSM_GROUPS
D_STATE = 128
CONV_W = 4
CONV_DIM = SSM_WIDTH + 2 * N_SSM_GROUPS * D_STATE
SSD_CHUNK = 128
N_EXPERTS = 32
TOP_K = 4
SWIGLU_LIMIT = 7.0
SWIGLU_ALPHA = 1.702
EPS = 1e-5
NEG = -1e30

Q0, K0, V0, Z0, X0, DT0 = 0, 1024, 1280, 1536, 2560, 4096
IN_PROJ = DT0 + N_SSM_HEADS
IN_PROJ_PAD = DT0 + LANES

TM_PROJ = 512
TM_OUT = 256
TM_MOE = 512
TF_MOE = 512
TM_FIN = 256


def _cparams(sem):
    return pltpu.CompilerParams(dimension_semantics=sem, vmem_limit_bytes=VMEM_LIMIT)


def _const_spec(shape):
    nd = len(shape)
    return pl.BlockSpec(shape, lambda *_: (0,) * nd, pipeline_mode=pl.Buffered(1))


def _rms(x, g):
    ms = jnp.mean(x * x, axis=-1, keepdims=True)
    return x * lax.rsqrt(ms + EPS) * g


def _silu(x):
    return x * (1.0 / (1.0 + jnp.exp(-x)))


def _inproj_kernel(x_ref, g_ref, w_ref, q_ref, k_ref, v_ref, z_ref, xbc_ref, dt_ref):
    xn = _rms(x_ref[...], g_ref[...]).astype(BF16)

    def mm(lo, hi):
        return jnp.dot(xn, w_ref[:, lo:hi], preferred_element_type=F32)

    q_ref[...] = (mm(Q0, K0) * ATT_SCALE).astype(BF16)
    k_ref[...] = mm(K0, V0)
    v_ref[...] = mm(V0, Z0)
    z_ref[...] = mm(Z0, X0)
    xbc_ref[...] = mm(X0, DT0)
    dt_ref[...] = mm(DT0, IN_PROJ_PAD)


def _inproj(x2d, g, w_bf, tm):
    m, d = x2d.shape
    widths = (ATT_WIDTH, KV_WIDTH, KV_WIDTH, SSM_WIDTH, CONV_DIM, LANES)
    dtypes = (BF16, F32, F32, F32, F32, F32)
    return pl.pallas_call(
        _inproj_kernel,
        grid=(m // tm,),
        in_specs=[pl.BlockSpec((tm, d), lambda i: (i, 0)),
                  _const_spec((1, d)),
                  _const_spec((d, IN_PROJ_PAD))],
        out_specs=[pl.BlockSpec((tm, w), lambda i: (i, 0)) for w in widths],
        out_shape=[jax.ShapeDtypeStruct((m, w), t) for w, t in zip(widths, dtypes)],
        compiler_params=_cparams(("parallel",)),
        name=f"inproj_{tm}",
    )(x2d, g.reshape(1, d), w_bf)


def _deinterleave_matrix(n, dil):
    p = np.zeros((n, n), np.float32)
    src = np.arange(n)
    p[(src % dil) * (n // dil) + src // dil, src] = 1.0
    return p


def _inproj_prompt_kernel(x_ref, g_ref, w_ref, p_mid_ref, p_far_ref, k_ref, v_ref, z_ref, xbc_ref, dt_ref,
                          q0_ref, kv0_ref, q1_ref, kv1_ref, q2_ref, kv2_ref):
    tm = x_ref.shape[0]
    xn = _rms(x_ref[...], g_ref[...]).astype(BF16)

    def mm(lo, hi):
        return jnp.dot(xn, w_ref[:, lo:hi], preferred_element_type=F32)

    q = (mm(Q0, K0) * ATT_SCALE).astype(BF16)
    k = mm(K0, V0)
    v = mm(V0, Z0)
    k_ref[...] = k
    v_ref[...] = v
    z_ref[...] = mm(Z0, X0)
    xbc_ref[...] = mm(X0, DT0)
    dt_ref[...] = mm(DT0, IN_PROJ_PAD)
    kv = jnp.concatenate([k, v], axis=-1).astype(BF16)
    q0_ref[0, 0] = q
    kv0_ref[0, 0] = kv
    for dil, p_ref, qd_ref, kvd_ref in ((DILATIONS[1], p_mid_ref, q1_ref, kv1_ref),
                                        (DILATIONS[2], p_far_ref, q2_ref, kv2_ref)):
        qp = jnp.dot(p_ref[...], q, preferred_element_type=F32).astype(BF16)
        kvp = jnp.dot(p_ref[...], kv, preferred_element_type=F32).astype(BF16)
        rows = tm // dil
        for r in range(dil):
            qd_ref[0, r] = qp[r * rows:(r + 1) * rows]
            kvd_ref[0, r] = kvp[r * rows:(r + 1) * rows]


def _inproj_prompt(x2d, g, w_bf, batch):
    m, d = x2d.shape
    tm = TM_PROJ
    seq = m // batch
    per_b = seq // tm
    assert DILATIONS[0] == 1 and seq % tm == 0 and tm % (max(DILATIONS) * 2 * SUBLANES) == 0
    widths = (KV_WIDTH, KV_WIDTH, SSM_WIDTH, CONV_DIM, LANES)
    rows = lambda w: pl.BlockSpec((tm, w), lambda i: (i, 0))
    split = lambda dil, w: pl.BlockSpec((1, dil, tm // dil, w), lambda i: (i // per_b, 0, i % per_b, 0))
    branch_specs, branch_shapes = [], []
    for dil in DILATIONS:
        for w in (ATT_WIDTH, 2 * KV_WIDTH):
            branch_specs.append(split(dil, w))
            branch_shapes.append(jax.ShapeDtypeStruct((batch, dil, seq // dil, w), BF16))
    perms = [jnp.asarray(_deinterleave_matrix(tm, dil), BF16) for dil in DILATIONS[1:]]
    return pl.pallas_call(
        _inproj_prompt_kernel,
        grid=(m // tm,),
        in_specs=[rows(d), _const_spec((1, d)), _const_spec((d, IN_PROJ_PAD)),
                  _const_spec((tm, tm)), _const_spec((tm, tm))],
        out_specs=[rows(w) for w in widths] + branch_specs,
        out_shape=[jax.ShapeDtypeStruct((m, w), F32) for w in widths] + branch_shapes,
        compiler_params=_cparams(("parallel",)),
        name="inproj_prompt",
    )(x2d, g.reshape(1, d), w_bf, *perms)


def _attn_kernel(q_ref, kvp_ref, kvc_ref, bias_ref, o_ref, lse_ref):
    first = pl.program_id(2) == 0
    lane = lax.broadcasted_iota(jnp.int32, (1, 2 * WIN), 1)
    prev_mask = jnp.where(jnp.logical_and(first, lane < WIN), NEG, 0.0)
    head_lane = lax.broadcasted_iota(jnp.int32, (WIN, LANES), 1)
    lse_tile = jnp.zeros((WIN, LANES), F32)
    for kvh in range(N_KV_HEADS):
        ks = slice(kvh * HEAD_DIM, (kvh + 1) * HEAD_DIM)
        vs = slice(KV_WIDTH + kvh * HEAD_DIM, KV_WIDTH + (kvh + 1) * HEAD_DIM)
        kw = jnp.concatenate([kvp_ref[0, 0, :, ks], kvc_ref[0, 0, :, ks]], axis=0)
        vw = jnp.concatenate([kvp_ref[0, 0, :, vs], kvc_ref[0, 0, :, vs]], axis=0)
        for pair in range(KV_REP // 2):
            outs = []
            for r in range(2):
                h = kvh * KV_REP + pair * 2 + r
                qh = q_ref[0, 0, :, h * HEAD_DIM:(h + 1) * HEAD_DIM]
                s = lax.dot_general(qh, kw, (((1,), (1,)), ((), ())), preferred_element_type=F32)
                s = s + bias_ref[h] + prev_mask
                mx = jnp.max(s, axis=-1, keepdims=True)
                p = jnp.exp(s - mx)
                l = jnp.sum(p, axis=-1, keepdims=True)
                o = jnp.dot(p.astype(BF16), vw, preferred_element_type=F32)
                outs.append(o * (1.0 / l))
                lse_tile = jnp.where(head_lane == h, mx + jnp.log(l), lse_tile)
            h0 = kvh * KV_REP + pair * 2
            o_ref[0, 0, :, h0 * HEAD_DIM:(h0 + 2) * HEAD_DIM] = jnp.concatenate(outs, axis=-1).astype(BF16)
    lse_ref[0, 0] = lse_tile


def _attn_branch(q, kv, bias_mat):
    b, dil, sub, _ = q.shape
    nb = sub // WIN
    cur = lambda bb, r, i: (bb, r, i, 0)
    prev = lambda bb, r, i: (bb, r, jnp.maximum(i - 1, 0), 0)
    return pl.pallas_call(
        _attn_kernel,
        grid=(b, dil, nb),
        in_specs=[pl.BlockSpec((1, 1, WIN, ATT_WIDTH), cur),
                  pl.BlockSpec((1, 1, WIN, 2 * KV_WIDTH), prev),
                  pl.BlockSpec((1, 1, WIN, 2 * KV_WIDTH), cur),
                  _const_spec((N_ATT_HEADS, WIN, 2 * WIN))],
        out_specs=[pl.BlockSpec((1, 1, WIN, ATT_WIDTH), cur),
                   pl.BlockSpec((1, 1, WIN, LANES), cur)],
        out_shape=[jax.ShapeDtypeStruct((b, dil, sub, ATT_WIDTH), BF16),
                   jax.ShapeDtypeStruct((b, dil, sub, LANES), F32)],
        compiler_params=_cparams(("parallel", "parallel", "arbitrary")),
        name=f"attn_dil{dil}",
    )(q, kv, kv, bias_mat)


def _softplus(x):
    return jnp.maximum(x, 0.0) + jnp.log(1.0 + jnp.exp(-jnp.abs(x)))


def _ssd_kernel(xbc_ref, dt_ref, z_ref, cw_ref, cb_ref, dtb_ref, alog_ref, dskip_ref, gn_ref, e_ref,
                y_ref, st_ref, ext_ref, state_ref):
    c = pl.program_id(1)
    L = SSD_CHUNK

    @pl.when(c == 0)
    def _():
        ext_ref[0:SUBLANES, :] = jnp.zeros((SUBLANES, CONV_DIM), F32)
        state_ref[...] = jnp.zeros_like(state_ref)

    ext_ref[SUBLANES:SUBLANES + L, :] = xbc_ref[...]
    acc = cb_ref[...] + ext_ref[SUBLANES:SUBLANES + L, :] * cw_ref[CONV_W - 1:CONV_W, :]
    for i in range(CONV_W - 1):
        off = SUBLANES - (CONV_W - 1) + i
        acc = acc + ext_ref[off:off + L, :] * cw_ref[i:i + 1, :]
    ext_ref[0:SUBLANES, :] = ext_ref[L:L + SUBLANES, :]
    xa = _silu(acc)

    dt = _softplus(dt_ref[...] + dtb_ref[...])
    da = dt * (-jnp.exp(alog_ref[...]))
    row = lax.broadcasted_iota(jnp.int32, (L, L), 0)
    col = lax.broadcasted_iota(jnp.int32, (L, L), 1)
    tri = row >= col
    a_cs = jnp.dot(tri.astype(F32), da, preferred_element_type=F32, precision=HIGHEST)
    a_cs_t = a_cs.T
    expand = e_ref[...]
    acs_full = jnp.dot(a_cs, expand, preferred_element_type=F32, precision=HIGHEST)
    dt_full = jnp.dot(dt, expand, preferred_element_type=F32, precision=HIGHEST)
    exp_acs = jnp.exp(acs_full)
    a_last = acs_full[L - 1:L, :]
    exp_last = exp_acs[L - 1:L, :]
    xs = xa[:, :SSM_WIDTH]
    xdt = xs * dt_full
    xw = xdt * jnp.exp(a_last - acs_full)

    for g in range(N_SSM_GROUPS):
        b0 = SSM_WIDTH + g * D_STATE
        c0 = SSM_WIDTH + N_SSM_GROUPS * D_STATE + g * D_STATE
        bg_t = xa[:, b0:b0 + D_STATE].T.astype(BF16)
        cg = xa[:, c0:c0 + D_STATE].astype(BF16)
        gram = jnp.dot(cg, bg_t, preferred_element_type=F32)
        for hh in range(HEADS_PER_GROUP):
            h = g * HEADS_PER_GROUP + hh
            hs = slice(h * SSM_HEAD_DIM, (h + 1) * SSM_HEAD_DIM)
            seg = jnp.where(tri, a_cs[:, h:h + 1] - a_cs_t[h:h + 1, :], NEG)
            scores = (gram * jnp.exp(seg)).astype(BF16)
            y_diag = jnp.dot(scores, xdt[:, hs].astype(BF16), preferred_element_type=F32)
            st = state_ref[h]
            y_off = jnp.dot(cg, st.astype(BF16), preferred_element_type=F32) * exp_acs[:, hs]
            y_ref[:, hs] = y_diag + y_off
            state_ref[h] = exp_last[:, hs] * st + jnp.dot(bg_t, xw[:, hs].astype(BF16),
                                                          preferred_element_type=F32)

    y = y_ref[...] + dskip_ref[...] * xs
    u = y * _silu(z_ref[...])
    gw = SSM_WIDTH // N_SSM_GROUPS
    parts = []
    for g in range(N_SSM_GROUPS):
        ug = u[:, g * gw:(g + 1) * gw]
        parts.append(ug * lax.rsqrt(jnp.mean(ug * ug, axis=-1, keepdims=True) + EPS))
    y_ref[...] = jnp.concatenate(parts, axis=-1) * gn_ref[...]

    @pl.when(c == pl.num_programs(1) - 1)
    def _():
        st_ref[0] = state_ref[...]


def _head_expand():
    e = np.zeros((LANES, SSM_WIDTH), np.float32)
    for h in range(N_SSM_HEADS):
        e[h, h * SSM_HEAD_DIM:(h + 1) * SSM_HEAD_DIM] = 1.0
    return jnp.asarray(e)


def _pad_lanes(v):
    return jnp.pad(v.astype(F32), (0, LANES - v.shape[0])).reshape(1, LANES)


def _ssd_prompt(xbc, dt_raw, z, conv_w, conv_b, dt_bias, a_log, d_skip, ssd_norm, batch):
    m = xbc.shape[0]
    nc = m // batch // SSD_CHUNK
    L = SSD_CHUNK
    rows = lambda b, c: (b * nc + c, 0)
    y, st = pl.pallas_call(
        _ssd_kernel,
        grid=(batch, nc),
        in_specs=[pl.BlockSpec((L, CONV_DIM), rows),
                  pl.BlockSpec((L, LANES), rows),
                  pl.BlockSpec((L, SSM_WIDTH), rows),
                  _const_spec((CONV_W, CONV_DIM)),
                  _const_spec((1, CONV_DIM)),
                  _const_spec((1, LANES)),
                  _const_spec((1, LANES)),
                  _const_spec((1, SSM_WIDTH)),
                  _const_spec((1, SSM_WIDTH)),
                  _const_spec((LANES, SSM_WIDTH))],
        out_specs=[pl.BlockSpec((L, SSM_WIDTH), rows),
                   pl.BlockSpec((1, N_SSM_HEADS, D_STATE, SSM_HEAD_DIM), lambda b, c: (b, 0, 0, 0))],
        out_shape=[jax.ShapeDtypeStruct((m, SSM_WIDTH), F32),
                   jax.ShapeDtypeStruct((batch, N_SSM_HEADS, D_STATE, SSM_HEAD_DIM), F32)],
        scratch_shapes=[pltpu.VMEM((SUBLANES + L, CONV_DIM), F32),
                        pltpu.VMEM((N_SSM_HEADS, D_STATE, SSM_HEAD_DIM), F32)],
        compiler_params=_cparams(("parallel", "arbitrary")),
        name="ssd_prompt",
    )(xbc, dt_raw, z, conv_w, conv_b.reshape(1, CONV_DIM), _pad_lanes(dt_bias), _pad_lanes(a_log),
      jnp.repeat(d_skip, SSM_HEAD_DIM).reshape(1, SSM_WIDTH), ssd_norm.reshape(1, SSM_WIDTH),
      _head_expand())
    return y, jnp.swapaxes(st, -1, -2)


def _router(xn_bf, wr_ref, br_ref):
    return jnp.dot(xn_bf, wr_ref[...], preferred_element_type=F32) + br_ref[...]


def _outproj_kernel(o0, l0, o1, l1, o2, l2, u1_ref, u2_ref, e_ref, ssm_ref, x_ref, w_ref, g_ref, wr_ref,
                    br_ref, x1_ref, xn_ref, lg_ref):
    def natural(o_ref, l_ref, u_ref):
        dil = o_ref.shape[1]
        if dil == 1:
            return o_ref[0, 0].astype(F32), l_ref[0, 0]
        o_cat = jnp.concatenate([o_ref[0, r] for r in range(dil)], axis=0)
        l_cat = jnp.concatenate([l_ref[0, r] for r in range(dil)], axis=0)
        hi = l_cat.astype(BF16)
        lo = (l_cat - hi.astype(F32)).astype(BF16)
        u = u_ref[...]
        return (jnp.dot(u, o_cat, preferred_element_type=F32),
                jnp.dot(u, hi, preferred_element_type=F32) + jnp.dot(u, lo, preferred_element_type=F32))

    branches = [natural(o0, l0, None), natural(o1, l1, u1_ref), natural(o2, l2, u2_ref)]
    lses = [jnp.dot(l, e_ref[...], preferred_element_type=F32, precision=HIGHEST) for _, l in branches]
    mx = functools.reduce(jnp.maximum, lses)
    ws = [jnp.exp(l - mx) for l in lses]
    num = functools.reduce(jnp.add, [w * o for w, (o, _) in zip(ws, branches)])
    att = num * (1.0 / functools.reduce(jnp.add, ws))
    y = jnp.dot(att.astype(BF16), w_ref[:ATT_WIDTH, :], preferred_element_type=F32)
    y = y + jnp.dot(ssm_ref[...].astype(BF16), w_ref[ATT_WIDTH:, :], preferred_element_type=F32)
    x1 = x_ref[...] + y
    x1_ref[...] = x1
    xn = _rms(x1, g_ref[...]).astype(BF16)
    xn_ref[...] = xn
    lg_ref[...] = _router(xn, wr_ref, br_ref)


def _outproj_prompt(att_parts, ssm, x2d, w_bf, g, wr, br):
    m, d = x2d.shape
    mix = w_bf.shape[0]
    tm = TM_OUT
    batch = att_parts[0][0].shape[0]
    per_b = m // batch // tm
    assert tm % (max(DILATIONS) * 2 * SUBLANES) == 0
    row = lambda w: pl.BlockSpec((tm, w), lambda i: (i, 0))
    split = lambda dil, w: pl.BlockSpec((1, dil, tm // dil, w), lambda i: (i // per_b, 0, i % per_b, 0))
    branch_specs = [split(o.shape[1], w) for o, _ in att_parts for w in (ATT_WIDTH, LANES)]
    interleave = [jnp.asarray(_deinterleave_matrix(tm, dil).T, BF16) for dil in DILATIONS[1:]]
    return pl.pallas_call(
        _outproj_kernel,
        grid=(m // tm,),
        in_specs=branch_specs + [_const_spec((tm, tm)), _const_spec((tm, tm)),
                                 _const_spec((LANES, ATT_WIDTH)), row(SSM_WIDTH), row(d),
                                 _const_spec((mix, d)), _const_spec((1, d)),
                                 _const_spec((d, LANES)), _const_spec((1, LANES))],
        out_specs=[row(d), row(d), row(LANES)],
        out_shape=[jax.ShapeDtypeStruct((m, d), F32), jax.ShapeDtypeStruct((m, d), BF16),
                   jax.ShapeDtypeStruct((m, LANES), F32)],
        compiler_params=_cparams(("parallel",)),
        name="outproj_prompt",
    )(*[a for part in att_parts for a in part], *interleave, _head_expand(), ssm, x2d, w_bf,
      g.reshape(1, d), wr, br)


def _attn_sample_kernel(q_ref, kn_ref, vn_ref, kc_ref, vc_ref, bias_ref, bias0_ref, o_ref):
    w = kc_ref.shape[1]
    q = q_ref[0]
    head_grp = lax.broadcasted_iota(jnp.int32, (N_ATT_HEADS, 1), 0) // KV_REP
    kn = kn_ref[0].astype(BF16).astype(F32)
    vn = vn_ref[0].astype(BF16).astype(F32)
    s_self = jnp.sum(q.astype(F32) * kn, axis=-1, keepdims=True)

    def taps(c_ref, dil):
        span = WIN * dil
        rows = c_ref[0, w - span:w, :].astype(BF16)
        if dil == 1:
            return rows
        tap = lax.broadcasted_iota(jnp.int32, (WIN, span), 0)
        col = lax.broadcasted_iota(jnp.int32, (WIN, span), 1)
        pick = jnp.where(col == tap * dil, 1.0, 0.0).astype(BF16)
        return jnp.dot(pick, rows, preferred_element_type=F32).astype(BF16)

    scores, selfs, lses = [], [], []
    for g, dil in enumerate(DILATIONS):
        kk = taps(kc_ref, dil)
        s = jnp.zeros((N_ATT_HEADS, WIN), F32)
        for kvh in range(N_KV_HEADS):
            cs = slice(kvh * HEAD_DIM, (kvh + 1) * HEAD_DIM)
            sk = lax.dot_general(q, kk[:, cs], (((1,), (1,)), ((), ())), preferred_element_type=F32)
            s = jnp.where(head_grp == kvh, sk, s)
        s = s + bias_ref[g]
        s0 = s_self + bias0_ref[g]
        mx = jnp.maximum(jnp.max(s, axis=-1, keepdims=True), s0)
        lse = mx + jnp.log(jnp.sum(jnp.exp(s - mx), axis=-1, keepdims=True) + jnp.exp(s0 - mx))
        scores.append(s); selfs.append(s0); lses.append(lse)
    top = functools.reduce(jnp.maximum, lses)
    es = [jnp.exp(l - top) for l in lses]
    tot = functools.reduce(jnp.add, es)
    o = jnp.zeros((N_ATT_HEADS, HEAD_DIM), F32)
    for s, s0, lse, e, dil in zip(scores, selfs, lses, es, DILATIONS):
        wgt = e / tot
        p = (jnp.exp(s - lse) * wgt).astype(BF16)
        p0 = (jnp.exp(s0 - lse) * wgt).astype(BF16).astype(F32)
        vv = taps(vc_ref, dil)
        o = o + p0 * vn
        for kvh in range(N_KV_HEADS):
            cs = slice(kvh * HEAD_DIM, (kvh + 1) * HEAD_DIM)
            ok = jnp.dot(p, vv[:, cs], preferred_element_type=F32)
            o = o + jnp.where(head_grp == kvh, ok, 0.0)
    o_ref[0] = o


def _attn_sample(q, k_new, v_new, k_cache, v_cache, bias_s, bias0_s):
    n, w = k_cache.shape[0], k_cache.shape[1]
    assert w % (max(DILATIONS) * WIN) == 0
    tok = lambda b: (b, 0, 0)
    head = pl.BlockSpec((1, N_ATT_HEADS, HEAD_DIM), tok)
    window = pl.BlockSpec((1, w, KV_WIDTH), tok)
    return pl.pallas_call(
        _attn_sample_kernel,
        grid=(n,),
        in_specs=[head, head, head, window, window,
                  _const_spec((len(DILATIONS), N_ATT_HEADS, WIN)),
                  _const_spec((len(DILATIONS), N_ATT_HEADS, 1))],
        out_specs=head,
        out_shape=jax.ShapeDtypeStruct((n, N_ATT_HEADS, HEAD_DIM), F32),
        compiler_params=_cparams(("parallel",)),
        name="attn_sample",
    )(q, jnp.repeat(k_new, KV_REP, axis=1), jnp.repeat(v_new, KV_REP, axis=1), k_cache, v_cache,
      bias_s, bias0_s)


def _conv_sample_kernel(xbc_ref, b0_ref, b1_ref, b2_ref, cw_ref, cb_ref, dt_ref, dtb_ref, alog_ref,
                        e_ref, xa_ref, xdt_ref, decay_ref):
    acc = cb_ref[...] + xbc_ref[...] * cw_ref[CONV_W - 1:CONV_W, :]
    for i, buf in enumerate((b0_ref, b1_ref, b2_ref)):
        acc = acc + buf[...] * cw_ref[i:i + 1, :]
    xa = _silu(acc)
    xa_ref[...] = xa
    dt = _softplus(dt_ref[...] + dtb_ref[...])
    decay_ref[...] = jnp.exp(dt * (-jnp.exp(alog_ref[...])))
    dt_full = jnp.dot(dt, e_ref[...], preferred_element_type=F32, precision=HIGHEST)
    xdt_ref[...] = xa[:, :SSM_WIDTH] * dt_full


def _conv_sample(xbc, conv_buf, conv_w, conv_b, dt_raw, dt_bias, a_log):
    n = xbc.shape[0]
    args = (xbc, conv_buf[:, 0], conv_buf[:, 1], conv_buf[:, 2], conv_w, conv_b.reshape(1, CONV_DIM),
            dt_raw, _pad_lanes(dt_bias), _pad_lanes(a_log),
            _head_expand())
    return pl.pallas_call(
        _conv_sample_kernel,
        grid=(1,),
        in_specs=[_const_spec(a.shape) for a in args],
        out_specs=[_const_spec((n, CONV_DIM)), _const_spec((n, SSM_WIDTH)), _const_spec((n, LANES))],
        out_shape=[jax.ShapeDtypeStruct((n, CONV_DIM), F32), jax.ShapeDtypeStruct((n, SSM_WIDTH), F32),
                   jax.ShapeDtypeStruct((n, LANES), F32)],
        compiler_params=_cparams(("arbitrary",)),
        name="conv_sample",
    )(*args)


def _ssm_sample_kernel(xdt_ref, decay_ref, b_ref, c_ref, h0_ref, hn_ref, y_ref):
    for g in range(N_SSM_GROUPS):
        hs = slice(g * HEADS_PER_GROUP, (g + 1) * HEADS_PER_GROUP)
        hn = decay_ref[0, hs] * h0_ref[0, hs] + xdt_ref[0, hs] * b_ref[0, g]
        hn_ref[0, hs] = hn
        c_row = c_ref[0, g].astype(BF16).astype(F32)
        y_ref[0, hs] = jnp.sum(hn.astype(BF16).astype(F32) * c_row, axis=-1, keepdims=True)


def _ssm_sample(xdt, decay, bmat, cmat, h0):
    n = xdt.shape[0]
    p = SSM_HEAD_DIM
    tok4 = lambda b: (b, 0, 0, 0)
    hn, y = pl.pallas_call(
        _ssm_sample_kernel,
        grid=(n,),
        in_specs=[pl.BlockSpec((1, N_SSM_HEADS, p, 1), tok4),
                  pl.BlockSpec((1, N_SSM_HEADS, 1, 1), tok4),
                  pl.BlockSpec((1, N_SSM_GROUPS, 1, D_STATE), tok4),
                  pl.BlockSpec((1, N_SSM_GROUPS, 1, D_STATE), tok4),
                  pl.BlockSpec((1, N_SSM_HEADS, p, D_STATE), tok4)],
        out_specs=[pl.BlockSpec((1, N_SSM_HEADS, p, D_STATE), tok4),
                   pl.BlockSpec((1, N_SSM_HEADS, p, 1), tok4)],
        out_shape=[jax.ShapeDtypeStruct((n, N_SSM_HEADS, p, D_STATE), F32),
                   jax.ShapeDtypeStruct((n, N_SSM_HEADS, p, 1), F32)],
        compiler_params=_cparams(("parallel",)),
        name="ssm_sample",
    )(xdt.reshape(n, N_SSM_HEADS, p, 1), decay[:, :N_SSM_HEADS].reshape(n, N_SSM_HEADS, 1, 1),
      bmat.reshape(n, N_SSM_GROUPS, 1, D_STATE), cmat.reshape(n, N_SSM_GROUPS, 1, D_STATE), h0)
    return hn, y.reshape(n, SSM_WIDTH)


def _outproj_sample_kernel(att_ref, y_ref, xs_ref, z_ref, dskip_ref, gn_ref, x_ref, w_ref, g_ref,
                           wr_ref, br_ref, x1_ref, xn_ref, lg_ref):
    y = y_ref[...] + dskip_ref[...] * xs_ref[...]
    u = y * _silu(z_ref[...])
    gw = SSM_WIDTH // N_SSM_GROUPS
    parts = []
    for g in range(N_SSM_GROUPS):
        ug = u[:, g * gw:(g + 1) * gw]
        parts.append(ug * lax.rsqrt(jnp.mean(ug * ug, axis=-1, keepdims=True) + EPS)
                     * gn_ref[:, g * gw:(g + 1) * gw])
    mix = jnp.concatenate([att_ref[...]] + parts, axis=-1).astype(BF16)
    x1 = x_ref[...] + jnp.dot(mix, w_ref[...], preferred_element_type=F32)
    x1_ref[...] = x1
    xn = _rms(x1, g_ref[...]).astype(BF16)
    xn_ref[...] = xn
    lg_ref[...] = _router(xn, wr_ref, br_ref)


def _outproj_sample(att, y, xs, z, d_skip, ssd_norm, x2d, w_bf, g, wr, br):
    n, d = x2d.shape
    args = (att, y, xs, z, jnp.repeat(d_skip, SSM_HEAD_DIM).reshape(1, SSM_WIDTH),
            ssd_norm.reshape(1, SSM_WIDTH), x2d, w_bf, g.reshape(1, d), wr, br)
    return pl.pallas_call(
        _outproj_sample_kernel,
        grid=(1,),
        in_specs=[_const_spec(a.shape) for a in args],
        out_specs=[_const_spec((n, d)), _const_spec((n, d)), _const_spec((n, LANES))],
        out_shape=[jax.ShapeDtypeStruct((n, d), F32), jax.ShapeDtypeStruct((n, d), BF16),
                   jax.ShapeDtypeStruct((n, LANES), F32)],
        compiler_params=_cparams(("arbitrary",)),
        name="outproj_sample",
    )(*args)


TT = 256
CHUNK_ALIGN = SUBLANES
LROWS = -(-(TT * TOP_K + N_EXPERTS * (CHUNK_ALIGN - 1)) // TT) * TT
CHUNK_SIZES = tuple(1 << b for b in range(TT.bit_length() - 1, CHUNK_ALIGN.bit_length() - 2, -1))
SUB = 256
SUB_TAIL = SUB // 2
RG = 5 * SUB
TF = 256
VMEM_LIMIT_FFN = 60000 * 1024


def _chunk_loop(cnt_ref, loff_ref, dest_ref, tile, fn):
    def per_expert(e, carry):
        idx = tile * N_EXPERTS + e
        n, off, dst = cnt_ref[idx], loff_ref[idx], dest_ref[idx]
        for size in CHUNK_SIZES:
            take = (n & size) != 0

            @pl.when(take)
            def _(off=off, dst=dst, size=size):
                fn(pl.multiple_of(off, CHUNK_ALIGN), pl.multiple_of(dst, CHUNK_ALIGN), size)

            step = jnp.where(take, size, 0)
            off, dst = off + step, dst + step
        return carry

    lax.fori_loop(0, N_EXPERTS, per_expert, 0)


def _dispatch_kernel(cnt_ref, loff_ref, dest_ref, xp_ref, xs_ref, lpos_ref, gate_ref, out_hbm, buf, sem):
    t = pl.program_id(0)
    last = pl.num_programs(0) - 1
    d = xp_ref.shape[1]
    x = jnp.where(t == last, xs_ref[...], xp_ref[...])
    rows = lax.broadcasted_iota(jnp.int32, (LROWS, TT), 0)
    onehot = jnp.zeros((LROWS, TT), F32)
    wcol = jnp.zeros((LROWS, 1), F32)
    for k in range(TOP_K):
        hit = jnp.where(rows == lpos_ref[0, k:k + 1, :], 1.0, 0.0)
        onehot = onehot + hit
        wcol = wcol + jnp.sum(hit * gate_ref[0, k:k + 1, :], axis=-1, keepdims=True)
    tile_buf = buf.at[t % 2]
    tile_buf[:, 0:d] = jnp.dot(onehot.astype(BF16), x, preferred_element_type=F32)
    tile_buf[:, d:d + LANES] = jnp.broadcast_to(wcol, (LROWS, LANES))

    def copy(tile, off, dst, size):
        return pltpu.make_async_copy(buf.at[tile % 2, pl.ds(off, size)], out_hbm.at[pl.ds(dst, size)],
                                     sem.at[tile % 2])

    _chunk_loop(cnt_ref, loff_ref, dest_ref, t, lambda o, g, s: copy(t, o, g, s).start())

    @pl.when(t > 0)
    def _():
        _chunk_loop(cnt_ref, loff_ref, dest_ref, t - 1, lambda o, g, s: copy(t - 1, o, g, s).wait())

    @pl.when(t == last)
    def _():
        _chunk_loop(cnt_ref, loff_ref, dest_ref, t, lambda o, g, s: copy(t, o, g, s).wait())


def _dispatch(tabs, xn_p, xn_s, lpos_t, gate_t, n_rows):
    nt = lpos_t.shape[0]
    d = xn_p.shape[1]
    last_p = xn_p.shape[0] // TT - 1
    grid_spec = pltpu.PrefetchScalarGridSpec(
        num_scalar_prefetch=3,
        grid=(nt,),
        in_specs=[pl.BlockSpec((TT, d), lambda t, *_: (jnp.minimum(t, last_p), 0)),
                  pl.BlockSpec((TT, d), lambda t, *_: (0, 0)),
                  pl.BlockSpec((1, TOP_K, TT), lambda t, *_: (t, 0, 0)),
                  pl.BlockSpec((1, TOP_K, TT), lambda t, *_: (t, 0, 0))],
        out_specs=pl.BlockSpec(memory_space=pl.ANY),
        scratch_shapes=[pltpu.VMEM((2, LROWS, d + LANES), F32), pltpu.SemaphoreType.DMA((2,))],
    )
    return pl.pallas_call(
        _dispatch_kernel,
        grid_spec=grid_spec,
        out_shape=jax.ShapeDtypeStruct((n_rows, d + LANES), F32),
        compiler_params=_cparams(("arbitrary",)),
        name="moe_dispatch",
    )(*tabs, xn_p, xn_s, lpos_t, gate_t)


def _ffn_kernel(ge_ref, gs_ref, gn_ref, gt_ref, ng_ref, xs_hbm, wgu_hbm, wd_hbm, bgu_ref, bdn_ref, out_hbm,
                xbuf, acc, ostage, wg_st, wu_st, wd_st, wg_bf, wu_bf, wd_bf, sem_w, sem_x, sem_o):
    d = acc.shape[1]
    d_ff = wd_hbm.shape[1]
    nf = d_ff // TF
    n_groups = ng_ref[0]
    total = n_groups * nf

    def w_copies(s, slot):
        g = s // nf
        f = s - g * nf
        e = ge_ref[g]
        c0 = pl.multiple_of(f * TF, TF)
        return (pltpu.make_async_copy(wgu_hbm.at[e, :, pl.ds(c0, TF)], wg_st.at[slot], sem_w.at[slot, 0]),
                pltpu.make_async_copy(wgu_hbm.at[e, :, pl.ds(d_ff + c0, TF)], wu_st.at[slot], sem_w.at[slot, 1]),
                pltpu.make_async_copy(wd_hbm.at[e, pl.ds(c0, TF), :], wd_st.at[slot], sem_w.at[slot, 2]))

    def x_copy(g, j, size=SUB):
        r0 = pl.multiple_of(j * SUB, SUB)
        return pltpu.make_async_copy(xs_hbm.at[pl.ds(pl.multiple_of(gs_ref[g] + r0, CHUNK_ALIGN), size)],
                                     xbuf.at[g % 2, pl.ds(r0, size)], sem_x)

    def o_copy(g, j, size=SUB):
        r0 = pl.multiple_of(j * SUB, SUB)
        return pltpu.make_async_copy(ostage.at[j % 2, pl.ds(0, size)],
                                     out_hbm.at[pl.ds(pl.multiple_of(gs_ref[g] + r0, CHUNK_ALIGN), size)],
                                     sem_o.at[j % 2])

    def loop(n, fn):
        lax.fori_loop(0, n, lambda j, c: (fn(j), c)[1], 0)

    def group_rows(g, op):
        loop(gn_ref[g], lambda j: op(x_copy(g, j)))

        @pl.when(gt_ref[g] == 1)
        def _():
            op(x_copy(g, gn_ref[g], SUB_TAIL))

    def drain_stores(g):
        n = gn_ref[g]
        tail = gt_ref[g]

        @pl.when(tail == 1)
        def _():
            o_copy(g, n, SUB_TAIL).wait()

        @pl.when(n >= 1)
        def _():
            o_copy(g, n - 1).wait()

        @pl.when(jnp.logical_and(n >= 2, tail == 0))
        def _():
            o_copy(g, n - 2).wait()

    @pl.when(total > 0)
    def _():
        for c in w_copies(0, 0):
            c.start()
        group_rows(0, lambda c: c.start())

    def item(s, carry):
        slot = s % 2
        g = s // nf
        f = s - g * nf
        e = ge_ref[g]
        nsub = gn_ref[g]
        xg = xbuf.at[g % 2]

        @pl.when(s + 1 < total)
        def _():
            for c in w_copies(s + 1, 1 - slot):
                c.start()

        @pl.when(f == 0)
        def _():
            group_rows(g, lambda c: c.wait())

        @pl.when(jnp.logical_and(f == 1, g + 1 < n_groups))
        def _():
            group_rows(g + 1, lambda c: c.start())

        for c in w_copies(s, slot):
            c.wait()
        bg = bgu_ref[pl.ds(e * 2 * nf + f, 1), :]
        bu = bgu_ref[pl.ds(e * 2 * nf + nf + f, 1), :]

        def sub_block(j, phase, cast=False, size=SUB):
            rs = pl.ds(pl.multiple_of(j * SUB, SUB), size)
            x = xg[rs, 0:d].astype(BF16)
            if cast:
                wg, wu, wd = (st[slot].astype(BF16) for st in (wg_st, wu_st, wd_st))
                wg_bf[...], wu_bf[...], wd_bf[...] = wg, wu, wd
            else:
                wg, wu, wd = wg_bf[...], wu_bf[...], wd_bf[...]
            hg = jnp.dot(x, wg, preferred_element_type=F32) + bg
            hu = jnp.dot(x, wu, preferred_element_type=F32) + bu
            gg = jnp.minimum(hg, SWIGLU_LIMIT)
            uu = jnp.clip(hu, -SWIGLU_LIMIT, SWIGLU_LIMIT)
            act = gg * (1.0 / (1.0 + jnp.exp(-SWIGLU_ALPHA * gg))) * (uu + 1.0)
            part = jnp.dot(act.astype(BF16), wd, preferred_element_type=F32)
            if phase == "first":
                acc[rs, :] = part + bdn_ref[pl.ds(e, 1), :]
            elif phase == "middle":
                acc[rs, :] += part
            else:
                @pl.when(j >= 2)
                def _():
                    o_copy(g, j - 2).wait()

                ostage[j % 2, 0:size] = (acc[rs, :] + part) * xg[rs, d:d + 1]
                o_copy(g, j, size).start()

        def all_sub_blocks(phase):
            sub_block(jnp.int32(0), phase, cast=True)

            def pair(p):
                sub_block(2 * p + 1, phase)
                sub_block(2 * p + 2, phase)

            loop((nsub - 1) // 2, pair)
            odd_one = nsub % 2 == 0
            tail = gt_ref[g] == 1

            @pl.when(jnp.logical_and(odd_one, jnp.logical_not(tail)))
            def _():
                sub_block(nsub - 1, phase)

            @pl.when(jnp.logical_and(jnp.logical_not(odd_one), tail))
            def _():
                sub_block(nsub, phase, size=SUB_TAIL)

            @pl.when(jnp.logical_and(odd_one, tail))
            def _():
                sub_block(nsub - 1, phase)
                sub_block(nsub, phase, size=SUB_TAIL)

        @pl.when(f == 0)
        def _():
            all_sub_blocks("first")

        @pl.when(jnp.logical_and(f > 0, f < nf - 1))
        def _():
            all_sub_blocks("middle")

        @pl.when(f == nf - 1)
        def _():
            @pl.when(g > 0)
            def _():
                drain_stores(g - 1)

            all_sub_blocks("last")

        return carry

    lax.fori_loop(0, total, item, 0)

    @pl.when(total > 0)
    def _():
        drain_stores(n_groups - 1)


def _moe_ffn(groups, x_sorted, w_gate_up, b_gu, w_down, b_dn):
    n_rows = x_sorted.shape[0]
    _, d_ff, d = w_down.shape
    nf = d_ff // TF
    assert nf >= 2
    bgu2 = b_gu.reshape(N_EXPERTS * 2 * nf, TF)
    grid_spec = pltpu.PrefetchScalarGridSpec(
        num_scalar_prefetch=5,
        grid=(1,),
        in_specs=[pl.BlockSpec(memory_space=pl.ANY), pl.BlockSpec(memory_space=pl.ANY),
                  pl.BlockSpec(memory_space=pl.ANY),
                  pl.BlockSpec(bgu2.shape, lambda i, *_: (0, 0), pipeline_mode=pl.Buffered(1)),
                  pl.BlockSpec(b_dn.shape, lambda i, *_: (0, 0), pipeline_mode=pl.Buffered(1))],
        out_specs=pl.BlockSpec(memory_space=pl.ANY),
        scratch_shapes=[pltpu.VMEM((2, RG, d + LANES), F32), pltpu.VMEM((RG, d), F32),
                        pltpu.VMEM((2, SUB, d), F32),
                        pltpu.VMEM((2, d, TF), F32), pltpu.VMEM((2, d, TF), F32), pltpu.VMEM((2, TF, d), F32),
                        pltpu.VMEM((d, TF), BF16), pltpu.VMEM((d, TF), BF16), pltpu.VMEM((TF, d), BF16),
                        pltpu.SemaphoreType.DMA((2, 3)), pltpu.SemaphoreType.DMA(()),
                        pltpu.SemaphoreType.DMA((2,))],
    )
    return pl.pallas_call(
        _ffn_kernel,
        grid_spec=grid_spec,
        out_shape=jax.ShapeDtypeStruct((n_rows, d), F32),
        compiler_params=pltpu.CompilerParams(dimension_semantics=("arbitrary",),
                                             vmem_limit_bytes=VMEM_LIMIT_FFN),
        name="moe_ffn",
    )(*groups, x_sorted, w_gate_up, w_down, bgu2, b_dn)


def _route(logits, m_pad):
    m = logits.shape[0]
    nt = m_pad // TT
    top_v, top_i = lax.top_k(logits, TOP_K)
    gate = jnp.pad(jax.nn.softmax(top_v, axis=-1), ((0, m_pad - m), (0, 0)))
    top_i = jnp.pad(top_i.astype(jnp.int32), ((0, m_pad - m), (0, 0)), constant_values=-1)
    chosen = (top_i[:, :, None] == jnp.arange(N_EXPERTS, dtype=jnp.int32)).astype(jnp.int32)
    tiles = chosen.sum(axis=1).reshape(nt, TT, N_EXPERTS)
    cnt = (tiles.sum(axis=1) + CHUNK_ALIGN - 1) // CHUNK_ALIGN * CHUNK_ALIGN
    loff = jnp.cumsum(cnt, axis=1) - cnt
    seg = cnt.sum(axis=0)
    seg_start = jnp.cumsum(seg) - seg
    dest = seg_start[None, :] + jnp.cumsum(cnt, axis=0) - cnt
    rank = jnp.cumsum(tiles, axis=1) - tiles
    lpos_all = (loff[:, None, :] + rank).reshape(m_pad, N_EXPERTS)
    lpos = (lpos_all[:, None, :] * chosen).sum(axis=-1)
    lpos = jnp.where(top_i >= 0, lpos, -1).astype(jnp.int32)
    n_rows_bound = nt * LROWS
    n_grp_max = n_rows_bound // RG + N_EXPERTS
    grp = (seg + RG - 1) // RG
    grp_end = jnp.cumsum(grp)
    gi = jnp.arange(n_grp_max, dtype=jnp.int32)
    g_exp = jnp.minimum((gi[:, None] >= grp_end[None, :]).sum(axis=1), N_EXPERTS - 1).astype(jnp.int32)
    within = gi - (grp_end[g_exp] - grp[g_exp])
    g_start = (seg_start[g_exp] + within * RG).astype(jnp.int32)
    g_rows = jnp.clip(seg[g_exp] - within * RG, 0, RG)
    g_rows = jnp.where(gi < grp_end[-1], g_rows, 0)
    n_full = g_rows // SUB
    rest = g_rows - n_full * SUB
    g_tail = (rest > 0) & (rest <= SUB_TAIL) & (n_full >= 1)
    g_nsub = n_full + ((rest > 0) & ~g_tail)
    groups = (g_exp, g_start, g_nsub.astype(jnp.int32), g_tail.astype(jnp.int32),
              grp_end[-1:].astype(jnp.int32))
    tabs = tuple(a.reshape(-1).astype(jnp.int32) for a in (cnt, loff, dest))
    return tabs, lpos, gate, groups, n_rows_bound + SUB


def _combine_kernel(cnt_ref, loff_ref, dest_ref, src_hbm, lpos_ref, x1p_ref, x1s_ref, g_ref,
                    yp_ref, ys_ref, buf, sem):
    t = pl.program_id(0)
    last = pl.num_programs(0) - 1

    def copy(tile, off, src, size):
        return pltpu.make_async_copy(src_hbm.at[pl.ds(src, size)], buf.at[tile % 2, pl.ds(off, size)],
                                     sem.at[tile % 2])

    @pl.when(t == 0)
    def _():
        buf[...] = jnp.zeros_like(buf)
        _chunk_loop(cnt_ref, loff_ref, dest_ref, t, lambda o, g, s: copy(t, o, g, s).start())

    @pl.when(t < last)
    def _():
        _chunk_loop(cnt_ref, loff_ref, dest_ref, t + 1, lambda o, g, s: copy(t + 1, o, g, s).start())

    _chunk_loop(cnt_ref, loff_ref, dest_ref, t, lambda o, g, s: copy(t, o, g, s).wait())
    cols = lax.broadcasted_iota(jnp.int32, (TT, LROWS), 1)
    sel = jnp.zeros((TT, LROWS), F32)
    for k in range(TOP_K):
        sel = sel + jnp.where(cols == lpos_ref[:, k:k + 1], 1.0, 0.0)
    f = jnp.dot(sel.astype(BF16), buf[t % 2].astype(BF16), preferred_element_type=F32)

    @pl.when(t < last)
    def _():
        yp_ref[...] = _rms(x1p_ref[...] + f, g_ref[...])

    @pl.when(t == last)
    def _():
        ys_ref[...] = _rms(x1s_ref[...] + f, g_ref[...])


def _combine(tabs, out_sorted, lpos, x1p, x1s, g):
    d = x1p.shape[1]
    nt = lpos.shape[0] // TT
    last_p = x1p.shape[0] // TT - 1
    prompt = pl.BlockSpec((TT, d), lambda t, *_: (jnp.minimum(t, last_p), 0))
    sample = pl.BlockSpec((TT, d), lambda t, *_: (0, 0))
    grid_spec = pltpu.PrefetchScalarGridSpec(
        num_scalar_prefetch=3,
        grid=(nt,),
        in_specs=[pl.BlockSpec(memory_space=pl.ANY),
                  pl.BlockSpec((TT, TOP_K), lambda t, *_: (t, 0)),
                  prompt, sample,
                  pl.BlockSpec((1, d), lambda t, *_: (0, 0))],
        out_specs=[prompt, sample],
        scratch_shapes=[pltpu.VMEM((2, LROWS, d), F32), pltpu.SemaphoreType.DMA((2,))],
    )
    return pl.pallas_call(
        _combine_kernel,
        grid_spec=grid_spec,
        out_shape=[jax.ShapeDtypeStruct(x1p.shape, F32), jax.ShapeDtypeStruct((TT, d), F32)],
        compiler_params=_cparams(("arbitrary",)),
        name="moe_combine",
    )(*tabs, out_sorted, lpos, x1p, x1s, g.reshape(1, d))


def _t5_bucket(dist):
    max_exact = N_BUCKETS // 2
    dd = dist.astype(F32)
    large = max_exact + (jnp.log(jnp.maximum(dd, 1.0) / max_exact)
                         / math.log(BUCKET_MAX_DIST / max_exact) * (N_BUCKETS - max_exact)).astype(jnp.int32)
    large = jnp.minimum(large, N_BUCKETS - 1)
    return jnp.where(dist < max_exact, dist, large)


def _bias_tables(rel_bias):
    dist = jnp.asarray(np.arange(N_TAPS)[None, :] * np.array(DILATIONS)[:, None], jnp.int32)
    bias = jnp.transpose(rel_bias[_t5_bucket(dist)], (2, 0, 1)).astype(F32)
    by_branch = jnp.transpose(bias, (1, 0, 2))
    g, h = by_branch.shape[:2]
    row = jnp.concatenate([by_branch[:, :, ::-1], jnp.full((g, h, WIN), NEG, F32)], axis=-1)
    flat = jnp.broadcast_to(row[:, :, None, :], (g, h, WIN, 2 * WIN + 1)).reshape(g, h, -1)
    band = flat[:, :, :WIN * 2 * WIN].reshape(g, h, WIN, 2 * WIN)
    return band, by_branch[:, :, :0:-1], by_branch[:, :, 0:1]


def kernel(x_prompt, x_sample, cache_k_win, cache_v_win, state_conv, state_ssm, rel_bias, attn_norm, w_in, conv_w, conv_b, dt_bias, a_log, d_skip, ssd_norm, w_out, ffn_norm, w_router, b_router, w_gate_up, b_gate_up, w_down, b_down, final_norm):
    bp, tp, d = x_prompt.shape
    bs, ts, _ = x_sample.shape
    depth = w_in.shape[0]
    assert depth == 1 and ts == 1 and tp % (max(DILATIONS) * WIN) == 0
    keep = min(max(DILATIONS) * WIN, tp)
    band, samp, samp0 = _bias_tables(rel_bias)
    l = 0

    xp = x_prompt.reshape(bp * tp, d)
    xs = x_sample.reshape(bs * ts, d)
    w_in_bf = jnp.pad(w_in[l], ((0, 0), (0, IN_PROJ_PAD - IN_PROJ))).astype(BF16)
    w_out_bf = w_out[l].astype(BF16)
    wr = jnp.pad(w_router[l], ((0, 0), (0, LANES - N_EXPERTS))).astype(BF16)
    br = jnp.pad(b_router[l], (0, LANES - N_EXPERTS), constant_values=NEG).reshape(1, LANES)

    k, v, z, xbc, dt_raw, *branch_qkv = _inproj_prompt(xp, attn_norm[l], w_in_bf, bp)
    k3 = k.reshape(bp, tp, KV_WIDTH)
    v3 = v.reshape(bp, tp, KV_WIDTH)
    att_parts = [_attn_branch(branch_qkv[2 * gi], branch_qkv[2 * gi + 1], band[gi])
                 for gi in range(len(DILATIONS))]
    ssm, st_p = _ssd_prompt(xbc, dt_raw, z, conv_w[l], conv_b[l], dt_bias[l], a_log[l], d_skip[l],
                            ssd_norm[l], bp)
    x1p, xnp_, lgp = _outproj_prompt(att_parts, ssm, xp, w_out_bf, ffn_norm[l], wr, br)
    k_win_p = k3[:, tp - keep:].reshape(1, bp, keep, N_KV_HEADS, HEAD_DIM)
    v_win_p = v3[:, tp - keep:].reshape(1, bp, keep, N_KV_HEADS, HEAD_DIM)
    conv_p = xbc.reshape(bp, tp, CONV_DIM)[:, tp - (CONV_W - 1):][None]

    q_s, k_s, v_s, z_s, xbc_s, dt_s = _inproj(xs, attn_norm[l], w_in_bf, bs * ts)
    q_s = q_s.reshape(bs, N_ATT_HEADS, HEAD_DIM)
    k_s = k_s.reshape(bs, N_KV_HEADS, HEAD_DIM)
    v_s = v_s.reshape(bs, N_KV_HEADS, HEAD_DIM)
    wbuf = cache_k_win.shape[2]
    att_s = _attn_sample(q_s, k_s, v_s, cache_k_win[l].reshape(bs, wbuf, KV_WIDTH),
                         cache_v_win[l].reshape(bs, wbuf, KV_WIDTH), samp, samp0)
    xa_s, xdt_s, decay_s = _conv_sample(xbc_s, state_conv[l], conv_w[l], conv_b[l], dt_s, dt_bias[l],
                                        a_log[l])
    nbc = N_SSM_GROUPS * D_STATE
    h_s, y_s = _ssm_sample(xdt_s, decay_s, xa_s[:, SSM_WIDTH:SSM_WIDTH + nbc], xa_s[:, SSM_WIDTH + nbc:],
                           state_ssm[l])
    x1s, xns, lgs = _outproj_sample(att_s.reshape(bs, ATT_WIDTH), y_s, xa_s[:, :SSM_WIDTH], z_s,
                                    d_skip[l], ssd_norm[l], xs, w_out_bf, ffn_norm[l], wr, br)
    conv_s = jnp.concatenate([state_conv[l][:, 1:], xbc_s[:, None]], axis=1)[None]

    n_s = bs * ts
    assert (bp * tp) % TT == 0 and n_s <= TT
    m_pad = bp * tp + TT
    logits = jnp.concatenate([lgp[:, :N_EXPERTS], lgs[:, :N_EXPERTS]], axis=0)
    tabs, lpos, gate, groups, n_rows = _route(logits, m_pad)
    by_tile = lambda a: jnp.transpose(a.reshape(m_pad // TT, TT, TOP_K), (0, 2, 1))
    pad_rows = lambda a: jnp.pad(a, ((0, TT - n_s), (0, 0)))
    x_sorted = _dispatch(tabs, xnp_, pad_rows(xns), by_tile(lpos), by_tile(gate), n_rows)
    out_sorted = _moe_ffn(groups, x_sorted, w_gate_up[l], b_gate_up[l], w_down[l], b_down[l])
    y_p, y_s_out = _combine(tabs, out_sorted, lpos, x1p, pad_rows(x1s), final_norm)
    y_s_out = y_s_out[:n_s]

    return (y_p.reshape(bp, tp, d), y_s_out.reshape(bs, ts, d), k_win_p, v_win_p, conv_p, st_p[None],
            k_s.reshape(1, bs, ts, N_KV_HEADS, HEAD_DIM), v_s.reshape(1, bs, ts, N_KV_HEADS, HEAD_DIM),
            conv_s, h_s[None])
```

```python
import functools
import math

import jax
import jax.numpy as jnp
import numpy as np
from jax import lax
from jax.experimental import pallas as pl
from jax.experimental.pallas import tpu as pltpu

F32 = jnp.float32
BF16 = jnp.bfloat16
HIGHEST = lax.Precision.HIGHEST

LANES = 128
SUBLANES = 8
VMEM_LIMIT = 56 * 1024 * 1024

HEAD_DIM = 64
N_ATT_HEADS = 16
N_KV_HEADS = 4
KV_REP = N_ATT_HEADS // N_KV_HEADS
ATT_WIDTH = N_ATT_HEADS * HEAD_DIM
KV_WIDTH = N_KV_HEADS * HEAD_DIM
DILATIONS = (1, 4, 16)
N_TAPS = 129
WIN = N_TAPS - 1
ATT_SCALE = HEAD_DIM ** -0.5
N_BUCKETS = 32
BUCKET_MAX_DIST = 2048
SSM_HEAD_DIM = 64
N_SSM_HEADS = 16
SSM_WIDTH = N_SSM_HEADS * SSM_HEAD_DIM
N_SSM_GROUPS = 2
HEADS_PER_GROUP = N_SSM_HEADS // N_SSM_GROUPS
D_STATE = 128
CONV_W = 4
CONV_DIM = SSM_WIDTH + 2 * N_SSM_GROUPS * D_STATE
SSD_CHUNK = 128
N_EXPERTS = 32
TOP_K = 4
SWIGLU_LIMIT = 7.0
SWIGLU_ALPHA = 1.702
EPS = 1e-5
NEG = -1e30

Q0, K0, V0, Z0, X0, DT0 = 0, 1024, 1280, 1536, 2560, 4096
IN_PROJ = DT0 + N_SSM_HEADS
IN_PROJ_PAD = DT0 + LANES

TM_PROJ = 512
TM_OUT = 256
TM_MOE = 512
TF_MOE = 512
TM_FIN = 256


def _cparams(sem):
    return pltpu.CompilerParams(dimension_semantics=sem, vmem_limit_bytes=VMEM_LIMIT)


def _const_spec(shape):
    nd = len(shape)
    return pl.BlockSpec(shape, lambda *_: (0,) * nd, pipeline_mode=pl.Buffered(1))


def _rms(x, g):
    ms = jnp.mean(x * x, axis=-1, keepdims=True)
    return x * lax.rsqrt(ms + EPS) * g


def _silu(x):
    return x * (1.0 / (1.0 + jnp.exp(-x)))


def _inproj_kernel(x_ref, g_ref, w_ref, q_ref, k_ref, v_ref, z_ref, xbc_ref, dt_ref):
    xn = _rms(x_ref[...], g_ref[...]).astype(BF16)

    def mm(lo, hi):
        return jnp.dot(xn, w_ref[:, lo:hi], preferred_element_type=F32)

    q_ref[...] = (mm(Q0, K0) * ATT_SCALE).astype(BF16)
    k_ref[...] = mm(K0, V0)
    v_ref[...] = mm(V0, Z0)
    z_ref[...] = mm(Z0, X0)
    xbc_ref[...] = mm(X0, DT0)
    dt_ref[...] = mm(DT0, IN_PROJ_PAD)


def _inproj(x2d, g, w_bf, tm):
    m, d = x2d.shape
    widths = (ATT_WIDTH, KV_WIDTH, KV_WIDTH, SSM_WIDTH, CONV_DIM, LANES)
    dtypes = (BF16, F32, F32, F32, F32, F32)
    return pl.pallas_call(
        _inproj_kernel,
        grid=(m // tm,),
        in_specs=[pl.BlockSpec((tm, d), lambda i: (i, 0)),
                  _const_spec((1, d)),
                  _const_spec((d, IN_PROJ_PAD))],
        out_specs=[pl.BlockSpec((tm, w), lambda i: (i, 0)) for w in widths],
        out_shape=[jax.ShapeDtypeStruct((m, w), t) for w, t in zip(widths, dtypes)],
        compiler_params=_cparams(("parallel",)),
        name=f"inproj_{tm}",
    )(x2d, g.reshape(1, d), w_bf)


def _deinterleave_matrix(n, dil):
    p = np.zeros((n, n), np.float32)
    src = np.arange(n)
    p[(src % dil) * (n // dil) + src // dil, src] = 1.0
    return p


def _inproj_prompt_kernel(x_ref, g_ref, w_ref, p_mid_ref, p_far_ref, k_ref, v_ref, z_ref, xbc_ref, dt_ref,
                          q0_ref, kv0_ref, q1_ref, kv1_ref, q2_ref, kv2_ref):
    tm = x_ref.shape[0]
    xn = _rms(x_ref[...], g_ref[...]).astype(BF16)

    def mm(lo, hi):
        return jnp.dot(xn, w_ref[:, lo:hi], preferred_element_type=F32)

    q = (mm(Q0, K0) * ATT_SCALE).astype(BF16)
    k = mm(K0, V0)
    v = mm(V0, Z0)
    k_ref[...] = k
    v_ref[...] = v
    z_ref[...] = mm(Z0, X0)
    xbc_ref[...] = mm(X0, DT0)
    dt_ref[...] = mm(DT0, IN_PROJ_PAD)
    kv = jnp.concatenate([k, v], axis=-1).astype(BF16)
    q0_ref[0, 0] = q
    kv0_ref[0, 0] = kv
    for dil, p_ref, qd_ref, kvd_ref in ((DILATIONS[1], p_mid_ref, q1_ref, kv1_ref),
                                        (DILATIONS[2], p_far_ref, q2_ref, kv2_ref)):
        qp = jnp.dot(p_ref[...], q, preferred_element_type=F32).astype(BF16)
        kvp = jnp.dot(p_ref[...], kv, preferred_element_type=F32).astype(BF16)
        rows = tm // dil
        for r in range(dil):
            qd_ref[0, r] = qp[r * rows:(r + 1) * rows]
            kvd_ref[0, r] = kvp[r * rows:(r + 1) * rows]


def _inproj_prompt(x2d, g, w_bf, batch):
    m, d = x2d.shape
    tm = TM_PROJ
    seq = m // batch
    per_b = seq // tm
    assert DILATIONS[0] == 1 and seq % tm == 0 and tm % (max(DILATIONS) * 2 * SUBLANES) == 0
    widths = (KV_WIDTH, KV_WIDTH, SSM_WIDTH, CONV_DIM, LANES)
    rows = lambda w: pl.BlockSpec((tm, w), lambda i: (i, 0))
    split = lambda dil, w: pl.BlockSpec((1, dil, tm // dil, w), lambda i: (i // per_b, 0, i % per_b, 0))
    branch_specs, branch_shapes = [], []
    for dil in DILATIONS:
        for w in (ATT_WIDTH, 2 * KV_WIDTH):
            branch_specs.append(split(dil, w))
            branch_shapes.append(jax.ShapeDtypeStruct((batch, dil, seq // dil, w), BF16))
    perms = [jnp.asarray(_deinterleave_matrix(tm, dil), BF16) for dil in DILATIONS[1:]]
    return pl.pallas_call(
        _inproj_prompt_kernel,
        grid=(m // tm,),
        in_specs=[rows(d), _const_spec((1, d)), _const_spec((d, IN_PROJ_PAD)),
                  _const_spec((tm, tm)), _const_spec((tm, tm))],
        out_specs=[rows(w) for w in widths] + branch_specs,
        out_shape=[jax.ShapeDtypeStruct((m, w), F32) for w in widths] + branch_shapes,
        compiler_params=_cparams(("parallel",)),
        name="inproj_prompt",
    )(x2d, g.reshape(1, d), w_bf, *perms)


def _attn_kernel(q_ref, kvp_ref, kvc_ref, bias_ref, o_ref, lse_ref):
    first = pl.program_id(2) == 0
    lane = lax.broadcasted_iota(jnp.int32, (1, 2 * WIN), 1)
    prev_mask = jnp.where(jnp.logical_and(first, lane < WIN), NEG, 0.0)
    head_lane = lax.broadcasted_iota(jnp.int32, (WIN, LANES), 1)
    lse_tile = jnp.zeros((WIN, LANES), F32)
    for kvh in range(N_KV_HEADS):
        ks = slice(kvh * HEAD_DIM, (kvh + 1) * HEAD_DIM)
        vs = slice(KV_WIDTH + kvh * HEAD_DIM, KV_WIDTH + (kvh + 1) * HEAD_DIM)
        kw = jnp.concatenate([kvp_ref[0, 0, :, ks], kvc_ref[0, 0, :, ks]], axis=0)
        vw = jnp.concatenate([kvp_ref[0, 0, :, vs], kvc_ref[0, 0, :, vs]], axis=0)
        for pair in range(KV_REP // 2):
            outs = []
            for r in range(2):
                h = kvh * KV_REP + pair * 2 + r
                qh = q_ref[0, 0, :, h * HEAD_DIM:(h + 1) * HEAD_DIM]
                s = lax.dot_general(qh, kw, (((1,), (1,)), ((), ())), preferred_element_type=F32)
                s = s + bias_ref[h] + prev_mask
                mx = jnp.max(s, axis=-1, keepdims=True)
                p = jnp.exp(s - mx)
                l = jnp.sum(p, axis=-1, keepdims=True)
                o = jnp.dot(p.astype(BF16), vw, preferred_element_type=F32)
                outs.append(o * (1.0 / l))
                lse_tile = jnp.where(head_lane == h, mx + jnp.log(l), lse_tile)
            h0 = kvh * KV_REP + pair * 2
            o_ref[0, 0, :, h0 * HEAD_DIM:(h0 + 2) * HEAD_DIM] = jnp.concatenate(outs, axis=-1).astype(BF16)
    lse_ref[0, 0] = lse_tile


def _attn_branch(q, kv, bias_mat):
    b, dil, sub, _ = q.shape
    nb = sub // WIN
    cur = lambda bb, r, i: (bb, r, i, 0)
    prev = lambda bb, r, i: (bb, r, jnp.maximum(i - 1, 0), 0)
    return pl.pallas_call(
        _attn_kernel,
        grid=(b, dil, nb),
        in_specs=[pl.BlockSpec((1, 1, WIN, ATT_WIDTH), cur),
                  pl.BlockSpec((1, 1, WIN, 2 * KV_WIDTH), prev),
                  pl.BlockSpec((1, 1, WIN, 2 * KV_WIDTH), cur),
                  _const_spec((N_ATT_HEADS, WIN, 2 * WIN))],
        out_specs=[pl.BlockSpec((1, 1, WIN, ATT_WIDTH), cur),
                   pl.BlockSpec((1, 1, WIN, LANES), cur)],
        out_shape=[jax.ShapeDtypeStruct((b, dil, sub, ATT_WIDTH), BF16),
                   jax.ShapeDtypeStruct((b, dil, sub, LANES), F32)],
        compiler_params=_cparams(("parallel", "parallel", "arbitrary")),
        name=f"attn_dil{dil}",
    )(q, kv, kv, bias_mat)


def _softplus(x):
    return jnp.maximum(x, 0.0) + jnp.log(1.0 + jnp.exp(-jnp.abs(x)))


def _ssd_kernel(xbc_ref, dt_ref, z_ref, cw_ref, cb_ref, dtb_ref, alog_ref, dskip_ref, gn_ref, e_ref,
                y_ref, st_ref, ext_ref, state_ref):
    c = pl.program_id(1)
    L = SSD_CHUNK

    @pl.when(c == 0)
    def _():
        ext_ref[0:SUBLANES, :] = jnp.zeros((SUBLANES, CONV_DIM), F32)
        state_ref[...] = jnp.zeros_like(state_ref)

    ext_ref[SUBLANES:SUBLANES + L, :] = xbc_ref[...]
    acc = cb_ref[...] + ext_ref[SUBLANES:SUBLANES + L, :] * cw_ref[CONV_W - 1:CONV_W, :]
    for i in range(CONV_W - 1):
        off = SUBLANES - (CONV_W - 1) + i
        acc = acc + ext_ref[off:off + L, :] * cw_ref[i:i + 1, :]
    ext_ref[0:SUBLANES, :] = ext_ref[L:L + SUBLANES, :]
    xa = _silu(acc)

    dt = _softplus(dt_ref[...] + dtb_ref[...])
    da = dt * (-jnp.exp(alog_ref[...]))
    row = lax.broadcasted_iota(jnp.int32, (L, L), 0)
    col = lax.broadcasted_iota(jnp.int32, (L, L), 1)
    tri = row >= col
    a_cs = jnp.dot(tri.astype(F32), da, preferred_element_type=F32, precision=HIGHEST)
    a_cs_t = a_cs.T
    expand = e_ref[...]
    acs_full = jnp.dot(a_cs, expand, preferred_element_type=F32, precision=HIGHEST)
    dt_full = jnp.dot(dt, expand, preferred_element_type=F32, precision=HIGHEST)
    exp_acs = jnp.exp(acs_full)
    a_last = acs_full[L - 1:L, :]
    exp_last = exp_acs[L - 1:L, :]
    xs = xa[:, :SSM_WIDTH]
    xdt = xs * dt_full
    xw = xdt * jnp.exp(a_last - acs_full)

    for g in range(N_SSM_GROUPS):
        b0 = SSM_WIDTH + g * D_STATE
        c0 = SSM_WIDTH + N_SSM_GROUPS * D_STATE + g * D_STATE
        bg_t = xa[:, b0:b0 + D_STATE].T.astype(BF16)
        cg = xa[:, c0:c0 + D_STATE].astype(BF16)
        gram = jnp.dot(cg, bg_t, preferred_element_type=F32)
        for hh in range(HEADS_PER_GROUP):
            h = g * HEADS_PER_GROUP + hh
            hs = slice(h * SSM_HEAD_DIM, (h + 1) * SSM_HEAD_DIM)
            seg = jnp.where(tri, a_cs[:, h:h + 1] - a_cs_t[h:h + 1, :], NEG)
            scores = (gram * jnp.exp(seg)).astype(BF16)
            y_diag = jnp.dot(scores, xdt[:, hs].astype(BF16), preferred_element_type=F32)
            st = state_ref[h]
            y_off = jnp.dot(cg, st.astype(BF16), preferred_element_type=F32) * exp_acs[:, hs]
            y_ref[:, hs] = y_diag + y_off
            state_ref[h] = exp_last[:, hs] * st + jnp.dot(bg_t, xw[:, hs].astype(BF16),
                                                          preferred_element_type=F32)

    y = y_ref[...] + dskip_ref[...] * xs
    u = y * _silu(z_ref[...])
    gw = SSM_WIDTH // N_SSM_GROUPS
    parts = []
    for g in range(N_SSM_GROUPS):
        ug = u[:, g * gw:(g + 1) * gw]
        parts.append(ug * lax.rsqrt(jnp.mean(ug * ug, axis=-1, keepdims=True) + EPS))
    y_ref[...] = jnp.concatenate(parts, axis=-1) * gn_ref[...]

    @pl.when(c == pl.num_programs(1) - 1)
    def _():
        st_ref[0] = state_ref[...]


def _head_expand():
    e = np.zeros((LANES, SSM_WIDTH), np.float32)
    for h in range(N_SSM_HEADS):
        e[h, h * SSM_HEAD_DIM:(h + 1) * SSM_HEAD_DIM] = 1.0
    return jnp.asarray(e)


def _pad_lanes(v):
    return jnp.pad(v.astype(F32), (0, LANES - v.shape[0])).reshape(1, LANES)


def _ssd_prompt(xbc, dt_raw, z, conv_w, conv_b, dt_bias, a_log, d_skip, ssd_norm, batch):
    m = xbc.shape[0]
    nc = m // batch // SSD_CHUNK
    L = SSD_CHUNK
    rows = lambda b, c: (b * nc + c, 0)
    y, st = pl.pallas_call(
        _ssd_kernel,
        grid=(batch, nc),
        in_specs=[pl.BlockSpec((L, CONV_DIM), rows),
                  pl.BlockSpec((L, LANES), rows),
                  pl.BlockSpec((L, SSM_WIDTH), rows),
                  _const_spec((CONV_W, CONV_DIM)),
                  _const_spec((1, CONV_DIM)),
                  _const_spec((1, LANES)),
                  _const_spec((1, LANES)),
                  _const_spec((1, SSM_WIDTH)),
                  _const_spec((1, SSM_WIDTH)),
                  _const_spec((LANES, SSM_WIDTH))],
        out_specs=[pl.BlockSpec((L, SSM_WIDTH), rows),
                   pl.BlockSpec((1, N_SSM_HEADS, D_STATE, SSM_HEAD_DIM), lambda b, c: (b, 0, 0, 0))],
        out_shape=[jax.ShapeDtypeStruct((m, SSM_WIDTH), F32),
                   jax.ShapeDtypeStruct((batch, N_SSM_HEADS, D_STATE, SSM_HEAD_DIM), F32)],
        scratch_shapes=[pltpu.VMEM((SUBLANES + L, CONV_DIM), F32),
                        pltpu.VMEM((N_SSM_HEADS, D_STATE, SSM_HEAD_DIM), F32)],
        compiler_params=_cparams(("parallel", "arbitrary")),
        name="ssd_prompt",
    )(xbc, dt_raw, z, conv_w, conv_b.reshape(1, CONV_DIM), _pad_lanes(dt_bias), _pad_lanes(a_log),
      jnp.repeat(d_skip, SSM_HEAD_DIM).reshape(1, SSM_WIDTH), ssd_norm.reshape(1, SSM_WIDTH),
      _head_expand())
    return y, jnp.swapaxes(st, -1, -2)


def _router(xn_bf, wr_ref, br_ref):
    return jnp.dot(xn_bf, wr_ref[...], preferred_element_type=F32) + br_ref[...]


def _outproj_kernel(o0, l0, o1, l1, o2, l2, u1_ref, u2_ref, e_ref, ssm_ref, x_ref, w_ref, g_ref, wr_ref,
                    br_ref, x1_ref, xn_ref, lg_ref):
    def natural(o_ref, l_ref, u_ref):
        dil = o_ref.shape[1]
        if dil == 1:
            return o_ref[0, 0].astype(F32), l_ref[0, 0]
        o_cat = jnp.concatenate([o_ref[0, r] for r in range(dil)], axis=0)
        l_cat = jnp.concatenate([l_ref[0, r] for r in range(dil)], axis=0)
        hi = l_cat.astype(BF16)
        lo = (l_cat - hi.astype(F32)).astype(BF16)
        u = u_ref[...]
        return (jnp.dot(u, o_cat, preferred_element_type=F32),
                jnp.dot(u, hi, preferred_element_type=F32) + jnp.dot(u, lo, preferred_element_type=F32))

    branches = [natural(o0, l0, None), natural(o1, l1, u1_ref), natural(o2, l2, u2_ref)]
    lses = [jnp.dot(l, e_ref[...], preferred_element_type=F32, precision=HIGHEST) for _, l in branches]
    mx = functools.reduce(jnp.maximum, lses)
    ws = [jnp.exp(l - mx) for l in lses]
    num = functools.reduce(jnp.add, [w * o for w, (o, _) in zip(ws, branches)])
    att = num * (1.0 / functools.reduce(jnp.add, ws))
    y = jnp.dot(att.astype(BF16), w_ref[:ATT_WIDTH, :], preferred_element_type=F32)
    y = y + jnp.dot(ssm_ref[...].astype(BF16), w_ref[ATT_WIDTH:, :], preferred_element_type=F32)
    x1 = x_ref[...] + y
    x1_ref[...] = x1
    xn = _rms(x1, g_ref[...]).astype(BF16)
    xn_ref[...] = xn
    lg_ref[...] = _router(xn, wr_ref, br_ref)


def _outproj_prompt(att_parts, ssm, x2d, w_bf, g, wr, br):
    m, d = x2d.shape
    mix = w_bf.shape[0]
    tm = TM_OUT
    batch = att_parts[0][0].shape[0]
    per_b = m // batch // tm
    assert tm % (max(DILATIONS) * 2 * SUBLANES) == 0
    row = lambda w: pl.BlockSpec((tm, w), lambda i: (i, 0))
    split = lambda dil, w: pl.BlockSpec((1, dil, tm // dil, w), lambda i: (i // per_b, 0, i % per_b, 0))
    branch_specs = [split(o.shape[1], w) for o, _ in att_parts for w in (ATT_WIDTH, LANES)]
    interleave = [jnp.asarray(_deinterleave_matrix(tm, dil).T, BF16) for dil in DILATIONS[1:]]
    return pl.pallas_call(
        _outproj_kernel,
        grid=(m // tm,),
        in_specs=branch_specs + [_const_spec((tm, tm)), _const_spec((tm, tm)),
                                 _const_spec((LANES, ATT_WIDTH)), row(SSM_WIDTH), row(d),
                                 _const_spec((mix, d)), _const_spec((1, d)),
                                 _const_spec((d, LANES)), _const_spec((1, LANES))],
        out_specs=[row(d), row(d), row(LANES)],
        out_shape=[jax.ShapeDtypeStruct((m, d), F32), jax.ShapeDtypeStruct((m, d), BF16),
                   jax.ShapeDtypeStruct((m, LANES), F32)],
        compiler_params=_cparams(("parallel",)),
        name="outproj_prompt",
    )(*[a for part in att_parts for a in part], *interleave, _head_expand(), ssm, x2d, w_bf,
      g.reshape(1, d), wr, br)


def _attn_sample_kernel(q_ref, kn_ref, vn_ref, kc_ref, vc_ref, bias_ref, bias0_ref, o_ref):
    w = kc_ref.shape[1]
    q = q_ref[0]
    head_grp = lax.broadcasted_iota(jnp.int32, (N_ATT_HEADS, 1), 0) // KV_REP
    kn = kn_ref[0].astype(BF16).astype(F32)
    vn = vn_ref[0].astype(BF16).astype(F32)
    s_self = jnp.sum(q.astype(F32) * kn, axis=-1, keepdims=True)

    def taps(c_ref, dil):
        span = WIN * dil
        rows = c_ref[0, w - span:w, :].astype(BF16)
        if dil == 1:
            return rows
        tap = lax.broadcasted_iota(jnp.int32, (WIN, span), 0)
        col = lax.broadcasted_iota(jnp.int32, (WIN, span), 1)
        pick = jnp.where(col == tap * dil, 1.0, 0.0).astype(BF16)
        return jnp.dot(pick, rows, preferred_element_type=F32).astype(BF16)

    scores, selfs, lses = [], [], []
    for g, dil in enumerate(DILATIONS):
        kk = taps(kc_ref, dil)
        s = jnp.zeros((N_ATT_HEADS, WIN), F32)
        for kvh in range(N_KV_HEADS):
            cs = slice(kvh * HEAD_DIM, (kvh + 1) * HEAD_DIM)
            sk = lax.dot_general(q, kk[:, cs], (((1,), (1,)), ((), ())), preferred_element_type=F32)
            s = jnp.where(head_grp == kvh, sk, s)
        s = s + bias_ref[g]
        s0 = s_self + bias0_ref[g]
        mx = jnp.maximum(jnp.max(s, axis=-1, keepdims=True), s0)
        lse = mx + jnp.log(jnp.sum(jnp.exp(s - mx), axis=-1, keepdims=True) + jnp.exp(s0 - mx))
        scores.append(s); selfs.append(s0); lses.append(lse)
    top = functools.reduce(jnp.maximum, lses)
    es = [jnp.exp(l - top) for l in lses]
    tot = functools.reduce(jnp.add, es)
    o = jnp.zeros((N_ATT_HEADS, HEAD_DIM), F32)
    for s, s0, lse, e, dil in zip(scores, selfs, lses, es, DILATIONS):
        wgt = e / tot
        p = (jnp.exp(s - lse) * wgt).astype(BF16)
        p0 = (jnp.exp(s0 - lse) * wgt).astype(BF16).astype(F32)
        vv = taps(vc_ref, dil)
        o = o + p0 * vn
        for kvh in range(N_KV_HEADS):
            cs = slice(kvh * HEAD_DIM, (kvh + 1) * HEAD_DIM)
            ok = jnp.dot(p, vv[:, cs], preferred_element_type=F32)
            o = o + jnp.where(head_grp == kvh, ok, 0.0)
    o_ref[0] = o


def _attn_sample(q, k_new, v_new, k_cache, v_cache, bias_s, bias0_s):
    n, w = k_cache.shape[0], k_cache.shape[1]
    assert w % (max(DILATIONS) * WIN) == 0
    tok = lambda b: (b, 0, 0)
    head = pl.BlockSpec((1, N_ATT_HEADS, HEAD_DIM), tok)
    window = pl.BlockSpec((1, w, KV_WIDTH), tok)
    return pl.pallas_call(
        _attn_sample_kernel,
        grid=(n,),
        in_specs=[head, head, head, window, window,
                  _const_spec((len(DILATIONS), N_ATT_HEADS, WIN)),
                  _const_spec((len(DILATIONS), N_ATT_HEADS, 1))],
        out_specs=head,
        out_shape=jax.ShapeDtypeStruct((n, N_ATT_HEADS, HEAD_DIM), F32),
        compiler_params=_cparams(("parallel",)),
        name="attn_sample",
    )(q, jnp.repeat(k_new, KV_REP, axis=1), jnp.repeat(v_new, KV_REP, axis=1), k_cache, v_cache,
      bias_s, bias0_s)


def _conv_sample_kernel(xbc_ref, b0_ref, b1_ref, b2_ref, cw_ref, cb_ref, dt_ref, dtb_ref, alog_ref,
                        e_ref, xa_ref, xdt_ref, decay_ref):
    acc = cb_ref[...] + xbc_ref[...] * cw_ref[CONV_W - 1:CONV_W, :]
    for i, buf in enumerate((b0_ref, b1_ref, b2_ref)):
        acc = acc + buf[...] * cw_ref[i:i + 1, :]
    xa = _silu(acc)
    xa_ref[...] = xa
    dt = _softplus(dt_ref[...] + dtb_ref[...])
    decay_ref[...] = jnp.exp(dt * (-jnp.exp(alog_ref[...])))
    dt_full = jnp.dot(dt, e_ref[...], preferred_element_type=F32, precision=HIGHEST)
    xdt_ref[...] = xa[:, :SSM_WIDTH] * dt_full


def _conv_sample(xbc, conv_buf, conv_w, conv_b, dt_raw, dt_bias, a_log):
    n = xbc.shape[0]
    args = (xbc, conv_buf[:, 0], conv_buf[:, 1], conv_buf[:, 2], conv_w, conv_b.reshape(1, CONV_DIM),
            dt_raw, _pad_lanes(dt_bias), _pad_lanes(a_log),
            _head_expand())
    return pl.pallas_call(
        _conv_sample_kernel,
        grid=(1,),
        in_specs=[_const_spec(a.shape) for a in args],
        out_specs=[_const_spec((n, CONV_DIM)), _const_spec((n, SSM_WIDTH)), _const_spec((n, LANES))],
        out_shape=[jax.ShapeDtypeStruct((n, CONV_DIM), F32), jax.ShapeDtypeStruct((n, SSM_WIDTH), F32),
                   jax.ShapeDtypeStruct((n, LANES), F32)],
        compiler_params=_cparams(("arbitrary",)),
        name="conv_sample",
    )(*args)


def _ssm_sample_kernel(xdt_ref, decay_ref, b_ref, c_ref, h0_ref, hn_ref, y_ref):
    for g in range(N_SSM_GROUPS):
        hs = slice(g * HEADS_PER_GROUP, (g + 1) * HEADS_PER_GROUP)
        hn = decay_ref[0, hs] * h0_ref[0, hs] + xdt_ref[0, hs] * b_ref[0, g]
        hn_ref[0, hs] = hn
        c_row = c_ref[0, g].astype(BF16).astype(F32)
        y_ref[0, hs] = jnp.sum(hn.astype(BF16).astype(F32) * c_row, axis=-1, keepdims=True)


def _ssm_sample(xdt, decay, bmat, cmat, h0):
    n = xdt.shape[0]
    p = SSM_HEAD_DIM
    tok4 = lambda b: (b, 0, 0, 0)
    hn, y = pl.pallas_call(
        _ssm_sample_kernel,
        grid=(n,),
        in_specs=[pl.BlockSpec((1, N_SSM_HEADS, p, 1), tok4),
                  pl.BlockSpec((1, N_SSM_HEADS, 1, 1), tok4),
                  pl.BlockSpec((1, N_SSM_GROUPS, 1, D_STATE), tok4),
                  pl.BlockSpec((1, N_SSM_GROUPS, 1, D_STATE), tok4),
                  pl.BlockSpec((1, N_SSM_HEADS, p, D_STATE), tok4)],
        out_specs=[pl.BlockSpec((1, N_SSM_HEADS, p, D_STATE), tok4),
                   pl.BlockSpec((1, N_SSM_HEADS, p, 1), tok4)],
        out_shape=[jax.ShapeDtypeStruct((n, N_SSM_HEADS, p, D_STATE), F32),
                   jax.ShapeDtypeStruct((n, N_SSM_HEADS, p, 1), F32)],
        compiler_params=_cparams(("parallel",)),
        name="ssm_sample",
    )(xdt.reshape(n, N_SSM_HEADS, p, 1), decay[:, :N_SSM_HEADS].reshape(n, N_SSM_HEADS, 1, 1),
      bmat.reshape(n, N_SSM_GROUPS, 1, D_STATE), cmat.reshape(n, N_SSM_GROUPS, 1, D_STATE), h0)
    return hn, y.reshape(n, SSM_WIDTH)


def _outproj_sample_kernel(att_ref, y_ref, xs_ref, z_ref, dskip_ref, gn_ref, x_ref, w_ref, g_ref,
                           wr_ref, br_ref, x1_ref, xn_ref, lg_ref):
    y = y_ref[...] + dskip_ref[...] * xs_ref[...]
    u = y * _silu(z_ref[...])
    gw = SSM_WIDTH // N_SSM_GROUPS
    parts = []
    for g in range(N_SSM_GROUPS):
        ug = u[:, g * gw:(g + 1) * gw]
        parts.append(ug * lax.rsqrt(jnp.mean(ug * ug, axis=-1, keepdims=True) + EPS)
                     * gn_ref[:, g * gw:(g + 1) * gw])
    mix = jnp.concatenate([att_ref[...]] + parts, axis=-1).astype(BF16)
    x1 = x_ref[...] + jnp.dot(mix, w_ref[...], preferred_element_type=F32)
    x1_ref[...] = x1
    xn = _rms(x1, g_ref[...]).astype(BF16)
    xn_ref[...] = xn
    lg_ref[...] = _router(xn, wr_ref, br_ref)


def _outproj_sample(att, y, xs, z, d_skip, ssd_norm, x2d, w_bf, g, wr, br):
    n, d = x2d.shape
    args = (att, y, xs, z, jnp.repeat(d_skip, SSM_HEAD_DIM).reshape(1, SSM_WIDTH),
            ssd_norm.reshape(1, SSM_WIDTH), x2d, w_bf, g.reshape(1, d), wr, br)
    return pl.pallas_call(
        _outproj_sample_kernel,
        grid=(1,),
        in_specs=[_const_spec(a.shape) for a in args],
        out_specs=[_const_spec((n, d)), _const_spec((n, d)), _const_spec((n, LANES))],
        out_shape=[jax.ShapeDtypeStruct((n, d), F32), jax.ShapeDtypeStruct((n, d), BF16),
                   jax.ShapeDtypeStruct((n, LANES), F32)],
        compiler_params=_cparams(("arbitrary",)),
        name="outproj_sample",
    )(*args)


TT = 256
CHUNK_ALIGN = SUBLANES
LROWS = -(-(TT * TOP_K + N_EXPERTS * (CHUNK_ALIGN - 1)) // TT) * TT
CHUNK_SIZES = tuple(1 << b for b in range(TT.bit_length() - 1, CHUNK_ALIGN.bit_length() - 2, -1))
SUB = 256
SUB_TAIL = SUB // 2
RG = 5 * SUB
TF = 256
VMEM_LIMIT_FFN = 60000 * 1024


def _chunk_loop(cnt_ref, loff_ref, dest_ref, tile, fn):
    def per_expert(e, carry):
        idx = tile * N_EXPERTS + e
        n, off, dst = cnt_ref[idx], loff_ref[idx], dest_ref[idx]
        for size in CHUNK_SIZES:
            take = (n & size) != 0

            @pl.when(take)
            def _(off=off, dst=dst, size=size):
                fn(pl.multiple_of(off, CHUNK_ALIGN), pl.multiple_of(dst, CHUNK_ALIGN), size)

            step = jnp.where(take, size, 0)
            off, dst = off + step, dst + step
        return carry

    lax.fori_loop(0, N_EXPERTS, per_expert, 0)


def _dispatch_kernel(cnt_ref, loff_ref, dest_ref, xp_ref, xs_ref, lpos_ref, gate_ref, out_hbm, buf, sem):
    t = pl.program_id(0)
    last = pl.num_programs(0) - 1
    d = xp_ref.shape[1]
    x = jnp.where(t == last, xs_ref[...], xp_ref[...])
    rows = lax.broadcasted_iota(jnp.int32, (LROWS, TT), 0)
    onehot = jnp.zeros((LROWS, TT), F32)
    wcol = jnp.zeros((LROWS, 1), F32)
    for k in range(TOP_K):
        hit = jnp.where(rows == lpos_ref[0, k:k + 1, :], 1.0, 0.0)
        onehot = onehot + hit
        wcol = wcol + jnp.sum(hit * gate_ref[0, k:k + 1, :], axis=-1, keepdims=True)
    tile_buf = buf.at[t % 2]
    tile_buf[:, 0:d] = jnp.dot(onehot.astype(BF16), x, preferred_element_type=F32)
    tile_buf[:, d:d + LANES] = jnp.broadcast_to(wcol, (LROWS, LANES))

    def copy(tile, off, dst, size):
        return pltpu.make_async_copy(buf.at[tile % 2, pl.ds(off, size)], out_hbm.at[pl.ds(dst, size)],
                                     sem.at[tile % 2])

    _chunk_loop(cnt_ref, loff_ref, dest_ref, t, lambda o, g, s: copy(t, o, g, s).start())

    @pl.when(t > 0)
    def _():
        _chunk_loop(cnt_ref, loff_ref, dest_ref, t - 1, lambda o, g, s: copy(t - 1, o, g, s).wait())

    @pl.when(t == last)
    def _():
        _chunk_loop(cnt_ref, loff_ref, dest_ref, t, lambda o, g, s: copy(t, o, g, s).wait())


def _dispatch(tabs, xn_p, xn_s, lpos_t, gate_t, n_rows):
    nt = lpos_t.shape[0]
    d = xn_p.shape[1]
    last_p = xn_p.shape[0] // TT - 1
    grid_spec = pltpu.PrefetchScalarGridSpec(
        num_scalar_prefetch=3,
        grid=(nt,),
        in_specs=[pl.BlockSpec((TT, d), lambda t, *_: (jnp.minimum(t, last_p), 0)),
                  pl.BlockSpec((TT, d), lambda t, *_: (0, 0)),
                  pl.BlockSpec((1, TOP_K, TT), lambda t, *_: (t, 0, 0)),
                  pl.BlockSpec((1, TOP_K, TT), lambda t, *_: (t, 0, 0))],
        out_specs=pl.BlockSpec(memory_space=pl.ANY),
        scratch_shapes=[pltpu.VMEM((2, LROWS, d + LANES), F32), pltpu.SemaphoreType.DMA((2,))],
    )
    return pl.pallas_call(
        _dispatch_kernel,
        grid_spec=grid_spec,
        out_shape=jax.ShapeDtypeStruct((n_rows, d + LANES), F32),
        compiler_params=_cparams(("arbitrary",)),
        name="moe_dispatch",
    )(*tabs, xn_p, xn_s, lpos_t, gate_t)


def _ffn_kernel(ge_ref, gs_ref, gn_ref, gt_ref, ng_ref, xs_hbm, wgu_hbm, wd_hbm, bgu_ref, bdn_ref, out_hbm,
                xbuf, acc, ostage, wg_st, wu_st, wd_st, wg_bf, wu_bf, wd_bf, sem_w, sem_x, sem_o):
    d = acc.shape[1]
    d_ff = wd_hbm.shape[1]
    nf = d_ff // TF
    n_groups = ng_ref[0]
    total = n_groups * nf

    def w_copies(s, slot):
        g = s // nf
        f = s - g * nf
        e = ge_ref[g]
        c0 = pl.multiple_of(f * TF, TF)
        return (pltpu.make_async_copy(wgu_hbm.at[e, :, pl.ds(c0, TF)], wg_st.at[slot], sem_w.at[slot, 0]),
                pltpu.make_async_copy(wgu_hbm.at[e, :, pl.ds(d_ff + c0, TF)], wu_st.at[slot], sem_w.at[slot, 1]),
                pltpu.make_async_copy(wd_hbm.at[e, pl.ds(c0, TF), :], wd_st.at[slot], sem_w.at[slot, 2]))

    def x_copy(g, j, size=SUB):
        r0 = pl.multiple_of(j * SUB, SUB)
        return pltpu.make_async_copy(xs_hbm.at[pl.ds(pl.multiple_of(gs_ref[g] + r0, CHUNK_ALIGN), size)],
                                     xbuf.at[g % 2, pl.ds(r0, size)], sem_x)

    def o_copy(g, j, size=SUB):
        r0 = pl.multiple_of(j * SUB, SUB)
        return pltpu.make_async_copy(ostage.at[j % 2, pl.ds(0, size)],
                                     out_hbm.at[pl.ds(pl.multiple_of(gs_ref[g] + r0, CHUNK_ALIGN), size)],
                                     sem_o.at[j % 2])

    def loop(n, fn):
        lax.fori_loop(0, n, lambda j, c: (fn(j), c)[1], 0)

    def group_rows(g, op):
        loop(gn_ref[g], lambda j: op(x_copy(g, j)))

        @pl.when(gt_ref[g] == 1)
        def _():
            op(x_copy(g, gn_ref[g], SUB_TAIL))

    def drain_stores(g):
        n = gn_ref[g]
        tail = gt_ref[g]

        @pl.when(tail == 1)
        def _():
            o_copy(g, n, SUB_TAIL).wait()

        @pl.when(n >= 1)
        def _():
            o_copy(g, n - 1).wait()

        @pl.when(jnp.logical_and(n >= 2, tail == 0))
        def _():
            o_copy(g, n - 2).wait()

    @pl.when(total > 0)
    def _():
        for c in w_copies(0, 0):
            c.start()
        group_rows(0, lambda c: c.start())

    def item(s, carry):
        slot = s % 2
        g = s // nf
        f = s - g * nf
        e = ge_ref[g]
        nsub = gn_ref[g]
        xg = xbuf.at[g % 2]

        @pl.when(s + 1 < total)
        def _():
            for c in w_copies(s + 1, 1 - slot):
                c.start()

        @pl.when(f == 0)
        def _():
            group_rows(g, lambda c: c.wait())

        @pl.when(jnp.logical_and(f == 1, g + 1 < n_groups))
        def _():
            group_rows(g + 1, lambda c: c.start())

        for c in w_copies(s, slot):
            c.wait()
        bg = bgu_ref[pl.ds(e * 2 * nf + f, 1), :]
        bu = bgu_ref[pl.ds(e * 2 * nf + nf + f, 1), :]

        def sub_block(j, phase, cast=False, size=SUB):
            rs = pl.ds(pl.multiple_of(j * SUB, SUB), size)
            x = xg[rs, 0:d].astype(BF16)
            if cast:
                wg, wu, wd = (st[slot].astype(BF16) for st in (wg_st, wu_st, wd_st))
                wg_bf[...], wu_bf[...], wd_bf[...] = wg, wu, wd
            else:
                wg, wu, wd = wg_bf[...], wu_bf[...], wd_bf[...]
            hg = jnp.dot(x, wg, preferred_element_type=F32) + bg
            hu = jnp.dot(x, wu, preferred_element_type=F32) + bu
            gg = jnp.minimum(hg, SWIGLU_LIMIT)
            uu = jnp.clip(hu, -SWIGLU_LIMIT, SWIGLU_LIMIT)
            act = gg * (1.0 / (1.0 + jnp.exp(-SWIGLU_ALPHA * gg))) * (uu + 1.0)
            part = jnp.dot(act.astype(BF16), wd, preferred_element_type=F32)
            if phase == "first":
                acc[rs, :] = part + bdn_ref[pl.ds(e, 1), :]
            elif phase == "middle":
                acc[rs, :] += part
            else:
                @pl.when(j >= 2)
                def _():
                    o_copy(g, j - 2).wait()

                ostage[j % 2, 0:size] = (acc[rs, :] + part) * xg[rs, d:d + 1]
                o_copy(g, j, size).start()

        def all_sub_blocks(phase):
            sub_block(jnp.int32(0), phase, cast=True)

            def pair(p):
                if phase == "last":
                    sub_block(2 * p + 1, phase)
                    sub_block(2 * p + 2, phase)
                else:
                    sub_block(2 * p + 1, phase, size=2 * SUB)

            loop((nsub - 1) // 2, pair)
            odd_one = nsub % 2 == 0
            tail = gt_ref[g] == 1

            @pl.when(jnp.logical_and(odd_one, jnp.logical_not(tail)))
            def _():
                sub_block(nsub - 1, phase)

            @pl.when(jnp.logical_and(jnp.logical_not(odd_one), tail))
            def _():
                sub_block(nsub, phase, size=SUB_TAIL)

            @pl.when(jnp.logical_and(odd_one, tail))
            def _():
                if phase == "last":
                    sub_block(nsub - 1, phase)
                    sub_block(nsub, phase, size=SUB_TAIL)
                else:
                    sub_block(nsub - 1, phase, size=SUB + SUB_TAIL)

        @pl.when(f == 0)
        def _():
            all_sub_blocks("first")

        @pl.when(jnp.logical_and(f > 0, f < nf - 1))
        def _():
            all_sub_blocks("middle")

        @pl.when(f == nf - 1)
        def _():
            @pl.when(g > 0)
            def _():
                drain_stores(g - 1)

            all_sub_blocks("last")

        return carry

    lax.fori_loop(0, total, item, 0)

    @pl.when(total > 0)
    def _():
        drain_stores(n_groups - 1)


def _moe_ffn(groups, x_sorted, w_gate_up, b_gu, w_down, b_dn):
    n_rows = x_sorted.shape[0]
    _, d_ff, d = w_down.shape
    nf = d_ff // TF
    assert nf >= 2
    bgu2 = b_gu.reshape(N_EXPERTS * 2 * nf, TF)
    grid_spec = pltpu.PrefetchScalarGridSpec(
        num_scalar_prefetch=5,
        grid=(1,),
        in_specs=[pl.BlockSpec(memory_space=pl.ANY), pl.BlockSpec(memory_space=pl.ANY),
                  pl.BlockSpec(memory_space=pl.ANY),
                  pl.BlockSpec(bgu2.shape, lambda i, *_: (0, 0), pipeline_mode=pl.Buffered(1)),
                  pl.BlockSpec(b_dn.shape, lambda i, *_: (0, 0), pipeline_mode=pl.Buffered(1))],
        out_specs=pl.BlockSpec(memory_space=pl.ANY),
        scratch_shapes=[pltpu.VMEM((2, RG, d + LANES), F32), pltpu.VMEM((RG, d), F32),
                        pltpu.VMEM((2, SUB, d), F32),
                        pltpu.VMEM((2, d, TF), F32), pltpu.VMEM((2, d, TF), F32), pltpu.VMEM((2, TF, d), F32),
                        pltpu.VMEM((d, TF), BF16), pltpu.VMEM((d, TF), BF16), pltpu.VMEM((TF, d), BF16),
                        pltpu.SemaphoreType.DMA((2, 3)), pltpu.SemaphoreType.DMA(()),
                        pltpu.SemaphoreType.DMA((2,))],
    )
    return pl.pallas_call(
        _ffn_kernel,
        grid_spec=grid_spec,
        out_shape=jax.ShapeDtypeStruct((n_rows, d), F32),
        compiler_params=pltpu.CompilerParams(dimension_semantics=("arbitrary",),
                                             vmem_limit_bytes=VMEM_LIMIT_FFN),
        name="moe_ffn",
    )(*groups, x_sorted, w_gate_up, w_down, bgu2, b_dn)


def _route(logits, m_pad):
    m = logits.shape[0]
    nt = m_pad // TT
    top_v, top_i = lax.top_k(logits, TOP_K)
    gate = jnp.pad(jax.nn.softmax(top_v, axis=-1), ((0, m_pad - m), (0, 0)))
    top_i = jnp.pad(top_i.astype(jnp.int32), ((0, m_pad - m), (0, 0)), constant_values=-1)
    chosen = (top_i[:, :, None] == jnp.arange(N_EXPERTS, dtype=jnp.int32)).astype(jnp.int32)
    tiles = chosen.sum(axis=1).reshape(nt, TT, N_EXPERTS)
    cnt = (tiles.sum(axis=1) + CHUNK_ALIGN - 1) // CHUNK_ALIGN * CHUNK_ALIGN
    loff = jnp.cumsum(cnt, axis=1) - cnt
    seg = cnt.sum(axis=0)
    seg_start = jnp.cumsum(seg) - seg
    dest = seg_start[None, :] + jnp.cumsum(cnt, axis=0) - cnt
    rank = jnp.cumsum(tiles, axis=1) - tiles
    lpos_all = (loff[:, None, :] + rank).reshape(m_pad, N_EXPERTS)
    lpos = (lpos_all[:, None, :] * chosen).sum(axis=-1)
    lpos = jnp.where(top_i >= 0, lpos, -1).astype(jnp.int32)
    n_rows_bound = nt * LROWS
    n_grp_max = n_rows_bound // RG + N_EXPERTS
    grp = (seg + RG - 1) // RG
    grp_end = jnp.cumsum(grp)
    gi = jnp.arange(n_grp_max, dtype=jnp.int32)
    g_exp = jnp.minimum((gi[:, None] >= grp_end[None, :]).sum(axis=1), N_EXPERTS - 1).astype(jnp.int32)
    within = gi - (grp_end[g_exp] - grp[g_exp])
    g_start = (seg_start[g_exp] + within * RG).astype(jnp.int32)
    g_rows = jnp.clip(seg[g_exp] - within * RG, 0, RG)
    g_rows = jnp.where(gi < grp_end[-1], g_rows, 0)
    n_full = g_rows // SUB
    rest = g_rows - n_full * SUB
    g_tail = (rest > 0) & (rest <= SUB_TAIL) & (n_full >= 1)
    g_nsub = n_full + ((rest > 0) & ~g_tail)
    groups = (g_exp, g_start, g_nsub.astype(jnp.int32), g_tail.astype(jnp.int32),
              grp_end[-1:].astype(jnp.int32))
    tabs = tuple(a.reshape(-1).astype(jnp.int32) for a in (cnt, loff, dest))
    return tabs, lpos, gate, groups, n_rows_bound + SUB


def _combine_kernel(cnt_ref, loff_ref, dest_ref, src_hbm, lpos_ref, x1p_ref, x1s_ref, g_ref,
                    yp_ref, ys_ref, buf, sem):
    t = pl.program_id(0)
    last = pl.num_programs(0) - 1

    def copy(tile, off, src, size):
        return pltpu.make_async_copy(src_hbm.at[pl.ds(src, size)], buf.at[tile % 2, pl.ds(off, size)],
                                     sem.at[tile % 2])

    @pl.when(t == 0)
    def _():
        buf[...] = jnp.zeros_like(buf)
        _chunk_loop(cnt_ref, loff_ref, dest_ref, t, lambda o, g, s: copy(t, o, g, s).start())

    @pl.when(t < last)
    def _():
        _chunk_loop(cnt_ref, loff_ref, dest_ref, t + 1, lambda o, g, s: copy(t + 1, o, g, s).start())

    _chunk_loop(cnt_ref, loff_ref, dest_ref, t, lambda o, g, s: copy(t, o, g, s).wait())
    cols = lax.broadcasted_iota(jnp.int32, (TT, LROWS), 1)
    sel = jnp.zeros((TT, LROWS), F32)
    for k in range(TOP_K):
        sel = sel + jnp.where(cols == lpos_ref[:, k:k + 1], 1.0, 0.0)
    f = jnp.dot(sel.astype(BF16), buf[t % 2].astype(BF16), preferred_element_type=F32)

    @pl.when(t < last)
    def _():
        yp_ref[...] = _rms(x1p_ref[...] + f, g_ref[...])

    @pl.when(t == last)
    def _():
        ys_ref[...] = _rms(x1s_ref[...] + f, g_ref[...])


def _combine(tabs, out_sorted, lpos, x1p, x1s, g):
    d = x1p.shape[1]
    nt = lpos.shape[0] // TT
    last_p = x1p.shape[0] // TT - 1
    prompt = pl.BlockSpec((TT, d), lambda t, *_: (jnp.minimum(t, last_p), 0))
    sample = pl.BlockSpec((TT, d), lambda t, *_: (0, 0))
    grid_spec = pltpu.PrefetchScalarGridSpec(
        num_scalar_prefetch=3,
        grid=(nt,),
        in_specs=[pl.BlockSpec(memory_space=pl.ANY),
                  pl.BlockSpec((TT, TOP_K), lambda t, *_: (t, 0)),
                  prompt, sample,
                  pl.BlockSpec((1, d), lambda t, *_: (0, 0))],
        out_specs=[prompt, sample],
        scratch_shapes=[pltpu.VMEM((2, LROWS, d), F32), pltpu.SemaphoreType.DMA((2,))],
    )
    return pl.pallas_call(
        _combine_kernel,
        grid_spec=grid_spec,
        out_shape=[jax.ShapeDtypeStruct(x1p.shape, F32), jax.ShapeDtypeStruct((TT, d), F32)],
        compiler_params=_cparams(("arbitrary",)),
        name="moe_combine",
    )(*tabs, out_sorted, lpos, x1p, x1s, g.reshape(1, d))


def _t5_bucket(dist):
    max_exact = N_BUCKETS // 2
    dd = dist.astype(F32)
    large = max_exact + (jnp.log(jnp.maximum(dd, 1.0) / max_exact)
                         / math.log(BUCKET_MAX_DIST / max_exact) * (N_BUCKETS - max_exact)).astype(jnp.int32)
    large = jnp.minimum(large, N_BUCKETS - 1)
    return jnp.where(dist < max_exact, dist, large)


def _bias_tables(rel_bias):
    dist = jnp.asarray(np.arange(N_TAPS)[None, :] * np.array(DILATIONS)[:, None], jnp.int32)
    bias = jnp.transpose(rel_bias[_t5_bucket(dist)], (2, 0, 1)).astype(F32)
    by_branch = jnp.transpose(bias, (1, 0, 2))
    g, h = by_branch.shape[:2]
    row = jnp.concatenate([by_branch[:, :, ::-1], jnp.full((g, h, WIN), NEG, F32)], axis=-1)
    flat = jnp.broadcast_to(row[:, :, None, :], (g, h, WIN, 2 * WIN + 1)).reshape(g, h, -1)
    band = flat[:, :, :WIN * 2 * WIN].reshape(g, h, WIN, 2 * WIN)
    return band, by_branch[:, :, :0:-1], by_branch[:, :, 0:1]


def kernel(x_prompt, x_sample, cache_k_win, cache_v_win, state_conv, state_ssm, rel_bias, attn_norm, w_in, conv_w, conv_b, dt_bias, a_log, d_skip, ssd_norm, w_out, ffn_norm, w_router, b_router, w_gate_up, b_gate_up, w_down, b_down, final_norm):
    bp, tp, d = x_prompt.shape
    bs, ts, _ = x_sample.shape
    depth = w_in.shape[0]
    assert depth == 1 and ts == 1 and tp % (max(DILATIONS) * WIN) == 0
    keep = min(max(DILATIONS) * WIN, tp)
    band, samp, samp0 = _bias_tables(rel_bias)
    l = 0

    xp = x_prompt.reshape(bp * tp, d)
    xs = x_sample.reshape(bs * ts, d)
    w_in_bf = jnp.pad(w_in[l], ((0, 0), (0, IN_PROJ_PAD - IN_PROJ))).astype(BF16)
    w_out_bf = w_out[l].astype(BF16)
    wr = jnp.pad(w_router[l], ((0, 0), (0, LANES - N_EXPERTS))).astype(BF16)
    br = jnp.pad(b_router[l], (0, LANES - N_EXPERTS), constant_values=NEG).reshape(1, LANES)

    k, v, z, xbc, dt_raw, *branch_qkv = _inproj_prompt(xp, attn_norm[l], w_in_bf, bp)
    k3 = k.reshape(bp, tp, KV_WIDTH)
    v3 = v.reshape(bp, tp, KV_WIDTH)
    att_parts = [_attn_branch(branch_qkv[2 * gi], branch_qkv[2 * gi + 1], band[gi])
                 for gi in range(len(DILATIONS))]
    ssm, st_p = _ssd_prompt(xbc, dt_raw, z, conv_w[l], conv_b[l], dt_bias[l], a_log[l], d_skip[l],
                            ssd_norm[l], bp)
    x1p, xnp_, lgp = _outproj_prompt(att_parts, ssm, xp, w_out_bf, ffn_norm[l], wr, br)
    k_win_p = k3[:, tp - keep:].reshape(1, bp, keep, N_KV_HEADS, HEAD_DIM)
    v_win_p = v3[:, tp - keep:].reshape(1, bp, keep, N_KV_HEADS, HEAD_DIM)
    conv_p = xbc.reshape(bp, tp, CONV_DIM)[:, tp - (CONV_W - 1):][None]

    q_s, k_s, v_s, z_s, xbc_s, dt_s = _inproj(xs, attn_norm[l], w_in_bf, bs * ts)
    q_s = q_s.reshape(bs, N_ATT_HEADS, HEAD_DIM)
    k_s = k_s.reshape(bs, N_KV_HEADS, HEAD_DIM)
    v_s = v_s.reshape(bs, N_KV_HEADS, HEAD_DIM)
    wbuf = cache_k_win.shape[2]
    att_s = _attn_sample(q_s, k_s, v_s, cache_k_win[l].reshape(bs, wbuf, KV_WIDTH),
                         cache_v_win[l].reshape(bs, wbuf, KV_WIDTH), samp, samp0)
    xa_s, xdt_s, decay_s = _conv_sample(xbc_s, state_conv[l], conv_w[l], conv_b[l], dt_s, dt_bias[l],
                                        a_log[l])
    nbc = N_SSM_GROUPS * D_STATE
    h_s, y_s = _ssm_sample(xdt_s, decay_s, xa_s[:, SSM_WIDTH:SSM_WIDTH + nbc], xa_s[:, SSM_WIDTH + nbc:],
                           state_ssm[l])
    x1s, xns, lgs = _outproj_sample(att_s.reshape(bs, ATT_WIDTH), y_s, xa_s[:, :SSM_WIDTH], z_s,
                                    d_skip[l], ssd_norm[l], xs, w_out_bf, ffn_norm[l], wr, br)
    conv_s = jnp.concatenate([state_conv[l][:, 1:], xbc_s[:, None]], axis=1)[None]

    n_s = bs * ts
    assert (bp * tp) % TT == 0 and n_s <= TT
    m_pad = bp * tp + TT
    logits = jnp.concatenate([lgp[:, :N_EXPERTS], lgs[:, :N_EXPERTS]], axis=0)
    tabs, lpos, gate, groups, n_rows = _route(logits, m_pad)
    by_tile = lambda a: jnp.transpose(a.reshape(m_pad // TT, TT, TOP_K), (0, 2, 1))
    pad_rows = lambda a: jnp.pad(a, ((0, TT - n_s), (0, 0)))
    x_sorted = _dispatch(tabs, xnp_, pad_rows(xns), by_tile(lpos), by_tile(gate), n_rows)
    out_sorted = _moe_ffn(groups, x_sorted, w_gate_up[l], b_gate_up[l], w_down[l], b_down[l])
    y_p, y_s_out = _combine(tabs, out_sorted, lpos, x1p, pad_rows(x1s), final_norm)
    y_s_out = y_s_out[:n_s]

    return (y_p.reshape(bp, tp, d), y_s_out.reshape(bs, ts, d), k_win_p, v_win_p, conv_p, st_p[None],
            k_s.reshape(1, bs, ts, N_KV_HEADS, HEAD_DIM), v_s.reshape(1, bs, ts, N_KV_HEADS, HEAD_DIM),
            conv_s, h_s[None])
```

```python
import functools
import math

import jax
import jax.numpy as jnp
import numpy as np
from jax import lax
from jax.experimental import pallas as pl
from jax.experimental.pallas import tpu as pltpu

F32 = jnp.float32
BF16 = jnp.bfloat16
HIGHEST = lax.Precision.HIGHEST

LANES = 128
SUBLANES = 8
VMEM_LIMIT = 56 * 1024 * 1024

HEAD_DIM = 64
N_ATT_HEADS = 16
N_KV_HEADS = 4
KV_REP = N_ATT_HEADS // N_KV_HEADS
ATT_WIDTH = N_ATT_HEADS * HEAD_DIM
KV_WIDTH = N_KV_HEADS * HEAD_DIM
DILATIONS = (1, 4, 16)
N_TAPS = 129
WIN = N_TAPS - 1
ATT_SCALE = HEAD_DIM ** -0.5
N_BUCKETS = 32
BUCKET_MAX_DIST = 2048
SSM_HEAD_DIM = 64
N_SSM_HEADS = 16
SSM_WIDTH = N_SSM_HEADS * SSM_HEAD_DIM
N_SSM_GROUPS = 2
HEADS_PER_GROUP = N_SSM_HEADS // N_SSM_GROUPS
D_STATE = 128
CONV_W = 4
CONV_DIM = SSM_WIDTH + 2 * N_SSM_GROUPS * D_STATE
SSD_CHUNK = 128
N_EXPERTS = 32
TOP_K = 4
SWIGLU_LIMIT = 7.0
SWIGLU_ALPHA = 1.702
EPS = 1e-5
NEG = -1e30

Q0, K0, V0, Z0, X0, DT0 = 0, 1024, 1280, 1536, 2560, 4096
IN_PROJ = DT0 + N_SSM_HEADS
IN_PROJ_PAD = DT0 + LANES

TM_PROJ = 512
TM_OUT = 256
TM_MOE = 512
TF_MOE = 512
TM_FIN = 256


def _cparams(sem):
    return pltpu.CompilerParams(dimension_semantics=sem, vmem_limit_bytes=VMEM_LIMIT)


def _const_spec(shape):
    nd = len(shape)
    return pl.BlockSpec(shape, lambda *_: (0,) * nd, pipeline_mode=pl.Buffered(1))


def _spread(a, onehot_bf, terms):
    out, rest = None, a
    for _ in range(terms):
        piece = rest.astype(BF16)
        part = jnp.dot(piece, onehot_bf, preferred_element_type=F32)
        out = part if out is None else out + part
        rest = rest - piece.astype(F32)
    return out


def _rms(x, g):
    ms = jnp.mean(x * x, axis=-1, keepdims=True)
    return x * lax.rsqrt(ms + EPS) * g


def _silu(x):
    return x * (1.0 / (1.0 + jnp.exp(-x)))


def _inproj_kernel(x_ref, g_ref, w_ref, q_ref, k_ref, v_ref, z_ref, xbc_ref, dt_ref):
    xn = _rms(x_ref[...], g_ref[...]).astype(BF16)

    def mm(lo, hi):
        return jnp.dot(xn, w_ref[:, lo:hi], preferred_element_type=F32)

    q_ref[...] = (mm(Q0, K0) * ATT_SCALE).astype(BF16)
    k_ref[...] = mm(K0, V0)
    v_ref[...] = mm(V0, Z0)
    z_ref[...] = mm(Z0, X0)
    xbc_ref[...] = mm(X0, DT0)
    dt_ref[...] = mm(DT0, IN_PROJ_PAD)


def _inproj(x2d, g, w_bf, tm):
    m, d = x2d.shape
    widths = (ATT_WIDTH, KV_WIDTH, KV_WIDTH, SSM_WIDTH, CONV_DIM, LANES)
    dtypes = (BF16, F32, F32, F32, F32, F32)
    return pl.pallas_call(
        _inproj_kernel,
        grid=(m // tm,),
        in_specs=[pl.BlockSpec((tm, d), lambda i: (i, 0)),
                  _const_spec((1, d)),
                  _const_spec((d, IN_PROJ_PAD))],
        out_specs=[pl.BlockSpec((tm, w), lambda i: (i, 0)) for w in widths],
        out_shape=[jax.ShapeDtypeStruct((m, w), t) for w, t in zip(widths, dtypes)],
        compiler_params=_cparams(("parallel",)),
        name=f"inproj_{tm}",
    )(x2d, g.reshape(1, d), w_bf)


def _deinterleave_matrix(n, dil):
    p = np.zeros((n, n), np.float32)
    src = np.arange(n)
    p[(src % dil) * (n // dil) + src // dil, src] = 1.0
    return p


def _inproj_prompt_kernel(x_ref, g_ref, w_ref, p_mid_ref, p_far_ref, k_ref, v_ref, z_ref, xbc_ref, dt_ref,
                          q0_ref, kv0_ref, q1_ref, kv1_ref, q2_ref, kv2_ref):
    tm = x_ref.shape[0]
    xn = _rms(x_ref[...], g_ref[...]).astype(BF16)

    def mm(lo, hi):
        return jnp.dot(xn, w_ref[:, lo:hi], preferred_element_type=F32)

    q = (mm(Q0, K0) * ATT_SCALE).astype(BF16)
    k = mm(K0, V0)
    v = mm(V0, Z0)
    k_ref[...] = k
    v_ref[...] = v
    z_ref[...] = mm(Z0, X0)
    xbc_ref[...] = mm(X0, DT0)
    dt_ref[...] = mm(DT0, IN_PROJ_PAD)
    kv = jnp.concatenate([k, v], axis=-1).astype(BF16)
    q0_ref[0, 0] = q
    kv0_ref[0, 0] = kv
    for dil, p_ref, qd_ref, kvd_ref in ((DILATIONS[1], p_mid_ref, q1_ref, kv1_ref),
                                        (DILATIONS[2], p_far_ref, q2_ref, kv2_ref)):
        qp = jnp.dot(p_ref[...], q, preferred_element_type=F32).astype(BF16)
        kvp = jnp.dot(p_ref[...], kv, preferred_element_type=F32).astype(BF16)
        rows = tm // dil
        for r in range(dil):
            qd_ref[0, r] = qp[r * rows:(r + 1) * rows]
            kvd_ref[0, r] = kvp[r * rows:(r + 1) * rows]


def _inproj_prompt(x2d, g, w_bf, batch):
    m, d = x2d.shape
    tm = TM_PROJ
    seq = m // batch
    per_b = seq // tm
    assert DILATIONS[0] == 1 and seq % tm == 0 and tm % (max(DILATIONS) * 2 * SUBLANES) == 0
    widths = (KV_WIDTH, KV_WIDTH, SSM_WIDTH, CONV_DIM, LANES)
    rows = lambda w: pl.BlockSpec((tm, w), lambda i: (i, 0))
    split = lambda dil, w: pl.BlockSpec((1, dil, tm // dil, w), lambda i: (i // per_b, 0, i % per_b, 0))
    branch_specs, branch_shapes = [], []
    for dil in DILATIONS:
        for w in (ATT_WIDTH, 2 * KV_WIDTH):
            branch_specs.append(split(dil, w))
            branch_shapes.append(jax.ShapeDtypeStruct((batch, dil, seq // dil, w), BF16))
    perms = [jnp.asarray(_deinterleave_matrix(tm, dil), BF16) for dil in DILATIONS[1:]]
    return pl.pallas_call(
        _inproj_prompt_kernel,
        grid=(m // tm,),
        in_specs=[rows(d), _const_spec((1, d)), _const_spec((d, IN_PROJ_PAD)),
                  _const_spec((tm, tm)), _const_spec((tm, tm))],
        out_specs=[rows(w) for w in widths] + branch_specs,
        out_shape=[jax.ShapeDtypeStruct((m, w), F32) for w in widths] + branch_shapes,
        compiler_params=_cparams(("parallel",)),
        name="inproj_prompt",
    )(x2d, g.reshape(1, d), w_bf, *perms)


def _attn_kernel(q_ref, kvp_ref, kvc_ref, bias_ref, o_ref, lse_ref):
    first = pl.program_id(2) == 0
    lane = lax.broadcasted_iota(jnp.int32, (1, 2 * WIN), 1)
    prev_mask = jnp.where(jnp.logical_and(first, lane < WIN), NEG, 0.0)
    head_lane = lax.broadcasted_iota(jnp.int32, (WIN, LANES), 1)
    lse_tile = jnp.zeros((WIN, LANES), F32)
    for kvh in range(N_KV_HEADS):
        ks = slice(kvh * HEAD_DIM, (kvh + 1) * HEAD_DIM)
        vs = slice(KV_WIDTH + kvh * HEAD_DIM, KV_WIDTH + (kvh + 1) * HEAD_DIM)
        kw = jnp.concatenate([kvp_ref[0, 0, :, ks], kvc_ref[0, 0, :, ks]], axis=0)
        vw = jnp.concatenate([kvp_ref[0, 0, :, vs], kvc_ref[0, 0, :, vs]], axis=0)
        for pair in range(KV_REP // 2):
            outs = []
            for r in range(2):
                h = kvh * KV_REP + pair * 2 + r
                qh = q_ref[0, 0, :, h * HEAD_DIM:(h + 1) * HEAD_DIM]
                s = lax.dot_general(qh, kw, (((1,), (1,)), ((), ())), preferred_element_type=F32)
                s = s + bias_ref[h] + prev_mask
                mx = jnp.max(s, axis=-1, keepdims=True)
                p = jnp.exp(s - mx)
                l = jnp.sum(p, axis=-1, keepdims=True)
                o = jnp.dot(p.astype(BF16), vw, preferred_element_type=F32)
                outs.append(o * (1.0 / l))
                lse_tile = jnp.where(head_lane == h, mx + jnp.log(l), lse_tile)
            h0 = kvh * KV_REP + pair * 2
            o_ref[0, 0, :, h0 * HEAD_DIM:(h0 + 2) * HEAD_DIM] = jnp.concatenate(outs, axis=-1).astype(BF16)
    lse_ref[0, 0] = lse_tile


def _attn_branch(q, kv, bias_mat):
    b, dil, sub, _ = q.shape
    nb = sub // WIN
    cur = lambda bb, r, i: (bb, r, i, 0)
    prev = lambda bb, r, i: (bb, r, jnp.maximum(i - 1, 0), 0)
    return pl.pallas_call(
        _attn_kernel,
        grid=(b, dil, nb),
        in_specs=[pl.BlockSpec((1, 1, WIN, ATT_WIDTH), cur),
                  pl.BlockSpec((1, 1, WIN, 2 * KV_WIDTH), prev),
                  pl.BlockSpec((1, 1, WIN, 2 * KV_WIDTH), cur),
                  _const_spec((N_ATT_HEADS, WIN, 2 * WIN))],
        out_specs=[pl.BlockSpec((1, 1, WIN, ATT_WIDTH), cur),
                   pl.BlockSpec((1, 1, WIN, LANES), cur)],
        out_shape=[jax.ShapeDtypeStruct((b, dil, sub, ATT_WIDTH), BF16),
                   jax.ShapeDtypeStruct((b, dil, sub, LANES), F32)],
        compiler_params=_cparams(("parallel", "parallel", "arbitrary")),
        name=f"attn_dil{dil}",
    )(q, kv, kv, bias_mat)


def _softplus(x):
    return jnp.maximum(x, 0.0) + jnp.log(1.0 + jnp.exp(-jnp.abs(x)))


def _ssd_kernel(xbc_ref, dt_ref, z_ref, cw_ref, cb_ref, dtb_ref, alog_ref, dskip_ref, gn_ref, e_ref,
                y_ref, st_ref, ext_ref, state_ref):
    c = pl.program_id(1)
    L = SSD_CHUNK

    @pl.when(c == 0)
    def _():
        ext_ref[0:SUBLANES, :] = jnp.zeros((SUBLANES, CONV_DIM), F32)
        state_ref[...] = jnp.zeros_like(state_ref)

    ext_ref[SUBLANES:SUBLANES + L, :] = xbc_ref[...]
    acc = cb_ref[...] + ext_ref[SUBLANES:SUBLANES + L, :] * cw_ref[CONV_W - 1:CONV_W, :]
    for i in range(CONV_W - 1):
        off = SUBLANES - (CONV_W - 1) + i
        acc = acc + ext_ref[off:off + L, :] * cw_ref[i:i + 1, :]
    ext_ref[0:SUBLANES, :] = ext_ref[L:L + SUBLANES, :]
    xa = _silu(acc)

    dt = _softplus(dt_ref[...] + dtb_ref[...])
    da = dt * (-jnp.exp(alog_ref[...]))
    row = lax.broadcasted_iota(jnp.int32, (L, L), 0)
    col = lax.broadcasted_iota(jnp.int32, (L, L), 1)
    tri = row >= col
    a_cs = jnp.dot(tri.astype(F32), da, preferred_element_type=F32, precision=HIGHEST)
    a_cs_t = a_cs.T
    expand = e_ref[...]
    acs_full = _spread(a_cs, expand, 3)
    dt_full = _spread(dt, expand, 3)
    exp_acs = jnp.exp(acs_full)
    a_last = acs_full[L - 1:L, :]
    exp_last = exp_acs[L - 1:L, :]
    xs = xa[:, :SSM_WIDTH]
    xdt = xs * dt_full
    xw = xdt * jnp.exp(a_last - acs_full)

    for g in range(N_SSM_GROUPS):
        b0 = SSM_WIDTH + g * D_STATE
        c0 = SSM_WIDTH + N_SSM_GROUPS * D_STATE + g * D_STATE
        bg_t = xa[:, b0:b0 + D_STATE].T.astype(BF16)
        cg = xa[:, c0:c0 + D_STATE].astype(BF16)
        gram = jnp.dot(cg, bg_t, preferred_element_type=F32)
        for hh in range(HEADS_PER_GROUP):
            h = g * HEADS_PER_GROUP + hh
            hs = slice(h * SSM_HEAD_DIM, (h + 1) * SSM_HEAD_DIM)
            seg = jnp.where(tri, a_cs[:, h:h + 1] - a_cs_t[h:h + 1, :], NEG)
            scores = (gram * jnp.exp(seg)).astype(BF16)
            y_diag = jnp.dot(scores, xdt[:, hs].astype(BF16), preferred_element_type=F32)
            st = state_ref[h]
            y_off = jnp.dot(cg, st.astype(BF16), preferred_element_type=F32) * exp_acs[:, hs]
            y_ref[:, hs] = y_diag + y_off
            state_ref[h] = exp_last[:, hs] * st + jnp.dot(bg_t, xw[:, hs].astype(BF16),
                                                          preferred_element_type=F32)

    y = y_ref[...] + dskip_ref[...] * xs
    u = y * _silu(z_ref[...])
    gw = SSM_WIDTH // N_SSM_GROUPS
    parts = []
    for g in range(N_SSM_GROUPS):
        ug = u[:, g * gw:(g + 1) * gw]
        parts.append(ug * lax.rsqrt(jnp.mean(ug * ug, axis=-1, keepdims=True) + EPS))
    y_ref[...] = jnp.concatenate(parts, axis=-1) * gn_ref[...]

    @pl.when(c == pl.num_programs(1) - 1)
    def _():
        st_ref[0] = state_ref[...]


def _head_expand():
    e = np.zeros((LANES, SSM_WIDTH), np.float32)
    for h in range(N_SSM_HEADS):
        e[h, h * SSM_HEAD_DIM:(h + 1) * SSM_HEAD_DIM] = 1.0
    return jnp.asarray(e)


def _pad_lanes(v):
    return jnp.pad(v.astype(F32), (0, LANES - v.shape[0])).reshape(1, LANES)


def _ssd_prompt(xbc, dt_raw, z, conv_w, conv_b, dt_bias, a_log, d_skip, ssd_norm, batch):
    m = xbc.shape[0]
    nc = m // batch // SSD_CHUNK
    L = SSD_CHUNK
    rows = lambda b, c: (b * nc + c, 0)
    y, st = pl.pallas_call(
        _ssd_kernel,
        grid=(batch, nc),
        in_specs=[pl.BlockSpec((L, CONV_DIM), rows),
                  pl.BlockSpec((L, LANES), rows),
                  pl.BlockSpec((L, SSM_WIDTH), rows),
                  _const_spec((CONV_W, CONV_DIM)),
                  _const_spec((1, CONV_DIM)),
                  _const_spec((1, LANES)),
                  _const_spec((1, LANES)),
                  _const_spec((1, SSM_WIDTH)),
                  _const_spec((1, SSM_WIDTH)),
                  _const_spec((LANES, SSM_WIDTH))],
        out_specs=[pl.BlockSpec((L, SSM_WIDTH), rows),
                   pl.BlockSpec((1, N_SSM_HEADS, D_STATE, SSM_HEAD_DIM), lambda b, c: (b, 0, 0, 0))],
        out_shape=[jax.ShapeDtypeStruct((m, SSM_WIDTH), F32),
                   jax.ShapeDtypeStruct((batch, N_SSM_HEADS, D_STATE, SSM_HEAD_DIM), F32)],
        scratch_shapes=[pltpu.VMEM((SUBLANES + L, CONV_DIM), F32),
                        pltpu.VMEM((N_SSM_HEADS, D_STATE, SSM_HEAD_DIM), F32)],
        compiler_params=_cparams(("parallel", "arbitrary")),
        name="ssd_prompt",
    )(xbc, dt_raw, z, conv_w, conv_b.reshape(1, CONV_DIM), _pad_lanes(dt_bias), _pad_lanes(a_log),
      jnp.repeat(d_skip, SSM_HEAD_DIM).reshape(1, SSM_WIDTH), ssd_norm.reshape(1, SSM_WIDTH),
      _head_expand().astype(BF16))
    return y, jnp.swapaxes(st, -1, -2)


def _router(xn_bf, wr_ref, br_ref):
    return jnp.dot(xn_bf, wr_ref[...], preferred_element_type=F32) + br_ref[...]


def _outproj_kernel(o0, l0, o1, l1, o2, l2, u1_ref, u2_ref, e_ref, ssm_ref, x_ref, w_ref, g_ref, wr_ref,
                    br_ref, x1_ref, xn_ref, lg_ref):
    def natural(o_ref, l_ref, u_ref):
        dil = o_ref.shape[1]
        if dil == 1:
            return o_ref[0, 0].astype(F32), l_ref[0, 0]
        o_cat = jnp.concatenate([o_ref[0, r] for r in range(dil)], axis=0)
        l_cat = jnp.concatenate([l_ref[0, r] for r in range(dil)], axis=0)
        hi = l_cat.astype(BF16)
        lo = (l_cat - hi.astype(F32)).astype(BF16)
        u = u_ref[...]
        return (jnp.dot(u, o_cat, preferred_element_type=F32),
                jnp.dot(u, hi, preferred_element_type=F32) + jnp.dot(u, lo, preferred_element_type=F32))

    branches = [natural(o0, l0, None), natural(o1, l1, u1_ref), natural(o2, l2, u2_ref)]
    lses = [_spread(l, e_ref[...], 2) for _, l in branches]
    mx = functools.reduce(jnp.maximum, lses)
    ws = [jnp.exp(l - mx) for l in lses]
    num = functools.reduce(jnp.add, [w * o for w, (o, _) in zip(ws, branches)])
    att = num * (1.0 / functools.reduce(jnp.add, ws))
    y = jnp.dot(att.astype(BF16), w_ref[:ATT_WIDTH, :], preferred_element_type=F32)
    y = y + jnp.dot(ssm_ref[...].astype(BF16), w_ref[ATT_WIDTH:, :], preferred_element_type=F32)
    x1 = x_ref[...] + y
    x1_ref[...] = x1
    xn = _rms(x1, g_ref[...]).astype(BF16)
    xn_ref[...] = xn
    lg_ref[...] = _router(xn, wr_ref, br_ref)


def _outproj_prompt(att_parts, ssm, x2d, w_bf, g, wr, br):
    m, d = x2d.shape
    mix = w_bf.shape[0]
    tm = TM_OUT
    batch = att_parts[0][0].shape[0]
    per_b = m // batch // tm
    assert tm % (max(DILATIONS) * 2 * SUBLANES) == 0
    row = lambda w: pl.BlockSpec((tm, w), lambda i: (i, 0))
    split = lambda dil, w: pl.BlockSpec((1, dil, tm // dil, w), lambda i: (i // per_b, 0, i % per_b, 0))
    branch_specs = [split(o.shape[1], w) for o, _ in att_parts for w in (ATT_WIDTH, LANES)]
    interleave = [jnp.asarray(_deinterleave_matrix(tm, dil).T, BF16) for dil in DILATIONS[1:]]
    return pl.pallas_call(
        _outproj_kernel,
        grid=(m // tm,),
        in_specs=branch_specs + [_const_spec((tm, tm)), _const_spec((tm, tm)),
                                 _const_spec((LANES, ATT_WIDTH)), row(SSM_WIDTH), row(d),
                                 _const_spec((mix, d)), _const_spec((1, d)),
                                 _const_spec((d, LANES)), _const_spec((1, LANES))],
        out_specs=[row(d), row(d), row(LANES)],
        out_shape=[jax.ShapeDtypeStruct((m, d), F32), jax.ShapeDtypeStruct((m, d), BF16),
                   jax.ShapeDtypeStruct((m, LANES), F32)],
        compiler_params=_cparams(("parallel",)),
        name="outproj_prompt",
    )(*[a for part in att_parts for a in part], *interleave, _head_expand().astype(BF16), ssm, x2d, w_bf,
      g.reshape(1, d), wr, br)


def _attn_sample_kernel(q_ref, kn_ref, vn_ref, kc_ref, vc_ref, bias_ref, bias0_ref, o_ref):
    w = kc_ref.shape[1]
    q = q_ref[0]
    head_grp = lax.broadcasted_iota(jnp.int32, (N_ATT_HEADS, 1), 0) // KV_REP
    kn = kn_ref[0].astype(BF16).astype(F32)
    vn = vn_ref[0].astype(BF16).astype(F32)
    s_self = jnp.sum(q.astype(F32) * kn, axis=-1, keepdims=True)

    def taps(c_ref, dil):
        span = WIN * dil
        rows = c_ref[0, w - span:w, :].astype(BF16)
        if dil == 1:
            return rows
        tap = lax.broadcasted_iota(jnp.int32, (WIN, span), 0)
        col = lax.broadcasted_iota(jnp.int32, (WIN, span), 1)
        pick = jnp.where(col == tap * dil, 1.0, 0.0).astype(BF16)
        return jnp.dot(pick, rows, preferred_element_type=F32).astype(BF16)

    scores, selfs, lses = [], [], []
    for g, dil in enumerate(DILATIONS):
        kk = taps(kc_ref, dil)
        s = jnp.zeros((N_ATT_HEADS, WIN), F32)
        for kvh in range(N_KV_HEADS):
            cs = slice(kvh * HEAD_DIM, (kvh + 1) * HEAD_DIM)
            sk = lax.dot_general(q, kk[:, cs], (((1,), (1,)), ((), ())), preferred_element_type=F32)
            s = jnp.where(head_grp == kvh, sk, s)
        s = s + bias_ref[g]
        s0 = s_self + bias0_ref[g]
        mx = jnp.maximum(jnp.max(s, axis=-1, keepdims=True), s0)
        lse = mx + jnp.log(jnp.sum(jnp.exp(s - mx), axis=-1, keepdims=True) + jnp.exp(s0 - mx))
        scores.append(s); selfs.append(s0); lses.append(lse)
    top = functools.reduce(jnp.maximum, lses)
    es = [jnp.exp(l - top) for l in lses]
    tot = functools.reduce(jnp.add, es)
    o = jnp.zeros((N_ATT_HEADS, HEAD_DIM), F32)
    for s, s0, lse, e, dil in zip(scores, selfs, lses, es, DILATIONS):
        wgt = e / tot
        p = (jnp.exp(s - lse) * wgt).astype(BF16)
        p0 = (jnp.exp(s0 - lse) * wgt).astype(BF16).astype(F32)
        vv = taps(vc_ref, dil)
        o = o + p0 * vn
        for kvh in range(N_KV_HEADS):
            cs = slice(kvh * HEAD_DIM, (kvh + 1) * HEAD_DIM)
            ok = jnp.dot(p, vv[:, cs], preferred_element_type=F32)
            o = o + jnp.where(head_grp == kvh, ok, 0.0)
    o_ref[0] = o


def _attn_sample(q, k_new, v_new, k_cache, v_cache, bias_s, bias0_s):
    n, w = k_cache.shape[0], k_cache.shape[1]
    assert w % (max(DILATIONS) * WIN) == 0
    tok = lambda b: (b, 0, 0)
    head = pl.BlockSpec((1, N_ATT_HEADS, HEAD_DIM), tok)
    window = pl.BlockSpec((1, w, KV_WIDTH), tok)
    return pl.pallas_call(
        _attn_sample_kernel,
        grid=(n,),
        in_specs=[head, head, head, window, window,
                  _const_spec((len(DILATIONS), N_ATT_HEADS, WIN)),
                  _const_spec((len(DILATIONS), N_ATT_HEADS, 1))],
        out_specs=head,
        out_shape=jax.ShapeDtypeStruct((n, N_ATT_HEADS, HEAD_DIM), F32),
        compiler_params=_cparams(("parallel",)),
        name="attn_sample",
    )(q, jnp.repeat(k_new, KV_REP, axis=1), jnp.repeat(v_new, KV_REP, axis=1), k_cache, v_cache,
      bias_s, bias0_s)


def _conv_sample_kernel(xbc_ref, b0_ref, b1_ref, b2_ref, cw_ref, cb_ref, dt_ref, dtb_ref, alog_ref,
                        e_ref, xa_ref, xdt_ref, decay_ref):
    acc = cb_ref[...] + xbc_ref[...] * cw_ref[CONV_W - 1:CONV_W, :]
    for i, buf in enumerate((b0_ref, b1_ref, b2_ref)):
        acc = acc + buf[...] * cw_ref[i:i + 1, :]
    xa = _silu(acc)
    xa_ref[...] = xa
    dt = _softplus(dt_ref[...] + dtb_ref[...])
    decay_ref[...] = jnp.exp(dt * (-jnp.exp(alog_ref[...])))
    dt_full = jnp.dot(dt, e_ref[...], preferred_element_type=F32, precision=HIGHEST)
    xdt_ref[...] = xa[:, :SSM_WIDTH] * dt_full


def _conv_sample(xbc, conv_buf, conv_w, conv_b, dt_raw, dt_bias, a_log):
    n = xbc.shape[0]
    args = (xbc, conv_buf[:, 0], conv_buf[:, 1], conv_buf[:, 2], conv_w, conv_b.reshape(1, CONV_DIM),
            dt_raw, _pad_lanes(dt_bias), _pad_lanes(a_log),
            _head_expand())
    return pl.pallas_call(
        _conv_sample_kernel,
        grid=(1,),
        in_specs=[_const_spec(a.shape) for a in args],
        out_specs=[_const_spec((n, CONV_DIM)), _const_spec((n, SSM_WIDTH)), _const_spec((n, LANES))],
        out_shape=[jax.ShapeDtypeStruct((n, CONV_DIM), F32), jax.ShapeDtypeStruct((n, SSM_WIDTH), F32),
                   jax.ShapeDtypeStruct((n, LANES), F32)],
        compiler_params=_cparams(("arbitrary",)),
        name="conv_sample",
    )(*args)


def _ssm_sample_kernel(xdt_ref, decay_ref, b_ref, c_ref, h0_ref, hn_ref, y_ref):
    for g in range(N_SSM_GROUPS):
        hs = slice(g * HEADS_PER_GROUP, (g + 1) * HEADS_PER_GROUP)
        hn = decay_ref[0, hs] * h0_ref[0, hs] + xdt_ref[0, hs] * b_ref[0, g]
        hn_ref[0, hs] = hn
        c_row = c_ref[0, g].astype(BF16).astype(F32)
        y_ref[0, hs] = jnp.sum(hn.astype(BF16).astype(F32) * c_row, axis=-1, keepdims=True)


def _ssm_sample(xdt, decay, bmat, cmat, h0):
    n = xdt.shape[0]
    p = SSM_HEAD_DIM
    tok4 = lambda b: (b, 0, 0, 0)
    hn, y = pl.pallas_call(
        _ssm_sample_kernel,
        grid=(n,),
        in_specs=[pl.BlockSpec((1, N_SSM_HEADS, p, 1), tok4),
                  pl.BlockSpec((1, N_SSM_HEADS, 1, 1), tok4),
                  pl.BlockSpec((1, N_SSM_GROUPS, 1, D_STATE), tok4),
                  pl.BlockSpec((1, N_SSM_GROUPS, 1, D_STATE), tok4),
                  pl.BlockSpec((1, N_SSM_HEADS, p, D_STATE), tok4)],
        out_specs=[pl.BlockSpec((1, N_SSM_HEADS, p, D_STATE), tok4),
                   pl.BlockSpec((1, N_SSM_HEADS, p, 1), tok4)],
        out_shape=[jax.ShapeDtypeStruct((n, N_SSM_HEADS, p, D_STATE), F32),
                   jax.ShapeDtypeStruct((n, N_SSM_HEADS, p, 1), F32)],
        compiler_params=_cparams(("parallel",)),
        name="ssm_sample",
    )(xdt.reshape(n, N_SSM_HEADS, p, 1), decay[:, :N_SSM_HEADS].reshape(n, N_SSM_HEADS, 1, 1),
      bmat.reshape(n, N_SSM_GROUPS, 1, D_STATE), cmat.reshape(n, N_SSM_GROUPS, 1, D_STATE), h0)
    return hn, y.reshape(n, SSM_WIDTH)


def _outproj_sample_kernel(att_ref, y_ref, xs_ref, z_ref, dskip_ref, gn_ref, x_ref, w_ref, g_ref,
                           wr_ref, br_ref, x1_ref, xn_ref, lg_ref):
    y = y_ref[...] + dskip_ref[...] * xs_ref[...]
    u = y * _silu(z_ref[...])
    gw = SSM_WIDTH // N_SSM_GROUPS
    parts = []
    for g in range(N_SSM_GROUPS):
        ug = u[:, g * gw:(g + 1) * gw]
        parts.append(ug * lax.rsqrt(jnp.mean(ug * ug, axis=-1, keepdims=True) + EPS)
                     * gn_ref[:, g * gw:(g + 1) * gw])
    mix = jnp.concatenate([att_ref[...]] + parts, axis=-1).astype(BF16)
    x1 = x_ref[...] + jnp.dot(mix, w_ref[...], preferred_element_type=F32)
    x1_ref[...] = x1
    xn = _rms(x1, g_ref[...]).astype(BF16)
    xn_ref[...] = xn
    lg_ref[...] = _router(xn, wr_ref, br_ref)


def _outproj_sample(att, y, xs, z, d_skip, ssd_norm, x2d, w_bf, g, wr, br):
    n, d = x2d.shape
    args = (att, y, xs, z, jnp.repeat(d_skip, SSM_HEAD_DIM).reshape(1, SSM_WIDTH),
            ssd_norm.reshape(1, SSM_WIDTH), x2d, w_bf, g.reshape(1, d), wr, br)
    return pl.pallas_call(
        _outproj_sample_kernel,
        grid=(1,),
        in_specs=[_const_spec(a.shape) for a in args],
        out_specs=[_const_spec((n, d)), _const_spec((n, d)), _const_spec((n, LANES))],
        out_shape=[jax.ShapeDtypeStruct((n, d), F32), jax.ShapeDtypeStruct((n, d), BF16),
                   jax.ShapeDtypeStruct((n, LANES), F32)],
        compiler_params=_cparams(("arbitrary",)),
        name="outproj_sample",
    )(*args)


TT = 256
CHUNK_ALIGN = SUBLANES
LROWS = -(-(TT * TOP_K + N_EXPERTS * (CHUNK_ALIGN - 1)) // TT) * TT
CHUNK_SIZES = tuple(1 << b for b in range(TT.bit_length() - 1, CHUNK_ALIGN.bit_length() - 2, -1))
SUB = 256
SUB_TAIL = SUB // 2
RG = 5 * SUB
TF = 256
VMEM_LIMIT_FFN = 60000 * 1024


def _chunk_loop(cnt_ref, loff_ref, dest_ref, tile, fn):
    def per_expert(e, carry):
        idx = tile * N_EXPERTS + e
        n, off, dst = cnt_ref[idx], loff_ref[idx], dest_ref[idx]
        for size in CHUNK_SIZES:
            take = (n & size) != 0

            @pl.when(take)
            def _(off=off, dst=dst, size=size):
                fn(pl.multiple_of(off, CHUNK_ALIGN), pl.multiple_of(dst, CHUNK_ALIGN), size)

            step = jnp.where(take, size, 0)
            off, dst = off + step, dst + step
        return carry

    lax.fori_loop(0, N_EXPERTS, per_expert, 0)


def _dispatch_kernel(cnt_ref, loff_ref, dest_ref, xp_ref, xs_ref, lpos_ref, gate_ref, out_hbm, buf, sem):
    t = pl.program_id(0)
    last = pl.num_programs(0) - 1
    d = xp_ref.shape[1]
    x = jnp.where(t == last, xs_ref[...], xp_ref[...])
    rows = lax.broadcasted_iota(jnp.int32, (LROWS, TT), 0)
    onehot = jnp.zeros((LROWS, TT), F32)
    wcol = jnp.zeros((LROWS, 1), F32)
    for k in range(TOP_K):
        hit = jnp.where(rows == lpos_ref[0, k:k + 1, :], 1.0, 0.0)
        onehot = onehot + hit
        wcol = wcol + jnp.sum(hit * gate_ref[0, k:k + 1, :], axis=-1, keepdims=True)
    tile_buf = buf.at[t % 2]
    tile_buf[:, 0:d] = jnp.dot(onehot.astype(BF16), x, preferred_element_type=F32)
    tile_buf[:, d:d + LANES] = jnp.broadcast_to(wcol, (LROWS, LANES))

    def copy(tile, off, dst, size):
        return pltpu.make_async_copy(buf.at[tile % 2, pl.ds(off, size)], out_hbm.at[pl.ds(dst, size)],
                                     sem.at[tile % 2])

    _chunk_loop(cnt_ref, loff_ref, dest_ref, t, lambda o, g, s: copy(t, o, g, s).start())

    @pl.when(t > 0)
    def _():
        _chunk_loop(cnt_ref, loff_ref, dest_ref, t - 1, lambda o, g, s: copy(t - 1, o, g, s).wait())

    @pl.when(t == last)
    def _():
        _chunk_loop(cnt_ref, loff_ref, dest_ref, t, lambda o, g, s: copy(t, o, g, s).wait())


def _dispatch(tabs, xn_p, xn_s, lpos_t, gate_t, n_rows):
    nt = lpos_t.shape[0]
    d = xn_p.shape[1]
    last_p = xn_p.shape[0] // TT - 1
    grid_spec = pltpu.PrefetchScalarGridSpec(
        num_scalar_prefetch=3,
        grid=(nt,),
        in_specs=[pl.BlockSpec((TT, d), lambda t, *_: (jnp.minimum(t, last_p), 0)),
                  pl.BlockSpec((TT, d), lambda t, *_: (0, 0)),
                  pl.BlockSpec((1, TOP_K, TT), lambda t, *_: (t, 0, 0)),
                  pl.BlockSpec((1, TOP_K, TT), lambda t, *_: (t, 0, 0))],
        out_specs=pl.BlockSpec(memory_space=pl.ANY),
        scratch_shapes=[pltpu.VMEM((2, LROWS, d + LANES), F32), pltpu.SemaphoreType.DMA((2,))],
    )
    return pl.pallas_call(
        _dispatch_kernel,
        grid_spec=grid_spec,
        out_shape=jax.ShapeDtypeStruct((n_rows, d + LANES), F32),
        compiler_params=_cparams(("arbitrary",)),
        name="moe_dispatch",
    )(*tabs, xn_p, xn_s, lpos_t, gate_t)


def _ffn_kernel(ge_ref, gs_ref, gn_ref, gt_ref, ng_ref, xs_hbm, wgu_hbm, wd_hbm, bgu_ref, bdn_ref, out_hbm,
                xbuf, acc, ostage, wg_st, wu_st, wd_st, wg_bf, wu_bf, wd_bf, sem_w, sem_x, sem_o):
    d = acc.shape[1]
    d_ff = wd_hbm.shape[1]
    nf = d_ff // TF
    n_groups = ng_ref[0]
    total = n_groups * nf

    def w_copies(s, slot):
        g = s // nf
        f = s - g * nf
        e = ge_ref[g]
        c0 = pl.multiple_of(f * TF, TF)
        return (pltpu.make_async_copy(wgu_hbm.at[e, :, pl.ds(c0, TF)], wg_st.at[slot], sem_w.at[slot, 0]),
                pltpu.make_async_copy(wgu_hbm.at[e, :, pl.ds(d_ff + c0, TF)], wu_st.at[slot], sem_w.at[slot, 1]),
                pltpu.make_async_copy(wd_hbm.at[e, pl.ds(c0, TF), :], wd_st.at[slot], sem_w.at[slot, 2]))

    def x_copy(g, j, size=SUB):
        r0 = pl.multiple_of(j * SUB, SUB)
        return pltpu.make_async_copy(xs_hbm.at[pl.ds(pl.multiple_of(gs_ref[g] + r0, CHUNK_ALIGN), size)],
                                     xbuf.at[g % 2, pl.ds(r0, size)], sem_x)

    def o_copy(g, j, size=SUB):
        r0 = pl.multiple_of(j * SUB, SUB)
        return pltpu.make_async_copy(ostage.at[j % 2, pl.ds(0, size)],
                                     out_hbm.at[pl.ds(pl.multiple_of(gs_ref[g] + r0, CHUNK_ALIGN), size)],
                                     sem_o.at[j % 2])

    def loop(n, fn):
        lax.fori_loop(0, n, lambda j, c: (fn(j), c)[1], 0)

    def group_rows(g, op):
        loop(gn_ref[g], lambda j: op(x_copy(g, j)))

        @pl.when(gt_ref[g] == 1)
        def _():
            op(x_copy(g, gn_ref[g], SUB_TAIL))

    def drain_stores(g):
        n = gn_ref[g]
        tail = gt_ref[g]

        @pl.when(tail == 1)
        def _():
            o_copy(g, n, SUB_TAIL).wait()

        @pl.when(n >= 1)
        def _():
            o_copy(g, n - 1).wait()

        @pl.when(jnp.logical_and(n >= 2, tail == 0))
        def _():
            o_copy(g, n - 2).wait()

    @pl.when(total > 0)
    def _():
        for c in w_copies(0, 0):
            c.start()
        group_rows(0, lambda c: c.start())

    def item(s, carry):
        slot = s % 2
        g = s // nf
        f = s - g * nf
        e = ge_ref[g]
        nsub = gn_ref[g]
        xg = xbuf.at[g % 2]

        @pl.when(s + 1 < total)
        def _():
            for c in w_copies(s + 1, 1 - slot):
                c.start()

        @pl.when(f == 0)
        def _():
            group_rows(g, lambda c: c.wait())

        @pl.when(jnp.logical_and(f == 1, g + 1 < n_groups))
        def _():
            group_rows(g + 1, lambda c: c.start())

        for c in w_copies(s, slot):
            c.wait()
        bg = bgu_ref[pl.ds(e * 2 * nf + f, 1), :]
        bu = bgu_ref[pl.ds(e * 2 * nf + nf + f, 1), :]

        def sub_block(j, phase, cast=False, size=SUB):
            rs = pl.ds(pl.multiple_of(j * SUB, SUB), size)
            x = xg[rs, 0:d].astype(BF16)
            if cast:
                wg, wu, wd = (st[slot].astype(BF16) for st in (wg_st, wu_st, wd_st))
                wg_bf[...], wu_bf[...], wd_bf[...] = wg, wu, wd
            else:
                wg, wu, wd = wg_bf[...], wu_bf[...], wd_bf[...]
            hg = jnp.dot(x, wg, preferred_element_type=F32) + bg
            hu = jnp.dot(x, wu, preferred_element_type=F32) + bu
            gg = jnp.minimum(hg, SWIGLU_LIMIT)
            uu = jnp.clip(hu, -SWIGLU_LIMIT, SWIGLU_LIMIT)
            act = gg * (1.0 / (1.0 + jnp.exp(-SWIGLU_ALPHA * gg))) * (uu + 1.0)
            part = jnp.dot(act.astype(BF16), wd, preferred_element_type=F32)
            if phase == "first":
                acc[rs, :] = part + bdn_ref[pl.ds(e, 1), :]
            elif phase == "middle":
                acc[rs, :] += part
            else:
                @pl.when(j >= 2)
                def _():
                    o_copy(g, j - 2).wait()

                ostage[j % 2, 0:size] = (acc[rs, :] + part) * xg[rs, d:d + 1]
                o_copy(g, j, size).start()

        def all_sub_blocks(phase):
            sub_block(jnp.int32(0), phase, cast=True)

            def pair(p):
                if phase == "last":
                    sub_block(2 * p + 1, phase)
                    sub_block(2 * p + 2, phase)
                else:
                    sub_block(2 * p + 1, phase, size=2 * SUB)

            loop((nsub - 1) // 2, pair)
            odd_one = nsub % 2 == 0
            tail = gt_ref[g] == 1

            @pl.when(jnp.logical_and(odd_one, jnp.logical_not(tail)))
            def _():
                sub_block(nsub - 1, phase)

            @pl.when(jnp.logical_and(jnp.logical_not(odd_one), tail))
            def _():
                sub_block(nsub, phase, size=SUB_TAIL)

            @pl.when(jnp.logical_and(odd_one, tail))
            def _():
                if phase == "last":
                    sub_block(nsub - 1, phase)
                    sub_block(nsub, phase, size=SUB_TAIL)
                else:
                    sub_block(nsub - 1, phase, size=SUB + SUB_TAIL)

        @pl.when(f == 0)
        def _():
            all_sub_blocks("first")

        @pl.when(jnp.logical_and(f > 0, f < nf - 1))
        def _():
            all_sub_blocks("middle")

        @pl.when(f == nf - 1)
        def _():
            @pl.when(g > 0)
            def _():
                drain_stores(g - 1)

            all_sub_blocks("last")

        return carry

    lax.fori_loop(0, total, item, 0)

    @pl.when(total > 0)
    def _():
        drain_stores(n_groups - 1)


def _moe_ffn(groups, x_sorted, w_gate_up, b_gu, w_down, b_dn):
    n_rows = x_sorted.shape[0]
    _, d_ff, d = w_down.shape
    nf = d_ff // TF
    assert nf >= 2
    bgu2 = b_gu.reshape(N_EXPERTS * 2 * nf, TF)
    grid_spec = pltpu.PrefetchScalarGridSpec(
        num_scalar_prefetch=5,
        grid=(1,),
        in_specs=[pl.BlockSpec(memory_space=pl.ANY), pl.BlockSpec(memory_space=pl.ANY),
                  pl.BlockSpec(memory_space=pl.ANY),
                  pl.BlockSpec(bgu2.shape, lambda i, *_: (0, 0), pipeline_mode=pl.Buffered(1)),
                  pl.BlockSpec(b_dn.shape, lambda i, *_: (0, 0), pipeline_mode=pl.Buffered(1))],
        out_specs=pl.BlockSpec(memory_space=pl.ANY),
        scratch_shapes=[pltpu.VMEM((2, RG, d + LANES), F32), pltpu.VMEM((RG, d), F32),
                        pltpu.VMEM((2, SUB, d), F32),
                        pltpu.VMEM((2, d, TF), F32), pltpu.VMEM((2, d, TF), F32), pltpu.VMEM((2, TF, d), F32),
                        pltpu.VMEM((d, TF), BF16), pltpu.VMEM((d, TF), BF16), pltpu.VMEM((TF, d), BF16),
                        pltpu.SemaphoreType.DMA((2, 3)), pltpu.SemaphoreType.DMA(()),
                        pltpu.SemaphoreType.DMA((2,))],
    )
    return pl.pallas_call(
        _ffn_kernel,
        grid_spec=grid_spec,
        out_shape=jax.ShapeDtypeStruct((n_rows, d), F32),
        compiler_params=pltpu.CompilerParams(dimension_semantics=("arbitrary",),
                                             vmem_limit_bytes=VMEM_LIMIT_FFN),
        name="moe_ffn",
    )(*groups, x_sorted, w_gate_up, w_down, bgu2, b_dn)


def _route(logits, m_pad):
    m = logits.shape[0]
    nt = m_pad // TT
    top_v, top_i = lax.top_k(logits, TOP_K)
    gate = jnp.pad(jax.nn.softmax(top_v, axis=-1), ((0, m_pad - m), (0, 0)))
    top_i = jnp.pad(top_i.astype(jnp.int32), ((0, m_pad - m), (0, 0)), constant_values=-1)
    chosen = (top_i[:, :, None] == jnp.arange(N_EXPERTS, dtype=jnp.int32)).astype(jnp.int32)
    tiles = chosen.sum(axis=1).reshape(nt, TT, N_EXPERTS)
    cnt = (tiles.sum(axis=1) + CHUNK_ALIGN - 1) // CHUNK_ALIGN * CHUNK_ALIGN
    loff = jnp.cumsum(cnt, axis=1) - cnt
    seg = cnt.sum(axis=0)
    seg_start = jnp.cumsum(seg) - seg
    dest = seg_start[None, :] + jnp.cumsum(cnt, axis=0) - cnt
    rank = jnp.cumsum(tiles, axis=1) - tiles
    lpos_all = (loff[:, None, :] + rank).reshape(m_pad, N_EXPERTS)
    lpos = (lpos_all[:, None, :] * chosen).sum(axis=-1)
    lpos = jnp.where(top_i >= 0, lpos, -1).astype(jnp.int32)
    n_rows_bound = nt * LROWS
    n_grp_max = n_rows_bound // RG + N_EXPERTS
    grp = (seg + RG - 1) // RG
    grp_end = jnp.cumsum(grp)
    gi = jnp.arange(n_grp_max, dtype=jnp.int32)
    g_exp = jnp.minimum((gi[:, None] >= grp_end[None, :]).sum(axis=1), N_EXPERTS - 1).astype(jnp.int32)
    within = gi - (grp_end[g_exp] - grp[g_exp])
    g_start = (seg_start[g_exp] + within * RG).astype(jnp.int32)
    g_rows = jnp.clip(seg[g_exp] - within * RG, 0, RG)
    g_rows = jnp.where(gi < grp_end[-1], g_rows, 0)
    n_full = g_rows // SUB
    rest = g_rows - n_full * SUB
    g_tail = (rest > 0) & (rest <= SUB_TAIL) & (n_full >= 1)
    g_nsub = n_full + ((rest > 0) & ~g_tail)
    groups = (g_exp, g_start, g_nsub.astype(jnp.int32), g_tail.astype(jnp.int32),
              grp_end[-1:].astype(jnp.int32))
    tabs = tuple(a.reshape(-1).astype(jnp.int32) for a in (cnt, loff, dest))
    return tabs, lpos, gate, groups, n_rows_bound + SUB


def _combine_kernel(cnt_ref, loff_ref, dest_ref, src_hbm, lpos_ref, x1p_ref, x1s_ref, g_ref,
                    yp_ref, ys_ref, buf, sem):
    t = pl.program_id(0)
    last = pl.num_programs(0) - 1

    def copy(tile, off, src, size):
        return pltpu.make_async_copy(src_hbm.at[pl.ds(src, size)], buf.at[tile % 2, pl.ds(off, size)],
                                     sem.at[tile % 2])

    @pl.when(t == 0)
    def _():
        buf[...] = jnp.zeros_like(buf)
        _chunk_loop(cnt_ref, loff_ref, dest_ref, t, lambda o, g, s: copy(t, o, g, s).start())

    @pl.when(t < last)
    def _():
        _chunk_loop(cnt_ref, loff_ref, dest_ref, t + 1, lambda o, g, s: copy(t + 1, o, g, s).start())

    _chunk_loop(cnt_ref, loff_ref, dest_ref, t, lambda o, g, s: copy(t, o, g, s).wait())
    cols = lax.broadcasted_iota(jnp.int32, (TT, LROWS), 1)
    sel = jnp.zeros((TT, LROWS), F32)
    for k in range(TOP_K):
        sel = sel + jnp.where(cols == lpos_ref[:, k:k + 1], 1.0, 0.0)
    f = jnp.dot(sel.astype(BF16), buf[t % 2].astype(BF16), preferred_element_type=F32)

    @pl.when(t < last)
    def _():
        yp_ref[...] = _rms(x1p_ref[...] + f, g_ref[...])

    @pl.when(t == last)
    def _():
        ys_ref[...] = _rms(x1s_ref[...] + f, g_ref[...])


def _combine(tabs, out_sorted, lpos, x1p, x1s, g):
    d = x1p.shape[1]
    nt = lpos.shape[0] // TT
    last_p = x1p.shape[0] // TT - 1
    prompt = pl.BlockSpec((TT, d), lambda t, *_: (jnp.minimum(t, last_p), 0))
    sample = pl.BlockSpec((TT, d), lambda t, *_: (0, 0))
    grid_spec = pltpu.PrefetchScalarGridSpec(
        num_scalar_prefetch=3,
        grid=(nt,),
        in_specs=[pl.BlockSpec(memory_space=pl.ANY),
                  pl.BlockSpec((TT, TOP_K), lambda t, *_: (t, 0)),
                  prompt, sample,
                  pl.BlockSpec((1, d), lambda t, *_: (0, 0))],
        out_specs=[prompt, sample],
        scratch_shapes=[pltpu.VMEM((2, LROWS, d), F32), pltpu.SemaphoreType.DMA((2,))],
    )
    return pl.pallas_call(
        _combine_kernel,
        grid_spec=grid_spec,
        out_shape=[jax.ShapeDtypeStruct(x1p.shape, F32), jax.ShapeDtypeStruct((TT, d), F32)],
        compiler_params=_cparams(("arbitrary",)),
        name="moe_combine",
    )(*tabs, out_sorted, lpos, x1p, x1s, g.reshape(1, d))


def _t5_bucket(dist):
    max_exact = N_BUCKETS // 2
    dd = dist.astype(F32)
    large = max_exact + (jnp.log(jnp.maximum(dd, 1.0) / max_exact)
                         / math.log(BUCKET_MAX_DIST / max_exact) * (N_BUCKETS - max_exact)).astype(jnp.int32)
    large = jnp.minimum(large, N_BUCKETS - 1)
    return jnp.where(dist < max_exact, dist, large)


def _bias_tables(rel_bias):
    dist = jnp.asarray(np.arange(N_TAPS)[None, :] * np.array(DILATIONS)[:, None], jnp.int32)
    bias = jnp.transpose(rel_bias[_t5_bucket(dist)], (2, 0, 1)).astype(F32)
    by_branch = jnp.transpose(bias, (1, 0, 2))
    g, h = by_branch.shape[:2]
    row = jnp.concatenate([by_branch[:, :, ::-1], jnp.full((g, h, WIN), NEG, F32)], axis=-1)
    flat = jnp.broadcast_to(row[:, :, None, :], (g, h, WIN, 2 * WIN + 1)).reshape(g, h, -1)
    band = flat[:, :, :WIN * 2 * WIN].reshape(g, h, WIN, 2 * WIN)
    return band, by_branch[:, :, :0:-1], by_branch[:, :, 0:1]


def kernel(x_prompt, x_sample, cache_k_win, cache_v_win, state_conv, state_ssm, rel_bias, attn_norm, w_in, conv_w, conv_b, dt_bias, a_log, d_skip, ssd_norm, w_out, ffn_norm, w_router, b_router, w_gate_up, b_gate_up, w_down, b_down, final_norm):
    bp, tp, d = x_prompt.shape
    bs, ts, _ = x_sample.shape
    depth = w_in.shape[0]
    assert depth == 1 and ts == 1 and tp % (max(DILATIONS) * WIN) == 0
    keep = min(max(DILATIONS) * WIN, tp)
    band, samp, samp0 = _bias_tables(rel_bias)
    l = 0

    xp = x_prompt.reshape(bp * tp, d)
    xs = x_sample.reshape(bs * ts, d)
    w_in_bf = jnp.pad(w_in[l], ((0, 0), (0, IN_PROJ_PAD - IN_PROJ))).astype(BF16)
    w_out_bf = w_out[l].astype(BF16)
    wr = jnp.pad(w_router[l], ((0, 0), (0, LANES - N_EXPERTS))).astype(BF16)
    br = jnp.pad(b_router[l], (0, LANES - N_EXPERTS), constant_values=NEG).reshape(1, LANES)

    k, v, z, xbc, dt_raw, *branch_qkv = _inproj_prompt(xp, attn_norm[l], w_in_bf, bp)
    k3 = k.reshape(bp, tp, KV_WIDTH)
    v3 = v.reshape(bp, tp, KV_WIDTH)
    att_parts = [_attn_branch(branch_qkv[2 * gi], branch_qkv[2 * gi + 1], band[gi])
                 for gi in range(len(DILATIONS))]
    ssm, st_p = _ssd_prompt(xbc, dt_raw, z, conv_w[l], conv_b[l], dt_bias[l], a_log[l], d_skip[l],
                            ssd_norm[l], bp)
    x1p, xnp_, lgp = _outproj_prompt(att_parts, ssm, xp, w_out_bf, ffn_norm[l], wr, br)
    k_win_p = k3[:, tp - keep:].reshape(1, bp, keep, N_KV_HEADS, HEAD_DIM)
    v_win_p = v3[:, tp - keep:].reshape(1, bp, keep, N_KV_HEADS, HEAD_DIM)
    conv_p = xbc.reshape(bp, tp, CONV_DIM)[:, tp - (CONV_W - 1):][None]

    q_s, k_s, v_s, z_s, xbc_s, dt_s = _inproj(xs, attn_norm[l], w_in_bf, bs * ts)
    q_s = q_s.reshape(bs, N_ATT_HEADS, HEAD_DIM)
    k_s = k_s.reshape(bs, N_KV_HEADS, HEAD_DIM)
    v_s = v_s.reshape(bs, N_KV_HEADS, HEAD_DIM)
    wbuf = cache_k_win.shape[2]
    att_s = _attn_sample(q_s, k_s, v_s, cache_k_win[l].reshape(bs, wbuf, KV_WIDTH),
                         cache_v_win[l].reshape(bs, wbuf, KV_WIDTH), samp, samp0)
    xa_s, xdt_s, decay_s = _conv_sample(xbc_s, state_conv[l], conv_w[l], conv_b[l], dt_s, dt_bias[l],
                                        a_log[l])
    nbc = N_SSM_GROUPS * D_STATE
    h_s, y_s = _ssm_sample(xdt_s, decay_s, xa_s[:, SSM_WIDTH:SSM_WIDTH + nbc], xa_s[:, SSM_WIDTH + nbc:],
                           state_ssm[l])
    x1s, xns, lgs = _outproj_sample(att_s.reshape(bs, ATT_WIDTH), y_s, xa_s[:, :SSM_WIDTH], z_s,
                                    d_skip[l], ssd_norm[l], xs, w_out_bf, ffn_norm[l], wr, br)
    conv_s = jnp.concatenate([state_conv[l][:, 1:], xbc_s[:, None]], axis=1)[None]

    n_s = bs * ts
    assert (bp * tp) % TT == 0 and n_s <= TT
    m_pad = bp * tp + TT
    logits = jnp.concatenate([lgp[:, :N_EXPERTS], lgs[:, :N_EXPERTS]], axis=0)
    tabs, lpos, gate, groups, n_rows = _route(logits, m_pad)
    by_tile = lambda a: jnp.transpose(a.reshape(m_pad // TT, TT, TOP_K), (0, 2, 1))
    pad_rows = lambda a: jnp.pad(a, ((0, TT - n_s), (0, 0)))
    x_sorted = _dispatch(tabs, xnp_, pad_rows(xns), by_tile(lpos), by_tile(gate), n_rows)
    out_sorted = _moe_ffn(groups, x_sorted, w_gate_up[l], b_gate_up[l], w_down[l], b_down[l])
    y_p, y_s_out = _combine(tabs, out_sorted, lpos, x1p, pad_rows(x1s), final_norm)
    y_s_out = y_s_out[:n_s]

    return (y_p.reshape(bp, tp, d), y_s_out.reshape(bs, ts, d), k_win_p, v_win_p, conv_p, st_p[None],
            k_s.reshape(1, bs, ts, N_KV_HEADS, HEAD_DIM), v_s.reshape(1, bs, ts, N_KV_HEADS, HEAD_DIM),
            conv_s, h_s[None])
```

```python
import functools
import math

import jax
import jax.numpy as jnp
import numpy as np
from jax import lax
from jax.experimental import pallas as pl
from jax.experimental.pallas import tpu as pltpu

F32 = jnp.float32
BF16 = jnp.bfloat16
HIGHEST = lax.Precision.HIGHEST

LANES = 128
SUBLANES = 8
VMEM_LIMIT = 56 * 1024 * 1024

HEAD_DIM = 64
N_ATT_HEADS = 16
N_KV_HEADS = 4
KV_REP = N_ATT_HEADS // N_KV_HEADS
ATT_WIDTH = N_ATT_HEADS * HEAD_DIM
KV_WIDTH = N_KV_HEADS * HEAD_DIM
DILATIONS = (1, 4, 16)
N_TAPS = 129
WIN = N_TAPS - 1
ATT_SCALE = HEAD_DIM ** -0.5
N_BUCKETS = 32
BUCKET_MAX_DIST = 2048
SSM_HEAD_DIM = 64
N_SSM_HEADS = 16
SSM_WIDTH = N_SSM_HEADS * SSM_HEAD_DIM
N_SSM_GROUPS = 2
HEADS_PER_GROUP = N_SSM_HEADS // N_SSM_GROUPS
D_STATE = 128
CONV_W = 4
CONV_DIM = SSM_WIDTH + 2 * N_SSM_GROUPS * D_STATE
SSD_CHUNK = 128
N_EXPERTS = 32
TOP_K = 4
SWIGLU_LIMIT = 7.0
SWIGLU_ALPHA = 1.702
EPS = 1e-5
NEG = -1e30

Q0, K0, V0, Z0, X0, DT0 = 0, 1024, 1280, 1536, 2560, 4096
IN_PROJ = DT0 + N_SSM_HEADS
IN_PROJ_PAD = DT0 + LANES

TM_PROJ = 512
TM_OUT = 256
TM_MOE = 512
TF_MOE = 512
TM_FIN = 256


def _cparams(sem):
    return pltpu.CompilerParams(dimension_semantics=sem, vmem_limit_bytes=VMEM_LIMIT)


def _const_spec(shape):
    nd = len(shape)
    return pl.BlockSpec(shape, lambda *_: (0,) * nd, pipeline_mode=pl.Buffered(1))


def _spread(a, onehot_bf, terms):
    out, rest = None, a
    for _ in range(terms):
        piece = rest.astype(BF16)
        part = jnp.dot(piece, onehot_bf, preferred_element_type=F32)
        out = part if out is None else out + part
        rest = rest - piece.astype(F32)
    return out


def _rms(x, g):
    ms = jnp.mean(x * x, axis=-1, keepdims=True)
    return x * lax.rsqrt(ms + EPS) * g


def _silu(x):
    return x * (1.0 / (1.0 + jnp.exp(-x)))


def _inproj_kernel(x_ref, g_ref, w_ref, q_ref, k_ref, v_ref, z_ref, xbc_ref, dt_ref):
    xn = _rms(x_ref[...], g_ref[...]).astype(BF16)

    def mm(lo, hi):
        return jnp.dot(xn, w_ref[:, lo:hi], preferred_element_type=F32)

    q_ref[...] = (mm(Q0, K0) * ATT_SCALE).astype(BF16)
    k_ref[...] = mm(K0, V0)
    v_ref[...] = mm(V0, Z0)
    z_ref[...] = mm(Z0, X0)
    xbc_ref[...] = mm(X0, DT0)
    dt_ref[...] = mm(DT0, IN_PROJ_PAD)


def _inproj(x2d, g, w_bf, tm):
    m, d = x2d.shape
    widths = (ATT_WIDTH, KV_WIDTH, KV_WIDTH, SSM_WIDTH, CONV_DIM, LANES)
    dtypes = (BF16, F32, F32, F32, F32, F32)
    return pl.pallas_call(
        _inproj_kernel,
        grid=(m // tm,),
        in_specs=[pl.BlockSpec((tm, d), lambda i: (i, 0)),
                  _const_spec((1, d)),
                  _const_spec((d, IN_PROJ_PAD))],
        out_specs=[pl.BlockSpec((tm, w), lambda i: (i, 0)) for w in widths],
        out_shape=[jax.ShapeDtypeStruct((m, w), t) for w, t in zip(widths, dtypes)],
        compiler_params=_cparams(("parallel",)),
        name=f"inproj_{tm}",
    )(x2d, g.reshape(1, d), w_bf)


def _deinterleave_matrix(n, dil):
    p = np.zeros((n, n), np.float32)
    src = np.arange(n)
    p[(src % dil) * (n // dil) + src // dil, src] = 1.0
    return p


def _inproj_prompt_kernel(x_ref, g_ref, w_ref, p_mid_ref, p_far_ref, k_ref, v_ref, z_ref, xbc_ref, dt_ref,
                          q0_ref, kv0_ref, q1_ref, kv1_ref, q2_ref, kv2_ref):
    tm = x_ref.shape[0]
    xn = _rms(x_ref[...], g_ref[...]).astype(BF16)

    def mm(lo, hi):
        return jnp.dot(xn, w_ref[:, lo:hi], preferred_element_type=F32)

    q = (mm(Q0, K0) * ATT_SCALE).astype(BF16)
    k = mm(K0, V0)
    v = mm(V0, Z0)
    k_ref[...] = k
    v_ref[...] = v
    z_ref[...] = mm(Z0, X0)
    xbc_ref[...] = mm(X0, DT0)
    dt_ref[...] = mm(DT0, IN_PROJ_PAD)
    kv = jnp.concatenate([k, v], axis=-1).astype(BF16)
    q0_ref[0, 0] = q
    kv0_ref[0, 0] = kv
    for dil, p_ref, qd_ref, kvd_ref in ((DILATIONS[1], p_mid_ref, q1_ref, kv1_ref),
                                        (DILATIONS[2], p_far_ref, q2_ref, kv2_ref)):
        qp = jnp.dot(p_ref[...], q, preferred_element_type=F32).astype(BF16)
        kvp = jnp.dot(p_ref[...], kv, preferred_element_type=F32).astype(BF16)
        rows = tm // dil
        for r in range(dil):
            qd_ref[0, r] = qp[r * rows:(r + 1) * rows]
            kvd_ref[0, r] = kvp[r * rows:(r + 1) * rows]


def _inproj_prompt(x2d, g, w_bf, batch):
    m, d = x2d.shape
    tm = TM_PROJ
    seq = m // batch
    per_b = seq // tm
    assert DILATIONS[0] == 1 and seq % tm == 0 and tm % (max(DILATIONS) * 2 * SUBLANES) == 0
    widths = (KV_WIDTH, KV_WIDTH, SSM_WIDTH, CONV_DIM, LANES)
    rows = lambda w: pl.BlockSpec((tm, w), lambda i: (i, 0))
    split = lambda dil, w: pl.BlockSpec((1, dil, tm // dil, w), lambda i: (i // per_b, 0, i % per_b, 0))
    branch_specs, branch_shapes = [], []
    for dil in DILATIONS:
        for w in (ATT_WIDTH, 2 * KV_WIDTH):
            branch_specs.append(split(dil, w))
            branch_shapes.append(jax.ShapeDtypeStruct((batch, dil, seq // dil, w), BF16))
    perms = [jnp.asarray(_deinterleave_matrix(tm, dil), BF16) for dil in DILATIONS[1:]]
    return pl.pallas_call(
        _inproj_prompt_kernel,
        grid=(m // tm,),
        in_specs=[rows(d), _const_spec((1, d)), _const_spec((d, IN_PROJ_PAD)),
                  _const_spec((tm, tm)), _const_spec((tm, tm))],
        out_specs=[rows(w) for w in widths] + branch_specs,
        out_shape=[jax.ShapeDtypeStruct((m, w), F32) for w in widths] + branch_shapes,
        compiler_params=_cparams(("parallel",)),
        name="inproj_prompt",
    )(x2d, g.reshape(1, d), w_bf, *perms)


def _attn_kernel(q_ref, kvp_ref, kvc_ref, bias_ref, o_ref, lse_ref):
    first = pl.program_id(2) == 0
    lane = lax.broadcasted_iota(jnp.int32, (1, 2 * WIN), 1)
    prev_mask = jnp.where(jnp.logical_and(first, lane < WIN), NEG, 0.0)
    head_lane = lax.broadcasted_iota(jnp.int32, (WIN, LANES), 1)
    lse_tile = jnp.zeros((WIN, LANES), F32)
    for kvh in range(N_KV_HEADS):
        ks = slice(kvh * HEAD_DIM, (kvh + 1) * HEAD_DIM)
        vs = slice(KV_WIDTH + kvh * HEAD_DIM, KV_WIDTH + (kvh + 1) * HEAD_DIM)
        kw = jnp.concatenate([kvp_ref[0, 0, :, ks], kvc_ref[0, 0, :, ks]], axis=0)
        vw = jnp.concatenate([kvp_ref[0, 0, :, vs], kvc_ref[0, 0, :, vs]], axis=0)
        for pair in range(KV_REP // 2):
            outs = []
            for r in range(2):
                h = kvh * KV_REP + pair * 2 + r
                qh = q_ref[0, 0, :, h * HEAD_DIM:(h + 1) * HEAD_DIM]
                s = lax.dot_general(qh, kw, (((1,), (1,)), ((), ())), preferred_element_type=F32)
                s = s + bias_ref[h] + prev_mask
                mx = jnp.max(s, axis=-1, keepdims=True)
                p = jnp.exp(s - mx)
                l = jnp.sum(p, axis=-1, keepdims=True)
                o = jnp.dot(p.astype(BF16), vw, preferred_element_type=F32)
                outs.append(o * (1.0 / l))
                lse_tile = jnp.where(head_lane == h, mx + jnp.log(l), lse_tile)
            h0 = kvh * KV_REP + pair * 2
            o_ref[0, 0, :, h0 * HEAD_DIM:(h0 + 2) * HEAD_DIM] = jnp.concatenate(outs, axis=-1).astype(BF16)
    lse_ref[0, 0] = lse_tile


def _attn_branch(q, kv, bias_mat):
    b, dil, sub, _ = q.shape
    nb = sub // WIN
    cur = lambda bb, r, i: (bb, r, i, 0)
    prev = lambda bb, r, i: (bb, r, jnp.maximum(i - 1, 0), 0)
    return pl.pallas_call(
        _attn_kernel,
        grid=(b, dil, nb),
        in_specs=[pl.BlockSpec((1, 1, WIN, ATT_WIDTH), cur),
                  pl.BlockSpec((1, 1, WIN, 2 * KV_WIDTH), prev),
                  pl.BlockSpec((1, 1, WIN, 2 * KV_WIDTH), cur),
                  _const_spec((N_ATT_HEADS, WIN, 2 * WIN))],
        out_specs=[pl.BlockSpec((1, 1, WIN, ATT_WIDTH), cur),
                   pl.BlockSpec((1, 1, WIN, LANES), cur)],
        out_shape=[jax.ShapeDtypeStruct((b, dil, sub, ATT_WIDTH), BF16),
                   jax.ShapeDtypeStruct((b, dil, sub, LANES), F32)],
        compiler_params=_cparams(("parallel", "parallel", "arbitrary")),
        name=f"attn_dil{dil}",
    )(q, kv, kv, bias_mat)


def _softplus(x):
    return jnp.maximum(x, 0.0) + jnp.log(1.0 + jnp.exp(-jnp.abs(x)))


def _ssd_kernel(xbc_ref, dt_ref, z_ref, cw_ref, cb_ref, dtb_ref, alog_ref, dskip_ref, gn_ref, e_ref,
                y_ref, st_ref, ext_ref, state_ref):
    c = pl.program_id(1)
    L = SSD_CHUNK

    @pl.when(c == 0)
    def _():
        ext_ref[0:SUBLANES, :] = jnp.zeros((SUBLANES, CONV_DIM), F32)
        state_ref[...] = jnp.zeros_like(state_ref)

    ext_ref[SUBLANES:SUBLANES + L, :] = xbc_ref[...]
    acc = cb_ref[...] + ext_ref[SUBLANES:SUBLANES + L, :] * cw_ref[CONV_W - 1:CONV_W, :]
    for i in range(CONV_W - 1):
        off = SUBLANES - (CONV_W - 1) + i
        acc = acc + ext_ref[off:off + L, :] * cw_ref[i:i + 1, :]
    ext_ref[0:SUBLANES, :] = ext_ref[L:L + SUBLANES, :]
    xa = _silu(acc)

    dt = _softplus(dt_ref[...] + dtb_ref[...])
    da = dt * (-jnp.exp(alog_ref[...]))
    row = lax.broadcasted_iota(jnp.int32, (L, L), 0)
    col = lax.broadcasted_iota(jnp.int32, (L, L), 1)
    tri = row >= col
    a_cs = jnp.dot(tri.astype(F32), da, preferred_element_type=F32, precision=HIGHEST)
    a_cs_t = a_cs.T
    expand = e_ref[...]
    acs_full = _spread(a_cs, expand, 3)
    dt_full = _spread(dt, expand, 3)
    exp_acs = jnp.exp(acs_full)
    a_last = acs_full[L - 1:L, :]
    exp_last = exp_acs[L - 1:L, :]
    xs = xa[:, :SSM_WIDTH]
    xdt = xs * dt_full
    xw = xdt * jnp.exp(a_last - acs_full)

    for g in range(N_SSM_GROUPS):
        b0 = SSM_WIDTH + g * D_STATE
        c0 = SSM_WIDTH + N_SSM_GROUPS * D_STATE + g * D_STATE
        bg_t = xa[:, b0:b0 + D_STATE].T.astype(BF16)
        cg = xa[:, c0:c0 + D_STATE].astype(BF16)
        gram = jnp.dot(cg, bg_t, preferred_element_type=F32)
        for hh in range(HEADS_PER_GROUP):
            h = g * HEADS_PER_GROUP + hh
            hs = slice(h * SSM_HEAD_DIM, (h + 1) * SSM_HEAD_DIM)
            seg = jnp.where(tri, a_cs[:, h:h + 1] - a_cs_t[h:h + 1, :], NEG)
            scores = (gram * jnp.exp(seg)).astype(BF16)
            y_diag = jnp.dot(scores, xdt[:, hs].astype(BF16), preferred_element_type=F32)
            st = state_ref[h]
            y_off = jnp.dot(cg, st.astype(BF16), preferred_element_type=F32) * exp_acs[:, hs]
            y_ref[:, hs] = y_diag + y_off
            state_ref[h] = exp_last[:, hs] * st + jnp.dot(bg_t, xw[:, hs].astype(BF16),
                                                          preferred_element_type=F32)

    y = y_ref[...] + dskip_ref[...] * xs
    u = y * _silu(z_ref[...])
    gw = SSM_WIDTH // N_SSM_GROUPS
    parts = []
    for g in range(N_SSM_GROUPS):
        ug = u[:, g * gw:(g + 1) * gw]
        parts.append(ug * lax.rsqrt(jnp.mean(ug * ug, axis=-1, keepdims=True) + EPS))
    y_ref[...] = jnp.concatenate(parts, axis=-1) * gn_ref[...]

    @pl.when(c == pl.num_programs(1) - 1)
    def _():
        st_ref[0] = state_ref[...]


def _head_expand():
    e = np.zeros((LANES, SSM_WIDTH), np.float32)
    for h in range(N_SSM_HEADS):
        e[h, h * SSM_HEAD_DIM:(h + 1) * SSM_HEAD_DIM] = 1.0
    return jnp.asarray(e)


def _pad_lanes(v):
    return jnp.pad(v.astype(F32), (0, LANES - v.shape[0])).reshape(1, LANES)


def _ssd_prompt(xbc, dt_raw, z, conv_w, conv_b, dt_bias, a_log, d_skip, ssd_norm, batch):
    m = xbc.shape[0]
    nc = m // batch // SSD_CHUNK
    L = SSD_CHUNK
    rows = lambda b, c: (b * nc + c, 0)
    y, st = pl.pallas_call(
        _ssd_kernel,
        grid=(batch, nc),
        in_specs=[pl.BlockSpec((L, CONV_DIM), rows),
                  pl.BlockSpec((L, LANES), rows),
                  pl.BlockSpec((L, SSM_WIDTH), rows),
                  _const_spec((CONV_W, CONV_DIM)),
                  _const_spec((1, CONV_DIM)),
                  _const_spec((1, LANES)),
                  _const_spec((1, LANES)),
                  _const_spec((1, SSM_WIDTH)),
                  _const_spec((1, SSM_WIDTH)),
                  _const_spec((LANES, SSM_WIDTH))],
        out_specs=[pl.BlockSpec((L, SSM_WIDTH), rows),
                   pl.BlockSpec((1, N_SSM_HEADS, D_STATE, SSM_HEAD_DIM), lambda b, c: (b, 0, 0, 0))],
        out_shape=[jax.ShapeDtypeStruct((m, SSM_WIDTH), F32),
                   jax.ShapeDtypeStruct((batch, N_SSM_HEADS, D_STATE, SSM_HEAD_DIM), F32)],
        scratch_shapes=[pltpu.VMEM((SUBLANES + L, CONV_DIM), F32),
                        pltpu.VMEM((N_SSM_HEADS, D_STATE, SSM_HEAD_DIM), F32)],
        compiler_params=_cparams(("parallel", "arbitrary")),
        name="ssd_prompt",
    )(xbc, dt_raw, z, conv_w, conv_b.reshape(1, CONV_DIM), _pad_lanes(dt_bias), _pad_lanes(a_log),
      jnp.repeat(d_skip, SSM_HEAD_DIM).reshape(1, SSM_WIDTH), ssd_norm.reshape(1, SSM_WIDTH),
      _head_expand().astype(BF16))
    return y, jnp.swapaxes(st, -1, -2)


def _router(xn_bf, wr_ref, br_ref):
    return jnp.dot(xn_bf, wr_ref[...], preferred_element_type=F32) + br_ref[...]


def _outproj_kernel(o0, l0, o1, l1, o2, l2, u1_ref, u2_ref, e_ref, ssm_ref, x_ref, w_ref, g_ref, wr_ref,
                    br_ref, x1_ref, xn_ref, lg_ref):
    def natural(o_ref, l_ref, u_ref):
        dil = o_ref.shape[1]
        if dil == 1:
            return o_ref[0, 0].astype(F32), l_ref[0, 0]
        o_cat = jnp.concatenate([o_ref[0, r] for r in range(dil)], axis=0)
        l_cat = jnp.concatenate([l_ref[0, r] for r in range(dil)], axis=0)
        hi = l_cat.astype(BF16)
        lo = (l_cat - hi.astype(F32)).astype(BF16)
        u = u_ref[...]
        return (jnp.dot(u, o_cat, preferred_element_type=F32),
                jnp.dot(u, hi, preferred_element_type=F32) + jnp.dot(u, lo, preferred_element_type=F32))

    branches = [natural(o0, l0, None), natural(o1, l1, u1_ref), natural(o2, l2, u2_ref)]
    lses = [_spread(l, e_ref[...], 2) for _, l in branches]
    mx = functools.reduce(jnp.maximum, lses)
    ws = [jnp.exp(l - mx) for l in lses]
    num = functools.reduce(jnp.add, [w * o for w, (o, _) in zip(ws, branches)])
    att = num * (1.0 / functools.reduce(jnp.add, ws))
    y = jnp.dot(att.astype(BF16), w_ref[:ATT_WIDTH, :], preferred_element_type=F32)
    y = y + jnp.dot(ssm_ref[...].astype(BF16), w_ref[ATT_WIDTH:, :], preferred_element_type=F32)
    x1 = x_ref[...] + y
    x1_ref[...] = x1
    xn = _rms(x1, g_ref[...]).astype(BF16)
    xn_ref[...] = xn
    lg_ref[...] = _router(xn, wr_ref, br_ref)


def _outproj_prompt(att_parts, ssm, x2d, w_bf, g, wr, br):
    m, d = x2d.shape
    mix = w_bf.shape[0]
    tm = TM_OUT
    batch = att_parts[0][0].shape[0]
    per_b = m // batch // tm
    assert tm % (max(DILATIONS) * 2 * SUBLANES) == 0
    row = lambda w: pl.BlockSpec((tm, w), lambda i: (i, 0))
    split = lambda dil, w: pl.BlockSpec((1, dil, tm // dil, w), lambda i: (i // per_b, 0, i % per_b, 0))
    branch_specs = [split(o.shape[1], w) for o, _ in att_parts for w in (ATT_WIDTH, LANES)]
    interleave = [jnp.asarray(_deinterleave_matrix(tm, dil).T, BF16) for dil in DILATIONS[1:]]
    return pl.pallas_call(
        _outproj_kernel,
        grid=(m // tm,),
        in_specs=branch_specs + [_const_spec((tm, tm)), _const_spec((tm, tm)),
                                 _const_spec((LANES, ATT_WIDTH)), row(SSM_WIDTH), row(d),
                                 _const_spec((mix, d)), _const_spec((1, d)),
                                 _const_spec((d, LANES)), _const_spec((1, LANES))],
        out_specs=[row(d), row(d), row(LANES)],
        out_shape=[jax.ShapeDtypeStruct((m, d), F32), jax.ShapeDtypeStruct((m, d), BF16),
                   jax.ShapeDtypeStruct((m, LANES), F32)],
        compiler_params=_cparams(("parallel",)),
        name="outproj_prompt",
    )(*[a for part in att_parts for a in part], *interleave, _head_expand().astype(BF16), ssm, x2d, w_bf,
      g.reshape(1, d), wr, br)


def _attn_sample_kernel(q_ref, kn_ref, vn_ref, kc_ref, vc_ref, bias_ref, bias0_ref, o_ref):
    w = kc_ref.shape[1]
    q = q_ref[0]
    head_grp = lax.broadcasted_iota(jnp.int32, (N_ATT_HEADS, 1), 0) // KV_REP
    kn = kn_ref[0].astype(BF16).astype(F32)
    vn = vn_ref[0].astype(BF16).astype(F32)
    s_self = jnp.sum(q.astype(F32) * kn, axis=-1, keepdims=True)

    def taps(c_ref, dil):
        span = WIN * dil
        rows = c_ref[0, w - span:w, :].astype(BF16)
        if dil == 1:
            return rows
        tap = lax.broadcasted_iota(jnp.int32, (WIN, span), 0)
        col = lax.broadcasted_iota(jnp.int32, (WIN, span), 1)
        pick = jnp.where(col == tap * dil, 1.0, 0.0).astype(BF16)
        return jnp.dot(pick, rows, preferred_element_type=F32).astype(BF16)

    scores, selfs, lses = [], [], []
    for g, dil in enumerate(DILATIONS):
        kk = taps(kc_ref, dil)
        s = jnp.zeros((N_ATT_HEADS, WIN), F32)
        for kvh in range(N_KV_HEADS):
            cs = slice(kvh * HEAD_DIM, (kvh + 1) * HEAD_DIM)
            sk = lax.dot_general(q, kk[:, cs], (((1,), (1,)), ((), ())), preferred_element_type=F32)
            s = jnp.where(head_grp == kvh, sk, s)
        s = s + bias_ref[g]
        s0 = s_self + bias0_ref[g]
        mx = jnp.maximum(jnp.max(s, axis=-1, keepdims=True), s0)
        lse = mx + jnp.log(jnp.sum(jnp.exp(s - mx), axis=-1, keepdims=True) + jnp.exp(s0 - mx))
        scores.append(s); selfs.append(s0); lses.append(lse)
    top = functools.reduce(jnp.maximum, lses)
    es = [jnp.exp(l - top) for l in lses]
    tot = functools.reduce(jnp.add, es)
    o = jnp.zeros((N_ATT_HEADS, HEAD_DIM), F32)
    for s, s0, lse, e, dil in zip(scores, selfs, lses, es, DILATIONS):
        wgt = e / tot
        p = (jnp.exp(s - lse) * wgt).astype(BF16)
        p0 = (jnp.exp(s0 - lse) * wgt).astype(BF16).astype(F32)
        vv = taps(vc_ref, dil)
        o = o + p0 * vn
        for kvh in range(N_KV_HEADS):
            cs = slice(kvh * HEAD_DIM, (kvh + 1) * HEAD_DIM)
            ok = jnp.dot(p, vv[:, cs], preferred_element_type=F32)
            o = o + jnp.where(head_grp == kvh, ok, 0.0)
    o_ref[0] = o


def _attn_sample(q, k_new, v_new, k_cache, v_cache, bias_s, bias0_s):
    n, w = k_cache.shape[0], k_cache.shape[1]
    assert w % (max(DILATIONS) * WIN) == 0
    tok = lambda b: (b, 0, 0)
    head = pl.BlockSpec((1, N_ATT_HEADS, HEAD_DIM), tok)
    window = pl.BlockSpec((1, w, KV_WIDTH), tok)
    return pl.pallas_call(
        _attn_sample_kernel,
        grid=(n,),
        in_specs=[head, head, head, window, window,
                  _const_spec((len(DILATIONS), N_ATT_HEADS, WIN)),
                  _const_spec((len(DILATIONS), N_ATT_HEADS, 1))],
        out_specs=head,
        out_shape=jax.ShapeDtypeStruct((n, N_ATT_HEADS, HEAD_DIM), F32),
        compiler_params=_cparams(("parallel",)),
        name="attn_sample",
    )(q, jnp.repeat(k_new, KV_REP, axis=1), jnp.repeat(v_new, KV_REP, axis=1), k_cache, v_cache,
      bias_s, bias0_s)


def _conv_sample_kernel(xbc_ref, b0_ref, b1_ref, b2_ref, cw_ref, cb_ref, dt_ref, dtb_ref, alog_ref,
                        e_ref, xa_ref, xdt_ref, decay_ref):
    acc = cb_ref[...] + xbc_ref[...] * cw_ref[CONV_W - 1:CONV_W, :]
    for i, buf in enumerate((b0_ref, b1_ref, b2_ref)):
        acc = acc + buf[...] * cw_ref[i:i + 1, :]
    xa = _silu(acc)
    xa_ref[...] = xa
    dt = _softplus(dt_ref[...] + dtb_ref[...])
    decay_ref[...] = jnp.exp(dt * (-jnp.exp(alog_ref[...])))
    dt_full = jnp.dot(dt, e_ref[...], preferred_element_type=F32, precision=HIGHEST)
    xdt_ref[...] = xa[:, :SSM_WIDTH] * dt_full


def _conv_sample(xbc, conv_buf, conv_w, conv_b, dt_raw, dt_bias, a_log):
    n = xbc.shape[0]
    args = (xbc, conv_buf[:, 0], conv_buf[:, 1], conv_buf[:, 2], conv_w, conv_b.reshape(1, CONV_DIM),
            dt_raw, _pad_lanes(dt_bias), _pad_lanes(a_log),
            _head_expand())
    return pl.pallas_call(
        _conv_sample_kernel,
        grid=(1,),
        in_specs=[_const_spec(a.shape) for a in args],
        out_specs=[_const_spec((n, CONV_DIM)), _const_spec((n, SSM_WIDTH)), _const_spec((n, LANES))],
        out_shape=[jax.ShapeDtypeStruct((n, CONV_DIM), F32), jax.ShapeDtypeStruct((n, SSM_WIDTH), F32),
                   jax.ShapeDtypeStruct((n, LANES), F32)],
        compiler_params=_cparams(("arbitrary",)),
        name="conv_sample",
    )(*args)


def _ssm_sample_kernel(xdt_ref, decay_ref, b_ref, c_ref, h0_ref, hn_ref, y_ref):
    for g in range(N_SSM_GROUPS):
        hs = slice(g * HEADS_PER_GROUP, (g + 1) * HEADS_PER_GROUP)
        hn = decay_ref[0, hs] * h0_ref[0, hs] + xdt_ref[0, hs] * b_ref[0, g]
        hn_ref[0, hs] = hn
        c_row = c_ref[0, g].astype(BF16).astype(F32)
        y_ref[0, hs] = jnp.sum(hn.astype(BF16).astype(F32) * c_row, axis=-1, keepdims=True)


def _ssm_sample(xdt, decay, bmat, cmat, h0):
    n = xdt.shape[0]
    p = SSM_HEAD_DIM
    tok4 = lambda b: (b, 0, 0, 0)
    hn, y = pl.pallas_call(
        _ssm_sample_kernel,
        grid=(n,),
        in_specs=[pl.BlockSpec((1, N_SSM_HEADS, p, 1), tok4),
                  pl.BlockSpec((1, N_SSM_HEADS, 1, 1), tok4),
                  pl.BlockSpec((1, N_SSM_GROUPS, 1, D_STATE), tok4),
                  pl.BlockSpec((1, N_SSM_GROUPS, 1, D_STATE), tok4),
                  pl.BlockSpec((1, N_SSM_HEADS, p, D_STATE), tok4)],
        out_specs=[pl.BlockSpec((1, N_SSM_HEADS, p, D_STATE), tok4),
                   pl.BlockSpec((1, N_SSM_HEADS, p, 1), tok4)],
        out_shape=[jax.ShapeDtypeStruct((n, N_SSM_HEADS, p, D_STATE), F32),
                   jax.ShapeDtypeStruct((n, N_SSM_HEADS, p, 1), F32)],
        compiler_params=_cparams(("parallel",)),
        name="ssm_sample",
    )(xdt.reshape(n, N_SSM_HEADS, p, 1), decay[:, :N_SSM_HEADS].reshape(n, N_SSM_HEADS, 1, 1),
      bmat.reshape(n, N_SSM_GROUPS, 1, D_STATE), cmat.reshape(n, N_SSM_GROUPS, 1, D_STATE), h0)
    return hn, y.reshape(n, SSM_WIDTH)


def _outproj_sample_kernel(att_ref, y_ref, xs_ref, z_ref, dskip_ref, gn_ref, x_ref, w_ref, g_ref,
                           wr_ref, br_ref, x1_ref, xn_ref, lg_ref):
    y = y_ref[...] + dskip_ref[...] * xs_ref[...]
    u = y * _silu(z_ref[...])
    gw = SSM_WIDTH // N_SSM_GROUPS
    parts = []
    for g in range(N_SSM_GROUPS):
        ug = u[:, g * gw:(g + 1) * gw]
        parts.append(ug * lax.rsqrt(jnp.mean(ug * ug, axis=-1, keepdims=True) + EPS)
                     * gn_ref[:, g * gw:(g + 1) * gw])
    mix = jnp.concatenate([att_ref[...]] + parts, axis=-1).astype(BF16)
    x1 = x_ref[...] + jnp.dot(mix, w_ref[...], preferred_element_type=F32)
    x1_ref[...] = x1
    xn = _rms(x1, g_ref[...]).astype(BF16)
    xn_ref[...] = xn
    lg_ref[...] = _router(xn, wr_ref, br_ref)


def _outproj_sample(att, y, xs, z, d_skip, ssd_norm, x2d, w_bf, g, wr, br):
    n, d = x2d.shape
    args = (att, y, xs, z, jnp.repeat(d_skip, SSM_HEAD_DIM).reshape(1, SSM_WIDTH),
            ssd_norm.reshape(1, SSM_WIDTH), x2d, w_bf, g.reshape(1, d), wr, br)
    return pl.pallas_call(
        _outproj_sample_kernel,
        grid=(1,),
        in_specs=[_const_spec(a.shape) for a in args],
        out_specs=[_const_spec((n, d)), _const_spec((n, d)), _const_spec((n, LANES))],
        out_shape=[jax.ShapeDtypeStruct((n, d), F32), jax.ShapeDtypeStruct((n, d), BF16),
                   jax.ShapeDtypeStruct((n, LANES), F32)],
        compiler_params=_cparams(("arbitrary",)),
        name="outproj_sample",
    )(*args)


TT = 256
CHUNK_ALIGN = SUBLANES
LROWS = -(-(TT * TOP_K + N_EXPERTS * (CHUNK_ALIGN - 1)) // TT) * TT
CHUNK_SIZES = tuple(1 << b for b in range(TT.bit_length() - 1, CHUNK_ALIGN.bit_length() - 2, -1))
SUB = 256
SUB_TAIL = SUB // 2
RG = 5 * SUB
TF = 256
VMEM_LIMIT_FFN = 60000 * 1024


def _chunk_loop(cnt_ref, loff_ref, dest_ref, tile, fn):
    def per_expert(e, carry):
        idx = tile * N_EXPERTS + e
        n, off, dst = cnt_ref[idx], loff_ref[idx], dest_ref[idx]
        for size in CHUNK_SIZES:
            take = (n & size) != 0

            @pl.when(take)
            def _(off=off, dst=dst, size=size):
                fn(pl.multiple_of(off, CHUNK_ALIGN), pl.multiple_of(dst, CHUNK_ALIGN), size)

            step = jnp.where(take, size, 0)
            off, dst = off + step, dst + step
        return carry

    lax.fori_loop(0, N_EXPERTS, per_expert, 0)


def _dispatch_kernel(cnt_ref, loff_ref, dest_ref, xp_ref, xs_ref, lpos_ref, gate_ref, out_hbm, buf, sem):
    t = pl.program_id(0)
    last = pl.num_programs(0) - 1
    d = xp_ref.shape[1]
    x = jnp.where(t == last, xs_ref[...], xp_ref[...])
    rows = lax.broadcasted_iota(jnp.int32, (LROWS, TT), 0)
    onehot = jnp.zeros((LROWS, TT), F32)
    wcol = jnp.zeros((LROWS, 1), F32)
    for k in range(TOP_K):
        hit = jnp.where(rows == lpos_ref[0, k:k + 1, :], 1.0, 0.0)
        onehot = onehot + hit
        wcol = wcol + jnp.sum(hit * gate_ref[0, k:k + 1, :], axis=-1, keepdims=True)
    tile_buf = buf.at[t % 2]
    tile_buf[:, 0:d] = jnp.dot(onehot.astype(BF16), x, preferred_element_type=F32)
    tile_buf[:, d:d + LANES] = jnp.broadcast_to(wcol, (LROWS, LANES))

    def copy(tile, off, dst, size):
        return pltpu.make_async_copy(buf.at[tile % 2, pl.ds(off, size)], out_hbm.at[pl.ds(dst, size)],
                                     sem.at[tile % 2])

    _chunk_loop(cnt_ref, loff_ref, dest_ref, t, lambda o, g, s: copy(t, o, g, s).start())

    @pl.when(t > 0)
    def _():
        _chunk_loop(cnt_ref, loff_ref, dest_ref, t - 1, lambda o, g, s: copy(t - 1, o, g, s).wait())

    @pl.when(t == last)
    def _():
        _chunk_loop(cnt_ref, loff_ref, dest_ref, t, lambda o, g, s: copy(t, o, g, s).wait())


def _dispatch(tabs, xn_p, xn_s, lpos_t, gate_t, n_rows):
    nt = lpos_t.shape[0]
    d = xn_p.shape[1]
    last_p = xn_p.shape[0] // TT - 1
    grid_spec = pltpu.PrefetchScalarGridSpec(
        num_scalar_prefetch=3,
        grid=(nt,),
        in_specs=[pl.BlockSpec((TT, d), lambda t, *_: (jnp.minimum(t, last_p), 0)),
                  pl.BlockSpec((TT, d), lambda t, *_: (0, 0)),
                  pl.BlockSpec((1, TOP_K, TT), lambda t, *_: (t, 0, 0)),
                  pl.BlockSpec((1, TOP_K, TT), lambda t, *_: (t, 0, 0))],
        out_specs=pl.BlockSpec(memory_space=pl.ANY),
        scratch_shapes=[pltpu.VMEM((2, LROWS, d + LANES), F32), pltpu.SemaphoreType.DMA((2,))],
    )
    return pl.pallas_call(
        _dispatch_kernel,
        grid_spec=grid_spec,
        out_shape=jax.ShapeDtypeStruct((n_rows, d + LANES), F32),
        compiler_params=_cparams(("arbitrary",)),
        name="moe_dispatch",
    )(*tabs, xn_p, xn_s, lpos_t, gate_t)


def _ffn_kernel(ge_ref, gs_ref, gn_ref, gt_ref, ng_ref, xs_hbm, wgu_hbm, wd_hbm, bgu_ref, bdn_ref, out_hbm,
                xbuf, acc, ostage, wg_st, wu_st, wd_st, wg_bf, wu_bf, wd_bf, sem_w, sem_x, sem_o):
    d = acc.shape[1]
    d_ff = wd_hbm.shape[1]
    nf = d_ff // TF
    n_groups = ng_ref[0]
    total = n_groups * nf

    def w_copies(s, slot):
        g = s // nf
        f = s - g * nf
        e = ge_ref[g]
        c0 = pl.multiple_of(f * TF, TF)
        return (pltpu.make_async_copy(wgu_hbm.at[e, :, pl.ds(c0, TF)], wg_st.at[slot], sem_w.at[slot, 0]),
                pltpu.make_async_copy(wgu_hbm.at[e, :, pl.ds(d_ff + c0, TF)], wu_st.at[slot], sem_w.at[slot, 1]),
                pltpu.make_async_copy(wd_hbm.at[e, pl.ds(c0, TF), :], wd_st.at[slot], sem_w.at[slot, 2]))

    def x_copy(g, j, size=SUB):
        r0 = pl.multiple_of(j * SUB, SUB)
        return pltpu.make_async_copy(xs_hbm.at[pl.ds(pl.multiple_of(gs_ref[g] + r0, CHUNK_ALIGN), size)],
                                     xbuf.at[g % 2, pl.ds(r0, size)], sem_x)

    def o_copy(g, j, size=SUB):
        r0 = pl.multiple_of(j * SUB, SUB)
        return pltpu.make_async_copy(ostage.at[j % 2, pl.ds(0, size)],
                                     out_hbm.at[pl.ds(pl.multiple_of(gs_ref[g] + r0, CHUNK_ALIGN), size)],
                                     sem_o.at[j % 2])

    def loop(n, fn):
        lax.fori_loop(0, n, lambda j, c: (fn(j), c)[1], 0)

    def group_rows(g, op):
        loop(gn_ref[g], lambda j: op(x_copy(g, j)))

        @pl.when(gt_ref[g] == 1)
        def _():
            op(x_copy(g, gn_ref[g], SUB_TAIL))

    def drain_stores(g):
        n = gn_ref[g]
        tail = gt_ref[g]

        @pl.when(tail == 1)
        def _():
            o_copy(g, n, SUB_TAIL).wait()

        @pl.when(n >= 1)
        def _():
            o_copy(g, n - 1).wait()

        @pl.when(jnp.logical_and(n >= 2, tail == 0))
        def _():
            o_copy(g, n - 2).wait()

    @pl.when(total > 0)
    def _():
        for c in w_copies(0, 0):
            c.start()
        group_rows(0, lambda c: c.start())

    def item(s, carry):
        slot = s % 2
        g = s // nf
        f = s - g * nf
        e = ge_ref[g]
        nsub = gn_ref[g]
        xg = xbuf.at[g % 2]

        @pl.when(s + 1 < total)
        def _():
            for c in w_copies(s + 1, 1 - slot):
                c.start()

        @pl.when(f == 0)
        def _():
            group_rows(g, lambda c: c.wait())

        @pl.when(jnp.logical_and(f == 1, g + 1 < n_groups))
        def _():
            group_rows(g + 1, lambda c: c.start())

        for c in w_copies(s, slot):
            c.wait()
        bg = bgu_ref[pl.ds(e * 2 * nf + f, 1), :]
        bu = bgu_ref[pl.ds(e * 2 * nf + nf + f, 1), :]

        def sub_block(j, phase, cast=False, size=SUB):
            rs = pl.ds(pl.multiple_of(j * SUB, SUB), size)
            x = xg[rs, 0:d].astype(BF16)
            if cast:
                wg, wu, wd = (st[slot].astype(BF16) for st in (wg_st, wu_st, wd_st))
                wg_bf[...], wu_bf[...], wd_bf[...] = wg, wu, wd
            else:
                wg, wu, wd = wg_bf[...], wu_bf[...], wd_bf[...]
            hg = jnp.dot(x, wg, preferred_element_type=F32) + bg
            hu = jnp.dot(x, wu, preferred_element_type=F32) + bu
            gg = jnp.minimum(hg, SWIGLU_LIMIT)
            uu = jnp.clip(hu, -SWIGLU_LIMIT, SWIGLU_LIMIT)
            act = gg * (1.0 / (1.0 + jnp.exp(-SWIGLU_ALPHA * gg))) * (uu + 1.0)
            part = jnp.dot(act.astype(BF16), wd, preferred_element_type=F32)
            if phase == "first":
                acc[rs, :] = part + bdn_ref[pl.ds(e, 1), :]
            elif phase == "middle":
                acc[rs, :] += part
            else:
                @pl.when(j >= 2)
                def _():
                    o_copy(g, j - 2).wait()

                ostage[j % 2, 0:size] = (acc[rs, :] + part) * xg[rs, d:d + 1]
                o_copy(g, j, size).start()

        tail = gt_ref[g] == 1

        def storing_blocks():
            sub_block(jnp.int32(0), "last", cast=True)

            def pair(p):
                sub_block(2 * p + 1, "last")
                sub_block(2 * p + 2, "last")

            loop((nsub - 1) // 2, pair)

            @pl.when(nsub % 2 == 0)
            def _():
                sub_block(nsub - 1, "last")

            @pl.when(tail)
            def _():
                sub_block(nsub, "last", size=SUB_TAIL)

        def accumulating_blocks(phase):
            @pl.when(nsub < 2)
            def _():
                sub_block(jnp.int32(0), phase, cast=True)

                @pl.when(tail)
                def _():
                    sub_block(jnp.int32(1), phase, size=SUB_TAIL)

            @pl.when(nsub >= 2)
            def _():
                sub_block(jnp.int32(0), phase, cast=True, size=2 * SUB)
                rest = 2 * (nsub - 2) + gt_ref[g]
                for units in range(1, (RG - 2 * SUB) // SUB_TAIL + 1):
                    @pl.when(rest == units)
                    def _(units=units):
                        sub_block(jnp.int32(2), phase, size=units * SUB_TAIL)

        @pl.when(f == 0)
        def _():
            accumulating_blocks("first")

        @pl.when(jnp.logical_and(f > 0, f < nf - 1))
        def _():
            accumulating_blocks("middle")

        @pl.when(f == nf - 1)
        def _():
            @pl.when(g > 0)
            def _():
                drain_stores(g - 1)

            storing_blocks()

        return carry

    lax.fori_loop(0, total, item, 0)

    @pl.when(total > 0)
    def _():
        drain_stores(n_groups - 1)


def _moe_ffn(groups, x_sorted, w_gate_up, b_gu, w_down, b_dn):
    n_rows = x_sorted.shape[0]
    _, d_ff, d = w_down.shape
    nf = d_ff // TF
    assert nf >= 2
    bgu2 = b_gu.reshape(N_EXPERTS * 2 * nf, TF)
    grid_spec = pltpu.PrefetchScalarGridSpec(
        num_scalar_prefetch=5,
        grid=(1,),
        in_specs=[pl.BlockSpec(memory_space=pl.ANY), pl.BlockSpec(memory_space=pl.ANY),
                  pl.BlockSpec(memory_space=pl.ANY),
                  pl.BlockSpec(bgu2.shape, lambda i, *_: (0, 0), pipeline_mode=pl.Buffered(1)),
                  pl.BlockSpec(b_dn.shape, lambda i, *_: (0, 0), pipeline_mode=pl.Buffered(1))],
        out_specs=pl.BlockSpec(memory_space=pl.ANY),
        scratch_shapes=[pltpu.VMEM((2, RG, d + LANES), F32), pltpu.VMEM((RG, d), F32),
                        pltpu.VMEM((2, SUB, d), F32),
                        pltpu.VMEM((2, d, TF), F32), pltpu.VMEM((2, d, TF), F32), pltpu.VMEM((2, TF, d), F32),
                        pltpu.VMEM((d, TF), BF16), pltpu.VMEM((d, TF), BF16), pltpu.VMEM((TF, d), BF16),
                        pltpu.SemaphoreType.DMA((2, 3)), pltpu.SemaphoreType.DMA(()),
                        pltpu.SemaphoreType.DMA((2,))],
    )
    return pl.pallas_call(
        _ffn_kernel,
        grid_spec=grid_spec,
        out_shape=jax.ShapeDtypeStruct((n_rows, d), F32),
        compiler_params=pltpu.CompilerParams(dimension_semantics=("arbitrary",),
                                             vmem_limit_bytes=VMEM_LIMIT_FFN),
        name="moe_ffn",
    )(*groups, x_sorted, w_gate_up, w_down, bgu2, b_dn)


def _route(logits, m_pad):
    m = logits.shape[0]
    nt = m_pad // TT
    top_v, top_i = lax.top_k(logits, TOP_K)
    gate = jnp.pad(jax.nn.softmax(top_v, axis=-1), ((0, m_pad - m), (0, 0)))
    top_i = jnp.pad(top_i.astype(jnp.int32), ((0, m_pad - m), (0, 0)), constant_values=-1)
    chosen = (top_i[:, :, None] == jnp.arange(N_EXPERTS, dtype=jnp.int32)).astype(jnp.int32)
    tiles = chosen.sum(axis=1).reshape(nt, TT, N_EXPERTS)
    cnt = (tiles.sum(axis=1) + CHUNK_ALIGN - 1) // CHUNK_ALIGN * CHUNK_ALIGN
    loff = jnp.cumsum(cnt, axis=1) - cnt
    seg = cnt.sum(axis=0)
    seg_start = jnp.cumsum(seg) - seg
    dest = seg_start[None, :] + jnp.cumsum(cnt, axis=0) - cnt
    rank = jnp.cumsum(tiles, axis=1) - tiles
    lpos_all = (loff[:, None, :] + rank).reshape(m_pad, N_EXPERTS)
    lpos = (lpos_all[:, None, :] * chosen).sum(axis=-1)
    lpos = jnp.where(top_i >= 0, lpos, -1).astype(jnp.int32)
    n_rows_bound = nt * LROWS
    n_grp_max = n_rows_bound // RG + N_EXPERTS
    grp = (seg + RG - 1) // RG
    grp_end = jnp.cumsum(grp)
    gi = jnp.arange(n_grp_max, dtype=jnp.int32)
    g_exp = jnp.minimum((gi[:, None] >= grp_end[None, :]).sum(axis=1), N_EXPERTS - 1).astype(jnp.int32)
    within = gi - (grp_end[g_exp] - grp[g_exp])
    g_start = (seg_start[g_exp] + within * RG).astype(jnp.int32)
    g_rows = jnp.clip(seg[g_exp] - within * RG, 0, RG)
    g_rows = jnp.where(gi < grp_end[-1], g_rows, 0)
    n_full = g_rows // SUB
    rest = g_rows - n_full * SUB
    g_tail = (rest > 0) & (rest <= SUB_TAIL) & (n_full >= 1)
    g_nsub = n_full + ((rest > 0) & ~g_tail)
    groups = (g_exp, g_start, g_nsub.astype(jnp.int32), g_tail.astype(jnp.int32),
              grp_end[-1:].astype(jnp.int32))
    tabs = tuple(a.reshape(-1).astype(jnp.int32) for a in (cnt, loff, dest))
    return tabs, lpos, gate, groups, n_rows_bound + SUB


def _combine_kernel(cnt_ref, loff_ref, dest_ref, src_hbm, lpos_ref, x1p_ref, x1s_ref, g_ref,
                    yp_ref, ys_ref, buf, sem):
    t = pl.program_id(0)
    last = pl.num_programs(0) - 1

    def copy(tile, off, src, size):
        return pltpu.make_async_copy(src_hbm.at[pl.ds(src, size)], buf.at[tile % 2, pl.ds(off, size)],
                                     sem.at[tile % 2])

    @pl.when(t == 0)
    def _():
        buf[...] = jnp.zeros_like(buf)
        _chunk_loop(cnt_ref, loff_ref, dest_ref, t, lambda o, g, s: copy(t, o, g, s).start())

    @pl.when(t < last)
    def _():
        _chunk_loop(cnt_ref, loff_ref, dest_ref, t + 1, lambda o, g, s: copy(t + 1, o, g, s).start())

    _chunk_loop(cnt_ref, loff_ref, dest_ref, t, lambda o, g, s: copy(t, o, g, s).wait())
    cols = lax.broadcasted_iota(jnp.int32, (TT, LROWS), 1)
    sel = jnp.zeros((TT, LROWS), F32)
    for k in range(TOP_K):
        sel = sel + jnp.where(cols == lpos_ref[:, k:k + 1], 1.0, 0.0)
    f = jnp.dot(sel.astype(BF16), buf[t % 2].astype(BF16), preferred_element_type=F32)

    @pl.when(t < last)
    def _():
        yp_ref[...] = _rms(x1p_ref[...] + f, g_ref[...])

    @pl.when(t == last)
    def _():
        ys_ref[...] = _rms(x1s_ref[...] + f, g_ref[...])


def _combine(tabs, out_sorted, lpos, x1p, x1s, g):
    d = x1p.shape[1]
    nt = lpos.shape[0] // TT
    last_p = x1p.shape[0] // TT - 1
    prompt = pl.BlockSpec((TT, d), lambda t, *_: (jnp.minimum(t, last_p), 0))
    sample = pl.BlockSpec((TT, d), lambda t, *_: (0, 0))
    grid_spec = pltpu.PrefetchScalarGridSpec(
        num_scalar_prefetch=3,
        grid=(nt,),
        in_specs=[pl.BlockSpec(memory_space=pl.ANY),
                  pl.BlockSpec((TT, TOP_K), lambda t, *_: (t, 0)),
                  prompt, sample,
                  pl.BlockSpec((1, d), lambda t, *_: (0, 0))],
        out_specs=[prompt, sample],
        scratch_shapes=[pltpu.VMEM((2, LROWS, d), F32), pltpu.SemaphoreType.DMA((2,))],
    )
    return pl.pallas_call(
        _combine_kernel,
        grid_spec=grid_spec,
        out_shape=[jax.ShapeDtypeStruct(x1p.shape, F32), jax.ShapeDtypeStruct((TT, d), F32)],
        compiler_params=_cparams(("arbitrary",)),
        name="moe_combine",
    )(*tabs, out_sorted, lpos, x1p, x1s, g.reshape(1, d))


def _t5_bucket(dist):
    max_exact = N_BUCKETS // 2
    dd = dist.astype(F32)
    large = max_exact + (jnp.log(jnp.maximum(dd, 1.0) / max_exact)
                         / math.log(BUCKET_MAX_DIST / max_exact) * (N_BUCKETS - max_exact)).astype(jnp.int32)
    large = jnp.minimum(large, N_BUCKETS - 1)
    return jnp.where(dist < max_exact, dist, large)


def _bias_tables(rel_bias):
    dist = jnp.asarray(np.arange(N_TAPS)[None, :] * np.array(DILATIONS)[:, None], jnp.int32)
    bias = jnp.transpose(rel_bias[_t5_bucket(dist)], (2, 0, 1)).astype(F32)
    by_branch = jnp.transpose(bias, (1, 0, 2))
    g, h = by_branch.shape[:2]
    row = jnp.concatenate([by_branch[:, :, ::-1], jnp.full((g, h, WIN), NEG, F32)], axis=-1)
    flat = jnp.broadcast_to(row[:, :, None, :], (g, h, WIN, 2 * WIN + 1)).reshape(g, h, -1)
    band = flat[:, :, :WIN * 2 * WIN].reshape(g, h, WIN, 2 * WIN)
    return band, by_branch[:, :, :0:-1], by_branch[:, :, 0:1]


def kernel(x_prompt, x_sample, cache_k_win, cache_v_win, state_conv, state_ssm, rel_bias, attn_norm, w_in, conv_w, conv_b, dt_bias, a_log, d_skip, ssd_norm, w_out, ffn_norm, w_router, b_router, w_gate_up, b_gate_up, w_down, b_down, final_norm):
    bp, tp, d = x_prompt.shape
    bs, ts, _ = x_sample.shape
    depth = w_in.shape[0]
    assert depth == 1 and ts == 1 and tp % (max(DILATIONS) * WIN) == 0
    keep = min(max(DILATIONS) * WIN, tp)
    band, samp, samp0 = _bias_tables(rel_bias)
    l = 0

    xp = x_prompt.reshape(bp * tp, d)
    xs = x_sample.reshape(bs * ts, d)
    w_in_bf = jnp.pad(w_in[l], ((0, 0), (0, IN_PROJ_PAD - IN_PROJ))).astype(BF16)
    w_out_bf = w_out[l].astype(BF16)
    wr = jnp.pad(w_router[l], ((0, 0), (0, LANES - N_EXPERTS))).astype(BF16)
    br = jnp.pad(b_router[l], (0, LANES - N_EXPERTS), constant_values=NEG).reshape(1, LANES)

    k, v, z, xbc, dt_raw, *branch_qkv = _inproj_prompt(xp, attn_norm[l], w_in_bf, bp)
    k3 = k.reshape(bp, tp, KV_WIDTH)
    v3 = v.reshape(bp, tp, KV_WIDTH)
    att_parts = [_attn_branch(branch_qkv[2 * gi], branch_qkv[2 * gi + 1], band[gi])
                 for gi in range(len(DILATIONS))]
    ssm, st_p = _ssd_prompt(xbc, dt_raw, z, conv_w[l], conv_b[l], dt_bias[l], a_log[l], d_skip[l],
                            ssd_norm[l], bp)
    x1p, xnp_, lgp = _outproj_prompt(att_parts, ssm, xp, w_out_bf, ffn_norm[l], wr, br)
    k_win_p = k3[:, tp - keep:].reshape(1, bp, keep, N_KV_HEADS, HEAD_DIM)
    v_win_p = v3[:, tp - keep:].reshape(1, bp, keep, N_KV_HEADS, HEAD_DIM)
    conv_p = xbc.reshape(bp, tp, CONV_DIM)[:, tp - (CONV_W - 1):][None]

    q_s, k_s, v_s, z_s, xbc_s, dt_s = _inproj(xs, attn_norm[l], w_in_bf, bs * ts)
    q_s = q_s.reshape(bs, N_ATT_HEADS, HEAD_DIM)
    k_s = k_s.reshape(bs, N_KV_HEADS, HEAD_DIM)
    v_s = v_s.reshape(bs, N_KV_HEADS, HEAD_DIM)
    wbuf = cache_k_win.shape[2]
    att_s = _attn_sample(q_s, k_s, v_s, cache_k_win[l].reshape(bs, wbuf, KV_WIDTH),
                         cache_v_win[l].reshape(bs, wbuf, KV_WIDTH), samp, samp0)
    xa_s, xdt_s, decay_s = _conv_sample(xbc_s, state_conv[l], conv_w[l], conv_b[l], dt_s, dt_bias[l],
                                        a_log[l])
    nbc = N_SSM_GROUPS * D_STATE
    h_s, y_s = _ssm_sample(xdt_s, decay_s, xa_s[:, SSM_WIDTH:SSM_WIDTH + nbc], xa_s[:, SSM_WIDTH + nbc:],
                           state_ssm[l])
    x1s, xns, lgs = _outproj_sample(att_s.reshape(bs, ATT_WIDTH), y_s, xa_s[:, :SSM_WIDTH], z_s,
                                    d_skip[l], ssd_norm[l], xs, w_out_bf, ffn_norm[l], wr, br)
    conv_s = jnp.concatenate([state_conv[l][:, 1:], xbc_s[:, None]], axis=1)[None]

    n_s = bs * ts
    assert (bp * tp) % TT == 0 and n_s <= TT
    m_pad = bp * tp + TT
    logits = jnp.concatenate([lgp[:, :N_EXPERTS], lgs[:, :N_EXPERTS]], axis=0)
    tabs, lpos, gate, groups, n_rows = _route(logits, m_pad)
    by_tile = lambda a: jnp.transpose(a.reshape(m_pad // TT, TT, TOP_K), (0, 2, 1))
    pad_rows = lambda a: jnp.pad(a, ((0, TT - n_s), (0, 0)))
    x_sorted = _dispatch(tabs, xnp_, pad_rows(xns), by_tile(lpos), by_tile(gate), n_rows)
    out_sorted = _moe_ffn(groups, x_sorted, w_gate_up[l], b_gate_up[l], w_down[l], b_down[l])
    y_p, y_s_out = _combine(tabs, out_sorted, lpos, x1p, pad_rows(x1s), final_norm)
    y_s_out = y_s_out[:n_s]

    return (y_p.reshape(bp, tp, d), y_s_out.reshape(bs, ts, d), k_win_p, v_win_p, conv_p, st_p[None],
            k_s.reshape(1, bs, ts, N_KV_HEADS, HEAD_DIM), v_s.reshape(1, bs, ts, N_KV_HEADS, HEAD_DIM),
            conv_s, h_s[None])
```

```python
import functools
import math

import jax
import jax.numpy as jnp
import numpy as np
from jax import lax
from jax.experimental import pallas as pl
from jax.experimental.pallas import tpu as pltpu

F32 = jnp.float32
BF16 = jnp.bfloat16
HIGHEST = lax.Precision.HIGHEST

LANES = 128
SUBLANES = 8
VMEM_LIMIT = 56 * 1024 * 1024

HEAD_DIM = 64
N_ATT_HEADS = 16
N_KV_HEADS = 4
KV_REP = N_ATT_HEADS // N_KV_HEADS
ATT_WIDTH = N_ATT_HEADS * HEAD_DIM
KV_WIDTH = N_KV_HEADS * HEAD_DIM
DILATIONS = (1, 4, 16)
N_TAPS = 129
WIN = N_TAPS - 1
ATT_SCALE = HEAD_DIM ** -0.5
N_BUCKETS = 32
BUCKET_MAX_DIST = 2048
SSM_HEAD_DIM = 64
N_SSM_HEADS = 16
SSM_WIDTH = N_SSM_HEADS * SSM_HEAD_DIM
N_SSM_GROUPS = 2
HEADS_PER_GROUP = N_SSM_HEADS // N_SSM_GROUPS
D_STATE = 128
CONV_W = 4
CONV_DIM = SSM_WIDTH + 2 * N_SSM_GROUPS * D_STATE
SSD_CHUNK = 128
N_EXPERTS = 32
TOP_K = 4
SWIGLU_LIMIT = 7.0
SWIGLU_ALPHA = 1.702
EPS = 1e-5
NEG = -1e30

Q0, K0, V0, Z0, X0, DT0 = 0, 1024, 1280, 1536, 2560, 4096
IN_PROJ = DT0 + N_SSM_HEADS
IN_PROJ_PAD = DT0 + LANES

TM_PROJ = 512
TM_OUT = 256
TM_MOE = 512
TF_MOE = 512
TM_FIN = 256


def _cparams(sem):
    return pltpu.CompilerParams(dimension_semantics=sem, vmem_limit_bytes=VMEM_LIMIT)


def _const_spec(shape):
    nd = len(shape)
    return pl.BlockSpec(shape, lambda *_: (0,) * nd, pipeline_mode=pl.Buffered(1))


def _spread(a, onehot_bf, terms):
    out, rest = None, a
    for _ in range(terms):
        piece = rest.astype(BF16)
        part = jnp.dot(piece, onehot_bf, preferred_element_type=F32)
        out = part if out is None else out + part
        rest = rest - piece.astype(F32)
    return out


def _rms(x, g):
    ms = jnp.mean(x * x, axis=-1, keepdims=True)
    return x * lax.rsqrt(ms + EPS) * g


def _silu(x):
    return x * (1.0 / (1.0 + jnp.exp(-x)))


def _inproj_kernel(x_ref, g_ref, w_ref, q_ref, k_ref, v_ref, z_ref, xbc_ref, dt_ref):
    xn = _rms(x_ref[...], g_ref[...]).astype(BF16)

    def mm(lo, hi):
        return jnp.dot(xn, w_ref[:, lo:hi], preferred_element_type=F32)

    q_ref[...] = (mm(Q0, K0) * ATT_SCALE).astype(BF16)
    k_ref[...] = mm(K0, V0)
    v_ref[...] = mm(V0, Z0)
    z_ref[...] = mm(Z0, X0)
    xbc_ref[...] = mm(X0, DT0)
    dt_ref[...] = mm(DT0, IN_PROJ_PAD)


def _inproj(x2d, g, w_bf, tm):
    m, d = x2d.shape
    widths = (ATT_WIDTH, KV_WIDTH, KV_WIDTH, SSM_WIDTH, CONV_DIM, LANES)
    dtypes = (BF16, F32, F32, F32, F32, F32)
    return pl.pallas_call(
        _inproj_kernel,
        grid=(m // tm,),
        in_specs=[pl.BlockSpec((tm, d), lambda i: (i, 0)),
                  _const_spec((1, d)),
                  _const_spec((d, IN_PROJ_PAD))],
        out_specs=[pl.BlockSpec((tm, w), lambda i: (i, 0)) for w in widths],
        out_shape=[jax.ShapeDtypeStruct((m, w), t) for w, t in zip(widths, dtypes)],
        compiler_params=_cparams(("parallel",)),
        name=f"inproj_{tm}",
    )(x2d, g.reshape(1, d), w_bf)


def _deinterleave_matrix(n, dil):
    p = np.zeros((n, n), np.float32)
    src = np.arange(n)
    p[(src % dil) * (n // dil) + src // dil, src] = 1.0
    return p


def _inproj_prompt_kernel(x_ref, g_ref, w_ref, p_mid_ref, p_far_ref, k_ref, v_ref, z_ref, xbc_ref, dt_ref,
                          q0_ref, kv0_ref, q1_ref, kv1_ref, q2_ref, kv2_ref):
    tm = x_ref.shape[0]
    xn = _rms(x_ref[...], g_ref[...]).astype(BF16)

    def mm(lo, hi):
        return jnp.dot(xn, w_ref[:, lo:hi], preferred_element_type=F32)

    q = (mm(Q0, K0) * ATT_SCALE).astype(BF16)
    k = mm(K0, V0)
    v = mm(V0, Z0)
    k_ref[...] = k
    v_ref[...] = v
    z_ref[...] = mm(Z0, X0)
    xbc_ref[...] = mm(X0, DT0)
    dt_ref[...] = mm(DT0, IN_PROJ_PAD)
    kv = jnp.concatenate([k, v], axis=-1).astype(BF16)
    q0_ref[0, 0] = q
    kv0_ref[0, 0] = kv
    for dil, p_ref, qd_ref, kvd_ref in ((DILATIONS[1], p_mid_ref, q1_ref, kv1_ref),
                                        (DILATIONS[2], p_far_ref, q2_ref, kv2_ref)):
        qp = jnp.dot(p_ref[...], q, preferred_element_type=F32).astype(BF16)
        kvp = jnp.dot(p_ref[...], kv, preferred_element_type=F32).astype(BF16)
        rows = tm // dil
        for r in range(dil):
            qd_ref[0, r] = qp[r * rows:(r + 1) * rows]
            kvd_ref[0, r] = kvp[r * rows:(r + 1) * rows]


def _inproj_prompt(x2d, g, w_bf, batch):
    m, d = x2d.shape
    tm = TM_PROJ
    seq = m // batch
    per_b = seq // tm
    assert DILATIONS[0] == 1 and seq % tm == 0 and tm % (max(DILATIONS) * 2 * SUBLANES) == 0
    widths = (KV_WIDTH, KV_WIDTH, SSM_WIDTH, CONV_DIM, LANES)
    rows = lambda w: pl.BlockSpec((tm, w), lambda i: (i, 0))
    split = lambda dil, w: pl.BlockSpec((1, dil, tm // dil, w), lambda i: (i // per_b, 0, i % per_b, 0))
    branch_specs, branch_shapes = [], []
    for dil in DILATIONS:
        for w in (ATT_WIDTH, 2 * KV_WIDTH):
            branch_specs.append(split(dil, w))
            branch_shapes.append(jax.ShapeDtypeStruct((batch, dil, seq // dil, w), BF16))
    perms = [jnp.asarray(_deinterleave_matrix(tm, dil), BF16) for dil in DILATIONS[1:]]
    return pl.pallas_call(
        _inproj_prompt_kernel,
        grid=(m // tm,),
        in_specs=[rows(d), _const_spec((1, d)), _const_spec((d, IN_PROJ_PAD)),
                  _const_spec((tm, tm)), _const_spec((tm, tm))],
        out_specs=[rows(w) for w in widths] + branch_specs,
        out_shape=[jax.ShapeDtypeStruct((m, w), F32) for w in widths] + branch_shapes,
        compiler_params=_cparams(("parallel",)),
        name="inproj_prompt",
    )(x2d, g.reshape(1, d), w_bf, *perms)


def _attn_kernel(q_ref, kvp_ref, kvc_ref, bias_ref, o_ref, lse_ref):
    first = pl.program_id(2) == 0
    lane = lax.broadcasted_iota(jnp.int32, (1, 2 * WIN), 1)
    prev_mask = jnp.where(jnp.logical_and(first, lane < WIN), NEG, 0.0)
    head_lane = lax.broadcasted_iota(jnp.int32, (WIN, LANES), 1)
    lse_tile = jnp.zeros((WIN, LANES), F32)
    for kvh in range(N_KV_HEADS):
        ks = slice(kvh * HEAD_DIM, (kvh + 1) * HEAD_DIM)
        vs = slice(KV_WIDTH + kvh * HEAD_DIM, KV_WIDTH + (kvh + 1) * HEAD_DIM)
        kw = jnp.concatenate([kvp_ref[0, 0, :, ks], kvc_ref[0, 0, :, ks]], axis=0)
        vw = jnp.concatenate([kvp_ref[0, 0, :, vs], kvc_ref[0, 0, :, vs]], axis=0)
        for pair in range(KV_REP // 2):
            outs = []
            for r in range(2):
                h = kvh * KV_REP + pair * 2 + r
                qh = q_ref[0, 0, :, h * HEAD_DIM:(h + 1) * HEAD_DIM]
                s = lax.dot_general(qh, kw, (((1,), (1,)), ((), ())), preferred_element_type=F32)
                s = s + bias_ref[h] + prev_mask
                mx = jnp.max(s, axis=-1, keepdims=True)
                p = jnp.exp(s - mx)
                l = jnp.sum(p, axis=-1, keepdims=True)
                o = jnp.dot(p.astype(BF16), vw, preferred_element_type=F32)
                outs.append(o * (1.0 / l))
                lse_tile = jnp.where(head_lane == h, mx + jnp.log(l), lse_tile)
            h0 = kvh * KV_REP + pair * 2
            o_ref[0, 0, :, h0 * HEAD_DIM:(h0 + 2) * HEAD_DIM] = jnp.concatenate(outs, axis=-1).astype(BF16)
    lse_ref[0, 0] = lse_tile


def _attn_branch(q, kv, bias_mat):
    b, dil, sub, _ = q.shape
    nb = sub // WIN
    cur = lambda bb, r, i: (bb, r, i, 0)
    prev = lambda bb, r, i: (bb, r, jnp.maximum(i - 1, 0), 0)
    return pl.pallas_call(
        _attn_kernel,
        grid=(b, dil, nb),
        in_specs=[pl.BlockSpec((1, 1, WIN, ATT_WIDTH), cur),
                  pl.BlockSpec((1, 1, WIN, 2 * KV_WIDTH), prev),
                  pl.BlockSpec((1, 1, WIN, 2 * KV_WIDTH), cur),
                  _const_spec((N_ATT_HEADS, WIN, 2 * WIN))],
        out_specs=[pl.BlockSpec((1, 1, WIN, ATT_WIDTH), cur),
                   pl.BlockSpec((1, 1, WIN, LANES), cur)],
        out_shape=[jax.ShapeDtypeStruct((b, dil, sub, ATT_WIDTH), BF16),
                   jax.ShapeDtypeStruct((b, dil, sub, LANES), F32)],
        compiler_params=_cparams(("parallel", "parallel", "arbitrary")),
        name=f"attn_dil{dil}",
    )(q, kv, kv, bias_mat)


def _softplus(x):
    return jnp.maximum(x, 0.0) + jnp.log(1.0 + jnp.exp(-jnp.abs(x)))


def _ssd_kernel(xbc_ref, dt_ref, z_ref, cw_ref, cb_ref, dtb_ref, alog_ref, dskip_ref, gn_ref, e_ref,
                y_ref, st_ref, ext_ref, state_ref):
    c = pl.program_id(1)
    L = SSD_CHUNK

    @pl.when(c == 0)
    def _():
        ext_ref[0:SUBLANES, :] = jnp.zeros((SUBLANES, CONV_DIM), F32)
        state_ref[...] = jnp.zeros_like(state_ref)

    ext_ref[SUBLANES:SUBLANES + L, :] = xbc_ref[...]
    acc = cb_ref[...] + ext_ref[SUBLANES:SUBLANES + L, :] * cw_ref[CONV_W - 1:CONV_W, :]
    for i in range(CONV_W - 1):
        off = SUBLANES - (CONV_W - 1) + i
        acc = acc + ext_ref[off:off + L, :] * cw_ref[i:i + 1, :]
    ext_ref[0:SUBLANES, :] = ext_ref[L:L + SUBLANES, :]
    xa = _silu(acc)

    dt = _softplus(dt_ref[...] + dtb_ref[...])
    da = dt * (-jnp.exp(alog_ref[...]))
    row = lax.broadcasted_iota(jnp.int32, (L, L), 0)
    col = lax.broadcasted_iota(jnp.int32, (L, L), 1)
    tri = row >= col
    a_cs = jnp.dot(tri.astype(F32), da, preferred_element_type=F32, precision=HIGHEST)
    a_cs_t = a_cs.T
    expand = e_ref[...]
    acs_full = _spread(a_cs, expand, 3)
    dt_full = _spread(dt, expand, 3)
    exp_acs = jnp.exp(acs_full)
    a_last = acs_full[L - 1:L, :]
    exp_last = exp_acs[L - 1:L, :]
    xs = xa[:, :SSM_WIDTH]
    xdt = xs * dt_full
    xw = xdt * jnp.exp(a_last - acs_full)

    for g in range(N_SSM_GROUPS):
        b0 = SSM_WIDTH + g * D_STATE
        c0 = SSM_WIDTH + N_SSM_GROUPS * D_STATE + g * D_STATE
        bg_t = xa[:, b0:b0 + D_STATE].T.astype(BF16)
        cg = xa[:, c0:c0 + D_STATE].astype(BF16)
        gram = jnp.dot(cg, bg_t, preferred_element_type=F32)
        for hh in range(HEADS_PER_GROUP):
            h = g * HEADS_PER_GROUP + hh
            hs = slice(h * SSM_HEAD_DIM, (h + 1) * SSM_HEAD_DIM)
            seg = jnp.where(tri, a_cs[:, h:h + 1] - a_cs_t[h:h + 1, :], NEG)
            scores = (gram * jnp.exp(seg)).astype(BF16)
            y_diag = jnp.dot(scores, xdt[:, hs].astype(BF16), preferred_element_type=F32)
            st = state_ref[h]
            y_off = jnp.dot(cg, st.astype(BF16), preferred_element_type=F32) * exp_acs[:, hs]
            y_ref[:, hs] = y_diag + y_off
            state_ref[h] = exp_last[:, hs] * st + jnp.dot(bg_t, xw[:, hs].astype(BF16),
                                                          preferred_element_type=F32)

    y = y_ref[...] + dskip_ref[...] * xs
    u = y * _silu(z_ref[...])
    gw = SSM_WIDTH // N_SSM_GROUPS
    parts = []
    for g in range(N_SSM_GROUPS):
        ug = u[:, g * gw:(g + 1) * gw]
        parts.append(ug * lax.rsqrt(jnp.mean(ug * ug, axis=-1, keepdims=True) + EPS))
    y_ref[...] = jnp.concatenate(parts, axis=-1) * gn_ref[...]

    @pl.when(c == pl.num_programs(1) - 1)
    def _():
        st_ref[0] = state_ref[...]


def _head_expand():
    e = np.zeros((LANES, SSM_WIDTH), np.float32)
    for h in range(N_SSM_HEADS):
        e[h, h * SSM_HEAD_DIM:(h + 1) * SSM_HEAD_DIM] = 1.0
    return jnp.asarray(e)


def _pad_lanes(v):
    return jnp.pad(v.astype(F32), (0, LANES - v.shape[0])).reshape(1, LANES)


def _ssd_prompt(xbc, dt_raw, z, conv_w, conv_b, dt_bias, a_log, d_skip, ssd_norm, batch):
    m = xbc.shape[0]
    nc = m // batch // SSD_CHUNK
    L = SSD_CHUNK
    rows = lambda b, c: (b * nc + c, 0)
    y, st = pl.pallas_call(
        _ssd_kernel,
        grid=(batch, nc),
        in_specs=[pl.BlockSpec((L, CONV_DIM), rows),
                  pl.BlockSpec((L, LANES), rows),
                  pl.BlockSpec((L, SSM_WIDTH), rows),
                  _const_spec((CONV_W, CONV_DIM)),
                  _const_spec((1, CONV_DIM)),
                  _const_spec((1, LANES)),
                  _const_spec((1, LANES)),
                  _const_spec((1, SSM_WIDTH)),
                  _const_spec((1, SSM_WIDTH)),
                  _const_spec((LANES, SSM_WIDTH))],
        out_specs=[pl.BlockSpec((L, SSM_WIDTH), rows),
                   pl.BlockSpec((1, N_SSM_HEADS, D_STATE, SSM_HEAD_DIM), lambda b, c: (b, 0, 0, 0))],
        out_shape=[jax.ShapeDtypeStruct((m, SSM_WIDTH), F32),
                   jax.ShapeDtypeStruct((batch, N_SSM_HEADS, D_STATE, SSM_HEAD_DIM), F32)],
        scratch_shapes=[pltpu.VMEM((SUBLANES + L, CONV_DIM), F32),
                        pltpu.VMEM((N_SSM_HEADS, D_STATE, SSM_HEAD_DIM), F32)],
        compiler_params=_cparams(("parallel", "arbitrary")),
        name="ssd_prompt",
    )(xbc, dt_raw, z, conv_w, conv_b.reshape(1, CONV_DIM), _pad_lanes(dt_bias), _pad_lanes(a_log),
      jnp.repeat(d_skip, SSM_HEAD_DIM).reshape(1, SSM_WIDTH), ssd_norm.reshape(1, SSM_WIDTH),
      _head_expand().astype(BF16))
    return y, jnp.swapaxes(st, -1, -2)


def _router(xn_bf, wr_ref, br_ref):
    return jnp.dot(xn_bf, wr_ref[...], preferred_element_type=F32) + br_ref[...]


def _outproj_kernel(o0, l0, o1, l1, o2, l2, u1_ref, u2_ref, e_ref, ssm_ref, x_ref, w_ref, g_ref, wr_ref,
                    br_ref, x1_ref, xn_ref, lg_ref):
    def natural(o_ref, l_ref, u_ref):
        dil = o_ref.shape[1]
        if dil == 1:
            return o_ref[0, 0].astype(F32), l_ref[0, 0]
        o_cat = jnp.concatenate([o_ref[0, r] for r in range(dil)], axis=0)
        l_cat = jnp.concatenate([l_ref[0, r] for r in range(dil)], axis=0)
        hi = l_cat.astype(BF16)
        lo = (l_cat - hi.astype(F32)).astype(BF16)
        u = u_ref[...]
        return (jnp.dot(u, o_cat, preferred_element_type=F32),
                jnp.dot(u, hi, preferred_element_type=F32) + jnp.dot(u, lo, preferred_element_type=F32))

    branches = [natural(o0, l0, None), natural(o1, l1, u1_ref), natural(o2, l2, u2_ref)]
    lses = [_spread(l, e_ref[...], 2) for _, l in branches]
    mx = functools.reduce(jnp.maximum, lses)
    ws = [jnp.exp(l - mx) for l in lses]
    num = functools.reduce(jnp.add, [w * o for w, (o, _) in zip(ws, branches)])
    att = num * (1.0 / functools.reduce(jnp.add, ws))
    y = jnp.dot(att.astype(BF16), w_ref[:ATT_WIDTH, :], preferred_element_type=F32)
    y = y + jnp.dot(ssm_ref[...].astype(BF16), w_ref[ATT_WIDTH:, :], preferred_element_type=F32)
    x1 = x_ref[...] + y
    x1_ref[...] = x1
    xn = _rms(x1, g_ref[...]).astype(BF16)
    xn_ref[...] = xn
    lg_ref[...] = _router(xn, wr_ref, br_ref)


def _outproj_prompt(att_parts, ssm, x2d, w_bf, g, wr, br):
    m, d = x2d.shape
    mix = w_bf.shape[0]
    tm = TM_OUT
    batch = att_parts[0][0].shape[0]
    per_b = m // batch // tm
    assert tm % (max(DILATIONS) * 2 * SUBLANES) == 0
    row = lambda w: pl.BlockSpec((tm, w), lambda i: (i, 0))
    split = lambda dil, w: pl.BlockSpec((1, dil, tm // dil, w), lambda i: (i // per_b, 0, i % per_b, 0))
    branch_specs = [split(o.shape[1], w) for o, _ in att_parts for w in (ATT_WIDTH, LANES)]
    interleave = [jnp.asarray(_deinterleave_matrix(tm, dil).T, BF16) for dil in DILATIONS[1:]]
    return pl.pallas_call(
        _outproj_kernel,
        grid=(m // tm,),
        in_specs=branch_specs + [_const_spec((tm, tm)), _const_spec((tm, tm)),
                                 _const_spec((LANES, ATT_WIDTH)), row(SSM_WIDTH), row(d),
                                 _const_spec((mix, d)), _const_spec((1, d)),
                                 _const_spec((d, LANES)), _const_spec((1, LANES))],
        out_specs=[row(d), row(d), row(LANES)],
        out_shape=[jax.ShapeDtypeStruct((m, d), F32), jax.ShapeDtypeStruct((m, d), BF16),
                   jax.ShapeDtypeStruct((m, LANES), F32)],
        compiler_params=_cparams(("parallel",)),
        name="outproj_prompt",
    )(*[a for part in att_parts for a in part], *interleave, _head_expand().astype(BF16), ssm, x2d, w_bf,
      g.reshape(1, d), wr, br)


def _attn_sample_kernel(q_ref, kn_ref, vn_ref, kc_ref, vc_ref, bias_ref, bias0_ref, o_ref):
    w = kc_ref.shape[1]
    q = q_ref[0]
    head_grp = lax.broadcasted_iota(jnp.int32, (N_ATT_HEADS, 1), 0) // KV_REP
    kn = kn_ref[0].astype(BF16).astype(F32)
    vn = vn_ref[0].astype(BF16).astype(F32)
    s_self = jnp.sum(q.astype(F32) * kn, axis=-1, keepdims=True)

    def taps(c_ref, dil):
        span = WIN * dil
        rows = c_ref[0, w - span:w, :].astype(BF16)
        if dil == 1:
            return rows
        tap = lax.broadcasted_iota(jnp.int32, (WIN, span), 0)
        col = lax.broadcasted_iota(jnp.int32, (WIN, span), 1)
        pick = jnp.where(col == tap * dil, 1.0, 0.0).astype(BF16)
        return jnp.dot(pick, rows, preferred_element_type=F32).astype(BF16)

    scores, selfs, lses = [], [], []
    for g, dil in enumerate(DILATIONS):
        kk = taps(kc_ref, dil)
        s = jnp.zeros((N_ATT_HEADS, WIN), F32)
        for kvh in range(N_KV_HEADS):
            cs = slice(kvh * HEAD_DIM, (kvh + 1) * HEAD_DIM)
            sk = lax.dot_general(q, kk[:, cs], (((1,), (1,)), ((), ())), preferred_element_type=F32)
            s = jnp.where(head_grp == kvh, sk, s)
        s = s + bias_ref[g]
        s0 = s_self + bias0_ref[g]
        mx = jnp.maximum(jnp.max(s, axis=-1, keepdims=True), s0)
        lse = mx + jnp.log(jnp.sum(jnp.exp(s - mx), axis=-1, keepdims=True) + jnp.exp(s0 - mx))
        scores.append(s); selfs.append(s0); lses.append(lse)
    top = functools.reduce(jnp.maximum, lses)
    es = [jnp.exp(l - top) for l in lses]
    tot = functools.reduce(jnp.add, es)
    o = jnp.zeros((N_ATT_HEADS, HEAD_DIM), F32)
    for s, s0, lse, e, dil in zip(scores, selfs, lses, es, DILATIONS):
        wgt = e / tot
        p = (jnp.exp(s - lse) * wgt).astype(BF16)
        p0 = (jnp.exp(s0 - lse) * wgt).astype(BF16).astype(F32)
        vv = taps(vc_ref, dil)
        o = o + p0 * vn
        for kvh in range(N_KV_HEADS):
            cs = slice(kvh * HEAD_DIM, (kvh + 1) * HEAD_DIM)
            ok = jnp.dot(p, vv[:, cs], preferred_element_type=F32)
            o = o + jnp.where(head_grp == kvh, ok, 0.0)
    o_ref[0] = o


def _attn_sample(q, k_new, v_new, k_cache, v_cache, bias_s, bias0_s):
    n, w = k_cache.shape[0], k_cache.shape[1]
    assert w % (max(DILATIONS) * WIN) == 0
    tok = lambda b: (b, 0, 0)
    head = pl.BlockSpec((1, N_ATT_HEADS, HEAD_DIM), tok)
    window = pl.BlockSpec((1, w, KV_WIDTH), tok)
    return pl.pallas_call(
        _attn_sample_kernel,
        grid=(n,),
        in_specs=[head, head, head, window, window,
                  _const_spec((len(DILATIONS), N_ATT_HEADS, WIN)),
                  _const_spec((len(DILATIONS), N_ATT_HEADS, 1))],
        out_specs=head,
        out_shape=jax.ShapeDtypeStruct((n, N_ATT_HEADS, HEAD_DIM), F32),
        compiler_params=_cparams(("parallel",)),
        name="attn_sample",
    )(q, jnp.repeat(k_new, KV_REP, axis=1), jnp.repeat(v_new, KV_REP, axis=1), k_cache, v_cache,
      bias_s, bias0_s)


def _conv_sample_kernel(xbc_ref, b0_ref, b1_ref, b2_ref, cw_ref, cb_ref, dt_ref, dtb_ref, alog_ref,
                        e_ref, xa_ref, xdt_ref, decay_ref):
    acc = cb_ref[...] + xbc_ref[...] * cw_ref[CONV_W - 1:CONV_W, :]
    for i, buf in enumerate((b0_ref, b1_ref, b2_ref)):
        acc = acc + buf[...] * cw_ref[i:i + 1, :]
    xa = _silu(acc)
    xa_ref[...] = xa
    dt = _softplus(dt_ref[...] + dtb_ref[...])
    decay_ref[...] = jnp.exp(dt * (-jnp.exp(alog_ref[...])))
    dt_full = jnp.dot(dt, e_ref[...], preferred_element_type=F32, precision=HIGHEST)
    xdt_ref[...] = xa[:, :SSM_WIDTH] * dt_full


def _conv_sample(xbc, conv_buf, conv_w, conv_b, dt_raw, dt_bias, a_log):
    n = xbc.shape[0]
    args = (xbc, conv_buf[:, 0], conv_buf[:, 1], conv_buf[:, 2], conv_w, conv_b.reshape(1, CONV_DIM),
            dt_raw, _pad_lanes(dt_bias), _pad_lanes(a_log),
            _head_expand())
    return pl.pallas_call(
        _conv_sample_kernel,
        grid=(1,),
        in_specs=[_const_spec(a.shape) for a in args],
        out_specs=[_const_spec((n, CONV_DIM)), _const_spec((n, SSM_WIDTH)), _const_spec((n, LANES))],
        out_shape=[jax.ShapeDtypeStruct((n, CONV_DIM), F32), jax.ShapeDtypeStruct((n, SSM_WIDTH), F32),
                   jax.ShapeDtypeStruct((n, LANES), F32)],
        compiler_params=_cparams(("arbitrary",)),
        name="conv_sample",
    )(*args)


def _ssm_sample_kernel(xdt_ref, decay_ref, b_ref, c_ref, h0_ref, hn_ref, y_ref):
    for g in range(N_SSM_GROUPS):
        hs = slice(g * HEADS_PER_GROUP, (g + 1) * HEADS_PER_GROUP)
        hn = decay_ref[0, hs] * h0_ref[0, hs] + xdt_ref[0, hs] * b_ref[0, g]
        hn_ref[0, hs] = hn
        c_row = c_ref[0, g].astype(BF16).astype(F32)
        y_ref[0, hs] = jnp.sum(hn.astype(BF16).astype(F32) * c_row, axis=-1, keepdims=True)


def _ssm_sample(xdt, decay, bmat, cmat, h0):
    n = xdt.shape[0]
    p = SSM_HEAD_DIM
    tok4 = lambda b: (b, 0, 0, 0)
    hn, y = pl.pallas_call(
        _ssm_sample_kernel,
        grid=(n,),
        in_specs=[pl.BlockSpec((1, N_SSM_HEADS, p, 1), tok4),
                  pl.BlockSpec((1, N_SSM_HEADS, 1, 1), tok4),
                  pl.BlockSpec((1, N_SSM_GROUPS, 1, D_STATE), tok4),
                  pl.BlockSpec((1, N_SSM_GROUPS, 1, D_STATE), tok4),
                  pl.BlockSpec((1, N_SSM_HEADS, p, D_STATE), tok4)],
        out_specs=[pl.BlockSpec((1, N_SSM_HEADS, p, D_STATE), tok4),
                   pl.BlockSpec((1, N_SSM_HEADS, p, 1), tok4)],
        out_shape=[jax.ShapeDtypeStruct((n, N_SSM_HEADS, p, D_STATE), F32),
                   jax.ShapeDtypeStruct((n, N_SSM_HEADS, p, 1), F32)],
        compiler_params=_cparams(("parallel",)),
        name="ssm_sample",
    )(xdt.reshape(n, N_SSM_HEADS, p, 1), decay[:, :N_SSM_HEADS].reshape(n, N_SSM_HEADS, 1, 1),
      bmat.reshape(n, N_SSM_GROUPS, 1, D_STATE), cmat.reshape(n, N_SSM_GROUPS, 1, D_STATE), h0)
    return hn, y.reshape(n, SSM_WIDTH)


def _outproj_sample_kernel(att_ref, y_ref, xs_ref, z_ref, dskip_ref, gn_ref, x_ref, w_ref, g_ref,
                           wr_ref, br_ref, x1_ref, xn_ref, lg_ref):
    y = y_ref[...] + dskip_ref[...] * xs_ref[...]
    u = y * _silu(z_ref[...])
    gw = SSM_WIDTH // N_SSM_GROUPS
    parts = []
    for g in range(N_SSM_GROUPS):
        ug = u[:, g * gw:(g + 1) * gw]
        parts.append(ug * lax.rsqrt(jnp.mean(ug * ug, axis=-1, keepdims=True) + EPS)
                     * gn_ref[:, g * gw:(g + 1) * gw])
    mix = jnp.concatenate([att_ref[...]] + parts, axis=-1).astype(BF16)
    x1 = x_ref[...] + jnp.dot(mix, w_ref[...], preferred_element_type=F32)
    x1_ref[...] = x1
    xn = _rms(x1, g_ref[...]).astype(BF16)
    xn_ref[...] = xn
    lg_ref[...] = _router(xn, wr_ref, br_ref)


def _outproj_sample(att, y, xs, z, d_skip, ssd_norm, x2d, w_bf, g, wr, br):
    n, d = x2d.shape
    args = (att, y, xs, z, jnp.repeat(d_skip, SSM_HEAD_DIM).reshape(1, SSM_WIDTH),
            ssd_norm.reshape(1, SSM_WIDTH), x2d, w_bf, g.reshape(1, d), wr, br)
    return pl.pallas_call(
        _outproj_sample_kernel,
        grid=(1,),
        in_specs=[_const_spec(a.shape) for a in args],
        out_specs=[_const_spec((n, d)), _const_spec((n, d)), _const_spec((n, LANES))],
        out_shape=[jax.ShapeDtypeStruct((n, d), F32), jax.ShapeDtypeStruct((n, d), BF16),
                   jax.ShapeDtypeStruct((n, LANES), F32)],
        compiler_params=_cparams(("arbitrary",)),
        name="outproj_sample",
    )(*args)


TT = 256
CHUNK_ALIGN = SUBLANES
LROWS = -(-(TT * TOP_K + N_EXPERTS * (CHUNK_ALIGN - 1)) // TT) * TT
CHUNK_SIZES = tuple(1 << b for b in range(TT.bit_length() - 1, CHUNK_ALIGN.bit_length() - 2, -1))
SUB = 256
SUB_TAIL = SUB // 2
RG = 5 * SUB
TF = 256
VMEM_LIMIT_FFN = 60000 * 1024


def _chunk_loop(cnt_ref, loff_ref, dest_ref, tile, fn):
    def per_expert(e, carry):
        idx = tile * N_EXPERTS + e
        n, off, dst = cnt_ref[idx], loff_ref[idx], dest_ref[idx]
        for size in CHUNK_SIZES:
            take = (n & size) != 0

            @pl.when(take)
            def _(off=off, dst=dst, size=size):
                fn(pl.multiple_of(off, CHUNK_ALIGN), pl.multiple_of(dst, CHUNK_ALIGN), size)

            step = jnp.where(take, size, 0)
            off, dst = off + step, dst + step
        return carry

    lax.fori_loop(0, N_EXPERTS, per_expert, 0)


def _dispatch_kernel(cnt_ref, loff_ref, dest_ref, xp_ref, xs_ref, lpos_ref, gate_ref, out_hbm, buf, sem):
    t = pl.program_id(0)
    last = pl.num_programs(0) - 1
    d = xp_ref.shape[1]
    x = jnp.where(t == last, xs_ref[...], xp_ref[...])
    rows = lax.broadcasted_iota(jnp.int32, (LROWS, TT), 0)
    onehot = jnp.zeros((LROWS, TT), F32)
    wcol = jnp.zeros((LROWS, 1), F32)
    for k in range(TOP_K):
        hit = jnp.where(rows == lpos_ref[0, k:k + 1, :], 1.0, 0.0)
        onehot = onehot + hit
        wcol = wcol + jnp.sum(hit * gate_ref[0, k:k + 1, :], axis=-1, keepdims=True)
    tile_buf = buf.at[t % 2]
    tile_buf[:, 0:d] = jnp.dot(onehot.astype(BF16), x, preferred_element_type=F32)
    tile_buf[:, d:d + LANES] = jnp.broadcast_to(wcol, (LROWS, LANES))

    def copy(tile, off, dst, size):
        return pltpu.make_async_copy(buf.at[tile % 2, pl.ds(off, size)], out_hbm.at[pl.ds(dst, size)],
                                     sem.at[tile % 2])

    _chunk_loop(cnt_ref, loff_ref, dest_ref, t, lambda o, g, s: copy(t, o, g, s).start())

    @pl.when(t > 0)
    def _():
        _chunk_loop(cnt_ref, loff_ref, dest_ref, t - 1, lambda o, g, s: copy(t - 1, o, g, s).wait())

    @pl.when(t == last)
    def _():
        _chunk_loop(cnt_ref, loff_ref, dest_ref, t, lambda o, g, s: copy(t, o, g, s).wait())


def _dispatch(tabs, xn_p, xn_s, lpos_t, gate_t, n_rows):
    nt = lpos_t.shape[0]
    d = xn_p.shape[1]
    last_p = xn_p.shape[0] // TT - 1
    grid_spec = pltpu.PrefetchScalarGridSpec(
        num_scalar_prefetch=3,
        grid=(nt,),
        in_specs=[pl.BlockSpec((TT, d), lambda t, *_: (jnp.minimum(t, last_p), 0)),
                  pl.BlockSpec((TT, d), lambda t, *_: (0, 0)),
                  pl.BlockSpec((1, TOP_K, TT), lambda t, *_: (t, 0, 0)),
                  pl.BlockSpec((1, TOP_K, TT), lambda t, *_: (t, 0, 0))],
        out_specs=pl.BlockSpec(memory_space=pl.ANY),
        scratch_shapes=[pltpu.VMEM((2, LROWS, d + LANES), F32), pltpu.SemaphoreType.DMA((2,))],
    )
    return pl.pallas_call(
        _dispatch_kernel,
        grid_spec=grid_spec,
        out_shape=jax.ShapeDtypeStruct((n_rows, d + LANES), F32),
        compiler_params=_cparams(("arbitrary",)),
        name="moe_dispatch",
    )(*tabs, xn_p, xn_s, lpos_t, gate_t)


def _ffn_kernel(ge_ref, gs_ref, gn_ref, gt_ref, ng_ref, xs_hbm, wgu_hbm, wd_hbm, bgu_ref, bdn_ref, out_hbm,
                xbuf, acc, ostage, wg_st, wu_st, wd_st, wg_bf, wu_bf, wd_bf, sem_w, sem_x, sem_o):
    d = acc.shape[1]
    d_ff = wd_hbm.shape[1]
    nf = d_ff // TF
    n_groups = ng_ref[0]
    total = n_groups * nf

    def w_copies(s, slot):
        g = s // nf
        f = s - g * nf
        e = ge_ref[g]
        c0 = pl.multiple_of(f * TF, TF)
        return (pltpu.make_async_copy(wgu_hbm.at[e, :, pl.ds(c0, TF)], wg_st.at[slot], sem_w.at[slot, 0]),
                pltpu.make_async_copy(wgu_hbm.at[e, :, pl.ds(d_ff + c0, TF)], wu_st.at[slot], sem_w.at[slot, 1]),
                pltpu.make_async_copy(wd_hbm.at[e, pl.ds(c0, TF), :], wd_st.at[slot], sem_w.at[slot, 2]))

    def x_copy(g, j, size=SUB):
        r0 = pl.multiple_of(j * SUB, SUB)
        return pltpu.make_async_copy(xs_hbm.at[pl.ds(pl.multiple_of(gs_ref[g] + r0, CHUNK_ALIGN), size)],
                                     xbuf.at[g % 2, pl.ds(r0, size)], sem_x)

    def o_copy(g, j, size=SUB):
        r0 = pl.multiple_of(j * SUB, SUB)
        return pltpu.make_async_copy(ostage.at[j % 2, pl.ds(0, size)],
                                     out_hbm.at[pl.ds(pl.multiple_of(gs_ref[g] + r0, CHUNK_ALIGN), size)],
                                     sem_o.at[j % 2])

    def loop(n, fn):
        lax.fori_loop(0, n, lambda j, c: (fn(j), c)[1], 0)

    def group_rows(g, op):
        loop(gn_ref[g], lambda j: op(x_copy(g, j)))

        @pl.when(gt_ref[g] == 1)
        def _():
            op(x_copy(g, gn_ref[g], SUB_TAIL))

    def drain_stores(g):
        n = gn_ref[g]
        tail = gt_ref[g]

        @pl.when(tail == 1)
        def _():
            o_copy(g, n, SUB_TAIL).wait()

        @pl.when(n >= 1)
        def _():
            o_copy(g, n - 1).wait()

        @pl.when(jnp.logical_and(n >= 2, tail == 0))
        def _():
            o_copy(g, n - 2).wait()

    @pl.when(total > 0)
    def _():
        for c in w_copies(0, 0):
            c.start()
        group_rows(0, lambda c: c.start())

    def item(s, carry):
        slot = s % 2
        g = s // nf
        f = s - g * nf
        e = ge_ref[g]
        nsub = gn_ref[g]
        xg = xbuf.at[g % 2]

        @pl.when(s + 1 < total)
        def _():
            for c in w_copies(s + 1, 1 - slot):
                c.start()

        @pl.when(f == 0)
        def _():
            group_rows(g, lambda c: c.wait())

        @pl.when(jnp.logical_and(f == 1, g + 1 < n_groups))
        def _():
            group_rows(g + 1, lambda c: c.start())

        for c in w_copies(s, slot):
            c.wait()
        bg = bgu_ref[pl.ds(e * 2 * nf + f, 1), :]
        bu = bgu_ref[pl.ds(e * 2 * nf + nf + f, 1), :]

        def sub_block(j, phase, cast=False, size=SUB):
            rs = pl.ds(pl.multiple_of(j * SUB, SUB), size)
            x = xg[rs, 0:d].astype(BF16)
            if cast:
                wg, wu, wd = (st[slot].astype(BF16) for st in (wg_st, wu_st, wd_st))
                wg_bf[...], wu_bf[...], wd_bf[...] = wg, wu, wd
            else:
                wg, wu, wd = wg_bf[...], wu_bf[...], wd_bf[...]
            hg = jnp.dot(x, wg, preferred_element_type=F32) + bg
            hu = jnp.dot(x, wu, preferred_element_type=F32) + bu
            gg = jnp.minimum(hg, SWIGLU_LIMIT)
            uu = jnp.clip(hu, -SWIGLU_LIMIT, SWIGLU_LIMIT)
            act = gg * (1.0 / (1.0 + jnp.exp(-SWIGLU_ALPHA * gg))) * (uu + 1.0)
            part = jnp.dot(act.astype(BF16), wd, preferred_element_type=F32)
            if phase == "first":
                acc[rs, :] = part + bdn_ref[pl.ds(e, 1), :]
            elif phase == "middle":
                acc[rs, :] += part
            else:
                @pl.when(j >= 2)
                def _():
                    o_copy(g, j - 2).wait()

                ostage[j % 2, 0:size] = (acc[rs, :] + part) * xg[rs, d:d + 1]
                o_copy(g, j, size).start()

        tail = gt_ref[g] == 1

        def storing_blocks():
            sub_block(jnp.int32(0), "last", cast=True)

            def pair(p):
                sub_block(2 * p + 1, "last")
                sub_block(2 * p + 2, "last")

            loop((nsub - 1) // 2, pair)

            @pl.when(nsub % 2 == 0)
            def _():
                sub_block(nsub - 1, "last")

            @pl.when(tail)
            def _():
                sub_block(nsub, "last", size=SUB_TAIL)

        def accumulating_blocks(phase):
            units_here = 2 * nsub + gt_ref[g]
            for units in range(1, RG // SUB_TAIL + 1):
                @pl.when(units_here == units)
                def _(units=units):
                    sub_block(jnp.int32(0), phase, cast=True, size=units * SUB_TAIL)

        @pl.when(f == 0)
        def _():
            accumulating_blocks("first")

        @pl.when(jnp.logical_and(f > 0, f < nf - 1))
        def _():
            accumulating_blocks("middle")

        @pl.when(f == nf - 1)
        def _():
            @pl.when(g > 0)
            def _():
                drain_stores(g - 1)

            storing_blocks()

        return carry

    lax.fori_loop(0, total, item, 0)

    @pl.when(total > 0)
    def _():
        drain_stores(n_groups - 1)


def _moe_ffn(groups, x_sorted, w_gate_up, b_gu, w_down, b_dn):
    n_rows = x_sorted.shape[0]
    _, d_ff, d = w_down.shape
    nf = d_ff // TF
    assert nf >= 2
    bgu2 = b_gu.reshape(N_EXPERTS * 2 * nf, TF)
    grid_spec = pltpu.PrefetchScalarGridSpec(
        num_scalar_prefetch=5,
        grid=(1,),
        in_specs=[pl.BlockSpec(memory_space=pl.ANY), pl.BlockSpec(memory_space=pl.ANY),
                  pl.BlockSpec(memory_space=pl.ANY),
                  pl.BlockSpec(bgu2.shape, lambda i, *_: (0, 0), pipeline_mode=pl.Buffered(1)),
                  pl.BlockSpec(b_dn.shape, lambda i, *_: (0, 0), pipeline_mode=pl.Buffered(1))],
        out_specs=pl.BlockSpec(memory_space=pl.ANY),
        scratch_shapes=[pltpu.VMEM((2, RG, d + LANES), F32), pltpu.VMEM((RG, d), F32),
                        pltpu.VMEM((2, SUB, d), F32),
                        pltpu.VMEM((2, d, TF), F32), pltpu.VMEM((2, d, TF), F32), pltpu.VMEM((2, TF, d), F32),
                        pltpu.VMEM((d, TF), BF16), pltpu.VMEM((d, TF), BF16), pltpu.VMEM((TF, d), BF16),
                        pltpu.SemaphoreType.DMA((2, 3)), pltpu.SemaphoreType.DMA(()),
                        pltpu.SemaphoreType.DMA((2,))],
    )
    return pl.pallas_call(
        _ffn_kernel,
        grid_spec=grid_spec,
        out_shape=jax.ShapeDtypeStruct((n_rows, d), F32),
        compiler_params=pltpu.CompilerParams(dimension_semantics=("arbitrary",),
                                             vmem_limit_bytes=VMEM_LIMIT_FFN),
        name="moe_ffn",
    )(*groups, x_sorted, w_gate_up, w_down, bgu2, b_dn)


def _route(logits, m_pad):
    m = logits.shape[0]
    nt = m_pad // TT
    top_v, top_i = lax.top_k(logits, TOP_K)
    gate = jnp.pad(jax.nn.softmax(top_v, axis=-1), ((0, m_pad - m), (0, 0)))
    top_i = jnp.pad(top_i.astype(jnp.int32), ((0, m_pad - m), (0, 0)), constant_values=-1)
    chosen = (top_i[:, :, None] == jnp.arange(N_EXPERTS, dtype=jnp.int32)).astype(jnp.int32)
    tiles = chosen.sum(axis=1).reshape(nt, TT, N_EXPERTS)
    cnt = (tiles.sum(axis=1) + CHUNK_ALIGN - 1) // CHUNK_ALIGN * CHUNK_ALIGN
    loff = jnp.cumsum(cnt, axis=1) - cnt
    seg = cnt.sum(axis=0)
    seg_start = jnp.cumsum(seg) - seg
    dest = seg_start[None, :] + jnp.cumsum(cnt, axis=0) - cnt
    rank = jnp.cumsum(tiles, axis=1) - tiles
    lpos_all = (loff[:, None, :] + rank).reshape(m_pad, N_EXPERTS)
    lpos = (lpos_all[:, None, :] * chosen).sum(axis=-1)
    lpos = jnp.where(top_i >= 0, lpos, -1).astype(jnp.int32)
    n_rows_bound = nt * LROWS
    n_grp_max = n_rows_bound // RG + N_EXPERTS
    grp = (seg + RG - 1) // RG
    grp_end = jnp.cumsum(grp)
    gi = jnp.arange(n_grp_max, dtype=jnp.int32)
    g_exp = jnp.minimum((gi[:, None] >= grp_end[None, :]).sum(axis=1), N_EXPERTS - 1).astype(jnp.int32)
    within = gi - (grp_end[g_exp] - grp[g_exp])
    g_start = (seg_start[g_exp] + within * RG).astype(jnp.int32)
    g_rows = jnp.clip(seg[g_exp] - within * RG, 0, RG)
    g_rows = jnp.where(gi < grp_end[-1], g_rows, 0)
    n_full = g_rows // SUB
    rest = g_rows - n_full * SUB
    g_tail = (rest > 0) & (rest <= SUB_TAIL) & (n_full >= 1)
    g_nsub = n_full + ((rest > 0) & ~g_tail)
    groups = (g_exp, g_start, g_nsub.astype(jnp.int32), g_tail.astype(jnp.int32),
              grp_end[-1:].astype(jnp.int32))
    tabs = tuple(a.reshape(-1).astype(jnp.int32) for a in (cnt, loff, dest))
    return tabs, lpos, gate, groups, n_rows_bound + SUB


def _combine_kernel(cnt_ref, loff_ref, dest_ref, src_hbm, lpos_ref, x1p_ref, x1s_ref, g_ref,
                    yp_ref, ys_ref, buf, sem):
    t = pl.program_id(0)
    last = pl.num_programs(0) - 1

    def copy(tile, off, src, size):
        return pltpu.make_async_copy(src_hbm.at[pl.ds(src, size)], buf.at[tile % 2, pl.ds(off, size)],
                                     sem.at[tile % 2])

    @pl.when(t == 0)
    def _():
        buf[...] = jnp.zeros_like(buf)
        _chunk_loop(cnt_ref, loff_ref, dest_ref, t, lambda o, g, s: copy(t, o, g, s).start())

    @pl.when(t < last)
    def _():
        _chunk_loop(cnt_ref, loff_ref, dest_ref, t + 1, lambda o, g, s: copy(t + 1, o, g, s).start())

    _chunk_loop(cnt_ref, loff_ref, dest_ref, t, lambda o, g, s: copy(t, o, g, s).wait())
    cols = lax.broadcasted_iota(jnp.int32, (TT, LROWS), 1)
    sel = jnp.zeros((TT, LROWS), F32)
    for k in range(TOP_K):
        sel = sel + jnp.where(cols == lpos_ref[:, k:k + 1], 1.0, 0.0)
    f = jnp.dot(sel.astype(BF16), buf[t % 2].astype(BF16), preferred_element_type=F32)

    @pl.when(t < last)
    def _():
        yp_ref[...] = _rms(x1p_ref[...] + f, g_ref[...])

    @pl.when(t == last)
    def _():
        ys_ref[...] = _rms(x1s_ref[...] + f, g_ref[...])


def _combine(tabs, out_sorted, lpos, x1p, x1s, g):
    d = x1p.shape[1]
    nt = lpos.shape[0] // TT
    last_p = x1p.shape[0] // TT - 1
    prompt = pl.BlockSpec((TT, d), lambda t, *_: (jnp.minimum(t, last_p), 0))
    sample = pl.BlockSpec((TT, d), lambda t, *_: (0, 0))
    grid_spec = pltpu.PrefetchScalarGridSpec(
        num_scalar_prefetch=3,
        grid=(nt,),
        in_specs=[pl.BlockSpec(memory_space=pl.ANY),
                  pl.BlockSpec((TT, TOP_K), lambda t, *_: (t, 0)),
                  prompt, sample,
                  pl.BlockSpec((1, d), lambda t, *_: (0, 0))],
        out_specs=[prompt, sample],
        scratch_shapes=[pltpu.VMEM((2, LROWS, d), F32), pltpu.SemaphoreType.DMA((2,))],
    )
    return pl.pallas_call(
        _combine_kernel,
        grid_spec=grid_spec,
        out_shape=[jax.ShapeDtypeStruct(x1p.shape, F32), jax.ShapeDtypeStruct((TT, d), F32)],
        compiler_params=_cparams(("arbitrary",)),
        name="moe_combine",
    )(*tabs, out_sorted, lpos, x1p, x1s, g.reshape(1, d))


def _t5_bucket(dist):
    max_exact = N_BUCKETS // 2
    dd = dist.astype(F32)
    large = max_exact + (jnp.log(jnp.maximum(dd, 1.0) / max_exact)
                         / math.log(BUCKET_MAX_DIST / max_exact) * (N_BUCKETS - max_exact)).astype(jnp.int32)
    large = jnp.minimum(large, N_BUCKETS - 1)
    return jnp.where(dist < max_exact, dist, large)


def _bias_tables(rel_bias):
    dist = jnp.asarray(np.arange(N_TAPS)[None, :] * np.array(DILATIONS)[:, None], jnp.int32)
    bias = jnp.transpose(rel_bias[_t5_bucket(dist)], (2, 0, 1)).astype(F32)
    by_branch = jnp.transpose(bias, (1, 0, 2))
    g, h = by_branch.shape[:2]
    row = jnp.concatenate([by_branch[:, :, ::-1], jnp.full((g, h, WIN), NEG, F32)], axis=-1)
    flat = jnp.broadcast_to(row[:, :, None, :], (g, h, WIN, 2 * WIN + 1)).reshape(g, h, -1)
    band = flat[:, :, :WIN * 2 * WIN].reshape(g, h, WIN, 2 * WIN)
    return band, by_branch[:, :, :0:-1], by_branch[:, :, 0:1]


def kernel(x_prompt, x_sample, cache_k_win, cache_v_win, state_conv, state_ssm, rel_bias, attn_norm, w_in, conv_w, conv_b, dt_bias, a_log, d_skip, ssd_norm, w_out, ffn_norm, w_router, b_router, w_gate_up, b_gate_up, w_down, b_down, final_norm):
    bp, tp, d = x_prompt.shape
    bs, ts, _ = x_sample.shape
    depth = w_in.shape[0]
    assert depth == 1 and ts == 1 and tp % (max(DILATIONS) * WIN) == 0
    keep = min(max(DILATIONS) * WIN, tp)
    band, samp, samp0 = _bias_tables(rel_bias)
    l = 0

    xp = x_prompt.reshape(bp * tp, d)
    xs = x_sample.reshape(bs * ts, d)
    w_in_bf = jnp.pad(w_in[l], ((0, 0), (0, IN_PROJ_PAD - IN_PROJ))).astype(BF16)
    w_out_bf = w_out[l].astype(BF16)
    wr = jnp.pad(w_router[l], ((0, 0), (0, LANES - N_EXPERTS))).astype(BF16)
    br = jnp.pad(b_router[l], (0, LANES - N_EXPERTS), constant_values=NEG).reshape(1, LANES)

    k, v, z, xbc, dt_raw, *branch_qkv = _inproj_prompt(xp, attn_norm[l], w_in_bf, bp)
    k3 = k.reshape(bp, tp, KV_WIDTH)
    v3 = v.reshape(bp, tp, KV_WIDTH)
    att_parts = [_attn_branch(branch_qkv[2 * gi], branch_qkv[2 * gi + 1], band[gi])
                 for gi in range(len(DILATIONS))]
    ssm, st_p = _ssd_prompt(xbc, dt_raw, z, conv_w[l], conv_b[l], dt_bias[l], a_log[l], d_skip[l],
                            ssd_norm[l], bp)
    x1p, xnp_, lgp = _outproj_prompt(att_parts, ssm, xp, w_out_bf, ffn_norm[l], wr, br)
    k_win_p = k3[:, tp - keep:].reshape(1, bp, keep, N_KV_HEADS, HEAD_DIM)
    v_win_p = v3[:, tp - keep:].reshape(1, bp, keep, N_KV_HEADS, HEAD_DIM)
    conv_p = xbc.reshape(bp, tp, CONV_DIM)[:, tp - (CONV_W - 1):][None]

    q_s, k_s, v_s, z_s, xbc_s, dt_s = _inproj(xs, attn_norm[l], w_in_bf, bs * ts)
    q_s = q_s.reshape(bs, N_ATT_HEADS, HEAD_DIM)
    k_s = k_s.reshape(bs, N_KV_HEADS, HEAD_DIM)
    v_s = v_s.reshape(bs, N_KV_HEADS, HEAD_DIM)
    wbuf = cache_k_win.shape[2]
    att_s = _attn_sample(q_s, k_s, v_s, cache_k_win[l].reshape(bs, wbuf, KV_WIDTH),
                         cache_v_win[l].reshape(bs, wbuf, KV_WIDTH), samp, samp0)
    xa_s, xdt_s, decay_s = _conv_sample(xbc_s, state_conv[l], conv_w[l], conv_b[l], dt_s, dt_bias[l],
                                        a_log[l])
    nbc = N_SSM_GROUPS * D_STATE
    h_s, y_s = _ssm_sample(xdt_s, decay_s, xa_s[:, SSM_WIDTH:SSM_WIDTH + nbc], xa_s[:, SSM_WIDTH + nbc:],
                           state_ssm[l])
    x1s, xns, lgs = _outproj_sample(att_s.reshape(bs, ATT_WIDTH), y_s, xa_s[:, :SSM_WIDTH], z_s,
                                    d_skip[l], ssd_norm[l], xs, w_out_bf, ffn_norm[l], wr, br)
    conv_s = jnp.concatenate([state_conv[l][:, 1:], xbc_s[:, None]], axis=1)[None]

    n_s = bs * ts
    assert (bp * tp) % TT == 0 and n_s <= TT
    m_pad = bp * tp + TT
    logits = jnp.concatenate([lgp[:, :N_EXPERTS], lgs[:, :N_EXPERTS]], axis=0)
    tabs, lpos, gate, groups, n_rows = _route(logits, m_pad)
    by_tile = lambda a: jnp.transpose(a.reshape(m_pad // TT, TT, TOP_K), (0, 2, 1))
    pad_rows = lambda a: jnp.pad(a, ((0, TT - n_s), (0, 0)))
    x_sorted = _dispatch(tabs, xnp_, pad_rows(xns), by_tile(lpos), by_tile(gate), n_rows)
    out_sorted = _moe_ffn(groups, x_sorted, w_gate_up[l], b_gate_up[l], w_down[l], b_down[l])
    y_p, y_s_out = _combine(tabs, out_sorted, lpos, x1p, pad_rows(x1s), final_norm)
    y_s_out = y_s_out[:n_s]

    return (y_p.reshape(bp, tp, d), y_s_out.reshape(bs, ts, d), k_win_p, v_win_p, conv_p, st_p[None],
            k_s.reshape(1, bs, ts, N_KV_HEADS, HEAD_DIM), v_s.reshape(1, bs, ts, N_KV_HEADS, HEAD_DIM),
            conv_s, h_s[None])
```

```python
import functools
import math

import jax
import jax.numpy as jnp
import numpy as np
from jax import lax
from jax.experimental import pallas as pl
from jax.experimental.pallas import tpu as pltpu

F32 = jnp.float32
BF16 = jnp.bfloat16
HIGHEST = lax.Precision.HIGHEST

LANES = 128
SUBLANES = 8
VMEM_LIMIT = 56 * 1024 * 1024

HEAD_DIM = 64
N_ATT_HEADS = 16
N_KV_HEADS = 4
KV_REP = N_ATT_HEADS // N_KV_HEADS
ATT_WIDTH = N_ATT_HEADS * HEAD_DIM
KV_WIDTH = N_KV_HEADS * HEAD_DIM
DILATIONS = (1, 4, 16)
N_TAPS = 129
WIN = N_TAPS - 1
ATT_SCALE = HEAD_DIM ** -0.5
N_BUCKETS = 32
BUCKET_MAX_DIST = 2048
SSM_HEAD_DIM = 64
N_SSM_HEADS = 16
SSM_WIDTH = N_SSM_HEADS * SSM_HEAD_DIM
N_SSM_GROUPS = 2
HEADS_PER_GROUP = N_SSM_HEADS // N_SSM_GROUPS
D_STATE = 128
CONV_W = 4
CONV_DIM = SSM_WIDTH + 2 * N_SSM_GROUPS * D_STATE
SSD_CHUNK = 128
N_EXPERTS = 32
TOP_K = 4
SWIGLU_LIMIT = 7.0
SWIGLU_ALPHA = 1.702
EPS = 1e-5
NEG = -1e30

Q0, K0, V0, Z0, X0, DT0 = 0, 1024, 1280, 1536, 2560, 4096
IN_PROJ = DT0 + N_SSM_HEADS
IN_PROJ_PAD = DT0 + LANES

TM_PROJ = 512
TM_OUT = 256
TQ_ATT = 2 * WIN


def _cparams(sem):
    return pltpu.CompilerParams(dimension_semantics=sem, vmem_limit_bytes=VMEM_LIMIT)


def _const_spec(shape):
    nd = len(shape)
    return pl.BlockSpec(shape, lambda *_: (0,) * nd, pipeline_mode=pl.Buffered(1))


def _spread(a, onehot_bf, terms):
    out, rest = None, a
    for _ in range(terms):
        piece = rest.astype(BF16)
        part = jnp.dot(piece, onehot_bf, preferred_element_type=F32)
        out = part if out is None else out + part
        rest = rest - piece.astype(F32)
    return out


def _rms(x, g):
    ms = jnp.mean(x * x, axis=-1, keepdims=True)
    return x * lax.rsqrt(ms + EPS) * g


def _silu(x):
    return x * (1.0 / (1.0 + jnp.exp(-x)))


def _inproj_kernel(x_ref, g_ref, w_ref, q_ref, k_ref, v_ref, z_ref, xbc_ref, dt_ref):
    xn = _rms(x_ref[...], g_ref[...]).astype(BF16)

    def mm(lo, hi):
        return jnp.dot(xn, w_ref[:, lo:hi], preferred_element_type=F32)

    q_ref[...] = (mm(Q0, K0) * ATT_SCALE).astype(BF16)
    k_ref[...] = mm(K0, V0)
    v_ref[...] = mm(V0, Z0)
    z_ref[...] = mm(Z0, X0)
    xbc_ref[...] = mm(X0, DT0)
    dt_ref[...] = mm(DT0, IN_PROJ_PAD)


def _inproj(x2d, g, w_bf, tm):
    m, d = x2d.shape
    widths = (ATT_WIDTH, KV_WIDTH, KV_WIDTH, SSM_WIDTH, CONV_DIM, LANES)
    dtypes = (BF16, F32, F32, F32, F32, F32)
    return pl.pallas_call(
        _inproj_kernel,
        grid=(m // tm,),
        in_specs=[pl.BlockSpec((tm, d), lambda i: (i, 0)),
                  _const_spec((1, d)),
                  _const_spec((d, IN_PROJ_PAD))],
        out_specs=[pl.BlockSpec((tm, w), lambda i: (i, 0)) for w in widths],
        out_shape=[jax.ShapeDtypeStruct((m, w), t) for w, t in zip(widths, dtypes)],
        compiler_params=_cparams(("parallel",)),
        name=f"inproj_{tm}",
    )(x2d, g.reshape(1, d), w_bf)


def _deinterleave_matrix(n, dil):
    p = np.zeros((n, n), np.float32)
    src = np.arange(n)
    p[(src % dil) * (n // dil) + src // dil, src] = 1.0
    return p


def _inproj_prompt_kernel(x_ref, g_ref, w_ref, p_mid_ref, p_far_ref, k_ref, v_ref, z_ref, xbc_ref, dt_ref,
                          q0_ref, kv0_ref, q1_ref, kv1_ref, q2_ref, kv2_ref):
    tm = x_ref.shape[0]
    xn = _rms(x_ref[...], g_ref[...]).astype(BF16)

    def mm(lo, hi):
        return jnp.dot(xn, w_ref[:, lo:hi], preferred_element_type=F32)

    q = (mm(Q0, K0) * ATT_SCALE).astype(BF16)
    k = mm(K0, V0)
    v = mm(V0, Z0)
    k_ref[...] = k
    v_ref[...] = v
    z_ref[...] = mm(Z0, X0)
    xbc_ref[...] = mm(X0, DT0)
    dt_ref[...] = mm(DT0, IN_PROJ_PAD)
    kv = jnp.concatenate([k, v], axis=-1).astype(BF16)
    q0_ref[0, 0] = q
    kv0_ref[0, 0] = kv
    for dil, p_ref, qd_ref, kvd_ref in ((DILATIONS[1], p_mid_ref, q1_ref, kv1_ref),
                                        (DILATIONS[2], p_far_ref, q2_ref, kv2_ref)):
        qp = jnp.dot(p_ref[...], q, preferred_element_type=F32).astype(BF16)
        kvp = jnp.dot(p_ref[...], kv, preferred_element_type=F32).astype(BF16)
        rows = tm // dil
        for r in range(dil):
            qd_ref[0, r] = qp[r * rows:(r + 1) * rows]
            kvd_ref[0, r] = kvp[r * rows:(r + 1) * rows]


def _inproj_prompt(x2d, g, w_bf, batch):
    m, d = x2d.shape
    tm = TM_PROJ
    seq = m // batch
    per_b = seq // tm
    assert DILATIONS[0] == 1 and seq % tm == 0 and tm % (max(DILATIONS) * 2 * SUBLANES) == 0
    widths = (KV_WIDTH, KV_WIDTH, SSM_WIDTH, CONV_DIM, LANES)
    rows = lambda w: pl.BlockSpec((tm, w), lambda i: (i, 0))
    split = lambda dil, w: pl.BlockSpec((1, dil, tm // dil, w), lambda i: (i // per_b, 0, i % per_b, 0))
    branch_specs, branch_shapes = [], []
    for dil in DILATIONS:
        for w in (ATT_WIDTH, 2 * KV_WIDTH):
            branch_specs.append(split(dil, w))
            branch_shapes.append(jax.ShapeDtypeStruct((batch, dil, seq // dil, w), BF16))
    perms = [jnp.asarray(_deinterleave_matrix(tm, dil), BF16) for dil in DILATIONS[1:]]
    return pl.pallas_call(
        _inproj_prompt_kernel,
        grid=(m // tm,),
        in_specs=[rows(d), _const_spec((1, d)), _const_spec((d, IN_PROJ_PAD)),
                  _const_spec((tm, tm)), _const_spec((tm, tm))],
        out_specs=[rows(w) for w in widths] + branch_specs,
        out_shape=[jax.ShapeDtypeStruct((m, w), F32) for w in widths] + branch_shapes,
        compiler_params=_cparams(("parallel",)),
        name="inproj_prompt",
    )(x2d, g.reshape(1, d), w_bf, *perms)


def _attn_kernel(q_ref, kvp_ref, kvc_ref, bias_ref, o_ref, lse_ref):
    first = pl.program_id(2) == 0
    lane = lax.broadcasted_iota(jnp.int32, (1, 2 * WIN), 1)
    prev_mask = jnp.where(jnp.logical_and(first, lane < WIN), NEG, 0.0)
    head_lane = lax.broadcasted_iota(jnp.int32, (WIN, LANES), 1)
    for blk in range(TQ_ATT // WIN):
        rows = slice(blk * WIN, (blk + 1) * WIN)
        lse_tile = jnp.zeros((WIN, LANES), F32)
        for kvh in range(N_KV_HEADS):
            ks = slice(kvh * HEAD_DIM, (kvh + 1) * HEAD_DIM)
            vs = slice(KV_WIDTH + kvh * HEAD_DIM, KV_WIDTH + (kvh + 1) * HEAD_DIM)
            if blk == 0:
                kw = jnp.concatenate([kvp_ref[0, 0, :, ks], kvc_ref[0, 0, 0:WIN, ks]], axis=0)
                vw = jnp.concatenate([kvp_ref[0, 0, :, vs], kvc_ref[0, 0, 0:WIN, vs]], axis=0)
            else:
                kw = kvc_ref[0, 0, (blk - 1) * WIN:(blk + 1) * WIN, ks]
                vw = kvc_ref[0, 0, (blk - 1) * WIN:(blk + 1) * WIN, vs]
            for pair in range(KV_REP // 2):
                outs = []
                for r in range(2):
                    h = kvh * KV_REP + pair * 2 + r
                    qh = q_ref[0, 0, rows, h * HEAD_DIM:(h + 1) * HEAD_DIM]
                    s = lax.dot_general(qh, kw, (((1,), (1,)), ((), ())), preferred_element_type=F32)
                    s = s + bias_ref[h]
                    if blk == 0:
                        s = s + prev_mask
                    mx = jnp.max(s, axis=-1, keepdims=True)
                    p = jnp.exp(s - mx)
                    l = jnp.sum(p, axis=-1, keepdims=True)
                    o = jnp.dot(p.astype(BF16), vw, preferred_element_type=F32)
                    outs.append(o * (1.0 / l))
                    lse_tile = jnp.where(head_lane == h, mx + jnp.log(l), lse_tile)
                h0 = kvh * KV_REP + pair * 2
                o_ref[0, 0, rows, h0 * HEAD_DIM:(h0 + 2) * HEAD_DIM] = (
                    jnp.concatenate(outs, axis=-1).astype(BF16))
        lse_ref[0, 0, rows, :] = lse_tile


def _attn_branch(q, kv, bias_mat):
    b, dil, sub, _ = q.shape
    assert sub % TQ_ATT == 0
    nb = sub // TQ_ATT
    per_step = TQ_ATT // WIN
    cur = lambda bb, r, i: (bb, r, i, 0)
    prev = lambda bb, r, i: (bb, r, jnp.maximum(i * per_step - 1, 0), 0)
    return pl.pallas_call(
        _attn_kernel,
        grid=(b, dil, nb),
        in_specs=[pl.BlockSpec((1, 1, TQ_ATT, ATT_WIDTH), cur),
                  pl.BlockSpec((1, 1, WIN, 2 * KV_WIDTH), prev),
                  pl.BlockSpec((1, 1, TQ_ATT, 2 * KV_WIDTH), cur),
                  _const_spec((N_ATT_HEADS, WIN, 2 * WIN))],
        out_specs=[pl.BlockSpec((1, 1, TQ_ATT, ATT_WIDTH), cur),
                   pl.BlockSpec((1, 1, TQ_ATT, LANES), cur)],
        out_shape=[jax.ShapeDtypeStruct((b, dil, sub, ATT_WIDTH), BF16),
                   jax.ShapeDtypeStruct((b, dil, sub, LANES), F32)],
        compiler_params=_cparams(("parallel", "parallel", "arbitrary")),
        name=f"attn_dil{dil}",
    )(q, kv, kv, bias_mat)


def _softplus(x):
    return jnp.maximum(x, 0.0) + jnp.log(1.0 + jnp.exp(-jnp.abs(x)))


def _ssd_kernel(xbc_ref, dt_ref, z_ref, cw_ref, cb_ref, dtb_ref, alog_ref, dskip_ref, gn_ref, e_ref,
                y_ref, st_ref, ext_ref, state_ref):
    c = pl.program_id(1)
    L = SSD_CHUNK

    @pl.when(c == 0)
    def _():
        ext_ref[0:SUBLANES, :] = jnp.zeros((SUBLANES, CONV_DIM), F32)
        state_ref[...] = jnp.zeros_like(state_ref)

    ext_ref[SUBLANES:SUBLANES + L, :] = xbc_ref[...]
    acc = cb_ref[...] + ext_ref[SUBLANES:SUBLANES + L, :] * cw_ref[CONV_W - 1:CONV_W, :]
    for i in range(CONV_W - 1):
        off = SUBLANES - (CONV_W - 1) + i
        acc = acc + ext_ref[off:off + L, :] * cw_ref[i:i + 1, :]
    ext_ref[0:SUBLANES, :] = ext_ref[L:L + SUBLANES, :]
    xa = _silu(acc)

    dt = _softplus(dt_ref[...] + dtb_ref[...])
    da = dt * (-jnp.exp(alog_ref[...]))
    row = lax.broadcasted_iota(jnp.int32, (L, L), 0)
    col = lax.broadcasted_iota(jnp.int32, (L, L), 1)
    tri = row >= col
    a_cs = jnp.dot(tri.astype(F32), da, preferred_element_type=F32, precision=HIGHEST)
    a_cs_t = a_cs.T
    expand = e_ref[...]
    acs_full = _spread(a_cs, expand, 3)
    dt_full = _spread(dt, expand, 3)
    exp_acs = jnp.exp(acs_full)
    a_last = acs_full[L - 1:L, :]
    exp_last = exp_acs[L - 1:L, :]
    xs = xa[:, :SSM_WIDTH]
    xdt = xs * dt_full
    xw = xdt * jnp.exp(a_last - acs_full)

    for g in range(N_SSM_GROUPS):
        b0 = SSM_WIDTH + g * D_STATE
        c0 = SSM_WIDTH + N_SSM_GROUPS * D_STATE + g * D_STATE
        bg_t = xa[:, b0:b0 + D_STATE].T.astype(BF16)
        cg = xa[:, c0:c0 + D_STATE].astype(BF16)
        gram = jnp.dot(cg, bg_t, preferred_element_type=F32)
        for hh in range(HEADS_PER_GROUP):
            h = g * HEADS_PER_GROUP + hh
            hs = slice(h * SSM_HEAD_DIM, (h + 1) * SSM_HEAD_DIM)
            seg = jnp.where(tri, a_cs[:, h:h + 1] - a_cs_t[h:h + 1, :], NEG)
            scores = (gram * jnp.exp(seg)).astype(BF16)
            y_diag = jnp.dot(scores, xdt[:, hs].astype(BF16), preferred_element_type=F32)
            st = state_ref[h]
            y_off = jnp.dot(cg, st.astype(BF16), preferred_element_type=F32) * exp_acs[:, hs]
            y_ref[:, hs] = y_diag + y_off
            state_ref[h] = exp_last[:, hs] * st + jnp.dot(bg_t, xw[:, hs].astype(BF16),
                                                          preferred_element_type=F32)

    y = y_ref[...] + dskip_ref[...] * xs
    u = y * _silu(z_ref[...])
    gw = SSM_WIDTH // N_SSM_GROUPS
    parts = []
    for g in range(N_SSM_GROUPS):
        ug = u[:, g * gw:(g + 1) * gw]
        parts.append(ug * lax.rsqrt(jnp.mean(ug * ug, axis=-1, keepdims=True) + EPS))
    y_ref[...] = jnp.concatenate(parts, axis=-1) * gn_ref[...]

    @pl.when(c == pl.num_programs(1) - 1)
    def _():
        st_ref[0] = state_ref[...]


def _head_expand():
    e = np.zeros((LANES, SSM_WIDTH), np.float32)
    for h in range(N_SSM_HEADS):
        e[h, h * SSM_HEAD_DIM:(h + 1) * SSM_HEAD_DIM] = 1.0
    return jnp.asarray(e)


def _pad_lanes(v):
    return jnp.pad(v.astype(F32), (0, LANES - v.shape[0])).reshape(1, LANES)


def _ssd_prompt(xbc, dt_raw, z, conv_w, conv_b, dt_bias, a_log, d_skip, ssd_norm, batch):
    m = xbc.shape[0]
    nc = m // batch // SSD_CHUNK
    L = SSD_CHUNK
    rows = lambda b, c: (b * nc + c, 0)
    y, st = pl.pallas_call(
        _ssd_kernel,
        grid=(batch, nc),
        in_specs=[pl.BlockSpec((L, CONV_DIM), rows),
                  pl.BlockSpec((L, LANES), rows),
                  pl.BlockSpec((L, SSM_WIDTH), rows),
                  _const_spec((CONV_W, CONV_DIM)),
                  _const_spec((1, CONV_DIM)),
                  _const_spec((1, LANES)),
                  _const_spec((1, LANES)),
                  _const_spec((1, SSM_WIDTH)),
                  _const_spec((1, SSM_WIDTH)),
                  _const_spec((LANES, SSM_WIDTH))],
        out_specs=[pl.BlockSpec((L, SSM_WIDTH), rows),
                   pl.BlockSpec((1, N_SSM_HEADS, D_STATE, SSM_HEAD_DIM), lambda b, c: (b, 0, 0, 0))],
        out_shape=[jax.ShapeDtypeStruct((m, SSM_WIDTH), F32),
                   jax.ShapeDtypeStruct((batch, N_SSM_HEADS, D_STATE, SSM_HEAD_DIM), F32)],
        scratch_shapes=[pltpu.VMEM((SUBLANES + L, CONV_DIM), F32),
                        pltpu.VMEM((N_SSM_HEADS, D_STATE, SSM_HEAD_DIM), F32)],
        compiler_params=_cparams(("parallel", "arbitrary")),
        name="ssd_prompt",
    )(xbc, dt_raw, z, conv_w, conv_b.reshape(1, CONV_DIM), _pad_lanes(dt_bias), _pad_lanes(a_log),
      jnp.repeat(d_skip, SSM_HEAD_DIM).reshape(1, SSM_WIDTH), ssd_norm.reshape(1, SSM_WIDTH),
      _head_expand().astype(BF16))
    return y, jnp.swapaxes(st, -1, -2)


def _router(xn_bf, wr_ref, br_ref):
    return jnp.dot(xn_bf, wr_ref[...], preferred_element_type=F32) + br_ref[...]


def _outproj_kernel(o0, l0, o1, l1, o2, l2, u1_ref, u2_ref, e_ref, ssm_ref, x_ref, w_ref, g_ref, wr_ref,
                    br_ref, x1_ref, xn_ref, lg_ref):
    def natural(o_ref, l_ref, u_ref):
        dil = o_ref.shape[1]
        if dil == 1:
            return o_ref[0, 0].astype(F32), l_ref[0, 0]
        o_cat = jnp.concatenate([o_ref[0, r] for r in range(dil)], axis=0)
        l_cat = jnp.concatenate([l_ref[0, r] for r in range(dil)], axis=0)
        hi = l_cat.astype(BF16)
        lo = (l_cat - hi.astype(F32)).astype(BF16)
        u = u_ref[...]
        return (jnp.dot(u, o_cat, preferred_element_type=F32),
                jnp.dot(u, hi, preferred_element_type=F32) + jnp.dot(u, lo, preferred_element_type=F32))

    branches = [natural(o0, l0, None), natural(o1, l1, u1_ref), natural(o2, l2, u2_ref)]
    lses = [_spread(l, e_ref[...], 2) for _, l in branches]
    mx = functools.reduce(jnp.maximum, lses)
    ws = [jnp.exp(l - mx) for l in lses]
    num = functools.reduce(jnp.add, [w * o for w, (o, _) in zip(ws, branches)])
    att = num * (1.0 / functools.reduce(jnp.add, ws))
    y = jnp.dot(att.astype(BF16), w_ref[:ATT_WIDTH, :], preferred_element_type=F32)
    y = y + jnp.dot(ssm_ref[...].astype(BF16), w_ref[ATT_WIDTH:, :], preferred_element_type=F32)
    x1 = x_ref[...] + y
    x1_ref[...] = x1
    xn = _rms(x1, g_ref[...]).astype(BF16)
    xn_ref[...] = xn
    lg_ref[...] = _router(xn, wr_ref, br_ref)


def _outproj_prompt(att_parts, ssm, x2d, w_bf, g, wr, br):
    m, d = x2d.shape
    mix = w_bf.shape[0]
    tm = TM_OUT
    batch = att_parts[0][0].shape[0]
    per_b = m // batch // tm
    assert tm % (max(DILATIONS) * 2 * SUBLANES) == 0
    row = lambda w: pl.BlockSpec((tm, w), lambda i: (i, 0))
    split = lambda dil, w: pl.BlockSpec((1, dil, tm // dil, w), lambda i: (i // per_b, 0, i % per_b, 0))
    branch_specs = [split(o.shape[1], w) for o, _ in att_parts for w in (ATT_WIDTH, LANES)]
    interleave = [jnp.asarray(_deinterleave_matrix(tm, dil).T, BF16) for dil in DILATIONS[1:]]
    return pl.pallas_call(
        _outproj_kernel,
        grid=(m // tm,),
        in_specs=branch_specs + [_const_spec((tm, tm)), _const_spec((tm, tm)),
                                 _const_spec((LANES, ATT_WIDTH)), row(SSM_WIDTH), row(d),
                                 _const_spec((mix, d)), _const_spec((1, d)),
                                 _const_spec((d, LANES)), _const_spec((1, LANES))],
        out_specs=[row(d), row(d), row(LANES)],
        out_shape=[jax.ShapeDtypeStruct((m, d), F32), jax.ShapeDtypeStruct((m, d), BF16),
                   jax.ShapeDtypeStruct((m, LANES), F32)],
        compiler_params=_cparams(("parallel",)),
        name="outproj_prompt",
    )(*[a for part in att_parts for a in part], *interleave, _head_expand().astype(BF16), ssm, x2d, w_bf,
      g.reshape(1, d), wr, br)


def _attn_sample_kernel(q_ref, kn_ref, vn_ref, kc_ref, vc_ref, bias_ref, bias0_ref, o_ref):
    w = kc_ref.shape[1]
    q = q_ref[0]
    head_grp = lax.broadcasted_iota(jnp.int32, (N_ATT_HEADS, 1), 0) // KV_REP
    kn = kn_ref[0].astype(BF16).astype(F32)
    vn = vn_ref[0].astype(BF16).astype(F32)
    s_self = jnp.sum(q.astype(F32) * kn, axis=-1, keepdims=True)

    def taps(c_ref, dil):
        span = WIN * dil
        rows = c_ref[0, w - span:w, :].astype(BF16)
        if dil == 1:
            return rows
        tap = lax.broadcasted_iota(jnp.int32, (WIN, span), 0)
        col = lax.broadcasted_iota(jnp.int32, (WIN, span), 1)
        pick = jnp.where(col == tap * dil, 1.0, 0.0).astype(BF16)
        return jnp.dot(pick, rows, preferred_element_type=F32).astype(BF16)

    scores, selfs, lses = [], [], []
    for g, dil in enumerate(DILATIONS):
        kk = taps(kc_ref, dil)
        s = jnp.zeros((N_ATT_HEADS, WIN), F32)
        for kvh in range(N_KV_HEADS):
            cs = slice(kvh * HEAD_DIM, (kvh + 1) * HEAD_DIM)
            sk = lax.dot_general(q, kk[:, cs], (((1,), (1,)), ((), ())), preferred_element_type=F32)
            s = jnp.where(head_grp == kvh, sk, s)
        s = s + bias_ref[g]
        s0 = s_self + bias0_ref[g]
        mx = jnp.maximum(jnp.max(s, axis=-1, keepdims=True), s0)
        lse = mx + jnp.log(jnp.sum(jnp.exp(s - mx), axis=-1, keepdims=True) + jnp.exp(s0 - mx))
        scores.append(s); selfs.append(s0); lses.append(lse)
    top = functools.reduce(jnp.maximum, lses)
    es = [jnp.exp(l - top) for l in lses]
    tot = functools.reduce(jnp.add, es)
    o = jnp.zeros((N_ATT_HEADS, HEAD_DIM), F32)
    for s, s0, lse, e, dil in zip(scores, selfs, lses, es, DILATIONS):
        wgt = e / tot
        p = (jnp.exp(s - lse) * wgt).astype(BF16)
        p0 = (jnp.exp(s0 - lse) * wgt).astype(BF16).astype(F32)
        vv = taps(vc_ref, dil)
        o = o + p0 * vn
        for kvh in range(N_KV_HEADS):
            cs = slice(kvh * HEAD_DIM, (kvh + 1) * HEAD_DIM)
            ok = jnp.dot(p, vv[:, cs], preferred_element_type=F32)
            o = o + jnp.where(head_grp == kvh, ok, 0.0)
    o_ref[0] = o


def _attn_sample(q, k_new, v_new, k_cache, v_cache, bias_s, bias0_s):
    n, w = k_cache.shape[0], k_cache.shape[1]
    assert w % (max(DILATIONS) * WIN) == 0
    tok = lambda b: (b, 0, 0)
    head = pl.BlockSpec((1, N_ATT_HEADS, HEAD_DIM), tok)
    window = pl.BlockSpec((1, w, KV_WIDTH), tok)
    return pl.pallas_call(
        _attn_sample_kernel,
        grid=(n,),
        in_specs=[head, head, head, window, window,
                  _const_spec((len(DILATIONS), N_ATT_HEADS, WIN)),
                  _const_spec((len(DILATIONS), N_ATT_HEADS, 1))],
        out_specs=head,
        out_shape=jax.ShapeDtypeStruct((n, N_ATT_HEADS, HEAD_DIM), F32),
        compiler_params=_cparams(("parallel",)),
        name="attn_sample",
    )(q, jnp.repeat(k_new, KV_REP, axis=1), jnp.repeat(v_new, KV_REP, axis=1), k_cache, v_cache,
      bias_s, bias0_s)


def _conv_sample_kernel(xbc_ref, b0_ref, b1_ref, b2_ref, cw_ref, cb_ref, dt_ref, dtb_ref, alog_ref,
                        e_ref, xa_ref, xdt_ref, decay_ref):
    acc = cb_ref[...] + xbc_ref[...] * cw_ref[CONV_W - 1:CONV_W, :]
    for i, buf in enumerate((b0_ref, b1_ref, b2_ref)):
        acc = acc + buf[...] * cw_ref[i:i + 1, :]
    xa = _silu(acc)
    xa_ref[...] = xa
    dt = _softplus(dt_ref[...] + dtb_ref[...])
    decay_ref[...] = jnp.exp(dt * (-jnp.exp(alog_ref[...])))
    dt_full = jnp.dot(dt, e_ref[...], preferred_element_type=F32, precision=HIGHEST)
    xdt_ref[...] = xa[:, :SSM_WIDTH] * dt_full


def _conv_sample(xbc, conv_buf, conv_w, conv_b, dt_raw, dt_bias, a_log):
    n = xbc.shape[0]
    args = (xbc, conv_buf[:, 0], conv_buf[:, 1], conv_buf[:, 2], conv_w, conv_b.reshape(1, CONV_DIM),
            dt_raw, _pad_lanes(dt_bias), _pad_lanes(a_log),
            _head_expand())
    return pl.pallas_call(
        _conv_sample_kernel,
        grid=(1,),
        in_specs=[_const_spec(a.shape) for a in args],
        out_specs=[_const_spec((n, CONV_DIM)), _const_spec((n, SSM_WIDTH)), _const_spec((n, LANES))],
        out_shape=[jax.ShapeDtypeStruct((n, CONV_DIM), F32), jax.ShapeDtypeStruct((n, SSM_WIDTH), F32),
                   jax.ShapeDtypeStruct((n, LANES), F32)],
        compiler_params=_cparams(("arbitrary",)),
        name="conv_sample",
    )(*args)


def _ssm_sample_kernel(xdt_ref, decay_ref, b_ref, c_ref, h0_ref, hn_ref, y_ref):
    for g in range(N_SSM_GROUPS):
        hs = slice(g * HEADS_PER_GROUP, (g + 1) * HEADS_PER_GROUP)
        hn = decay_ref[0, hs] * h0_ref[0, hs] + xdt_ref[0, hs] * b_ref[0, g]
        hn_ref[0, hs] = hn
        c_row = c_ref[0, g].astype(BF16).astype(F32)
        y_ref[0, hs] = jnp.sum(hn.astype(BF16).astype(F32) * c_row, axis=-1, keepdims=True)


def _ssm_sample(xdt, decay, bmat, cmat, h0):
    n = xdt.shape[0]
    p = SSM_HEAD_DIM
    tok4 = lambda b: (b, 0, 0, 0)
    hn, y = pl.pallas_call(
        _ssm_sample_kernel,
        grid=(n,),
        in_specs=[pl.BlockSpec((1, N_SSM_HEADS, p, 1), tok4),
                  pl.BlockSpec((1, N_SSM_HEADS, 1, 1), tok4),
                  pl.BlockSpec((1, N_SSM_GROUPS, 1, D_STATE), tok4),
                  pl.BlockSpec((1, N_SSM_GROUPS, 1, D_STATE), tok4),
                  pl.BlockSpec((1, N_SSM_HEADS, p, D_STATE), tok4)],
        out_specs=[pl.BlockSpec((1, N_SSM_HEADS, p, D_STATE), tok4),
                   pl.BlockSpec((1, N_SSM_HEADS, p, 1), tok4)],
        out_shape=[jax.ShapeDtypeStruct((n, N_SSM_HEADS, p, D_STATE), F32),
                   jax.ShapeDtypeStruct((n, N_SSM_HEADS, p, 1), F32)],
        compiler_params=_cparams(("parallel",)),
        name="ssm_sample",
    )(xdt.reshape(n, N_SSM_HEADS, p, 1), decay[:, :N_SSM_HEADS].reshape(n, N_SSM_HEADS, 1, 1),
      bmat.reshape(n, N_SSM_GROUPS, 1, D_STATE), cmat.reshape(n, N_SSM_GROUPS, 1, D_STATE), h0)
    return hn, y.reshape(n, SSM_WIDTH)


def _outproj_sample_kernel(att_ref, y_ref, xs_ref, z_ref, dskip_ref, gn_ref, x_ref, w_ref, g_ref,
                           wr_ref, br_ref, x1_ref, xn_ref, lg_ref):
    y = y_ref[...] + dskip_ref[...] * xs_ref[...]
    u = y * _silu(z_ref[...])
    gw = SSM_WIDTH // N_SSM_GROUPS
    parts = []
    for g in range(N_SSM_GROUPS):
        ug = u[:, g * gw:(g + 1) * gw]
        parts.append(ug * lax.rsqrt(jnp.mean(ug * ug, axis=-1, keepdims=True) + EPS)
                     * gn_ref[:, g * gw:(g + 1) * gw])
    mix = jnp.concatenate([att_ref[...]] + parts, axis=-1).astype(BF16)
    x1 = x_ref[...] + jnp.dot(mix, w_ref[...], preferred_element_type=F32)
    x1_ref[...] = x1
    xn = _rms(x1, g_ref[...]).astype(BF16)
    xn_ref[...] = xn
    lg_ref[...] = _router(xn, wr_ref, br_ref)


def _outproj_sample(att, y, xs, z, d_skip, ssd_norm, x2d, w_bf, g, wr, br):
    n, d = x2d.shape
    args = (att, y, xs, z, jnp.repeat(d_skip, SSM_HEAD_DIM).reshape(1, SSM_WIDTH),
            ssd_norm.reshape(1, SSM_WIDTH), x2d, w_bf, g.reshape(1, d), wr, br)
    return pl.pallas_call(
        _outproj_sample_kernel,
        grid=(1,),
        in_specs=[_const_spec(a.shape) for a in args],
        out_specs=[_const_spec((n, d)), _const_spec((n, d)), _const_spec((n, LANES))],
        out_shape=[jax.ShapeDtypeStruct((n, d), F32), jax.ShapeDtypeStruct((n, d), BF16),
                   jax.ShapeDtypeStruct((n, LANES), F32)],
        compiler_params=_cparams(("arbitrary",)),
        name="outproj_sample",
    )(*args)


TT = 256
CHUNK_ALIGN = SUBLANES
LROWS = -(-(TT * TOP_K + N_EXPERTS * (CHUNK_ALIGN - 1)) // TT) * TT
CHUNK_SIZES = tuple(1 << b for b in range(TT.bit_length() - 1, CHUNK_ALIGN.bit_length() - 2, -1))
SUB = 256
SUB_TAIL = SUB // 2
RG = 5 * SUB
TF = 256
VMEM_LIMIT_FFN = 60000 * 1024


def _chunk_loop(cnt_ref, loff_ref, dest_ref, tile, fn):
    def per_expert(e, carry):
        idx = tile * N_EXPERTS + e
        n, off, dst = cnt_ref[idx], loff_ref[idx], dest_ref[idx]
        for size in CHUNK_SIZES:
            take = (n & size) != 0

            @pl.when(take)
            def _(off=off, dst=dst, size=size):
                fn(pl.multiple_of(off, CHUNK_ALIGN), pl.multiple_of(dst, CHUNK_ALIGN), size)

            step = jnp.where(take, size, 0)
            off, dst = off + step, dst + step
        return carry

    lax.fori_loop(0, N_EXPERTS, per_expert, 0)


def _dispatch_kernel(cnt_ref, loff_ref, dest_ref, xp_ref, xs_ref, lpos_ref, gate_ref, out_hbm, buf, sem):
    t = pl.program_id(0)
    last = pl.num_programs(0) - 1
    d = xp_ref.shape[1]
    x = jnp.where(t == last, xs_ref[...], xp_ref[...])
    rows = lax.broadcasted_iota(jnp.int32, (LROWS, TT), 0)
    onehot = jnp.zeros((LROWS, TT), F32)
    wcol = jnp.zeros((LROWS, 1), F32)
    for k in range(TOP_K):
        hit = jnp.where(rows == lpos_ref[0, k:k + 1, :], 1.0, 0.0)
        onehot = onehot + hit
        wcol = wcol + jnp.sum(hit * gate_ref[0, k:k + 1, :], axis=-1, keepdims=True)
    tile_buf = buf.at[t % 2]
    tile_buf[:, 0:d] = jnp.dot(onehot.astype(BF16), x, preferred_element_type=F32)
    tile_buf[:, d:d + LANES] = jnp.broadcast_to(wcol, (LROWS, LANES))

    def copy(tile, off, dst, size):
        return pltpu.make_async_copy(buf.at[tile % 2, pl.ds(off, size)], out_hbm.at[pl.ds(dst, size)],
                                     sem.at[tile % 2])

    _chunk_loop(cnt_ref, loff_ref, dest_ref, t, lambda o, g, s: copy(t, o, g, s).start())

    @pl.when(t > 0)
    def _():
        _chunk_loop(cnt_ref, loff_ref, dest_ref, t - 1, lambda o, g, s: copy(t - 1, o, g, s).wait())

    @pl.when(t == last)
    def _():
        _chunk_loop(cnt_ref, loff_ref, dest_ref, t, lambda o, g, s: copy(t, o, g, s).wait())


def _dispatch(tabs, xn_p, xn_s, lpos_t, gate_t, n_rows):
    nt = lpos_t.shape[0]
    d = xn_p.shape[1]
    last_p = xn_p.shape[0] // TT - 1
    grid_spec = pltpu.PrefetchScalarGridSpec(
        num_scalar_prefetch=3,
        grid=(nt,),
        in_specs=[pl.BlockSpec((TT, d), lambda t, *_: (jnp.minimum(t, last_p), 0)),
                  pl.BlockSpec((TT, d), lambda t, *_: (0, 0)),
                  pl.BlockSpec((1, TOP_K, TT), lambda t, *_: (t, 0, 0)),
                  pl.BlockSpec((1, TOP_K, TT), lambda t, *_: (t, 0, 0))],
        out_specs=pl.BlockSpec(memory_space=pl.ANY),
        scratch_shapes=[pltpu.VMEM((2, LROWS, d + LANES), F32), pltpu.SemaphoreType.DMA((2,))],
    )
    return pl.pallas_call(
        _dispatch_kernel,
        grid_spec=grid_spec,
        out_shape=jax.ShapeDtypeStruct((n_rows, d + LANES), F32),
        compiler_params=_cparams(("arbitrary",)),
        name="moe_dispatch",
    )(*tabs, xn_p, xn_s, lpos_t, gate_t)


def _ffn_kernel(ge_ref, gs_ref, gn_ref, gt_ref, ng_ref, xs_hbm, wgu_hbm, wd_hbm, bgu_ref, bdn_ref, out_hbm,
                xbuf, acc, ostage, wg_st, wu_st, wd_st, wg_bf, wu_bf, wd_bf, sem_w, sem_x, sem_o):
    d = acc.shape[1]
    d_ff = wd_hbm.shape[1]
    nf = d_ff // TF
    n_groups = ng_ref[0]
    total = n_groups * nf

    def w_copies(s, slot):
        g = s // nf
        f = s - g * nf
        e = ge_ref[g]
        c0 = pl.multiple_of(f * TF, TF)
        return (pltpu.make_async_copy(wgu_hbm.at[e, :, pl.ds(c0, TF)], wg_st.at[slot], sem_w.at[slot, 0]),
                pltpu.make_async_copy(wgu_hbm.at[e, :, pl.ds(d_ff + c0, TF)], wu_st.at[slot], sem_w.at[slot, 1]),
                pltpu.make_async_copy(wd_hbm.at[e, pl.ds(c0, TF), :], wd_st.at[slot], sem_w.at[slot, 2]))

    def x_copy(g, j, size=SUB):
        r0 = pl.multiple_of(j * SUB, SUB)
        return pltpu.make_async_copy(xs_hbm.at[pl.ds(pl.multiple_of(gs_ref[g] + r0, CHUNK_ALIGN), size)],
                                     xbuf.at[g % 2, pl.ds(r0, size)], sem_x)

    def o_copy(g, j, size=SUB):
        r0 = pl.multiple_of(j * SUB, SUB)
        return pltpu.make_async_copy(ostage.at[j % 2, pl.ds(0, size)],
                                     out_hbm.at[pl.ds(pl.multiple_of(gs_ref[g] + r0, CHUNK_ALIGN), size)],
                                     sem_o.at[j % 2])

    def loop(n, fn):
        lax.fori_loop(0, n, lambda j, c: (fn(j), c)[1], 0)

    def group_rows(g, op):
        loop(gn_ref[g], lambda j: op(x_copy(g, j)))

        @pl.when(gt_ref[g] == 1)
        def _():
            op(x_copy(g, gn_ref[g], SUB_TAIL))

    def drain_stores(g):
        n = gn_ref[g]
        tail = gt_ref[g]

        @pl.when(tail == 1)
        def _():
            o_copy(g, n, SUB_TAIL).wait()

        @pl.when(n >= 1)
        def _():
            o_copy(g, n - 1).wait()

        @pl.when(jnp.logical_and(n >= 2, tail == 0))
        def _():
            o_copy(g, n - 2).wait()

    @pl.when(total > 0)
    def _():
        for c in w_copies(0, 0):
            c.start()
        group_rows(0, lambda c: c.start())

    def item(s, carry):
        slot = s % 2
        g = s // nf
        f = s - g * nf
        e = ge_ref[g]
        nsub = gn_ref[g]
        xg = xbuf.at[g % 2]

        @pl.when(s + 1 < total)
        def _():
            for c in w_copies(s + 1, 1 - slot):
                c.start()

        @pl.when(f == 0)
        def _():
            group_rows(g, lambda c: c.wait())

        @pl.when(jnp.logical_and(f == 1, g + 1 < n_groups))
        def _():
            group_rows(g + 1, lambda c: c.start())

        for c in w_copies(s, slot):
            c.wait()
        bg = bgu_ref[pl.ds(e * 2 * nf + f, 1), :]
        bu = bgu_ref[pl.ds(e * 2 * nf + nf + f, 1), :]

        def sub_block(j, phase, cast=False, size=SUB):
            rs = pl.ds(pl.multiple_of(j * SUB, SUB), size)
            x = xg[rs, 0:d].astype(BF16)
            if cast:
                wg, wu, wd = (st[slot].astype(BF16) for st in (wg_st, wu_st, wd_st))
                wg_bf[...], wu_bf[...], wd_bf[...] = wg, wu, wd
            else:
                wg, wu, wd = wg_bf[...], wu_bf[...], wd_bf[...]
            hg = jnp.dot(x, wg, preferred_element_type=F32) + bg
            hu = jnp.dot(x, wu, preferred_element_type=F32) + bu
            gg = jnp.minimum(hg, SWIGLU_LIMIT)
            uu = jnp.clip(hu, -SWIGLU_LIMIT, SWIGLU_LIMIT)
            act = gg * (1.0 / (1.0 + jnp.exp(-SWIGLU_ALPHA * gg))) * (uu + 1.0)
            part = jnp.dot(act.astype(BF16), wd, preferred_element_type=F32)
            if phase == "first":
                acc[rs, :] = part + bdn_ref[pl.ds(e, 1), :]
            elif phase == "middle":
                acc[rs, :] += part
            else:
                @pl.when(j >= 2)
                def _():
                    o_copy(g, j - 2).wait()

                ostage[j % 2, 0:size] = (acc[rs, :] + part) * xg[rs, d:d + 1]
                o_copy(g, j, size).start()

        tail = gt_ref[g] == 1

        def storing_blocks():
            sub_block(jnp.int32(0), "last", cast=True)

            def pair(p):
                sub_block(2 * p + 1, "last")
                sub_block(2 * p + 2, "last")

            loop((nsub - 1) // 2, pair)

            @pl.when(nsub % 2 == 0)
            def _():
                sub_block(nsub - 1, "last")

            @pl.when(tail)
            def _():
                sub_block(nsub, "last", size=SUB_TAIL)

        def accumulating_blocks(phase):
            @pl.when(nsub < 2)
            def _():
                sub_block(jnp.int32(0), phase, cast=True)

                @pl.when(tail)
                def _():
                    sub_block(jnp.int32(1), phase, size=SUB_TAIL)

            @pl.when(nsub >= 2)
            def _():
                sub_block(jnp.int32(0), phase, cast=True, size=2 * SUB)
                rest = 2 * (nsub - 2) + gt_ref[g]
                for units in range(1, (RG - 2 * SUB) // SUB_TAIL + 1):
                    @pl.when(rest == units)
                    def _(units=units):
                        sub_block(jnp.int32(2), phase, size=units * SUB_TAIL)

        @pl.when(f == 0)
        def _():
            accumulating_blocks("first")

        @pl.when(jnp.logical_and(f > 0, f < nf - 1))
        def _():
            accumulating_blocks("middle")

        @pl.when(f == nf - 1)
        def _():
            @pl.when(g > 0)
            def _():
                drain_stores(g - 1)

            storing_blocks()

        return carry

    lax.fori_loop(0, total, item, 0)

    @pl.when(total > 0)
    def _():
        drain_stores(n_groups - 1)


def _moe_ffn(groups, x_sorted, w_gate_up, b_gu, w_down, b_dn):
    n_rows = x_sorted.shape[0]
    _, d_ff, d = w_down.shape
    nf = d_ff // TF
    assert nf >= 2
    bgu2 = b_gu.reshape(N_EXPERTS * 2 * nf, TF)
    grid_spec = pltpu.PrefetchScalarGridSpec(
        num_scalar_prefetch=5,
        grid=(1,),
        in_specs=[pl.BlockSpec(memory_space=pl.ANY), pl.BlockSpec(memory_space=pl.ANY),
                  pl.BlockSpec(memory_space=pl.ANY),
                  pl.BlockSpec(bgu2.shape, lambda i, *_: (0, 0), pipeline_mode=pl.Buffered(1)),
                  pl.BlockSpec(b_dn.shape, lambda i, *_: (0, 0), pipeline_mode=pl.Buffered(1))],
        out_specs=pl.BlockSpec(memory_space=pl.ANY),
        scratch_shapes=[pltpu.VMEM((2, RG, d + LANES), F32), pltpu.VMEM((RG, d), F32),
                        pltpu.VMEM((2, SUB, d), F32),
                        pltpu.VMEM((2, d, TF), F32), pltpu.VMEM((2, d, TF), F32), pltpu.VMEM((2, TF, d), F32),
                        pltpu.VMEM((d, TF), BF16), pltpu.VMEM((d, TF), BF16), pltpu.VMEM((TF, d), BF16),
                        pltpu.SemaphoreType.DMA((2, 3)), pltpu.SemaphoreType.DMA(()),
                        pltpu.SemaphoreType.DMA((2,))],
    )
    return pl.pallas_call(
        _ffn_kernel,
        grid_spec=grid_spec,
        out_shape=jax.ShapeDtypeStruct((n_rows, d), F32),
        compiler_params=pltpu.CompilerParams(dimension_semantics=("arbitrary",),
                                             vmem_limit_bytes=VMEM_LIMIT_FFN),
        name="moe_ffn",
    )(*groups, x_sorted, w_gate_up, w_down, bgu2, b_dn)


def _route(logits, m_pad):
    m = logits.shape[0]
    nt = m_pad // TT
    top_v, top_i = lax.top_k(logits, TOP_K)
    gate = jnp.pad(jax.nn.softmax(top_v, axis=-1), ((0, m_pad - m), (0, 0)))
    top_i = jnp.pad(top_i.astype(jnp.int32), ((0, m_pad - m), (0, 0)), constant_values=-1)
    chosen = (top_i[:, :, None] == jnp.arange(N_EXPERTS, dtype=jnp.int32)).astype(jnp.int32)
    tiles = chosen.sum(axis=1).reshape(nt, TT, N_EXPERTS)
    cnt = (tiles.sum(axis=1) + CHUNK_ALIGN - 1) // CHUNK_ALIGN * CHUNK_ALIGN
    loff = jnp.cumsum(cnt, axis=1) - cnt
    seg = cnt.sum(axis=0)
    seg_start = jnp.cumsum(seg) - seg
    dest = seg_start[None, :] + jnp.cumsum(cnt, axis=0) - cnt
    before = jnp.asarray(np.tril(np.ones((TT, TT), np.float32), -1), BF16)
    rank = jnp.einsum('ij,tje->tie', before, tiles.astype(BF16),
                      preferred_element_type=F32).astype(jnp.int32)
    lpos_all = (loff[:, None, :] + rank).reshape(m_pad, N_EXPERTS)
    lpos = (lpos_all[:, None, :] * chosen).sum(axis=-1)
    lpos = jnp.where(top_i >= 0, lpos, -1).astype(jnp.int32)
    n_rows_bound = nt * LROWS
    n_grp_max = n_rows_bound // RG + N_EXPERTS
    grp = (seg + RG - 1) // RG
    grp_end = jnp.cumsum(grp)
    gi = jnp.arange(n_grp_max, dtype=jnp.int32)
    g_exp = jnp.minimum((gi[:, None] >= grp_end[None, :]).sum(axis=1), N_EXPERTS - 1).astype(jnp.int32)
    within = gi - (grp_end[g_exp] - grp[g_exp])
    g_start = (seg_start[g_exp] + within * RG).astype(jnp.int32)
    g_rows = jnp.clip(seg[g_exp] - within * RG, 0, RG)
    g_rows = jnp.where(gi < grp_end[-1], g_rows, 0)
    n_full = g_rows // SUB
    rest = g_rows - n_full * SUB
    g_tail = (rest > 0) & (rest <= SUB_TAIL) & (n_full >= 1)
    g_nsub = n_full + ((rest > 0) & ~g_tail)
    groups = (g_exp, g_start, g_nsub.astype(jnp.int32), g_tail.astype(jnp.int32),
              grp_end[-1:].astype(jnp.int32))
    tabs = tuple(a.reshape(-1).astype(jnp.int32) for a in (cnt, loff, dest))
    return tabs, lpos, gate, groups, n_rows_bound + SUB


def _combine_kernel(cnt_ref, loff_ref, dest_ref, src_hbm, lpos_ref, x1p_ref, x1s_ref, g_ref,
                    yp_ref, ys_ref, buf, sem):
    t = pl.program_id(0)
    last = pl.num_programs(0) - 1

    def copy(tile, off, src, size):
        return pltpu.make_async_copy(src_hbm.at[pl.ds(src, size)], buf.at[tile % 2, pl.ds(off, size)],
                                     sem.at[tile % 2])

    @pl.when(t == 0)
    def _():
        buf[...] = jnp.zeros_like(buf)
        _chunk_loop(cnt_ref, loff_ref, dest_ref, t, lambda o, g, s: copy(t, o, g, s).start())

    @pl.when(t < last)
    def _():
        _chunk_loop(cnt_ref, loff_ref, dest_ref, t + 1, lambda o, g, s: copy(t + 1, o, g, s).start())

    _chunk_loop(cnt_ref, loff_ref, dest_ref, t, lambda o, g, s: copy(t, o, g, s).wait())
    cols = lax.broadcasted_iota(jnp.int32, (TT, LROWS), 1)
    sel = jnp.zeros((TT, LROWS), F32)
    for k in range(TOP_K):
        sel = sel + jnp.where(cols == lpos_ref[:, k:k + 1], 1.0, 0.0)
    f = jnp.dot(sel.astype(BF16), buf[t % 2].astype(BF16), preferred_element_type=F32)

    @pl.when(t < last)
    def _():
        yp_ref[...] = _rms(x1p_ref[...] + f, g_ref[...])

    @pl.when(t == last)
    def _():
        ys_ref[...] = _rms(x1s_ref[...] + f, g_ref[...])


def _combine(tabs, out_sorted, lpos, x1p, x1s, g):
    d = x1p.shape[1]
    nt = lpos.shape[0] // TT
    last_p = x1p.shape[0] // TT - 1
    prompt = pl.BlockSpec((TT, d), lambda t, *_: (jnp.minimum(t, last_p), 0))
    sample = pl.BlockSpec((TT, d), lambda t, *_: (0, 0))
    grid_spec = pltpu.PrefetchScalarGridSpec(
        num_scalar_prefetch=3,
        grid=(nt,),
        in_specs=[pl.BlockSpec(memory_space=pl.ANY),
                  pl.BlockSpec((TT, TOP_K), lambda t, *_: (t, 0)),
                  prompt, sample,
                  pl.BlockSpec((1, d), lambda t, *_: (0, 0))],
        out_specs=[prompt, sample],
        scratch_shapes=[pltpu.VMEM((2, LROWS, d), F32), pltpu.SemaphoreType.DMA((2,))],
    )
    return pl.pallas_call(
        _combine_kernel,
        grid_spec=grid_spec,
        out_shape=[jax.ShapeDtypeStruct(x1p.shape, F32), jax.ShapeDtypeStruct((TT, d), F32)],
        compiler_params=_cparams(("arbitrary",)),
        name="moe_combine",
    )(*tabs, out_sorted, lpos, x1p, x1s, g.reshape(1, d))


def _t5_bucket(dist):
    max_exact = N_BUCKETS // 2
    dd = dist.astype(F32)
    large = max_exact + (jnp.log(jnp.maximum(dd, 1.0) / max_exact)
                         / math.log(BUCKET_MAX_DIST / max_exact) * (N_BUCKETS - max_exact)).astype(jnp.int32)
    large = jnp.minimum(large, N_BUCKETS - 1)
    return jnp.where(dist < max_exact, dist, large)


def _bias_tables(rel_bias):
    dist = jnp.asarray(np.arange(N_TAPS)[None, :] * np.array(DILATIONS)[:, None], jnp.int32)
    bias = jnp.transpose(rel_bias[_t5_bucket(dist)], (2, 0, 1)).astype(F32)
    by_branch = jnp.transpose(bias, (1, 0, 2))
    g, h = by_branch.shape[:2]
    row = jnp.concatenate([by_branch[:, :, ::-1], jnp.full((g, h, WIN), NEG, F32)], axis=-1)
    flat = jnp.broadcast_to(row[:, :, None, :], (g, h, WIN, 2 * WIN + 1)).reshape(g, h, -1)
    band = flat[:, :, :WIN * 2 * WIN].reshape(g, h, WIN, 2 * WIN)
    return band, by_branch[:, :, :0:-1], by_branch[:, :, 0:1]


def kernel(x_prompt, x_sample, cache_k_win, cache_v_win, state_conv, state_ssm, rel_bias, attn_norm, w_in, conv_w, conv_b, dt_bias, a_log, d_skip, ssd_norm, w_out, ffn_norm, w_router, b_router, w_gate_up, b_gate_up, w_down, b_down, final_norm):
    bp, tp, d = x_prompt.shape
    bs, ts, _ = x_sample.shape
    depth = w_in.shape[0]
    assert depth == 1 and ts == 1 and tp % (max(DILATIONS) * WIN) == 0
    keep = min(max(DILATIONS) * WIN, tp)
    band, samp, samp0 = _bias_tables(rel_bias)
    l = 0

    xp = x_prompt.reshape(bp * tp, d)
    xs = x_sample.reshape(bs * ts, d)
    w_in_bf = jnp.pad(w_in[l], ((0, 0), (0, IN_PROJ_PAD - IN_PROJ))).astype(BF16)
    w_out_bf = w_out[l].astype(BF16)
    wr = jnp.pad(w_router[l], ((0, 0), (0, LANES - N_EXPERTS))).astype(BF16)
    br = jnp.pad(b_router[l], (0, LANES - N_EXPERTS), constant_values=NEG).reshape(1, LANES)

    k, v, z, xbc, dt_raw, *branch_qkv = _inproj_prompt(xp, attn_norm[l], w_in_bf, bp)
    k3 = k.reshape(bp, tp, KV_WIDTH)
    v3 = v.reshape(bp, tp, KV_WIDTH)
    att_parts = [_attn_branch(branch_qkv[2 * gi], branch_qkv[2 * gi + 1], band[gi])
                 for gi in range(len(DILATIONS))]
    ssm, st_p = _ssd_prompt(xbc, dt_raw, z, conv_w[l], conv_b[l], dt_bias[l], a_log[l], d_skip[l],
                            ssd_norm[l], bp)
    x1p, xnp_, lgp = _outproj_prompt(att_parts, ssm, xp, w_out_bf, ffn_norm[l], wr, br)
    k_win_p = k3[:, tp - keep:].reshape(1, bp, keep, N_KV_HEADS, HEAD_DIM)
    v_win_p = v3[:, tp - keep:].reshape(1, bp, keep, N_KV_HEADS, HEAD_DIM)
    conv_p = xbc.reshape(bp, tp, CONV_DIM)[:, tp - (CONV_W - 1):][None]

    q_s, k_s, v_s, z_s, xbc_s, dt_s = _inproj(xs, attn_norm[l], w_in_bf, bs * ts)
    q_s = q_s.reshape(bs, N_ATT_HEADS, HEAD_DIM)
    k_s = k_s.reshape(bs, N_KV_HEADS, HEAD_DIM)
    v_s = v_s.reshape(bs, N_KV_HEADS, HEAD_DIM)
    wbuf = cache_k_win.shape[2]
    att_s = _attn_sample(q_s, k_s, v_s, cache_k_win[l].reshape(bs, wbuf, KV_WIDTH),
                         cache_v_win[l].reshape(bs, wbuf, KV_WIDTH), samp, samp0)
    xa_s, xdt_s, decay_s = _conv_sample(xbc_s, state_conv[l], conv_w[l], conv_b[l], dt_s, dt_bias[l],
                                        a_log[l])
    nbc = N_SSM_GROUPS * D_STATE
    h_s, y_s = _ssm_sample(xdt_s, decay_s, xa_s[:, SSM_WIDTH:SSM_WIDTH + nbc], xa_s[:, SSM_WIDTH + nbc:],
                           state_ssm[l])
    x1s, xns, lgs = _outproj_sample(att_s.reshape(bs, ATT_WIDTH), y_s, xa_s[:, :SSM_WIDTH], z_s,
                                    d_skip[l], ssd_norm[l], xs, w_out_bf, ffn_norm[l], wr, br)
    conv_s = jnp.concatenate([state_conv[l][:, 1:], xbc_s[:, None]], axis=1)[None]

    n_s = bs * ts
    assert (bp * tp) % TT == 0 and n_s <= TT
    m_pad = bp * tp + TT
    logits = jnp.concatenate([lgp[:, :N_EXPERTS], lgs[:, :N_EXPERTS]], axis=0)
    tabs, lpos, gate, groups, n_rows = _route(logits, m_pad)
    by_tile = lambda a: jnp.transpose(a.reshape(m_pad // TT, TT, TOP_K), (0, 2, 1))
    pad_rows = lambda a: jnp.pad(a, ((0, TT - n_s), (0, 0)))
    x_sorted = _dispatch(tabs, xnp_, pad_rows(xns), by_tile(lpos), by_tile(gate), n_rows)
    out_sorted = _moe_ffn(groups, x_sorted, w_gate_up[l], b_gate_up[l], w_down[l], b_down[l])
    y_p, y_s_out = _combine(tabs, out_sorted, lpos, x1p, pad_rows(x1s), final_norm)
    y_s_out = y_s_out[:n_s]

    return (y_p.reshape(bp, tp, d), y_s_out.reshape(bs, ts, d), k_win_p, v_win_p, conv_p, st_p[None],
            k_s.reshape(1, bs, ts, N_KV_HEADS, HEAD_DIM), v_s.reshape(1, bs, ts, N_KV_HEADS, HEAD_DIM),
            conv_s, h_s[None])
```

```python
import functools
import math

import jax
import jax.numpy as jnp
import numpy as np
from jax import lax
from jax.experimental import pallas as pl
from jax.experimental.pallas import tpu as pltpu

F32 = jnp.float32
BF16 = jnp.bfloat16
HIGHEST = lax.Precision.HIGHEST

LANES = 128
SUBLANES = 8
VMEM_LIMIT = 56 * 1024 * 1024

HEAD_DIM = 64
N_ATT_HEADS = 16
N_KV_HEADS = 4
KV_REP = N_ATT_HEADS // N_KV_HEADS
ATT_WIDTH = N_ATT_HEADS * HEAD_DIM
KV_WIDTH = N_KV_HEADS * HEAD_DIM
DILATIONS = (1, 4, 16)
N_TAPS = 129
WIN = N_TAPS - 1
ATT_SCALE = HEAD_DIM ** -0.5
N_BUCKETS = 32
BUCKET_MAX_DIST = 2048
SSM_HEAD_DIM = 64
N_SSM_HEADS = 16
SSM_WIDTH = N_SSM_HEADS * SSM_HEAD_DIM
N_SSM_GROUPS = 2
HEADS_PER_GROUP = N_SSM_HEADS // N_SSM_GROUPS
D_STATE = 128
CONV_W = 4
CONV_DIM = SSM_WIDTH + 2 * N_SSM_GROUPS * D_STATE
SSD_CHUNK = 128
N_EXPERTS = 32
TOP_K = 4
SWIGLU_LIMIT = 7.0
SWIGLU_ALPHA = 1.702
EPS = 1e-5
NEG = -1e30

Q0, K0, V0, Z0, X0, DT0 = 0, 1024, 1280, 1536, 2560, 4096
IN_PROJ = DT0 + N_SSM_HEADS
IN_PROJ_PAD = DT0 + LANES

TM_PROJ = 512
TM_OUT = 256
TQ_ATT = WIN


def _cparams(sem):
    return pltpu.CompilerParams(dimension_semantics=sem, vmem_limit_bytes=VMEM_LIMIT)


def _const_spec(shape):
    nd = len(shape)
    return pl.BlockSpec(shape, lambda *_: (0,) * nd, pipeline_mode=pl.Buffered(1))


def _spread(a, onehot_bf, terms):
    out, rest = None, a
    for _ in range(terms):
        piece = rest.astype(BF16)
        part = jnp.dot(piece, onehot_bf, preferred_element_type=F32)
        out = part if out is None else out + part
        rest = rest - piece.astype(F32)
    return out


def _rms(x, g):
    ms = jnp.mean(x * x, axis=-1, keepdims=True)
    return x * lax.rsqrt(ms + EPS) * g


def _silu(x):
    return x * (1.0 / (1.0 + jnp.exp(-x)))


def _inproj_kernel(x_ref, g_ref, w_ref, q_ref, k_ref, v_ref, z_ref, xbc_ref, dt_ref):
    xn = _rms(x_ref[...], g_ref[...]).astype(BF16)

    def mm(lo, hi):
        return jnp.dot(xn, w_ref[:, lo:hi], preferred_element_type=F32)

    q_ref[...] = (mm(Q0, K0) * ATT_SCALE).astype(BF16)
    k_ref[...] = mm(K0, V0)
    v_ref[...] = mm(V0, Z0)
    z_ref[...] = mm(Z0, X0)
    xbc_ref[...] = mm(X0, DT0)
    dt_ref[...] = mm(DT0, IN_PROJ_PAD)


def _inproj(x2d, g, w_bf, tm):
    m, d = x2d.shape
    widths = (ATT_WIDTH, KV_WIDTH, KV_WIDTH, SSM_WIDTH, CONV_DIM, LANES)
    dtypes = (BF16, F32, F32, F32, F32, F32)
    return pl.pallas_call(
        _inproj_kernel,
        grid=(m // tm,),
        in_specs=[pl.BlockSpec((tm, d), lambda i: (i, 0)),
                  _const_spec((1, d)),
                  _const_spec((d, IN_PROJ_PAD))],
        out_specs=[pl.BlockSpec((tm, w), lambda i: (i, 0)) for w in widths],
        out_shape=[jax.ShapeDtypeStruct((m, w), t) for w, t in zip(widths, dtypes)],
        compiler_params=_cparams(("parallel",)),
        name=f"inproj_{tm}",
    )(x2d, g.reshape(1, d), w_bf)


def _deinterleave_matrix(n, dil):
    p = np.zeros((n, n), np.float32)
    src = np.arange(n)
    p[(src % dil) * (n // dil) + src // dil, src] = 1.0
    return p


def _inproj_prompt_kernel(x_ref, g_ref, w_ref, p_mid_ref, p_far_ref, k_ref, v_ref, z_ref, xbc_ref, dt_ref,
                          q0_ref, kv0_ref, q1_ref, kv1_ref, q2_ref, kv2_ref):
    tm = x_ref.shape[0]
    xn = _rms(x_ref[...], g_ref[...]).astype(BF16)

    def mm(lo, hi):
        return jnp.dot(xn, w_ref[:, lo:hi], preferred_element_type=F32)

    q = (mm(Q0, K0) * ATT_SCALE).astype(BF16)
    k = mm(K0, V0)
    v = mm(V0, Z0)
    k_ref[...] = k
    v_ref[...] = v
    z_ref[...] = mm(Z0, X0)
    xbc_ref[...] = mm(X0, DT0)
    dt_ref[...] = mm(DT0, IN_PROJ_PAD)
    kv = jnp.concatenate([k, v], axis=-1).astype(BF16)
    q0_ref[0, 0] = q
    kv0_ref[0, 0] = kv
    for dil, p_ref, qd_ref, kvd_ref in ((DILATIONS[1], p_mid_ref, q1_ref, kv1_ref),
                                        (DILATIONS[2], p_far_ref, q2_ref, kv2_ref)):
        qp = jnp.dot(p_ref[...], q, preferred_element_type=F32).astype(BF16)
        kvp = jnp.dot(p_ref[...], kv, preferred_element_type=F32).astype(BF16)
        rows = tm // dil
        for r in range(dil):
            qd_ref[0, r] = qp[r * rows:(r + 1) * rows]
            kvd_ref[0, r] = kvp[r * rows:(r + 1) * rows]


def _inproj_prompt(x2d, g, w_bf, batch):
    m, d = x2d.shape
    tm = TM_PROJ
    seq = m // batch
    per_b = seq // tm
    assert DILATIONS[0] == 1 and seq % tm == 0 and tm % (max(DILATIONS) * 2 * SUBLANES) == 0
    widths = (KV_WIDTH, KV_WIDTH, SSM_WIDTH, CONV_DIM, LANES)
    rows = lambda w: pl.BlockSpec((tm, w), lambda i: (i, 0))
    split = lambda dil, w: pl.BlockSpec((1, dil, tm // dil, w), lambda i: (i // per_b, 0, i % per_b, 0))
    branch_specs, branch_shapes = [], []
    for dil in DILATIONS:
        for w in (ATT_WIDTH, 2 * KV_WIDTH):
            branch_specs.append(split(dil, w))
            branch_shapes.append(jax.ShapeDtypeStruct((batch, dil, seq // dil, w), BF16))
    perms = [jnp.asarray(_deinterleave_matrix(tm, dil), BF16) for dil in DILATIONS[1:]]
    return pl.pallas_call(
        _inproj_prompt_kernel,
        grid=(m // tm,),
        in_specs=[rows(d), _const_spec((1, d)), _const_spec((d, IN_PROJ_PAD)),
                  _const_spec((tm, tm)), _const_spec((tm, tm))],
        out_specs=[rows(w) for w in widths] + branch_specs,
        out_shape=[jax.ShapeDtypeStruct((m, w), F32) for w in widths] + branch_shapes,
        compiler_params=_cparams(("parallel",)),
        name="inproj_prompt",
    )(x2d, g.reshape(1, d), w_bf, *perms)


def _attn_kernel(q_ref, kvp_ref, kvc_ref, bias_ref, o_ref, lse_ref):
    first = pl.program_id(2) == 0
    lane = lax.broadcasted_iota(jnp.int32, (1, 2 * WIN), 1)
    prev_mask = jnp.where(jnp.logical_and(first, lane < WIN), NEG, 0.0)
    head_lane = lax.broadcasted_iota(jnp.int32, (WIN, LANES), 1)
    for blk in range(TQ_ATT // WIN):
        rows = slice(blk * WIN, (blk + 1) * WIN)
        lse_tile = jnp.zeros((WIN, LANES), F32)
        for kvh in range(N_KV_HEADS):
            ks = slice(kvh * HEAD_DIM, (kvh + 1) * HEAD_DIM)
            vs = slice(KV_WIDTH + kvh * HEAD_DIM, KV_WIDTH + (kvh + 1) * HEAD_DIM)
            if blk == 0:
                kw = jnp.concatenate([kvp_ref[0, 0, :, ks], kvc_ref[0, 0, 0:WIN, ks]], axis=0)
                vw = jnp.concatenate([kvp_ref[0, 0, :, vs], kvc_ref[0, 0, 0:WIN, vs]], axis=0)
            else:
                kw = kvc_ref[0, 0, (blk - 1) * WIN:(blk + 1) * WIN, ks]
                vw = kvc_ref[0, 0, (blk - 1) * WIN:(blk + 1) * WIN, vs]
            for pair in range(KV_REP // 2):
                outs = []
                for r in range(2):
                    h = kvh * KV_REP + pair * 2 + r
                    qh = q_ref[0, 0, rows, h * HEAD_DIM:(h + 1) * HEAD_DIM]
                    s = lax.dot_general(qh, kw, (((1,), (1,)), ((), ())), preferred_element_type=F32)
                    s = s + bias_ref[h]
                    if blk == 0:
                        s = s + prev_mask
                    mx = jnp.max(s, axis=-1, keepdims=True)
                    p = jnp.exp(s - mx)
                    l = jnp.sum(p, axis=-1, keepdims=True)
                    o = jnp.dot(p.astype(BF16), vw, preferred_element_type=F32)
                    outs.append(o * (1.0 / l))
                    lse_tile = jnp.where(head_lane == h, mx + jnp.log(l), lse_tile)
                h0 = kvh * KV_REP + pair * 2
                o_ref[0, 0, rows, h0 * HEAD_DIM:(h0 + 2) * HEAD_DIM] = (
                    jnp.concatenate(outs, axis=-1).astype(BF16))
        lse_ref[0, 0, rows, :] = lse_tile


def _attn_branch(q, kv, bias_mat):
    b, dil, sub, _ = q.shape
    assert sub % TQ_ATT == 0
    nb = sub // TQ_ATT
    per_step = TQ_ATT // WIN
    cur = lambda bb, r, i: (bb, r, i, 0)
    prev = lambda bb, r, i: (bb, r, jnp.maximum(i * per_step - 1, 0), 0)
    return pl.pallas_call(
        _attn_kernel,
        grid=(b, dil, nb),
        in_specs=[pl.BlockSpec((1, 1, TQ_ATT, ATT_WIDTH), cur),
                  pl.BlockSpec((1, 1, WIN, 2 * KV_WIDTH), prev),
                  pl.BlockSpec((1, 1, TQ_ATT, 2 * KV_WIDTH), cur),
                  _const_spec((N_ATT_HEADS, WIN, 2 * WIN))],
        out_specs=[pl.BlockSpec((1, 1, TQ_ATT, ATT_WIDTH), cur),
                   pl.BlockSpec((1, 1, TQ_ATT, LANES), cur)],
        out_shape=[jax.ShapeDtypeStruct((b, dil, sub, ATT_WIDTH), BF16),
                   jax.ShapeDtypeStruct((b, dil, sub, LANES), F32)],
        compiler_params=_cparams(("parallel", "parallel", "arbitrary")),
        name=f"attn_dil{dil}",
    )(q, kv, kv, bias_mat)


def _softplus(x):
    return jnp.maximum(x, 0.0) + jnp.log(1.0 + jnp.exp(-jnp.abs(x)))


def _ssd_kernel(xbc_ref, dt_ref, z_ref, cw_ref, cb_ref, dtb_ref, alog_ref, dskip_ref, gn_ref, e_ref,
                y_ref, st_ref, ext_ref, state_ref):
    c = pl.program_id(1)
    L = SSD_CHUNK

    @pl.when(c == 0)
    def _():
        ext_ref[0:SUBLANES, :] = jnp.zeros((SUBLANES, CONV_DIM), F32)
        state_ref[...] = jnp.zeros_like(state_ref)

    ext_ref[SUBLANES:SUBLANES + L, :] = xbc_ref[...]
    acc = cb_ref[...] + ext_ref[SUBLANES:SUBLANES + L, :] * cw_ref[CONV_W - 1:CONV_W, :]
    for i in range(CONV_W - 1):
        off = SUBLANES - (CONV_W - 1) + i
        acc = acc + ext_ref[off:off + L, :] * cw_ref[i:i + 1, :]
    ext_ref[0:SUBLANES, :] = ext_ref[L:L + SUBLANES, :]
    xa = _silu(acc)

    dt = _softplus(dt_ref[...] + dtb_ref[...])
    da = dt * (-jnp.exp(alog_ref[...]))
    row = lax.broadcasted_iota(jnp.int32, (L, L), 0)
    col = lax.broadcasted_iota(jnp.int32, (L, L), 1)
    tri = row >= col
    a_cs = jnp.dot(tri.astype(F32), da, preferred_element_type=F32, precision=HIGHEST)
    a_cs_t = a_cs.T
    expand = e_ref[...]
    acs_full = _spread(a_cs, expand, 3)
    dt_full = _spread(dt, expand, 3)
    exp_acs = jnp.exp(acs_full)
    a_last = acs_full[L - 1:L, :]
    exp_last = exp_acs[L - 1:L, :]
    xs = xa[:, :SSM_WIDTH]
    xdt = xs * dt_full
    xw = xdt * jnp.exp(a_last - acs_full)

    for g in range(N_SSM_GROUPS):
        b0 = SSM_WIDTH + g * D_STATE
        c0 = SSM_WIDTH + N_SSM_GROUPS * D_STATE + g * D_STATE
        bg_t = xa[:, b0:b0 + D_STATE].T.astype(BF16)
        cg = xa[:, c0:c0 + D_STATE].astype(BF16)
        gram = jnp.dot(cg, bg_t, preferred_element_type=F32)
        for hh in range(HEADS_PER_GROUP):
            h = g * HEADS_PER_GROUP + hh
            hs = slice(h * SSM_HEAD_DIM, (h + 1) * SSM_HEAD_DIM)
            seg = jnp.where(tri, a_cs[:, h:h + 1] - a_cs_t[h:h + 1, :], NEG)
            scores = (gram * jnp.exp(seg)).astype(BF16)
            y_diag = jnp.dot(scores, xdt[:, hs].astype(BF16), preferred_element_type=F32)
            st = state_ref[h]
            y_off = jnp.dot(cg, st.astype(BF16), preferred_element_type=F32) * exp_acs[:, hs]
            y_ref[:, hs] = y_diag + y_off
            state_ref[h] = exp_last[:, hs] * st + jnp.dot(bg_t, xw[:, hs].astype(BF16),
                                                          preferred_element_type=F32)

    y = y_ref[...] + dskip_ref[...] * xs
    u = y * _silu(z_ref[...])
    gw = SSM_WIDTH // N_SSM_GROUPS
    parts = []
    for g in range(N_SSM_GROUPS):
        ug = u[:, g * gw:(g + 1) * gw]
        parts.append(ug * lax.rsqrt(jnp.mean(ug * ug, axis=-1, keepdims=True) + EPS))
    y_ref[...] = jnp.concatenate(parts, axis=-1) * gn_ref[...]

    @pl.when(c == pl.num_programs(1) - 1)
    def _():
        st_ref[0] = state_ref[...]


def _head_expand():
    e = np.zeros((LANES, SSM_WIDTH), np.float32)
    for h in range(N_SSM_HEADS):
        e[h, h * SSM_HEAD_DIM:(h + 1) * SSM_HEAD_DIM] = 1.0
    return jnp.asarray(e)


def _pad_lanes(v):
    return jnp.pad(v.astype(F32), (0, LANES - v.shape[0])).reshape(1, LANES)


def _ssd_prompt(xbc, dt_raw, z, conv_w, conv_b, dt_bias, a_log, d_skip, ssd_norm, batch):
    m = xbc.shape[0]
    nc = m // batch // SSD_CHUNK
    L = SSD_CHUNK
    rows = lambda b, c: (b * nc + c, 0)
    y, st = pl.pallas_call(
        _ssd_kernel,
        grid=(batch, nc),
        in_specs=[pl.BlockSpec((L, CONV_DIM), rows),
                  pl.BlockSpec((L, LANES), rows),
                  pl.BlockSpec((L, SSM_WIDTH), rows),
                  _const_spec((CONV_W, CONV_DIM)),
                  _const_spec((1, CONV_DIM)),
                  _const_spec((1, LANES)),
                  _const_spec((1, LANES)),
                  _const_spec((1, SSM_WIDTH)),
                  _const_spec((1, SSM_WIDTH)),
                  _const_spec((LANES, SSM_WIDTH))],
        out_specs=[pl.BlockSpec((L, SSM_WIDTH), rows),
                   pl.BlockSpec((1, N_SSM_HEADS, D_STATE, SSM_HEAD_DIM), lambda b, c: (b, 0, 0, 0))],
        out_shape=[jax.ShapeDtypeStruct((m, SSM_WIDTH), F32),
                   jax.ShapeDtypeStruct((batch, N_SSM_HEADS, D_STATE, SSM_HEAD_DIM), F32)],
        scratch_shapes=[pltpu.VMEM((SUBLANES + L, CONV_DIM), F32),
                        pltpu.VMEM((N_SSM_HEADS, D_STATE, SSM_HEAD_DIM), F32)],
        compiler_params=_cparams(("parallel", "arbitrary")),
        name="ssd_prompt",
    )(xbc, dt_raw, z, conv_w, conv_b.reshape(1, CONV_DIM), _pad_lanes(dt_bias), _pad_lanes(a_log),
      jnp.repeat(d_skip, SSM_HEAD_DIM).reshape(1, SSM_WIDTH), ssd_norm.reshape(1, SSM_WIDTH),
      _head_expand().astype(BF16))
    return y, jnp.swapaxes(st, -1, -2)


def _router(xn_bf, wr_ref, br_ref):
    return jnp.dot(xn_bf, wr_ref[...], preferred_element_type=F32) + br_ref[...]


def _outproj_kernel(o0, l0, o1, l1, o2, l2, u1_ref, u2_ref, e_ref, ssm_ref, x_ref, w_ref, g_ref, wr_ref,
                    br_ref, x1_ref, xn_ref, lg_ref):
    def natural(o_ref, l_ref, u_ref):
        dil = o_ref.shape[1]
        if dil == 1:
            return o_ref[0, 0].astype(F32), l_ref[0, 0]
        o_cat = jnp.concatenate([o_ref[0, r] for r in range(dil)], axis=0)
        l_cat = jnp.concatenate([l_ref[0, r] for r in range(dil)], axis=0)
        hi = l_cat.astype(BF16)
        lo = (l_cat - hi.astype(F32)).astype(BF16)
        u = u_ref[...]
        return (jnp.dot(u, o_cat, preferred_element_type=F32),
                jnp.dot(u, hi, preferred_element_type=F32) + jnp.dot(u, lo, preferred_element_type=F32))

    branches = [natural(o0, l0, None), natural(o1, l1, u1_ref), natural(o2, l2, u2_ref)]
    lses = [_spread(l, e_ref[...], 2) for _, l in branches]
    mx = functools.reduce(jnp.maximum, lses)
    ws = [jnp.exp(l - mx) for l in lses]
    num = functools.reduce(jnp.add, [w * o for w, (o, _) in zip(ws, branches)])
    att = num * (1.0 / functools.reduce(jnp.add, ws))
    y = jnp.dot(att.astype(BF16), w_ref[:ATT_WIDTH, :], preferred_element_type=F32)
    y = y + jnp.dot(ssm_ref[...].astype(BF16), w_ref[ATT_WIDTH:, :], preferred_element_type=F32)
    x1 = x_ref[...] + y
    x1_ref[...] = x1
    xn = _rms(x1, g_ref[...]).astype(BF16)
    xn_ref[...] = xn
    lg_ref[...] = _router(xn, wr_ref, br_ref)


def _outproj_prompt(att_parts, ssm, x2d, w_bf, g, wr, br):
    m, d = x2d.shape
    mix = w_bf.shape[0]
    tm = TM_OUT
    batch = att_parts[0][0].shape[0]
    per_b = m // batch // tm
    assert tm % (max(DILATIONS) * 2 * SUBLANES) == 0
    row = lambda w: pl.BlockSpec((tm, w), lambda i: (i, 0))
    split = lambda dil, w: pl.BlockSpec((1, dil, tm // dil, w), lambda i: (i // per_b, 0, i % per_b, 0))
    branch_specs = [split(o.shape[1], w) for o, _ in att_parts for w in (ATT_WIDTH, LANES)]
    interleave = [jnp.asarray(_deinterleave_matrix(tm, dil).T, BF16) for dil in DILATIONS[1:]]
    return pl.pallas_call(
        _outproj_kernel,
        grid=(m // tm,),
        in_specs=branch_specs + [_const_spec((tm, tm)), _const_spec((tm, tm)),
                                 _const_spec((LANES, ATT_WIDTH)), row(SSM_WIDTH), row(d),
                                 _const_spec((mix, d)), _const_spec((1, d)),
                                 _const_spec((d, LANES)), _const_spec((1, LANES))],
        out_specs=[row(d), row(d), row(LANES)],
        out_shape=[jax.ShapeDtypeStruct((m, d), F32), jax.ShapeDtypeStruct((m, d), BF16),
                   jax.ShapeDtypeStruct((m, LANES), F32)],
        compiler_params=_cparams(("parallel",)),
        name="outproj_prompt",
    )(*[a for part in att_parts for a in part], *interleave, _head_expand().astype(BF16), ssm, x2d, w_bf,
      g.reshape(1, d), wr, br)


def _attn_sample_kernel(q_ref, kn_ref, vn_ref, kc_ref, vc_ref, bias_ref, bias0_ref, o_ref):
    w = kc_ref.shape[1]
    q = q_ref[0]
    head_grp = lax.broadcasted_iota(jnp.int32, (N_ATT_HEADS, 1), 0) // KV_REP
    kn = kn_ref[0].astype(BF16).astype(F32)
    vn = vn_ref[0].astype(BF16).astype(F32)
    s_self = jnp.sum(q.astype(F32) * kn, axis=-1, keepdims=True)

    def taps(c_ref, dil):
        span = WIN * dil
        rows = c_ref[0, w - span:w, :].astype(BF16)
        if dil == 1:
            return rows
        tap = lax.broadcasted_iota(jnp.int32, (WIN, span), 0)
        col = lax.broadcasted_iota(jnp.int32, (WIN, span), 1)
        pick = jnp.where(col == tap * dil, 1.0, 0.0).astype(BF16)
        return jnp.dot(pick, rows, preferred_element_type=F32).astype(BF16)

    scores, selfs, lses = [], [], []
    for g, dil in enumerate(DILATIONS):
        kk = taps(kc_ref, dil)
        s = jnp.zeros((N_ATT_HEADS, WIN), F32)
        for kvh in range(N_KV_HEADS):
            cs = slice(kvh * HEAD_DIM, (kvh + 1) * HEAD_DIM)
            sk = lax.dot_general(q, kk[:, cs], (((1,), (1,)), ((), ())), preferred_element_type=F32)
            s = jnp.where(head_grp == kvh, sk, s)
        s = s + bias_ref[g]
        s0 = s_self + bias0_ref[g]
        mx = jnp.maximum(jnp.max(s, axis=-1, keepdims=True), s0)
        lse = mx + jnp.log(jnp.sum(jnp.exp(s - mx), axis=-1, keepdims=True) + jnp.exp(s0 - mx))
        scores.append(s); selfs.append(s0); lses.append(lse)
    top = functools.reduce(jnp.maximum, lses)
    es = [jnp.exp(l - top) for l in lses]
    tot = functools.reduce(jnp.add, es)
    o = jnp.zeros((N_ATT_HEADS, HEAD_DIM), F32)
    for s, s0, lse, e, dil in zip(scores, selfs, lses, es, DILATIONS):
        wgt = e / tot
        p = (jnp.exp(s - lse) * wgt).astype(BF16)
        p0 = (jnp.exp(s0 - lse) * wgt).astype(BF16).astype(F32)
        vv = taps(vc_ref, dil)
        o = o + p0 * vn
        for kvh in range(N_KV_HEADS):
            cs = slice(kvh * HEAD_DIM, (kvh + 1) * HEAD_DIM)
            ok = jnp.dot(p, vv[:, cs], preferred_element_type=F32)
            o = o + jnp.where(head_grp == kvh, ok, 0.0)
    o_ref[0] = o


def _attn_sample(q, k_new, v_new, k_cache, v_cache, bias_s, bias0_s):
    n, w = k_cache.shape[0], k_cache.shape[1]
    assert w % (max(DILATIONS) * WIN) == 0
    tok = lambda b: (b, 0, 0)
    head = pl.BlockSpec((1, N_ATT_HEADS, HEAD_DIM), tok)
    window = pl.BlockSpec((1, w, KV_WIDTH), tok)
    return pl.pallas_call(
        _attn_sample_kernel,
        grid=(n,),
        in_specs=[head, head, head, window, window,
                  _const_spec((len(DILATIONS), N_ATT_HEADS, WIN)),
                  _const_spec((len(DILATIONS), N_ATT_HEADS, 1))],
        out_specs=head,
        out_shape=jax.ShapeDtypeStruct((n, N_ATT_HEADS, HEAD_DIM), F32),
        compiler_params=_cparams(("parallel",)),
        name="attn_sample",
    )(q, jnp.repeat(k_new, KV_REP, axis=1), jnp.repeat(v_new, KV_REP, axis=1), k_cache, v_cache,
      bias_s, bias0_s)


def _conv_sample_kernel(xbc_ref, b0_ref, b1_ref, b2_ref, cw_ref, cb_ref, dt_ref, dtb_ref, alog_ref,
                        e_ref, xa_ref, xdt_ref, decay_ref):
    acc = cb_ref[...] + xbc_ref[...] * cw_ref[CONV_W - 1:CONV_W, :]
    for i, buf in enumerate((b0_ref, b1_ref, b2_ref)):
        acc = acc + buf[...] * cw_ref[i:i + 1, :]
    xa = _silu(acc)
    xa_ref[...] = xa
    dt = _softplus(dt_ref[...] + dtb_ref[...])
    decay_ref[...] = jnp.exp(dt * (-jnp.exp(alog_ref[...])))
    dt_full = jnp.dot(dt, e_ref[...], preferred_element_type=F32, precision=HIGHEST)
    xdt_ref[...] = xa[:, :SSM_WIDTH] * dt_full


def _conv_sample(xbc, conv_buf, conv_w, conv_b, dt_raw, dt_bias, a_log):
    n = xbc.shape[0]
    args = (xbc, conv_buf[:, 0], conv_buf[:, 1], conv_buf[:, 2], conv_w, conv_b.reshape(1, CONV_DIM),
            dt_raw, _pad_lanes(dt_bias), _pad_lanes(a_log),
            _head_expand())
    return pl.pallas_call(
        _conv_sample_kernel,
        grid=(1,),
        in_specs=[_const_spec(a.shape) for a in args],
        out_specs=[_const_spec((n, CONV_DIM)), _const_spec((n, SSM_WIDTH)), _const_spec((n, LANES))],
        out_shape=[jax.ShapeDtypeStruct((n, CONV_DIM), F32), jax.ShapeDtypeStruct((n, SSM_WIDTH), F32),
                   jax.ShapeDtypeStruct((n, LANES), F32)],
        compiler_params=_cparams(("arbitrary",)),
        name="conv_sample",
    )(*args)


def _ssm_sample_kernel(xdt_ref, decay_ref, b_ref, c_ref, h0_ref, hn_ref, y_ref):
    for g in range(N_SSM_GROUPS):
        hs = slice(g * HEADS_PER_GROUP, (g + 1) * HEADS_PER_GROUP)
        hn = decay_ref[0, hs] * h0_ref[0, hs] + xdt_ref[0, hs] * b_ref[0, g]
        hn_ref[0, hs] = hn
        c_row = c_ref[0, g].astype(BF16).astype(F32)
        y_ref[0, hs] = jnp.sum(hn.astype(BF16).astype(F32) * c_row, axis=-1, keepdims=True)


def _ssm_sample(xdt, decay, bmat, cmat, h0):
    n = xdt.shape[0]
    p = SSM_HEAD_DIM
    tok4 = lambda b: (b, 0, 0, 0)
    hn, y = pl.pallas_call(
        _ssm_sample_kernel,
        grid=(n,),
        in_specs=[pl.BlockSpec((1, N_SSM_HEADS, p, 1), tok4),
                  pl.BlockSpec((1, N_SSM_HEADS, 1, 1), tok4),
                  pl.BlockSpec((1, N_SSM_GROUPS, 1, D_STATE), tok4),
                  pl.BlockSpec((1, N_SSM_GROUPS, 1, D_STATE), tok4),
                  pl.BlockSpec((1, N_SSM_HEADS, p, D_STATE), tok4)],
        out_specs=[pl.BlockSpec((1, N_SSM_HEADS, p, D_STATE), tok4),
                   pl.BlockSpec((1, N_SSM_HEADS, p, 1), tok4)],
        out_shape=[jax.ShapeDtypeStruct((n, N_SSM_HEADS, p, D_STATE), F32),
                   jax.ShapeDtypeStruct((n, N_SSM_HEADS, p, 1), F32)],
        compiler_params=_cparams(("parallel",)),
        name="ssm_sample",
    )(xdt.reshape(n, N_SSM_HEADS, p, 1), decay[:, :N_SSM_HEADS].reshape(n, N_SSM_HEADS, 1, 1),
      bmat.reshape(n, N_SSM_GROUPS, 1, D_STATE), cmat.reshape(n, N_SSM_GROUPS, 1, D_STATE), h0)
    return hn, y.reshape(n, SSM_WIDTH)


def _outproj_sample_kernel(att_ref, y_ref, xs_ref, z_ref, dskip_ref, gn_ref, x_ref, w_ref, g_ref,
                           wr_ref, br_ref, x1_ref, xn_ref, lg_ref):
    y = y_ref[...] + dskip_ref[...] * xs_ref[...]
    u = y * _silu(z_ref[...])
    gw = SSM_WIDTH // N_SSM_GROUPS
    parts = []
    for g in range(N_SSM_GROUPS):
        ug = u[:, g * gw:(g + 1) * gw]
        parts.append(ug * lax.rsqrt(jnp.mean(ug * ug, axis=-1, keepdims=True) + EPS)
                     * gn_ref[:, g * gw:(g + 1) * gw])
    mix = jnp.concatenate([att_ref[...]] + parts, axis=-1).astype(BF16)
    x1 = x_ref[...] + jnp.dot(mix, w_ref[...], preferred_element_type=F32)
    x1_ref[...] = x1
    xn = _rms(x1, g_ref[...]).astype(BF16)
    xn_ref[...] = xn
    lg_ref[...] = _router(xn, wr_ref, br_ref)


def _outproj_sample(att, y, xs, z, d_skip, ssd_norm, x2d, w_bf, g, wr, br):
    n, d = x2d.shape
    args = (att, y, xs, z, jnp.repeat(d_skip, SSM_HEAD_DIM).reshape(1, SSM_WIDTH),
            ssd_norm.reshape(1, SSM_WIDTH), x2d, w_bf, g.reshape(1, d), wr, br)
    return pl.pallas_call(
        _outproj_sample_kernel,
        grid=(1,),
        in_specs=[_const_spec(a.shape) for a in args],
        out_specs=[_const_spec((n, d)), _const_spec((n, d)), _const_spec((n, LANES))],
        out_shape=[jax.ShapeDtypeStruct((n, d), F32), jax.ShapeDtypeStruct((n, d), BF16),
                   jax.ShapeDtypeStruct((n, LANES), F32)],
        compiler_params=_cparams(("arbitrary",)),
        name="outproj_sample",
    )(*args)


TT = 256
CHUNK_ALIGN = SUBLANES
LROWS = -(-(TT * TOP_K + N_EXPERTS * (CHUNK_ALIGN - 1)) // TT) * TT
CHUNK_SIZES = tuple(1 << b for b in range(TT.bit_length() - 1, CHUNK_ALIGN.bit_length() - 2, -1))
SUB = 256
SUB_TAIL = SUB // 2
RG = 5 * SUB
TF = 256
VMEM_LIMIT_FFN = 60000 * 1024


def _chunk_loop(cnt_ref, loff_ref, dest_ref, tile, fn):
    def per_expert(e, carry):
        idx = tile * N_EXPERTS + e
        n, off, dst = cnt_ref[idx], loff_ref[idx], dest_ref[idx]
        for size in CHUNK_SIZES:
            take = (n & size) != 0

            @pl.when(take)
            def _(off=off, dst=dst, size=size):
                fn(pl.multiple_of(off, CHUNK_ALIGN), pl.multiple_of(dst, CHUNK_ALIGN), size)

            step = jnp.where(take, size, 0)
            off, dst = off + step, dst + step
        return carry

    lax.fori_loop(0, N_EXPERTS, per_expert, 0)


def _dispatch_kernel(cnt_ref, loff_ref, dest_ref, xp_ref, xs_ref, lpos_ref, gate_ref, out_hbm, buf, sem):
    t = pl.program_id(0)
    last = pl.num_programs(0) - 1
    d = xp_ref.shape[1]
    x = jnp.where(t == last, xs_ref[...], xp_ref[...])
    rows = lax.broadcasted_iota(jnp.int32, (LROWS, TT), 0)
    onehot = jnp.zeros((LROWS, TT), F32)
    wcol = jnp.zeros((LROWS, 1), F32)
    for k in range(TOP_K):
        hit = jnp.where(rows == lpos_ref[0, k:k + 1, :], 1.0, 0.0)
        onehot = onehot + hit
        wcol = wcol + jnp.sum(hit * gate_ref[0, k:k + 1, :], axis=-1, keepdims=True)
    tile_buf = buf.at[t % 2]
    tile_buf[:, 0:d] = jnp.dot(onehot.astype(BF16), x, preferred_element_type=F32)
    tile_buf[:, d:d + LANES] = jnp.broadcast_to(wcol, (LROWS, LANES))

    def copy(tile, off, dst, size):
        return pltpu.make_async_copy(buf.at[tile % 2, pl.ds(off, size)], out_hbm.at[pl.ds(dst, size)],
                                     sem.at[tile % 2])

    _chunk_loop(cnt_ref, loff_ref, dest_ref, t, lambda o, g, s: copy(t, o, g, s).start())

    @pl.when(t > 0)
    def _():
        _chunk_loop(cnt_ref, loff_ref, dest_ref, t - 1, lambda o, g, s: copy(t - 1, o, g, s).wait())

    @pl.when(t == last)
    def _():
        _chunk_loop(cnt_ref, loff_ref, dest_ref, t, lambda o, g, s: copy(t, o, g, s).wait())


def _dispatch(tabs, xn_p, xn_s, lpos_t, gate_t, n_rows):
    nt = lpos_t.shape[0]
    d = xn_p.shape[1]
    last_p = xn_p.shape[0] // TT - 1
    grid_spec = pltpu.PrefetchScalarGridSpec(
        num_scalar_prefetch=3,
        grid=(nt,),
        in_specs=[pl.BlockSpec((TT, d), lambda t, *_: (jnp.minimum(t, last_p), 0)),
                  pl.BlockSpec((TT, d), lambda t, *_: (0, 0)),
                  pl.BlockSpec((1, TOP_K, TT), lambda t, *_: (t, 0, 0)),
                  pl.BlockSpec((1, TOP_K, TT), lambda t, *_: (t, 0, 0))],
        out_specs=pl.BlockSpec(memory_space=pl.ANY),
        scratch_shapes=[pltpu.VMEM((2, LROWS, d + LANES), F32), pltpu.SemaphoreType.DMA((2,))],
    )
    return pl.pallas_call(
        _dispatch_kernel,
        grid_spec=grid_spec,
        out_shape=jax.ShapeDtypeStruct((n_rows, d + LANES), F32),
        compiler_params=_cparams(("arbitrary",)),
        name="moe_dispatch",
    )(*tabs, xn_p, xn_s, lpos_t, gate_t)


def _ffn_kernel(ge_ref, gs_ref, gn_ref, gt_ref, ng_ref, xs_hbm, wgu_hbm, wd_hbm, bgu_ref, bdn_ref, out_hbm,
                xbuf, acc, ostage, wg_st, wu_st, wd_st, wg_bf, wu_bf, wd_bf, sem_w, sem_x, sem_o):
    d = acc.shape[1]
    d_ff = wd_hbm.shape[1]
    nf = d_ff // TF
    n_groups = ng_ref[0]
    total = n_groups * nf

    def w_copies(s, slot):
        g = s // nf
        f = s - g * nf
        e = ge_ref[g]
        c0 = pl.multiple_of(f * TF, TF)
        return (pltpu.make_async_copy(wgu_hbm.at[e, :, pl.ds(c0, TF)], wg_st.at[slot], sem_w.at[slot, 0]),
                pltpu.make_async_copy(wgu_hbm.at[e, :, pl.ds(d_ff + c0, TF)], wu_st.at[slot], sem_w.at[slot, 1]),
                pltpu.make_async_copy(wd_hbm.at[e, pl.ds(c0, TF), :], wd_st.at[slot], sem_w.at[slot, 2]))

    def x_copy(g, j, size=SUB):
        r0 = pl.multiple_of(j * SUB, SUB)
        return pltpu.make_async_copy(xs_hbm.at[pl.ds(pl.multiple_of(gs_ref[g] + r0, CHUNK_ALIGN), size)],
                                     xbuf.at[g % 2, pl.ds(r0, size)], sem_x)

    def o_copy(g, j, size=SUB):
        r0 = pl.multiple_of(j * SUB, SUB)
        return pltpu.make_async_copy(ostage.at[j % 2, pl.ds(0, size)],
                                     out_hbm.at[pl.ds(pl.multiple_of(gs_ref[g] + r0, CHUNK_ALIGN), size)],
                                     sem_o.at[j % 2])

    def loop(n, fn):
        lax.fori_loop(0, n, lambda j, c: (fn(j), c)[1], 0)

    def group_rows(g, op):
        loop(gn_ref[g], lambda j: op(x_copy(g, j)))

        @pl.when(gt_ref[g] == 1)
        def _():
            op(x_copy(g, gn_ref[g], SUB_TAIL))

    def drain_stores(g):
        n = gn_ref[g]
        tail = gt_ref[g]

        @pl.when(tail == 1)
        def _():
            o_copy(g, n, SUB_TAIL).wait()

        @pl.when(n >= 1)
        def _():
            o_copy(g, n - 1).wait()

        @pl.when(jnp.logical_and(n >= 2, tail == 0))
        def _():
            o_copy(g, n - 2).wait()

    @pl.when(total > 0)
    def _():
        for c in w_copies(0, 0):
            c.start()
        group_rows(0, lambda c: c.start())

    def item(s, carry):
        slot = s % 2
        g = s // nf
        f = s - g * nf
        e = ge_ref[g]
        nsub = gn_ref[g]
        xg = xbuf.at[g % 2]

        @pl.when(s + 1 < total)
        def _():
            for c in w_copies(s + 1, 1 - slot):
                c.start()

        @pl.when(f == 0)
        def _():
            group_rows(g, lambda c: c.wait())

        @pl.when(jnp.logical_and(f == 1, g + 1 < n_groups))
        def _():
            group_rows(g + 1, lambda c: c.start())

        for c in w_copies(s, slot):
            c.wait()
        bg = bgu_ref[pl.ds(e * 2 * nf + f, 1), :]
        bu = bgu_ref[pl.ds(e * 2 * nf + nf + f, 1), :]

        def sub_block(j, phase, cast=False, size=SUB):
            rs = pl.ds(pl.multiple_of(j * SUB, SUB), size)
            x = xg[rs, 0:d].astype(BF16)
            if cast:
                wg, wu, wd = (st[slot].astype(BF16) for st in (wg_st, wu_st, wd_st))
                wg_bf[...], wu_bf[...], wd_bf[...] = wg, wu, wd
            else:
                wg, wu, wd = wg_bf[...], wu_bf[...], wd_bf[...]
            hg = jnp.dot(x, wg, preferred_element_type=F32) + bg
            hu = jnp.dot(x, wu, preferred_element_type=F32) + bu
            gg = jnp.minimum(hg, SWIGLU_LIMIT)
            uu = jnp.clip(hu, -SWIGLU_LIMIT, SWIGLU_LIMIT)
            act = gg * (1.0 / (1.0 + jnp.exp(-SWIGLU_ALPHA * gg))) * (uu + 1.0)
            part = jnp.dot(act.astype(BF16), wd, preferred_element_type=F32)
            if phase == "first":
                acc[rs, :] = part + bdn_ref[pl.ds(e, 1), :]
            elif phase == "middle":
                acc[rs, :] += part
            else:
                @pl.when(j >= 2)
                def _():
                    o_copy(g, j - 2).wait()

                ostage[j % 2, 0:size] = (acc[rs, :] + part) * xg[rs, d:d + 1]
                o_copy(g, j, size).start()

        tail = gt_ref[g] == 1

        def storing_blocks():
            sub_block(jnp.int32(0), "last", cast=True)

            def pair(p):
                sub_block(2 * p + 1, "last")
                sub_block(2 * p + 2, "last")

            loop((nsub - 1) // 2, pair)

            @pl.when(nsub % 2 == 0)
            def _():
                sub_block(nsub - 1, "last")

            @pl.when(tail)
            def _():
                sub_block(nsub, "last", size=SUB_TAIL)

        def accumulating_blocks(phase):
            @pl.when(nsub < 2)
            def _():
                sub_block(jnp.int32(0), phase, cast=True)

                @pl.when(tail)
                def _():
                    sub_block(jnp.int32(1), phase, size=SUB_TAIL)

            @pl.when(nsub >= 2)
            def _():
                sub_block(jnp.int32(0), phase, cast=True, size=2 * SUB)
                rest = 2 * (nsub - 2) + gt_ref[g]
                for units in range(1, (RG - 2 * SUB) // SUB_TAIL + 1):
                    @pl.when(rest == units)
                    def _(units=units):
                        sub_block(jnp.int32(2), phase, size=units * SUB_TAIL)

        @pl.when(f == 0)
        def _():
            accumulating_blocks("first")

        @pl.when(jnp.logical_and(f > 0, f < nf - 1))
        def _():
            accumulating_blocks("middle")

        @pl.when(f == nf - 1)
        def _():
            @pl.when(g > 0)
            def _():
                drain_stores(g - 1)

            storing_blocks()

        return carry

    lax.fori_loop(0, total, item, 0)

    @pl.when(total > 0)
    def _():
        drain_stores(n_groups - 1)


def _moe_ffn(groups, x_sorted, w_gate_up, b_gu, w_down, b_dn):
    n_rows = x_sorted.shape[0]
    _, d_ff, d = w_down.shape
    nf = d_ff // TF
    assert nf >= 2
    bgu2 = b_gu.reshape(N_EXPERTS * 2 * nf, TF)
    grid_spec = pltpu.PrefetchScalarGridSpec(
        num_scalar_prefetch=5,
        grid=(1,),
        in_specs=[pl.BlockSpec(memory_space=pl.ANY), pl.BlockSpec(memory_space=pl.ANY),
                  pl.BlockSpec(memory_space=pl.ANY),
                  pl.BlockSpec(bgu2.shape, lambda i, *_: (0, 0), pipeline_mode=pl.Buffered(1)),
                  pl.BlockSpec(b_dn.shape, lambda i, *_: (0, 0), pipeline_mode=pl.Buffered(1))],
        out_specs=pl.BlockSpec(memory_space=pl.ANY),
        scratch_shapes=[pltpu.VMEM((2, RG, d + LANES), F32), pltpu.VMEM((RG, d), F32),
                        pltpu.VMEM((2, SUB, d), F32),
                        pltpu.VMEM((2, d, TF), F32), pltpu.VMEM((2, d, TF), F32), pltpu.VMEM((2, TF, d), F32),
                        pltpu.VMEM((d, TF), BF16), pltpu.VMEM((d, TF), BF16), pltpu.VMEM((TF, d), BF16),
                        pltpu.SemaphoreType.DMA((2, 3)), pltpu.SemaphoreType.DMA(()),
                        pltpu.SemaphoreType.DMA((2,))],
    )
    return pl.pallas_call(
        _ffn_kernel,
        grid_spec=grid_spec,
        out_shape=jax.ShapeDtypeStruct((n_rows, d), F32),
        compiler_params=pltpu.CompilerParams(dimension_semantics=("arbitrary",),
                                             vmem_limit_bytes=VMEM_LIMIT_FFN),
        name="moe_ffn",
    )(*groups, x_sorted, w_gate_up, w_down, bgu2, b_dn)


def _route(logits, m_pad):
    m = logits.shape[0]
    nt = m_pad // TT
    top_v, top_i = lax.top_k(logits, TOP_K)
    gate = jnp.pad(jax.nn.softmax(top_v, axis=-1), ((0, m_pad - m), (0, 0)))
    top_i = jnp.pad(top_i.astype(jnp.int32), ((0, m_pad - m), (0, 0)), constant_values=-1)
    chosen = (top_i[:, :, None] == jnp.arange(N_EXPERTS, dtype=jnp.int32)).astype(jnp.int32)
    tiles = chosen.sum(axis=1).reshape(nt, TT, N_EXPERTS)
    cnt = (tiles.sum(axis=1) + CHUNK_ALIGN - 1) // CHUNK_ALIGN * CHUNK_ALIGN
    loff = jnp.cumsum(cnt, axis=1) - cnt
    seg = cnt.sum(axis=0)
    seg_start = jnp.cumsum(seg) - seg
    dest = seg_start[None, :] + jnp.cumsum(cnt, axis=0) - cnt
    before = jnp.asarray(np.tril(np.ones((TT, TT), np.float32), -1), BF16)
    rank = jnp.einsum('ij,tje->tie', before, tiles.astype(BF16),
                      preferred_element_type=F32).astype(jnp.int32)
    lpos_all = (loff[:, None, :] + rank).reshape(m_pad, N_EXPERTS)
    lpos = (lpos_all[:, None, :] * chosen).sum(axis=-1)
    lpos = jnp.where(top_i >= 0, lpos, -1).astype(jnp.int32)
    n_rows_bound = nt * LROWS
    n_grp_max = n_rows_bound // RG + N_EXPERTS
    grp = (seg + RG - 1) // RG
    grp_end = jnp.cumsum(grp)
    gi = jnp.arange(n_grp_max, dtype=jnp.int32)
    g_exp = jnp.minimum((gi[:, None] >= grp_end[None, :]).sum(axis=1), N_EXPERTS - 1).astype(jnp.int32)
    within = gi - (grp_end[g_exp] - grp[g_exp])
    g_start = (seg_start[g_exp] + within * RG).astype(jnp.int32)
    g_rows = jnp.clip(seg[g_exp] - within * RG, 0, RG)
    g_rows = jnp.where(gi < grp_end[-1], g_rows, 0)
    n_full = g_rows // SUB
    rest = g_rows - n_full * SUB
    g_tail = (rest > 0) & (rest <= SUB_TAIL) & (n_full >= 1)
    g_nsub = n_full + ((rest > 0) & ~g_tail)
    groups = (g_exp, g_start, g_nsub.astype(jnp.int32), g_tail.astype(jnp.int32),
              grp_end[-1:].astype(jnp.int32))
    tabs = tuple(a.reshape(-1).astype(jnp.int32) for a in (cnt, loff, dest))
    return tabs, lpos, gate, groups, n_rows_bound + SUB


def _combine_kernel(cnt_ref, loff_ref, dest_ref, src_hbm, lpos_ref, x1p_ref, x1s_ref, g_ref,
                    yp_ref, ys_ref, buf, sem):
    t = pl.program_id(0)
    last = pl.num_programs(0) - 1

    def copy(tile, off, src, size):
        return pltpu.make_async_copy(src_hbm.at[pl.ds(src, size)], buf.at[tile % 2, pl.ds(off, size)],
                                     sem.at[tile % 2])

    @pl.when(t == 0)
    def _():
        buf[...] = jnp.zeros_like(buf)
        _chunk_loop(cnt_ref, loff_ref, dest_ref, t, lambda o, g, s: copy(t, o, g, s).start())

    @pl.when(t < last)
    def _():
        _chunk_loop(cnt_ref, loff_ref, dest_ref, t + 1, lambda o, g, s: copy(t + 1, o, g, s).start())

    _chunk_loop(cnt_ref, loff_ref, dest_ref, t, lambda o, g, s: copy(t, o, g, s).wait())
    cols = lax.broadcasted_iota(jnp.int32, (TT, LROWS), 1)
    sel = jnp.zeros((TT, LROWS), F32)
    for k in range(TOP_K):
        sel = sel + jnp.where(cols == lpos_ref[:, k:k + 1], 1.0, 0.0)
    f = jnp.dot(sel.astype(BF16), buf[t % 2].astype(BF16), preferred_element_type=F32)

    @pl.when(t < last)
    def _():
        yp_ref[...] = _rms(x1p_ref[...] + f, g_ref[...])

    @pl.when(t == last)
    def _():
        ys_ref[...] = _rms(x1s_ref[...] + f, g_ref[...])


def _combine(tabs, out_sorted, lpos, x1p, x1s, g):
    d = x1p.shape[1]
    nt = lpos.shape[0] // TT
    last_p = x1p.shape[0] // TT - 1
    prompt = pl.BlockSpec((TT, d), lambda t, *_: (jnp.minimum(t, last_p), 0))
    sample = pl.BlockSpec((TT, d), lambda t, *_: (0, 0))
    grid_spec = pltpu.PrefetchScalarGridSpec(
        num_scalar_prefetch=3,
        grid=(nt,),
        in_specs=[pl.BlockSpec(memory_space=pl.ANY),
                  pl.BlockSpec((TT, TOP_K), lambda t, *_: (t, 0)),
                  prompt, sample,
                  pl.BlockSpec((1, d), lambda t, *_: (0, 0))],
        out_specs=[prompt, sample],
        scratch_shapes=[pltpu.VMEM((2, LROWS, d), F32), pltpu.SemaphoreType.DMA((2,))],
    )
    return pl.pallas_call(
        _combine_kernel,
        grid_spec=grid_spec,
        out_shape=[jax.ShapeDtypeStruct(x1p.shape, F32), jax.ShapeDtypeStruct((TT, d), F32)],
        compiler_params=_cparams(("arbitrary",)),
        name="moe_combine",
    )(*tabs, out_sorted, lpos, x1p, x1s, g.reshape(1, d))


def _t5_bucket(dist):
    max_exact = N_BUCKETS // 2
    dd = dist.astype(F32)
    large = max_exact + (jnp.log(jnp.maximum(dd, 1.0) / max_exact)
                         / math.log(BUCKET_MAX_DIST / max_exact) * (N_BUCKETS - max_exact)).astype(jnp.int32)
    large = jnp.minimum(large, N_BUCKETS - 1)
    return jnp.where(dist < max_exact, dist, large)


def _bias_tables(rel_bias):
    dist = jnp.asarray(np.arange(N_TAPS)[None, :] * np.array(DILATIONS)[:, None], jnp.int32)
    bias = jnp.transpose(rel_bias[_t5_bucket(dist)], (2, 0, 1)).astype(F32)
    by_branch = jnp.transpose(bias, (1, 0, 2))
    g, h = by_branch.shape[:2]
    row = jnp.concatenate([by_branch[:, :, ::-1], jnp.full((g, h, WIN), NEG, F32)], axis=-1)
    flat = jnp.broadcast_to(row[:, :, None, :], (g, h, WIN, 2 * WIN + 1)).reshape(g, h, -1)
    band = flat[:, :, :WIN * 2 * WIN].reshape(g, h, WIN, 2 * WIN)
    return band, by_branch[:, :, :0:-1], by_branch[:, :, 0:1]


def kernel(x_prompt, x_sample, cache_k_win, cache_v_win, state_conv, state_ssm, rel_bias, attn_norm, w_in, conv_w, conv_b, dt_bias, a_log, d_skip, ssd_norm, w_out, ffn_norm, w_router, b_router, w_gate_up, b_gate_up, w_down, b_down, final_norm):
    bp, tp, d = x_prompt.shape
    bs, ts, _ = x_sample.shape
    depth = w_in.shape[0]
    assert depth == 1 and ts == 1 and tp % (max(DILATIONS) * WIN) == 0
    keep = min(max(DILATIONS) * WIN, tp)
    band, samp, samp0 = _bias_tables(rel_bias)
    l = 0

    xp = x_prompt.reshape(bp * tp, d)
    xs = x_sample.reshape(bs * ts, d)
    w_in_bf = jnp.pad(w_in[l], ((0, 0), (0, IN_PROJ_PAD - IN_PROJ))).astype(BF16)
    w_out_bf = w_out[l].astype(BF16)
    wr = jnp.pad(w_router[l], ((0, 0), (0, LANES - N_EXPERTS))).astype(BF16)
    br = jnp.pad(b_router[l], (0, LANES - N_EXPERTS), constant_values=NEG).reshape(1, LANES)

    k, v, z, xbc, dt_raw, *branch_qkv = _inproj_prompt(xp, attn_norm[l], w_in_bf, bp)
    k3 = k.reshape(bp, tp, KV_WIDTH)
    v3 = v.reshape(bp, tp, KV_WIDTH)
    att_parts = [_attn_branch(branch_qkv[2 * gi], branch_qkv[2 * gi + 1], band[gi])
                 for gi in range(len(DILATIONS))]
    ssm, st_p = _ssd_prompt(xbc, dt_raw, z, conv_w[l], conv_b[l], dt_bias[l], a_log[l], d_skip[l],
                            ssd_norm[l], bp)
    x1p, xnp_, lgp = _outproj_prompt(att_parts, ssm, xp, w_out_bf, ffn_norm[l], wr, br)
    k_win_p = k3[:, tp - keep:].reshape(1, bp, keep, N_KV_HEADS, HEAD_DIM)
    v_win_p = v3[:, tp - keep:].reshape(1, bp, keep, N_KV_HEADS, HEAD_DIM)
    conv_p = xbc.reshape(bp, tp, CONV_DIM)[:, tp - (CONV_W - 1):][None]

    q_s, k_s, v_s, z_s, xbc_s, dt_s = _inproj(xs, attn_norm[l], w_in_bf, bs * ts)
    q_s = q_s.reshape(bs, N_ATT_HEADS, HEAD_DIM)
    k_s = k_s.reshape(bs, N_KV_HEADS, HEAD_DIM)
    v_s = v_s.reshape(bs, N_KV_HEADS, HEAD_DIM)
    wbuf = cache_k_win.shape[2]
    att_s = _attn_sample(q_s, k_s, v_s, cache_k_win[l].reshape(bs, wbuf, KV_WIDTH),
                         cache_v_win[l].reshape(bs, wbuf, KV_WIDTH), samp, samp0)
    xa_s, xdt_s, decay_s = _conv_sample(xbc_s, state_conv[l], conv_w[l], conv_b[l], dt_s, dt_bias[l],
                                        a_log[l])
    nbc = N_SSM_GROUPS * D_STATE
    h_s, y_s = _ssm_sample(xdt_s, decay_s, xa_s[:, SSM_WIDTH:SSM_WIDTH + nbc], xa_s[:, SSM_WIDTH + nbc:],
                           state_ssm[l])
    x1s, xns, lgs = _outproj_sample(att_s.reshape(bs, ATT_WIDTH), y_s, xa_s[:, :SSM_WIDTH], z_s,
                                    d_skip[l], ssd_norm[l], xs, w_out_bf, ffn_norm[l], wr, br)
    conv_s = jnp.concatenate([state_conv[l][:, 1:], xbc_s[:, None]], axis=1)[None]

    n_s = bs * ts
    assert (bp * tp) % TT == 0 and n_s <= TT
    m_pad = bp * tp + TT
    logits = jnp.concatenate([lgp[:, :N_EXPERTS], lgs[:, :N_EXPERTS]], axis=0)
    tabs, lpos, gate, groups, n_rows = _route(logits, m_pad)
    by_tile = lambda a: jnp.transpose(a.reshape(m_pad // TT, TT, TOP_K), (0, 2, 1))
    pad_rows = lambda a: jnp.pad(a, ((0, TT - n_s), (0, 0)))
    x_sorted = _dispatch(tabs, xnp_, pad_rows(xns), by_tile(lpos), by_tile(gate), n_rows)
    out_sorted = _moe_ffn(groups, x_sorted, w_gate_up[l], b_gate_up[l], w_down[l], b_down[l])
    y_p, y_s_out = _combine(tabs, out_sorted, lpos, x1p, pad_rows(x1s), final_norm)
    y_s_out = y_s_out[:n_s]

    return (y_p.reshape(bp, tp, d), y_s_out.reshape(bs, ts, d), k_win_p, v_win_p, conv_p, st_p[None],
            k_s.reshape(1, bs, ts, N_KV_HEADS, HEAD_DIM), v_s.reshape(1, bs, ts, N_KV_HEADS, HEAD_DIM),
            conv_s, h_s[None])
```

```python
import functools
import math

import jax
import jax.numpy as jnp
import numpy as np
from jax import lax
from jax.experimental import pallas as pl
from jax.experimental.pallas import tpu as pltpu

F32 = jnp.float32
BF16 = jnp.bfloat16
HIGHEST = lax.Precision.HIGHEST

LANES = 128
SUBLANES = 8
VMEM_LIMIT = 56 * 1024 * 1024

HEAD_DIM = 64
N_ATT_HEADS = 16
N_KV_HEADS = 4
KV_REP = N_ATT_HEADS // N_KV_HEADS
ATT_WIDTH = N_ATT_HEADS * HEAD_DIM
KV_WIDTH = N_KV_HEADS * HEAD_DIM
DILATIONS = (1, 4, 16)
N_TAPS = 129
WIN = N_TAPS - 1
ATT_SCALE = HEAD_DIM ** -0.5
N_BUCKETS = 32
BUCKET_MAX_DIST = 2048
SSM_HEAD_DIM = 64
N_SSM_HEADS = 16
SSM_WIDTH = N_SSM_HEADS * SSM_HEAD_DIM
N_SSM_GROUPS = 2
HEADS_PER_GROUP = N_SSM_HEADS // N_SSM_GROUPS
D_STATE = 128
CONV_W = 4
CONV_DIM = SSM_WIDTH + 2 * N_SSM_GROUPS * D_STATE
SSD_CHUNK = 128
N_EXPERTS = 32
TOP_K = 4
SWIGLU_LIMIT = 7.0
SWIGLU_ALPHA = 1.702
EPS = 1e-5
NEG = -1e30

Q0, K0, V0, Z0, X0, DT0 = 0, 1024, 1280, 1536, 2560, 4096
IN_PROJ = DT0 + N_SSM_HEADS
IN_PROJ_PAD = DT0 + LANES

TM_PROJ = 512
TM_OUT = 256
TQ_ATT = WIN


def _cparams(sem):
    return pltpu.CompilerParams(dimension_semantics=sem, vmem_limit_bytes=VMEM_LIMIT)


def _const_spec(shape):
    nd = len(shape)
    return pl.BlockSpec(shape, lambda *_: (0,) * nd, pipeline_mode=pl.Buffered(1))


def _spread(a, onehot_bf, terms):
    out, rest = None, a
    for _ in range(terms):
        piece = rest.astype(BF16)
        part = jnp.dot(piece, onehot_bf, preferred_element_type=F32)
        out = part if out is None else out + part
        rest = rest - piece.astype(F32)
    return out


def _rms(x, g):
    ms = jnp.mean(x * x, axis=-1, keepdims=True)
    return x * lax.rsqrt(ms + EPS) * g


def _silu(x):
    return x * (1.0 / (1.0 + jnp.exp(-x)))


def _inproj_kernel(x_ref, g_ref, w_ref, q_ref, k_ref, v_ref, z_ref, xbc_ref, dt_ref):
    xn = _rms(x_ref[...], g_ref[...]).astype(BF16)

    def mm(lo, hi):
        return jnp.dot(xn, w_ref[:, lo:hi], preferred_element_type=F32)

    q_ref[...] = (mm(Q0, K0) * ATT_SCALE).astype(BF16)
    k_ref[...] = mm(K0, V0)
    v_ref[...] = mm(V0, Z0)
    z_ref[...] = mm(Z0, X0)
    xbc_ref[...] = mm(X0, DT0)
    dt_ref[...] = mm(DT0, IN_PROJ_PAD)


def _inproj(x2d, g, w_bf, tm):
    m, d = x2d.shape
    widths = (ATT_WIDTH, KV_WIDTH, KV_WIDTH, SSM_WIDTH, CONV_DIM, LANES)
    dtypes = (BF16, F32, F32, F32, F32, F32)
    return pl.pallas_call(
        _inproj_kernel,
        grid=(m // tm,),
        in_specs=[pl.BlockSpec((tm, d), lambda i: (i, 0)),
                  _const_spec((1, d)),
                  _const_spec((d, IN_PROJ_PAD))],
        out_specs=[pl.BlockSpec((tm, w), lambda i: (i, 0)) for w in widths],
        out_shape=[jax.ShapeDtypeStruct((m, w), t) for w, t in zip(widths, dtypes)],
        compiler_params=_cparams(("parallel",)),
        name=f"inproj_{tm}",
    )(x2d, g.reshape(1, d), w_bf)


def _deinterleave_matrix(n, dil):
    p = np.zeros((n, n), np.float32)
    src = np.arange(n)
    p[(src % dil) * (n // dil) + src // dil, src] = 1.0
    return p


def _inproj_prompt_kernel(x_ref, g_ref, w_ref, p_mid_ref, p_far_ref, k_ref, v_ref, z_ref, xbc_ref, dt_ref,
                          q0_ref, kv0_ref, q1_ref, kv1_ref, q2_ref, kv2_ref):
    tm = x_ref.shape[0]
    xn = _rms(x_ref[...], g_ref[...]).astype(BF16)

    def mm(lo, hi):
        return jnp.dot(xn, w_ref[:, lo:hi], preferred_element_type=F32)

    q = (mm(Q0, K0) * ATT_SCALE).astype(BF16)
    k = mm(K0, V0)
    v = mm(V0, Z0)
    k_ref[...] = k
    v_ref[...] = v
    z_ref[...] = mm(Z0, X0)
    xbc_ref[...] = mm(X0, DT0)
    dt_ref[...] = mm(DT0, IN_PROJ_PAD)
    kv = jnp.concatenate([k, v], axis=-1).astype(BF16)
    q0_ref[0, 0] = q
    kv0_ref[0, 0] = kv
    for dil, p_ref, qd_ref, kvd_ref in ((DILATIONS[1], p_mid_ref, q1_ref, kv1_ref),
                                        (DILATIONS[2], p_far_ref, q2_ref, kv2_ref)):
        qp = jnp.dot(p_ref[...], q, preferred_element_type=F32).astype(BF16)
        kvp = jnp.dot(p_ref[...], kv, preferred_element_type=F32).astype(BF16)
        rows = tm // dil
        for r in range(dil):
            qd_ref[0, r] = qp[r * rows:(r + 1) * rows]
            kvd_ref[0, r] = kvp[r * rows:(r + 1) * rows]


def _inproj_prompt(x2d, g, w_bf, batch):
    m, d = x2d.shape
    tm = TM_PROJ
    seq = m // batch
    per_b = seq // tm
    assert DILATIONS[0] == 1 and seq % tm == 0 and tm % (max(DILATIONS) * 2 * SUBLANES) == 0
    widths = (KV_WIDTH, KV_WIDTH, SSM_WIDTH, CONV_DIM, LANES)
    rows = lambda w: pl.BlockSpec((tm, w), lambda i: (i, 0))
    split = lambda dil, w: pl.BlockSpec((1, dil, tm // dil, w), lambda i: (i // per_b, 0, i % per_b, 0))
    branch_specs, branch_shapes = [], []
    for dil in DILATIONS:
        for w in (ATT_WIDTH, 2 * KV_WIDTH):
            branch_specs.append(split(dil, w))
            branch_shapes.append(jax.ShapeDtypeStruct((batch, dil, seq // dil, w), BF16))
    perms = [jnp.asarray(_deinterleave_matrix(tm, dil), BF16) for dil in DILATIONS[1:]]
    return pl.pallas_call(
        _inproj_prompt_kernel,
        grid=(m // tm,),
        in_specs=[rows(d), _const_spec((1, d)), _const_spec((d, IN_PROJ_PAD)),
                  _const_spec((tm, tm)), _const_spec((tm, tm))],
        out_specs=[rows(w) for w in widths] + branch_specs,
        out_shape=[jax.ShapeDtypeStruct((m, w), F32) for w in widths] + branch_shapes,
        compiler_params=_cparams(("parallel",)),
        name="inproj_prompt",
    )(x2d, g.reshape(1, d), w_bf, *perms)


def _attn_kernel(q_ref, kvp_ref, kvc_ref, bias_ref, o_ref, lse_ref):
    first = pl.program_id(2) == 0
    lane = lax.broadcasted_iota(jnp.int32, (1, 2 * WIN), 1)
    prev_mask = jnp.where(jnp.logical_and(first, lane < WIN), NEG, 0.0)
    head_lane = lax.broadcasted_iota(jnp.int32, (WIN, LANES), 1)
    for blk in range(TQ_ATT // WIN):
        rows = slice(blk * WIN, (blk + 1) * WIN)
        lse_tile = jnp.zeros((WIN, LANES), F32)
        for kvh in range(N_KV_HEADS):
            ks = slice(kvh * HEAD_DIM, (kvh + 1) * HEAD_DIM)
            vs = slice(KV_WIDTH + kvh * HEAD_DIM, KV_WIDTH + (kvh + 1) * HEAD_DIM)
            if blk == 0:
                kw = jnp.concatenate([kvp_ref[0, 0, :, ks], kvc_ref[0, 0, 0:WIN, ks]], axis=0)
                vw = jnp.concatenate([kvp_ref[0, 0, :, vs], kvc_ref[0, 0, 0:WIN, vs]], axis=0)
            else:
                kw = kvc_ref[0, 0, (blk - 1) * WIN:(blk + 1) * WIN, ks]
                vw = kvc_ref[0, 0, (blk - 1) * WIN:(blk + 1) * WIN, vs]
            for pair in range(KV_REP // 2):
                outs = []
                for r in range(2):
                    h = kvh * KV_REP + pair * 2 + r
                    qh = q_ref[0, 0, rows, h * HEAD_DIM:(h + 1) * HEAD_DIM]
                    s = lax.dot_general(qh, kw, (((1,), (1,)), ((), ())), preferred_element_type=F32)
                    s = s + bias_ref[h]
                    if blk == 0:
                        s = s + prev_mask
                    mx = jnp.max(s, axis=-1, keepdims=True)
                    p = jnp.exp(s - mx)
                    l = jnp.sum(p, axis=-1, keepdims=True)
                    o = jnp.dot(p.astype(BF16), vw, preferred_element_type=F32)
                    outs.append(o * (1.0 / l))
                    lse_tile = jnp.where(head_lane == h, mx + jnp.log(l), lse_tile)
                h0 = kvh * KV_REP + pair * 2
                o_ref[0, 0, rows, h0 * HEAD_DIM:(h0 + 2) * HEAD_DIM] = (
                    jnp.concatenate(outs, axis=-1).astype(BF16))
        lse_ref[0, 0, rows, :] = lse_tile


def _attn_branch(q, kv, bias_mat):
    b, dil, sub, _ = q.shape
    assert sub % TQ_ATT == 0
    nb = sub // TQ_ATT
    per_step = TQ_ATT // WIN
    cur = lambda bb, r, i: (bb, r, i, 0)
    prev = lambda bb, r, i: (bb, r, jnp.maximum(i * per_step - 1, 0), 0)
    return pl.pallas_call(
        _attn_kernel,
        grid=(b, dil, nb),
        in_specs=[pl.BlockSpec((1, 1, TQ_ATT, ATT_WIDTH), cur),
                  pl.BlockSpec((1, 1, WIN, 2 * KV_WIDTH), prev),
                  pl.BlockSpec((1, 1, TQ_ATT, 2 * KV_WIDTH), cur),
                  _const_spec((N_ATT_HEADS, WIN, 2 * WIN))],
        out_specs=[pl.BlockSpec((1, 1, TQ_ATT, ATT_WIDTH), cur),
                   pl.BlockSpec((1, 1, TQ_ATT, LANES), cur)],
        out_shape=[jax.ShapeDtypeStruct((b, dil, sub, ATT_WIDTH), BF16),
                   jax.ShapeDtypeStruct((b, dil, sub, LANES), F32)],
        compiler_params=_cparams(("parallel", "parallel", "arbitrary")),
        name=f"attn_dil{dil}",
    )(q, kv, kv, bias_mat)


def _softplus(x):
    return jnp.maximum(x, 0.0) + jnp.log(1.0 + jnp.exp(-jnp.abs(x)))


def _ssd_kernel(xbc_ref, dt_ref, z_ref, cw_ref, cb_ref, dtb_ref, alog_ref, dskip_ref, gn_ref, e_ref,
                y_ref, st_ref, ext_ref, state_ref):
    c = pl.program_id(1)
    L = SSD_CHUNK

    @pl.when(c == 0)
    def _():
        ext_ref[0:SUBLANES, :] = jnp.zeros((SUBLANES, CONV_DIM), F32)
        state_ref[...] = jnp.zeros_like(state_ref)

    ext_ref[SUBLANES:SUBLANES + L, :] = xbc_ref[...]
    acc = cb_ref[...] + ext_ref[SUBLANES:SUBLANES + L, :] * cw_ref[CONV_W - 1:CONV_W, :]
    for i in range(CONV_W - 1):
        off = SUBLANES - (CONV_W - 1) + i
        acc = acc + ext_ref[off:off + L, :] * cw_ref[i:i + 1, :]
    ext_ref[0:SUBLANES, :] = ext_ref[L:L + SUBLANES, :]
    xa = _silu(acc)

    dt = _softplus(dt_ref[...] + dtb_ref[...])
    da = dt * (-jnp.exp(alog_ref[...]))
    row = lax.broadcasted_iota(jnp.int32, (L, L), 0)
    col = lax.broadcasted_iota(jnp.int32, (L, L), 1)
    tri = row >= col
    a_cs = jnp.dot(tri.astype(F32), da, preferred_element_type=F32, precision=HIGHEST)
    a_cs_t = a_cs.T
    expand = e_ref[...]
    acs_full = _spread(a_cs, expand, 3)
    dt_full = _spread(dt, expand, 3)
    exp_acs = jnp.exp(acs_full)
    a_last = acs_full[L - 1:L, :]
    exp_last = exp_acs[L - 1:L, :]
    xs = xa[:, :SSM_WIDTH]
    xdt = xs * dt_full
    xw = xdt * jnp.exp(a_last - acs_full)

    for g in range(N_SSM_GROUPS):
        b0 = SSM_WIDTH + g * D_STATE
        c0 = SSM_WIDTH + N_SSM_GROUPS * D_STATE + g * D_STATE
        bg_t = xa[:, b0:b0 + D_STATE].T.astype(BF16)
        cg = xa[:, c0:c0 + D_STATE].astype(BF16)
        gram = jnp.dot(cg, bg_t, preferred_element_type=F32)
        for hh in range(HEADS_PER_GROUP):
            h = g * HEADS_PER_GROUP + hh
            hs = slice(h * SSM_HEAD_DIM, (h + 1) * SSM_HEAD_DIM)
            seg = jnp.where(tri, a_cs[:, h:h + 1] - a_cs_t[h:h + 1, :], NEG)
            scores = (gram * jnp.exp(seg)).astype(BF16)
            y_diag = jnp.dot(scores, xdt[:, hs].astype(BF16), preferred_element_type=F32)
            st = state_ref[h]
            y_off = jnp.dot(cg, st.astype(BF16), preferred_element_type=F32) * exp_acs[:, hs]
            y_ref[:, hs] = y_diag + y_off
            state_ref[h] = exp_last[:, hs] * st + jnp.dot(bg_t, xw[:, hs].astype(BF16),
                                                          preferred_element_type=F32)

    y = y_ref[...] + dskip_ref[...] * xs
    u = y * _silu(z_ref[...])
    gw = SSM_WIDTH // N_SSM_GROUPS
    parts = []
    for g in range(N_SSM_GROUPS):
        ug = u[:, g * gw:(g + 1) * gw]
        parts.append(ug * lax.rsqrt(jnp.mean(ug * ug, axis=-1, keepdims=True) + EPS))
    y_ref[...] = jnp.concatenate(parts, axis=-1) * gn_ref[...]

    @pl.when(c == pl.num_programs(1) - 1)
    def _():
        st_ref[0] = state_ref[...]


def _head_expand():
    e = np.zeros((LANES, SSM_WIDTH), np.float32)
    for h in range(N_SSM_HEADS):
        e[h, h * SSM_HEAD_DIM:(h + 1) * SSM_HEAD_DIM] = 1.0
    return jnp.asarray(e)


def _pad_lanes(v):
    return jnp.pad(v.astype(F32), (0, LANES - v.shape[0])).reshape(1, LANES)


def _ssd_prompt(xbc, dt_raw, z, conv_w, conv_b, dt_bias, a_log, d_skip, ssd_norm, batch):
    m = xbc.shape[0]
    nc = m // batch // SSD_CHUNK
    L = SSD_CHUNK
    rows = lambda b, c: (b * nc + c, 0)
    y, st = pl.pallas_call(
        _ssd_kernel,
        grid=(batch, nc),
        in_specs=[pl.BlockSpec((L, CONV_DIM), rows),
                  pl.BlockSpec((L, LANES), rows),
                  pl.BlockSpec((L, SSM_WIDTH), rows),
                  _const_spec((CONV_W, CONV_DIM)),
                  _const_spec((1, CONV_DIM)),
                  _const_spec((1, LANES)),
                  _const_spec((1, LANES)),
                  _const_spec((1, SSM_WIDTH)),
                  _const_spec((1, SSM_WIDTH)),
                  _const_spec((LANES, SSM_WIDTH))],
        out_specs=[pl.BlockSpec((L, SSM_WIDTH), rows),
                   pl.BlockSpec((1, N_SSM_HEADS, D_STATE, SSM_HEAD_DIM), lambda b, c: (b, 0, 0, 0))],
        out_shape=[jax.ShapeDtypeStruct((m, SSM_WIDTH), F32),
                   jax.ShapeDtypeStruct((batch, N_SSM_HEADS, D_STATE, SSM_HEAD_DIM), F32)],
        scratch_shapes=[pltpu.VMEM((SUBLANES + L, CONV_DIM), F32),
                        pltpu.VMEM((N_SSM_HEADS, D_STATE, SSM_HEAD_DIM), F32)],
        compiler_params=_cparams(("parallel", "arbitrary")),
        name="ssd_prompt",
    )(xbc, dt_raw, z, conv_w, conv_b.reshape(1, CONV_DIM), _pad_lanes(dt_bias), _pad_lanes(a_log),
      jnp.repeat(d_skip, SSM_HEAD_DIM).reshape(1, SSM_WIDTH), ssd_norm.reshape(1, SSM_WIDTH),
      _head_expand().astype(BF16))
    return y, jnp.swapaxes(st, -1, -2)


def _router(xn_bf, wr_ref, br_ref):
    return jnp.dot(xn_bf, wr_ref[...], preferred_element_type=F32) + br_ref[...]


def _outproj_kernel(o0, l0, o1, l1, o2, l2, u1_ref, u2_ref, e_ref, ssm_ref, x_ref, w_ref, g_ref, wr_ref,
                    br_ref, x1_ref, xn_ref, lg_ref):
    def natural(o_ref, l_ref, u_ref):
        dil = o_ref.shape[1]
        if dil == 1:
            return o_ref[0, 0].astype(F32), l_ref[0, 0]
        o_cat = jnp.concatenate([o_ref[0, r] for r in range(dil)], axis=0)
        l_cat = jnp.concatenate([l_ref[0, r] for r in range(dil)], axis=0)
        hi = l_cat.astype(BF16)
        lo = (l_cat - hi.astype(F32)).astype(BF16)
        u = u_ref[...]
        return (jnp.dot(u, o_cat, preferred_element_type=F32),
                jnp.dot(u, hi, preferred_element_type=F32) + jnp.dot(u, lo, preferred_element_type=F32))

    branches = [natural(o0, l0, None), natural(o1, l1, u1_ref), natural(o2, l2, u2_ref)]
    lses = [_spread(l, e_ref[...], 2) for _, l in branches]
    mx = functools.reduce(jnp.maximum, lses)
    ws = [jnp.exp(l - mx) for l in lses]
    num = functools.reduce(jnp.add, [w * o for w, (o, _) in zip(ws, branches)])
    att = num * (1.0 / functools.reduce(jnp.add, ws))
    y = jnp.dot(att.astype(BF16), w_ref[:ATT_WIDTH, :], preferred_element_type=F32)
    y = y + jnp.dot(ssm_ref[...].astype(BF16), w_ref[ATT_WIDTH:, :], preferred_element_type=F32)
    x1 = x_ref[...] + y
    x1_ref[...] = x1
    xn = _rms(x1, g_ref[...]).astype(BF16)
    xn_ref[...] = xn
    lg_ref[...] = _router(xn, wr_ref, br_ref)


def _outproj_prompt(att_parts, ssm, x2d, w_bf, g, wr, br):
    m, d = x2d.shape
    mix = w_bf.shape[0]
    tm = TM_OUT
    batch = att_parts[0][0].shape[0]
    per_b = m // batch // tm
    assert tm % (max(DILATIONS) * 2 * SUBLANES) == 0
    assert (N_ATT_HEADS, HEAD_DIM) == (N_SSM_HEADS, SSM_HEAD_DIM)
    row = lambda w: pl.BlockSpec((tm, w), lambda i: (i, 0))
    split = lambda dil, w: pl.BlockSpec((1, dil, tm // dil, w), lambda i: (i // per_b, 0, i % per_b, 0))
    branch_specs = [split(o.shape[1], w) for o, _ in att_parts for w in (ATT_WIDTH, LANES)]
    interleave = [jnp.asarray(_deinterleave_matrix(tm, dil).T, BF16) for dil in DILATIONS[1:]]
    return pl.pallas_call(
        _outproj_kernel,
        grid=(m // tm,),
        in_specs=branch_specs + [_const_spec((tm, tm)), _const_spec((tm, tm)),
                                 _const_spec((LANES, ATT_WIDTH)), row(SSM_WIDTH), row(d),
                                 _const_spec((mix, d)), _const_spec((1, d)),
                                 _const_spec((d, LANES)), _const_spec((1, LANES))],
        out_specs=[row(d), row(d), row(LANES)],
        out_shape=[jax.ShapeDtypeStruct((m, d), F32), jax.ShapeDtypeStruct((m, d), BF16),
                   jax.ShapeDtypeStruct((m, LANES), F32)],
        compiler_params=_cparams(("parallel",)),
        name="outproj_prompt",
    )(*[a for part in att_parts for a in part], *interleave, _head_expand().astype(BF16), ssm, x2d, w_bf,
      g.reshape(1, d), wr, br)


def _attn_sample_kernel(q_ref, kn_ref, vn_ref, kc_ref, vc_ref, bias_ref, bias0_ref, o_ref):
    w = kc_ref.shape[1]
    q = q_ref[0]
    head_grp = lax.broadcasted_iota(jnp.int32, (N_ATT_HEADS, 1), 0) // KV_REP
    kn = kn_ref[0].astype(BF16).astype(F32)
    vn = vn_ref[0].astype(BF16).astype(F32)
    s_self = jnp.sum(q.astype(F32) * kn, axis=-1, keepdims=True)

    def taps(c_ref, dil):
        span = WIN * dil
        rows = c_ref[0, w - span:w, :].astype(BF16)
        if dil == 1:
            return rows
        tap = lax.broadcasted_iota(jnp.int32, (WIN, span), 0)
        col = lax.broadcasted_iota(jnp.int32, (WIN, span), 1)
        pick = jnp.where(col == tap * dil, 1.0, 0.0).astype(BF16)
        return jnp.dot(pick, rows, preferred_element_type=F32).astype(BF16)

    scores, selfs, lses = [], [], []
    for g, dil in enumerate(DILATIONS):
        kk = taps(kc_ref, dil)
        s = jnp.zeros((N_ATT_HEADS, WIN), F32)
        for kvh in range(N_KV_HEADS):
            cs = slice(kvh * HEAD_DIM, (kvh + 1) * HEAD_DIM)
            sk = lax.dot_general(q, kk[:, cs], (((1,), (1,)), ((), ())), preferred_element_type=F32)
            s = jnp.where(head_grp == kvh, sk, s)
        s = s + bias_ref[g]
        s0 = s_self + bias0_ref[g]
        mx = jnp.maximum(jnp.max(s, axis=-1, keepdims=True), s0)
        lse = mx + jnp.log(jnp.sum(jnp.exp(s - mx), axis=-1, keepdims=True) + jnp.exp(s0 - mx))
        scores.append(s)
        selfs.append(s0)
        lses.append(lse)
    top = functools.reduce(jnp.maximum, lses)
    es = [jnp.exp(l - top) for l in lses]
    tot = functools.reduce(jnp.add, es)
    o = jnp.zeros((N_ATT_HEADS, HEAD_DIM), F32)
    for s, s0, lse, e, dil in zip(scores, selfs, lses, es, DILATIONS):
        wgt = e / tot
        p = (jnp.exp(s - lse) * wgt).astype(BF16)
        p0 = (jnp.exp(s0 - lse) * wgt).astype(BF16).astype(F32)
        vv = taps(vc_ref, dil)
        o = o + p0 * vn
        for kvh in range(N_KV_HEADS):
            cs = slice(kvh * HEAD_DIM, (kvh + 1) * HEAD_DIM)
            ok = jnp.dot(p, vv[:, cs], preferred_element_type=F32)
            o = o + jnp.where(head_grp == kvh, ok, 0.0)
    o_ref[0] = o


def _attn_sample(q, k_new, v_new, k_cache, v_cache, bias_s, bias0_s):
    n, w = k_cache.shape[0], k_cache.shape[1]
    assert w % (max(DILATIONS) * WIN) == 0
    tok = lambda b: (b, 0, 0)
    head = pl.BlockSpec((1, N_ATT_HEADS, HEAD_DIM), tok)
    window = pl.BlockSpec((1, w, KV_WIDTH), tok)
    return pl.pallas_call(
        _attn_sample_kernel,
        grid=(n,),
        in_specs=[head, head, head, window, window,
                  _const_spec((len(DILATIONS), N_ATT_HEADS, WIN)),
                  _const_spec((len(DILATIONS), N_ATT_HEADS, 1))],
        out_specs=head,
        out_shape=jax.ShapeDtypeStruct((n, N_ATT_HEADS, HEAD_DIM), F32),
        compiler_params=_cparams(("parallel",)),
        name="attn_sample",
    )(q, jnp.repeat(k_new, KV_REP, axis=1), jnp.repeat(v_new, KV_REP, axis=1), k_cache, v_cache,
      bias_s, bias0_s)


def _conv_sample_kernel(xbc_ref, b0_ref, b1_ref, b2_ref, cw_ref, cb_ref, dt_ref, dtb_ref, alog_ref,
                        e_ref, xa_ref, xdt_ref, decay_ref):
    acc = cb_ref[...] + xbc_ref[...] * cw_ref[CONV_W - 1:CONV_W, :]
    for i, buf in enumerate((b0_ref, b1_ref, b2_ref)):
        acc = acc + buf[...] * cw_ref[i:i + 1, :]
    xa = _silu(acc)
    xa_ref[...] = xa
    dt = _softplus(dt_ref[...] + dtb_ref[...])
    decay_ref[...] = jnp.exp(dt * (-jnp.exp(alog_ref[...])))
    dt_full = jnp.dot(dt, e_ref[...], preferred_element_type=F32, precision=HIGHEST)
    xdt_ref[...] = xa[:, :SSM_WIDTH] * dt_full


def _conv_sample(xbc, conv_buf, conv_w, conv_b, dt_raw, dt_bias, a_log):
    n = xbc.shape[0]
    args = (xbc, conv_buf[:, 0], conv_buf[:, 1], conv_buf[:, 2], conv_w, conv_b.reshape(1, CONV_DIM),
            dt_raw, _pad_lanes(dt_bias), _pad_lanes(a_log),
            _head_expand())
    return pl.pallas_call(
        _conv_sample_kernel,
        grid=(1,),
        in_specs=[_const_spec(a.shape) for a in args],
        out_specs=[_const_spec((n, CONV_DIM)), _const_spec((n, SSM_WIDTH)), _const_spec((n, LANES))],
        out_shape=[jax.ShapeDtypeStruct((n, CONV_DIM), F32), jax.ShapeDtypeStruct((n, SSM_WIDTH), F32),
                   jax.ShapeDtypeStruct((n, LANES), F32)],
        compiler_params=_cparams(("arbitrary",)),
        name="conv_sample",
    )(*args)


def _ssm_sample_kernel(xdt_ref, decay_ref, b_ref, c_ref, h0_ref, hn_ref, y_ref):
    for g in range(N_SSM_GROUPS):
        hs = slice(g * HEADS_PER_GROUP, (g + 1) * HEADS_PER_GROUP)
        hn = decay_ref[0, hs] * h0_ref[0, hs] + xdt_ref[0, hs] * b_ref[0, g]
        hn_ref[0, hs] = hn
        c_row = c_ref[0, g].astype(BF16).astype(F32)
        y_ref[0, hs] = jnp.sum(hn.astype(BF16).astype(F32) * c_row, axis=-1, keepdims=True)


def _ssm_sample(xdt, decay, bmat, cmat, h0):
    n = xdt.shape[0]
    p = SSM_HEAD_DIM
    tok4 = lambda b: (b, 0, 0, 0)
    hn, y = pl.pallas_call(
        _ssm_sample_kernel,
        grid=(n,),
        in_specs=[pl.BlockSpec((1, N_SSM_HEADS, p, 1), tok4),
                  pl.BlockSpec((1, N_SSM_HEADS, 1, 1), tok4),
                  pl.BlockSpec((1, N_SSM_GROUPS, 1, D_STATE), tok4),
                  pl.BlockSpec((1, N_SSM_GROUPS, 1, D_STATE), tok4),
                  pl.BlockSpec((1, N_SSM_HEADS, p, D_STATE), tok4)],
        out_specs=[pl.BlockSpec((1, N_SSM_HEADS, p, D_STATE), tok4),
                   pl.BlockSpec((1, N_SSM_HEADS, p, 1), tok4)],
        out_shape=[jax.ShapeDtypeStruct((n, N_SSM_HEADS, p, D_STATE), F32),
                   jax.ShapeDtypeStruct((n, N_SSM_HEADS, p, 1), F32)],
        compiler_params=_cparams(("parallel",)),
        name="ssm_sample",
    )(xdt.reshape(n, N_SSM_HEADS, p, 1), decay[:, :N_SSM_HEADS].reshape(n, N_SSM_HEADS, 1, 1),
      bmat.reshape(n, N_SSM_GROUPS, 1, D_STATE), cmat.reshape(n, N_SSM_GROUPS, 1, D_STATE), h0)
    return hn, y.reshape(n, SSM_WIDTH)


def _outproj_sample_kernel(att_ref, y_ref, xs_ref, z_ref, dskip_ref, gn_ref, x_ref, w_ref, g_ref,
                           wr_ref, br_ref, x1_ref, xn_ref, lg_ref):
    y = y_ref[...] + dskip_ref[...] * xs_ref[...]
    u = y * _silu(z_ref[...])
    gw = SSM_WIDTH // N_SSM_GROUPS
    parts = []
    for g in range(N_SSM_GROUPS):
        ug = u[:, g * gw:(g + 1) * gw]
        parts.append(ug * lax.rsqrt(jnp.mean(ug * ug, axis=-1, keepdims=True) + EPS)
                     * gn_ref[:, g * gw:(g + 1) * gw])
    mix = jnp.concatenate([att_ref[...]] + parts, axis=-1).astype(BF16)
    x1 = x_ref[...] + jnp.dot(mix, w_ref[...], preferred_element_type=F32)
    x1_ref[...] = x1
    xn = _rms(x1, g_ref[...]).astype(BF16)
    xn_ref[...] = xn
    lg_ref[...] = _router(xn, wr_ref, br_ref)


def _outproj_sample(att, y, xs, z, d_skip, ssd_norm, x2d, w_bf, g, wr, br):
    n, d = x2d.shape
    args = (att, y, xs, z, jnp.repeat(d_skip, SSM_HEAD_DIM).reshape(1, SSM_WIDTH),
            ssd_norm.reshape(1, SSM_WIDTH), x2d, w_bf, g.reshape(1, d), wr, br)
    return pl.pallas_call(
        _outproj_sample_kernel,
        grid=(1,),
        in_specs=[_const_spec(a.shape) for a in args],
        out_specs=[_const_spec((n, d)), _const_spec((n, d)), _const_spec((n, LANES))],
        out_shape=[jax.ShapeDtypeStruct((n, d), F32), jax.ShapeDtypeStruct((n, d), BF16),
                   jax.ShapeDtypeStruct((n, LANES), F32)],
        compiler_params=_cparams(("arbitrary",)),
        name="outproj_sample",
    )(*args)


TT = 256
CHUNK_ALIGN = SUBLANES
LROWS = -(-(TT * TOP_K + N_EXPERTS * (CHUNK_ALIGN - 1)) // TT) * TT
CHUNK_SIZES = tuple(1 << b for b in range(TT.bit_length() - 1, CHUNK_ALIGN.bit_length() - 2, -1))
SUB = 256
SUB_TAIL = SUB // 2
RG = 5 * SUB
TF = 256
VMEM_LIMIT_FFN = 60000 * 1024


def _chunk_loop(cnt_ref, loff_ref, dest_ref, tile, fn):
    def per_expert(e, carry):
        idx = tile * N_EXPERTS + e
        n, off, dst = cnt_ref[idx], loff_ref[idx], dest_ref[idx]
        for size in CHUNK_SIZES:
            take = (n & size) != 0

            @pl.when(take)
            def _(off=off, dst=dst, size=size):
                fn(pl.multiple_of(off, CHUNK_ALIGN), pl.multiple_of(dst, CHUNK_ALIGN), size)

            step = jnp.where(take, size, 0)
            off, dst = off + step, dst + step
        return carry

    lax.fori_loop(0, N_EXPERTS, per_expert, 0)


def _dispatch_kernel(cnt_ref, loff_ref, dest_ref, xp_ref, xs_ref, lpos_ref, gate_ref, out_hbm, buf, sem):
    t = pl.program_id(0)
    last = pl.num_programs(0) - 1
    d = xp_ref.shape[1]
    x = jnp.where(t == last, xs_ref[...], xp_ref[...])
    rows = lax.broadcasted_iota(jnp.int32, (LROWS, TT), 0)
    onehot = jnp.zeros((LROWS, TT), F32)
    wcol = jnp.zeros((LROWS, 1), F32)
    for k in range(TOP_K):
        hit = jnp.where(rows == lpos_ref[0, k:k + 1, :], 1.0, 0.0)
        onehot = onehot + hit
        wcol = wcol + jnp.sum(hit * gate_ref[0, k:k + 1, :], axis=-1, keepdims=True)
    tile_buf = buf.at[t % 2]
    tile_buf[:, 0:d] = jnp.dot(onehot.astype(BF16), x, preferred_element_type=F32)
    tile_buf[:, d:d + LANES] = jnp.broadcast_to(wcol, (LROWS, LANES))

    def copy(tile, off, dst, size):
        return pltpu.make_async_copy(buf.at[tile % 2, pl.ds(off, size)], out_hbm.at[pl.ds(dst, size)],
                                     sem.at[tile % 2])

    _chunk_loop(cnt_ref, loff_ref, dest_ref, t, lambda o, g, s: copy(t, o, g, s).start())

    @pl.when(t > 0)
    def _():
        _chunk_loop(cnt_ref, loff_ref, dest_ref, t - 1, lambda o, g, s: copy(t - 1, o, g, s).wait())

    @pl.when(t == last)
    def _():
        _chunk_loop(cnt_ref, loff_ref, dest_ref, t, lambda o, g, s: copy(t, o, g, s).wait())


def _dispatch(tabs, xn_p, xn_s, lpos_t, gate_t, n_rows):
    nt = lpos_t.shape[0]
    d = xn_p.shape[1]
    last_p = xn_p.shape[0] // TT - 1
    grid_spec = pltpu.PrefetchScalarGridSpec(
        num_scalar_prefetch=3,
        grid=(nt,),
        in_specs=[pl.BlockSpec((TT, d), lambda t, *_: (jnp.minimum(t, last_p), 0)),
                  pl.BlockSpec((TT, d), lambda t, *_: (0, 0)),
                  pl.BlockSpec((1, TOP_K, TT), lambda t, *_: (t, 0, 0)),
                  pl.BlockSpec((1, TOP_K, TT), lambda t, *_: (t, 0, 0))],
        out_specs=pl.BlockSpec(memory_space=pl.ANY),
        scratch_shapes=[pltpu.VMEM((2, LROWS, d + LANES), F32), pltpu.SemaphoreType.DMA((2,))],
    )
    return pl.pallas_call(
        _dispatch_kernel,
        grid_spec=grid_spec,
        out_shape=jax.ShapeDtypeStruct((n_rows, d + LANES), F32),
        compiler_params=_cparams(("arbitrary",)),
        name="moe_dispatch",
    )(*tabs, xn_p, xn_s, lpos_t, gate_t)


def _ffn_kernel(ge_ref, gs_ref, gn_ref, gt_ref, ng_ref, xs_hbm, wgu_hbm, wd_hbm, bgu_ref, bdn_ref, out_hbm,
                xbuf, acc, ostage, wg_st, wu_st, wd_st, wg_bf, wu_bf, wd_bf, sem_w, sem_x, sem_o):
    d = acc.shape[1]
    d_ff = wd_hbm.shape[1]
    nf = d_ff // TF
    n_groups = ng_ref[0]
    total = n_groups * nf

    def w_copies(s, slot):
        g = s // nf
        f = s - g * nf
        e = ge_ref[g]
        c0 = pl.multiple_of(f * TF, TF)
        return (pltpu.make_async_copy(wgu_hbm.at[e, :, pl.ds(c0, TF)], wg_st.at[slot], sem_w.at[slot, 0]),
                pltpu.make_async_copy(wgu_hbm.at[e, :, pl.ds(d_ff + c0, TF)], wu_st.at[slot], sem_w.at[slot, 1]),
                pltpu.make_async_copy(wd_hbm.at[e, pl.ds(c0, TF), :], wd_st.at[slot], sem_w.at[slot, 2]))

    def x_copy(g, j, size=SUB):
        r0 = pl.multiple_of(j * SUB, SUB)
        return pltpu.make_async_copy(xs_hbm.at[pl.ds(pl.multiple_of(gs_ref[g] + r0, CHUNK_ALIGN), size)],
                                     xbuf.at[g % 2, pl.ds(r0, size)], sem_x)

    def o_copy(g, j, size=SUB):
        r0 = pl.multiple_of(j * SUB, SUB)
        return pltpu.make_async_copy(ostage.at[j % 2, pl.ds(0, size)],
                                     out_hbm.at[pl.ds(pl.multiple_of(gs_ref[g] + r0, CHUNK_ALIGN), size)],
                                     sem_o.at[j % 2])

    def loop(n, fn):
        lax.fori_loop(0, n, lambda j, c: (fn(j), c)[1], 0)

    def group_rows(g, op):
        loop(gn_ref[g], lambda j: op(x_copy(g, j)))

        @pl.when(gt_ref[g] == 1)
        def _():
            op(x_copy(g, gn_ref[g], SUB_TAIL))

    def drain_stores(g):
        n = gn_ref[g]
        tail = gt_ref[g]

        @pl.when(tail == 1)
        def _():
            o_copy(g, n, SUB_TAIL).wait()

        @pl.when(n >= 1)
        def _():
            o_copy(g, n - 1).wait()

        @pl.when(jnp.logical_and(n >= 2, tail == 0))
        def _():
            o_copy(g, n - 2).wait()

    @pl.when(total > 0)
    def _():
        for c in w_copies(0, 0):
            c.start()
        group_rows(0, lambda c: c.start())

    def item(s, carry):
        slot = s % 2
        g = s // nf
        f = s - g * nf
        e = ge_ref[g]
        nsub = gn_ref[g]
        xg = xbuf.at[g % 2]

        @pl.when(s + 1 < total)
        def _():
            for c in w_copies(s + 1, 1 - slot):
                c.start()

        @pl.when(f == 0)
        def _():
            group_rows(g, lambda c: c.wait())

        @pl.when(jnp.logical_and(f == 1, g + 1 < n_groups))
        def _():
            group_rows(g + 1, lambda c: c.start())

        for c in w_copies(s, slot):
            c.wait()
        bg = bgu_ref[pl.ds(e * 2 * nf + f, 1), :]
        bu = bgu_ref[pl.ds(e * 2 * nf + nf + f, 1), :]

        def sub_block(j, phase, cast=False, size=SUB):
            rs = pl.ds(pl.multiple_of(j * SUB, SUB), size)
            x = xg[rs, 0:d].astype(BF16)
            if cast:
                wg, wu, wd = (st[slot].astype(BF16) for st in (wg_st, wu_st, wd_st))
                wg_bf[...], wu_bf[...], wd_bf[...] = wg, wu, wd
            else:
                wg, wu, wd = wg_bf[...], wu_bf[...], wd_bf[...]
            hg = jnp.dot(x, wg, preferred_element_type=F32) + bg
            hu = jnp.dot(x, wu, preferred_element_type=F32) + bu
            gg = jnp.minimum(hg, SWIGLU_LIMIT)
            uu = jnp.clip(hu, -SWIGLU_LIMIT, SWIGLU_LIMIT)
            act = gg * (1.0 / (1.0 + jnp.exp(-SWIGLU_ALPHA * gg))) * (uu + 1.0)
            part = jnp.dot(act.astype(BF16), wd, preferred_element_type=F32)
            if phase == "first":
                acc[rs, :] = part + bdn_ref[pl.ds(e, 1), :]
            elif phase == "middle":
                acc[rs, :] += part
            else:
                @pl.when(j >= 2)
                def _():
                    o_copy(g, j - 2).wait()

                ostage[j % 2, 0:size] = (acc[rs, :] + part) * xg[rs, d:d + 1]
                o_copy(g, j, size).start()

        tail = gt_ref[g] == 1

        def storing_blocks():
            sub_block(jnp.int32(0), "last", cast=True)

            def pair(p):
                sub_block(2 * p + 1, "last")
                sub_block(2 * p + 2, "last")

            loop((nsub - 1) // 2, pair)

            @pl.when(nsub % 2 == 0)
            def _():
                sub_block(nsub - 1, "last")

            @pl.when(tail)
            def _():
                sub_block(nsub, "last", size=SUB_TAIL)

        def accumulating_blocks(phase):
            @pl.when(nsub < 2)
            def _():
                sub_block(jnp.int32(0), phase, cast=True)

                @pl.when(tail)
                def _():
                    sub_block(jnp.int32(1), phase, size=SUB_TAIL)

            @pl.when(nsub >= 2)
            def _():
                sub_block(jnp.int32(0), phase, cast=True, size=2 * SUB)
                rest = 2 * (nsub - 2) + gt_ref[g]
                for units in range(1, (RG - 2 * SUB) // SUB_TAIL + 1):
                    @pl.when(rest == units)
                    def _(units=units):
                        sub_block(jnp.int32(2), phase, size=units * SUB_TAIL)

        @pl.when(f == 0)
        def _():
            accumulating_blocks("first")

        @pl.when(jnp.logical_and(f > 0, f < nf - 1))
        def _():
            accumulating_blocks("middle")

        @pl.when(f == nf - 1)
        def _():
            @pl.when(g > 0)
            def _():
                drain_stores(g - 1)

            storing_blocks()

        return carry

    lax.fori_loop(0, total, item, 0)

    @pl.when(total > 0)
    def _():
        drain_stores(n_groups - 1)


def _moe_ffn(groups, x_sorted, w_gate_up, b_gu, w_down, b_dn):
    n_rows = x_sorted.shape[0]
    _, d_ff, d = w_down.shape
    nf = d_ff // TF
    assert nf >= 2
    bgu2 = b_gu.reshape(N_EXPERTS * 2 * nf, TF)
    grid_spec = pltpu.PrefetchScalarGridSpec(
        num_scalar_prefetch=5,
        grid=(1,),
        in_specs=[pl.BlockSpec(memory_space=pl.ANY), pl.BlockSpec(memory_space=pl.ANY),
                  pl.BlockSpec(memory_space=pl.ANY),
                  pl.BlockSpec(bgu2.shape, lambda i, *_: (0, 0), pipeline_mode=pl.Buffered(1)),
                  pl.BlockSpec(b_dn.shape, lambda i, *_: (0, 0), pipeline_mode=pl.Buffered(1))],
        out_specs=pl.BlockSpec(memory_space=pl.ANY),
        scratch_shapes=[pltpu.VMEM((2, RG, d + LANES), F32), pltpu.VMEM((RG, d), F32),
                        pltpu.VMEM((2, SUB, d), F32),
                        pltpu.VMEM((2, d, TF), F32), pltpu.VMEM((2, d, TF), F32), pltpu.VMEM((2, TF, d), F32),
                        pltpu.VMEM((d, TF), BF16), pltpu.VMEM((d, TF), BF16), pltpu.VMEM((TF, d), BF16),
                        pltpu.SemaphoreType.DMA((2, 3)), pltpu.SemaphoreType.DMA(()),
                        pltpu.SemaphoreType.DMA((2,))],
    )
    return pl.pallas_call(
        _ffn_kernel,
        grid_spec=grid_spec,
        out_shape=jax.ShapeDtypeStruct((n_rows, d), F32),
        compiler_params=pltpu.CompilerParams(dimension_semantics=("arbitrary",),
                                             vmem_limit_bytes=VMEM_LIMIT_FFN),
        name="moe_ffn",
    )(*groups, x_sorted, w_gate_up, w_down, bgu2, b_dn)


def _route(logits, m_pad):
    m = logits.shape[0]
    nt = m_pad // TT
    top_v, top_i = lax.top_k(logits, TOP_K)
    gate = jnp.pad(jax.nn.softmax(top_v, axis=-1), ((0, m_pad - m), (0, 0)))
    top_i = jnp.pad(top_i.astype(jnp.int32), ((0, m_pad - m), (0, 0)), constant_values=-1)
    chosen = (top_i[:, :, None] == jnp.arange(N_EXPERTS, dtype=jnp.int32)).astype(jnp.int32)
    tiles = chosen.sum(axis=1).reshape(nt, TT, N_EXPERTS)
    cnt = (tiles.sum(axis=1) + CHUNK_ALIGN - 1) // CHUNK_ALIGN * CHUNK_ALIGN
    loff = jnp.cumsum(cnt, axis=1) - cnt
    seg = cnt.sum(axis=0)
    seg_start = jnp.cumsum(seg) - seg
    dest = seg_start[None, :] + jnp.cumsum(cnt, axis=0) - cnt
    before = jnp.asarray(np.tril(np.ones((TT, TT), np.float32), -1), BF16)
    rank = jnp.einsum('ij,tje->tie', before, tiles.astype(BF16),
                      preferred_element_type=F32).astype(jnp.int32)
    lpos_all = (loff[:, None, :] + rank).reshape(m_pad, N_EXPERTS)
    lpos = (lpos_all[:, None, :] * chosen).sum(axis=-1)
    lpos = jnp.where(top_i >= 0, lpos, -1).astype(jnp.int32)
    n_rows_bound = nt * LROWS
    n_grp_max = n_rows_bound // RG + N_EXPERTS
    grp = (seg + RG - 1) // RG
    grp_end = jnp.cumsum(grp)
    gi = jnp.arange(n_grp_max, dtype=jnp.int32)
    g_exp = jnp.minimum((gi[:, None] >= grp_end[None, :]).sum(axis=1), N_EXPERTS - 1).astype(jnp.int32)
    within = gi - (grp_end[g_exp] - grp[g_exp])
    g_start = (seg_start[g_exp] + within * RG).astype(jnp.int32)
    g_rows = jnp.clip(seg[g_exp] - within * RG, 0, RG)
    g_rows = jnp.where(gi < grp_end[-1], g_rows, 0)
    n_full = g_rows // SUB
    rest = g_rows - n_full * SUB
    g_tail = (rest > 0) & (rest <= SUB_TAIL) & (n_full >= 1)
    g_nsub = n_full + ((rest > 0) & ~g_tail)
    groups = (g_exp, g_start, g_nsub.astype(jnp.int32), g_tail.astype(jnp.int32),
              grp_end[-1:].astype(jnp.int32))
    tabs = tuple(a.reshape(-1).astype(jnp.int32) for a in (cnt, loff, dest))
    return tabs, lpos, gate, groups, n_rows_bound + SUB


def _combine_kernel(cnt_ref, loff_ref, dest_ref, src_hbm, lpos_ref, x1p_ref, x1s_ref, g_ref,
                    yp_ref, ys_ref, buf, sem):
    t = pl.program_id(0)
    last = pl.num_programs(0) - 1

    def copy(tile, off, src, size):
        return pltpu.make_async_copy(src_hbm.at[pl.ds(src, size)], buf.at[tile % 2, pl.ds(off, size)],
                                     sem.at[tile % 2])

    @pl.when(t == 0)
    def _():
        buf[...] = jnp.zeros_like(buf)
        _chunk_loop(cnt_ref, loff_ref, dest_ref, t, lambda o, g, s: copy(t, o, g, s).start())

    @pl.when(t < last)
    def _():
        _chunk_loop(cnt_ref, loff_ref, dest_ref, t + 1, lambda o, g, s: copy(t + 1, o, g, s).start())

    _chunk_loop(cnt_ref, loff_ref, dest_ref, t, lambda o, g, s: copy(t, o, g, s).wait())
    cols = lax.broadcasted_iota(jnp.int32, (TT, LROWS), 1)
    sel = jnp.zeros((TT, LROWS), F32)
    for k in range(TOP_K):
        sel = sel + jnp.where(cols == lpos_ref[:, k:k + 1], 1.0, 0.0)
    f = jnp.dot(sel.astype(BF16), buf[t % 2].astype(BF16), preferred_element_type=F32)

    @pl.when(t < last)
    def _():
        yp_ref[...] = _rms(x1p_ref[...] + f, g_ref[...])

    @pl.when(t == last)
    def _():
        ys_ref[...] = _rms(x1s_ref[...] + f, g_ref[...])


def _combine(tabs, out_sorted, lpos, x1p, x1s, g):
    d = x1p.shape[1]
    nt = lpos.shape[0] // TT
    last_p = x1p.shape[0] // TT - 1
    prompt = pl.BlockSpec((TT, d), lambda t, *_: (jnp.minimum(t, last_p), 0))
    sample = pl.BlockSpec((TT, d), lambda t, *_: (0, 0))
    grid_spec = pltpu.PrefetchScalarGridSpec(
        num_scalar_prefetch=3,
        grid=(nt,),
        in_specs=[pl.BlockSpec(memory_space=pl.ANY),
                  pl.BlockSpec((TT, TOP_K), lambda t, *_: (t, 0)),
                  prompt, sample,
                  pl.BlockSpec((1, d), lambda t, *_: (0, 0))],
        out_specs=[prompt, sample],
        scratch_shapes=[pltpu.VMEM((2, LROWS, d), F32), pltpu.SemaphoreType.DMA((2,))],
    )
    return pl.pallas_call(
        _combine_kernel,
        grid_spec=grid_spec,
        out_shape=[jax.ShapeDtypeStruct(x1p.shape, F32), jax.ShapeDtypeStruct((TT, d), F32)],
        compiler_params=_cparams(("arbitrary",)),
        name="moe_combine",
    )(*tabs, out_sorted, lpos, x1p, x1s, g.reshape(1, d))


def _t5_bucket(dist):
    max_exact = N_BUCKETS // 2
    dd = dist.astype(F32)
    large = max_exact + (jnp.log(jnp.maximum(dd, 1.0) / max_exact)
                         / math.log(BUCKET_MAX_DIST / max_exact) * (N_BUCKETS - max_exact)).astype(jnp.int32)
    large = jnp.minimum(large, N_BUCKETS - 1)
    return jnp.where(dist < max_exact, dist, large)


def _bias_tables(rel_bias):
    dist = jnp.asarray(np.arange(N_TAPS)[None, :] * np.array(DILATIONS)[:, None], jnp.int32)
    bias = jnp.transpose(rel_bias[_t5_bucket(dist)], (2, 0, 1)).astype(F32)
    by_branch = jnp.transpose(bias, (1, 0, 2))
    g, h = by_branch.shape[:2]
    row = jnp.concatenate([by_branch[:, :, ::-1], jnp.full((g, h, WIN), NEG, F32)], axis=-1)
    flat = jnp.broadcast_to(row[:, :, None, :], (g, h, WIN, 2 * WIN + 1)).reshape(g, h, -1)
    band = flat[:, :, :WIN * 2 * WIN].reshape(g, h, WIN, 2 * WIN)
    return band, by_branch[:, :, :0:-1], by_branch[:, :, 0:1]


def kernel(x_prompt, x_sample, cache_k_win, cache_v_win, state_conv, state_ssm, rel_bias, attn_norm, w_in, conv_w, conv_b, dt_bias, a_log, d_skip, ssd_norm, w_out, ffn_norm, w_router, b_router, w_gate_up, b_gate_up, w_down, b_down, final_norm):
    bp, tp, d = x_prompt.shape
    bs, ts, _ = x_sample.shape
    depth = w_in.shape[0]
    assert depth == 1 and ts == 1 and tp % (max(DILATIONS) * WIN) == 0
    keep = min(max(DILATIONS) * WIN, tp)
    band, samp, samp0 = _bias_tables(rel_bias)
    l = 0

    xp = x_prompt.reshape(bp * tp, d)
    xs = x_sample.reshape(bs * ts, d)
    w_in_bf = jnp.pad(w_in[l], ((0, 0), (0, IN_PROJ_PAD - IN_PROJ))).astype(BF16)
    w_out_bf = w_out[l].astype(BF16)
    wr = jnp.pad(w_router[l], ((0, 0), (0, LANES - N_EXPERTS))).astype(BF16)
    br = jnp.pad(b_router[l], (0, LANES - N_EXPERTS), constant_values=NEG).reshape(1, LANES)

    k, v, z, xbc, dt_raw, *branch_qkv = _inproj_prompt(xp, attn_norm[l], w_in_bf, bp)
    k3 = k.reshape(bp, tp, KV_WIDTH)
    v3 = v.reshape(bp, tp, KV_WIDTH)
    att_parts = [_attn_branch(branch_qkv[2 * gi], branch_qkv[2 * gi + 1], band[gi])
                 for gi in range(len(DILATIONS))]
    ssm, st_p = _ssd_prompt(xbc, dt_raw, z, conv_w[l], conv_b[l], dt_bias[l], a_log[l], d_skip[l],
                            ssd_norm[l], bp)
    x1p, xnp_, lgp = _outproj_prompt(att_parts, ssm, xp, w_out_bf, ffn_norm[l], wr, br)
    k_win_p = k3[:, tp - keep:].reshape(1, bp, keep, N_KV_HEADS, HEAD_DIM)
    v_win_p = v3[:, tp - keep:].reshape(1, bp, keep, N_KV_HEADS, HEAD_DIM)
    conv_p = xbc.reshape(bp, tp, CONV_DIM)[:, tp - (CONV_W - 1):][None]

    q_s, k_s, v_s, z_s, xbc_s, dt_s = _inproj(xs, attn_norm[l], w_in_bf, bs * ts)
    q_s = q_s.reshape(bs, N_ATT_HEADS, HEAD_DIM)
    k_s = k_s.reshape(bs, N_KV_HEADS, HEAD_DIM)
    v_s = v_s.reshape(bs, N_KV_HEADS, HEAD_DIM)
    wbuf = cache_k_win.shape[2]
    att_s = _attn_sample(q_s, k_s, v_s, cache_k_win[l].reshape(bs, wbuf, KV_WIDTH),
                         cache_v_win[l].reshape(bs, wbuf, KV_WIDTH), samp, samp0)
    xa_s, xdt_s, decay_s = _conv_sample(xbc_s, state_conv[l], conv_w[l], conv_b[l], dt_s, dt_bias[l],
                                        a_log[l])
    nbc = N_SSM_GROUPS * D_STATE
    h_s, y_s = _ssm_sample(xdt_s, decay_s, xa_s[:, SSM_WIDTH:SSM_WIDTH + nbc], xa_s[:, SSM_WIDTH + nbc:],
                           state_ssm[l])
    x1s, xns, lgs = _outproj_sample(att_s.reshape(bs, ATT_WIDTH), y_s, xa_s[:, :SSM_WIDTH], z_s,
                                    d_skip[l], ssd_norm[l], xs, w_out_bf, ffn_norm[l], wr, br)
    conv_s = jnp.concatenate([state_conv[l][:, 1:], xbc_s[:, None]], axis=1)[None]

    n_s = bs * ts
    assert (bp * tp) % TT == 0 and n_s <= TT
    m_pad = bp * tp + TT
    logits = jnp.concatenate([lgp[:, :N_EXPERTS], lgs[:, :N_EXPERTS]], axis=0)
    tabs, lpos, gate, groups, n_rows = _route(logits, m_pad)
    by_tile = lambda a: jnp.transpose(a.reshape(m_pad // TT, TT, TOP_K), (0, 2, 1))
    pad_rows = lambda a: jnp.pad(a, ((0, TT - n_s), (0, 0)))
    x_sorted = _dispatch(tabs, xnp_, pad_rows(xns), by_tile(lpos), by_tile(gate), n_rows)
    out_sorted = _moe_ffn(groups, x_sorted, w_gate_up[l], b_gate_up[l], w_down[l], b_down[l])
    y_p, y_s_out = _combine(tabs, out_sorted, lpos, x1p, pad_rows(x1s), final_norm)
    y_s_out = y_s_out[:n_s]

    return (y_p.reshape(bp, tp, d), y_s_out.reshape(bs, ts, d), k_win_p, v_win_p, conv_p, st_p[None],
            k_s.reshape(1, bs, ts, N_KV_HEADS, HEAD_DIM), v_s.reshape(1, bs, ts, N_KV_HEADS, HEAD_DIM),
            conv_s, h_s[None])
```

```python
import functools
import math

import jax
import jax.numpy as jnp
import numpy as np
from jax import lax
from jax.experimental import pallas as pl
from jax.experimental.pallas import tpu as pltpu

F32 = jnp.float32
BF16 = jnp.bfloat16
HIGHEST = lax.Precision.HIGHEST

LANES = 128
SUBLANES = 8
VMEM_LIMIT = 56 * 1024 * 1024

HEAD_DIM = 64
N_ATT_HEADS = 16
N_KV_HEADS = 4
KV_REP = N_ATT_HEADS // N_KV_HEADS
ATT_WIDTH = N_ATT_HEADS * HEAD_DIM
KV_WIDTH = N_KV_HEADS * HEAD_DIM
DILATIONS = (1, 4, 16)
N_TAPS = 129
WIN = N_TAPS - 1
ATT_SCALE = HEAD_DIM ** -0.5
N_BUCKETS = 32
BUCKET_MAX_DIST = 2048
SSM_HEAD_DIM = 64
N_SSM_HEADS = 16
SSM_WIDTH = N_SSM_HEADS * SSM_HEAD_DIM
N_SSM_GROUPS = 2
HEADS_PER_GROUP = N_SSM_HEADS // N_SSM_GROUPS
D_STATE = 128
CONV_W = 4
CONV_DIM = SSM_WIDTH + 2 * N_SSM_GROUPS * D_STATE
SSD_CHUNK = 128
N_EXPERTS = 32
TOP_K = 4
SWIGLU_LIMIT = 7.0
SWIGLU_ALPHA = 1.702
EPS = 1e-5
NEG = -1e30

Q0, K0, V0, Z0, X0, DT0 = 0, 1024, 1280, 1536, 2560, 4096
IN_PROJ = DT0 + N_SSM_HEADS
IN_PROJ_PAD = DT0 + LANES

TM_PROJ = 512
TM_OUT = 256
TQ_ATT = WIN


def _cparams(sem):
    return pltpu.CompilerParams(dimension_semantics=sem, vmem_limit_bytes=VMEM_LIMIT)


def _const_spec(shape):
    nd = len(shape)
    return pl.BlockSpec(shape, lambda *_: (0,) * nd, pipeline_mode=pl.Buffered(1))


def _spread(a, onehot_bf, terms):
    out, rest = None, a
    for _ in range(terms):
        piece = rest.astype(BF16)
        part = jnp.dot(piece, onehot_bf, preferred_element_type=F32)
        out = part if out is None else out + part
        rest = rest - piece.astype(F32)
    return out


def _rms(x, g):
    ms = jnp.mean(x * x, axis=-1, keepdims=True)
    return x * lax.rsqrt(ms + EPS) * g


def _silu(x):
    return x * (1.0 / (1.0 + jnp.exp(-x)))


def _inproj_kernel(x_ref, g_ref, w_ref, q_ref, k_ref, v_ref, z_ref, xbc_ref, dt_ref):
    xn = _rms(x_ref[...], g_ref[...]).astype(BF16)

    def mm(lo, hi):
        return jnp.dot(xn, w_ref[:, lo:hi], preferred_element_type=F32)

    q_ref[...] = (mm(Q0, K0) * ATT_SCALE).astype(BF16)
    k_ref[...] = mm(K0, V0)
    v_ref[...] = mm(V0, Z0)
    z_ref[...] = mm(Z0, X0)
    xbc_ref[...] = mm(X0, DT0)
    dt_ref[...] = mm(DT0, IN_PROJ_PAD)


def _inproj(x2d, g, w_bf, tm):
    m, d = x2d.shape
    widths = (ATT_WIDTH, KV_WIDTH, KV_WIDTH, SSM_WIDTH, CONV_DIM, LANES)
    dtypes = (BF16, F32, F32, F32, F32, F32)
    return pl.pallas_call(
        _inproj_kernel,
        grid=(m // tm,),
        in_specs=[pl.BlockSpec((tm, d), lambda i: (i, 0)),
                  _const_spec((1, d)),
                  _const_spec((d, IN_PROJ_PAD))],
        out_specs=[pl.BlockSpec((tm, w), lambda i: (i, 0)) for w in widths],
        out_shape=[jax.ShapeDtypeStruct((m, w), t) for w, t in zip(widths, dtypes)],
        compiler_params=_cparams(("parallel",)),
        name=f"inproj_{tm}",
    )(x2d, g.reshape(1, d), w_bf)


def _deinterleave_matrix(n, dil):
    p = np.zeros((n, n), np.float32)
    src = np.arange(n)
    p[(src % dil) * (n // dil) + src // dil, src] = 1.0
    return p


def _inproj_prompt_kernel(x_ref, g_ref, w_ref, p_mid_ref, p_far_ref, k_ref, v_ref, z_ref, xbc_ref, dt_ref,
                          q0_ref, kv0_ref, q1_ref, kv1_ref, q2_ref, kv2_ref):
    tm = x_ref.shape[0]
    xn = _rms(x_ref[...], g_ref[...]).astype(BF16)

    def mm(lo, hi):
        return jnp.dot(xn, w_ref[:, lo:hi], preferred_element_type=F32)

    q = (mm(Q0, K0) * ATT_SCALE).astype(BF16)
    k = mm(K0, V0)
    v = mm(V0, Z0)
    k_ref[...] = k
    v_ref[...] = v
    z_ref[...] = mm(Z0, X0)
    xbc_ref[...] = mm(X0, DT0)
    dt_ref[...] = mm(DT0, IN_PROJ_PAD)
    kv = jnp.concatenate([k, v], axis=-1).astype(BF16)
    q0_ref[0, 0] = q
    kv0_ref[0, 0] = kv
    for dil, p_ref, qd_ref, kvd_ref in ((DILATIONS[1], p_mid_ref, q1_ref, kv1_ref),
                                        (DILATIONS[2], p_far_ref, q2_ref, kv2_ref)):
        qp = jnp.dot(p_ref[...], q, preferred_element_type=F32).astype(BF16)
        kvp = jnp.dot(p_ref[...], kv, preferred_element_type=F32).astype(BF16)
        rows = tm // dil
        for r in range(dil):
            qd_ref[0, r] = qp[r * rows:(r + 1) * rows]
            kvd_ref[0, r] = kvp[r * rows:(r + 1) * rows]


def _inproj_prompt(x2d, g, w_bf, batch):
    m, d = x2d.shape
    tm = TM_PROJ
    seq = m // batch
    per_b = seq // tm
    assert DILATIONS[0] == 1 and seq % tm == 0 and tm % (max(DILATIONS) * 2 * SUBLANES) == 0
    widths = (KV_WIDTH, KV_WIDTH, SSM_WIDTH, CONV_DIM, LANES)
    rows = lambda w: pl.BlockSpec((tm, w), lambda i: (i, 0))
    split = lambda dil, w: pl.BlockSpec((1, dil, tm // dil, w), lambda i: (i // per_b, 0, i % per_b, 0))
    branch_specs, branch_shapes = [], []
    for dil in DILATIONS:
        for w in (ATT_WIDTH, 2 * KV_WIDTH):
            branch_specs.append(split(dil, w))
            branch_shapes.append(jax.ShapeDtypeStruct((batch, dil, seq // dil, w), BF16))
    perms = [jnp.asarray(_deinterleave_matrix(tm, dil), BF16) for dil in DILATIONS[1:]]
    return pl.pallas_call(
        _inproj_prompt_kernel,
        grid=(m // tm,),
        in_specs=[rows(d), _const_spec((1, d)), _const_spec((d, IN_PROJ_PAD)),
                  _const_spec((tm, tm)), _const_spec((tm, tm))],
        out_specs=[rows(w) for w in widths] + branch_specs,
        out_shape=[jax.ShapeDtypeStruct((m, w), F32) for w in widths] + branch_shapes,
        compiler_params=_cparams(("parallel",)),
        name="inproj_prompt",
    )(x2d, g.reshape(1, d), w_bf, *perms)


def _attn_kernel(q_ref, kvp_ref, kvc_ref, bias_ref, o_ref, lse_ref):
    first = pl.program_id(2) == 0
    lane = lax.broadcasted_iota(jnp.int32, (1, 2 * WIN), 1)
    prev_mask = jnp.where(jnp.logical_and(first, lane < WIN), NEG, 0.0)
    head_lane = lax.broadcasted_iota(jnp.int32, (WIN, LANES), 1)
    for blk in range(TQ_ATT // WIN):
        rows = slice(blk * WIN, (blk + 1) * WIN)
        lse_tile = jnp.zeros((WIN, LANES), F32)
        for kvh in range(N_KV_HEADS):
            ks = slice(kvh * HEAD_DIM, (kvh + 1) * HEAD_DIM)
            vs = slice(KV_WIDTH + kvh * HEAD_DIM, KV_WIDTH + (kvh + 1) * HEAD_DIM)
            if blk == 0:
                kw = jnp.concatenate([kvp_ref[0, 0, :, ks], kvc_ref[0, 0, 0:WIN, ks]], axis=0)
                vw = jnp.concatenate([kvp_ref[0, 0, :, vs], kvc_ref[0, 0, 0:WIN, vs]], axis=0)
            else:
                kw = kvc_ref[0, 0, (blk - 1) * WIN:(blk + 1) * WIN, ks]
                vw = kvc_ref[0, 0, (blk - 1) * WIN:(blk + 1) * WIN, vs]
            for pair in range(KV_REP // 2):
                outs = []
                for r in range(2):
                    h = kvh * KV_REP + pair * 2 + r
                    qh = q_ref[0, 0, rows, h * HEAD_DIM:(h + 1) * HEAD_DIM]
                    s = lax.dot_general(qh, kw, (((1,), (1,)), ((), ())), preferred_element_type=F32)
                    s = s + bias_ref[h]
                    if blk == 0:
                        s = s + prev_mask
                    mx = jnp.max(s, axis=-1, keepdims=True)
                    p = jnp.exp(s - mx)
                    l = jnp.sum(p, axis=-1, keepdims=True)
                    o = jnp.dot(p.astype(BF16), vw, preferred_element_type=F32)
                    outs.append(o * (1.0 / l))
                    lse_tile = jnp.where(head_lane == h, mx + jnp.log(l), lse_tile)
                h0 = kvh * KV_REP + pair * 2
                o_ref[0, 0, rows, h0 * HEAD_DIM:(h0 + 2) * HEAD_DIM] = (
                    jnp.concatenate(outs, axis=-1).astype(BF16))
        lse_ref[0, 0, rows, :] = lse_tile


def _attn_branch(q, kv, bias_mat):
    b, dil, sub, _ = q.shape
    assert sub % TQ_ATT == 0
    nb = sub // TQ_ATT
    per_step = TQ_ATT // WIN
    cur = lambda bb, r, i: (bb, r, i, 0)
    prev = lambda bb, r, i: (bb, r, jnp.maximum(i * per_step - 1, 0), 0)
    return pl.pallas_call(
        _attn_kernel,
        grid=(b, dil, nb),
        in_specs=[pl.BlockSpec((1, 1, TQ_ATT, ATT_WIDTH), cur),
                  pl.BlockSpec((1, 1, WIN, 2 * KV_WIDTH), prev),
                  pl.BlockSpec((1, 1, TQ_ATT, 2 * KV_WIDTH), cur),
                  _const_spec((N_ATT_HEADS, WIN, 2 * WIN))],
        out_specs=[pl.BlockSpec((1, 1, TQ_ATT, ATT_WIDTH), cur),
                   pl.BlockSpec((1, 1, TQ_ATT, LANES), cur)],
        out_shape=[jax.ShapeDtypeStruct((b, dil, sub, ATT_WIDTH), BF16),
                   jax.ShapeDtypeStruct((b, dil, sub, LANES), F32)],
        compiler_params=_cparams(("parallel", "parallel", "arbitrary")),
        name=f"attn_dil{dil}",
    )(q, kv, kv, bias_mat)


def _softplus(x):
    return jnp.maximum(x, 0.0) + jnp.log(1.0 + jnp.exp(-jnp.abs(x)))


def _ssd_kernel(xbc_ref, dt_ref, z_ref, cw_ref, cb_ref, dtb_ref, alog_ref, dskip_ref, gn_ref, e_ref,
                y_ref, st_ref, ext_ref, state_ref):
    c = pl.program_id(1)
    L = SSD_CHUNK

    @pl.when(c == 0)
    def _():
        ext_ref[0:SUBLANES, :] = jnp.zeros((SUBLANES, CONV_DIM), F32)
        state_ref[...] = jnp.zeros_like(state_ref)

    ext_ref[SUBLANES:SUBLANES + L, :] = xbc_ref[...]
    acc = cb_ref[...] + ext_ref[SUBLANES:SUBLANES + L, :] * cw_ref[CONV_W - 1:CONV_W, :]
    for i in range(CONV_W - 1):
        off = SUBLANES - (CONV_W - 1) + i
        acc = acc + ext_ref[off:off + L, :] * cw_ref[i:i + 1, :]
    ext_ref[0:SUBLANES, :] = ext_ref[L:L + SUBLANES, :]
    xa = _silu(acc)

    dt = _softplus(dt_ref[...] + dtb_ref[...])
    da = dt * (-jnp.exp(alog_ref[...]))
    row = lax.broadcasted_iota(jnp.int32, (L, L), 0)
    col = lax.broadcasted_iota(jnp.int32, (L, L), 1)
    tri = row >= col
    a_cs = jnp.dot(tri.astype(F32), da, preferred_element_type=F32, precision=HIGHEST)
    a_cs_t = a_cs.T
    expand = e_ref[...]
    acs_full = _spread(a_cs, expand, 3)
    dt_full = _spread(dt, expand, 3)
    exp_acs = jnp.exp(acs_full)
    a_last = acs_full[L - 1:L, :]
    exp_last = exp_acs[L - 1:L, :]
    xs = xa[:, :SSM_WIDTH]
    xdt = xs * dt_full
    xw = xdt * jnp.exp(a_last - acs_full)

    for g in range(N_SSM_GROUPS):
        b0 = SSM_WIDTH + g * D_STATE
        c0 = SSM_WIDTH + N_SSM_GROUPS * D_STATE + g * D_STATE
        bg_t = xa[:, b0:b0 + D_STATE].T.astype(BF16)
        cg = xa[:, c0:c0 + D_STATE].astype(BF16)
        gram = jnp.dot(cg, bg_t, preferred_element_type=F32)
        for hh in range(HEADS_PER_GROUP):
            h = g * HEADS_PER_GROUP + hh
            hs = slice(h * SSM_HEAD_DIM, (h + 1) * SSM_HEAD_DIM)
            seg = jnp.where(tri, a_cs[:, h:h + 1] - a_cs_t[h:h + 1, :], NEG)
            scores = (gram * jnp.exp(seg)).astype(BF16)
            y_diag = jnp.dot(scores, xdt[:, hs].astype(BF16), preferred_element_type=F32)
            st = state_ref[h]
            y_off = jnp.dot(cg, st.astype(BF16), preferred_element_type=F32) * exp_acs[:, hs]
            y_ref[:, hs] = y_diag + y_off
            state_ref[h] = exp_last[:, hs] * st + jnp.dot(bg_t, xw[:, hs].astype(BF16),
                                                          preferred_element_type=F32)

    y = y_ref[...] + dskip_ref[...] * xs
    u = y * _silu(z_ref[...])
    gw = SSM_WIDTH // N_SSM_GROUPS
    parts = []
    for g in range(N_SSM_GROUPS):
        ug = u[:, g * gw:(g + 1) * gw]
        parts.append(ug * lax.rsqrt(jnp.mean(ug * ug, axis=-1, keepdims=True) + EPS))
    y_ref[...] = jnp.concatenate(parts, axis=-1) * gn_ref[...]

    @pl.when(c == pl.num_programs(1) - 1)
    def _():
        st_ref[0] = state_ref[...]


def _head_expand():
    e = np.zeros((LANES, SSM_WIDTH), np.float32)
    for h in range(N_SSM_HEADS):
        e[h, h * SSM_HEAD_DIM:(h + 1) * SSM_HEAD_DIM] = 1.0
    return jnp.asarray(e)


def _pad_lanes(v):
    return jnp.pad(v.astype(F32), (0, LANES - v.shape[0])).reshape(1, LANES)


def _ssd_prompt(xbc, dt_raw, z, conv_w, conv_b, dt_bias, a_log, d_skip, ssd_norm, batch):
    m = xbc.shape[0]
    nc = m // batch // SSD_CHUNK
    L = SSD_CHUNK
    rows = lambda b, c: (b * nc + c, 0)
    y, st = pl.pallas_call(
        _ssd_kernel,
        grid=(batch, nc),
        in_specs=[pl.BlockSpec((L, CONV_DIM), rows),
                  pl.BlockSpec((L, LANES), rows),
                  pl.BlockSpec((L, SSM_WIDTH), rows),
                  _const_spec((CONV_W, CONV_DIM)),
                  _const_spec((1, CONV_DIM)),
                  _const_spec((1, LANES)),
                  _const_spec((1, LANES)),
                  _const_spec((1, SSM_WIDTH)),
                  _const_spec((1, SSM_WIDTH)),
                  _const_spec((LANES, SSM_WIDTH))],
        out_specs=[pl.BlockSpec((L, SSM_WIDTH), rows),
                   pl.BlockSpec((1, N_SSM_HEADS, D_STATE, SSM_HEAD_DIM), lambda b, c: (b, 0, 0, 0))],
        out_shape=[jax.ShapeDtypeStruct((m, SSM_WIDTH), F32),
                   jax.ShapeDtypeStruct((batch, N_SSM_HEADS, D_STATE, SSM_HEAD_DIM), F32)],
        scratch_shapes=[pltpu.VMEM((SUBLANES + L, CONV_DIM), F32),
                        pltpu.VMEM((N_SSM_HEADS, D_STATE, SSM_HEAD_DIM), F32)],
        compiler_params=_cparams(("parallel", "arbitrary")),
        name="ssd_prompt",
    )(xbc, dt_raw, z, conv_w, conv_b.reshape(1, CONV_DIM), _pad_lanes(dt_bias), _pad_lanes(a_log),
      jnp.repeat(d_skip, SSM_HEAD_DIM).reshape(1, SSM_WIDTH), ssd_norm.reshape(1, SSM_WIDTH),
      _head_expand().astype(BF16))
    return y, jnp.swapaxes(st, -1, -2)


def _router(xn_bf, wr_ref, br_ref):
    return jnp.dot(xn_bf, wr_ref[...], preferred_element_type=F32) + br_ref[...]


def _outproj_kernel(o0, l0, o1, l1, o2, l2, u1_ref, u2_ref, e_ref, ssm_ref, x_ref, w_ref, g_ref, wr_ref,
                    br_ref, x1_ref, xn_ref, lg_ref):
    def natural(o_ref, l_ref, u_ref):
        dil = o_ref.shape[1]
        if dil == 1:
            return o_ref[0, 0].astype(F32), l_ref[0, 0]
        o_cat = jnp.concatenate([o_ref[0, r] for r in range(dil)], axis=0)
        l_cat = jnp.concatenate([l_ref[0, r] for r in range(dil)], axis=0)
        hi = l_cat.astype(BF16)
        lo = (l_cat - hi.astype(F32)).astype(BF16)
        u = u_ref[...]
        return (jnp.dot(u, o_cat, preferred_element_type=F32),
                jnp.dot(u, hi, preferred_element_type=F32) + jnp.dot(u, lo, preferred_element_type=F32))

    branches = [natural(o0, l0, None), natural(o1, l1, u1_ref), natural(o2, l2, u2_ref)]
    lses = [_spread(l, e_ref[...], 2) for _, l in branches]
    mx = functools.reduce(jnp.maximum, lses)
    ws = [jnp.exp(l - mx) for l in lses]
    num = functools.reduce(jnp.add, [w * o for w, (o, _) in zip(ws, branches)])
    att = num * (1.0 / functools.reduce(jnp.add, ws))
    y = jnp.dot(att.astype(BF16), w_ref[:ATT_WIDTH, :], preferred_element_type=F32)
    y = y + jnp.dot(ssm_ref[...].astype(BF16), w_ref[ATT_WIDTH:, :], preferred_element_type=F32)
    x1 = x_ref[...] + y
    x1_ref[...] = x1
    xn = _rms(x1, g_ref[...]).astype(BF16)
    xn_ref[...] = xn
    lg_ref[...] = _router(xn, wr_ref, br_ref)


def _outproj_prompt(att_parts, ssm, x2d, w_bf, g, wr, br):
    m, d = x2d.shape
    mix = w_bf.shape[0]
    tm = TM_OUT
    batch = att_parts[0][0].shape[0]
    per_b = m // batch // tm
    assert tm % (max(DILATIONS) * 2 * SUBLANES) == 0
    assert (N_ATT_HEADS, HEAD_DIM) == (N_SSM_HEADS, SSM_HEAD_DIM)
    row = lambda w: pl.BlockSpec((tm, w), lambda i: (i, 0))
    split = lambda dil, w: pl.BlockSpec((1, dil, tm // dil, w), lambda i: (i // per_b, 0, i % per_b, 0))
    branch_specs = [split(o.shape[1], w) for o, _ in att_parts for w in (ATT_WIDTH, LANES)]
    interleave = [jnp.asarray(_deinterleave_matrix(tm, dil).T, BF16) for dil in DILATIONS[1:]]
    return pl.pallas_call(
        _outproj_kernel,
        grid=(m // tm,),
        in_specs=branch_specs + [_const_spec((tm, tm)), _const_spec((tm, tm)),
                                 _const_spec((LANES, ATT_WIDTH)), row(SSM_WIDTH), row(d),
                                 _const_spec((mix, d)), _const_spec((1, d)),
                                 _const_spec((d, LANES)), _const_spec((1, LANES))],
        out_specs=[row(d), row(d), row(LANES)],
        out_shape=[jax.ShapeDtypeStruct((m, d), F32), jax.ShapeDtypeStruct((m, d), BF16),
                   jax.ShapeDtypeStruct((m, LANES), F32)],
        compiler_params=_cparams(("parallel",)),
        name="outproj_prompt",
    )(*[a for part in att_parts for a in part], *interleave, _head_expand().astype(BF16), ssm, x2d, w_bf,
      g.reshape(1, d), wr, br)


def _attn_sample_kernel(q_ref, kn_ref, vn_ref, kc_ref, vc_ref, bias_ref, bias0_ref, o_ref):
    w = kc_ref.shape[1]
    q = q_ref[0]
    head_grp = lax.broadcasted_iota(jnp.int32, (N_ATT_HEADS, 1), 0) // KV_REP
    kn = kn_ref[0].astype(BF16).astype(F32)
    vn = vn_ref[0].astype(BF16).astype(F32)
    s_self = jnp.sum(q.astype(F32) * kn, axis=-1, keepdims=True)

    def taps(c_ref, dil):
        span = WIN * dil
        rows = c_ref[0, w - span:w, :].astype(BF16)
        if dil == 1:
            return rows
        tap = lax.broadcasted_iota(jnp.int32, (WIN, span), 0)
        col = lax.broadcasted_iota(jnp.int32, (WIN, span), 1)
        pick = jnp.where(col == tap * dil, 1.0, 0.0).astype(BF16)
        return jnp.dot(pick, rows, preferred_element_type=F32).astype(BF16)

    scores, selfs, lses = [], [], []
    for g, dil in enumerate(DILATIONS):
        kk = taps(kc_ref, dil)
        s = jnp.zeros((N_ATT_HEADS, WIN), F32)
        for kvh in range(N_KV_HEADS):
            cs = slice(kvh * HEAD_DIM, (kvh + 1) * HEAD_DIM)
            sk = lax.dot_general(q, kk[:, cs], (((1,), (1,)), ((), ())), preferred_element_type=F32)
            s = jnp.where(head_grp == kvh, sk, s)
        s = s + bias_ref[g]
        s0 = s_self + bias0_ref[g]
        mx = jnp.maximum(jnp.max(s, axis=-1, keepdims=True), s0)
        lse = mx + jnp.log(jnp.sum(jnp.exp(s - mx), axis=-1, keepdims=True) + jnp.exp(s0 - mx))
        scores.append(s)
        selfs.append(s0)
        lses.append(lse)
    top = functools.reduce(jnp.maximum, lses)
    es = [jnp.exp(l - top) for l in lses]
    tot = functools.reduce(jnp.add, es)
    o = jnp.zeros((N_ATT_HEADS, HEAD_DIM), F32)
    for s, s0, lse, e, dil in zip(scores, selfs, lses, es, DILATIONS):
        wgt = e / tot
        p = (jnp.exp(s - lse) * wgt).astype(BF16)
        p0 = (jnp.exp(s0 - lse) * wgt).astype(BF16).astype(F32)
        vv = taps(vc_ref, dil)
        o = o + p0 * vn
        for kvh in range(N_KV_HEADS):
            cs = slice(kvh * HEAD_DIM, (kvh + 1) * HEAD_DIM)
            ok = jnp.dot(p, vv[:, cs], preferred_element_type=F32)
            o = o + jnp.where(head_grp == kvh, ok, 0.0)
    o_ref[0] = o


def _attn_sample(q, k_new, v_new, k_cache, v_cache, bias_s, bias0_s):
    n, w = k_cache.shape[0], k_cache.shape[1]
    assert w % (max(DILATIONS) * WIN) == 0
    tok = lambda b: (b, 0, 0)
    head = pl.BlockSpec((1, N_ATT_HEADS, HEAD_DIM), tok)
    window = pl.BlockSpec((1, w, KV_WIDTH), tok)
    return pl.pallas_call(
        _attn_sample_kernel,
        grid=(n,),
        in_specs=[head, head, head, window, window,
                  _const_spec((len(DILATIONS), N_ATT_HEADS, WIN)),
                  _const_spec((len(DILATIONS), N_ATT_HEADS, 1))],
        out_specs=head,
        out_shape=jax.ShapeDtypeStruct((n, N_ATT_HEADS, HEAD_DIM), F32),
        compiler_params=_cparams(("parallel",)),
        name="attn_sample",
    )(q, jnp.repeat(k_new, KV_REP, axis=1), jnp.repeat(v_new, KV_REP, axis=1), k_cache, v_cache,
      bias_s, bias0_s)


def _conv_sample_kernel(xbc_ref, b0_ref, b1_ref, b2_ref, cw_ref, cb_ref, dt_ref, dtb_ref, alog_ref,
                        e_ref, xa_ref, xdt_ref, decay_ref):
    acc = cb_ref[...] + xbc_ref[...] * cw_ref[CONV_W - 1:CONV_W, :]
    for i, buf in enumerate((b0_ref, b1_ref, b2_ref)):
        acc = acc + buf[...] * cw_ref[i:i + 1, :]
    xa = _silu(acc)
    xa_ref[...] = xa
    dt = _softplus(dt_ref[...] + dtb_ref[...])
    decay_ref[...] = jnp.exp(dt * (-jnp.exp(alog_ref[...])))
    dt_full = jnp.dot(dt, e_ref[...], preferred_element_type=F32, precision=HIGHEST)
    xdt_ref[...] = xa[:, :SSM_WIDTH] * dt_full


def _conv_sample(xbc, conv_buf, conv_w, conv_b, dt_raw, dt_bias, a_log):
    n = xbc.shape[0]
    args = (xbc, conv_buf[:, 0], conv_buf[:, 1], conv_buf[:, 2], conv_w, conv_b.reshape(1, CONV_DIM),
            dt_raw, _pad_lanes(dt_bias), _pad_lanes(a_log),
            _head_expand())
    return pl.pallas_call(
        _conv_sample_kernel,
        grid=(1,),
        in_specs=[_const_spec(a.shape) for a in args],
        out_specs=[_const_spec((n, CONV_DIM)), _const_spec((n, SSM_WIDTH)), _const_spec((n, LANES))],
        out_shape=[jax.ShapeDtypeStruct((n, CONV_DIM), F32), jax.ShapeDtypeStruct((n, SSM_WIDTH), F32),
                   jax.ShapeDtypeStruct((n, LANES), F32)],
        compiler_params=_cparams(("arbitrary",)),
        name="conv_sample",
    )(*args)


def _ssm_sample_kernel(xdt_ref, decay_ref, b_ref, c_ref, h0_ref, hn_ref, y_ref):
    for g in range(N_SSM_GROUPS):
        hs = slice(g * HEADS_PER_GROUP, (g + 1) * HEADS_PER_GROUP)
        hn = decay_ref[0, hs] * h0_ref[0, hs] + xdt_ref[0, hs] * b_ref[0, g]
        hn_ref[0, hs] = hn
        c_row = c_ref[0, g].astype(BF16).astype(F32)
        y_ref[0, hs] = jnp.sum(hn.astype(BF16).astype(F32) * c_row, axis=-1, keepdims=True)


def _ssm_sample(xdt, decay, bmat, cmat, h0):
    n = xdt.shape[0]
    p = SSM_HEAD_DIM
    tok4 = lambda b: (b, 0, 0, 0)
    hn, y = pl.pallas_call(
        _ssm_sample_kernel,
        grid=(n,),
        in_specs=[pl.BlockSpec((1, N_SSM_HEADS, p, 1), tok4),
                  pl.BlockSpec((1, N_SSM_HEADS, 1, 1), tok4),
                  pl.BlockSpec((1, N_SSM_GROUPS, 1, D_STATE), tok4),
                  pl.BlockSpec((1, N_SSM_GROUPS, 1, D_STATE), tok4),
                  pl.BlockSpec((1, N_SSM_HEADS, p, D_STATE), tok4)],
        out_specs=[pl.BlockSpec((1, N_SSM_HEADS, p, D_STATE), tok4),
                   pl.BlockSpec((1, N_SSM_HEADS, p, 1), tok4)],
        out_shape=[jax.ShapeDtypeStruct((n, N_SSM_HEADS, p, D_STATE), F32),
                   jax.ShapeDtypeStruct((n, N_SSM_HEADS, p, 1), F32)],
        compiler_params=_cparams(("parallel",)),
        name="ssm_sample",
    )(xdt.reshape(n, N_SSM_HEADS, p, 1), decay[:, :N_SSM_HEADS].reshape(n, N_SSM_HEADS, 1, 1),
      bmat.reshape(n, N_SSM_GROUPS, 1, D_STATE), cmat.reshape(n, N_SSM_GROUPS, 1, D_STATE), h0)
    return hn, y.reshape(n, SSM_WIDTH)


def _outproj_sample_kernel(att_ref, y_ref, xs_ref, z_ref, dskip_ref, gn_ref, x_ref, w_ref, g_ref,
                           wr_ref, br_ref, x1_ref, xn_ref, lg_ref):
    y = y_ref[...] + dskip_ref[...] * xs_ref[...]
    u = y * _silu(z_ref[...])
    gw = SSM_WIDTH // N_SSM_GROUPS
    parts = []
    for g in range(N_SSM_GROUPS):
        ug = u[:, g * gw:(g + 1) * gw]
        parts.append(ug * lax.rsqrt(jnp.mean(ug * ug, axis=-1, keepdims=True) + EPS)
                     * gn_ref[:, g * gw:(g + 1) * gw])
    mix = jnp.concatenate([att_ref[...]] + parts, axis=-1).astype(BF16)
    x1 = x_ref[...] + jnp.dot(mix, w_ref[...], preferred_element_type=F32)
    x1_ref[...] = x1
    xn = _rms(x1, g_ref[...]).astype(BF16)
    xn_ref[...] = xn
    lg_ref[...] = _router(xn, wr_ref, br_ref)


def _outproj_sample(att, y, xs, z, d_skip, ssd_norm, x2d, w_bf, g, wr, br):
    n, d = x2d.shape
    args = (att, y, xs, z, jnp.repeat(d_skip, SSM_HEAD_DIM).reshape(1, SSM_WIDTH),
            ssd_norm.reshape(1, SSM_WIDTH), x2d, w_bf, g.reshape(1, d), wr, br)
    return pl.pallas_call(
        _outproj_sample_kernel,
        grid=(1,),
        in_specs=[_const_spec(a.shape) for a in args],
        out_specs=[_const_spec((n, d)), _const_spec((n, d)), _const_spec((n, LANES))],
        out_shape=[jax.ShapeDtypeStruct((n, d), F32), jax.ShapeDtypeStruct((n, d), BF16),
                   jax.ShapeDtypeStruct((n, LANES), F32)],
        compiler_params=_cparams(("arbitrary",)),
        name="outproj_sample",
    )(*args)


TT = 256
CHUNK_ALIGN = SUBLANES
LROWS = -(-(TT * TOP_K + N_EXPERTS * (CHUNK_ALIGN - 1)) // TT) * TT
CHUNK_SIZES = tuple(1 << b for b in range(TT.bit_length() - 1, CHUNK_ALIGN.bit_length() - 2, -1))
SUB = 256
SUB_TAIL = SUB // 2
RG = 5 * SUB
TF = 256
VMEM_LIMIT_FFN = 60000 * 1024


def _chunk_loop(cnt_ref, loff_ref, dest_ref, tile, fn):
    def per_expert(e, carry):
        idx = tile * N_EXPERTS + e
        n, off, dst = cnt_ref[idx], loff_ref[idx], dest_ref[idx]
        for size in CHUNK_SIZES:
            take = (n & size) != 0

            @pl.when(take)
            def _(off=off, dst=dst, size=size):
                fn(pl.multiple_of(off, CHUNK_ALIGN), pl.multiple_of(dst, CHUNK_ALIGN), size)

            step = jnp.where(take, size, 0)
            off, dst = off + step, dst + step
        return carry

    lax.fori_loop(0, N_EXPERTS, per_expert, 0)


def _dispatch_kernel(cnt_ref, loff_ref, dest_ref, xp_ref, xs_ref, lpos_ref, gate_ref, out_hbm, buf, sem):
    t = pl.program_id(0)
    last = pl.num_programs(0) - 1
    d = xp_ref.shape[1]
    x = jnp.where(t == last, xs_ref[...], xp_ref[...])
    rows = lax.broadcasted_iota(jnp.int32, (LROWS, TT), 0)
    onehot = jnp.zeros((LROWS, TT), F32)
    wcol = jnp.zeros((LROWS, 1), F32)
    for k in range(TOP_K):
        hit = jnp.where(rows == lpos_ref[0, k:k + 1, :], 1.0, 0.0)
        onehot = onehot + hit
        wcol = wcol + jnp.sum(hit * gate_ref[0, k:k + 1, :], axis=-1, keepdims=True)
    tile_buf = buf.at[t % 2]
    tile_buf[:, 0:d] = jnp.dot(onehot.astype(BF16), x, preferred_element_type=F32)
    tile_buf[:, d:d + LANES] = jnp.broadcast_to(wcol, (LROWS, LANES))

    def copy(tile, off, dst, size):
        return pltpu.make_async_copy(buf.at[tile % 2, pl.ds(off, size)], out_hbm.at[pl.ds(dst, size)],
                                     sem.at[tile % 2])

    _chunk_loop(cnt_ref, loff_ref, dest_ref, t, lambda o, g, s: copy(t, o, g, s).start())

    @pl.when(t > 0)
    def _():
        _chunk_loop(cnt_ref, loff_ref, dest_ref, t - 1, lambda o, g, s: copy(t - 1, o, g, s).wait())

    @pl.when(t == last)
    def _():
        _chunk_loop(cnt_ref, loff_ref, dest_ref, t, lambda o, g, s: copy(t, o, g, s).wait())


def _dispatch(tabs, xn_p, xn_s, lpos_t, gate_t, n_rows):
    nt = lpos_t.shape[0]
    d = xn_p.shape[1]
    last_p = xn_p.shape[0] // TT - 1
    grid_spec = pltpu.PrefetchScalarGridSpec(
        num_scalar_prefetch=3,
        grid=(nt,),
        in_specs=[pl.BlockSpec((TT, d), lambda t, *_: (jnp.minimum(t, last_p), 0)),
                  pl.BlockSpec((TT, d), lambda t, *_: (0, 0)),
                  pl.BlockSpec((1, TOP_K, TT), lambda t, *_: (t, 0, 0)),
                  pl.BlockSpec((1, TOP_K, TT), lambda t, *_: (t, 0, 0))],
        out_specs=pl.BlockSpec(memory_space=pl.ANY),
        scratch_shapes=[pltpu.VMEM((2, LROWS, d + LANES), F32), pltpu.SemaphoreType.DMA((2,))],
    )
    return pl.pallas_call(
        _dispatch_kernel,
        grid_spec=grid_spec,
        out_shape=jax.ShapeDtypeStruct((n_rows, d + LANES), F32),
        compiler_params=_cparams(("arbitrary",)),
        name="moe_dispatch",
    )(*tabs, xn_p, xn_s, lpos_t, gate_t)


def _ffn_kernel(ge_ref, gs_ref, gn_ref, gt_ref, ng_ref, xs_hbm, wgu_hbm, wd_hbm, bgu_ref, bdn_ref, out_hbm,
                xbuf, acc, ostage, wg_st, wu_st, wd_st, wg_bf, wu_bf, wd_bf, sem_w, sem_x, sem_o):
    d = acc.shape[1]
    d_ff = wd_hbm.shape[1]
    nf = d_ff // TF
    n_groups = ng_ref[0]
    total = n_groups * nf

    def w_copies(s, slot):
        g = s // nf
        f = s - g * nf
        e = ge_ref[g]
        c0 = pl.multiple_of(f * TF, TF)
        return (pltpu.make_async_copy(wgu_hbm.at[e, :, pl.ds(c0, TF)], wg_st.at[slot], sem_w.at[slot, 0]),
                pltpu.make_async_copy(wgu_hbm.at[e, :, pl.ds(d_ff + c0, TF)], wu_st.at[slot], sem_w.at[slot, 1]),
                pltpu.make_async_copy(wd_hbm.at[e, pl.ds(c0, TF), :], wd_st.at[slot], sem_w.at[slot, 2]))

    def x_copy(g, j, size=SUB):
        r0 = pl.multiple_of(j * SUB, SUB)
        return pltpu.make_async_copy(xs_hbm.at[pl.ds(pl.multiple_of(gs_ref[g] + r0, CHUNK_ALIGN), size)],
                                     xbuf.at[g % 2, pl.ds(r0, size)], sem_x)

    def o_copy(g, j, size=SUB):
        r0 = pl.multiple_of(j * SUB, SUB)
        return pltpu.make_async_copy(ostage.at[j % 2, pl.ds(0, size)],
                                     out_hbm.at[pl.ds(pl.multiple_of(gs_ref[g] + r0, CHUNK_ALIGN), size)],
                                     sem_o.at[j % 2])

    def loop(n, fn):
        lax.fori_loop(0, n, lambda j, c: (fn(j), c)[1], 0)

    def group_rows(g, op):
        loop(gn_ref[g], lambda j: op(x_copy(g, j)))

        @pl.when(gt_ref[g] == 1)
        def _():
            op(x_copy(g, gn_ref[g], SUB_TAIL))

    def drain_stores(g):
        n = gn_ref[g]
        tail = gt_ref[g]

        @pl.when(tail == 1)
        def _():
            o_copy(g, n, SUB_TAIL).wait()

        @pl.when(n >= 1)
        def _():
            o_copy(g, n - 1).wait()

        @pl.when(jnp.logical_and(n >= 2, tail == 0))
        def _():
            o_copy(g, n - 2).wait()

    @pl.when(total > 0)
    def _():
        for c in w_copies(0, 0):
            c.start()
        group_rows(0, lambda c: c.start())

    def item(s, carry):
        slot = s % 2
        g = s // nf
        f = s - g * nf
        e = ge_ref[g]
        nsub = gn_ref[g]
        xg = xbuf.at[g % 2]

        @pl.when(s + 1 < total)
        def _():
            for c in w_copies(s + 1, 1 - slot):
                c.start()

        @pl.when(f == 0)
        def _():
            group_rows(g, lambda c: c.wait())

        @pl.when(jnp.logical_and(f == 1, g + 1 < n_groups))
        def _():
            group_rows(g + 1, lambda c: c.start())

        for c in w_copies(s, slot):
            c.wait()
        bg = bgu_ref[pl.ds(e * 2 * nf + f, 1), :]
        bu = bgu_ref[pl.ds(e * 2 * nf + nf + f, 1), :]

        def sub_block(j, phase, cast=False, size=SUB):
            rs = pl.ds(pl.multiple_of(j * SUB, SUB), size)
            x = xg[rs, 0:d].astype(BF16)
            if cast:
                wg, wu, wd = (st[slot].astype(BF16) for st in (wg_st, wu_st, wd_st))
                wg_bf[...], wu_bf[...], wd_bf[...] = wg, wu, wd
            else:
                wg, wu, wd = wg_bf[...], wu_bf[...], wd_bf[...]
            hg = jnp.dot(x, wg, preferred_element_type=F32) + bg
            hu = jnp.dot(x, wu, preferred_element_type=F32) + bu
            gg = jnp.minimum(hg, SWIGLU_LIMIT)
            uu = jnp.clip(hu, -SWIGLU_LIMIT, SWIGLU_LIMIT)
            act = gg * (1.0 / (1.0 + jnp.exp(-SWIGLU_ALPHA * gg))) * (uu + 1.0)
            part = jnp.dot(act.astype(BF16), wd, preferred_element_type=F32)
            if phase == "middle":
                acc[rs, :] += part
            else:
                @pl.when(j >= 2)
                def _():
                    o_copy(g, j - 2).wait()

                ostage[j % 2, 0:size] = (acc[rs, :] + part) * xg[rs, d:d + 1]
                o_copy(g, j, size).start()

        tail = gt_ref[g] == 1

        def storing_blocks():
            sub_block(jnp.int32(0), "last", cast=True)

            def pair(p):
                sub_block(2 * p + 1, "last")
                sub_block(2 * p + 2, "last")

            loop((nsub - 1) // 2, pair)

            @pl.when(nsub % 2 == 0)
            def _():
                sub_block(nsub - 1, "last")

            @pl.when(tail)
            def _():
                sub_block(nsub, "last", size=SUB_TAIL)

        def accumulating_blocks(phase):
            @pl.when(nsub < 2)
            def _():
                sub_block(jnp.int32(0), phase, cast=True)

                @pl.when(tail)
                def _():
                    sub_block(jnp.int32(1), phase, size=SUB_TAIL)

            @pl.when(nsub >= 2)
            def _():
                sub_block(jnp.int32(0), phase, cast=True, size=2 * SUB)
                rest = 2 * (nsub - 2) + gt_ref[g]
                for units in range(1, (RG - 2 * SUB) // SUB_TAIL + 1):
                    @pl.when(rest == units)
                    def _(units=units):
                        sub_block(jnp.int32(2), phase, size=units * SUB_TAIL)

        @pl.when(f == 0)
        def _():
            acc[...] = jnp.broadcast_to(bdn_ref[pl.ds(e, 1), :], acc.shape)

        @pl.when(f < nf - 1)
        def _():
            accumulating_blocks("middle")

        @pl.when(f == nf - 1)
        def _():
            @pl.when(g > 0)
            def _():
                drain_stores(g - 1)

            storing_blocks()

        return carry

    lax.fori_loop(0, total, item, 0)

    @pl.when(total > 0)
    def _():
        drain_stores(n_groups - 1)


def _moe_ffn(groups, x_sorted, w_gate_up, b_gu, w_down, b_dn):
    n_rows = x_sorted.shape[0]
    _, d_ff, d = w_down.shape
    nf = d_ff // TF
    assert nf >= 2
    bgu2 = b_gu.reshape(N_EXPERTS * 2 * nf, TF)
    grid_spec = pltpu.PrefetchScalarGridSpec(
        num_scalar_prefetch=5,
        grid=(1,),
        in_specs=[pl.BlockSpec(memory_space=pl.ANY), pl.BlockSpec(memory_space=pl.ANY),
                  pl.BlockSpec(memory_space=pl.ANY),
                  pl.BlockSpec(bgu2.shape, lambda i, *_: (0, 0), pipeline_mode=pl.Buffered(1)),
                  pl.BlockSpec(b_dn.shape, lambda i, *_: (0, 0), pipeline_mode=pl.Buffered(1))],
        out_specs=pl.BlockSpec(memory_space=pl.ANY),
        scratch_shapes=[pltpu.VMEM((2, RG, d + LANES), F32), pltpu.VMEM((RG, d), F32),
                        pltpu.VMEM((2, SUB, d), F32),
                        pltpu.VMEM((2, d, TF), F32), pltpu.VMEM((2, d, TF), F32), pltpu.VMEM((2, TF, d), F32),
                        pltpu.VMEM((d, TF), BF16), pltpu.VMEM((d, TF), BF16), pltpu.VMEM((TF, d), BF16),
                        pltpu.SemaphoreType.DMA((2, 3)), pltpu.SemaphoreType.DMA(()),
                        pltpu.SemaphoreType.DMA((2,))],
    )
    return pl.pallas_call(
        _ffn_kernel,
        grid_spec=grid_spec,
        out_shape=jax.ShapeDtypeStruct((n_rows, d), F32),
        compiler_params=pltpu.CompilerParams(dimension_semantics=("arbitrary",),
                                             vmem_limit_bytes=VMEM_LIMIT_FFN),
        name="moe_ffn",
    )(*groups, x_sorted, w_gate_up, w_down, bgu2, b_dn)


def _route(logits, m_pad):
    m = logits.shape[0]
    nt = m_pad // TT
    top_v, top_i = lax.top_k(logits, TOP_K)
    gate = jnp.pad(jax.nn.softmax(top_v, axis=-1), ((0, m_pad - m), (0, 0)))
    top_i = jnp.pad(top_i.astype(jnp.int32), ((0, m_pad - m), (0, 0)), constant_values=-1)
    chosen = (top_i[:, :, None] == jnp.arange(N_EXPERTS, dtype=jnp.int32)).astype(jnp.int32)
    tiles = chosen.sum(axis=1).reshape(nt, TT, N_EXPERTS)
    cnt = (tiles.sum(axis=1) + CHUNK_ALIGN - 1) // CHUNK_ALIGN * CHUNK_ALIGN
    loff = jnp.cumsum(cnt, axis=1) - cnt
    seg = cnt.sum(axis=0)
    seg_start = jnp.cumsum(seg) - seg
    dest = seg_start[None, :] + jnp.cumsum(cnt, axis=0) - cnt
    before = jnp.asarray(np.tril(np.ones((TT, TT), np.float32), -1), BF16)
    rank = jnp.einsum('ij,tje->tie', before, tiles.astype(BF16),
                      preferred_element_type=F32).astype(jnp.int32)
    lpos_all = (loff[:, None, :] + rank).reshape(m_pad, N_EXPERTS)
    lpos = (lpos_all[:, None, :] * chosen).sum(axis=-1)
    lpos = jnp.where(top_i >= 0, lpos, -1).astype(jnp.int32)
    n_rows_bound = nt * LROWS
    n_grp_max = n_rows_bound // RG + N_EXPERTS
    grp = (seg + RG - 1) // RG
    grp_end = jnp.cumsum(grp)
    gi = jnp.arange(n_grp_max, dtype=jnp.int32)
    g_exp = jnp.minimum((gi[:, None] >= grp_end[None, :]).sum(axis=1), N_EXPERTS - 1).astype(jnp.int32)
    within = gi - (grp_end[g_exp] - grp[g_exp])
    g_start = (seg_start[g_exp] + within * RG).astype(jnp.int32)
    g_rows = jnp.clip(seg[g_exp] - within * RG, 0, RG)
    g_rows = jnp.where(gi < grp_end[-1], g_rows, 0)
    n_full = g_rows // SUB
    rest = g_rows - n_full * SUB
    g_tail = (rest > 0) & (rest <= SUB_TAIL) & (n_full >= 1)
    g_nsub = n_full + ((rest > 0) & ~g_tail)
    groups = (g_exp, g_start, g_nsub.astype(jnp.int32), g_tail.astype(jnp.int32),
              grp_end[-1:].astype(jnp.int32))
    tabs = tuple(a.reshape(-1).astype(jnp.int32) for a in (cnt, loff, dest))
    return tabs, lpos, gate, groups, n_rows_bound + SUB


def _combine_kernel(cnt_ref, loff_ref, dest_ref, src_hbm, lpos_ref, x1p_ref, x1s_ref, g_ref,
                    yp_ref, ys_ref, buf, sem):
    t = pl.program_id(0)
    last = pl.num_programs(0) - 1

    def copy(tile, off, src, size):
        return pltpu.make_async_copy(src_hbm.at[pl.ds(src, size)], buf.at[tile % 2, pl.ds(off, size)],
                                     sem.at[tile % 2])

    @pl.when(t == 0)
    def _():
        buf[...] = jnp.zeros_like(buf)
        _chunk_loop(cnt_ref, loff_ref, dest_ref, t, lambda o, g, s: copy(t, o, g, s).start())

    @pl.when(t < last)
    def _():
        _chunk_loop(cnt_ref, loff_ref, dest_ref, t + 1, lambda o, g, s: copy(t + 1, o, g, s).start())

    _chunk_loop(cnt_ref, loff_ref, dest_ref, t, lambda o, g, s: copy(t, o, g, s).wait())
    cols = lax.broadcasted_iota(jnp.int32, (TT, LROWS), 1)
    sel = jnp.zeros((TT, LROWS), F32)
    for k in range(TOP_K):
        sel = sel + jnp.where(cols == lpos_ref[:, k:k + 1], 1.0, 0.0)
    f = jnp.dot(sel.astype(BF16), buf[t % 2].astype(BF16), preferred_element_type=F32)

    @pl.when(t < last)
    def _():
        yp_ref[...] = _rms(x1p_ref[...] + f, g_ref[...])

    @pl.when(t == last)
    def _():
        ys_ref[...] = _rms(x1s_ref[...] + f, g_ref[...])


def _combine(tabs, out_sorted, lpos, x1p, x1s, g):
    d = x1p.shape[1]
    nt = lpos.shape[0] // TT
    last_p = x1p.shape[0] // TT - 1
    prompt = pl.BlockSpec((TT, d), lambda t, *_: (jnp.minimum(t, last_p), 0))
    sample = pl.BlockSpec((TT, d), lambda t, *_: (0, 0))
    grid_spec = pltpu.PrefetchScalarGridSpec(
        num_scalar_prefetch=3,
        grid=(nt,),
        in_specs=[pl.BlockSpec(memory_space=pl.ANY),
                  pl.BlockSpec((TT, TOP_K), lambda t, *_: (t, 0)),
                  prompt, sample,
                  pl.BlockSpec((1, d), lambda t, *_: (0, 0))],
        out_specs=[prompt, sample],
        scratch_shapes=[pltpu.VMEM((2, LROWS, d), F32), pltpu.SemaphoreType.DMA((2,))],
    )
    return pl.pallas_call(
        _combine_kernel,
        grid_spec=grid_spec,
        out_shape=[jax.ShapeDtypeStruct(x1p.shape, F32), jax.ShapeDtypeStruct((TT, d), F32)],
        compiler_params=_cparams(("arbitrary",)),
        name="moe_combine",
    )(*tabs, out_sorted, lpos, x1p, x1s, g.reshape(1, d))


def _t5_bucket(dist):
    max_exact = N_BUCKETS // 2
    dd = dist.astype(F32)
    large = max_exact + (jnp.log(jnp.maximum(dd, 1.0) / max_exact)
                         / math.log(BUCKET_MAX_DIST / max_exact) * (N_BUCKETS - max_exact)).astype(jnp.int32)
    large = jnp.minimum(large, N_BUCKETS - 1)
    return jnp.where(dist < max_exact, dist, large)


def _bias_tables(rel_bias):
    dist = jnp.asarray(np.arange(N_TAPS)[None, :] * np.array(DILATIONS)[:, None], jnp.int32)
    bias = jnp.transpose(rel_bias[_t5_bucket(dist)], (2, 0, 1)).astype(F32)
    by_branch = jnp.transpose(bias, (1, 0, 2))
    g, h = by_branch.shape[:2]
    row = jnp.concatenate([by_branch[:, :, ::-1], jnp.full((g, h, WIN), NEG, F32)], axis=-1)
    flat = jnp.broadcast_to(row[:, :, None, :], (g, h, WIN, 2 * WIN + 1)).reshape(g, h, -1)
    band = flat[:, :, :WIN * 2 * WIN].reshape(g, h, WIN, 2 * WIN)
    return band, by_branch[:, :, :0:-1], by_branch[:, :, 0:1]


def kernel(x_prompt, x_sample, cache_k_win, cache_v_win, state_conv, state_ssm, rel_bias, attn_norm, w_in, conv_w, conv_b, dt_bias, a_log, d_skip, ssd_norm, w_out, ffn_norm, w_router, b_router, w_gate_up, b_gate_up, w_down, b_down, final_norm):
    bp, tp, d = x_prompt.shape
    bs, ts, _ = x_sample.shape
    depth = w_in.shape[0]
    assert depth == 1 and ts == 1 and tp % (max(DILATIONS) * WIN) == 0
    keep = min(max(DILATIONS) * WIN, tp)
    band, samp, samp0 = _bias_tables(rel_bias)
    l = 0

    xp = x_prompt.reshape(bp * tp, d)
    xs = x_sample.reshape(bs * ts, d)
    w_in_bf = jnp.pad(w_in[l], ((0, 0), (0, IN_PROJ_PAD - IN_PROJ))).astype(BF16)
    w_out_bf = w_out[l].astype(BF16)
    wr = jnp.pad(w_router[l], ((0, 0), (0, LANES - N_EXPERTS))).astype(BF16)
    br = jnp.pad(b_router[l], (0, LANES - N_EXPERTS), constant_values=NEG).reshape(1, LANES)

    k, v, z, xbc, dt_raw, *branch_qkv = _inproj_prompt(xp, attn_norm[l], w_in_bf, bp)
    k3 = k.reshape(bp, tp, KV_WIDTH)
    v3 = v.reshape(bp, tp, KV_WIDTH)
    att_parts = [_attn_branch(branch_qkv[2 * gi], branch_qkv[2 * gi + 1], band[gi])
                 for gi in range(len(DILATIONS))]
    ssm, st_p = _ssd_prompt(xbc, dt_raw, z, conv_w[l], conv_b[l], dt_bias[l], a_log[l], d_skip[l],
                            ssd_norm[l], bp)
    x1p, xnp_, lgp = _outproj_prompt(att_parts, ssm, xp, w_out_bf, ffn_norm[l], wr, br)
    k_win_p = k3[:, tp - keep:].reshape(1, bp, keep, N_KV_HEADS, HEAD_DIM)
    v_win_p = v3[:, tp - keep:].reshape(1, bp, keep, N_KV_HEADS, HEAD_DIM)
    conv_p = xbc.reshape(bp, tp, CONV_DIM)[:, tp - (CONV_W - 1):][None]

    q_s, k_s, v_s, z_s, xbc_s, dt_s = _inproj(xs, attn_norm[l], w_in_bf, bs * ts)
    q_s = q_s.reshape(bs, N_ATT_HEADS, HEAD_DIM)
    k_s = k_s.reshape(bs, N_KV_HEADS, HEAD_DIM)
    v_s = v_s.reshape(bs, N_KV_HEADS, HEAD_DIM)
    wbuf = cache_k_win.shape[2]
    att_s = _attn_sample(q_s, k_s, v_s, cache_k_win[l].reshape(bs, wbuf, KV_WIDTH),
                         cache_v_win[l].reshape(bs, wbuf, KV_WIDTH), samp, samp0)
    xa_s, xdt_s, decay_s = _conv_sample(xbc_s, state_conv[l], conv_w[l], conv_b[l], dt_s, dt_bias[l],
                                        a_log[l])
    nbc = N_SSM_GROUPS * D_STATE
    h_s, y_s = _ssm_sample(xdt_s, decay_s, xa_s[:, SSM_WIDTH:SSM_WIDTH + nbc], xa_s[:, SSM_WIDTH + nbc:],
                           state_ssm[l])
    x1s, xns, lgs = _outproj_sample(att_s.reshape(bs, ATT_WIDTH), y_s, xa_s[:, :SSM_WIDTH], z_s,
                                    d_skip[l], ssd_norm[l], xs, w_out_bf, ffn_norm[l], wr, br)
    conv_s = jnp.concatenate([state_conv[l][:, 1:], xbc_s[:, None]], axis=1)[None]

    n_s = bs * ts
    assert (bp * tp) % TT == 0 and n_s <= TT
    m_pad = bp * tp + TT
    logits = jnp.concatenate([lgp[:, :N_EXPERTS], lgs[:, :N_EXPERTS]], axis=0)
    tabs, lpos, gate, groups, n_rows = _route(logits, m_pad)
    by_tile = lambda a: jnp.transpose(a.reshape(m_pad // TT, TT, TOP_K), (0, 2, 1))
    pad_rows = lambda a: jnp.pad(a, ((0, TT - n_s), (0, 0)))
    x_sorted = _dispatch(tabs, xnp_, pad_rows(xns), by_tile(lpos), by_tile(gate), n_rows)
    out_sorted = _moe_ffn(groups, x_sorted, w_gate_up[l], b_gate_up[l], w_down[l], b_down[l])
    y_p, y_s_out = _combine(tabs, out_sorted, lpos, x1p, pad_rows(x1s), final_norm)
    y_s_out = y_s_out[:n_s]

    return (y_p.reshape(bp, tp, d), y_s_out.reshape(bs, ts, d), k_win_p, v_win_p, conv_p, st_p[None],
            k_s.reshape(1, bs, ts, N_KV_HEADS, HEAD_DIM), v_s.reshape(1, bs, ts, N_KV_HEADS, HEAD_DIM),
            conv_s, h_s[None])
```

```python
import functools
import math

import jax
import jax.numpy as jnp
import numpy as np
from jax import lax
from jax.experimental import pallas as pl
from jax.experimental.pallas import tpu as pltpu

F32 = jnp.float32
BF16 = jnp.bfloat16
HIGHEST = lax.Precision.HIGHEST

LANES = 128
SUBLANES = 8
VMEM_LIMIT = 56 * 1024 * 1024

HEAD_DIM = 64
N_ATT_HEADS = 16
N_KV_HEADS = 4
KV_REP = N_ATT_HEADS // N_KV_HEADS
ATT_WIDTH = N_ATT_HEADS * HEAD_DIM
KV_WIDTH = N_KV_HEADS * HEAD_DIM
DILATIONS = (1, 4, 16)
N_TAPS = 129
WIN = N_TAPS - 1
ATT_SCALE = HEAD_DIM ** -0.5
N_BUCKETS = 32
BUCKET_MAX_DIST = 2048
SSM_HEAD_DIM = 64
N_SSM_HEADS = 16
SSM_WIDTH = N_SSM_HEADS * SSM_HEAD_DIM
N_SSM_GROUPS = 2
HEADS_PER_GROUP = N_SSM_HEADS // N_SSM_GROUPS
D_STATE = 128
CONV_W = 4
CONV_DIM = SSM_WIDTH + 2 * N_SSM_GROUPS * D_STATE
SSD_CHUNK = 128
N_EXPERTS = 32
TOP_K = 4
SWIGLU_LIMIT = 7.0
SWIGLU_ALPHA = 1.702
EPS = 1e-5
NEG = -1e30

Q0, K0, V0, Z0, X0, DT0 = 0, 1024, 1280, 1536, 2560, 4096
IN_PROJ = DT0 + N_SSM_HEADS
IN_PROJ_PAD = DT0 + LANES

TM_PROJ = 512
TM_OUT = 256
TQ_ATT = WIN


def _cparams(sem):
    return pltpu.CompilerParams(dimension_semantics=sem, vmem_limit_bytes=VMEM_LIMIT)


def _const_spec(shape):
    nd = len(shape)
    return pl.BlockSpec(shape, lambda *_: (0,) * nd, pipeline_mode=pl.Buffered(1))


def _spread(a, onehot_bf, terms):
    out, rest = None, a
    for _ in range(terms):
        piece = rest.astype(BF16)
        part = jnp.dot(piece, onehot_bf, preferred_element_type=F32)
        out = part if out is None else out + part
        rest = rest - piece.astype(F32)
    return out


def _rms(x, g):
    ms = jnp.mean(x * x, axis=-1, keepdims=True)
    return x * lax.rsqrt(ms + EPS) * g


def _silu(x):
    return x * (1.0 / (1.0 + jnp.exp(-x)))


def _inproj_kernel(x_ref, g_ref, w_ref, q_ref, k_ref, v_ref, z_ref, xbc_ref, dt_ref):
    xn = _rms(x_ref[...], g_ref[...]).astype(BF16)

    def mm(lo, hi):
        return jnp.dot(xn, w_ref[:, lo:hi], preferred_element_type=F32)

    q_ref[...] = (mm(Q0, K0) * ATT_SCALE).astype(BF16)
    k_ref[...] = mm(K0, V0)
    v_ref[...] = mm(V0, Z0)
    z_ref[...] = mm(Z0, X0)
    xbc_ref[...] = mm(X0, DT0)
    dt_ref[...] = mm(DT0, IN_PROJ_PAD)


def _inproj(x2d, g, w_bf, tm):
    m, d = x2d.shape
    widths = (ATT_WIDTH, KV_WIDTH, KV_WIDTH, SSM_WIDTH, CONV_DIM, LANES)
    dtypes = (BF16, F32, F32, F32, F32, F32)
    return pl.pallas_call(
        _inproj_kernel,
        grid=(m // tm,),
        in_specs=[pl.BlockSpec((tm, d), lambda i: (i, 0)),
                  _const_spec((1, d)),
                  _const_spec((d, IN_PROJ_PAD))],
        out_specs=[pl.BlockSpec((tm, w), lambda i: (i, 0)) for w in widths],
        out_shape=[jax.ShapeDtypeStruct((m, w), t) for w, t in zip(widths, dtypes)],
        compiler_params=_cparams(("parallel",)),
        name=f"inproj_{tm}",
    )(x2d, g.reshape(1, d), w_bf)


def _deinterleave_matrix(n, dil):
    p = np.zeros((n, n), np.float32)
    src = np.arange(n)
    p[(src % dil) * (n // dil) + src // dil, src] = 1.0
    return p


def _inproj_prompt_kernel(x_ref, g_ref, w_ref, p_mid_ref, p_far_ref, k_ref, v_ref, z_ref, xbc_ref, dt_ref,
                          q0_ref, kv0_ref, q1_ref, kv1_ref, q2_ref, kv2_ref):
    tm = x_ref.shape[0]
    xn = _rms(x_ref[...], g_ref[...]).astype(BF16)

    def mm(lo, hi):
        return jnp.dot(xn, w_ref[:, lo:hi], preferred_element_type=F32)

    q = (mm(Q0, K0) * ATT_SCALE).astype(BF16)
    k = mm(K0, V0)
    v = mm(V0, Z0)
    k_ref[...] = k
    v_ref[...] = v
    z_ref[...] = mm(Z0, X0)
    xbc_ref[...] = mm(X0, DT0)
    dt_ref[...] = mm(DT0, IN_PROJ_PAD)
    kv = jnp.concatenate([k, v], axis=-1).astype(BF16)
    q0_ref[0, 0] = q
    kv0_ref[0, 0] = kv
    for dil, p_ref, qd_ref, kvd_ref in ((DILATIONS[1], p_mid_ref, q1_ref, kv1_ref),
                                        (DILATIONS[2], p_far_ref, q2_ref, kv2_ref)):
        qp = jnp.dot(p_ref[...], q, preferred_element_type=F32).astype(BF16)
        kvp = jnp.dot(p_ref[...], kv, preferred_element_type=F32).astype(BF16)
        rows = tm // dil
        for r in range(dil):
            qd_ref[0, r] = qp[r * rows:(r + 1) * rows]
            kvd_ref[0, r] = kvp[r * rows:(r + 1) * rows]


def _inproj_prompt(x2d, g, w_bf, batch):
    m, d = x2d.shape
    tm = TM_PROJ
    seq = m // batch
    per_b = seq // tm
    assert DILATIONS[0] == 1 and seq % tm == 0 and tm % (max(DILATIONS) * 2 * SUBLANES) == 0
    widths = (KV_WIDTH, KV_WIDTH, SSM_WIDTH, CONV_DIM, LANES)
    rows = lambda w: pl.BlockSpec((tm, w), lambda i: (i, 0))
    split = lambda dil, w: pl.BlockSpec((1, dil, tm // dil, w), lambda i: (i // per_b, 0, i % per_b, 0))
    branch_specs, branch_shapes = [], []
    for dil in DILATIONS:
        for w in (ATT_WIDTH, 2 * KV_WIDTH):
            branch_specs.append(split(dil, w))
            branch_shapes.append(jax.ShapeDtypeStruct((batch, dil, seq // dil, w), BF16))
    perms = [jnp.asarray(_deinterleave_matrix(tm, dil), BF16) for dil in DILATIONS[1:]]
    return pl.pallas_call(
        _inproj_prompt_kernel,
        grid=(m // tm,),
        in_specs=[rows(d), _const_spec((1, d)), _const_spec((d, IN_PROJ_PAD)),
                  _const_spec((tm, tm)), _const_spec((tm, tm))],
        out_specs=[rows(w) for w in widths] + branch_specs,
        out_shape=[jax.ShapeDtypeStruct((m, w), F32) for w in widths] + branch_shapes,
        compiler_params=_cparams(("parallel",)),
        name="inproj_prompt",
    )(x2d, g.reshape(1, d), w_bf, *perms)


def _attn_kernel(q_ref, kvp_ref, kvc_ref, bias_ref, o_ref, lse_ref):
    first = pl.program_id(2) == 0
    lane = lax.broadcasted_iota(jnp.int32, (1, 2 * WIN), 1)
    prev_mask = jnp.where(jnp.logical_and(first, lane < WIN), NEG, 0.0)
    head_lane = lax.broadcasted_iota(jnp.int32, (WIN, LANES), 1)
    for blk in range(TQ_ATT // WIN):
        rows = slice(blk * WIN, (blk + 1) * WIN)
        lse_tile = jnp.zeros((WIN, LANES), F32)
        for kvh in range(N_KV_HEADS):
            ks = slice(kvh * HEAD_DIM, (kvh + 1) * HEAD_DIM)
            vs = slice(KV_WIDTH + kvh * HEAD_DIM, KV_WIDTH + (kvh + 1) * HEAD_DIM)
            if blk == 0:
                kw = jnp.concatenate([kvp_ref[0, 0, :, ks], kvc_ref[0, 0, 0:WIN, ks]], axis=0)
                vw = jnp.concatenate([kvp_ref[0, 0, :, vs], kvc_ref[0, 0, 0:WIN, vs]], axis=0)
            else:
                kw = kvc_ref[0, 0, (blk - 1) * WIN:(blk + 1) * WIN, ks]
                vw = kvc_ref[0, 0, (blk - 1) * WIN:(blk + 1) * WIN, vs]
            for pair in range(KV_REP // 2):
                outs = []
                for r in range(2):
                    h = kvh * KV_REP + pair * 2 + r
                    qh = q_ref[0, 0, rows, h * HEAD_DIM:(h + 1) * HEAD_DIM]
                    s = lax.dot_general(qh, kw, (((1,), (1,)), ((), ())), preferred_element_type=F32)
                    s = s + bias_ref[h]
                    if blk == 0:
                        s = s + prev_mask
                    mx = jnp.max(s, axis=-1, keepdims=True)
                    p = jnp.exp(s - mx)
                    l = jnp.sum(p, axis=-1, keepdims=True)
                    o = jnp.dot(p.astype(BF16), vw, preferred_element_type=F32)
                    outs.append(o * (1.0 / l))
                    lse_tile = jnp.where(head_lane == h, mx + jnp.log(l), lse_tile)
                h0 = kvh * KV_REP + pair * 2
                o_ref[0, 0, rows, h0 * HEAD_DIM:(h0 + 2) * HEAD_DIM] = (
                    jnp.concatenate(outs, axis=-1).astype(BF16))
        lse_ref[0, 0, rows, :] = lse_tile


def _attn_branch(q, kv, bias_mat):
    b, dil, sub, _ = q.shape
    assert sub % TQ_ATT == 0
    nb = sub // TQ_ATT
    per_step = TQ_ATT // WIN
    cur = lambda bb, r, i: (bb, r, i, 0)
    prev = lambda bb, r, i: (bb, r, jnp.maximum(i * per_step - 1, 0), 0)
    return pl.pallas_call(
        _attn_kernel,
        grid=(b, dil, nb),
        in_specs=[pl.BlockSpec((1, 1, TQ_ATT, ATT_WIDTH), cur),
                  pl.BlockSpec((1, 1, WIN, 2 * KV_WIDTH), prev),
                  pl.BlockSpec((1, 1, TQ_ATT, 2 * KV_WIDTH), cur),
                  _const_spec((N_ATT_HEADS, WIN, 2 * WIN))],
        out_specs=[pl.BlockSpec((1, 1, TQ_ATT, ATT_WIDTH), cur),
                   pl.BlockSpec((1, 1, TQ_ATT, LANES), cur)],
        out_shape=[jax.ShapeDtypeStruct((b, dil, sub, ATT_WIDTH), BF16),
                   jax.ShapeDtypeStruct((b, dil, sub, LANES), F32)],
        compiler_params=_cparams(("parallel", "parallel", "arbitrary")),
        name=f"attn_dil{dil}",
    )(q, kv, kv, bias_mat)


def _softplus(x):
    return jnp.maximum(x, 0.0) + jnp.log(1.0 + jnp.exp(-jnp.abs(x)))


def _ssd_kernel(xbc_ref, dt_ref, z_ref, cw_ref, cb_ref, dtb_ref, alog_ref, dskip_ref, gn_ref, e_ref,
                y_ref, st_ref, ext_ref, state_ref):
    c = pl.program_id(1)
    L = SSD_CHUNK

    @pl.when(c == 0)
    def _():
        ext_ref[0:SUBLANES, :] = jnp.zeros((SUBLANES, CONV_DIM), F32)
        state_ref[...] = jnp.zeros_like(state_ref)

    ext_ref[SUBLANES:SUBLANES + L, :] = xbc_ref[...]
    acc = cb_ref[...] + ext_ref[SUBLANES:SUBLANES + L, :] * cw_ref[CONV_W - 1:CONV_W, :]
    for i in range(CONV_W - 1):
        off = SUBLANES - (CONV_W - 1) + i
        acc = acc + ext_ref[off:off + L, :] * cw_ref[i:i + 1, :]
    ext_ref[0:SUBLANES, :] = ext_ref[L:L + SUBLANES, :]
    xa = _silu(acc)

    dt = _softplus(dt_ref[...] + dtb_ref[...])
    da = dt * (-jnp.exp(alog_ref[...]))
    row = lax.broadcasted_iota(jnp.int32, (L, L), 0)
    col = lax.broadcasted_iota(jnp.int32, (L, L), 1)
    tri = row >= col
    a_cs = jnp.dot(tri.astype(F32), da, preferred_element_type=F32, precision=HIGHEST)
    a_cs_t = a_cs.T
    expand = e_ref[...]
    acs_full = _spread(a_cs, expand, 3)
    dt_full = _spread(dt, expand, 3)
    exp_acs = jnp.exp(acs_full)
    a_last = acs_full[L - 1:L, :]
    exp_last = exp_acs[L - 1:L, :]
    xs = xa[:, :SSM_WIDTH]
    xdt = xs * dt_full
    xw = xdt * jnp.exp(a_last - acs_full)

    for g in range(N_SSM_GROUPS):
        b0 = SSM_WIDTH + g * D_STATE
        c0 = SSM_WIDTH + N_SSM_GROUPS * D_STATE + g * D_STATE
        bg_t = xa[:, b0:b0 + D_STATE].T.astype(BF16)
        cg = xa[:, c0:c0 + D_STATE].astype(BF16)
        gram = jnp.dot(cg, bg_t, preferred_element_type=F32)
        gs = slice(g * HEADS_PER_GROUP * SSM_HEAD_DIM, (g + 1) * HEADS_PER_GROUP * SSM_HEAD_DIM)
        st = state_ref[g]
        y_off = jnp.dot(cg, st.astype(BF16), preferred_element_type=F32) * exp_acs[:, gs]
        state_ref[g] = exp_last[:, gs] * st + jnp.dot(bg_t, xw[:, gs].astype(BF16),
                                                      preferred_element_type=F32)
        y_diag = []
        for hh in range(HEADS_PER_GROUP):
            h = g * HEADS_PER_GROUP + hh
            hs = slice(h * SSM_HEAD_DIM, (h + 1) * SSM_HEAD_DIM)
            seg = jnp.where(tri, a_cs[:, h:h + 1] - a_cs_t[h:h + 1, :], NEG)
            scores = (gram * jnp.exp(seg)).astype(BF16)
            y_diag.append(jnp.dot(scores, xdt[:, hs].astype(BF16), preferred_element_type=F32))
        y_ref[:, gs] = jnp.concatenate(y_diag, axis=-1) + y_off

    y = y_ref[...] + dskip_ref[...] * xs
    u = y * _silu(z_ref[...])
    gw = SSM_WIDTH // N_SSM_GROUPS
    parts = []
    for g in range(N_SSM_GROUPS):
        ug = u[:, g * gw:(g + 1) * gw]
        parts.append(ug * lax.rsqrt(jnp.mean(ug * ug, axis=-1, keepdims=True) + EPS))
    y_ref[...] = jnp.concatenate(parts, axis=-1) * gn_ref[...]

    @pl.when(c == pl.num_programs(1) - 1)
    def _():
        st_ref[0] = state_ref[...]


def _head_expand():
    e = np.zeros((LANES, SSM_WIDTH), np.float32)
    for h in range(N_SSM_HEADS):
        e[h, h * SSM_HEAD_DIM:(h + 1) * SSM_HEAD_DIM] = 1.0
    return jnp.asarray(e)


def _pad_lanes(v):
    return jnp.pad(v.astype(F32), (0, LANES - v.shape[0])).reshape(1, LANES)


def _ssd_prompt(xbc, dt_raw, z, conv_w, conv_b, dt_bias, a_log, d_skip, ssd_norm, batch):
    m = xbc.shape[0]
    nc = m // batch // SSD_CHUNK
    L = SSD_CHUNK
    group_w = HEADS_PER_GROUP * SSM_HEAD_DIM
    rows = lambda b, c: (b * nc + c, 0)
    y, st = pl.pallas_call(
        _ssd_kernel,
        grid=(batch, nc),
        in_specs=[pl.BlockSpec((L, CONV_DIM), rows),
                  pl.BlockSpec((L, LANES), rows),
                  pl.BlockSpec((L, SSM_WIDTH), rows),
                  _const_spec((CONV_W, CONV_DIM)),
                  _const_spec((1, CONV_DIM)),
                  _const_spec((1, LANES)),
                  _const_spec((1, LANES)),
                  _const_spec((1, SSM_WIDTH)),
                  _const_spec((1, SSM_WIDTH)),
                  _const_spec((LANES, SSM_WIDTH))],
        out_specs=[pl.BlockSpec((L, SSM_WIDTH), rows),
                   pl.BlockSpec((1, N_SSM_GROUPS, D_STATE, group_w), lambda b, c: (b, 0, 0, 0))],
        out_shape=[jax.ShapeDtypeStruct((m, SSM_WIDTH), F32),
                   jax.ShapeDtypeStruct((batch, N_SSM_GROUPS, D_STATE, group_w), F32)],
        scratch_shapes=[pltpu.VMEM((SUBLANES + L, CONV_DIM), F32),
                        pltpu.VMEM((N_SSM_GROUPS, D_STATE, group_w), F32)],
        compiler_params=_cparams(("parallel", "arbitrary")),
        name="ssd_prompt",
    )(xbc, dt_raw, z, conv_w, conv_b.reshape(1, CONV_DIM), _pad_lanes(dt_bias), _pad_lanes(a_log),
      jnp.repeat(d_skip, SSM_HEAD_DIM).reshape(1, SSM_WIDTH), ssd_norm.reshape(1, SSM_WIDTH),
      _head_expand().astype(BF16))
    st = st.reshape(batch, N_SSM_GROUPS, D_STATE, HEADS_PER_GROUP, SSM_HEAD_DIM)
    return y, jnp.transpose(st, (0, 1, 3, 4, 2)).reshape(batch, N_SSM_HEADS, SSM_HEAD_DIM, D_STATE)


def _router(xn_bf, wr_ref, br_ref):
    return jnp.dot(xn_bf, wr_ref[...], preferred_element_type=F32) + br_ref[...]


def _outproj_kernel(o0, l0, o1, l1, o2, l2, u1_ref, u2_ref, e_ref, ssm_ref, x_ref, w_ref, g_ref, wr_ref,
                    br_ref, x1_ref, xn_ref, lg_ref):
    def natural(o_ref, l_ref, u_ref):
        dil = o_ref.shape[1]
        if dil == 1:
            return o_ref[0, 0].astype(F32), l_ref[0, 0]
        o_cat = jnp.concatenate([o_ref[0, r] for r in range(dil)], axis=0)
        l_cat = jnp.concatenate([l_ref[0, r] for r in range(dil)], axis=0)
        hi = l_cat.astype(BF16)
        lo = (l_cat - hi.astype(F32)).astype(BF16)
        u = u_ref[...]
        return (jnp.dot(u, o_cat, preferred_element_type=F32),
                jnp.dot(u, hi, preferred_element_type=F32) + jnp.dot(u, lo, preferred_element_type=F32))

    branches = [natural(o0, l0, None), natural(o1, l1, u1_ref), natural(o2, l2, u2_ref)]
    lses = [_spread(l, e_ref[...], 2) for _, l in branches]
    mx = functools.reduce(jnp.maximum, lses)
    ws = [jnp.exp(l - mx) for l in lses]
    num = functools.reduce(jnp.add, [w * o for w, (o, _) in zip(ws, branches)])
    att = num * (1.0 / functools.reduce(jnp.add, ws))
    y = jnp.dot(att.astype(BF16), w_ref[:ATT_WIDTH, :], preferred_element_type=F32)
    y = y + jnp.dot(ssm_ref[...].astype(BF16), w_ref[ATT_WIDTH:, :], preferred_element_type=F32)
    x1 = x_ref[...] + y
    x1_ref[...] = x1
    xn = _rms(x1, g_ref[...]).astype(BF16)
    xn_ref[...] = xn
    lg_ref[...] = _router(xn, wr_ref, br_ref)


def _outproj_prompt(att_parts, ssm, x2d, w_bf, g, wr, br):
    m, d = x2d.shape
    mix = w_bf.shape[0]
    tm = TM_OUT
    batch = att_parts[0][0].shape[0]
    per_b = m // batch // tm
    assert tm % (max(DILATIONS) * 2 * SUBLANES) == 0
    assert (N_ATT_HEADS, HEAD_DIM) == (N_SSM_HEADS, SSM_HEAD_DIM)
    row = lambda w: pl.BlockSpec((tm, w), lambda i: (i, 0))
    split = lambda dil, w: pl.BlockSpec((1, dil, tm // dil, w), lambda i: (i // per_b, 0, i % per_b, 0))
    branch_specs = [split(o.shape[1], w) for o, _ in att_parts for w in (ATT_WIDTH, LANES)]
    interleave = [jnp.asarray(_deinterleave_matrix(tm, dil).T, BF16) for dil in DILATIONS[1:]]
    return pl.pallas_call(
        _outproj_kernel,
        grid=(m // tm,),
        in_specs=branch_specs + [_const_spec((tm, tm)), _const_spec((tm, tm)),
                                 _const_spec((LANES, ATT_WIDTH)), row(SSM_WIDTH), row(d),
                                 _const_spec((mix, d)), _const_spec((1, d)),
                                 _const_spec((d, LANES)), _const_spec((1, LANES))],
        out_specs=[row(d), row(d), row(LANES)],
        out_shape=[jax.ShapeDtypeStruct((m, d), F32), jax.ShapeDtypeStruct((m, d), BF16),
                   jax.ShapeDtypeStruct((m, LANES), F32)],
        compiler_params=_cparams(("parallel",)),
        name="outproj_prompt",
    )(*[a for part in att_parts for a in part], *interleave, _head_expand().astype(BF16), ssm, x2d, w_bf,
      g.reshape(1, d), wr, br)


def _attn_sample_kernel(q_ref, kn_ref, vn_ref, kc_ref, vc_ref, bias_ref, bias0_ref, o_ref):
    w = kc_ref.shape[1]
    q = q_ref[0]
    head_grp = lax.broadcasted_iota(jnp.int32, (N_ATT_HEADS, 1), 0) // KV_REP
    kn = kn_ref[0].astype(BF16).astype(F32)
    vn = vn_ref[0].astype(BF16).astype(F32)
    s_self = jnp.sum(q.astype(F32) * kn, axis=-1, keepdims=True)

    def taps(c_ref, dil):
        span = WIN * dil
        rows = c_ref[0, w - span:w, :].astype(BF16)
        if dil == 1:
            return rows
        tap = lax.broadcasted_iota(jnp.int32, (WIN, span), 0)
        col = lax.broadcasted_iota(jnp.int32, (WIN, span), 1)
        pick = jnp.where(col == tap * dil, 1.0, 0.0).astype(BF16)
        return jnp.dot(pick, rows, preferred_element_type=F32).astype(BF16)

    scores, selfs, lses = [], [], []
    for g, dil in enumerate(DILATIONS):
        kk = taps(kc_ref, dil)
        s = jnp.zeros((N_ATT_HEADS, WIN), F32)
        for kvh in range(N_KV_HEADS):
            cs = slice(kvh * HEAD_DIM, (kvh + 1) * HEAD_DIM)
            sk = lax.dot_general(q, kk[:, cs], (((1,), (1,)), ((), ())), preferred_element_type=F32)
            s = jnp.where(head_grp == kvh, sk, s)
        s = s + bias_ref[g]
        s0 = s_self + bias0_ref[g]
        mx = jnp.maximum(jnp.max(s, axis=-1, keepdims=True), s0)
        lse = mx + jnp.log(jnp.sum(jnp.exp(s - mx), axis=-1, keepdims=True) + jnp.exp(s0 - mx))
        scores.append(s)
        selfs.append(s0)
        lses.append(lse)
    top = functools.reduce(jnp.maximum, lses)
    es = [jnp.exp(l - top) for l in lses]
    tot = functools.reduce(jnp.add, es)
    o = jnp.zeros((N_ATT_HEADS, HEAD_DIM), F32)
    for s, s0, lse, e, dil in zip(scores, selfs, lses, es, DILATIONS):
        wgt = e / tot
        p = (jnp.exp(s - lse) * wgt).astype(BF16)
        p0 = (jnp.exp(s0 - lse) * wgt).astype(BF16).astype(F32)
        vv = taps(vc_ref, dil)
        o = o + p0 * vn
        for kvh in range(N_KV_HEADS):
            cs = slice(kvh * HEAD_DIM, (kvh + 1) * HEAD_DIM)
            ok = jnp.dot(p, vv[:, cs], preferred_element_type=F32)
            o = o + jnp.where(head_grp == kvh, ok, 0.0)
    o_ref[0] = o


def _attn_sample(q, k_new, v_new, k_cache, v_cache, bias_s, bias0_s):
    n, w = k_cache.shape[0], k_cache.shape[1]
    assert w % (max(DILATIONS) * WIN) == 0
    tok = lambda b: (b, 0, 0)
    head = pl.BlockSpec((1, N_ATT_HEADS, HEAD_DIM), tok)
    window = pl.BlockSpec((1, w, KV_WIDTH), tok)
    return pl.pallas_call(
        _attn_sample_kernel,
        grid=(n,),
        in_specs=[head, head, head, window, window,
                  _const_spec((len(DILATIONS), N_ATT_HEADS, WIN)),
                  _const_spec((len(DILATIONS), N_ATT_HEADS, 1))],
        out_specs=head,
        out_shape=jax.ShapeDtypeStruct((n, N_ATT_HEADS, HEAD_DIM), F32),
        compiler_params=_cparams(("parallel",)),
        name="attn_sample",
    )(q, jnp.repeat(k_new, KV_REP, axis=1), jnp.repeat(v_new, KV_REP, axis=1), k_cache, v_cache,
      bias_s, bias0_s)


def _conv_sample_kernel(xbc_ref, b0_ref, b1_ref, b2_ref, cw_ref, cb_ref, dt_ref, dtb_ref, alog_ref,
                        e_ref, xa_ref, xdt_ref, decay_ref):
    acc = cb_ref[...] + xbc_ref[...] * cw_ref[CONV_W - 1:CONV_W, :]
    for i, buf in enumerate((b0_ref, b1_ref, b2_ref)):
        acc = acc + buf[...] * cw_ref[i:i + 1, :]
    xa = _silu(acc)
    xa_ref[...] = xa
    dt = _softplus(dt_ref[...] + dtb_ref[...])
    decay_ref[...] = jnp.exp(dt * (-jnp.exp(alog_ref[...])))
    dt_full = jnp.dot(dt, e_ref[...], preferred_element_type=F32, precision=HIGHEST)
    xdt_ref[...] = xa[:, :SSM_WIDTH] * dt_full


def _conv_sample(xbc, conv_buf, conv_w, conv_b, dt_raw, dt_bias, a_log):
    n = xbc.shape[0]
    args = (xbc, conv_buf[:, 0], conv_buf[:, 1], conv_buf[:, 2], conv_w, conv_b.reshape(1, CONV_DIM),
            dt_raw, _pad_lanes(dt_bias), _pad_lanes(a_log),
            _head_expand())
    return pl.pallas_call(
        _conv_sample_kernel,
        grid=(1,),
        in_specs=[_const_spec(a.shape) for a in args],
        out_specs=[_const_spec((n, CONV_DIM)), _const_spec((n, SSM_WIDTH)), _const_spec((n, LANES))],
        out_shape=[jax.ShapeDtypeStruct((n, CONV_DIM), F32), jax.ShapeDtypeStruct((n, SSM_WIDTH), F32),
                   jax.ShapeDtypeStruct((n, LANES), F32)],
        compiler_params=_cparams(("arbitrary",)),
        name="conv_sample",
    )(*args)


def _ssm_sample_kernel(xdt_ref, decay_ref, b_ref, c_ref, h0_ref, hn_ref, y_ref):
    for g in range(N_SSM_GROUPS):
        hs = slice(g * HEADS_PER_GROUP, (g + 1) * HEADS_PER_GROUP)
        hn = decay_ref[0, hs] * h0_ref[0, hs] + xdt_ref[0, hs] * b_ref[0, g]
        hn_ref[0, hs] = hn
        c_row = c_ref[0, g].astype(BF16).astype(F32)
        y_ref[0, hs] = jnp.sum(hn.astype(BF16).astype(F32) * c_row, axis=-1, keepdims=True)


def _ssm_sample(xdt, decay, bmat, cmat, h0):
    n = xdt.shape[0]
    p = SSM_HEAD_DIM
    tok4 = lambda b: (b, 0, 0, 0)
    hn, y = pl.pallas_call(
        _ssm_sample_kernel,
        grid=(n,),
        in_specs=[pl.BlockSpec((1, N_SSM_HEADS, p, 1), tok4),
                  pl.BlockSpec((1, N_SSM_HEADS, 1, 1), tok4),
                  pl.BlockSpec((1, N_SSM_GROUPS, 1, D_STATE), tok4),
                  pl.BlockSpec((1, N_SSM_GROUPS, 1, D_STATE), tok4),
                  pl.BlockSpec((1, N_SSM_HEADS, p, D_STATE), tok4)],
        out_specs=[pl.BlockSpec((1, N_SSM_HEADS, p, D_STATE), tok4),
                   pl.BlockSpec((1, N_SSM_HEADS, p, 1), tok4)],
        out_shape=[jax.ShapeDtypeStruct((n, N_SSM_HEADS, p, D_STATE), F32),
                   jax.ShapeDtypeStruct((n, N_SSM_HEADS, p, 1), F32)],
        compiler_params=_cparams(("parallel",)),
        name="ssm_sample",
    )(xdt.reshape(n, N_SSM_HEADS, p, 1), decay[:, :N_SSM_HEADS].reshape(n, N_SSM_HEADS, 1, 1),
      bmat.reshape(n, N_SSM_GROUPS, 1, D_STATE), cmat.reshape(n, N_SSM_GROUPS, 1, D_STATE), h0)
    return hn, y.reshape(n, SSM_WIDTH)


def _outproj_sample_kernel(att_ref, y_ref, xs_ref, z_ref, dskip_ref, gn_ref, x_ref, w_ref, g_ref,
                           wr_ref, br_ref, x1_ref, xn_ref, lg_ref):
    y = y_ref[...] + dskip_ref[...] * xs_ref[...]
    u = y * _silu(z_ref[...])
    gw = SSM_WIDTH // N_SSM_GROUPS
    parts = []
    for g in range(N_SSM_GROUPS):
        ug = u[:, g * gw:(g + 1) * gw]
        parts.append(ug * lax.rsqrt(jnp.mean(ug * ug, axis=-1, keepdims=True) + EPS)
                     * gn_ref[:, g * gw:(g + 1) * gw])
    mix = jnp.concatenate([att_ref[...]] + parts, axis=-1).astype(BF16)
    x1 = x_ref[...] + jnp.dot(mix, w_ref[...], preferred_element_type=F32)
    x1_ref[...] = x1
    xn = _rms(x1, g_ref[...]).astype(BF16)
    xn_ref[...] = xn
    lg_ref[...] = _router(xn, wr_ref, br_ref)


def _outproj_sample(att, y, xs, z, d_skip, ssd_norm, x2d, w_bf, g, wr, br):
    n, d = x2d.shape
    args = (att, y, xs, z, jnp.repeat(d_skip, SSM_HEAD_DIM).reshape(1, SSM_WIDTH),
            ssd_norm.reshape(1, SSM_WIDTH), x2d, w_bf, g.reshape(1, d), wr, br)
    return pl.pallas_call(
        _outproj_sample_kernel,
        grid=(1,),
        in_specs=[_const_spec(a.shape) for a in args],
        out_specs=[_const_spec((n, d)), _const_spec((n, d)), _const_spec((n, LANES))],
        out_shape=[jax.ShapeDtypeStruct((n, d), F32), jax.ShapeDtypeStruct((n, d), BF16),
                   jax.ShapeDtypeStruct((n, LANES), F32)],
        compiler_params=_cparams(("arbitrary",)),
        name="outproj_sample",
    )(*args)


TT = 256
CHUNK_ALIGN = SUBLANES
LROWS = -(-(TT * TOP_K + N_EXPERTS * (CHUNK_ALIGN - 1)) // TT) * TT
CHUNK_SIZES = tuple(1 << b for b in range(TT.bit_length() - 1, CHUNK_ALIGN.bit_length() - 2, -1))
SUB = 256
SUB_TAIL = SUB // 2
RG = 5 * SUB
TF = 256
VMEM_LIMIT_FFN = 60000 * 1024


def _chunk_loop(cnt_ref, loff_ref, dest_ref, tile, fn):
    def per_expert(e, carry):
        idx = tile * N_EXPERTS + e
        n, off, dst = cnt_ref[idx], loff_ref[idx], dest_ref[idx]
        for size in CHUNK_SIZES:
            take = (n & size) != 0

            @pl.when(take)
            def _(off=off, dst=dst, size=size):
                fn(pl.multiple_of(off, CHUNK_ALIGN), pl.multiple_of(dst, CHUNK_ALIGN), size)

            step = jnp.where(take, size, 0)
            off, dst = off + step, dst + step
        return carry

    lax.fori_loop(0, N_EXPERTS, per_expert, 0)


def _dispatch_kernel(cnt_ref, loff_ref, dest_ref, xp_ref, xs_ref, lpos_ref, gate_ref, out_hbm, buf, sem):
    t = pl.program_id(0)
    last = pl.num_programs(0) - 1
    d = xp_ref.shape[1]
    x = jnp.where(t == last, xs_ref[...], xp_ref[...])
    rows = lax.broadcasted_iota(jnp.int32, (LROWS, TT), 0)
    onehot = jnp.zeros((LROWS, TT), F32)
    wcol = jnp.zeros((LROWS, 1), F32)
    for k in range(TOP_K):
        hit = jnp.where(rows == lpos_ref[0, k:k + 1, :], 1.0, 0.0)
        onehot = onehot + hit
        wcol = wcol + jnp.sum(hit * gate_ref[0, k:k + 1, :], axis=-1, keepdims=True)
    tile_buf = buf.at[t % 2]
    tile_buf[:, 0:d] = jnp.dot(onehot.astype(BF16), x, preferred_element_type=F32)
    tile_buf[:, d:d + LANES] = jnp.broadcast_to(wcol, (LROWS, LANES))

    def copy(tile, off, dst, size):
        return pltpu.make_async_copy(buf.at[tile % 2, pl.ds(off, size)], out_hbm.at[pl.ds(dst, size)],
                                     sem.at[tile % 2])

    _chunk_loop(cnt_ref, loff_ref, dest_ref, t, lambda o, g, s: copy(t, o, g, s).start())

    @pl.when(t > 0)
    def _():
        _chunk_loop(cnt_ref, loff_ref, dest_ref, t - 1, lambda o, g, s: copy(t - 1, o, g, s).wait())

    @pl.when(t == last)
    def _():
        _chunk_loop(cnt_ref, loff_ref, dest_ref, t, lambda o, g, s: copy(t, o, g, s).wait())


def _dispatch(tabs, xn_p, xn_s, lpos_t, gate_t, n_rows):
    nt = lpos_t.shape[0]
    d = xn_p.shape[1]
    last_p = xn_p.shape[0] // TT - 1
    grid_spec = pltpu.PrefetchScalarGridSpec(
        num_scalar_prefetch=3,
        grid=(nt,),
        in_specs=[pl.BlockSpec((TT, d), lambda t, *_: (jnp.minimum(t, last_p), 0)),
                  pl.BlockSpec((TT, d), lambda t, *_: (0, 0)),
                  pl.BlockSpec((1, TOP_K, TT), lambda t, *_: (t, 0, 0)),
                  pl.BlockSpec((1, TOP_K, TT), lambda t, *_: (t, 0, 0))],
        out_specs=pl.BlockSpec(memory_space=pl.ANY),
        scratch_shapes=[pltpu.VMEM((2, LROWS, d + LANES), F32), pltpu.SemaphoreType.DMA((2,))],
    )
    return pl.pallas_call(
        _dispatch_kernel,
        grid_spec=grid_spec,
        out_shape=jax.ShapeDtypeStruct((n_rows, d + LANES), F32),
        compiler_params=_cparams(("arbitrary",)),
        name="moe_dispatch",
    )(*tabs, xn_p, xn_s, lpos_t, gate_t)


def _ffn_kernel(ge_ref, gs_ref, gn_ref, gt_ref, ng_ref, xs_hbm, wgu_hbm, wd_hbm, bgu_ref, bdn_ref, out_hbm,
                xbuf, acc, ostage, wg_st, wu_st, wd_st, wg_bf, wu_bf, wd_bf, sem_w, sem_x, sem_o):
    d = acc.shape[1]
    d_ff = wd_hbm.shape[1]
    nf = d_ff // TF
    n_groups = ng_ref[0]
    total = n_groups * nf

    def w_copies(s, slot):
        g = s // nf
        f = s - g * nf
        e = ge_ref[g]
        c0 = pl.multiple_of(f * TF, TF)
        return (pltpu.make_async_copy(wgu_hbm.at[e, :, pl.ds(c0, TF)], wg_st.at[slot], sem_w.at[slot, 0]),
                pltpu.make_async_copy(wgu_hbm.at[e, :, pl.ds(d_ff + c0, TF)], wu_st.at[slot], sem_w.at[slot, 1]),
                pltpu.make_async_copy(wd_hbm.at[e, pl.ds(c0, TF), :], wd_st.at[slot], sem_w.at[slot, 2]))

    def x_copy(g, j, size=SUB):
        r0 = pl.multiple_of(j * SUB, SUB)
        return pltpu.make_async_copy(xs_hbm.at[pl.ds(pl.multiple_of(gs_ref[g] + r0, CHUNK_ALIGN), size)],
                                     xbuf.at[g % 2, pl.ds(r0, size)], sem_x)

    def o_copy(g, j, size=SUB):
        r0 = pl.multiple_of(j * SUB, SUB)
        return pltpu.make_async_copy(ostage.at[j % 2, pl.ds(0, size)],
                                     out_hbm.at[pl.ds(pl.multiple_of(gs_ref[g] + r0, CHUNK_ALIGN), size)],
                                     sem_o.at[j % 2])

    def loop(n, fn):
        lax.fori_loop(0, n, lambda j, c: (fn(j), c)[1], 0)

    def group_rows(g, op):
        loop(gn_ref[g], lambda j: op(x_copy(g, j)))

        @pl.when(gt_ref[g] == 1)
        def _():
            op(x_copy(g, gn_ref[g], SUB_TAIL))

    def drain_stores(g):
        n = gn_ref[g]
        tail = gt_ref[g]

        @pl.when(tail == 1)
        def _():
            o_copy(g, n, SUB_TAIL).wait()

        @pl.when(n >= 1)
        def _():
            o_copy(g, n - 1).wait()

        @pl.when(jnp.logical_and(n >= 2, tail == 0))
        def _():
            o_copy(g, n - 2).wait()

    @pl.when(total > 0)
    def _():
        for c in w_copies(0, 0):
            c.start()
        group_rows(0, lambda c: c.start())

    def item(s, carry):
        slot = s % 2
        g = s // nf
        f = s - g * nf
        e = ge_ref[g]
        nsub = gn_ref[g]
        xg = xbuf.at[g % 2]

        @pl.when(s + 1 < total)
        def _():
            for c in w_copies(s + 1, 1 - slot):
                c.start()

        @pl.when(f == 0)
        def _():
            group_rows(g, lambda c: c.wait())

        @pl.when(jnp.logical_and(f == 1, g + 1 < n_groups))
        def _():
            group_rows(g + 1, lambda c: c.start())

        for c in w_copies(s, slot):
            c.wait()
        bg = bgu_ref[pl.ds(e * 2 * nf + f, 1), :]
        bu = bgu_ref[pl.ds(e * 2 * nf + nf + f, 1), :]

        def sub_block(j, phase, cast=False, size=SUB):
            rs = pl.ds(pl.multiple_of(j * SUB, SUB), size)
            x = xg[rs, 0:d].astype(BF16)
            if cast:
                wg, wu, wd = (st[slot].astype(BF16) for st in (wg_st, wu_st, wd_st))
                wg_bf[...], wu_bf[...], wd_bf[...] = wg, wu, wd
            else:
                wg, wu, wd = wg_bf[...], wu_bf[...], wd_bf[...]
            hg = jnp.dot(x, wg, preferred_element_type=F32) + bg
            hu = jnp.dot(x, wu, preferred_element_type=F32) + bu
            gg = jnp.minimum(hg, SWIGLU_LIMIT)
            uu = jnp.clip(hu, -SWIGLU_LIMIT, SWIGLU_LIMIT)
            act = gg * (1.0 / (1.0 + jnp.exp(-SWIGLU_ALPHA * gg))) * (uu + 1.0)
            part = jnp.dot(act.astype(BF16), wd, preferred_element_type=F32)
            if phase == "first":
                acc[rs, :] = part + bdn_ref[pl.ds(e, 1), :]
            elif phase == "middle":
                acc[rs, :] += part
            else:
                @pl.when(j >= 2)
                def _():
                    o_copy(g, j - 2).wait()

                ostage[j % 2, 0:size] = (acc[rs, :] + part) * xg[rs, d:d + 1]
                o_copy(g, j, size).start()

        tail = gt_ref[g] == 1

        def storing_blocks():
            sub_block(jnp.int32(0), "last", cast=True)

            def pair(p):
                sub_block(2 * p + 1, "last")
                sub_block(2 * p + 2, "last")

            loop((nsub - 1) // 2, pair)

            @pl.when(nsub % 2 == 0)
            def _():
                sub_block(nsub - 1, "last")

            @pl.when(tail)
            def _():
                sub_block(nsub, "last", size=SUB_TAIL)

        def accumulating_blocks(phase):
            @pl.when(nsub < 2)
            def _():
                sub_block(jnp.int32(0), phase, cast=True)

                @pl.when(tail)
                def _():
                    sub_block(jnp.int32(1), phase, size=SUB_TAIL)

            @pl.when(nsub >= 2)
            def _():
                sub_block(jnp.int32(0), phase, cast=True, size=2 * SUB)
                rest = 2 * (nsub - 2) + gt_ref[g]
                for units in range(1, (RG - 2 * SUB) // SUB_TAIL + 1):
                    @pl.when(rest == units)
                    def _(units=units):
                        sub_block(jnp.int32(2), phase, size=units * SUB_TAIL)

        @pl.when(f == 0)
        def _():
            accumulating_blocks("first")

        @pl.when(jnp.logical_and(f > 0, f < nf - 1))
        def _():
            accumulating_blocks("middle")

        @pl.when(f == nf - 1)
        def _():
            @pl.when(g > 0)
            def _():
                drain_stores(g - 1)

            storing_blocks()

        return carry

    lax.fori_loop(0, total, item, 0)

    @pl.when(total > 0)
    def _():
        drain_stores(n_groups - 1)


def _moe_ffn(groups, x_sorted, w_gate_up, b_gu, w_down, b_dn):
    n_rows = x_sorted.shape[0]
    _, d_ff, d = w_down.shape
    nf = d_ff // TF
    assert nf >= 2
    bgu2 = b_gu.reshape(N_EXPERTS * 2 * nf, TF)
    grid_spec = pltpu.PrefetchScalarGridSpec(
        num_scalar_prefetch=5,
        grid=(1,),
        in_specs=[pl.BlockSpec(memory_space=pl.ANY), pl.BlockSpec(memory_space=pl.ANY),
                  pl.BlockSpec(memory_space=pl.ANY),
                  pl.BlockSpec(bgu2.shape, lambda i, *_: (0, 0), pipeline_mode=pl.Buffered(1)),
                  pl.BlockSpec(b_dn.shape, lambda i, *_: (0, 0), pipeline_mode=pl.Buffered(1))],
        out_specs=pl.BlockSpec(memory_space=pl.ANY),
        scratch_shapes=[pltpu.VMEM((2, RG, d + LANES), F32), pltpu.VMEM((RG, d), F32),
                        pltpu.VMEM((2, SUB, d), F32),
                        pltpu.VMEM((2, d, TF), F32), pltpu.VMEM((2, d, TF), F32), pltpu.VMEM((2, TF, d), F32),
                        pltpu.VMEM((d, TF), BF16), pltpu.VMEM((d, TF), BF16), pltpu.VMEM((TF, d), BF16),
                        pltpu.SemaphoreType.DMA((2, 3)), pltpu.SemaphoreType.DMA(()),
                        pltpu.SemaphoreType.DMA((2,))],
    )
    return pl.pallas_call(
        _ffn_kernel,
        grid_spec=grid_spec,
        out_shape=jax.ShapeDtypeStruct((n_rows, d), F32),
        compiler_params=pltpu.CompilerParams(dimension_semantics=("arbitrary",),
                                             vmem_limit_bytes=VMEM_LIMIT_FFN),
        name="moe_ffn",
    )(*groups, x_sorted, w_gate_up, w_down, bgu2, b_dn)


def _route(logits, m_pad):
    m = logits.shape[0]
    nt = m_pad // TT
    top_v, top_i = lax.top_k(logits, TOP_K)
    gate = jnp.pad(jax.nn.softmax(top_v, axis=-1), ((0, m_pad - m), (0, 0)))
    top_i = jnp.pad(top_i.astype(jnp.int32), ((0, m_pad - m), (0, 0)), constant_values=-1)
    chosen = (top_i[:, :, None] == jnp.arange(N_EXPERTS, dtype=jnp.int32)).astype(jnp.int32)
    tiles = chosen.sum(axis=1).reshape(nt, TT, N_EXPERTS)
    cnt = (tiles.sum(axis=1) + CHUNK_ALIGN - 1) // CHUNK_ALIGN * CHUNK_ALIGN
    loff = jnp.cumsum(cnt, axis=1) - cnt
    seg = cnt.sum(axis=0)
    seg_start = jnp.cumsum(seg) - seg
    dest = seg_start[None, :] + jnp.cumsum(cnt, axis=0) - cnt
    before = jnp.asarray(np.tril(np.ones((TT, TT), np.float32), -1), BF16)
    rank = jnp.einsum('ij,tje->tie', before, tiles.astype(BF16),
                      preferred_element_type=F32).astype(jnp.int32)
    lpos_all = (loff[:, None, :] + rank).reshape(m_pad, N_EXPERTS)
    lpos = (lpos_all[:, None, :] * chosen).sum(axis=-1)
    lpos = jnp.where(top_i >= 0, lpos, -1).astype(jnp.int32)
    n_rows_bound = nt * LROWS
    n_grp_max = n_rows_bound // RG + N_EXPERTS
    grp = (seg + RG - 1) // RG
    grp_end = jnp.cumsum(grp)
    gi = jnp.arange(n_grp_max, dtype=jnp.int32)
    g_exp = jnp.minimum((gi[:, None] >= grp_end[None, :]).sum(axis=1), N_EXPERTS - 1).astype(jnp.int32)
    within = gi - (grp_end[g_exp] - grp[g_exp])
    g_start = (seg_start[g_exp] + within * RG).astype(jnp.int32)
    g_rows = jnp.clip(seg[g_exp] - within * RG, 0, RG)
    g_rows = jnp.where(gi < grp_end[-1], g_rows, 0)
    n_full = g_rows // SUB
    rest = g_rows - n_full * SUB
    g_tail = (rest > 0) & (rest <= SUB_TAIL) & (n_full >= 1)
    g_nsub = n_full + ((rest > 0) & ~g_tail)
    groups = (g_exp, g_start, g_nsub.astype(jnp.int32), g_tail.astype(jnp.int32),
              grp_end[-1:].astype(jnp.int32))
    tabs = tuple(a.reshape(-1).astype(jnp.int32) for a in (cnt, loff, dest))
    return tabs, lpos, gate, groups, n_rows_bound + SUB


def _combine_kernel(cnt_ref, loff_ref, dest_ref, src_hbm, lpos_ref, x1p_ref, x1s_ref, g_ref,
                    yp_ref, ys_ref, buf, sem):
    t = pl.program_id(0)
    last = pl.num_programs(0) - 1

    def copy(tile, off, src, size):
        return pltpu.make_async_copy(src_hbm.at[pl.ds(src, size)], buf.at[tile % 2, pl.ds(off, size)],
                                     sem.at[tile % 2])

    @pl.when(t == 0)
    def _():
        buf[...] = jnp.zeros_like(buf)
        _chunk_loop(cnt_ref, loff_ref, dest_ref, t, lambda o, g, s: copy(t, o, g, s).start())

    @pl.when(t < last)
    def _():
        _chunk_loop(cnt_ref, loff_ref, dest_ref, t + 1, lambda o, g, s: copy(t + 1, o, g, s).start())

    _chunk_loop(cnt_ref, loff_ref, dest_ref, t, lambda o, g, s: copy(t, o, g, s).wait())
    cols = lax.broadcasted_iota(jnp.int32, (TT, LROWS), 1)
    sel = jnp.zeros((TT, LROWS), F32)
    for k in range(TOP_K):
        sel = sel + jnp.where(cols == lpos_ref[:, k:k + 1], 1.0, 0.0)
    f = jnp.dot(sel.astype(BF16), buf[t % 2].astype(BF16), preferred_element_type=F32)

    @pl.when(t < last)
    def _():
        yp_ref[...] = _rms(x1p_ref[...] + f, g_ref[...])

    @pl.when(t == last)
    def _():
        ys_ref[...] = _rms(x1s_ref[...] + f, g_ref[...])


def _combine(tabs, out_sorted, lpos, x1p, x1s, g):
    d = x1p.shape[1]
    nt = lpos.shape[0] // TT
    last_p = x1p.shape[0] // TT - 1
    prompt = pl.BlockSpec((TT, d), lambda t, *_: (jnp.minimum(t, last_p), 0))
    sample = pl.BlockSpec((TT, d), lambda t, *_: (0, 0))
    grid_spec = pltpu.PrefetchScalarGridSpec(
        num_scalar_prefetch=3,
        grid=(nt,),
        in_specs=[pl.BlockSpec(memory_space=pl.ANY),
                  pl.BlockSpec((TT, TOP_K), lambda t, *_: (t, 0)),
                  prompt, sample,
                  pl.BlockSpec((1, d), lambda t, *_: (0, 0))],
        out_specs=[prompt, sample],
        scratch_shapes=[pltpu.VMEM((2, LROWS, d), F32), pltpu.SemaphoreType.DMA((2,))],
    )
    return pl.pallas_call(
        _combine_kernel,
        grid_spec=grid_spec,
        out_shape=[jax.ShapeDtypeStruct(x1p.shape, F32), jax.ShapeDtypeStruct((TT, d), F32)],
        compiler_params=_cparams(("arbitrary",)),
        name="moe_combine",
    )(*tabs, out_sorted, lpos, x1p, x1s, g.reshape(1, d))


def _t5_bucket(dist):
    max_exact = N_BUCKETS // 2
    dd = dist.astype(F32)
    large = max_exact + (jnp.log(jnp.maximum(dd, 1.0) / max_exact)
                         / math.log(BUCKET_MAX_DIST / max_exact) * (N_BUCKETS - max_exact)).astype(jnp.int32)
    large = jnp.minimum(large, N_BUCKETS - 1)
    return jnp.where(dist < max_exact, dist, large)


def _bias_tables(rel_bias):
    dist = jnp.asarray(np.arange(N_TAPS)[None, :] * np.array(DILATIONS)[:, None], jnp.int32)
    bias = jnp.transpose(rel_bias[_t5_bucket(dist)], (2, 0, 1)).astype(F32)
    by_branch = jnp.transpose(bias, (1, 0, 2))
    g, h = by_branch.shape[:2]
    row = jnp.concatenate([by_branch[:, :, ::-1], jnp.full((g, h, WIN), NEG, F32)], axis=-1)
    flat = jnp.broadcast_to(row[:, :, None, :], (g, h, WIN, 2 * WIN + 1)).reshape(g, h, -1)
    band = flat[:, :, :WIN * 2 * WIN].reshape(g, h, WIN, 2 * WIN)
    return band, by_branch[:, :, :0:-1], by_branch[:, :, 0:1]


def kernel(x_prompt, x_sample, cache_k_win, cache_v_win, state_conv, state_ssm, rel_bias, attn_norm, w_in, conv_w, conv_b, dt_bias, a_log, d_skip, ssd_norm, w_out, ffn_norm, w_router, b_router, w_gate_up, b_gate_up, w_down, b_down, final_norm):
    bp, tp, d = x_prompt.shape
    bs, ts, _ = x_sample.shape
    depth = w_in.shape[0]
    assert depth == 1 and ts == 1 and tp % (max(DILATIONS) * WIN) == 0
    keep = min(max(DILATIONS) * WIN, tp)
    band, samp, samp0 = _bias_tables(rel_bias)
    l = 0

    xp = x_prompt.reshape(bp * tp, d)
    xs = x_sample.reshape(bs * ts, d)
    w_in_bf = jnp.pad(w_in[l], ((0, 0), (0, IN_PROJ_PAD - IN_PROJ))).astype(BF16)
    w_out_bf = w_out[l].astype(BF16)
    wr = jnp.pad(w_router[l], ((0, 0), (0, LANES - N_EXPERTS))).astype(BF16)
    br = jnp.pad(b_router[l], (0, LANES - N_EXPERTS), constant_values=NEG).reshape(1, LANES)

    k, v, z, xbc, dt_raw, *branch_qkv = _inproj_prompt(xp, attn_norm[l], w_in_bf, bp)
    k3 = k.reshape(bp, tp, KV_WIDTH)
    v3 = v.reshape(bp, tp, KV_WIDTH)
    att_parts = [_attn_branch(branch_qkv[2 * gi], branch_qkv[2 * gi + 1], band[gi])
                 for gi in range(len(DILATIONS))]
    ssm, st_p = _ssd_prompt(xbc, dt_raw, z, conv_w[l], conv_b[l], dt_bias[l], a_log[l], d_skip[l],
                            ssd_norm[l], bp)
    x1p, xnp_, lgp = _outproj_prompt(att_parts, ssm, xp, w_out_bf, ffn_norm[l], wr, br)
    k_win_p = k3[:, tp - keep:].reshape(1, bp, keep, N_KV_HEADS, HEAD_DIM)
    v_win_p = v3[:, tp - keep:].reshape(1, bp, keep, N_KV_HEADS, HEAD_DIM)
    conv_p = xbc.reshape(bp, tp, CONV_DIM)[:, tp - (CONV_W - 1):][None]

    q_s, k_s, v_s, z_s, xbc_s, dt_s = _inproj(xs, attn_norm[l], w_in_bf, bs * ts)
    q_s = q_s.reshape(bs, N_ATT_HEADS, HEAD_DIM)
    k_s = k_s.reshape(bs, N_KV_HEADS, HEAD_DIM)
    v_s = v_s.reshape(bs, N_KV_HEADS, HEAD_DIM)
    wbuf = cache_k_win.shape[2]
    att_s = _attn_sample(q_s, k_s, v_s, cache_k_win[l].reshape(bs, wbuf, KV_WIDTH),
                         cache_v_win[l].reshape(bs, wbuf, KV_WIDTH), samp, samp0)
    xa_s, xdt_s, decay_s = _conv_sample(xbc_s, state_conv[l], conv_w[l], conv_b[l], dt_s, dt_bias[l],
                                        a_log[l])
    nbc = N_SSM_GROUPS * D_STATE
    h_s, y_s = _ssm_sample(xdt_s, decay_s, xa_s[:, SSM_WIDTH:SSM_WIDTH + nbc], xa_s[:, SSM_WIDTH + nbc:],
                           state_ssm[l])
    x1s, xns, lgs = _outproj_sample(att_s.reshape(bs, ATT_WIDTH), y_s, xa_s[:, :SSM_WIDTH], z_s,
                                    d_skip[l], ssd_norm[l], xs, w_out_bf, ffn_norm[l], wr, br)
    conv_s = jnp.concatenate([state_conv[l][:, 1:], xbc_s[:, None]], axis=1)[None]

    n_s = bs * ts
    assert (bp * tp) % TT == 0 and n_s <= TT
    m_pad = bp * tp + TT
    logits = jnp.concatenate([lgp[:, :N_EXPERTS], lgs[:, :N_EXPERTS]], axis=0)
    tabs, lpos, gate, groups, n_rows = _route(logits, m_pad)
    by_tile = lambda a: jnp.transpose(a.reshape(m_pad // TT, TT, TOP_K), (0, 2, 1))
    pad_rows = lambda a: jnp.pad(a, ((0, TT - n_s), (0, 0)))
    x_sorted = _dispatch(tabs, xnp_, pad_rows(xns), by_tile(lpos), by_tile(gate), n_rows)
    out_sorted = _moe_ffn(groups, x_sorted, w_gate_up[l], b_gate_up[l], w_down[l], b_down[l])
    y_p, y_s_out = _combine(tabs, out_sorted, lpos, x1p, pad_rows(x1s), final_norm)
    y_s_out = y_s_out[:n_s]

    return (y_p.reshape(bp, tp, d), y_s_out.reshape(bs, ts, d), k_win_p, v_win_p, conv_p, st_p[None],
            k_s.reshape(1, bs, ts, N_KV_HEADS, HEAD_DIM), v_s.reshape(1, bs, ts, N_KV_HEADS, HEAD_DIM),
            conv_s, h_s[None])
```

```python
import functools
import math

import jax
import jax.numpy as jnp
import numpy as np
from jax import lax
from jax.experimental import pallas as pl
from jax.experimental.pallas import tpu as pltpu

F32 = jnp.float32
BF16 = jnp.bfloat16
HIGHEST = lax.Precision.HIGHEST

LANES = 128
SUBLANES = 8
VMEM_LIMIT = 56 * 1024 * 1024

HEAD_DIM = 64
N_ATT_HEADS = 16
N_KV_HEADS = 4
KV_REP = N_ATT_HEADS // N_KV_HEADS
ATT_WIDTH = N_ATT_HEADS * HEAD_DIM
KV_WIDTH = N_KV_HEADS * HEAD_DIM
DILATIONS = (1, 4, 16)
N_TAPS = 129
WIN = N_TAPS - 1
ATT_SCALE = HEAD_DIM ** -0.5
N_BUCKETS = 32
BUCKET_MAX_DIST = 2048
SSM_HEAD_DIM = 64
N_SSM_HEADS = 16
SSM_WIDTH = N_SSM_HEADS * SSM_HEAD_DIM
N_SSM_GROUPS = 2
HEADS_PER_GROUP = N_SSM_HEADS // N_SSM_GROUPS
D_STATE = 128
CONV_W = 4
CONV_DIM = SSM_WIDTH + 2 * N_SSM_GROUPS * D_STATE
SSD_CHUNK = 128
N_EXPERTS = 32
TOP_K = 4
SWIGLU_LIMIT = 7.0
SWIGLU_ALPHA = 1.702
EPS = 1e-5
NEG = -1e30

Q0, K0, V0, Z0, X0, DT0 = 0, 1024, 1280, 1536, 2560, 4096
IN_PROJ = DT0 + N_SSM_HEADS
IN_PROJ_PAD = DT0 + LANES

TM_PROJ = 512
TM_OUT = 256
TQ_ATT = WIN


def _cparams(sem):
    return pltpu.CompilerParams(dimension_semantics=sem, vmem_limit_bytes=VMEM_LIMIT)


def _const_spec(shape):
    nd = len(shape)
    return pl.BlockSpec(shape, lambda *_: (0,) * nd, pipeline_mode=pl.Buffered(1))


def _spread(a, onehot_bf, terms):
    out, rest = None, a
    for _ in range(terms):
        piece = rest.astype(BF16)
        part = jnp.dot(piece, onehot_bf, preferred_element_type=F32)
        out = part if out is None else out + part
        rest = rest - piece.astype(F32)
    return out


def _rms(x, g):
    ms = jnp.mean(x * x, axis=-1, keepdims=True)
    return x * lax.rsqrt(ms + EPS) * g


def _silu(x):
    return x * (1.0 / (1.0 + jnp.exp(-x)))


def _inproj_kernel(x_ref, g_ref, w_ref, q_ref, k_ref, v_ref, z_ref, xbc_ref, dt_ref):
    xn = _rms(x_ref[...], g_ref[...]).astype(BF16)

    def mm(lo, hi):
        return jnp.dot(xn, w_ref[:, lo:hi], preferred_element_type=F32)

    q_ref[...] = (mm(Q0, K0) * ATT_SCALE).astype(BF16)
    k_ref[...] = mm(K0, V0)
    v_ref[...] = mm(V0, Z0)
    z_ref[...] = mm(Z0, X0)
    xbc_ref[...] = mm(X0, DT0)
    dt_ref[...] = mm(DT0, IN_PROJ_PAD)


def _inproj(x2d, g, w_bf, tm):
    m, d = x2d.shape
    widths = (ATT_WIDTH, KV_WIDTH, KV_WIDTH, SSM_WIDTH, CONV_DIM, LANES)
    dtypes = (BF16, F32, F32, F32, F32, F32)
    return pl.pallas_call(
        _inproj_kernel,
        grid=(m // tm,),
        in_specs=[pl.BlockSpec((tm, d), lambda i: (i, 0)),
                  _const_spec((1, d)),
                  _const_spec((d, IN_PROJ_PAD))],
        out_specs=[pl.BlockSpec((tm, w), lambda i: (i, 0)) for w in widths],
        out_shape=[jax.ShapeDtypeStruct((m, w), t) for w, t in zip(widths, dtypes)],
        compiler_params=_cparams(("parallel",)),
        name=f"inproj_{tm}",
    )(x2d, g.reshape(1, d), w_bf)


def _deinterleave_matrix(n, dil):
    p = np.zeros((n, n), np.float32)
    src = np.arange(n)
    p[(src % dil) * (n // dil) + src // dil, src] = 1.0
    return p


def _inproj_prompt_kernel(x_ref, g_ref, w_ref, p_mid_ref, p_far_ref, k_ref, v_ref, z_ref, xbc_ref, dt_ref,
                          q0_ref, kv0_ref, q1_ref, kv1_ref, q2_ref, kv2_ref):
    tm = x_ref.shape[0]
    xn = _rms(x_ref[...], g_ref[...]).astype(BF16)

    def mm(lo, hi):
        return jnp.dot(xn, w_ref[:, lo:hi], preferred_element_type=F32)

    q = (mm(Q0, K0) * ATT_SCALE).astype(BF16)
    k = mm(K0, V0)
    v = mm(V0, Z0)
    k_ref[...] = k
    v_ref[...] = v
    z_ref[...] = mm(Z0, X0)
    xbc_ref[...] = mm(X0, DT0)
    dt_ref[...] = mm(DT0, IN_PROJ_PAD)
    kv = jnp.concatenate([k, v], axis=-1).astype(BF16)
    q0_ref[0, 0] = q
    kv0_ref[0, 0] = kv
    for dil, p_ref, qd_ref, kvd_ref in ((DILATIONS[1], p_mid_ref, q1_ref, kv1_ref),
                                        (DILATIONS[2], p_far_ref, q2_ref, kv2_ref)):
        qp = jnp.dot(p_ref[...], q, preferred_element_type=F32).astype(BF16)
        kvp = jnp.dot(p_ref[...], kv, preferred_element_type=F32).astype(BF16)
        rows = tm // dil
        for r in range(dil):
            qd_ref[0, r] = qp[r * rows:(r + 1) * rows]
            kvd_ref[0, r] = kvp[r * rows:(r + 1) * rows]


def _inproj_prompt(x2d, g, w_bf, batch):
    m, d = x2d.shape
    tm = TM_PROJ
    seq = m // batch
    per_b = seq // tm
    assert DILATIONS[0] == 1 and seq % tm == 0 and tm % (max(DILATIONS) * 2 * SUBLANES) == 0
    widths = (KV_WIDTH, KV_WIDTH, SSM_WIDTH, CONV_DIM, LANES)
    rows = lambda w: pl.BlockSpec((tm, w), lambda i: (i, 0))
    split = lambda dil, w: pl.BlockSpec((1, dil, tm // dil, w), lambda i: (i // per_b, 0, i % per_b, 0))
    branch_specs, branch_shapes = [], []
    for dil in DILATIONS:
        for w in (ATT_WIDTH, 2 * KV_WIDTH):
            branch_specs.append(split(dil, w))
            branch_shapes.append(jax.ShapeDtypeStruct((batch, dil, seq // dil, w), BF16))
    perms = [jnp.asarray(_deinterleave_matrix(tm, dil), BF16) for dil in DILATIONS[1:]]
    return pl.pallas_call(
        _inproj_prompt_kernel,
        grid=(m // tm,),
        in_specs=[rows(d), _const_spec((1, d)), _const_spec((d, IN_PROJ_PAD)),
                  _const_spec((tm, tm)), _const_spec((tm, tm))],
        out_specs=[rows(w) for w in widths] + branch_specs,
        out_shape=[jax.ShapeDtypeStruct((m, w), F32) for w in widths] + branch_shapes,
        compiler_params=_cparams(("parallel",)),
        name="inproj_prompt",
    )(x2d, g.reshape(1, d), w_bf, *perms)


def _attn_kernel(q_ref, kvp_ref, kvc_ref, bias_ref, o_ref, lse_ref):
    first = pl.program_id(2) == 0
    lane = lax.broadcasted_iota(jnp.int32, (1, 2 * WIN), 1)
    prev_mask = jnp.where(jnp.logical_and(first, lane < WIN), NEG, 0.0)
    head_lane = lax.broadcasted_iota(jnp.int32, (WIN, LANES), 1)
    for blk in range(TQ_ATT // WIN):
        rows = slice(blk * WIN, (blk + 1) * WIN)
        lse_tile = jnp.zeros((WIN, LANES), F32)
        for kvh in range(N_KV_HEADS):
            ks = slice(kvh * HEAD_DIM, (kvh + 1) * HEAD_DIM)
            vs = slice(KV_WIDTH + kvh * HEAD_DIM, KV_WIDTH + (kvh + 1) * HEAD_DIM)
            if blk == 0:
                kw = jnp.concatenate([kvp_ref[0, 0, :, ks], kvc_ref[0, 0, 0:WIN, ks]], axis=0)
                vw = jnp.concatenate([kvp_ref[0, 0, :, vs], kvc_ref[0, 0, 0:WIN, vs]], axis=0)
            else:
                kw = kvc_ref[0, 0, (blk - 1) * WIN:(blk + 1) * WIN, ks]
                vw = kvc_ref[0, 0, (blk - 1) * WIN:(blk + 1) * WIN, vs]
            for pair in range(KV_REP // 2):
                h0 = kvh * KV_REP + pair * 2
                q2 = jnp.concatenate([q_ref[0, 0, rows, (h0 + r) * HEAD_DIM:(h0 + r + 1) * HEAD_DIM]
                                      for r in range(2)], axis=0)
                s = lax.dot_general(q2, kw, (((1,), (1,)), ((), ())), preferred_element_type=F32)
                s = s + bias_ref[h0:h0 + 2].reshape(2 * WIN, 2 * WIN)
                if blk == 0:
                    s = s + prev_mask
                mx = jnp.max(s, axis=-1, keepdims=True)
                p = jnp.exp(s - mx)
                l = jnp.sum(p, axis=-1, keepdims=True)
                o = jnp.dot(p.astype(BF16), vw, preferred_element_type=F32) * (1.0 / l)
                lse = mx + jnp.log(l)
                for r in range(2):
                    lse_tile = jnp.where(head_lane == h0 + r, lse[r * WIN:(r + 1) * WIN], lse_tile)
                o_ref[0, 0, rows, h0 * HEAD_DIM:(h0 + 2) * HEAD_DIM] = (
                    jnp.concatenate([o[0:WIN], o[WIN:2 * WIN]], axis=-1).astype(BF16))
        lse_ref[0, 0, rows, :] = lse_tile


def _attn_branch(q, kv, bias_mat):
    b, dil, sub, _ = q.shape
    assert sub % TQ_ATT == 0
    nb = sub // TQ_ATT
    per_step = TQ_ATT // WIN
    cur = lambda bb, r, i: (bb, r, i, 0)
    prev = lambda bb, r, i: (bb, r, jnp.maximum(i * per_step - 1, 0), 0)
    return pl.pallas_call(
        _attn_kernel,
        grid=(b, dil, nb),
        in_specs=[pl.BlockSpec((1, 1, TQ_ATT, ATT_WIDTH), cur),
                  pl.BlockSpec((1, 1, WIN, 2 * KV_WIDTH), prev),
                  pl.BlockSpec((1, 1, TQ_ATT, 2 * KV_WIDTH), cur),
                  _const_spec((N_ATT_HEADS, WIN, 2 * WIN))],
        out_specs=[pl.BlockSpec((1, 1, TQ_ATT, ATT_WIDTH), cur),
                   pl.BlockSpec((1, 1, TQ_ATT, LANES), cur)],
        out_shape=[jax.ShapeDtypeStruct((b, dil, sub, ATT_WIDTH), BF16),
                   jax.ShapeDtypeStruct((b, dil, sub, LANES), F32)],
        compiler_params=_cparams(("parallel", "parallel", "arbitrary")),
        name=f"attn_dil{dil}",
    )(q, kv, kv, bias_mat)


def _softplus(x):
    return jnp.maximum(x, 0.0) + jnp.log(1.0 + jnp.exp(-jnp.abs(x)))


def _ssd_kernel(xbc_ref, dt_ref, z_ref, cw_ref, cb_ref, dtb_ref, alog_ref, dskip_ref, gn_ref, e_ref,
                y_ref, st_ref, ext_ref, state_ref):
    c = pl.program_id(1)
    L = SSD_CHUNK

    @pl.when(c == 0)
    def _():
        ext_ref[0:SUBLANES, :] = jnp.zeros((SUBLANES, CONV_DIM), F32)
        state_ref[...] = jnp.zeros_like(state_ref)

    ext_ref[SUBLANES:SUBLANES + L, :] = xbc_ref[...]
    acc = cb_ref[...] + ext_ref[SUBLANES:SUBLANES + L, :] * cw_ref[CONV_W - 1:CONV_W, :]
    for i in range(CONV_W - 1):
        off = SUBLANES - (CONV_W - 1) + i
        acc = acc + ext_ref[off:off + L, :] * cw_ref[i:i + 1, :]
    ext_ref[0:SUBLANES, :] = ext_ref[L:L + SUBLANES, :]
    xa = _silu(acc)

    dt = _softplus(dt_ref[...] + dtb_ref[...])
    da = dt * (-jnp.exp(alog_ref[...]))
    row = lax.broadcasted_iota(jnp.int32, (L, L), 0)
    col = lax.broadcasted_iota(jnp.int32, (L, L), 1)
    tri = row >= col
    a_cs = jnp.dot(tri.astype(F32), da, preferred_element_type=F32, precision=HIGHEST)
    a_cs_t = a_cs.T
    expand = e_ref[...]
    acs_full = _spread(a_cs, expand, 3)
    dt_full = _spread(dt, expand, 3)
    exp_acs = jnp.exp(acs_full)
    a_last = acs_full[L - 1:L, :]
    exp_last = exp_acs[L - 1:L, :]
    xs = xa[:, :SSM_WIDTH]
    xdt = xs * dt_full
    xw = xdt * jnp.exp(a_last - acs_full)

    for g in range(N_SSM_GROUPS):
        b0 = SSM_WIDTH + g * D_STATE
        c0 = SSM_WIDTH + N_SSM_GROUPS * D_STATE + g * D_STATE
        bg_t = xa[:, b0:b0 + D_STATE].T.astype(BF16)
        cg = xa[:, c0:c0 + D_STATE].astype(BF16)
        gram = jnp.dot(cg, bg_t, preferred_element_type=F32)
        gs = slice(g * HEADS_PER_GROUP * SSM_HEAD_DIM, (g + 1) * HEADS_PER_GROUP * SSM_HEAD_DIM)
        st = state_ref[g]
        y_off = jnp.dot(cg, st.astype(BF16), preferred_element_type=F32) * exp_acs[:, gs]
        state_ref[g] = exp_last[:, gs] * st + jnp.dot(bg_t, xw[:, gs].astype(BF16),
                                                      preferred_element_type=F32)
        y_diag = []
        for hh in range(HEADS_PER_GROUP):
            h = g * HEADS_PER_GROUP + hh
            hs = slice(h * SSM_HEAD_DIM, (h + 1) * SSM_HEAD_DIM)
            seg = jnp.where(tri, a_cs[:, h:h + 1] - a_cs_t[h:h + 1, :], NEG)
            scores = (gram * jnp.exp(seg)).astype(BF16)
            y_diag.append(jnp.dot(scores, xdt[:, hs].astype(BF16), preferred_element_type=F32))
        y_ref[:, gs] = jnp.concatenate(y_diag, axis=-1) + y_off

    y = y_ref[...] + dskip_ref[...] * xs
    u = y * _silu(z_ref[...])
    gw = SSM_WIDTH // N_SSM_GROUPS
    parts = []
    for g in range(N_SSM_GROUPS):
        ug = u[:, g * gw:(g + 1) * gw]
        parts.append(ug * lax.rsqrt(jnp.mean(ug * ug, axis=-1, keepdims=True) + EPS))
    y_ref[...] = jnp.concatenate(parts, axis=-1) * gn_ref[...]

    @pl.when(c == pl.num_programs(1) - 1)
    def _():
        st_ref[0] = state_ref[...]


def _head_expand():
    e = np.zeros((LANES, SSM_WIDTH), np.float32)
    for h in range(N_SSM_HEADS):
        e[h, h * SSM_HEAD_DIM:(h + 1) * SSM_HEAD_DIM] = 1.0
    return jnp.asarray(e)


def _pad_lanes(v):
    return jnp.pad(v.astype(F32), (0, LANES - v.shape[0])).reshape(1, LANES)


def _ssd_prompt(xbc, dt_raw, z, conv_w, conv_b, dt_bias, a_log, d_skip, ssd_norm, batch):
    m = xbc.shape[0]
    nc = m // batch // SSD_CHUNK
    L = SSD_CHUNK
    group_w = HEADS_PER_GROUP * SSM_HEAD_DIM
    rows = lambda b, c: (b * nc + c, 0)
    y, st = pl.pallas_call(
        _ssd_kernel,
        grid=(batch, nc),
        in_specs=[pl.BlockSpec((L, CONV_DIM), rows),
                  pl.BlockSpec((L, LANES), rows),
                  pl.BlockSpec((L, SSM_WIDTH), rows),
                  _const_spec((CONV_W, CONV_DIM)),
                  _const_spec((1, CONV_DIM)),
                  _const_spec((1, LANES)),
                  _const_spec((1, LANES)),
                  _const_spec((1, SSM_WIDTH)),
                  _const_spec((1, SSM_WIDTH)),
                  _const_spec((LANES, SSM_WIDTH))],
        out_specs=[pl.BlockSpec((L, SSM_WIDTH), rows),
                   pl.BlockSpec((1, N_SSM_GROUPS, D_STATE, group_w), lambda b, c: (b, 0, 0, 0))],
        out_shape=[jax.ShapeDtypeStruct((m, SSM_WIDTH), F32),
                   jax.ShapeDtypeStruct((batch, N_SSM_GROUPS, D_STATE, group_w), F32)],
        scratch_shapes=[pltpu.VMEM((SUBLANES + L, CONV_DIM), F32),
                        pltpu.VMEM((N_SSM_GROUPS, D_STATE, group_w), F32)],
        compiler_params=_cparams(("parallel", "arbitrary")),
        name="ssd_prompt",
    )(xbc, dt_raw, z, conv_w, conv_b.reshape(1, CONV_DIM), _pad_lanes(dt_bias), _pad_lanes(a_log),
      jnp.repeat(d_skip, SSM_HEAD_DIM).reshape(1, SSM_WIDTH), ssd_norm.reshape(1, SSM_WIDTH),
      _head_expand().astype(BF16))
    st = st.reshape(batch, N_SSM_GROUPS, D_STATE, HEADS_PER_GROUP, SSM_HEAD_DIM)
    return y, jnp.transpose(st, (0, 1, 3, 4, 2)).reshape(batch, N_SSM_HEADS, SSM_HEAD_DIM, D_STATE)


def _router(xn_bf, wr_ref, br_ref):
    return jnp.dot(xn_bf, wr_ref[...], preferred_element_type=F32) + br_ref[...]


def _outproj_kernel(o0, l0, o1, l1, o2, l2, u1_ref, u2_ref, e_ref, ssm_ref, x_ref, w_ref, g_ref, wr_ref,
                    br_ref, x1_ref, xn_ref, lg_ref):
    def natural(o_ref, l_ref, u_ref):
        dil = o_ref.shape[1]
        if dil == 1:
            return o_ref[0, 0].astype(F32), l_ref[0, 0]
        o_cat = jnp.concatenate([o_ref[0, r] for r in range(dil)], axis=0)
        l_cat = jnp.concatenate([l_ref[0, r] for r in range(dil)], axis=0)
        hi = l_cat.astype(BF16)
        lo = (l_cat - hi.astype(F32)).astype(BF16)
        u = u_ref[...]
        return (jnp.dot(u, o_cat, preferred_element_type=F32),
                jnp.dot(u, hi, preferred_element_type=F32) + jnp.dot(u, lo, preferred_element_type=F32))

    branches = [natural(o0, l0, None), natural(o1, l1, u1_ref), natural(o2, l2, u2_ref)]
    lses = [_spread(l, e_ref[...], 2) for _, l in branches]
    mx = functools.reduce(jnp.maximum, lses)
    ws = [jnp.exp(l - mx) for l in lses]
    num = functools.reduce(jnp.add, [w * o for w, (o, _) in zip(ws, branches)])
    att = num * (1.0 / functools.reduce(jnp.add, ws))
    y = jnp.dot(att.astype(BF16), w_ref[:ATT_WIDTH, :], preferred_element_type=F32)
    y = y + jnp.dot(ssm_ref[...].astype(BF16), w_ref[ATT_WIDTH:, :], preferred_element_type=F32)
    x1 = x_ref[...] + y
    x1_ref[...] = x1
    xn = _rms(x1, g_ref[...]).astype(BF16)
    xn_ref[...] = xn
    lg_ref[...] = _router(xn, wr_ref, br_ref)


def _outproj_prompt(att_parts, ssm, x2d, w_bf, g, wr, br):
    m, d = x2d.shape
    mix = w_bf.shape[0]
    tm = TM_OUT
    batch = att_parts[0][0].shape[0]
    per_b = m // batch // tm
    assert tm % (max(DILATIONS) * 2 * SUBLANES) == 0
    assert (N_ATT_HEADS, HEAD_DIM) == (N_SSM_HEADS, SSM_HEAD_DIM)
    row = lambda w: pl.BlockSpec((tm, w), lambda i: (i, 0))
    split = lambda dil, w: pl.BlockSpec((1, dil, tm // dil, w), lambda i: (i // per_b, 0, i % per_b, 0))
    branch_specs = [split(o.shape[1], w) for o, _ in att_parts for w in (ATT_WIDTH, LANES)]
    interleave = [jnp.asarray(_deinterleave_matrix(tm, dil).T, BF16) for dil in DILATIONS[1:]]
    return pl.pallas_call(
        _outproj_kernel,
        grid=(m // tm,),
        in_specs=branch_specs + [_const_spec((tm, tm)), _const_spec((tm, tm)),
                                 _const_spec((LANES, ATT_WIDTH)), row(SSM_WIDTH), row(d),
                                 _const_spec((mix, d)), _const_spec((1, d)),
                                 _const_spec((d, LANES)), _const_spec((1, LANES))],
        out_specs=[row(d), row(d), row(LANES)],
        out_shape=[jax.ShapeDtypeStruct((m, d), F32), jax.ShapeDtypeStruct((m, d), BF16),
                   jax.ShapeDtypeStruct((m, LANES), F32)],
        compiler_params=_cparams(("parallel",)),
        name="outproj_prompt",
    )(*[a for part in att_parts for a in part], *interleave, _head_expand().astype(BF16), ssm, x2d, w_bf,
      g.reshape(1, d), wr, br)


def _attn_sample_kernel(q_ref, kn_ref, vn_ref, kc_ref, vc_ref, bias_ref, bias0_ref, o_ref):
    w = kc_ref.shape[1]
    q = q_ref[0]
    head_grp = lax.broadcasted_iota(jnp.int32, (N_ATT_HEADS, 1), 0) // KV_REP
    kn = kn_ref[0].astype(BF16).astype(F32)
    vn = vn_ref[0].astype(BF16).astype(F32)
    s_self = jnp.sum(q.astype(F32) * kn, axis=-1, keepdims=True)

    def taps(c_ref, dil):
        span = WIN * dil
        rows = c_ref[0, w - span:w, :].astype(BF16)
        if dil == 1:
            return rows
        tap = lax.broadcasted_iota(jnp.int32, (WIN, span), 0)
        col = lax.broadcasted_iota(jnp.int32, (WIN, span), 1)
        pick = jnp.where(col == tap * dil, 1.0, 0.0).astype(BF16)
        return jnp.dot(pick, rows, preferred_element_type=F32).astype(BF16)

    scores, selfs, lses = [], [], []
    for g, dil in enumerate(DILATIONS):
        kk = taps(kc_ref, dil)
        s = jnp.zeros((N_ATT_HEADS, WIN), F32)
        for kvh in range(N_KV_HEADS):
            cs = slice(kvh * HEAD_DIM, (kvh + 1) * HEAD_DIM)
            sk = lax.dot_general(q, kk[:, cs], (((1,), (1,)), ((), ())), preferred_element_type=F32)
            s = jnp.where(head_grp == kvh, sk, s)
        s = s + bias_ref[g]
        s0 = s_self + bias0_ref[g]
        mx = jnp.maximum(jnp.max(s, axis=-1, keepdims=True), s0)
        lse = mx + jnp.log(jnp.sum(jnp.exp(s - mx), axis=-1, keepdims=True) + jnp.exp(s0 - mx))
        scores.append(s)
        selfs.append(s0)
        lses.append(lse)
    top = functools.reduce(jnp.maximum, lses)
    es = [jnp.exp(l - top) for l in lses]
    tot = functools.reduce(jnp.add, es)
    o = jnp.zeros((N_ATT_HEADS, HEAD_DIM), F32)
    for s, s0, lse, e, dil in zip(scores, selfs, lses, es, DILATIONS):
        wgt = e / tot
        p = (jnp.exp(s - lse) * wgt).astype(BF16)
        p0 = (jnp.exp(s0 - lse) * wgt).astype(BF16).astype(F32)
        vv = taps(vc_ref, dil)
        o = o + p0 * vn
        for kvh in range(N_KV_HEADS):
            cs = slice(kvh * HEAD_DIM, (kvh + 1) * HEAD_DIM)
            ok = jnp.dot(p, vv[:, cs], preferred_element_type=F32)
            o = o + jnp.where(head_grp == kvh, ok, 0.0)
    o_ref[0] = o


def _attn_sample(q, k_new, v_new, k_cache, v_cache, bias_s, bias0_s):
    n, w = k_cache.shape[0], k_cache.shape[1]
    assert w % (max(DILATIONS) * WIN) == 0
    tok = lambda b: (b, 0, 0)
    head = pl.BlockSpec((1, N_ATT_HEADS, HEAD_DIM), tok)
    window = pl.BlockSpec((1, w, KV_WIDTH), tok)
    return pl.pallas_call(
        _attn_sample_kernel,
        grid=(n,),
        in_specs=[head, head, head, window, window,
                  _const_spec((len(DILATIONS), N_ATT_HEADS, WIN)),
                  _const_spec((len(DILATIONS), N_ATT_HEADS, 1))],
        out_specs=head,
        out_shape=jax.ShapeDtypeStruct((n, N_ATT_HEADS, HEAD_DIM), F32),
        compiler_params=_cparams(("parallel",)),
        name="attn_sample",
    )(q, jnp.repeat(k_new, KV_REP, axis=1), jnp.repeat(v_new, KV_REP, axis=1), k_cache, v_cache,
      bias_s, bias0_s)


def _conv_sample_kernel(xbc_ref, b0_ref, b1_ref, b2_ref, cw_ref, cb_ref, dt_ref, dtb_ref, alog_ref,
                        e_ref, xa_ref, xdt_ref, decay_ref):
    acc = cb_ref[...] + xbc_ref[...] * cw_ref[CONV_W - 1:CONV_W, :]
    for i, buf in enumerate((b0_ref, b1_ref, b2_ref)):
        acc = acc + buf[...] * cw_ref[i:i + 1, :]
    xa = _silu(acc)
    xa_ref[...] = xa
    dt = _softplus(dt_ref[...] + dtb_ref[...])
    decay_ref[...] = jnp.exp(dt * (-jnp.exp(alog_ref[...])))
    dt_full = jnp.dot(dt, e_ref[...], preferred_element_type=F32, precision=HIGHEST)
    xdt_ref[...] = xa[:, :SSM_WIDTH] * dt_full


def _conv_sample(xbc, conv_buf, conv_w, conv_b, dt_raw, dt_bias, a_log):
    n = xbc.shape[0]
    args = (xbc, conv_buf[:, 0], conv_buf[:, 1], conv_buf[:, 2], conv_w, conv_b.reshape(1, CONV_DIM),
            dt_raw, _pad_lanes(dt_bias), _pad_lanes(a_log),
            _head_expand())
    return pl.pallas_call(
        _conv_sample_kernel,
        grid=(1,),
        in_specs=[_const_spec(a.shape) for a in args],
        out_specs=[_const_spec((n, CONV_DIM)), _const_spec((n, SSM_WIDTH)), _const_spec((n, LANES))],
        out_shape=[jax.ShapeDtypeStruct((n, CONV_DIM), F32), jax.ShapeDtypeStruct((n, SSM_WIDTH), F32),
                   jax.ShapeDtypeStruct((n, LANES), F32)],
        compiler_params=_cparams(("arbitrary",)),
        name="conv_sample",
    )(*args)


def _ssm_sample_kernel(xdt_ref, decay_ref, b_ref, c_ref, h0_ref, hn_ref, y_ref):
    for g in range(N_SSM_GROUPS):
        hs = slice(g * HEADS_PER_GROUP, (g + 1) * HEADS_PER_GROUP)
        hn = decay_ref[0, hs] * h0_ref[0, hs] + xdt_ref[0, hs] * b_ref[0, g]
        hn_ref[0, hs] = hn
        c_row = c_ref[0, g].astype(BF16).astype(F32)
        y_ref[0, hs] = jnp.sum(hn.astype(BF16).astype(F32) * c_row, axis=-1, keepdims=True)


def _ssm_sample(xdt, decay, bmat, cmat, h0):
    n = xdt.shape[0]
    p = SSM_HEAD_DIM
    tok4 = lambda b: (b, 0, 0, 0)
    hn, y = pl.pallas_call(
        _ssm_sample_kernel,
        grid=(n,),
        in_specs=[pl.BlockSpec((1, N_SSM_HEADS, p, 1), tok4),
                  pl.BlockSpec((1, N_SSM_HEADS, 1, 1), tok4),
                  pl.BlockSpec((1, N_SSM_GROUPS, 1, D_STATE), tok4),
                  pl.BlockSpec((1, N_SSM_GROUPS, 1, D_STATE), tok4),
                  pl.BlockSpec((1, N_SSM_HEADS, p, D_STATE), tok4)],
        out_specs=[pl.BlockSpec((1, N_SSM_HEADS, p, D_STATE), tok4),
                   pl.BlockSpec((1, N_SSM_HEADS, p, 1), tok4)],
        out_shape=[jax.ShapeDtypeStruct((n, N_SSM_HEADS, p, D_STATE), F32),
                   jax.ShapeDtypeStruct((n, N_SSM_HEADS, p, 1), F32)],
        compiler_params=_cparams(("parallel",)),
        name="ssm_sample",
    )(xdt.reshape(n, N_SSM_HEADS, p, 1), decay[:, :N_SSM_HEADS].reshape(n, N_SSM_HEADS, 1, 1),
      bmat.reshape(n, N_SSM_GROUPS, 1, D_STATE), cmat.reshape(n, N_SSM_GROUPS, 1, D_STATE), h0)
    return hn, y.reshape(n, SSM_WIDTH)


def _outproj_sample_kernel(att_ref, y_ref, xs_ref, z_ref, dskip_ref, gn_ref, x_ref, w_ref, g_ref,
                           wr_ref, br_ref, x1_ref, xn_ref, lg_ref):
    y = y_ref[...] + dskip_ref[...] * xs_ref[...]
    u = y * _silu(z_ref[...])
    gw = SSM_WIDTH // N_SSM_GROUPS
    parts = []
    for g in range(N_SSM_GROUPS):
        ug = u[:, g * gw:(g + 1) * gw]
        parts.append(ug * lax.rsqrt(jnp.mean(ug * ug, axis=-1, keepdims=True) + EPS)
                     * gn_ref[:, g * gw:(g + 1) * gw])
    mix = jnp.concatenate([att_ref[...]] + parts, axis=-1).astype(BF16)
    x1 = x_ref[...] + jnp.dot(mix, w_ref[...], preferred_element_type=F32)
    x1_ref[...] = x1
    xn = _rms(x1, g_ref[...]).astype(BF16)
    xn_ref[...] = xn
    lg_ref[...] = _router(xn, wr_ref, br_ref)


def _outproj_sample(att, y, xs, z, d_skip, ssd_norm, x2d, w_bf, g, wr, br):
    n, d = x2d.shape
    args = (att, y, xs, z, jnp.repeat(d_skip, SSM_HEAD_DIM).reshape(1, SSM_WIDTH),
            ssd_norm.reshape(1, SSM_WIDTH), x2d, w_bf, g.reshape(1, d), wr, br)
    return pl.pallas_call(
        _outproj_sample_kernel,
        grid=(1,),
        in_specs=[_const_spec(a.shape) for a in args],
        out_specs=[_const_spec((n, d)), _const_spec((n, d)), _const_spec((n, LANES))],
        out_shape=[jax.ShapeDtypeStruct((n, d), F32), jax.ShapeDtypeStruct((n, d), BF16),
                   jax.ShapeDtypeStruct((n, LANES), F32)],
        compiler_params=_cparams(("arbitrary",)),
        name="outproj_sample",
    )(*args)


TT = 256
CHUNK_ALIGN = SUBLANES
LROWS = -(-(TT * TOP_K + N_EXPERTS * (CHUNK_ALIGN - 1)) // TT) * TT
CHUNK_SIZES = tuple(1 << b for b in range(TT.bit_length() - 1, CHUNK_ALIGN.bit_length() - 2, -1))
SUB = 256
SUB_TAIL = SUB // 2
RG = 5 * SUB
TF = 256
VMEM_LIMIT_FFN = 60000 * 1024


def _chunk_loop(cnt_ref, loff_ref, dest_ref, tile, fn):
    def per_expert(e, carry):
        idx = tile * N_EXPERTS + e
        n, off, dst = cnt_ref[idx], loff_ref[idx], dest_ref[idx]
        for size in CHUNK_SIZES:
            take = (n & size) != 0

            @pl.when(take)
            def _(off=off, dst=dst, size=size):
                fn(pl.multiple_of(off, CHUNK_ALIGN), pl.multiple_of(dst, CHUNK_ALIGN), size)

            step = jnp.where(take, size, 0)
            off, dst = off + step, dst + step
        return carry

    lax.fori_loop(0, N_EXPERTS, per_expert, 0)


def _dispatch_kernel(cnt_ref, loff_ref, dest_ref, xp_ref, xs_ref, lpos_ref, gate_ref, out_hbm, buf, sem):
    t = pl.program_id(0)
    last = pl.num_programs(0) - 1
    d = xp_ref.shape[1]
    x = jnp.where(t == last, xs_ref[...], xp_ref[...])
    rows = lax.broadcasted_iota(jnp.int32, (LROWS, TT), 0)
    onehot = jnp.zeros((LROWS, TT), F32)
    wcol = jnp.zeros((LROWS, 1), F32)
    for k in range(TOP_K):
        hit = jnp.where(rows == lpos_ref[0, k:k + 1, :], 1.0, 0.0)
        onehot = onehot + hit
        wcol = wcol + jnp.sum(hit * gate_ref[0, k:k + 1, :], axis=-1, keepdims=True)
    tile_buf = buf.at[t % 2]
    tile_buf[:, 0:d] = jnp.dot(onehot.astype(BF16), x, preferred_element_type=F32)
    tile_buf[:, d:d + LANES] = jnp.broadcast_to(wcol, (LROWS, LANES))

    def copy(tile, off, dst, size):
        return pltpu.make_async_copy(buf.at[tile % 2, pl.ds(off, size)], out_hbm.at[pl.ds(dst, size)],
                                     sem.at[tile % 2])

    _chunk_loop(cnt_ref, loff_ref, dest_ref, t, lambda o, g, s: copy(t, o, g, s).start())

    @pl.when(t > 0)
    def _():
        _chunk_loop(cnt_ref, loff_ref, dest_ref, t - 1, lambda o, g, s: copy(t - 1, o, g, s).wait())

    @pl.when(t == last)
    def _():
        _chunk_loop(cnt_ref, loff_ref, dest_ref, t, lambda o, g, s: copy(t, o, g, s).wait())


def _dispatch(tabs, xn_p, xn_s, lpos_t, gate_t, n_rows):
    nt = lpos_t.shape[0]
    d = xn_p.shape[1]
    last_p = xn_p.shape[0] // TT - 1
    grid_spec = pltpu.PrefetchScalarGridSpec(
        num_scalar_prefetch=3,
        grid=(nt,),
        in_specs=[pl.BlockSpec((TT, d), lambda t, *_: (jnp.minimum(t, last_p), 0)),
                  pl.BlockSpec((TT, d), lambda t, *_: (0, 0)),
                  pl.BlockSpec((1, TOP_K, TT), lambda t, *_: (t, 0, 0)),
                  pl.BlockSpec((1, TOP_K, TT), lambda t, *_: (t, 0, 0))],
        out_specs=pl.BlockSpec(memory_space=pl.ANY),
        scratch_shapes=[pltpu.VMEM((2, LROWS, d + LANES), F32), pltpu.SemaphoreType.DMA((2,))],
    )
    return pl.pallas_call(
        _dispatch_kernel,
        grid_spec=grid_spec,
        out_shape=jax.ShapeDtypeStruct((n_rows, d + LANES), F32),
        compiler_params=_cparams(("arbitrary",)),
        name="moe_dispatch",
    )(*tabs, xn_p, xn_s, lpos_t, gate_t)


def _ffn_kernel(ge_ref, gs_ref, gn_ref, gt_ref, ng_ref, xs_hbm, wgu_hbm, wd_hbm, bgu_ref, bdn_ref, out_hbm,
                xbuf, acc, ostage, wg_st, wu_st, wd_st, wg_bf, wu_bf, wd_bf, sem_w, sem_x, sem_o):
    d = acc.shape[1]
    d_ff = wd_hbm.shape[1]
    nf = d_ff // TF
    n_groups = ng_ref[0]
    total = n_groups * nf

    def w_copies(s, slot):
        g = s // nf
        f = s - g * nf
        e = ge_ref[g]
        c0 = pl.multiple_of(f * TF, TF)
        return (pltpu.make_async_copy(wgu_hbm.at[e, :, pl.ds(c0, TF)], wg_st.at[slot], sem_w.at[slot, 0]),
                pltpu.make_async_copy(wgu_hbm.at[e, :, pl.ds(d_ff + c0, TF)], wu_st.at[slot], sem_w.at[slot, 1]),
                pltpu.make_async_copy(wd_hbm.at[e, pl.ds(c0, TF), :], wd_st.at[slot], sem_w.at[slot, 2]))

    def x_copy(g, j, size=SUB):
        r0 = pl.multiple_of(j * SUB, SUB)
        return pltpu.make_async_copy(xs_hbm.at[pl.ds(pl.multiple_of(gs_ref[g] + r0, CHUNK_ALIGN), size)],
                                     xbuf.at[g % 2, pl.ds(r0, size)], sem_x)

    def o_copy(g, j, size=SUB):
        r0 = pl.multiple_of(j * SUB, SUB)
        return pltpu.make_async_copy(ostage.at[j % 2, pl.ds(0, size)],
                                     out_hbm.at[pl.ds(pl.multiple_of(gs_ref[g] + r0, CHUNK_ALIGN), size)],
                                     sem_o.at[j % 2])

    def loop(n, fn):
        lax.fori_loop(0, n, lambda j, c: (fn(j), c)[1], 0)

    def group_rows(g, op):
        loop(gn_ref[g], lambda j: op(x_copy(g, j)))

        @pl.when(gt_ref[g] == 1)
        def _():
            op(x_copy(g, gn_ref[g], SUB_TAIL))

    def drain_stores(g):
        n = gn_ref[g]
        tail = gt_ref[g]

        @pl.when(tail == 1)
        def _():
            o_copy(g, n, SUB_TAIL).wait()

        @pl.when(n >= 1)
        def _():
            o_copy(g, n - 1).wait()

        @pl.when(jnp.logical_and(n >= 2, tail == 0))
        def _():
            o_copy(g, n - 2).wait()

    @pl.when(total > 0)
    def _():
        for c in w_copies(0, 0):
            c.start()
        group_rows(0, lambda c: c.start())

    def item(s, carry):
        slot = s % 2
        g = s // nf
        f = s - g * nf
        e = ge_ref[g]
        nsub = gn_ref[g]
        xg = xbuf.at[g % 2]

        @pl.when(s + 1 < total)
        def _():
            for c in w_copies(s + 1, 1 - slot):
                c.start()

        @pl.when(f == 0)
        def _():
            group_rows(g, lambda c: c.wait())

        @pl.when(jnp.logical_and(f == 1, g + 1 < n_groups))
        def _():
            group_rows(g + 1, lambda c: c.start())

        for c in w_copies(s, slot):
            c.wait()
        bg = bgu_ref[pl.ds(e * 2 * nf + f, 1), :]
        bu = bgu_ref[pl.ds(e * 2 * nf + nf + f, 1), :]

        def sub_block(j, phase, cast=False, size=SUB):
            rs = pl.ds(pl.multiple_of(j * SUB, SUB), size)
            x = xg[rs, 0:d].astype(BF16)
            if cast:
                wg, wu, wd = (st[slot].astype(BF16) for st in (wg_st, wu_st, wd_st))
                wg_bf[...], wu_bf[...], wd_bf[...] = wg, wu, wd
            else:
                wg, wu, wd = wg_bf[...], wu_bf[...], wd_bf[...]
            hg = jnp.dot(x, wg, preferred_element_type=F32) + bg
            hu = jnp.dot(x, wu, preferred_element_type=F32) + bu
            gg = jnp.minimum(hg, SWIGLU_LIMIT)
            uu = jnp.clip(hu, -SWIGLU_LIMIT, SWIGLU_LIMIT)
            act = gg * (1.0 / (1.0 + jnp.exp(-SWIGLU_ALPHA * gg))) * (uu + 1.0)
            part = jnp.dot(act.astype(BF16), wd, preferred_element_type=F32)
            if phase == "first":
                acc[rs, :] = part + bdn_ref[pl.ds(e, 1), :]
            elif phase == "middle":
                acc[rs, :] += part
            else:
                @pl.when(j >= 2)
                def _():
                    o_copy(g, j - 2).wait()

                ostage[j % 2, 0:size] = (acc[rs, :] + part) * xg[rs, d:d + 1]
                o_copy(g, j, size).start()

        tail = gt_ref[g] == 1

        def storing_blocks():
            sub_block(jnp.int32(0), "last", cast=True)

            def pair(p):
                sub_block(2 * p + 1, "last")
                sub_block(2 * p + 2, "last")

            loop((nsub - 1) // 2, pair)

            @pl.when(nsub % 2 == 0)
            def _():
                sub_block(nsub - 1, "last")

            @pl.when(tail)
            def _():
                sub_block(nsub, "last", size=SUB_TAIL)

        def accumulating_blocks(phase):
            @pl.when(nsub < 2)
            def _():
                sub_block(jnp.int32(0), phase, cast=True)

                @pl.when(tail)
                def _():
                    sub_block(jnp.int32(1), phase, size=SUB_TAIL)

            @pl.when(nsub >= 2)
            def _():
                sub_block(jnp.int32(0), phase, cast=True, size=2 * SUB)
                rest = 2 * (nsub - 2) + gt_ref[g]
                for units in range(1, (RG - 2 * SUB) // SUB_TAIL + 1):
                    @pl.when(rest == units)
                    def _(units=units):
                        sub_block(jnp.int32(2), phase, size=units * SUB_TAIL)

        @pl.when(f == 0)
        def _():
            accumulating_blocks("first")

        @pl.when(jnp.logical_and(f > 0, f < nf - 1))
        def _():
            accumulating_blocks("middle")

        @pl.when(f == nf - 1)
        def _():
            @pl.when(g > 0)
            def _():
                drain_stores(g - 1)

            storing_blocks()

        return carry

    lax.fori_loop(0, total, item, 0)

    @pl.when(total > 0)
    def _():
        drain_stores(n_groups - 1)


def _moe_ffn(groups, x_sorted, w_gate_up, b_gu, w_down, b_dn):
    n_rows = x_sorted.shape[0]
    _, d_ff, d = w_down.shape
    nf = d_ff // TF
    assert nf >= 2
    bgu2 = b_gu.reshape(N_EXPERTS * 2 * nf, TF)
    grid_spec = pltpu.PrefetchScalarGridSpec(
        num_scalar_prefetch=5,
        grid=(1,),
        in_specs=[pl.BlockSpec(memory_space=pl.ANY), pl.BlockSpec(memory_space=pl.ANY),
                  pl.BlockSpec(memory_space=pl.ANY),
                  pl.BlockSpec(bgu2.shape, lambda i, *_: (0, 0), pipeline_mode=pl.Buffered(1)),
                  pl.BlockSpec(b_dn.shape, lambda i, *_: (0, 0), pipeline_mode=pl.Buffered(1))],
        out_specs=pl.BlockSpec(memory_space=pl.ANY),
        scratch_shapes=[pltpu.VMEM((2, RG, d + LANES), F32), pltpu.VMEM((RG, d), F32),
                        pltpu.VMEM((2, SUB, d), F32),
                        pltpu.VMEM((2, d, TF), F32), pltpu.VMEM((2, d, TF), F32), pltpu.VMEM((2, TF, d), F32),
                        pltpu.VMEM((d, TF), BF16), pltpu.VMEM((d, TF), BF16), pltpu.VMEM((TF, d), BF16),
                        pltpu.SemaphoreType.DMA((2, 3)), pltpu.SemaphoreType.DMA(()),
                        pltpu.SemaphoreType.DMA((2,))],
    )
    return pl.pallas_call(
        _ffn_kernel,
        grid_spec=grid_spec,
        out_shape=jax.ShapeDtypeStruct((n_rows, d), F32),
        compiler_params=pltpu.CompilerParams(dimension_semantics=("arbitrary",),
                                             vmem_limit_bytes=VMEM_LIMIT_FFN),
        name="moe_ffn",
    )(*groups, x_sorted, w_gate_up, w_down, bgu2, b_dn)


def _route(logits, m_pad):
    m = logits.shape[0]
    nt = m_pad // TT
    top_v, top_i = lax.top_k(logits, TOP_K)
    gate = jnp.pad(jax.nn.softmax(top_v, axis=-1), ((0, m_pad - m), (0, 0)))
    top_i = jnp.pad(top_i.astype(jnp.int32), ((0, m_pad - m), (0, 0)), constant_values=-1)
    chosen = (top_i[:, :, None] == jnp.arange(N_EXPERTS, dtype=jnp.int32)).astype(jnp.int32)
    tiles = chosen.sum(axis=1).reshape(nt, TT, N_EXPERTS)
    cnt = (tiles.sum(axis=1) + CHUNK_ALIGN - 1) // CHUNK_ALIGN * CHUNK_ALIGN
    loff = jnp.cumsum(cnt, axis=1) - cnt
    seg = cnt.sum(axis=0)
    seg_start = jnp.cumsum(seg) - seg
    dest = seg_start[None, :] + jnp.cumsum(cnt, axis=0) - cnt
    before = jnp.asarray(np.tril(np.ones((TT, TT), np.float32), -1), BF16)
    rank = jnp.einsum('ij,tje->tie', before, tiles.astype(BF16),
                      preferred_element_type=F32).astype(jnp.int32)
    lpos_all = (loff[:, None, :] + rank).reshape(m_pad, N_EXPERTS)
    lpos = (lpos_all[:, None, :] * chosen).sum(axis=-1)
    lpos = jnp.where(top_i >= 0, lpos, -1).astype(jnp.int32)
    n_rows_bound = nt * LROWS
    n_grp_max = n_rows_bound // RG + N_EXPERTS
    grp = (seg + RG - 1) // RG
    grp_end = jnp.cumsum(grp)
    gi = jnp.arange(n_grp_max, dtype=jnp.int32)
    g_exp = jnp.minimum((gi[:, None] >= grp_end[None, :]).sum(axis=1), N_EXPERTS - 1).astype(jnp.int32)
    within = gi - (grp_end[g_exp] - grp[g_exp])
    g_start = (seg_start[g_exp] + within * RG).astype(jnp.int32)
    g_rows = jnp.clip(seg[g_exp] - within * RG, 0, RG)
    g_rows = jnp.where(gi < grp_end[-1], g_rows, 0)
    n_full = g_rows // SUB
    rest = g_rows - n_full * SUB
    g_tail = (rest > 0) & (rest <= SUB_TAIL) & (n_full >= 1)
    g_nsub = n_full + ((rest > 0) & ~g_tail)
    groups = (g_exp, g_start, g_nsub.astype(jnp.int32), g_tail.astype(jnp.int32),
              grp_end[-1:].astype(jnp.int32))
    tabs = tuple(a.reshape(-1).astype(jnp.int32) for a in (cnt, loff, dest))
    return tabs, lpos, gate, groups, n_rows_bound + SUB


def _combine_kernel(cnt_ref, loff_ref, dest_ref, src_hbm, lpos_ref, x1p_ref, x1s_ref, g_ref,
                    yp_ref, ys_ref, buf, sem):
    t = pl.program_id(0)
    last = pl.num_programs(0) - 1

    def copy(tile, off, src, size):
        return pltpu.make_async_copy(src_hbm.at[pl.ds(src, size)], buf.at[tile % 2, pl.ds(off, size)],
                                     sem.at[tile % 2])

    @pl.when(t == 0)
    def _():
        buf[...] = jnp.zeros_like(buf)
        _chunk_loop(cnt_ref, loff_ref, dest_ref, t, lambda o, g, s: copy(t, o, g, s).start())

    @pl.when(t < last)
    def _():
        _chunk_loop(cnt_ref, loff_ref, dest_ref, t + 1, lambda o, g, s: copy(t + 1, o, g, s).start())

    _chunk_loop(cnt_ref, loff_ref, dest_ref, t, lambda o, g, s: copy(t, o, g, s).wait())
    cols = lax.broadcasted_iota(jnp.int32, (TT, LROWS), 1)
    sel = jnp.zeros((TT, LROWS), F32)
    for k in range(TOP_K):
        sel = sel + jnp.where(cols == lpos_ref[:, k:k + 1], 1.0, 0.0)
    f = jnp.dot(sel.astype(BF16), buf[t % 2].astype(BF16), preferred_element_type=F32)

    @pl.when(t < last)
    def _():
        yp_ref[...] = _rms(x1p_ref[...] + f, g_ref[...])

    @pl.when(t == last)
    def _():
        ys_ref[...] = _rms(x1s_ref[...] + f, g_ref[...])


def _combine(tabs, out_sorted, lpos, x1p, x1s, g):
    d = x1p.shape[1]
    nt = lpos.shape[0] // TT
    last_p = x1p.shape[0] // TT - 1
    prompt = pl.BlockSpec((TT, d), lambda t, *_: (jnp.minimum(t, last_p), 0))
    sample = pl.BlockSpec((TT, d), lambda t, *_: (0, 0))
    grid_spec = pltpu.PrefetchScalarGridSpec(
        num_scalar_prefetch=3,
        grid=(nt,),
        in_specs=[pl.BlockSpec(memory_space=pl.ANY),
                  pl.BlockSpec((TT, TOP_K), lambda t, *_: (t, 0)),
                  prompt, sample,
                  pl.BlockSpec((1, d), lambda t, *_: (0, 0))],
        out_specs=[prompt, sample],
        scratch_shapes=[pltpu.VMEM((2, LROWS, d), F32), pltpu.SemaphoreType.DMA((2,))],
    )
    return pl.pallas_call(
        _combine_kernel,
        grid_spec=grid_spec,
        out_shape=[jax.ShapeDtypeStruct(x1p.shape, F32), jax.ShapeDtypeStruct((TT, d), F32)],
        compiler_params=_cparams(("arbitrary",)),
        name="moe_combine",
    )(*tabs, out_sorted, lpos, x1p, x1s, g.reshape(1, d))


def _t5_bucket(dist):
    max_exact = N_BUCKETS // 2
    dd = dist.astype(F32)
    large = max_exact + (jnp.log(jnp.maximum(dd, 1.0) / max_exact)
                         / math.log(BUCKET_MAX_DIST / max_exact) * (N_BUCKETS - max_exact)).astype(jnp.int32)
    large = jnp.minimum(large, N_BUCKETS - 1)
    return jnp.where(dist < max_exact, dist, large)


def _bias_tables(rel_bias):
    dist = jnp.asarray(np.arange(N_TAPS)[None, :] * np.array(DILATIONS)[:, None], jnp.int32)
    bias = jnp.transpose(rel_bias[_t5_bucket(dist)], (2, 0, 1)).astype(F32)
    by_branch = jnp.transpose(bias, (1, 0, 2))
    g, h = by_branch.shape[:2]
    row = jnp.concatenate([by_branch[:, :, ::-1], jnp.full((g, h, WIN), NEG, F32)], axis=-1)
    flat = jnp.broadcast_to(row[:, :, None, :], (g, h, WIN, 2 * WIN + 1)).reshape(g, h, -1)
    band = flat[:, :, :WIN * 2 * WIN].reshape(g, h, WIN, 2 * WIN)
    return band, by_branch[:, :, :0:-1], by_branch[:, :, 0:1]


def kernel(x_prompt, x_sample, cache_k_win, cache_v_win, state_conv, state_ssm, rel_bias, attn_norm, w_in, conv_w, conv_b, dt_bias, a_log, d_skip, ssd_norm, w_out, ffn_norm, w_router, b_router, w_gate_up, b_gate_up, w_down, b_down, final_norm):
    bp, tp, d = x_prompt.shape
    bs, ts, _ = x_sample.shape
    depth = w_in.shape[0]
    assert depth == 1 and ts == 1 and tp % (max(DILATIONS) * WIN) == 0
    keep = min(max(DILATIONS) * WIN, tp)
    band, samp, samp0 = _bias_tables(rel_bias)
    l = 0

    xp = x_prompt.reshape(bp * tp, d)
    xs = x_sample.reshape(bs * ts, d)
    w_in_bf = jnp.pad(w_in[l], ((0, 0), (0, IN_PROJ_PAD - IN_PROJ))).astype(BF16)
    w_out_bf = w_out[l].astype(BF16)
    wr = jnp.pad(w_router[l], ((0, 0), (0, LANES - N_EXPERTS))).astype(BF16)
    br = jnp.pad(b_router[l], (0, LANES - N_EXPERTS), constant_values=NEG).reshape(1, LANES)

    k, v, z, xbc, dt_raw, *branch_qkv = _inproj_prompt(xp, attn_norm[l], w_in_bf, bp)
    k3 = k.reshape(bp, tp, KV_WIDTH)
    v3 = v.reshape(bp, tp, KV_WIDTH)
    att_parts = [_attn_branch(branch_qkv[2 * gi], branch_qkv[2 * gi + 1], band[gi])
                 for gi in range(len(DILATIONS))]
    ssm, st_p = _ssd_prompt(xbc, dt_raw, z, conv_w[l], conv_b[l], dt_bias[l], a_log[l], d_skip[l],
                            ssd_norm[l], bp)
    x1p, xnp_, lgp = _outproj_prompt(att_parts, ssm, xp, w_out_bf, ffn_norm[l], wr, br)
    k_win_p = k3[:, tp - keep:].reshape(1, bp, keep, N_KV_HEADS, HEAD_DIM)
    v_win_p = v3[:, tp - keep:].reshape(1, bp, keep, N_KV_HEADS, HEAD_DIM)
    conv_p = xbc.reshape(bp, tp, CONV_DIM)[:, tp - (CONV_W - 1):][None]

    q_s, k_s, v_s, z_s, xbc_s, dt_s = _inproj(xs, attn_norm[l], w_in_bf, bs * ts)
    q_s = q_s.reshape(bs, N_ATT_HEADS, HEAD_DIM)
    k_s = k_s.reshape(bs, N_KV_HEADS, HEAD_DIM)
    v_s = v_s.reshape(bs, N_KV_HEADS, HEAD_DIM)
    wbuf = cache_k_win.shape[2]
    att_s = _attn_sample(q_s, k_s, v_s, cache_k_win[l].reshape(bs, wbuf, KV_WIDTH),
                         cache_v_win[l].reshape(bs, wbuf, KV_WIDTH), samp, samp0)
    xa_s, xdt_s, decay_s = _conv_sample(xbc_s, state_conv[l], conv_w[l], conv_b[l], dt_s, dt_bias[l],
                                        a_log[l])
    nbc = N_SSM_GROUPS * D_STATE
    h_s, y_s = _ssm_sample(xdt_s, decay_s, xa_s[:, SSM_WIDTH:SSM_WIDTH + nbc], xa_s[:, SSM_WIDTH + nbc:],
                           state_ssm[l])
    x1s, xns, lgs = _outproj_sample(att_s.reshape(bs, ATT_WIDTH), y_s, xa_s[:, :SSM_WIDTH], z_s,
                                    d_skip[l], ssd_norm[l], xs, w_out_bf, ffn_norm[l], wr, br)
    conv_s = jnp.concatenate([state_conv[l][:, 1:], xbc_s[:, None]], axis=1)[None]

    n_s = bs * ts
    assert (bp * tp) % TT == 0 and n_s <= TT
    m_pad = bp * tp + TT
    logits = jnp.concatenate([lgp[:, :N_EXPERTS], lgs[:, :N_EXPERTS]], axis=0)
    tabs, lpos, gate, groups, n_rows = _route(logits, m_pad)
    by_tile = lambda a: jnp.transpose(a.reshape(m_pad // TT, TT, TOP_K), (0, 2, 1))
    pad_rows = lambda a: jnp.pad(a, ((0, TT - n_s), (0, 0)))
    x_sorted = _dispatch(tabs, xnp_, pad_rows(xns), by_tile(lpos), by_tile(gate), n_rows)
    out_sorted = _moe_ffn(groups, x_sorted, w_gate_up[l], b_gate_up[l], w_down[l], b_down[l])
    y_p, y_s_out = _combine(tabs, out_sorted, lpos, x1p, pad_rows(x1s), final_norm)
    y_s_out = y_s_out[:n_s]

    return (y_p.reshape(bp, tp, d), y_s_out.reshape(bs, ts, d), k_win_p, v_win_p, conv_p, st_p[None],
            k_s.reshape(1, bs, ts, N_KV_HEADS, HEAD_DIM), v_s.reshape(1, bs, ts, N_KV_HEADS, HEAD_DIM),
            conv_s, h_s[None])
```
